```python
import jax, jax.numpy as jnp
from jax import lax
import numpy as np

D_MODEL = 1024
BATCH = 8
SEQ = 2048
DEPTH = 1
DEC_BATCH = 128
DEC_SEQ = 1
PAST_LEN = 16384
PAGE_SIZE = 128

MIX_WIDTH = D_MODEL
LRU_WIDTH = MIX_WIDTH // 2
CF_WIDTH = MIX_WIDTH - LRU_WIDTH
LRU_HEADS = 8
LRU_HEAD_DIM = LRU_WIDTH // LRU_HEADS
LRU_CONV = 4
LRU_C = 8.0
CF_CONV = 31
IN_WIDTH = 2 * LRU_WIDTH + 2 * CF_WIDTH
N_MEM = 256
XA_HEADS = 4
XA_HEAD_DIM = D_MODEL // XA_HEADS
N_EXPERTS = 32
TOP_K = 4
D_EXPERT = D_MODEL
SWIGLU_LIMIT = 7.0
SWIGLU_ALPHA = 1.702
MOE_BLOCK = 128
EPS = 1e-6

kernel_name = "hymba_rglru_conformer_moe_xattn_step"


def rms_norm(x, g):
    xf = x.astype(jnp.float32)
    y = xf * lax.rsqrt(jnp.mean(xf * xf, axis=-1, keepdims=True) + EPS)
    return (y * g.astype(jnp.float32)).astype(x.dtype)


def layer_norm(x, g, b):
    xf = x.astype(jnp.float32)
    mu = jnp.mean(xf, axis=-1, keepdims=True)
    xc = xf - mu
    y = xc * lax.rsqrt(jnp.mean(xc * xc, axis=-1, keepdims=True) + EPS)
    return (y * g.astype(jnp.float32) + b.astype(jnp.float32)).astype(x.dtype)


def causal_dwconv(buf, x, w, b):
    width = w.shape[0]
    xp = jnp.concatenate([buf.astype(x.dtype), x], axis=1)
    y = lax.conv_general_dilated(
        xp, w[:, None, :].astype(x.dtype), window_strides=(1,), padding='VALID',
        dimension_numbers=('NWC', 'WIO', 'NWC'), feature_group_count=x.shape[-1])
    return y + b.astype(x.dtype), xp[:, xp.shape[1] - (width - 1):, :]


def rg_lru(x, h0, w_a, b_a, w_x, b_x, lam):
    B, T, _ = x.shape
    xf = x.astype(jnp.float32)
    xh = xf.reshape(B, T, LRU_HEADS, LRU_HEAD_DIM)
    r = jax.nn.sigmoid(jnp.einsum('bthi,hij->bthj', xh, w_a.astype(jnp.float32)).reshape(B, T, LRU_WIDTH) + b_a.astype(jnp.float32))
    i = jax.nn.sigmoid(jnp.einsum('bthi,hij->bthj', xh, w_x.astype(jnp.float32)).reshape(B, T, LRU_WIDTH) + b_x.astype(jnp.float32))
    log_a = -LRU_C * r * jax.nn.softplus(-lam.astype(jnp.float32))
    a = jnp.exp(log_a)
    u = jnp.sqrt(-jnp.expm1(2.0 * log_a)) * (i * xf)

    def step(h, au):
        a_t, u_t = au
        h = a_t * h + u_t
        return h, h

    h_last, hs = lax.scan(step, h0.astype(jnp.float32), (jnp.swapaxes(a, 0, 1), jnp.swapaxes(u, 0, 1)))
    return jnp.swapaxes(hs, 0, 1).astype(x.dtype), h_last


def head_group_mixer(xn, lru_h, lru_buf, cf_buf, mix_w):
    (w_in, w_lru_conv, b_lru_conv, w_lru_a, b_lru_a, w_lru_x, b_lru_x, lru_lambda,
     w_cf_conv, b_cf_conv, g_cf_ln, b_cf_ln, w_out) = mix_w
    z = xn @ w_in
    zx, zg, za, zb = jnp.split(z, [LRU_WIDTH, 2 * LRU_WIDTH, 2 * LRU_WIDTH + CF_WIDTH], axis=-1)
    xc, lru_buf_new = causal_dwconv(lru_buf, zx, w_lru_conv, b_lru_conv)
    hs, lru_h_new = rg_lru(xc, lru_h, w_lru_a, b_lru_a, w_lru_x, b_lru_x, lru_lambda)
    y_lru = hs * jax.nn.gelu(zg)
    glu = za * jax.nn.sigmoid(zb)
    gc, cf_buf_new = causal_dwconv(cf_buf, glu, w_cf_conv, b_cf_conv)
    y_cf = jax.nn.silu(layer_norm(gc, g_cf_ln, b_cf_ln))
    y = jnp.concatenate([y_lru, y_cf], axis=-1) @ w_out
    return y, lru_h_new, lru_buf_new, cf_buf_new


def memory_kv(mem, g_mem, w_k, w_v):
    B = mem.shape[0]
    mn = rms_norm(mem, g_mem)
    k = (mn @ w_k).reshape(B, N_MEM, XA_HEADS, XA_HEAD_DIM)
    v = (mn @ w_v).reshape(B, N_MEM, XA_HEADS, XA_HEAD_DIM)
    return k, v


def cross_attention(xn, k, v, w_q, w_o):
    B, T, _ = xn.shape
    q = (xn @ w_q).reshape(B, T, XA_HEADS, XA_HEAD_DIM)
    s = jnp.einsum('bthd,bmhd->bhtm', q, k.astype(q.dtype)).astype(jnp.float32) * (XA_HEAD_DIM ** -0.5)
    p = jax.nn.softmax(s, axis=-1).astype(v.dtype)
    o = jnp.einsum('bhtm,bmhd->bthd', p, v).reshape(B, T, D_MODEL).astype(xn.dtype)
    return o @ w_o


def routed_experts(xn, moe_w):
    w_router, b_router, w_e1, b_e1, w_e2, b_e2 = moe_w
    B, T, D = xn.shape
    N = B * T
    xf = xn.reshape(N, D)
    logits = (xf @ w_router).astype(jnp.float32) + b_router.astype(jnp.float32)
    top_v, top_e = lax.top_k(logits, TOP_K)
    gates = jax.nn.softmax(top_v, axis=-1)
    A = N * TOP_K
    e_flat = top_e.reshape(A)
    order = jnp.argsort(e_flat)
    e_sorted = e_flat[order]
    counts = jnp.bincount(e_flat, length=N_EXPERTS)
    padded = (counts + MOE_BLOCK - 1) // MOE_BLOCK * MOE_BLOCK
    start = jnp.cumsum(counts) - counts
    pstart = jnp.cumsum(padded) - padded
    dest = pstart[e_sorted] + (jnp.arange(A) - start[e_sorted])
    n_blocks = -(-A // MOE_BLOCK) + N_EXPERTS
    P = n_blocks * MOE_BLOCK
    tok = jnp.full((P,), N, jnp.int32).at[dest].set((order // TOP_K).astype(jnp.int32))
    gate_rows = jnp.zeros((P,), jnp.float32).at[dest].set(gates.reshape(A)[order])
    blk_e = jnp.clip(jnp.searchsorted(jnp.cumsum(padded), jnp.arange(n_blocks) * MOE_BLOCK, side='right'),
                     0, N_EXPERTS - 1).astype(jnp.int32)
    x_pad = jnp.concatenate([xf, jnp.zeros((1, D), xf.dtype)], axis=0)

    def run_block(args):
        e, idx = args
        xb = x_pad[idx]
        gu = xb @ w_e1[e] + b_e1[e]
        g, u = gu[:, :D_EXPERT], gu[:, D_EXPERT:]
        g = jnp.minimum(g, SWIGLU_LIMIT)
        u = jnp.clip(u, -SWIGLU_LIMIT, SWIGLU_LIMIT)
        h = (u + 1.0) * (g * jax.nn.sigmoid(SWIGLU_ALPHA * g))
        return h @ w_e2[e] + b_e2[e]

    rows = lax.map(run_block, (blk_e, tok.reshape(n_blocks, MOE_BLOCK)))
    rows = rows.reshape(P, D).astype(jnp.float32) * gate_rows[:, None]
    out = jax.ops.segment_sum(rows, tok, num_segments=N + 1)[:N]
    return out.astype(xn.dtype).reshape(B, T, D)


def decoder_layer(x, lru_h, lru_buf, cf_buf, mem_k, mem_v, g_mix, g_xa, g_moe, mix_w, xa_w, moe_w):
    y, lru_h_new, lru_buf_new, cf_buf_new = head_group_mixer(rms_norm(x, g_mix), lru_h, lru_buf, cf_buf, mix_w)
    x = x + y
    x = x + cross_attention(rms_norm(x, g_xa), mem_k, mem_v, xa_w[0], xa_w[1])
    x = x + routed_experts(rms_norm(x, g_moe), moe_w)
    return x, lru_h_new, lru_buf_new, cf_buf_new


def setup_inputs(seed: int = 0) -> dict:
    key = jax.random.key(seed)
    keys = jax.random.split(key, 40)
    counter = [0]
    f32 = jnp.float32

    def nxt():
        k = keys[counter[0]]
        counter[0] += 1
        return k

    def nrm(shape, scale):
        return jax.random.normal(nxt(), shape, f32) * scale

    def gain(shape):
        return 1.0 + nrm(shape, 0.02)

    L = DEPTH
    x_prompt = nrm((BATCH, SEQ, D_MODEL), 1.0)
    x_sample = nrm((DEC_BATCH, DEC_SEQ, D_MODEL), 1.0)
    state_lru_h = nrm((L, DEC_BATCH, LRU_WIDTH), 0.5)
    cache_lru_conv = nrm((L, DEC_BATCH, LRU_CONV - 1, LRU_WIDTH), 1.0)
    cache_cf_conv = nrm((L, DEC_BATCH, CF_CONV - 1, CF_WIDTH), 0.5)
    cache_mem_k = nrm((L, DEC_BATCH, N_MEM, XA_HEADS, XA_HEAD_DIM), 1.0)
    cache_mem_v = nrm((L, DEC_BATCH, N_MEM, XA_HEADS, XA_HEAD_DIM), 1.0)
    mem_prompt = nrm((BATCH, N_MEM, D_MODEL), 1.0)

    g_mix = gain((L, D_MODEL))
    w_in = nrm((L, D_MODEL, IN_WIDTH), D_MODEL ** -0.5)
    w_lru_conv = nrm((L, LRU_CONV, LRU_WIDTH), LRU_CONV ** -0.5)
    b_lru_conv = nrm((L, LRU_WIDTH), 0.02)
    w_lru_a = nrm((L, LRU_HEADS, LRU_HEAD_DIM, LRU_HEAD_DIM), LRU_HEAD_DIM ** -0.5)
    b_lru_a = nrm((L, LRU_WIDTH), 0.02)
    w_lru_x = nrm((L, LRU_HEADS, LRU_HEAD_DIM, LRU_HEAD_DIM), LRU_HEAD_DIM ** -0.5)
    b_lru_x = nrm((L, LRU_WIDTH), 0.02)
    a0 = jax.random.uniform(nxt(), (L, LRU_WIDTH), f32, 0.9, 0.999)
    s0 = a0 ** (1.0 / LRU_C)
    lru_lambda = jnp.log(s0) - jnp.log1p(-s0)
    w_cf_conv = nrm((L, CF_CONV, CF_WIDTH), CF_CONV ** -0.5)
    b_cf_conv = nrm((L, CF_WIDTH), 0.02)
    g_cf_ln = gain((L, CF_WIDTH))
    b_cf_ln = nrm((L, CF_WIDTH), 0.02)
    w_out = nrm((L, MIX_WIDTH, D_MODEL), MIX_WIDTH ** -0.5)

    g_xa = gain((L, D_MODEL))
    g_mem = gain((L, D_MODEL))
    w_q = nrm((L, D_MODEL, D_MODEL), D_MODEL ** -0.5)
    w_k = nrm((L, D_MODEL, D_MODEL), D_MODEL ** -0.5)
    w_v = nrm((L, D_MODEL, D_MODEL), D_MODEL ** -0.5)
    w_o = nrm((L, D_MODEL, D_MODEL), D_MODEL ** -0.5)

    g_moe = gain((L, D_MODEL))
    w_router = nrm((L, D_MODEL, N_EXPERTS), D_MODEL ** -0.5)
    b_router = nrm((L, N_EXPERTS), 0.01)
    w_e1 = nrm((L, N_EXPERTS, D_MODEL, 2 * D_EXPERT), D_MODEL ** -0.5)
    b_e1 = nrm((L, N_EXPERTS, 2 * D_EXPERT), 0.01)
    w_e2 = nrm((L, N_EXPERTS, D_EXPERT, D_MODEL), D_EXPERT ** -0.5)
    b_e2 = nrm((L, N_EXPERTS, D_MODEL), 0.01)
    g_final = gain((D_MODEL,))

    return {
        "x_prompt": x_prompt, "x_sample": x_sample,
        "state_lru_h": state_lru_h, "cache_lru_conv": cache_lru_conv, "cache_cf_conv": cache_cf_conv,
        "cache_mem_k": cache_mem_k, "cache_mem_v": cache_mem_v,
        "mem_prompt": mem_prompt,
        "g_mix": g_mix, "w_in": w_in, "w_lru_conv": w_lru_conv, "b_lru_conv": b_lru_conv,
        "w_lru_a": w_lru_a, "b_lru_a": b_lru_a, "w_lru_x": w_lru_x, "b_lru_x": b_lru_x,
        "lru_lambda": lru_lambda, "w_cf_conv": w_cf_conv, "b_cf_conv": b_cf_conv,
        "g_cf_ln": g_cf_ln, "b_cf_ln": b_cf_ln, "w_out": w_out,
        "g_xa": g_xa, "g_mem": g_mem, "w_q": w_q, "w_k": w_k, "w_v": w_v, "w_o": w_o,
        "g_moe": g_moe, "w_router": w_router, "b_router": b_router,
        "w_e1": w_e1, "b_e1": b_e1, "w_e2": w_e2, "b_e2": b_e2,
        "g_final": g_final,
    }


def reference(x_prompt, x_sample, state_lru_h, cache_lru_conv, cache_cf_conv, cache_mem_k, cache_mem_v,
              mem_prompt, g_mix, w_in, w_lru_conv, b_lru_conv, w_lru_a, b_lru_a, w_lru_x, b_lru_x,
              lru_lambda, w_cf_conv, b_cf_conv, g_cf_ln, b_cf_ln, w_out, g_xa, g_mem, w_q, w_k, w_v, w_o,
              g_moe, w_router, b_router, w_e1, b_e1, w_e2, b_e2, g_final):
    xp, xs = x_prompt, x_sample
    hp_l, lcp_l, ccp_l, mkp_l, mvp_l = [], [], [], [], []
    hs_l, lcs_l, ccs_l = [], [], []
    for l in range(DEPTH):
        mix_w = (w_in[l], w_lru_conv[l], b_lru_conv[l], w_lru_a[l], b_lru_a[l], w_lru_x[l], b_lru_x[l],
                 lru_lambda[l], w_cf_conv[l], b_cf_conv[l], g_cf_ln[l], b_cf_ln[l], w_out[l])
        xa_w = (w_q[l], w_o[l])
        moe_w = (w_router[l], b_router[l], w_e1[l], b_e1[l], w_e2[l], b_e2[l])
        mk, mv = memory_kv(mem_prompt, g_mem[l], w_k[l], w_v[l])
        h0 = jnp.zeros((xp.shape[0], LRU_WIDTH), jnp.float32)
        lb0 = jnp.zeros((xp.shape[0], LRU_CONV - 1, LRU_WIDTH), xp.dtype)
        cb0 = jnp.zeros((xp.shape[0], CF_CONV - 1, CF_WIDTH), xp.dtype)
        xp, hp, lcp, ccp = decoder_layer(xp, h0, lb0, cb0, mk, mv, g_mix[l], g_xa[l], g_moe[l], mix_w, xa_w, moe_w)
        xs, hs, lcs, ccs = decoder_layer(xs, state_lru_h[l], cache_lru_conv[l], cache_cf_conv[l],
                                         cache_mem_k[l], cache_mem_v[l], g_mix[l], g_xa[l], g_moe[l],
                                         mix_w, xa_w, moe_w)
        hp_l.append(hp); lcp_l.append(lcp); ccp_l.append(ccp); mkp_l.append(mk); mvp_l.append(mv)
        hs_l.append(hs); lcs_l.append(lcs); ccs_l.append(ccs)
    y_prompt = rms_norm(xp, g_final)
    y_sample = rms_norm(xs, g_final)
    return (y_prompt, y_sample, jnp.stack(hp_l), jnp.stack(lcp_l), jnp.stack(ccp_l), jnp.stack(mkp_l),
            jnp.stack(mvp_l), jnp.stack(hs_l), jnp.stack(lcs_l), jnp.stack(ccs_l))
```

```python
import functools

import jax
import jax.numpy as jnp
from jax import lax
from jax.experimental import pallas as pl
from jax.experimental.pallas import tpu as pltpu

F32 = jnp.float32
BF16 = jnp.bfloat16
I32 = jnp.int32

D_MODEL = 1024
LRU_WIDTH = 512
CF_WIDTH = 512
LRU_HEADS = 8
LRU_CONV = 4
LRU_C = 8.0
CF_CONV = 31
IN_WIDTH = 2 * LRU_WIDTH + 2 * CF_WIDTH
N_MEM = 256
XA_HEADS = 4
XA_HEAD_DIM = D_MODEL // XA_HEADS
N_EXPERTS = 32
TOP_K = 4
D_EXPERT = D_MODEL
SWIGLU_LIMIT = 7.0
SWIGLU_ALPHA = 1.702
EPS = 1e-6

LANES = 128
SUBLANES = 8
VMEM_LIMIT = 56 * 1024 * 1024

MIX_TILE = 256
LRU_TAIL_ROWS = 8
CF_TAIL_ROWS = 32
ATT_TILE = 256
KV_TILE = 256
SAMPLE_GROUP = 8
SAMPLE_MIX_GROUP = 32
MOE_BLOCK = 256
DISPATCH_TILE = 512
COMBINE_TILE = 256


def _cparams(sem):
    return pltpu.CompilerParams(dimension_semantics=sem, vmem_limit_bytes=VMEM_LIMIT)


def _full(shape):
    n = len(shape)
    return pl.BlockSpec(shape, lambda *_: (0,) * n)


def _rms(x, g):
    return x * lax.rsqrt(jnp.mean(x * x, axis=-1, keepdims=True) + EPS) * g


def _bdot(a, b):
    return jnp.dot(a.astype(BF16), b, preferred_element_type=F32)


def _neg_expm1(y):
    u = jnp.exp(y)
    d = 1.0 - u
    return jnp.where(d == 0.0, -y, jnp.where(u == 0.0, 1.0, d * y / jnp.log(u)))


def _lru_coeffs(xc, wg_ref, bg_ref, lam_ref):
    gl = _bdot(xc, wg_ref[...]) + bg_ref[...]
    r = jax.nn.sigmoid(gl[:, :LRU_WIDTH])
    i = jax.nn.sigmoid(gl[:, LRU_WIDTH:])
    log_a = -LRU_C * r * jax.nn.softplus(-lam_ref[...])
    a = jnp.exp(log_a)
    u = jnp.sqrt(_neg_expm1(2.0 * log_a)) * (i * xc)
    return a, u


def _cf_post(gc, gln_ref, bln_ref):
    mu = jnp.mean(gc, axis=-1, keepdims=True)
    c = gc - mu
    y = c * lax.rsqrt(jnp.mean(c * c, axis=-1, keepdims=True) + EPS)
    return jax.nn.silu(y * gln_ref[...] + bln_ref[...])


def _mixer_out(x, y_lru, y_cf, wout_ref):
    y = _bdot(y_lru, wout_ref[:LRU_WIDTH, :]) + _bdot(y_cf, wout_ref[LRU_WIDTH:, :])
    return x + y


def _mixer_prompt_kernel(x_ref, gmix_ref, win_ref, wlc_ref, blc_ref, wg_ref, bg_ref, lam_ref,
                         wcc_ref, bcc_ref, gln_ref, bln_ref, wout_ref,
                         x1_ref, h_ref, ltail_ref, ctail_ref,
                         zx_ext, glu_ext, h_carry):
    tt = MIX_TILE
    j = pl.program_id(1)

    @pl.when(j == 0)
    def _():
        zx_ext[0:LRU_TAIL_ROWS, :] = jnp.zeros((LRU_TAIL_ROWS, LRU_WIDTH), F32)
        glu_ext[0:CF_TAIL_ROWS, :] = jnp.zeros((CF_TAIL_ROWS, CF_WIDTH), F32)
        h_carry[...] = jnp.zeros((1, LRU_WIDTH), F32)

    x = x_ref[...]
    z = _bdot(_rms(x, gmix_ref[...]), win_ref[...])
    zx = z[:, :LRU_WIDTH]
    zg = z[:, LRU_WIDTH:2 * LRU_WIDTH]
    za = z[:, 2 * LRU_WIDTH:2 * LRU_WIDTH + CF_WIDTH]
    zb = z[:, 2 * LRU_WIDTH + CF_WIDTH:]

    zx_ext[LRU_TAIL_ROWS:LRU_TAIL_ROWS + tt, :] = zx
    off = LRU_TAIL_ROWS - (LRU_CONV - 1)
    xc = blc_ref[...]
    for k in range(LRU_CONV):
        xc = xc + wlc_ref[k:k + 1, :] * zx_ext[off + k:off + k + tt, :]
    a, u = _lru_coeffs(xc, wg_ref, bg_ref, lam_ref)
    row = lax.broadcasted_iota(I32, (tt, LRU_WIDTH), 0)
    d = 1
    while d < tt:
        keep = row >= d
        a_sh = jnp.where(keep, pltpu.roll(a, d, 0), 1.0)
        u_sh = jnp.where(keep, pltpu.roll(u, d, 0), 0.0)
        u = u + a * u_sh
        a = a * a_sh
        d *= 2
    hs = a * h_carry[...] + u
    h_last = hs[tt - 1:tt, :]
    h_carry[...] = h_last
    y_lru = hs * jax.nn.gelu(zg)

    glu = za * jax.nn.sigmoid(zb)
    glu_ext[CF_TAIL_ROWS:CF_TAIL_ROWS + tt, :] = glu
    off = CF_TAIL_ROWS - (CF_CONV - 1)
    gc = bcc_ref[...]
    for k in range(CF_CONV):
        gc = gc + wcc_ref[k:k + 1, :] * glu_ext[off + k:off + k + tt, :]
    y_cf = _cf_post(gc, gln_ref, bln_ref)

    x1_ref[...] = _mixer_out(x, y_lru, y_cf, wout_ref)

    h_ref[0] = h_last
    ltail_ref[0] = zx_ext[LRU_TAIL_ROWS + tt - (LRU_CONV - 1):LRU_TAIL_ROWS + tt, :]
    ctail_ref[0] = glu_ext[CF_TAIL_ROWS + tt - (CF_CONV - 1):CF_TAIL_ROWS + tt, :]
    zx_ext[0:LRU_TAIL_ROWS, :] = zx_ext[tt:tt + LRU_TAIL_ROWS, :]
    glu_ext[0:CF_TAIL_ROWS, :] = glu_ext[tt:tt + CF_TAIL_ROWS, :]


def _mixer_weight_specs():
    return [
        _full((1, D_MODEL)),
        _full((D_MODEL, IN_WIDTH)),
        _full((LRU_CONV, LRU_WIDTH)), _full((1, LRU_WIDTH)),
        _full((LRU_WIDTH, 2 * LRU_WIDTH)), _full((1, 2 * LRU_WIDTH)),
        _full((1, LRU_WIDTH)),
        _full((CF_CONV, CF_WIDTH)), _full((1, CF_WIDTH)),
        _full((1, CF_WIDTH)), _full((1, CF_WIDTH)),
        _full((D_MODEL, D_MODEL)),
    ]


def _mixer_prompt(x, mix_w, batch, seq):
    tt = MIX_TILE
    nt = seq // tt
    return pl.pallas_call(
        _mixer_prompt_kernel,
        grid=(batch, nt),
        in_specs=[pl.BlockSpec((tt, D_MODEL), lambda b, j: (b * nt + j, 0))] + _mixer_weight_specs(),
        out_specs=[
            pl.BlockSpec((tt, D_MODEL), lambda b, j: (b * nt + j, 0)),
            pl.BlockSpec((1, 1, LRU_WIDTH), lambda b, j: (b, 0, 0)),
            pl.BlockSpec((1, LRU_CONV - 1, LRU_WIDTH), lambda b, j: (b, 0, 0)),
            pl.BlockSpec((1, CF_CONV - 1, CF_WIDTH), lambda b, j: (b, 0, 0)),
        ],
        out_shape=[
            jax.ShapeDtypeStruct((batch * seq, D_MODEL), F32),
            jax.ShapeDtypeStruct((batch, 1, LRU_WIDTH), F32),
            jax.ShapeDtypeStruct((batch, LRU_CONV - 1, LRU_WIDTH), F32),
            jax.ShapeDtypeStruct((batch, CF_CONV - 1, CF_WIDTH), F32),
        ],
        scratch_shapes=[
            pltpu.VMEM((LRU_TAIL_ROWS + tt, LRU_WIDTH), F32),
            pltpu.VMEM((CF_TAIL_ROWS + tt, CF_WIDTH), F32),
            pltpu.VMEM((1, LRU_WIDTH), F32),
        ],
        compiler_params=_cparams(("arbitrary", "arbitrary")),
        name="mixer_prompt",
    )(x, *mix_w)


def _mixer_sample_kernel(x_ref, gmix_ref, win_ref, wlc_ref, blc_ref, wg_ref, bg_ref, lam_ref,
                         wcc_ref, bcc_ref, gln_ref, bln_ref, wout_ref,
                         h0_ref, lbuf_ref, cbuf_ref,
                         x1_ref, h_ref, zx_ref, glu_ref):
    x = x_ref[...]
    z = _bdot(_rms(x, gmix_ref[...]), win_ref[...])
    zx = z[:, :LRU_WIDTH]
    zg = z[:, LRU_WIDTH:2 * LRU_WIDTH]
    za = z[:, 2 * LRU_WIDTH:2 * LRU_WIDTH + CF_WIDTH]
    zb = z[:, 2 * LRU_WIDTH + CF_WIDTH:]

    xc = blc_ref[...] + wlc_ref[LRU_CONV - 1:LRU_CONV, :] * zx
    xc = xc + jnp.sum(lbuf_ref[...] * wlc_ref[0:LRU_CONV - 1, :][None], axis=1)
    a, u = _lru_coeffs(xc, wg_ref, bg_ref, lam_ref)
    h = a * h0_ref[...] + u
    y_lru = h * jax.nn.gelu(zg)

    glu = za * jax.nn.sigmoid(zb)
    gc = bcc_ref[...] + wcc_ref[CF_CONV - 1:CF_CONV, :] * glu
    gc = gc + jnp.sum(cbuf_ref[...] * wcc_ref[0:CF_CONV - 1, :][None], axis=1)
    y_cf = _cf_post(gc, gln_ref, bln_ref)

    x1_ref[...] = _mixer_out(x, y_lru, y_cf, wout_ref)
    h_ref[...] = h
    zx_ref[...] = zx
    glu_ref[...] = glu


def _mixer_sample(x, mix_w, h0, lbuf, cbuf):
    n = x.shape[0]
    g = SAMPLE_MIX_GROUP
    row = lambda w: pl.BlockSpec((g, w), lambda i: (i, 0))
    return pl.pallas_call(
        _mixer_sample_kernel,
        grid=(n // g,),
        in_specs=[row(D_MODEL)] + _mixer_weight_specs() + [
            row(LRU_WIDTH),
            pl.BlockSpec((g, LRU_CONV - 1, LRU_WIDTH), lambda i: (i, 0, 0)),
            pl.BlockSpec((g, CF_CONV - 1, CF_WIDTH), lambda i: (i, 0, 0)),
        ],
        out_specs=[row(D_MODEL), row(LRU_WIDTH), row(LRU_WIDTH), row(CF_WIDTH)],
        out_shape=[
            jax.ShapeDtypeStruct((n, D_MODEL), F32),
            jax.ShapeDtypeStruct((n, LRU_WIDTH), F32),
            jax.ShapeDtypeStruct((n, LRU_WIDTH), F32),
            jax.ShapeDtypeStruct((n, CF_WIDTH), F32),
        ],
        compiler_params=_cparams(("arbitrary",)),
        name="mixer_sample",
    )(x, *mix_w, h0, lbuf, cbuf)


def _memkv_kernel(mem_ref, gmem_ref, wk_ref, wv_ref, k_ref, v_ref):
    mn = _rms(mem_ref[...], gmem_ref[...]).astype(BF16)
    k_ref[...] = jnp.dot(mn, wk_ref[...], preferred_element_type=F32)
    v_ref[...] = jnp.dot(mn, wv_ref[...], preferred_element_type=F32)


def _memkv(mem, g_mem, wk, wv):
    n = mem.shape[0]
    t = KV_TILE
    row = pl.BlockSpec((t, D_MODEL), lambda i: (i, 0))
    return pl.pallas_call(
        _memkv_kernel,
        grid=(n // t,),
        in_specs=[row, _full((1, D_MODEL)), _full((D_MODEL, D_MODEL)), _full((D_MODEL, D_MODEL))],
        out_specs=[row, row],
        out_shape=[jax.ShapeDtypeStruct((n, D_MODEL), F32)] * 2,
        compiler_params=_cparams(("arbitrary",)),
        name="memory_kv",
    )(mem, g_mem, wk, wv)


def _router(x2, gmoe_ref, wrt_ref, brt_ref, xn_ref, tope_ref, gates_ref):
    xn = _rms(x2, gmoe_ref[...])
    xn_ref[...] = xn
    logits = lax.dot_general(wrt_ref[...], xn.astype(BF16), (((1,), (1,)), ((), ())),
                             preferred_element_type=F32) + brt_ref[...]
    e_iota = lax.broadcasted_iota(I32, logits.shape, 0)
    work = logits
    vals, idxs = [], []
    for _ in range(TOP_K):
        m = jnp.max(work, axis=0, keepdims=True)
        idx = jnp.min(jnp.where(work == m, e_iota, N_EXPERTS), axis=0, keepdims=True)
        vals.append(m)
        idxs.append(idx)
        work = jnp.where(e_iota == idx, -jnp.inf, work)
    ex = [jnp.exp(v - vals[0]) for v in vals]
    den = ex[0] + ex[1] + ex[2] + ex[3]
    tope_ref[...] = jnp.concatenate(idxs, axis=0)
    gates_ref[...] = jnp.concatenate([e / den for e in ex], axis=0)


def _router_specs():
    return [_full((1, D_MODEL)), _full((N_EXPERTS, D_MODEL)), _full((N_EXPERTS, 1))]


def _router_out(n, tile, index):
    specs = [
        pl.BlockSpec((tile, D_MODEL), lambda *a: (index(*a), 0)),
        pl.BlockSpec((tile, D_MODEL), lambda *a: (index(*a), 0)),
        pl.BlockSpec((TOP_K, tile), lambda *a: (0, index(*a))),
        pl.BlockSpec((TOP_K, tile), lambda *a: (0, index(*a))),
    ]
    shapes = [
        jax.ShapeDtypeStruct((n, D_MODEL), F32),
        jax.ShapeDtypeStruct((n, D_MODEL), F32),
        jax.ShapeDtypeStruct((TOP_K, n), I32),
        jax.ShapeDtypeStruct((TOP_K, n), F32),
    ]
    return specs, shapes


def _attn_prompt_kernel(x1_ref, k_ref, v_ref, gxa_ref, wq_ref, wo_ref, gmoe_ref, wrt_ref, brt_ref,
                        x2_ref, xn_ref, tope_ref, gates_ref):
    x1 = x1_ref[...]
    q = (_bdot(_rms(x1, gxa_ref[...]), wq_ref[...]) * (XA_HEAD_DIM ** -0.5)).astype(BF16)
    outs = []
    for h in range(XA_HEADS):
        sl = slice(h * XA_HEAD_DIM, (h + 1) * XA_HEAD_DIM)
        kh = k_ref[:, sl].astype(BF16)
        vh = v_ref[:, sl].astype(BF16)
        s = lax.dot_general(q[:, sl], kh, (((1,), (1,)), ((), ())), preferred_element_type=F32)
        p = jnp.exp(s - jnp.max(s, axis=-1, keepdims=True))
        p = p / jnp.sum(p, axis=-1, keepdims=True)
        outs.append(_bdot(p, vh))
    o = jnp.concatenate(outs, axis=-1)
    x2 = x1 + _bdot(o, wo_ref[...])
    x2_ref[...] = x2
    _router(x2, gmoe_ref, wrt_ref, brt_ref, xn_ref, tope_ref, gates_ref)


def _attn_prompt(x1, k, v, gxa, wq, wo, router_w, batch, seq):
    tq = ATT_TILE
    nt = seq // tq
    out_specs, out_shapes = _router_out(batch * seq, tq, lambda b, j: b * nt + j)
    kv = pl.BlockSpec((N_MEM, D_MODEL), lambda b, j: (b, 0))
    return pl.pallas_call(
        _attn_prompt_kernel,
        grid=(batch, nt),
        in_specs=[pl.BlockSpec((tq, D_MODEL), lambda b, j: (b * nt + j, 0)), kv, kv,
                  _full((1, D_MODEL)), _full((D_MODEL, D_MODEL)), _full((D_MODEL, D_MODEL))] + _router_specs(),
        out_specs=out_specs,
        out_shape=out_shapes,
        compiler_params=_cparams(("arbitrary", "arbitrary")),
        name="attn_prompt",
    )(x1, k, v, gxa, wq, wo, *router_w)


def _qproj_kernel(x1_ref, gxa_ref, wq_ref, q_ref):
    q_ref[...] = _bdot(_rms(x1_ref[...], gxa_ref[...]), wq_ref[...]) * (XA_HEAD_DIM ** -0.5)


def _attn_sample_core_kernel(q_ref, k_ref, v_ref, o_ref):
    q = q_ref[...]
    outs = []
    for h in range(XA_HEADS):
        sl = slice(h * XA_HEAD_DIM, (h + 1) * XA_HEAD_DIM)
        s = jnp.sum(k_ref[:, :, sl] * q[:, None, sl], axis=-1, keepdims=True)
        p = jnp.exp(s - jnp.max(s, axis=1, keepdims=True))
        p = p / jnp.sum(p, axis=1, keepdims=True)
        outs.append(jnp.sum(p * v_ref[:, :, sl], axis=1))
    o_ref[...] = jnp.concatenate(outs, axis=-1)


def _oproj_router_kernel(x1_ref, o_ref, wo_ref, gmoe_ref, wrt_ref, brt_ref,
                         x2_ref, xn_ref, tope_ref, gates_ref):
    x2 = x1_ref[...] + _bdot(o_ref[...], wo_ref[...])
    x2_ref[...] = x2
    _router(x2, gmoe_ref, wrt_ref, brt_ref, xn_ref, tope_ref, gates_ref)


def _attn_sample(x1, k, v, gxa, wq, wo, router_w):
    n = x1.shape[0]
    g = SAMPLE_GROUP
    q = pl.pallas_call(
        _qproj_kernel,
        grid=(1,),
        in_specs=[_full((n, D_MODEL)), _full((1, D_MODEL)), _full((D_MODEL, D_MODEL))],
        out_specs=_full((n, D_MODEL)),
        out_shape=jax.ShapeDtypeStruct((n, D_MODEL), F32),
        compiler_params=_cparams(("arbitrary",)),
        name="qproj_sample",
    )(x1, gxa, wq)
    kv = pl.BlockSpec((g, N_MEM, D_MODEL), lambda i: (i, 0, 0))
    row = pl.BlockSpec((g, D_MODEL), lambda i: (i, 0))
    o = pl.pallas_call(
        _attn_sample_core_kernel,
        grid=(n // g,),
        in_specs=[row, kv, kv],
        out_specs=row,
        out_shape=jax.ShapeDtypeStruct((n, D_MODEL), F32),
        compiler_params=_cparams(("arbitrary",)),
        name="attn_sample_core",
    )(q, k, v)
    out_specs, out_shapes = _router_out(n, n, lambda i: 0)
    return pl.pallas_call(
        _oproj_router_kernel,
        grid=(1,),
        in_specs=[_full((n, D_MODEL)), _full((n, D_MODEL)), _full((D_MODEL, D_MODEL))] + _router_specs(),
        out_specs=out_specs,
        out_shape=out_shapes,
        compiler_params=_cparams(("arbitrary",)),
        name="oproj_router_sample",
    )(x1, o, wo, *router_w)


def _moe_num_blocks(n_assign):
    return pl.cdiv(n_assign, MOE_BLOCK) + N_EXPERTS


def _slots_kernel(tp_ref, ts_ref, destp_ref, dests_ref, blke_ref, nact_ref, rankp, ranks):
    c = LANES
    r_i = lax.broadcasted_iota(I32, (c, c), 0)
    c_i = lax.broadcasted_iota(I32, (c, c), 1)
    upper = (r_i < c_i).astype(BF16)
    e_iota = lax.broadcasted_iota(I32, (N_EXPERTS, c), 0)

    def make_body(top_ref, rank_ref):
        def body(ci, carry):
            lo = pl.multiple_of(ci * c, c)
            top = top_ref[:, pl.ds(lo, c)]
            hit = [e_iota == top[k:k + 1, :] for k in range(TOP_K)]
            cnt = sum(m.astype(F32) for m in hit)
            base = carry + jnp.dot(cnt.astype(BF16), upper, preferred_element_type=F32)
            rank_ref[:, pl.ds(lo, c)] = jnp.concatenate(
                [jnp.sum(jnp.where(m, base, 0.0), axis=0, keepdims=True) for m in hit], axis=0)
            return carry + jnp.sum(cnt, axis=1, keepdims=True)
        return body

    counts = jnp.zeros((N_EXPERTS, 1), F32)
    counts = lax.fori_loop(0, tp_ref.shape[1] // c, make_body(tp_ref, rankp), counts)
    counts = lax.fori_loop(0, ts_ref.shape[1] // c, make_body(ts_ref, ranks), counts)

    padded = jnp.floor((counts + (MOE_BLOCK - 1)) / MOE_BLOCK) * MOE_BLOCK
    er = lax.broadcasted_iota(I32, (N_EXPERTS, N_EXPERTS), 0)
    ec = lax.broadcasted_iota(I32, (N_EXPERTS, N_EXPERTS), 1)
    padded_row = jnp.sum(jnp.where(er == ec, padded, 0.0), axis=0, keepdims=True)
    cum = jnp.sum(jnp.where(ec <= er, padded_row, 0.0), axis=1, keepdims=True)
    pstart = cum - padded

    def dest(top_ref, rank_ref, out_ref):
        n = top_ref.shape[1]
        ei = lax.broadcasted_iota(I32, (N_EXPERTS, n), 0)
        rows = []
        for k in range(TOP_K):
            start = jnp.sum(jnp.where(ei == top_ref[k:k + 1, :], pstart, 0.0), axis=0, keepdims=True)
            rows.append(start + rank_ref[k:k + 1, :])
        out_ref[...] = jnp.concatenate(rows, axis=0).astype(I32)

    dest(tp_ref, rankp, destp_ref)
    dest(ts_ref, ranks, dests_ref)

    nb = blke_ref.shape[1]
    blk_lo = lax.broadcasted_iota(I32, (N_EXPERTS, nb), 1).astype(F32) * MOE_BLOCK
    blk = jnp.sum((cum <= blk_lo).astype(F32), axis=0, keepdims=True)
    blke_ref[...] = jnp.minimum(blk, N_EXPERTS - 1).astype(I32)
    nact_ref[...] = (cum[N_EXPERTS - 1:N_EXPERTS, :] / MOE_BLOCK).astype(I32)


def _slots(top_p, top_s):
    n_p, n_s = top_p.shape[1], top_s.shape[1]
    nb = _moe_num_blocks((n_p + n_s) * TOP_K)
    return pl.pallas_call(
        _slots_kernel,
        grid=(1,),
        in_specs=[_full((TOP_K, n_p)), _full((TOP_K, n_s))],
        out_specs=[_full((TOP_K, n_p)), _full((TOP_K, n_s)), _full((1, nb)), _full((1, 1))],
        out_shape=[
            jax.ShapeDtypeStruct((TOP_K, n_p), I32),
            jax.ShapeDtypeStruct((TOP_K, n_s), I32),
            jax.ShapeDtypeStruct((1, nb), I32),
            jax.ShapeDtypeStruct((1, 1), I32),
        ],
        scratch_shapes=[pltpu.VMEM((TOP_K, n_p), F32), pltpu.VMEM((TOP_K, n_s), F32)],
        compiler_params=_cparams(("arbitrary",)),
        name="moe_slots",
    )(top_p, top_s)


def _row_copy(src_ref, src_row, dst_ref, dst_row, sem):
    return pltpu.make_async_copy(src_ref.at[pl.ds(src_row, 1)], dst_ref.at[pl.ds(dst_row, 1)], sem)


def _dispatch_kernel(dest_ref, xn_ref, xs_in_ref, xs_ref, sem):
    del xs_in_ref
    tile = xn_ref.shape[0]

    def start(t, c):
        for k in range(TOP_K):
            _row_copy(xn_ref, t, xs_ref, dest_ref[k, t], sem).start()
        return c

    def wait(t, c):
        for k in range(TOP_K):
            _row_copy(xn_ref, t, xs_ref, dest_ref[k, t], sem).wait()
        return c

    lax.fori_loop(0, tile, start, 0)
    lax.fori_loop(0, tile, wait, 0)


def _dispatch(dest, xn, xs):
    n = xn.shape[0]
    tile = min(DISPATCH_TILE, n)
    return pl.pallas_call(
        _dispatch_kernel,
        grid=(n // tile,),
        in_specs=[
            pl.BlockSpec((TOP_K, tile), lambda i: (0, i), memory_space=pltpu.SMEM),
            pl.BlockSpec((tile, D_MODEL), lambda i: (i, 0)),
            pl.BlockSpec(memory_space=pl.ANY),
        ],
        out_specs=pl.BlockSpec(memory_space=pl.ANY),
        out_shape=jax.ShapeDtypeStruct(xs.shape, xs.dtype),
        scratch_shapes=[pltpu.SemaphoreType.DMA],
        input_output_aliases={2: 0},
        compiler_params=_cparams(("arbitrary",)),
        name="moe_dispatch",
    )(dest, xn, xs)


def _combine_kernel(dest_ref, gates_ref, x2_ref, gfin_ref, ys_ref, y_ref, buf, sem):
    tile = x2_ref.shape[0]

    def start(t, c):
        for k in range(TOP_K):
            _row_copy(ys_ref, dest_ref[k, t], buf.at[k], t, sem).start()
        return c

    def wait(t, c):
        for k in range(TOP_K):
            _row_copy(ys_ref, dest_ref[k, t], buf.at[k], t, sem).wait()
        return c

    lax.fori_loop(0, tile, start, 0)
    lax.fori_loop(0, tile, wait, 0)
    acc = x2_ref[...]
    for k in range(TOP_K):
        acc = acc + gates_ref[:, k:k + 1] * buf[k]
    y_ref[...] = _rms(acc, gfin_ref[...])


def _combine(dest, gates_t, x2, g_final, ys):
    n = x2.shape[0]
    tile = min(COMBINE_TILE, n)
    row = pl.BlockSpec((tile, D_MODEL), lambda i: (i, 0))
    return pl.pallas_call(
        _combine_kernel,
        grid=(n // tile,),
        in_specs=[
            pl.BlockSpec((TOP_K, tile), lambda i: (0, i), memory_space=pltpu.SMEM),
            pl.BlockSpec((tile, TOP_K), lambda i: (i, 0)),
            row,
            _full((1, D_MODEL)),
            pl.BlockSpec(memory_space=pl.ANY),
        ],
        out_specs=row,
        out_shape=jax.ShapeDtypeStruct((n, D_MODEL), F32),
        scratch_shapes=[pltpu.VMEM((TOP_K, tile, D_MODEL), F32), pltpu.SemaphoreType.DMA],
        compiler_params=_cparams(("arbitrary",)),
        name="moe_combine",
    )(dest, gates_t, x2, g_final, ys)


def _experts_kernel(blke_ref, nact_ref, xs_ref, w1_ref, b1_ref, w2_ref, b2_ref, ys_ref):
    active = pl.program_id(0) < nact_ref[0]

    @pl.when(active)
    def _():
        gu = _bdot(xs_ref[...], w1_ref[0]) + b1_ref[0]
        g = jnp.minimum(gu[:, :D_EXPERT], SWIGLU_LIMIT)
        u = jnp.clip(gu[:, D_EXPERT:], -SWIGLU_LIMIT, SWIGLU_LIMIT)
        h = (u + 1.0) * (g * jax.nn.sigmoid(SWIGLU_ALPHA * g))
        ys_ref[...] = _bdot(h, w2_ref[0]) + b2_ref[0]

    @pl.when(jnp.logical_not(active))
    def _():
        ys_ref[...] = jnp.zeros(ys_ref.shape, F32)


def _experts(blk_e, nact, xs, w1, b1, w2, b2):
    nb = xs.shape[0] // MOE_BLOCK
    blk = lambda j, be, na: (jnp.minimum(j, na[0] - 1), 0)
    exp = lambda j, be, na: (be[jnp.minimum(j, na[0] - 1)], 0, 0)
    return pl.pallas_call(
        _experts_kernel,
        grid_spec=pltpu.PrefetchScalarGridSpec(
            num_scalar_prefetch=2,
            grid=(nb,),
            in_specs=[
                pl.BlockSpec((MOE_BLOCK, D_MODEL), blk),
                pl.BlockSpec((1, D_MODEL, 2 * D_EXPERT), exp),
                pl.BlockSpec((1, 1, 2 * D_EXPERT), exp),
                pl.BlockSpec((1, D_EXPERT, D_MODEL), exp),
                pl.BlockSpec((1, 1, D_MODEL), exp),
            ],
            out_specs=pl.BlockSpec((MOE_BLOCK, D_MODEL), lambda j, be, na: (j, 0)),
        ),
        out_shape=jax.ShapeDtypeStruct(xs.shape, F32),
        compiler_params=_cparams(("arbitrary",)),
        name="moe_experts",
    )(blk_e, nact, xs, w1, b1, w2, b2)


def _block_diag(w):
    h, d, _ = w.shape
    eye = jnp.eye(h, dtype=w.dtype)
    return (eye[:, None, :, None] * w[:, :, None, :]).reshape(h * d, h * d)


def kernel(x_prompt, x_sample, state_lru_h, cache_lru_conv, cache_cf_conv, cache_mem_k, cache_mem_v,
           mem_prompt, g_mix, w_in, w_lru_conv, b_lru_conv, w_lru_a, b_lru_a, w_lru_x, b_lru_x,
           lru_lambda, w_cf_conv, b_cf_conv, g_cf_ln, b_cf_ln, w_out, g_xa, g_mem, w_q, w_k, w_v, w_o,
           g_moe, w_router, b_router, w_e1, b_e1, w_e2, b_e2, g_final):
    depth = g_mix.shape[0]
    assert depth == 1, "single-layer stack"
    batch, seq, _ = x_prompt.shape
    n_s = x_sample.shape[0]
    assert x_sample.shape[1] == 1
    n_p = batch * seq
    l = 0

    mix_w = (
        g_mix[l][None], w_in[l].astype(BF16), w_lru_conv[l], b_lru_conv[l][None],
        jnp.concatenate([_block_diag(w_lru_a[l]), _block_diag(w_lru_x[l])], axis=1).astype(BF16),
        jnp.concatenate([b_lru_a[l], b_lru_x[l]])[None], lru_lambda[l][None],
        w_cf_conv[l], b_cf_conv[l][None], g_cf_ln[l][None], b_cf_ln[l][None], w_out[l].astype(BF16),
    )
    router_w = (g_moe[l][None], w_router[l].T.astype(BF16), b_router[l][:, None])
    wq, wo = w_q[l].astype(BF16), w_o[l].astype(BF16)
    gxa = g_xa[l][None]

    xp = x_prompt.reshape(n_p, D_MODEL)
    xs_ = x_sample.reshape(n_s, D_MODEL)
    x1p, hp, lcp, ccp = _mixer_prompt(xp, mix_w, batch, seq)
    x1s, hs, zx_s, glu_s = _mixer_sample(xs_, mix_w, state_lru_h[l], cache_lru_conv[l], cache_cf_conv[l])

    mk, mv = _memkv(mem_prompt.reshape(batch * N_MEM, D_MODEL), g_mem[l][None],
                    w_k[l].astype(BF16), w_v[l].astype(BF16))
    x2p, xnp_, top_p, gates_p = _attn_prompt(x1p, mk, mv, gxa, wq, wo, router_w, batch, seq)
    x2s, xns, top_s, gates_s = _attn_sample(
        x1s, cache_mem_k[l].reshape(n_s, N_MEM, D_MODEL), cache_mem_v[l].reshape(n_s, N_MEM, D_MODEL),
        gxa, wq, wo, router_w)

    dest_p, dest_s, blk_e, nact = _slots(top_p, top_s)
    nb = _moe_num_blocks((n_p + n_s) * TOP_K)
    xs = jnp.zeros((nb * MOE_BLOCK, D_MODEL), F32)
    xs = _dispatch(dest_p, xnp_, xs)
    xs = _dispatch(dest_s, xns, xs)
    ys = _experts(blk_e.reshape(nb), nact.reshape(1), xs, w_e1[l].astype(BF16), b_e1[l][:, None, :],
                  w_e2[l].astype(BF16), b_e2[l][:, None, :])
    gfin = g_final[None]
    y_p = _combine(dest_p, gates_p.T, x2p, gfin, ys)
    y_s = _combine(dest_s, gates_s.T, x2s, gfin, ys)

    lru_conv_s = jnp.concatenate([cache_lru_conv[l][:, 1:], zx_s[:, None]], axis=1)
    cf_conv_s = jnp.concatenate([cache_cf_conv[l][:, 1:], glu_s[:, None]], axis=1)
    return (
        y_p.reshape(batch, seq, D_MODEL),
        y_s.reshape(n_s, 1, D_MODEL),
        hp.reshape(depth, batch, LRU_WIDTH),
        lcp[None],
        ccp[None],
        mk.reshape(depth, batch, N_MEM, XA_HEADS, XA_HEAD_DIM),
        mv.reshape(depth, batch, N_MEM, XA_HEADS, XA_HEAD_DIM),
        hs[None],
        lru_conv_s[None],
        cf_conv_s[None],
    )
```

```python
import functools

import jax
import jax.numpy as jnp
from jax import lax
from jax.experimental import pallas as pl
from jax.experimental.pallas import tpu as pltpu

F32 = jnp.float32
BF16 = jnp.bfloat16
I32 = jnp.int32

D_MODEL = 1024
LRU_WIDTH = 512
CF_WIDTH = 512
LRU_HEADS = 8
LRU_CONV = 4
LRU_C = 8.0
CF_CONV = 31
IN_WIDTH = 2 * LRU_WIDTH + 2 * CF_WIDTH
N_MEM = 256
XA_HEADS = 4
XA_HEAD_DIM = D_MODEL // XA_HEADS
N_EXPERTS = 32
TOP_K = 4
D_EXPERT = D_MODEL
SWIGLU_LIMIT = 7.0
SWIGLU_ALPHA = 1.702
EPS = 1e-6

LANES = 128
SUBLANES = 8
VMEM_LIMIT = 56 * 1024 * 1024

MIX_TILE = 256
LRU_TAIL_ROWS = 8
CF_TAIL_ROWS = 32
ATT_TILE = 256
KV_TILE = 256
SAMPLE_GROUP = 2
SAMPLE_MIX_GROUP = 32
MOE_BLOCK = 512
DMA_UNROLL = 8
DISPATCH_TILE = 512
COMBINE_TILE = 256


def _cparams(sem):
    return pltpu.CompilerParams(dimension_semantics=sem, vmem_limit_bytes=VMEM_LIMIT)


def _full(shape):
    n = len(shape)
    return pl.BlockSpec(shape, lambda *_: (0,) * n)


def _rms(x, g):
    return x * lax.rsqrt(jnp.mean(x * x, axis=-1, keepdims=True) + EPS) * g


def _bdot(a, b):
    return jnp.dot(a.astype(BF16), b, preferred_element_type=F32)


def _neg_expm1(y):
    u = jnp.exp(y)
    d = 1.0 - u
    return jnp.where(d == 0.0, -y, jnp.where(u == 0.0, 1.0, d * y / jnp.log(u)))


def _lru_coeffs(xc, wg_ref, bg_ref, lam_ref):
    gl = _bdot(xc, wg_ref[...]) + bg_ref[...]
    r = jax.nn.sigmoid(gl[:, :LRU_WIDTH])
    i = jax.nn.sigmoid(gl[:, LRU_WIDTH:])
    log_a = -LRU_C * r * jax.nn.softplus(-lam_ref[...])
    a = jnp.exp(log_a)
    u = jnp.sqrt(_neg_expm1(2.0 * log_a)) * (i * xc)
    return a, u


def _cf_post(gc, gln_ref, bln_ref):
    mu = jnp.mean(gc, axis=-1, keepdims=True)
    c = gc - mu
    y = c * lax.rsqrt(jnp.mean(c * c, axis=-1, keepdims=True) + EPS)
    return jax.nn.silu(y * gln_ref[...] + bln_ref[...])


def _mixer_out(x, y_lru, y_cf, wout_ref):
    y = _bdot(y_lru, wout_ref[:LRU_WIDTH, :]) + _bdot(y_cf, wout_ref[LRU_WIDTH:, :])
    return x + y


def _mixer_prompt_kernel(x_ref, gmix_ref, win_ref, wlc_ref, blc_ref, wg_ref, bg_ref, lam_ref,
                         wcc_ref, bcc_ref, gln_ref, bln_ref, wout_ref,
                         x1_ref, h_ref, ltail_ref, ctail_ref,
                         zx_ext, glu_ext, h_carry):
    tt = MIX_TILE
    j = pl.program_id(1)

    @pl.when(j == 0)
    def _():
        zx_ext[0:LRU_TAIL_ROWS, :] = jnp.zeros((LRU_TAIL_ROWS, LRU_WIDTH), F32)
        glu_ext[0:CF_TAIL_ROWS, :] = jnp.zeros((CF_TAIL_ROWS, CF_WIDTH), F32)
        h_carry[...] = jnp.zeros((1, LRU_WIDTH), F32)

    x = x_ref[...]
    z = _bdot(_rms(x, gmix_ref[...]), win_ref[...])
    zx = z[:, :LRU_WIDTH]
    zg = z[:, LRU_WIDTH:2 * LRU_WIDTH]
    za = z[:, 2 * LRU_WIDTH:2 * LRU_WIDTH + CF_WIDTH]
    zb = z[:, 2 * LRU_WIDTH + CF_WIDTH:]

    zx_ext[LRU_TAIL_ROWS:LRU_TAIL_ROWS + tt, :] = zx
    off = LRU_TAIL_ROWS - (LRU_CONV - 1)
    xc = blc_ref[...]
    for k in range(LRU_CONV):
        xc = xc + wlc_ref[k:k + 1, :] * zx_ext[off + k:off + k + tt, :]
    a, u = _lru_coeffs(xc, wg_ref, bg_ref, lam_ref)
    row = lax.broadcasted_iota(I32, (tt, LRU_WIDTH), 0)
    d = 1
    while d < tt:
        keep = row >= d
        a_sh = jnp.where(keep, pltpu.roll(a, d, 0), 1.0)
        u_sh = jnp.where(keep, pltpu.roll(u, d, 0), 0.0)
        u = u + a * u_sh
        a = a * a_sh
        d *= 2
    hs = a * h_carry[...] + u
    h_last = hs[tt - 1:tt, :]
    h_carry[...] = h_last
    y_lru = hs * jax.nn.gelu(zg)

    glu = za * jax.nn.sigmoid(zb)
    glu_ext[CF_TAIL_ROWS:CF_TAIL_ROWS + tt, :] = glu
    off = CF_TAIL_ROWS - (CF_CONV - 1)
    gc = bcc_ref[...]
    for k in range(CF_CONV):
        gc = gc + wcc_ref[k:k + 1, :] * glu_ext[off + k:off + k + tt, :]
    y_cf = _cf_post(gc, gln_ref, bln_ref)

    x1_ref[...] = _mixer_out(x, y_lru, y_cf, wout_ref)

    h_ref[0] = h_last
    ltail_ref[0] = zx_ext[LRU_TAIL_ROWS + tt - (LRU_CONV - 1):LRU_TAIL_ROWS + tt, :]
    ctail_ref[0] = glu_ext[CF_TAIL_ROWS + tt - (CF_CONV - 1):CF_TAIL_ROWS + tt, :]
    zx_ext[0:LRU_TAIL_ROWS, :] = zx_ext[tt:tt + LRU_TAIL_ROWS, :]
    glu_ext[0:CF_TAIL_ROWS, :] = glu_ext[tt:tt + CF_TAIL_ROWS, :]


def _mixer_weight_specs():
    return [
        _full((1, D_MODEL)),
        _full((D_MODEL, IN_WIDTH)),
        _full((LRU_CONV, LRU_WIDTH)), _full((1, LRU_WIDTH)),
        _full((LRU_WIDTH, 2 * LRU_WIDTH)), _full((1, 2 * LRU_WIDTH)),
        _full((1, LRU_WIDTH)),
        _full((CF_CONV, CF_WIDTH)), _full((1, CF_WIDTH)),
        _full((1, CF_WIDTH)), _full((1, CF_WIDTH)),
        _full((D_MODEL, D_MODEL)),
    ]


def _mixer_prompt(x, mix_w, batch, seq):
    tt = MIX_TILE
    nt = seq // tt
    return pl.pallas_call(
        _mixer_prompt_kernel,
        grid=(batch, nt),
        in_specs=[pl.BlockSpec((tt, D_MODEL), lambda b, j: (b * nt + j, 0))] + _mixer_weight_specs(),
        out_specs=[
            pl.BlockSpec((tt, D_MODEL), lambda b, j: (b * nt + j, 0)),
            pl.BlockSpec((1, 1, LRU_WIDTH), lambda b, j: (b, 0, 0)),
            pl.BlockSpec((1, LRU_CONV - 1, LRU_WIDTH), lambda b, j: (b, 0, 0)),
            pl.BlockSpec((1, CF_CONV - 1, CF_WIDTH), lambda b, j: (b, 0, 0)),
        ],
        out_shape=[
            jax.ShapeDtypeStruct((batch * seq, D_MODEL), F32),
            jax.ShapeDtypeStruct((batch, 1, LRU_WIDTH), F32),
            jax.ShapeDtypeStruct((batch, LRU_CONV - 1, LRU_WIDTH), F32),
            jax.ShapeDtypeStruct((batch, CF_CONV - 1, CF_WIDTH), F32),
        ],
        scratch_shapes=[
            pltpu.VMEM((LRU_TAIL_ROWS + tt, LRU_WIDTH), F32),
            pltpu.VMEM((CF_TAIL_ROWS + tt, CF_WIDTH), F32),
            pltpu.VMEM((1, LRU_WIDTH), F32),
        ],
        compiler_params=_cparams(("arbitrary", "arbitrary")),
        name="mixer_prompt",
    )(x, *mix_w)


def _mixer_sample_kernel(x_ref, gmix_ref, win_ref, wlc_ref, blc_ref, wg_ref, bg_ref, lam_ref,
                         wcc_ref, bcc_ref, gln_ref, bln_ref, wout_ref,
                         h0_ref, lbuf_ref, cbuf_ref,
                         x1_ref, h_ref, zx_ref, glu_ref):
    x = x_ref[...]
    z = _bdot(_rms(x, gmix_ref[...]), win_ref[...])
    zx = z[:, :LRU_WIDTH]
    zg = z[:, LRU_WIDTH:2 * LRU_WIDTH]
    za = z[:, 2 * LRU_WIDTH:2 * LRU_WIDTH + CF_WIDTH]
    zb = z[:, 2 * LRU_WIDTH + CF_WIDTH:]

    xc = blc_ref[...] + wlc_ref[LRU_CONV - 1:LRU_CONV, :] * zx
    xc = xc + jnp.sum(lbuf_ref[...] * wlc_ref[0:LRU_CONV - 1, :][None], axis=1)
    a, u = _lru_coeffs(xc, wg_ref, bg_ref, lam_ref)
    h = a * h0_ref[...] + u
    y_lru = h * jax.nn.gelu(zg)

    glu = za * jax.nn.sigmoid(zb)
    gc = bcc_ref[...] + wcc_ref[CF_CONV - 1:CF_CONV, :] * glu
    gc = gc + jnp.sum(cbuf_ref[...] * wcc_ref[0:CF_CONV - 1, :][None], axis=1)
    y_cf = _cf_post(gc, gln_ref, bln_ref)

    x1_ref[...] = _mixer_out(x, y_lru, y_cf, wout_ref)
    h_ref[...] = h
    zx_ref[...] = zx
    glu_ref[...] = glu


def _mixer_sample(x, mix_w, h0, lbuf, cbuf):
    n = x.shape[0]
    g = SAMPLE_MIX_GROUP
    row = lambda w: pl.BlockSpec((g, w), lambda i: (i, 0))
    return pl.pallas_call(
        _mixer_sample_kernel,
        grid=(n // g,),
        in_specs=[row(D_MODEL)] + _mixer_weight_specs() + [
            row(LRU_WIDTH),
            pl.BlockSpec((g, LRU_CONV - 1, LRU_WIDTH), lambda i: (i, 0, 0)),
            pl.BlockSpec((g, CF_CONV - 1, CF_WIDTH), lambda i: (i, 0, 0)),
        ],
        out_specs=[row(D_MODEL), row(LRU_WIDTH), row(LRU_WIDTH), row(CF_WIDTH)],
        out_shape=[
            jax.ShapeDtypeStruct((n, D_MODEL), F32),
            jax.ShapeDtypeStruct((n, LRU_WIDTH), F32),
            jax.ShapeDtypeStruct((n, LRU_WIDTH), F32),
            jax.ShapeDtypeStruct((n, CF_WIDTH), F32),
        ],
        compiler_params=_cparams(("arbitrary",)),
        name="mixer_sample",
    )(x, *mix_w, h0, lbuf, cbuf)


def _memkv_kernel(mem_ref, gmem_ref, wk_ref, wv_ref, k_ref, v_ref):
    mn = _rms(mem_ref[...], gmem_ref[...]).astype(BF16)
    k_ref[...] = jnp.dot(mn, wk_ref[...], preferred_element_type=F32)
    v_ref[...] = jnp.dot(mn, wv_ref[...], preferred_element_type=F32)


def _memkv(mem, g_mem, wk, wv):
    n = mem.shape[0]
    t = KV_TILE
    row = pl.BlockSpec((t, D_MODEL), lambda i: (i, 0))
    return pl.pallas_call(
        _memkv_kernel,
        grid=(n // t,),
        in_specs=[row, _full((1, D_MODEL)), _full((D_MODEL, D_MODEL)), _full((D_MODEL, D_MODEL))],
        out_specs=[row, row],
        out_shape=[jax.ShapeDtypeStruct((n, D_MODEL), F32)] * 2,
        compiler_params=_cparams(("arbitrary",)),
        name="memory_kv",
    )(mem, g_mem, wk, wv)


def _router(x2, gmoe_ref, wrt_ref, brt_ref, xn_ref, tope_ref, gates_ref):
    xn = _rms(x2, gmoe_ref[...])
    xn_ref[...] = xn
    logits = lax.dot_general(wrt_ref[...], xn.astype(BF16), (((1,), (1,)), ((), ())),
                             preferred_element_type=F32) + brt_ref[...]
    e_iota = lax.broadcasted_iota(I32, logits.shape, 0)
    work = logits
    vals, idxs = [], []
    for _ in range(TOP_K):
        m = jnp.max(work, axis=0, keepdims=True)
        idx = jnp.min(jnp.where(work == m, e_iota, N_EXPERTS), axis=0, keepdims=True)
        vals.append(m)
        idxs.append(idx)
        work = jnp.where(e_iota == idx, -jnp.inf, work)
    ex = [jnp.exp(v - vals[0]) for v in vals]
    den = ex[0] + ex[1] + ex[2] + ex[3]
    tope_ref[...] = jnp.concatenate(idxs, axis=0)
    gates_ref[...] = jnp.concatenate([e / den for e in ex], axis=0)


def _router_specs():
    return [_full((1, D_MODEL)), _full((N_EXPERTS, D_MODEL)), _full((N_EXPERTS, 1))]


def _router_out(n, tile, index):
    specs = [
        pl.BlockSpec((tile, D_MODEL), lambda *a: (index(*a), 0)),
        pl.BlockSpec((tile, D_MODEL), lambda *a: (index(*a), 0)),
        pl.BlockSpec((TOP_K, tile), lambda *a: (0, index(*a))),
        pl.BlockSpec((TOP_K, tile), lambda *a: (0, index(*a))),
    ]
    shapes = [
        jax.ShapeDtypeStruct((n, D_MODEL), F32),
        jax.ShapeDtypeStruct((n, D_MODEL), F32),
        jax.ShapeDtypeStruct((TOP_K, n), I32),
        jax.ShapeDtypeStruct((TOP_K, n), F32),
    ]
    return specs, shapes


def _attn_prompt_kernel(x1_ref, k_ref, v_ref, gxa_ref, wq_ref, wo_ref, gmoe_ref, wrt_ref, brt_ref,
                        x2_ref, xn_ref, tope_ref, gates_ref):
    x1 = x1_ref[...]
    q = (_bdot(_rms(x1, gxa_ref[...]), wq_ref[...]) * (XA_HEAD_DIM ** -0.5)).astype(BF16)
    outs = []
    for h in range(XA_HEADS):
        sl = slice(h * XA_HEAD_DIM, (h + 1) * XA_HEAD_DIM)
        kh = k_ref[:, sl].astype(BF16)
        vh = v_ref[:, sl].astype(BF16)
        s = lax.dot_general(q[:, sl], kh, (((1,), (1,)), ((), ())), preferred_element_type=F32)
        p = jnp.exp(s - jnp.max(s, axis=-1, keepdims=True))
        p = p / jnp.sum(p, axis=-1, keepdims=True)
        outs.append(_bdot(p, vh))
    o = jnp.concatenate(outs, axis=-1)
    x2 = x1 + _bdot(o, wo_ref[...])
    x2_ref[...] = x2
    _router(x2, gmoe_ref, wrt_ref, brt_ref, xn_ref, tope_ref, gates_ref)


def _attn_prompt(x1, k, v, gxa, wq, wo, router_w, batch, seq):
    tq = ATT_TILE
    nt = seq // tq
    out_specs, out_shapes = _router_out(batch * seq, tq, lambda b, j: b * nt + j)
    kv = pl.BlockSpec((N_MEM, D_MODEL), lambda b, j: (b, 0))
    return pl.pallas_call(
        _attn_prompt_kernel,
        grid=(batch, nt),
        in_specs=[pl.BlockSpec((tq, D_MODEL), lambda b, j: (b * nt + j, 0)), kv, kv,
                  _full((1, D_MODEL)), _full((D_MODEL, D_MODEL)), _full((D_MODEL, D_MODEL))] + _router_specs(),
        out_specs=out_specs,
        out_shape=out_shapes,
        compiler_params=_cparams(("arbitrary", "arbitrary")),
        name="attn_prompt",
    )(x1, k, v, gxa, wq, wo, *router_w)


def _qproj_kernel(x1_ref, gxa_ref, wq_ref, q_ref):
    q_ref[...] = _bdot(_rms(x1_ref[...], gxa_ref[...]), wq_ref[...]) * (XA_HEAD_DIM ** -0.5)


def _attn_sample_core_kernel(q_ref, k_ref, v_ref, o_ref):
    q = q_ref[...]
    s = jnp.sum(k_ref[...] * q[:, None], axis=-1, keepdims=True)
    p = jnp.exp(s - jnp.max(s, axis=1, keepdims=True))
    p = p / jnp.sum(p, axis=1, keepdims=True)
    o_ref[...] = jnp.sum(p * v_ref[...], axis=1)


def _oproj_router_kernel(x1_ref, o_ref, wo_ref, gmoe_ref, wrt_ref, brt_ref,
                         x2_ref, xn_ref, tope_ref, gates_ref):
    x2 = x1_ref[...] + _bdot(o_ref[...], wo_ref[...])
    x2_ref[...] = x2
    _router(x2, gmoe_ref, wrt_ref, brt_ref, xn_ref, tope_ref, gates_ref)


def _attn_sample(x1, k, v, gxa, wq, wo, router_w):
    n = x1.shape[0]
    g = SAMPLE_GROUP
    q = pl.pallas_call(
        _qproj_kernel,
        grid=(1,),
        in_specs=[_full((n, D_MODEL)), _full((1, D_MODEL)), _full((D_MODEL, D_MODEL))],
        out_specs=_full((n, D_MODEL)),
        out_shape=jax.ShapeDtypeStruct((n, D_MODEL), F32),
        compiler_params=_cparams(("arbitrary",)),
        name="qproj_sample",
    )(x1, gxa, wq)
    kv = pl.BlockSpec((g, N_MEM, XA_HEADS, XA_HEAD_DIM), lambda i: (i, 0, 0, 0))
    row = pl.BlockSpec((g, XA_HEADS, XA_HEAD_DIM), lambda i: (i, 0, 0))
    o = pl.pallas_call(
        _attn_sample_core_kernel,
        grid=(n // g,),
        in_specs=[row, kv, kv],
        out_specs=row,
        out_shape=jax.ShapeDtypeStruct((n, XA_HEADS, XA_HEAD_DIM), F32),
        compiler_params=_cparams(("arbitrary",)),
        name="attn_sample_core",
    )(q.reshape(n, XA_HEADS, XA_HEAD_DIM), k, v).reshape(n, D_MODEL)
    out_specs, out_shapes = _router_out(n, n, lambda i: 0)
    return pl.pallas_call(
        _oproj_router_kernel,
        grid=(1,),
        in_specs=[_full((n, D_MODEL)), _full((n, D_MODEL)), _full((D_MODEL, D_MODEL))] + _router_specs(),
        out_specs=out_specs,
        out_shape=out_shapes,
        compiler_params=_cparams(("arbitrary",)),
        name="oproj_router_sample",
    )(x1, o, wo, *router_w)


def _moe_num_blocks(n_assign):
    return pl.cdiv(n_assign, MOE_BLOCK) + N_EXPERTS


def _slots_kernel(tp_ref, ts_ref, destp_ref, dests_ref, blke_ref, nact_ref, cnt_ref, pstart_ref,
                  rankp, ranks):
    c = LANES
    r_i = lax.broadcasted_iota(I32, (c, c), 0)
    c_i = lax.broadcasted_iota(I32, (c, c), 1)
    upper = (r_i < c_i).astype(BF16)
    e_iota = lax.broadcasted_iota(I32, (N_EXPERTS, c), 0)

    def make_body(top_ref, rank_ref):
        def body(ci, carry):
            lo = pl.multiple_of(ci * c, c)
            top = top_ref[:, pl.ds(lo, c)]
            hit = [e_iota == top[k:k + 1, :] for k in range(TOP_K)]
            cnt = sum(m.astype(F32) for m in hit)
            base = carry + jnp.dot(cnt.astype(BF16), upper, preferred_element_type=F32)
            rank_ref[:, pl.ds(lo, c)] = jnp.concatenate(
                [jnp.sum(jnp.where(m, base, 0.0), axis=0, keepdims=True) for m in hit], axis=0)
            return carry + jnp.sum(cnt, axis=1, keepdims=True)
        return body

    counts = jnp.zeros((N_EXPERTS, 1), F32)
    counts = lax.fori_loop(0, tp_ref.shape[1] // c, make_body(tp_ref, rankp), counts)
    counts = lax.fori_loop(0, ts_ref.shape[1] // c, make_body(ts_ref, ranks), counts)

    padded = jnp.floor((counts + (MOE_BLOCK - 1)) / MOE_BLOCK) * MOE_BLOCK
    er = lax.broadcasted_iota(I32, (N_EXPERTS, N_EXPERTS), 0)
    ec = lax.broadcasted_iota(I32, (N_EXPERTS, N_EXPERTS), 1)
    padded_row = jnp.sum(jnp.where(er == ec, padded, 0.0), axis=0, keepdims=True)
    cum = jnp.sum(jnp.where(ec <= er, padded_row, 0.0), axis=1, keepdims=True)
    pstart = cum - padded
    cnt_ref[...] = jnp.sum(jnp.where(er == ec, counts, 0.0), axis=0, keepdims=True).astype(I32)
    pstart_ref[...] = jnp.sum(jnp.where(er == ec, pstart, 0.0), axis=0, keepdims=True).astype(I32)

    def dest(top_ref, rank_ref, out_ref):
        n = top_ref.shape[1]
        ei = lax.broadcasted_iota(I32, (N_EXPERTS, n), 0)
        rows = []
        for k in range(TOP_K):
            start = jnp.sum(jnp.where(ei == top_ref[k:k + 1, :], pstart, 0.0), axis=0, keepdims=True)
            rows.append(start + rank_ref[k:k + 1, :])
        out_ref[...] = jnp.concatenate(rows, axis=0).astype(I32)

    dest(tp_ref, rankp, destp_ref)
    dest(ts_ref, ranks, dests_ref)

    nb = blke_ref.shape[1]
    blk_lo = lax.broadcasted_iota(I32, (N_EXPERTS, nb), 1).astype(F32) * MOE_BLOCK
    blk = jnp.sum((cum <= blk_lo).astype(F32), axis=0, keepdims=True)
    blke_ref[...] = jnp.minimum(blk, N_EXPERTS - 1).astype(I32)
    nact_ref[...] = (cum[N_EXPERTS - 1:N_EXPERTS, :] / MOE_BLOCK).astype(I32)


def _slots(top_p, top_s):
    n_p, n_s = top_p.shape[1], top_s.shape[1]
    nb = _moe_num_blocks((n_p + n_s) * TOP_K)
    return pl.pallas_call(
        _slots_kernel,
        grid=(1,),
        in_specs=[_full((TOP_K, n_p)), _full((TOP_K, n_s))],
        out_specs=[_full((TOP_K, n_p)), _full((TOP_K, n_s)), _full((1, nb)), _full((1, 1)),
                   _full((1, N_EXPERTS)), _full((1, N_EXPERTS))],
        out_shape=[
            jax.ShapeDtypeStruct((TOP_K, n_p), I32),
            jax.ShapeDtypeStruct((TOP_K, n_s), I32),
            jax.ShapeDtypeStruct((1, nb), I32),
            jax.ShapeDtypeStruct((1, 1), I32),
            jax.ShapeDtypeStruct((1, N_EXPERTS), I32),
            jax.ShapeDtypeStruct((1, N_EXPERTS), I32),
        ],
        scratch_shapes=[pltpu.VMEM((TOP_K, n_p), F32), pltpu.VMEM((TOP_K, n_s), F32)],
        compiler_params=_cparams(("arbitrary",)),
        name="moe_slots",
    )(top_p, top_s)


def _row_copy(src_ref, src_row, dst_ref, dst_row, sem):
    return pltpu.make_async_copy(src_ref.at[pl.ds(src_row, 1)], dst_ref.at[pl.ds(dst_row, 1)], sem)


def _for_rows(n_rows, fn):
    def body(i, c):
        for u in range(DMA_UNROLL):
            fn(i * DMA_UNROLL + u)
        return c
    lax.fori_loop(0, n_rows // DMA_UNROLL, body, 0)


def _dispatch_kernel(cnt_ref, pstart_ref, nact_ref, destp_ref, dests_ref, xnp_ref, xns_ref, xs_ref,
                     zeros, sem, zsem):
    i = pl.program_id(0)
    last = pl.num_programs(0) - 1

    def scatter(x_ref, dest_ref):
        n = x_ref.shape[0]

        def one(t):
            for k in range(TOP_K):
                _row_copy(x_ref, t, xs_ref, dest_ref[k, t], sem).start(priority=k % 2)

        _for_rows(n, one)
        rows = xs_ref.at[pl.ds(0, n * TOP_K)]
        pltpu.make_async_copy(rows, rows, sem).wait()

    @pl.when(i < last)
    def _():
        scatter(xnp_ref, destp_ref)

    @pl.when(i == last)
    def _():
        scatter(xns_ref, dests_ref)
        zeros[...] = jnp.zeros(zeros.shape, F32)

        def pad_copies(e, act):
            c = cnt_ref[e]
            npad = (-c) & (MOE_BLOCK - 1)
            base = pstart_ref[e] + c
            nhead = npad & (SUBLANES - 1)
            for r in range(SUBLANES - 1):
                @pl.when(r < nhead)
                def _():
                    act(_row_copy(zeros, 0, xs_ref, base + r, zsem))
            for bit in range(SUBLANES.bit_length() - 1, MOE_BLOCK.bit_length() - 1):
                size = 1 << bit

                @pl.when((npad & size) != 0)
                def _():
                    off = pl.multiple_of(base + nhead + (npad & (size - SUBLANES)), SUBLANES)
                    act(pltpu.make_async_copy(zeros.at[pl.ds(0, size)], xs_ref.at[pl.ds(off, size)], zsem))

        def tail_copies(act):
            def body(j, c):
                act(pltpu.make_async_copy(zeros, xs_ref.at[pl.ds(j * MOE_BLOCK, MOE_BLOCK)], zsem))
                return c
            lax.fori_loop(nact_ref[0], xs_ref.shape[0] // MOE_BLOCK, body, 0)

        for act in (lambda d: d.start(), lambda d: d.wait()):
            lax.fori_loop(0, N_EXPERTS, lambda e, c, act=act: (pad_copies(e, act), c)[1], 0)
            tail_copies(act)


def _dispatch(cnt, pstart, nact, dest_p, dest_s, xn_p, xn_s, n_rows):
    n_p, n_s = xn_p.shape[0], xn_s.shape[0]
    tile = DISPATCH_TILE
    nt = n_p // tile
    smem = pl.BlockSpec(memory_space=pltpu.SMEM)
    return pl.pallas_call(
        _dispatch_kernel,
        grid=(nt + 1,),
        in_specs=[
            smem, smem, smem,
            pl.BlockSpec((TOP_K, tile), lambda i: (0, jnp.minimum(i, nt - 1)), memory_space=pltpu.SMEM),
            smem,
            pl.BlockSpec((tile, D_MODEL), lambda i: (jnp.minimum(i, nt - 1), 0)),
            _full((n_s, D_MODEL)),
        ],
        out_specs=pl.BlockSpec(memory_space=pl.ANY),
        out_shape=jax.ShapeDtypeStruct((n_rows, D_MODEL), F32),
        scratch_shapes=[pltpu.VMEM((MOE_BLOCK, D_MODEL), F32), pltpu.SemaphoreType.DMA,
                        pltpu.SemaphoreType.DMA],
        compiler_params=_cparams(("arbitrary",)),
        name="moe_dispatch",
    )(cnt, pstart, nact, dest_p, dest_s, xn_p, xn_s)


def _combine_kernel(dest_ref, gates_ref, x2_ref, gfin_ref, ys_ref, y_ref, buf, sem):
    tile = x2_ref.shape[0]

    def one(t):
        for k in range(TOP_K):
            _row_copy(ys_ref, dest_ref[k, t], buf.at[k], t, sem).start(priority=k % 2)

    _for_rows(tile, one)
    pltpu.make_async_copy(buf, buf, sem).wait()
    acc = x2_ref[...]
    for k in range(TOP_K):
        acc = acc + gates_ref[:, k:k + 1] * buf[k]
    y_ref[...] = _rms(acc, gfin_ref[...])


def _combine(dest, gates_t, x2, g_final, ys):
    n = x2.shape[0]
    tile = min(COMBINE_TILE, n)
    row = pl.BlockSpec((tile, D_MODEL), lambda i: (i, 0))
    return pl.pallas_call(
        _combine_kernel,
        grid=(n // tile,),
        in_specs=[
            pl.BlockSpec((TOP_K, tile), lambda i: (0, i), memory_space=pltpu.SMEM),
            pl.BlockSpec((tile, TOP_K), lambda i: (i, 0)),
            row,
            _full((1, D_MODEL)),
            pl.BlockSpec(memory_space=pl.ANY),
        ],
        out_specs=row,
        out_shape=jax.ShapeDtypeStruct((n, D_MODEL), F32),
        scratch_shapes=[pltpu.VMEM((TOP_K, tile, D_MODEL), F32), pltpu.SemaphoreType.DMA],
        compiler_params=_cparams(("arbitrary",)),
        name="moe_combine",
    )(dest, gates_t, x2, g_final, ys)


def _experts_kernel(blke_ref, nact_ref, xs_ref, w1_ref, b1_ref, w2_ref, b2_ref, ys_ref, w1b, w2b):
    j = pl.program_id(0)
    active = j < nact_ref[0]
    new_expert = jnp.logical_or(j == 0, blke_ref[j] != blke_ref[jnp.maximum(j - 1, 0)])

    @pl.when(jnp.logical_and(active, new_expert))
    def _():
        w1b[...] = w1_ref[0].astype(BF16)
        w2b[...] = w2_ref[0].astype(BF16)

    @pl.when(active)
    def _():
        gu = _bdot(xs_ref[...], w1b[...]) + b1_ref[0]
        g = jnp.minimum(gu[:, :D_EXPERT], SWIGLU_LIMIT)
        u = jnp.clip(gu[:, D_EXPERT:], -SWIGLU_LIMIT, SWIGLU_LIMIT)
        h = (u + 1.0) * (g * jax.nn.sigmoid(SWIGLU_ALPHA * g))
        ys_ref[...] = _bdot(h, w2b[...]) + b2_ref[0]

    @pl.when(jnp.logical_not(active))
    def _():
        ys_ref[...] = jnp.zeros(ys_ref.shape, F32)


def _experts(blk_e, nact, xs, w1, b1, w2, b2):
    nb = xs.shape[0] // MOE_BLOCK
    blk = lambda j, be, na: (jnp.minimum(j, na[0] - 1), 0)
    exp = lambda j, be, na: (be[jnp.minimum(j, na[0] - 1)], 0, 0)
    return pl.pallas_call(
        _experts_kernel,
        grid_spec=pltpu.PrefetchScalarGridSpec(
            num_scalar_prefetch=2,
            grid=(nb,),
            in_specs=[
                pl.BlockSpec((MOE_BLOCK, D_MODEL), blk),
                pl.BlockSpec((1, D_MODEL, 2 * D_EXPERT), exp),
                pl.BlockSpec((1, 1, 2 * D_EXPERT), exp),
                pl.BlockSpec((1, D_EXPERT, D_MODEL), exp),
                pl.BlockSpec((1, 1, D_MODEL), exp),
            ],
            out_specs=pl.BlockSpec((MOE_BLOCK, D_MODEL), lambda j, be, na: (j, 0)),
            scratch_shapes=[pltpu.VMEM((D_MODEL, 2 * D_EXPERT), BF16), pltpu.VMEM((D_EXPERT, D_MODEL), BF16)],
        ),
        out_shape=jax.ShapeDtypeStruct(xs.shape, F32),
        compiler_params=_cparams(("arbitrary",)),
        name="moe_experts",
    )(blk_e, nact, xs, w1, b1, w2, b2)


def _block_diag(w):
    h, d, _ = w.shape
    eye = jnp.eye(h, dtype=w.dtype)
    return (eye[:, None, :, None] * w[:, :, None, :]).reshape(h * d, h * d)


def kernel(x_prompt, x_sample, state_lru_h, cache_lru_conv, cache_cf_conv, cache_mem_k, cache_mem_v,
           mem_prompt, g_mix, w_in, w_lru_conv, b_lru_conv, w_lru_a, b_lru_a, w_lru_x, b_lru_x,
           lru_lambda, w_cf_conv, b_cf_conv, g_cf_ln, b_cf_ln, w_out, g_xa, g_mem, w_q, w_k, w_v, w_o,
           g_moe, w_router, b_router, w_e1, b_e1, w_e2, b_e2, g_final):
    depth = g_mix.shape[0]
    assert depth == 1, "single-layer stack"
    batch, seq, _ = x_prompt.shape
    n_s = x_sample.shape[0]
    assert x_sample.shape[1] == 1
    n_p = batch * seq
    l = 0

    mix_w = (
        g_mix[l][None], w_in[l].astype(BF16), w_lru_conv[l], b_lru_conv[l][None],
        jnp.concatenate([_block_diag(w_lru_a[l]), _block_diag(w_lru_x[l])], axis=1).astype(BF16),
        jnp.concatenate([b_lru_a[l], b_lru_x[l]])[None], lru_lambda[l][None],
        w_cf_conv[l], b_cf_conv[l][None], g_cf_ln[l][None], b_cf_ln[l][None], w_out[l].astype(BF16),
    )
    router_w = (g_moe[l][None], w_router[l].T.astype(BF16), b_router[l][:, None])
    wq, wo = w_q[l].astype(BF16), w_o[l].astype(BF16)
    gxa = g_xa[l][None]

    xp = x_prompt.reshape(n_p, D_MODEL)
    xs_ = x_sample.reshape(n_s, D_MODEL)
    x1p, hp, lcp, ccp = _mixer_prompt(xp, mix_w, batch, seq)
    x1s, hs, zx_s, glu_s = _mixer_sample(xs_, mix_w, state_lru_h[l], cache_lru_conv[l], cache_cf_conv[l])

    mk, mv = _memkv(mem_prompt.reshape(batch * N_MEM, D_MODEL), g_mem[l][None],
                    w_k[l].astype(BF16), w_v[l].astype(BF16))
    x2p, xnp_, top_p, gates_p = _attn_prompt(x1p, mk, mv, gxa, wq, wo, router_w, batch, seq)
    x2s, xns, top_s, gates_s = _attn_sample(x1s, cache_mem_k[l], cache_mem_v[l], gxa, wq, wo, router_w)

    dest_p, dest_s, blk_e, nact, cnt, pstart = _slots(top_p, top_s)
    nb = _moe_num_blocks((n_p + n_s) * TOP_K)
    nact = nact.reshape(1)
    xs = _dispatch(cnt.reshape(N_EXPERTS), pstart.reshape(N_EXPERTS), nact, dest_p, dest_s, xnp_, xns,
                   nb * MOE_BLOCK)
    ys = _experts(blk_e.reshape(nb), nact, xs, w_e1[l], b_e1[l][:, None, :], w_e2[l], b_e2[l][:, None, :])
    gfin = g_final[None]
    y_p = _combine(dest_p, gates_p.T, x2p, gfin, ys)
    y_s = _combine(dest_s, gates_s.T, x2s, gfin, ys)

    lru_conv_s = jnp.concatenate([cache_lru_conv[l][:, 1:], zx_s[:, None]], axis=1)
    cf_conv_s = jnp.concatenate([cache_cf_conv[l][:, 1:], glu_s[:, None]], axis=1)
    return (
        y_p.reshape(batch, seq, D_MODEL),
        y_s.reshape(n_s, 1, D_MODEL),
        hp.reshape(depth, batch, LRU_WIDTH),
        lcp[None],
        ccp[None],
        mk.reshape(depth, batch, N_MEM, XA_HEADS, XA_HEAD_DIM),
        mv.reshape(depth, batch, N_MEM, XA_HEADS, XA_HEAD_DIM),
        hs[None],
        lru_conv_s[None],
        cf_conv_s[None],
    )
```

```python
import functools

import jax
import jax.numpy as jnp
from jax import lax
from jax.experimental import pallas as pl
from jax.experimental.pallas import tpu as pltpu

F32 = jnp.float32
BF16 = jnp.bfloat16
I32 = jnp.int32

D_MODEL = 1024
LRU_WIDTH = 512
CF_WIDTH = 512
LRU_HEADS = 8
LRU_CONV = 4
LRU_C = 8.0
CF_CONV = 31
IN_WIDTH = 2 * LRU_WIDTH + 2 * CF_WIDTH
N_MEM = 256
XA_HEADS = 4
XA_HEAD_DIM = D_MODEL // XA_HEADS
N_EXPERTS = 32
TOP_K = 4
D_EXPERT = D_MODEL
SWIGLU_LIMIT = 7.0
SWIGLU_ALPHA = 1.702
EPS = 1e-6

LANES = 128
SUBLANES = 8
ROW_TILE = D_MODEL // LANES
assert ROW_TILE == SUBLANES
VMEM_LIMIT = 56 * 1024 * 1024

MIX_TILE = 256
LRU_TAIL_ROWS = 8
CF_TAIL_ROWS = 32
ATT_TILE = 256
KV_TILE = 256
SAMPLE_GROUP = 2
SAMPLE_MIX_GROUP = 32
MOE_BLOCK = 256
EXPERT_CHUNK = 1024
DMA_UNROLL = 8
DISPATCH_TILE = 512
COMBINE_TILE = 256


def _cparams(sem):
    return pltpu.CompilerParams(dimension_semantics=sem, vmem_limit_bytes=VMEM_LIMIT)


def _full(shape):
    n = len(shape)
    return pl.BlockSpec(shape, lambda *_: (0,) * n)


def _rms(x, g):
    return x * lax.rsqrt(jnp.mean(x * x, axis=-1, keepdims=True) + EPS) * g


def _bdot(a, b):
    return jnp.dot(a.astype(BF16), b, preferred_element_type=F32)


def _store_row_tiles(ref, x):
    m = x.shape[0]
    for s in range(ROW_TILE):
        ref[pl.ds(s, m, stride=ROW_TILE), :] = x[:, s * LANES:(s + 1) * LANES]


def _load_row_tiles(ref, m):
    return jnp.concatenate([ref[pl.ds(s, m, stride=ROW_TILE), :] for s in range(ROW_TILE)], axis=-1)


def _row_tile_copy(src_ref, src_row, dst_ref, dst_row, sem, n_rows=1):
    src = src_ref.at[pl.ds(pl.multiple_of(src_row * ROW_TILE, ROW_TILE), n_rows * ROW_TILE)]
    dst = dst_ref.at[pl.ds(pl.multiple_of(dst_row * ROW_TILE, ROW_TILE), n_rows * ROW_TILE)]
    return pltpu.make_async_copy(src, dst, sem)


def _neg_expm1(y):
    u = jnp.exp(y)
    d = 1.0 - u
    return jnp.where(d == 0.0, -y, jnp.where(u == 0.0, 1.0, d * y / jnp.log(u)))


def _lru_coeffs(xc, wg_ref, bg_ref, lam_ref):
    gl = _bdot(xc, wg_ref[...]) + bg_ref[...]
    r = jax.nn.sigmoid(gl[:, :LRU_WIDTH])
    i = jax.nn.sigmoid(gl[:, LRU_WIDTH:])
    log_a = -LRU_C * r * jax.nn.softplus(-lam_ref[...])
    a = jnp.exp(log_a)
    u = jnp.sqrt(_neg_expm1(2.0 * log_a)) * (i * xc)
    return a, u


def _cf_post(gc, gln_ref, bln_ref):
    mu = jnp.mean(gc, axis=-1, keepdims=True)
    c = gc - mu
    y = c * lax.rsqrt(jnp.mean(c * c, axis=-1, keepdims=True) + EPS)
    return jax.nn.silu(y * gln_ref[...] + bln_ref[...])


def _mixer_out(x, y_lru, y_cf, wout_ref):
    y = _bdot(y_lru, wout_ref[:LRU_WIDTH, :]) + _bdot(y_cf, wout_ref[LRU_WIDTH:, :])
    return x + y


def _mixer_prompt_kernel(x_ref, gmix_ref, win_ref, wlc_ref, blc_ref, wg_ref, bg_ref, lam_ref,
                         wcc_ref, bcc_ref, gln_ref, bln_ref, wout_ref,
                         x1_ref, h_ref, ltail_ref, ctail_ref,
                         zx_ext, glu_ext, h_carry):
    tt = MIX_TILE
    j = pl.program_id(1)

    @pl.when(j == 0)
    def _():
        zx_ext[0:LRU_TAIL_ROWS, :] = jnp.zeros((LRU_TAIL_ROWS, LRU_WIDTH), F32)
        glu_ext[0:CF_TAIL_ROWS, :] = jnp.zeros((CF_TAIL_ROWS, CF_WIDTH), F32)
        h_carry[...] = jnp.zeros((1, LRU_WIDTH), F32)

    x = x_ref[...]
    z = _bdot(_rms(x, gmix_ref[...]), win_ref[...])
    zx = z[:, :LRU_WIDTH]
    zg = z[:, LRU_WIDTH:2 * LRU_WIDTH]
    za = z[:, 2 * LRU_WIDTH:2 * LRU_WIDTH + CF_WIDTH]
    zb = z[:, 2 * LRU_WIDTH + CF_WIDTH:]

    zx_ext[LRU_TAIL_ROWS:LRU_TAIL_ROWS + tt, :] = zx
    off = LRU_TAIL_ROWS - (LRU_CONV - 1)
    xc = blc_ref[...]
    for k in range(LRU_CONV):
        xc = xc + wlc_ref[k:k + 1, :] * zx_ext[off + k:off + k + tt, :]
    a, u = _lru_coeffs(xc, wg_ref, bg_ref, lam_ref)
    sub = lax.broadcasted_iota(I32, (tt, LRU_WIDTH), 0) & (SUBLANES - 1)
    d = 1
    while d < SUBLANES:
        keep = sub >= d
        a_sh = jnp.where(keep, pltpu.roll(a, d, 0), 1.0)
        u_sh = jnp.where(keep, pltpu.roll(u, d, 0), 0.0)
        u = u + a * u_sh
        a = a * a_sh
        d *= 2
    h_last = h_carry[...]
    groups = []
    for r in range(0, tt, SUBLANES):
        hg = a[r:r + SUBLANES, :] * h_last + u[r:r + SUBLANES, :]
        groups.append(hg)
        h_last = hg[SUBLANES - 1:SUBLANES, :]
    hs = jnp.concatenate(groups, axis=0)
    h_carry[...] = h_last
    y_lru = hs * jax.nn.gelu(zg)

    glu = za * jax.nn.sigmoid(zb)
    glu_ext[CF_TAIL_ROWS:CF_TAIL_ROWS + tt, :] = glu
    off = CF_TAIL_ROWS - (CF_CONV - 1)
    gc = bcc_ref[...]
    for s in range(SUBLANES):
        rows = tt if s == 0 else tt + SUBLANES
        q = None
        for k in range(CF_CONV):
            if (off + k) % SUBLANES == s:
                base = off + k - s
                term = wcc_ref[k:k + 1, :] * glu_ext[base:base + rows, :]
                q = term if q is None else q + term
        gc = gc + (q if s == 0 else q[s:s + tt, :])
    y_cf = _cf_post(gc, gln_ref, bln_ref)

    x1_ref[...] = _mixer_out(x, y_lru, y_cf, wout_ref)

    h_ref[0] = h_last
    ltail_ref[0] = zx_ext[LRU_TAIL_ROWS + tt - (LRU_CONV - 1):LRU_TAIL_ROWS + tt, :]
    ctail_ref[0] = glu_ext[CF_TAIL_ROWS + tt - (CF_CONV - 1):CF_TAIL_ROWS + tt, :]
    zx_ext[0:LRU_TAIL_ROWS, :] = zx_ext[tt:tt + LRU_TAIL_ROWS, :]
    glu_ext[0:CF_TAIL_ROWS, :] = glu_ext[tt:tt + CF_TAIL_ROWS, :]


def _mixer_weight_specs():
    return [
        _full((1, D_MODEL)),
        _full((D_MODEL, IN_WIDTH)),
        _full((LRU_CONV, LRU_WIDTH)), _full((1, LRU_WIDTH)),
        _full((LRU_WIDTH, 2 * LRU_WIDTH)), _full((1, 2 * LRU_WIDTH)),
        _full((1, LRU_WIDTH)),
        _full((CF_CONV, CF_WIDTH)), _full((1, CF_WIDTH)),
        _full((1, CF_WIDTH)), _full((1, CF_WIDTH)),
        _full((D_MODEL, D_MODEL)),
    ]


def _mixer_prompt(x, mix_w, batch, seq):
    tt = MIX_TILE
    nt = seq // tt
    return pl.pallas_call(
        _mixer_prompt_kernel,
        grid=(batch, nt),
        in_specs=[pl.BlockSpec((tt, D_MODEL), lambda b, j: (b * nt + j, 0))] + _mixer_weight_specs(),
        out_specs=[
            pl.BlockSpec((tt, D_MODEL), lambda b, j: (b * nt + j, 0)),
            pl.BlockSpec((1, 1, LRU_WIDTH), lambda b, j: (b, 0, 0)),
            pl.BlockSpec((1, LRU_CONV - 1, LRU_WIDTH), lambda b, j: (b, 0, 0)),
            pl.BlockSpec((1, CF_CONV - 1, CF_WIDTH), lambda b, j: (b, 0, 0)),
        ],
        out_shape=[
            jax.ShapeDtypeStruct((batch * seq, D_MODEL), F32),
            jax.ShapeDtypeStruct((batch, 1, LRU_WIDTH), F32),
            jax.ShapeDtypeStruct((batch, LRU_CONV - 1, LRU_WIDTH), F32),
            jax.ShapeDtypeStruct((batch, CF_CONV - 1, CF_WIDTH), F32),
        ],
        scratch_shapes=[
            pltpu.VMEM((LRU_TAIL_ROWS + tt, LRU_WIDTH), F32),
            pltpu.VMEM((CF_TAIL_ROWS + tt, CF_WIDTH), F32),
            pltpu.VMEM((1, LRU_WIDTH), F32),
        ],
        compiler_params=_cparams(("arbitrary", "arbitrary")),
        name="mixer_prompt",
    )(x, *mix_w)


def _mixer_sample_kernel(x_ref, gmix_ref, win_ref, wlc_ref, blc_ref, wg_ref, bg_ref, lam_ref,
                         wcc_ref, bcc_ref, gln_ref, bln_ref, wout_ref,
                         h0_ref, lbuf_ref, cbuf_ref,
                         x1_ref, h_ref, zx_ref, glu_ref):
    x = x_ref[...]
    z = _bdot(_rms(x, gmix_ref[...]), win_ref[...])
    zx = z[:, :LRU_WIDTH]
    zg = z[:, LRU_WIDTH:2 * LRU_WIDTH]
    za = z[:, 2 * LRU_WIDTH:2 * LRU_WIDTH + CF_WIDTH]
    zb = z[:, 2 * LRU_WIDTH + CF_WIDTH:]

    xc = blc_ref[...] + wlc_ref[LRU_CONV - 1:LRU_CONV, :] * zx
    xc = xc + jnp.sum(lbuf_ref[...] * wlc_ref[0:LRU_CONV - 1, :][None], axis=1)
    a, u = _lru_coeffs(xc, wg_ref, bg_ref, lam_ref)
    h = a * h0_ref[...] + u
    y_lru = h * jax.nn.gelu(zg)

    glu = za * jax.nn.sigmoid(zb)
    gc = bcc_ref[...] + wcc_ref[CF_CONV - 1:CF_CONV, :] * glu
    gc = gc + jnp.sum(cbuf_ref[...] * wcc_ref[0:CF_CONV - 1, :][None], axis=1)
    y_cf = _cf_post(gc, gln_ref, bln_ref)

    x1_ref[...] = _mixer_out(x, y_lru, y_cf, wout_ref)
    h_ref[...] = h
    zx_ref[...] = zx
    glu_ref[...] = glu


def _mixer_sample(x, mix_w, h0, lbuf, cbuf):
    n = x.shape[0]
    g = SAMPLE_MIX_GROUP
    row = lambda w: pl.BlockSpec((g, w), lambda i: (i, 0))
    return pl.pallas_call(
        _mixer_sample_kernel,
        grid=(n // g,),
        in_specs=[row(D_MODEL)] + _mixer_weight_specs() + [
            row(LRU_WIDTH),
            pl.BlockSpec((g, LRU_CONV - 1, LRU_WIDTH), lambda i: (i, 0, 0)),
            pl.BlockSpec((g, CF_CONV - 1, CF_WIDTH), lambda i: (i, 0, 0)),
        ],
        out_specs=[row(D_MODEL), row(LRU_WIDTH), row(LRU_WIDTH), row(CF_WIDTH)],
        out_shape=[
            jax.ShapeDtypeStruct((n, D_MODEL), F32),
            jax.ShapeDtypeStruct((n, LRU_WIDTH), F32),
            jax.ShapeDtypeStruct((n, LRU_WIDTH), F32),
            jax.ShapeDtypeStruct((n, CF_WIDTH), F32),
        ],
        compiler_params=_cparams(("arbitrary",)),
        name="mixer_sample",
    )(x, *mix_w, h0, lbuf, cbuf)


def _memkv_kernel(mem_ref, gmem_ref, wk_ref, wv_ref, k_ref, v_ref):
    mn = _rms(mem_ref[...], gmem_ref[...]).astype(BF16)
    k_ref[...] = jnp.dot(mn, wk_ref[...], preferred_element_type=F32)
    v_ref[...] = jnp.dot(mn, wv_ref[...], preferred_element_type=F32)


def _memkv(mem, g_mem, wk, wv):
    n = mem.shape[0]
    t = KV_TILE
    row = pl.BlockSpec((t, D_MODEL), lambda i: (i, 0))
    return pl.pallas_call(
        _memkv_kernel,
        grid=(n // t,),
        in_specs=[row, _full((1, D_MODEL)), _full((D_MODEL, D_MODEL)), _full((D_MODEL, D_MODEL))],
        out_specs=[row, row],
        out_shape=[jax.ShapeDtypeStruct((n, D_MODEL), F32)] * 2,
        compiler_params=_cparams(("arbitrary",)),
        name="memory_kv",
    )(mem, g_mem, wk, wv)


def _router(x2, gmoe_ref, wrt_ref, brt_ref, xn_ref, tope_ref, gates_ref):
    xn = _rms(x2, gmoe_ref[...])
    _store_row_tiles(xn_ref, xn)
    logits = lax.dot_general(wrt_ref[...], xn.astype(BF16), (((1,), (1,)), ((), ())),
                             preferred_element_type=F32) + brt_ref[...]
    e_iota = lax.broadcasted_iota(I32, logits.shape, 0)
    work = logits
    vals, idxs = [], []
    for _ in range(TOP_K):
        m = jnp.max(work, axis=0, keepdims=True)
        idx = jnp.min(jnp.where(work == m, e_iota, N_EXPERTS), axis=0, keepdims=True)
        vals.append(m)
        idxs.append(idx)
        work = jnp.where(e_iota == idx, -jnp.inf, work)
    ex = [jnp.exp(v - vals[0]) for v in vals]
    den = ex[0] + ex[1] + ex[2] + ex[3]
    tope_ref[...] = jnp.concatenate(idxs, axis=0)
    gates_ref[...] = jnp.concatenate([e / den for e in ex], axis=0)


def _router_specs():
    return [_full((1, D_MODEL)), _full((N_EXPERTS, D_MODEL)), _full((N_EXPERTS, 1))]


def _router_out(n, tile, index):
    specs = [
        pl.BlockSpec((tile, D_MODEL), lambda *a: (index(*a), 0)),
        pl.BlockSpec((tile * ROW_TILE, LANES), lambda *a: (index(*a), 0)),
        pl.BlockSpec((TOP_K, tile), lambda *a: (0, index(*a))),
        pl.BlockSpec((TOP_K, tile), lambda *a: (0, index(*a))),
    ]
    shapes = [
        jax.ShapeDtypeStruct((n, D_MODEL), F32),
        jax.ShapeDtypeStruct((n * ROW_TILE, LANES), F32),
        jax.ShapeDtypeStruct((TOP_K, n), I32),
        jax.ShapeDtypeStruct((TOP_K, n), F32),
    ]
    return specs, shapes


def _attn_prompt_kernel(x1_ref, k_ref, v_ref, gxa_ref, wq_ref, wo_ref, gmoe_ref, wrt_ref, brt_ref,
                        x2_ref, xn_ref, tope_ref, gates_ref):
    x1 = x1_ref[...]
    q = (_bdot(_rms(x1, gxa_ref[...]), wq_ref[...]) * (XA_HEAD_DIM ** -0.5)).astype(BF16)
    outs = []
    for h in range(XA_HEADS):
        sl = slice(h * XA_HEAD_DIM, (h + 1) * XA_HEAD_DIM)
        kh = k_ref[:, sl].astype(BF16)
        vh = v_ref[:, sl].astype(BF16)
        s = lax.dot_general(q[:, sl], kh, (((1,), (1,)), ((), ())), preferred_element_type=F32)
        p = jnp.exp(s - jnp.max(s, axis=-1, keepdims=True))
        p = p / jnp.sum(p, axis=-1, keepdims=True)
        outs.append(_bdot(p, vh))
    o = jnp.concatenate(outs, axis=-1)
    x2 = x1 + _bdot(o, wo_ref[...])
    x2_ref[...] = x2
    _router(x2, gmoe_ref, wrt_ref, brt_ref, xn_ref, tope_ref, gates_ref)


def _attn_prompt(x1, k, v, gxa, wq, wo, router_w, batch, seq):
    tq = ATT_TILE
    nt = seq // tq
    out_specs, out_shapes = _router_out(batch * seq, tq, lambda b, j: b * nt + j)
    kv = pl.BlockSpec((N_MEM, D_MODEL), lambda b, j: (b, 0))
    return pl.pallas_call(
        _attn_prompt_kernel,
        grid=(batch, nt),
        in_specs=[pl.BlockSpec((tq, D_MODEL), lambda b, j: (b * nt + j, 0)), kv, kv,
                  _full((1, D_MODEL)), _full((D_MODEL, D_MODEL)), _full((D_MODEL, D_MODEL))] + _router_specs(),
        out_specs=out_specs,
        out_shape=out_shapes,
        compiler_params=_cparams(("arbitrary", "arbitrary")),
        name="attn_prompt",
    )(x1, k, v, gxa, wq, wo, *router_w)


def _qproj_kernel(x1_ref, gxa_ref, wq_ref, q_ref):
    q_ref[...] = _bdot(_rms(x1_ref[...], gxa_ref[...]), wq_ref[...]) * (XA_HEAD_DIM ** -0.5)


def _attn_sample_core_kernel(q_ref, k_ref, v_ref, o_ref):
    q = q_ref[...]
    s = jnp.sum(k_ref[...] * q[:, None], axis=-1, keepdims=True)
    p = jnp.exp(s - jnp.max(s, axis=1, keepdims=True))
    p = p / jnp.sum(p, axis=1, keepdims=True)
    o_ref[...] = jnp.sum(p * v_ref[...], axis=1)


def _oproj_router_kernel(x1_ref, o_ref, wo_ref, gmoe_ref, wrt_ref, brt_ref,
                         x2_ref, xn_ref, tope_ref, gates_ref):
    x2 = x1_ref[...] + _bdot(o_ref[...], wo_ref[...])
    x2_ref[...] = x2
    _router(x2, gmoe_ref, wrt_ref, brt_ref, xn_ref, tope_ref, gates_ref)


def _attn_sample(x1, k, v, gxa, wq, wo, router_w):
    n = x1.shape[0]
    g = SAMPLE_GROUP
    q = pl.pallas_call(
        _qproj_kernel,
        grid=(1,),
        in_specs=[_full((n, D_MODEL)), _full((1, D_MODEL)), _full((D_MODEL, D_MODEL))],
        out_specs=_full((n, D_MODEL)),
        out_shape=jax.ShapeDtypeStruct((n, D_MODEL), F32),
        compiler_params=_cparams(("arbitrary",)),
        name="qproj_sample",
    )(x1, gxa, wq)
    kv = pl.BlockSpec((g, N_MEM, XA_HEADS, XA_HEAD_DIM), lambda i: (i, 0, 0, 0))
    row = pl.BlockSpec((g, XA_HEADS, XA_HEAD_DIM), lambda i: (i, 0, 0))
    o = pl.pallas_call(
        _attn_sample_core_kernel,
        grid=(n // g,),
        in_specs=[row, kv, kv],
        out_specs=row,
        out_shape=jax.ShapeDtypeStruct((n, XA_HEADS, XA_HEAD_DIM), F32),
        compiler_params=_cparams(("arbitrary",)),
        name="attn_sample_core",
    )(q.reshape(n, XA_HEADS, XA_HEAD_DIM), k, v).reshape(n, D_MODEL)
    out_specs, out_shapes = _router_out(n, n, lambda i: 0)
    return pl.pallas_call(
        _oproj_router_kernel,
        grid=(1,),
        in_specs=[_full((n, D_MODEL)), _full((n, D_MODEL)), _full((D_MODEL, D_MODEL))] + _router_specs(),
        out_specs=out_specs,
        out_shape=out_shapes,
        compiler_params=_cparams(("arbitrary",)),
        name="oproj_router_sample",
    )(x1, o, wo, *router_w)


def _moe_num_blocks(n_assign):
    return pl.cdiv(n_assign, MOE_BLOCK) + N_EXPERTS


def _slots_kernel(tp_ref, ts_ref, destp_ref, dests_ref, blke_ref, nact_ref, cnt_ref, pstart_ref,
                  rankp, ranks):
    c = LANES
    r_i = lax.broadcasted_iota(I32, (c, c), 0)
    c_i = lax.broadcasted_iota(I32, (c, c), 1)
    upper = (r_i < c_i).astype(BF16)
    e_iota = lax.broadcasted_iota(I32, (N_EXPERTS, c), 0)

    def make_body(top_ref, rank_ref):
        def body(ci, carry):
            lo = pl.multiple_of(ci * c, c)
            top = top_ref[:, pl.ds(lo, c)]
            hit = [e_iota == top[k:k + 1, :] for k in range(TOP_K)]
            cnt = sum(m.astype(F32) for m in hit)
            base = carry + jnp.dot(cnt.astype(BF16), upper, preferred_element_type=F32)
            rank_ref[:, pl.ds(lo, c)] = jnp.concatenate(
                [jnp.sum(jnp.where(m, base, 0.0), axis=0, keepdims=True) for m in hit], axis=0)
            return carry + jnp.sum(cnt, axis=1, keepdims=True)
        return body

    counts = jnp.zeros((N_EXPERTS, 1), F32)
    counts = lax.fori_loop(0, tp_ref.shape[1] // c, make_body(tp_ref, rankp), counts)
    counts = lax.fori_loop(0, ts_ref.shape[1] // c, make_body(ts_ref, ranks), counts)

    padded = jnp.floor((counts + (MOE_BLOCK - 1)) / MOE_BLOCK) * MOE_BLOCK
    er = lax.broadcasted_iota(I32, (N_EXPERTS, N_EXPERTS), 0)
    ec = lax.broadcasted_iota(I32, (N_EXPERTS, N_EXPERTS), 1)
    padded_row = jnp.sum(jnp.where(er == ec, padded, 0.0), axis=0, keepdims=True)
    cum = jnp.sum(jnp.where(ec <= er, padded_row, 0.0), axis=1, keepdims=True)
    pstart = cum - padded
    cnt_ref[...] = jnp.sum(jnp.where(er == ec, counts, 0.0), axis=0, keepdims=True).astype(I32)
    pstart_ref[...] = jnp.sum(jnp.where(er == ec, pstart, 0.0), axis=0, keepdims=True).astype(I32)

    def dest(top_ref, rank_ref, out_ref):
        n = top_ref.shape[1]
        ei = lax.broadcasted_iota(I32, (N_EXPERTS, n), 0)
        rows = []
        for k in range(TOP_K):
            start = jnp.sum(jnp.where(ei == top_ref[k:k + 1, :], pstart, 0.0), axis=0, keepdims=True)
            rows.append(start + rank_ref[k:k + 1, :])
        out_ref[...] = jnp.concatenate(rows, axis=0).astype(I32)

    dest(tp_ref, rankp, destp_ref)
    dest(ts_ref, ranks, dests_ref)

    nb = blke_ref.shape[1]
    blk_lo = lax.broadcasted_iota(I32, (N_EXPERTS, nb), 1).astype(F32) * MOE_BLOCK
    blk = jnp.sum((cum <= blk_lo).astype(F32), axis=0, keepdims=True)
    blke_ref[...] = jnp.minimum(blk, N_EXPERTS - 1).astype(I32)
    nact_ref[...] = (cum[N_EXPERTS - 1:N_EXPERTS, :] / MOE_BLOCK).astype(I32)


def _slots(top_p, top_s):
    n_p, n_s = top_p.shape[1], top_s.shape[1]
    nb = _moe_num_blocks((n_p + n_s) * TOP_K)
    return pl.pallas_call(
        _slots_kernel,
        grid=(1,),
        in_specs=[_full((TOP_K, n_p)), _full((TOP_K, n_s))],
        out_specs=[_full((TOP_K, n_p)), _full((TOP_K, n_s)), _full((1, nb)), _full((1, 1)),
                   _full((1, N_EXPERTS)), _full((1, N_EXPERTS))],
        out_shape=[
            jax.ShapeDtypeStruct((TOP_K, n_p), I32),
            jax.ShapeDtypeStruct((TOP_K, n_s), I32),
            jax.ShapeDtypeStruct((1, nb), I32),
            jax.ShapeDtypeStruct((1, 1), I32),
            jax.ShapeDtypeStruct((1, N_EXPERTS), I32),
            jax.ShapeDtypeStruct((1, N_EXPERTS), I32),
        ],
        scratch_shapes=[pltpu.VMEM((TOP_K, n_p), F32), pltpu.VMEM((TOP_K, n_s), F32)],
        compiler_params=_cparams(("arbitrary",)),
        name="moe_slots",
    )(top_p, top_s)


def _for_rows(n_rows, fn):
    def body(i, c):
        for u in range(DMA_UNROLL):
            fn(i * DMA_UNROLL + u)
        return c
    lax.fori_loop(0, n_rows // DMA_UNROLL, body, 0)


def _dispatch_kernel(cnt_ref, pstart_ref, nact_ref, destp_ref, dests_ref, xnp_ref, xns_ref, xs_ref,
                     zeros, sem, zsem):
    i = pl.program_id(0)
    last = pl.num_programs(0) - 1

    def scatter(x_ref, dest_ref):
        n = x_ref.shape[0] // ROW_TILE

        def one(t):
            for k in range(TOP_K):
                _row_tile_copy(x_ref, t, xs_ref, dest_ref[k, t], sem).start(priority=k % 2)

        _for_rows(n, one)
        _row_tile_copy(xs_ref, 0, xs_ref, 0, sem, n * TOP_K).wait()

    @pl.when(i < last)
    def _():
        scatter(xnp_ref, destp_ref)

    @pl.when(i == last)
    def _():
        scatter(xns_ref, dests_ref)
        zeros[...] = jnp.zeros(zeros.shape, F32)

        def pad_copies(e, act):
            c = cnt_ref[e]
            npad = (-c) & (MOE_BLOCK - 1)
            base = pstart_ref[e] + c
            for bit in range(MOE_BLOCK.bit_length() - 1):
                size = 1 << bit

                @pl.when((npad & size) != 0)
                def _():
                    act(_row_tile_copy(zeros, 0, xs_ref, base + (npad & (size - 1)), zsem, size))

        def tail_copies(act):
            def body(j, c):
                act(_row_tile_copy(zeros, 0, xs_ref, j * MOE_BLOCK, zsem, MOE_BLOCK))
                return c
            lax.fori_loop(nact_ref[0], xs_ref.shape[0] // (MOE_BLOCK * ROW_TILE), body, 0)

        for act in (lambda d: d.start(), lambda d: d.wait()):
            lax.fori_loop(0, N_EXPERTS, lambda e, c, act=act: (pad_copies(e, act), c)[1], 0)
            tail_copies(act)


def _dispatch(cnt, pstart, nact, dest_p, dest_s, xn_p, xn_s, n_rows):
    n_p, n_s = xn_p.shape[0] // ROW_TILE, xn_s.shape[0] // ROW_TILE
    tile = DISPATCH_TILE
    nt = n_p // tile
    smem = pl.BlockSpec(memory_space=pltpu.SMEM)
    return pl.pallas_call(
        _dispatch_kernel,
        grid=(nt + 1,),
        in_specs=[
            smem, smem, smem,
            pl.BlockSpec((TOP_K, tile), lambda i: (0, jnp.minimum(i, nt - 1)), memory_space=pltpu.SMEM),
            smem,
            pl.BlockSpec((tile * ROW_TILE, LANES), lambda i: (jnp.minimum(i, nt - 1), 0)),
            _full((n_s * ROW_TILE, LANES)),
        ],
        out_specs=pl.BlockSpec(memory_space=pl.ANY),
        out_shape=jax.ShapeDtypeStruct((n_rows * ROW_TILE, LANES), F32),
        scratch_shapes=[pltpu.VMEM((MOE_BLOCK * ROW_TILE, LANES), F32), pltpu.SemaphoreType.DMA,
                        pltpu.SemaphoreType.DMA],
        compiler_params=_cparams(("arbitrary",)),
        name="moe_dispatch",
    )(cnt, pstart, nact, dest_p, dest_s, xn_p, xn_s)


def _combine_kernel(dest_ref, dest_next_ref, gates_ref, x2_ref, gfin_ref, ys_ref, y_ref, bufs, sems):
    tile = x2_ref.shape[0]
    i = pl.program_id(0)
    slot = i % 2

    def gather(d_ref, b):
        def one(t):
            for k in range(TOP_K):
                _row_tile_copy(ys_ref, d_ref[k, t], bufs.at[b, k], t, sems.at[b]).start(priority=k % 2)
        _for_rows(tile, one)

    @pl.when(i == 0)
    def _():
        gather(dest_ref, 0)

    for b in range(2):
        @pl.when(slot == b)
        def _():
            pltpu.make_async_copy(bufs.at[b], bufs.at[b], sems.at[b]).wait()

            @pl.when(i + 1 < pl.num_programs(0))
            def _():
                gather(dest_next_ref, 1 - b)

    buf = bufs.at[slot]
    sumsq = jnp.zeros((tile, 1), F32)
    gate = [jnp.broadcast_to(gates_ref[:, k:k + 1], (tile, LANES)) for k in range(TOP_K)]
    for s in range(ROW_TILE):
        cols = slice(s * LANES, (s + 1) * LANES)
        acc = x2_ref[:, cols]
        for k in range(TOP_K):
            acc = acc + gate[k] * buf[k, pl.ds(s, tile, stride=ROW_TILE), :]
        y_ref[:, cols] = acc
        sumsq = sumsq + jnp.sum(acc * acc, axis=-1, keepdims=True)
    scale = lax.rsqrt(sumsq / D_MODEL + EPS)
    for s in range(ROW_TILE):
        cols = slice(s * LANES, (s + 1) * LANES)
        y_ref[:, cols] = y_ref[:, cols] * scale * gfin_ref[:, cols]


def _combine(dest, gates_t, x2, g_final, ys):
    n = x2.shape[0]
    tile = min(COMBINE_TILE, n)
    row = pl.BlockSpec((tile, D_MODEL), lambda i: (i, 0))
    nt = n // tile
    return pl.pallas_call(
        _combine_kernel,
        grid=(nt,),
        in_specs=[
            pl.BlockSpec((TOP_K, tile), lambda i: (0, i), memory_space=pltpu.SMEM),
            pl.BlockSpec((TOP_K, tile), lambda i: (0, jnp.minimum(i + 1, nt - 1)), memory_space=pltpu.SMEM),
            pl.BlockSpec((tile, TOP_K), lambda i: (i, 0)),
            row,
            _full((1, D_MODEL)),
            pl.BlockSpec(memory_space=pl.ANY),
        ],
        out_specs=row,
        out_shape=jax.ShapeDtypeStruct((n, D_MODEL), F32),
        scratch_shapes=[pltpu.VMEM((2, TOP_K, tile * ROW_TILE, LANES), F32), pltpu.SemaphoreType.DMA((2,))],
        compiler_params=_cparams(("arbitrary",)),
        name="moe_combine",
    )(dest, dest, gates_t, x2, g_final, ys)


def _experts_kernel(blke_ref, nact_ref, xs_ref, w1_ref, b1_ref, w2_ref, b2_ref, ys_ref, w1b, w2b):
    j = pl.program_id(0)
    active = j < nact_ref[0]
    new_expert = jnp.logical_or(j == 0, blke_ref[j] != blke_ref[jnp.maximum(j - 1, 0)])

    @pl.when(jnp.logical_and(active, new_expert))
    def _():
        w1b[...] = w1_ref[0].astype(BF16)
        w2b[...] = w2_ref[0].astype(BF16)

    @pl.when(active)
    def _():
        x = _load_row_tiles(xs_ref, MOE_BLOCK).astype(BF16)
        y = b2_ref[0]
        for c in range(0, D_EXPERT, EXPERT_CHUNK):
            g = jnp.dot(x, w1b[:, c:c + EXPERT_CHUNK], preferred_element_type=F32) + b1_ref[0, :, c:c + EXPERT_CHUNK]
            u = (jnp.dot(x, w1b[:, D_EXPERT + c:D_EXPERT + c + EXPERT_CHUNK], preferred_element_type=F32)
                 + b1_ref[0, :, D_EXPERT + c:D_EXPERT + c + EXPERT_CHUNK])
            g = jnp.minimum(g, SWIGLU_LIMIT)
            u = jnp.clip(u, -SWIGLU_LIMIT, SWIGLU_LIMIT)
            h = (u + 1.0) * (g * jax.nn.sigmoid(SWIGLU_ALPHA * g))
            y = y + _bdot(h, w2b[c:c + EXPERT_CHUNK, :])
        _store_row_tiles(ys_ref, y)

    @pl.when(jnp.logical_not(active))
    def _():
        ys_ref[...] = jnp.zeros(ys_ref.shape, F32)


def _experts(blk_e, nact, xs, w1, b1, w2, b2):
    nb = xs.shape[0] // (MOE_BLOCK * ROW_TILE)
    blk = lambda j, be, na: (jnp.minimum(j, na[0] - 1), 0)
    exp = lambda j, be, na: (be[jnp.minimum(j, na[0] - 1)], 0, 0)
    rows = (MOE_BLOCK * ROW_TILE, LANES)
    return pl.pallas_call(
        _experts_kernel,
        grid_spec=pltpu.PrefetchScalarGridSpec(
            num_scalar_prefetch=2,
            grid=(nb,),
            in_specs=[
                pl.BlockSpec(rows, blk),
                pl.BlockSpec((1, D_MODEL, 2 * D_EXPERT), exp),
                pl.BlockSpec((1, 1, 2 * D_EXPERT), exp),
                pl.BlockSpec((1, D_EXPERT, D_MODEL), exp),
                pl.BlockSpec((1, 1, D_MODEL), exp),
            ],
            out_specs=pl.BlockSpec(rows, lambda j, be, na: (j, 0)),
            scratch_shapes=[pltpu.VMEM((D_MODEL, 2 * D_EXPERT), BF16), pltpu.VMEM((D_EXPERT, D_MODEL), BF16)],
        ),
        out_shape=jax.ShapeDtypeStruct(xs.shape, F32),
        compiler_params=_cparams(("arbitrary",)),
        name="moe_experts",
    )(blk_e, nact, xs, w1, b1, w2, b2)


def _block_diag(w):
    h, d, _ = w.shape
    eye = jnp.eye(h, dtype=w.dtype)
    return (eye[:, None, :, None] * w[:, :, None, :]).reshape(h * d, h * d)


def kernel(x_prompt, x_sample, state_lru_h, cache_lru_conv, cache_cf_conv, cache_mem_k, cache_mem_v,
           mem_prompt, g_mix, w_in, w_lru_conv, b_lru_conv, w_lru_a, b_lru_a, w_lru_x, b_lru_x,
           lru_lambda, w_cf_conv, b_cf_conv, g_cf_ln, b_cf_ln, w_out, g_xa, g_mem, w_q, w_k, w_v, w_o,
           g_moe, w_router, b_router, w_e1, b_e1, w_e2, b_e2, g_final):
    depth = g_mix.shape[0]
    assert depth == 1, "single-layer stack"
    batch, seq, _ = x_prompt.shape
    n_s = x_sample.shape[0]
    assert x_sample.shape[1] == 1
    n_p = batch * seq
    l = 0

    mix_w = (
        g_mix[l][None], w_in[l].astype(BF16), w_lru_conv[l], b_lru_conv[l][None],
        jnp.concatenate([_block_diag(w_lru_a[l]), _block_diag(w_lru_x[l])], axis=1).astype(BF16),
        jnp.concatenate([b_lru_a[l], b_lru_x[l]])[None], lru_lambda[l][None],
        w_cf_conv[l], b_cf_conv[l][None], g_cf_ln[l][None], b_cf_ln[l][None], w_out[l].astype(BF16),
    )
    router_w = (g_moe[l][None], w_router[l].T.astype(BF16), b_router[l][:, None])
    wq, wo = w_q[l].astype(BF16), w_o[l].astype(BF16)
    gxa = g_xa[l][None]

    xp = x_prompt.reshape(n_p, D_MODEL)
    xs_ = x_sample.reshape(n_s, D_MODEL)
    x1p, hp, lcp, ccp = _mixer_prompt(xp, mix_w, batch, seq)
    x1s, hs, zx_s, glu_s = _mixer_sample(xs_, mix_w, state_lru_h[l], cache_lru_conv[l], cache_cf_conv[l])

    mk, mv = _memkv(mem_prompt.reshape(batch * N_MEM, D_MODEL), g_mem[l][None],
                    w_k[l].astype(BF16), w_v[l].astype(BF16))
    x2p, xnp_, top_p, gates_p = _attn_prompt(x1p, mk, mv, gxa, wq, wo, router_w, batch, seq)
    x2s, xns, top_s, gates_s = _attn_sample(x1s, cache_mem_k[l], cache_mem_v[l], gxa, wq, wo, router_w)

    dest_p, dest_s, blk_e, nact, cnt, pstart = _slots(top_p, top_s)
    nb = _moe_num_blocks((n_p + n_s) * TOP_K)
    nact = nact.reshape(1)
    xs = _dispatch(cnt.reshape(N_EXPERTS), pstart.reshape(N_EXPERTS), nact, dest_p, dest_s, xnp_, xns,
                   nb * MOE_BLOCK)
    ys = _experts(blk_e.reshape(nb), nact, xs, w_e1[l], b_e1[l][:, None, :], w_e2[l], b_e2[l][:, None, :])
    gfin = g_final[None]
    y_p = _combine(dest_p, gates_p.T, x2p, gfin, ys)
    y_s = _combine(dest_s, gates_s.T, x2s, gfin, ys)

    lru_conv_s = jnp.concatenate([cache_lru_conv[l][:, 1:], zx_s[:, None]], axis=1)
    cf_conv_s = jnp.concatenate([cache_cf_conv[l][:, 1:], glu_s[:, None]], axis=1)
    return (
        y_p.reshape(batch, seq, D_MODEL),
        y_s.reshape(n_s, 1, D_MODEL),
        hp.reshape(depth, batch, LRU_WIDTH),
        lcp[None],
        ccp[None],
        mk.reshape(depth, batch, N_MEM, XA_HEADS, XA_HEAD_DIM),
        mv.reshape(depth, batch, N_MEM, XA_HEADS, XA_HEAD_DIM),
        hs[None],
        lru_conv_s[None],
        cf_conv_s[None],
    )
```

```python
import functools

import jax
import jax.numpy as jnp
from jax import lax
from jax.experimental import pallas as pl
from jax.experimental.pallas import tpu as pltpu

F32 = jnp.float32
BF16 = jnp.bfloat16
I32 = jnp.int32

D_MODEL = 1024
LRU_WIDTH = 512
CF_WIDTH = 512
LRU_HEADS = 8
LRU_CONV = 4
LRU_C = 8.0
CF_CONV = 31
IN_WIDTH = 2 * LRU_WIDTH + 2 * CF_WIDTH
N_MEM = 256
XA_HEADS = 4
XA_HEAD_DIM = D_MODEL // XA_HEADS
N_EXPERTS = 32
TOP_K = 4
D_EXPERT = D_MODEL
SWIGLU_LIMIT = 7.0
SWIGLU_ALPHA = 1.702
EPS = 1e-6

LANES = 128
SUBLANES = 8
ROW_TILE = D_MODEL // LANES
assert ROW_TILE == SUBLANES
VMEM_LIMIT = 56 * 1024 * 1024

MIX_TILE = 256
LRU_TAIL_ROWS = 8
CF_TAIL_ROWS = 32
ATT_TILE = 512
KV_TILE = 256
SAMPLE_GROUP = 2
SAMPLE_MIX_GROUP = 32
MOE_BLOCK = 256
DMA_UNROLL = 8
DISPATCH_TILE = 512
COMBINE_TILE = 256


def _cparams(sem):
    return pltpu.CompilerParams(dimension_semantics=sem, vmem_limit_bytes=VMEM_LIMIT)


def _full(shape):
    n = len(shape)
    return pl.BlockSpec(shape, lambda *_: (0,) * n)


def _rms(x, g):
    return x * lax.rsqrt(jnp.mean(x * x, axis=-1, keepdims=True) + EPS) * g


def _bdot(a, b):
    return jnp.dot(a.astype(BF16), b, preferred_element_type=F32)


def _store_row_tiles(ref, x):
    m = x.shape[0]
    for s in range(ROW_TILE):
        ref[pl.ds(s, m, stride=ROW_TILE), :] = x[:, s * LANES:(s + 1) * LANES]


def _load_row_tiles(ref, m):
    return jnp.concatenate([ref[pl.ds(s, m, stride=ROW_TILE), :] for s in range(ROW_TILE)], axis=-1)


def _row_tile_copy(src_ref, src_row, dst_ref, dst_row, sem, n_rows=1):
    src = src_ref.at[pl.ds(pl.multiple_of(src_row * ROW_TILE, ROW_TILE), n_rows * ROW_TILE)]
    dst = dst_ref.at[pl.ds(pl.multiple_of(dst_row * ROW_TILE, ROW_TILE), n_rows * ROW_TILE)]
    return pltpu.make_async_copy(src, dst, sem)


def _neg_expm1(y):
    u = jnp.exp(y)
    d = 1.0 - u
    return jnp.where(d == 0.0, -y, jnp.where(u == 0.0, 1.0, d * y / jnp.log(u)))


def _lru_coeffs(xc, wg_ref, bg_ref, lam_ref):
    gl = _bdot(xc, wg_ref[...]) + bg_ref[...]
    r = jax.nn.sigmoid(gl[:, :LRU_WIDTH])
    i = jax.nn.sigmoid(gl[:, LRU_WIDTH:])
    log_a = -LRU_C * r * jax.nn.softplus(-lam_ref[...])
    a = jnp.exp(log_a)
    u = jnp.sqrt(_neg_expm1(2.0 * log_a)) * (i * xc)
    return a, u


def _cf_post(gc, gln_ref, bln_ref):
    mu = jnp.mean(gc, axis=-1, keepdims=True)
    c = gc - mu
    y = c * lax.rsqrt(jnp.mean(c * c, axis=-1, keepdims=True) + EPS)
    return jax.nn.silu(y * gln_ref[...] + bln_ref[...])


def _mixer_out(x, y_lru, y_cf, wout_ref):
    y = _bdot(y_lru, wout_ref[:LRU_WIDTH, :]) + _bdot(y_cf, wout_ref[LRU_WIDTH:, :])
    return x + y


def _mixer_prompt_kernel(x_ref, gmix_ref, win_ref, wlc_ref, blc_ref, wg_ref, bg_ref, lam_ref,
                         wcc_ref, bcc_ref, gln_ref, bln_ref, wout_ref,
                         x1_ref, h_ref, ltail_ref, ctail_ref,
                         zx_ext, glu_ext, h_carry):
    tt = MIX_TILE
    j = pl.program_id(1)

    @pl.when(j == 0)
    def _():
        zx_ext[0:LRU_TAIL_ROWS, :] = jnp.zeros((LRU_TAIL_ROWS, LRU_WIDTH), F32)
        glu_ext[0:CF_TAIL_ROWS, :] = jnp.zeros((CF_TAIL_ROWS, CF_WIDTH), F32)
        h_carry[...] = jnp.zeros((1, LRU_WIDTH), F32)

    x = x_ref[...]
    z = _bdot(_rms(x, gmix_ref[...]), win_ref[...])
    zx = z[:, :LRU_WIDTH]
    zg = z[:, LRU_WIDTH:2 * LRU_WIDTH]
    za = z[:, 2 * LRU_WIDTH:2 * LRU_WIDTH + CF_WIDTH]
    zb = z[:, 2 * LRU_WIDTH + CF_WIDTH:]

    zx_ext[LRU_TAIL_ROWS:LRU_TAIL_ROWS + tt, :] = zx
    off = LRU_TAIL_ROWS - (LRU_CONV - 1)
    xc = blc_ref[...]
    for k in range(LRU_CONV):
        xc = xc + wlc_ref[k:k + 1, :] * zx_ext[off + k:off + k + tt, :]
    a, u = _lru_coeffs(xc, wg_ref, bg_ref, lam_ref)
    sub = lax.broadcasted_iota(I32, (tt, LRU_WIDTH), 0) & (SUBLANES - 1)
    d = 1
    while d < SUBLANES:
        keep = sub >= d
        a_sh = jnp.where(keep, pltpu.roll(a, d, 0), 1.0)
        u_sh = jnp.where(keep, pltpu.roll(u, d, 0), 0.0)
        u = u + a * u_sh
        a = a * a_sh
        d *= 2
    h_last = h_carry[...]
    groups = []
    for r in range(0, tt, SUBLANES):
        hg = a[r:r + SUBLANES, :] * h_last + u[r:r + SUBLANES, :]
        groups.append(hg)
        h_last = hg[SUBLANES - 1:SUBLANES, :]
    hs = jnp.concatenate(groups, axis=0)
    h_carry[...] = h_last
    y_lru = hs * jax.nn.gelu(zg)

    glu = za * jax.nn.sigmoid(zb)
    glu_ext[CF_TAIL_ROWS:CF_TAIL_ROWS + tt, :] = glu
    off = CF_TAIL_ROWS - (CF_CONV - 1)
    gc = bcc_ref[...]
    for s in range(SUBLANES):
        rows = tt if s == 0 else tt + SUBLANES
        q = None
        for k in range(CF_CONV):
            if (off + k) % SUBLANES == s:
                base = off + k - s
                term = wcc_ref[k:k + 1, :] * glu_ext[base:base + rows, :]
                q = term if q is None else q + term
        gc = gc + (q if s == 0 else q[s:s + tt, :])
    y_cf = _cf_post(gc, gln_ref, bln_ref)

    x1_ref[...] = _mixer_out(x, y_lru, y_cf, wout_ref)

    h_ref[0] = h_last
    ltail_ref[0] = zx_ext[LRU_TAIL_ROWS + tt - (LRU_CONV - 1):LRU_TAIL_ROWS + tt, :]
    ctail_ref[0] = glu_ext[CF_TAIL_ROWS + tt - (CF_CONV - 1):CF_TAIL_ROWS + tt, :]
    zx_ext[0:LRU_TAIL_ROWS, :] = zx_ext[tt:tt + LRU_TAIL_ROWS, :]
    glu_ext[0:CF_TAIL_ROWS, :] = glu_ext[tt:tt + CF_TAIL_ROWS, :]


def _mixer_weight_specs():
    return [
        _full((1, D_MODEL)),
        _full((D_MODEL, IN_WIDTH)),
        _full((LRU_CONV, LRU_WIDTH)), _full((1, LRU_WIDTH)),
        _full((LRU_WIDTH, 2 * LRU_WIDTH)), _full((1, 2 * LRU_WIDTH)),
        _full((1, LRU_WIDTH)),
        _full((CF_CONV, CF_WIDTH)), _full((1, CF_WIDTH)),
        _full((1, CF_WIDTH)), _full((1, CF_WIDTH)),
        _full((D_MODEL, D_MODEL)),
    ]


def _mixer_prompt(x, mix_w, batch, seq):
    tt = MIX_TILE
    nt = seq // tt
    return pl.pallas_call(
        _mixer_prompt_kernel,
        grid=(batch, nt),
        in_specs=[pl.BlockSpec((tt, D_MODEL), lambda b, j: (b * nt + j, 0))] + _mixer_weight_specs(),
        out_specs=[
            pl.BlockSpec((tt, D_MODEL), lambda b, j: (b * nt + j, 0)),
            pl.BlockSpec((1, 1, LRU_WIDTH), lambda b, j: (b, 0, 0)),
            pl.BlockSpec((1, LRU_CONV - 1, LRU_WIDTH), lambda b, j: (b, 0, 0)),
            pl.BlockSpec((1, CF_CONV - 1, CF_WIDTH), lambda b, j: (b, 0, 0)),
        ],
        out_shape=[
            jax.ShapeDtypeStruct((batch * seq, D_MODEL), F32),
            jax.ShapeDtypeStruct((batch, 1, LRU_WIDTH), F32),
            jax.ShapeDtypeStruct((batch, LRU_CONV - 1, LRU_WIDTH), F32),
            jax.ShapeDtypeStruct((batch, CF_CONV - 1, CF_WIDTH), F32),
        ],
        scratch_shapes=[
            pltpu.VMEM((LRU_TAIL_ROWS + tt, LRU_WIDTH), F32),
            pltpu.VMEM((CF_TAIL_ROWS + tt, CF_WIDTH), F32),
            pltpu.VMEM((1, LRU_WIDTH), F32),
        ],
        compiler_params=_cparams(("arbitrary", "arbitrary")),
        name="mixer_prompt",
    )(x, *mix_w)


def _mixer_sample_kernel(x_ref, gmix_ref, win_ref, wlc_ref, blc_ref, wg_ref, bg_ref, lam_ref,
                         wcc_ref, bcc_ref, gln_ref, bln_ref, wout_ref,
                         h0_ref, lbuf_ref, cbuf_ref,
                         x1_ref, h_ref, zx_ref, glu_ref):
    x = x_ref[...]
    z = _bdot(_rms(x, gmix_ref[...]), win_ref[...])
    zx = z[:, :LRU_WIDTH]
    zg = z[:, LRU_WIDTH:2 * LRU_WIDTH]
    za = z[:, 2 * LRU_WIDTH:2 * LRU_WIDTH + CF_WIDTH]
    zb = z[:, 2 * LRU_WIDTH + CF_WIDTH:]

    xc = blc_ref[...] + wlc_ref[LRU_CONV - 1:LRU_CONV, :] * zx
    xc = xc + jnp.sum(lbuf_ref[...] * wlc_ref[0:LRU_CONV - 1, :][None], axis=1)
    a, u = _lru_coeffs(xc, wg_ref, bg_ref, lam_ref)
    h = a * h0_ref[...] + u
    y_lru = h * jax.nn.gelu(zg)

    glu = za * jax.nn.sigmoid(zb)
    gc = bcc_ref[...] + wcc_ref[CF_CONV - 1:CF_CONV, :] * glu
    gc = gc + jnp.sum(cbuf_ref[...] * wcc_ref[0:CF_CONV - 1, :][None], axis=1)
    y_cf = _cf_post(gc, gln_ref, bln_ref)

    x1_ref[...] = _mixer_out(x, y_lru, y_cf, wout_ref)
    h_ref[...] = h
    zx_ref[...] = zx
    glu_ref[...] = glu


def _mixer_sample(x, mix_w, h0, lbuf, cbuf):
    n = x.shape[0]
    g = SAMPLE_MIX_GROUP
    row = lambda w: pl.BlockSpec((g, w), lambda i: (i, 0))
    return pl.pallas_call(
        _mixer_sample_kernel,
        grid=(n // g,),
        in_specs=[row(D_MODEL)] + _mixer_weight_specs() + [
            row(LRU_WIDTH),
            pl.BlockSpec((g, LRU_CONV - 1, LRU_WIDTH), lambda i: (i, 0, 0)),
            pl.BlockSpec((g, CF_CONV - 1, CF_WIDTH), lambda i: (i, 0, 0)),
        ],
        out_specs=[row(D_MODEL), row(LRU_WIDTH), row(LRU_WIDTH), row(CF_WIDTH)],
        out_shape=[
            jax.ShapeDtypeStruct((n, D_MODEL), F32),
            jax.ShapeDtypeStruct((n, LRU_WIDTH), F32),
            jax.ShapeDtypeStruct((n, LRU_WIDTH), F32),
            jax.ShapeDtypeStruct((n, CF_WIDTH), F32),
        ],
        compiler_params=_cparams(("arbitrary",)),
        name="mixer_sample",
    )(x, *mix_w, h0, lbuf, cbuf)


def _memkv_kernel(mem_ref, gmem_ref, wk_ref, wv_ref, k_ref, v_ref, kb_ref, vb_ref):
    mn = _rms(mem_ref[...], gmem_ref[...]).astype(BF16)
    k = jnp.dot(mn, wk_ref[...], preferred_element_type=F32)
    v = jnp.dot(mn, wv_ref[...], preferred_element_type=F32)
    k_ref[...] = k
    v_ref[...] = v
    kb_ref[...] = k.astype(BF16)
    vb_ref[...] = v.astype(BF16)


def _memkv(mem, g_mem, wk, wv):
    n = mem.shape[0]
    t = KV_TILE
    row = pl.BlockSpec((t, D_MODEL), lambda i: (i, 0))
    return pl.pallas_call(
        _memkv_kernel,
        grid=(n // t,),
        in_specs=[row, _full((1, D_MODEL)), _full((D_MODEL, D_MODEL)), _full((D_MODEL, D_MODEL))],
        out_specs=[row, row, row, row],
        out_shape=[jax.ShapeDtypeStruct((n, D_MODEL), F32)] * 2 + [jax.ShapeDtypeStruct((n, D_MODEL), BF16)] * 2,
        compiler_params=_cparams(("arbitrary",)),
        name="memory_kv",
    )(mem, g_mem, wk, wv)


def _router(x2, gmoe_ref, wrt_ref, brt_ref, xn_ref, tope_ref, gates_ref):
    xn = _rms(x2, gmoe_ref[...])
    _store_row_tiles(xn_ref, xn)
    logits = lax.dot_general(wrt_ref[...], xn.astype(BF16), (((1,), (1,)), ((), ())),
                             preferred_element_type=F32) + brt_ref[...]
    e_iota = lax.broadcasted_iota(I32, logits.shape, 0)
    work = logits
    vals, idxs = [], []
    for _ in range(TOP_K):
        m = jnp.max(work, axis=0, keepdims=True)
        idx = jnp.min(jnp.where(work == m, e_iota, N_EXPERTS), axis=0, keepdims=True)
        vals.append(m)
        idxs.append(idx)
        work = jnp.where(e_iota == idx, -jnp.inf, work)
    ex = [jnp.exp(v - vals[0]) for v in vals]
    den = ex[0] + ex[1] + ex[2] + ex[3]
    tope_ref[...] = jnp.concatenate(idxs, axis=0)
    gates_ref[...] = jnp.concatenate([e / den for e in ex], axis=0)


def _router_specs():
    return [_full((1, D_MODEL)), _full((N_EXPERTS, D_MODEL)), _full((N_EXPERTS, 1))]


def _router_out(n, tile, index):
    specs = [
        pl.BlockSpec((tile, D_MODEL), lambda *a: (index(*a), 0)),
        pl.BlockSpec((tile * ROW_TILE, LANES), lambda *a: (index(*a), 0)),
        pl.BlockSpec((TOP_K, tile), lambda *a: (0, index(*a))),
        pl.BlockSpec((TOP_K, tile), lambda *a: (0, index(*a))),
    ]
    shapes = [
        jax.ShapeDtypeStruct((n, D_MODEL), F32),
        jax.ShapeDtypeStruct((n * ROW_TILE, LANES), F32),
        jax.ShapeDtypeStruct((TOP_K, n), I32),
        jax.ShapeDtypeStruct((TOP_K, n), F32),
    ]
    return specs, shapes


def _attn_prompt_kernel(x1_ref, k_ref, v_ref, gxa_ref, wq_ref, wo_ref, gmoe_ref, wrt_ref, brt_ref,
                        x2_ref, xn_ref, tope_ref, gates_ref):
    x1 = x1_ref[...]
    q = (_bdot(_rms(x1, gxa_ref[...]), wq_ref[...]) * (XA_HEAD_DIM ** -0.5)).astype(BF16)
    outs = []
    for h in range(XA_HEADS):
        sl = slice(h * XA_HEAD_DIM, (h + 1) * XA_HEAD_DIM)
        s = lax.dot_general(q[:, sl], k_ref[:, sl], (((1,), (1,)), ((), ())), preferred_element_type=F32)
        p = jnp.exp(s - jnp.max(s, axis=-1, keepdims=True))
        p = p / jnp.sum(p, axis=-1, keepdims=True)
        outs.append(_bdot(p, v_ref[:, sl]))
    o = jnp.concatenate(outs, axis=-1)
    x2 = x1 + _bdot(o, wo_ref[...])
    x2_ref[...] = x2
    _router(x2, gmoe_ref, wrt_ref, brt_ref, xn_ref, tope_ref, gates_ref)


def _attn_prompt(x1, k, v, gxa, wq, wo, router_w, batch, seq):
    tq = ATT_TILE
    nt = seq // tq
    out_specs, out_shapes = _router_out(batch * seq, tq, lambda b, j: b * nt + j)
    kv = pl.BlockSpec((N_MEM, D_MODEL), lambda b, j: (b, 0))
    return pl.pallas_call(
        _attn_prompt_kernel,
        grid=(batch, nt),
        in_specs=[pl.BlockSpec((tq, D_MODEL), lambda b, j: (b * nt + j, 0)), kv, kv,
                  _full((1, D_MODEL)), _full((D_MODEL, D_MODEL)), _full((D_MODEL, D_MODEL))] + _router_specs(),
        out_specs=out_specs,
        out_shape=out_shapes,
        compiler_params=_cparams(("arbitrary", "arbitrary")),
        name="attn_prompt",
    )(x1, k, v, gxa, wq, wo, *router_w)


def _qproj_kernel(x1_ref, gxa_ref, wq_ref, q_ref):
    q_ref[...] = _bdot(_rms(x1_ref[...], gxa_ref[...]), wq_ref[...]) * (XA_HEAD_DIM ** -0.5)


def _attn_sample_core_kernel(q_ref, k_ref, v_ref, o_ref):
    q = q_ref[...]
    s = jnp.sum(k_ref[...] * q[:, None], axis=-1, keepdims=True)
    p = jnp.exp(s - jnp.max(s, axis=1, keepdims=True))
    o_ref[...] = jnp.sum(p * v_ref[...], axis=1) / jnp.sum(p, axis=1)


def _oproj_router_kernel(x1_ref, o_ref, wo_ref, gmoe_ref, wrt_ref, brt_ref,
                         x2_ref, xn_ref, tope_ref, gates_ref):
    x2 = x1_ref[...] + _bdot(o_ref[...], wo_ref[...])
    x2_ref[...] = x2
    _router(x2, gmoe_ref, wrt_ref, brt_ref, xn_ref, tope_ref, gates_ref)


def _attn_sample(x1, k, v, gxa, wq, wo, router_w):
    n = x1.shape[0]
    g = SAMPLE_GROUP
    q = pl.pallas_call(
        _qproj_kernel,
        grid=(1,),
        in_specs=[_full((n, D_MODEL)), _full((1, D_MODEL)), _full((D_MODEL, D_MODEL))],
        out_specs=_full((n, D_MODEL)),
        out_shape=jax.ShapeDtypeStruct((n, D_MODEL), F32),
        compiler_params=_cparams(("arbitrary",)),
        name="qproj_sample",
    )(x1, gxa, wq)
    kv = pl.BlockSpec((g, N_MEM, XA_HEADS, XA_HEAD_DIM), lambda i: (i, 0, 0, 0))
    row = pl.BlockSpec((g, XA_HEADS, XA_HEAD_DIM), lambda i: (i, 0, 0))
    o = pl.pallas_call(
        _attn_sample_core_kernel,
        grid=(n // g,),
        in_specs=[row, kv, kv],
        out_specs=row,
        out_shape=jax.ShapeDtypeStruct((n, XA_HEADS, XA_HEAD_DIM), F32),
        compiler_params=_cparams(("arbitrary",)),
        name="attn_sample_core",
    )(q.reshape(n, XA_HEADS, XA_HEAD_DIM), k, v).reshape(n, D_MODEL)
    out_specs, out_shapes = _router_out(n, n, lambda i: 0)
    return pl.pallas_call(
        _oproj_router_kernel,
        grid=(1,),
        in_specs=[_full((n, D_MODEL)), _full((n, D_MODEL)), _full((D_MODEL, D_MODEL))] + _router_specs(),
        out_specs=out_specs,
        out_shape=out_shapes,
        compiler_params=_cparams(("arbitrary",)),
        name="oproj_router_sample",
    )(x1, o, wo, *router_w)


def _moe_num_blocks(n_assign):
    return pl.cdiv(n_assign, MOE_BLOCK) + N_EXPERTS


def _slots_kernel(tp_ref, ts_ref, destp_ref, dests_ref, blke_ref, nact_ref, cnt_ref, pstart_ref,
                  rankp, ranks):
    c = LANES
    r_i = lax.broadcasted_iota(I32, (c, c), 0)
    c_i = lax.broadcasted_iota(I32, (c, c), 1)
    upper = (r_i < c_i).astype(BF16)
    e_iota = lax.broadcasted_iota(I32, (N_EXPERTS, c), 0)

    def make_body(top_ref, rank_ref):
        def body(ci, carry):
            lo = pl.multiple_of(ci * c, c)
            top = top_ref[:, pl.ds(lo, c)]
            hit = [e_iota == top[k:k + 1, :] for k in range(TOP_K)]
            cnt = sum(m.astype(F32) for m in hit)
            base = carry + jnp.dot(cnt.astype(BF16), upper, preferred_element_type=F32)
            rank_ref[:, pl.ds(lo, c)] = jnp.concatenate(
                [jnp.sum(jnp.where(m, base, 0.0), axis=0, keepdims=True) for m in hit], axis=0)
            return carry + jnp.sum(cnt, axis=1, keepdims=True)
        return body

    counts = jnp.zeros((N_EXPERTS, 1), F32)
    counts = lax.fori_loop(0, tp_ref.shape[1] // c, make_body(tp_ref, rankp), counts)
    counts = lax.fori_loop(0, ts_ref.shape[1] // c, make_body(ts_ref, ranks), counts)

    padded = jnp.floor((counts + (MOE_BLOCK - 1)) / MOE_BLOCK) * MOE_BLOCK
    er = lax.broadcasted_iota(I32, (N_EXPERTS, N_EXPERTS), 0)
    ec = lax.broadcasted_iota(I32, (N_EXPERTS, N_EXPERTS), 1)
    padded_row = jnp.sum(jnp.where(er == ec, padded, 0.0), axis=0, keepdims=True)
    cum = jnp.sum(jnp.where(ec <= er, padded_row, 0.0), axis=1, keepdims=True)
    pstart = cum - padded
    cnt_ref[...] = jnp.sum(jnp.where(er == ec, counts, 0.0), axis=0, keepdims=True).astype(I32)
    pstart_ref[...] = jnp.sum(jnp.where(er == ec, pstart, 0.0), axis=0, keepdims=True).astype(I32)

    def dest(top_ref, rank_ref, out_ref):
        n = top_ref.shape[1]
        ei = lax.broadcasted_iota(I32, (N_EXPERTS, n), 0)
        rows = []
        for k in range(TOP_K):
            start = jnp.sum(jnp.where(ei == top_ref[k:k + 1, :], pstart, 0.0), axis=0, keepdims=True)
            rows.append(start + rank_ref[k:k + 1, :])
        out_ref[...] = jnp.concatenate(rows, axis=0).astype(I32)

    dest(tp_ref, rankp, destp_ref)
    dest(ts_ref, ranks, dests_ref)

    nb = blke_ref.shape[1]
    blk_lo = lax.broadcasted_iota(I32, (N_EXPERTS, nb), 1).astype(F32) * MOE_BLOCK
    blk = jnp.sum((cum <= blk_lo).astype(F32), axis=0, keepdims=True)
    blke_ref[...] = jnp.minimum(blk, N_EXPERTS - 1).astype(I32)
    nact_ref[...] = (cum[N_EXPERTS - 1:N_EXPERTS, :] / MOE_BLOCK).astype(I32)


def _slots(top_p, top_s):
    n_p, n_s = top_p.shape[1], top_s.shape[1]
    nb = _moe_num_blocks((n_p + n_s) * TOP_K)
    return pl.pallas_call(
        _slots_kernel,
        grid=(1,),
        in_specs=[_full((TOP_K, n_p)), _full((TOP_K, n_s))],
        out_specs=[_full((TOP_K, n_p)), _full((TOP_K, n_s)), _full((1, nb)), _full((1, 1)),
                   _full((1, N_EXPERTS)), _full((1, N_EXPERTS))],
        out_shape=[
            jax.ShapeDtypeStruct((TOP_K, n_p), I32),
            jax.ShapeDtypeStruct((TOP_K, n_s), I32),
            jax.ShapeDtypeStruct((1, nb), I32),
            jax.ShapeDtypeStruct((1, 1), I32),
            jax.ShapeDtypeStruct((1, N_EXPERTS), I32),
            jax.ShapeDtypeStruct((1, N_EXPERTS), I32),
        ],
        scratch_shapes=[pltpu.VMEM((TOP_K, n_p), F32), pltpu.VMEM((TOP_K, n_s), F32)],
        compiler_params=_cparams(("arbitrary",)),
        name="moe_slots",
    )(top_p, top_s)


def _for_rows(n_rows, fn):
    def body(i, c):
        for u in range(DMA_UNROLL):
            fn(i * DMA_UNROLL + u)
        return c
    lax.fori_loop(0, n_rows // DMA_UNROLL, body, 0)


def _dispatch_kernel(cnt_ref, pstart_ref, nact_ref, destp_ref, dests_ref, xnp_ref, xns_ref, xs_ref,
                     zeros, sem, zsem):
    i = pl.program_id(0)
    last = pl.num_programs(0) - 1

    def scatter(x_ref, dest_ref):
        n = x_ref.shape[0] // ROW_TILE

        def one(t):
            for k in range(TOP_K):
                _row_tile_copy(x_ref, t, xs_ref, dest_ref[k, t], sem).start(priority=k % 2)

        _for_rows(n, one)
        _row_tile_copy(xs_ref, 0, xs_ref, 0, sem, n * TOP_K).wait()

    @pl.when(i < last)
    def _():
        scatter(xnp_ref, destp_ref)

    @pl.when(i == last)
    def _():
        scatter(xns_ref, dests_ref)
        zeros[...] = jnp.zeros(zeros.shape, F32)

        def pad_copies(e, act):
            c = cnt_ref[e]
            npad = (-c) & (MOE_BLOCK - 1)
            base = pstart_ref[e] + c
            for bit in range(MOE_BLOCK.bit_length() - 1):
                size = 1 << bit

                @pl.when((npad & size) != 0)
                def _():
                    act(_row_tile_copy(zeros, 0, xs_ref, base + (npad & (size - 1)), zsem, size))

        def tail_copies(act):
            def body(j, c):
                act(_row_tile_copy(zeros, 0, xs_ref, j * MOE_BLOCK, zsem, MOE_BLOCK))
                return c
            lax.fori_loop(nact_ref[0], xs_ref.shape[0] // (MOE_BLOCK * ROW_TILE), body, 0)

        for act in (lambda d: d.start(), lambda d: d.wait()):
            lax.fori_loop(0, N_EXPERTS, lambda e, c, act=act: (pad_copies(e, act), c)[1], 0)
            tail_copies(act)


def _dispatch(cnt, pstart, nact, dest_p, dest_s, xn_p, xn_s, n_rows):
    n_p, n_s = xn_p.shape[0] // ROW_TILE, xn_s.shape[0] // ROW_TILE
    tile = DISPATCH_TILE
    nt = n_p // tile
    smem = pl.BlockSpec(memory_space=pltpu.SMEM)
    return pl.pallas_call(
        _dispatch_kernel,
        grid=(nt + 1,),
        in_specs=[
            smem, smem, smem,
            pl.BlockSpec((TOP_K, tile), lambda i: (0, jnp.minimum(i, nt - 1)), memory_space=pltpu.SMEM),
            smem,
            pl.BlockSpec((tile * ROW_TILE, LANES), lambda i: (jnp.minimum(i, nt - 1), 0)),
            _full((n_s * ROW_TILE, LANES)),
        ],
        out_specs=pl.BlockSpec(memory_space=pl.ANY),
        out_shape=jax.ShapeDtypeStruct((n_rows * ROW_TILE, LANES), F32),
        scratch_shapes=[pltpu.VMEM((MOE_BLOCK * ROW_TILE, LANES), F32), pltpu.SemaphoreType.DMA,
                        pltpu.SemaphoreType.DMA],
        compiler_params=_cparams(("arbitrary",)),
        name="moe_dispatch",
    )(cnt, pstart, nact, dest_p, dest_s, xn_p, xn_s)


def _combine_kernel(dest_ref, dest_next_ref, gates_ref, x2_ref, gfin_ref, ys_ref, y_ref, bufs, sems):
    tile = x2_ref.shape[0]
    i = pl.program_id(0)
    slot = i % 2

    def gather(d_ref, b):
        def one(t):
            for k in range(TOP_K):
                _row_tile_copy(ys_ref, d_ref[k, t], bufs.at[b, k], t, sems.at[b]).start(priority=k % 2)
        _for_rows(tile, one)

    @pl.when(i == 0)
    def _():
        gather(dest_ref, 0)

    for b in range(2):
        @pl.when(slot == b)
        def _():
            pltpu.make_async_copy(bufs.at[b], bufs.at[b], sems.at[b]).wait()

            @pl.when(i + 1 < pl.num_programs(0))
            def _():
                gather(dest_next_ref, 1 - b)

    buf = bufs.at[slot]
    sumsq = jnp.zeros((tile, 1), F32)
    gate = [jnp.broadcast_to(gates_ref[:, k:k + 1], (tile, LANES)) for k in range(TOP_K)]
    for s in range(ROW_TILE):
        cols = slice(s * LANES, (s + 1) * LANES)
        acc = x2_ref[:, cols]
        for k in range(TOP_K):
            acc = acc + gate[k] * buf[k, pl.ds(s, tile, stride=ROW_TILE), :]
        y_ref[:, cols] = acc
        sumsq = sumsq + jnp.sum(acc * acc, axis=-1, keepdims=True)
    scale = lax.rsqrt(sumsq / D_MODEL + EPS)
    for s in range(ROW_TILE):
        cols = slice(s * LANES, (s + 1) * LANES)
        y_ref[:, cols] = y_ref[:, cols] * scale * gfin_ref[:, cols]


def _combine(dest, gates_t, x2, g_final, ys):
    n = x2.shape[0]
    tile = min(COMBINE_TILE, n)
    row = pl.BlockSpec((tile, D_MODEL), lambda i: (i, 0))
    nt = n // tile
    return pl.pallas_call(
        _combine_kernel,
        grid=(nt,),
        in_specs=[
            pl.BlockSpec((TOP_K, tile), lambda i: (0, i), memory_space=pltpu.SMEM),
            pl.BlockSpec((TOP_K, tile), lambda i: (0, jnp.minimum(i + 1, nt - 1)), memory_space=pltpu.SMEM),
            pl.BlockSpec((tile, TOP_K), lambda i: (i, 0)),
            row,
            _full((1, D_MODEL)),
            pl.BlockSpec(memory_space=pl.ANY),
        ],
        out_specs=row,
        out_shape=jax.ShapeDtypeStruct((n, D_MODEL), F32),
        scratch_shapes=[pltpu.VMEM((2, TOP_K, tile * ROW_TILE, LANES), F32), pltpu.SemaphoreType.DMA((2,))],
        compiler_params=_cparams(("arbitrary",)),
        name="moe_combine",
    )(dest, dest, gates_t, x2, g_final, ys)


def _experts_kernel(blke_ref, nact_ref, xs_ref, w1_hbm, b1_ref, w2_hbm, b2_ref, ys_ref,
                    w1f, w2f, w1b, w2b, wsem, started):
    j = pl.program_id(0)
    nact = nact_ref[0]
    active = j < nact
    e = blke_ref[j]
    new_expert = jnp.logical_or(j == 0, e != blke_ref[jnp.maximum(j - 1, 0)])

    def fetch(expert, slot):
        return (pltpu.make_async_copy(w1_hbm.at[expert], w1f.at[slot], wsem.at[slot]),
                pltpu.make_async_copy(w2_hbm.at[expert], w2f.at[slot], wsem.at[slot]))

    @pl.when(j == 0)
    def _():
        started[0] = 0
        for d in fetch(e, 0):
            d.start()

    @pl.when(jnp.logical_and(active, new_expert))
    def _():
        n = started[0]
        last = blke_ref.shape[0] - 1
        jn = lax.while_loop(lambda jj: jnp.logical_and(jj < nact, blke_ref[jnp.minimum(jj, last)] == e),
                            lambda jj: jj + 1, j + 1)
        for b in range(2):
            @pl.when(n % 2 == b)
            def _():
                for d in fetch(e, b):
                    d.wait()

                @pl.when(jn < nact)
                def _():
                    for d in fetch(blke_ref[jnp.minimum(jn, last)], 1 - b):
                        d.start()

                w1b[...] = w1f[b].astype(BF16)
                w2b[...] = w2f[b].astype(BF16)
        started[0] = n + 1

    @pl.when(active)
    def _():
        x = _load_row_tiles(xs_ref, MOE_BLOCK).astype(BF16)
        gu = jnp.dot(x, w1b[...], preferred_element_type=F32) + b1_ref[0]
        g = jnp.minimum(gu[:, :D_EXPERT], SWIGLU_LIMIT)
        u = jnp.clip(gu[:, D_EXPERT:], -SWIGLU_LIMIT, SWIGLU_LIMIT)
        h = (u + 1.0) * (g * jax.nn.sigmoid(SWIGLU_ALPHA * g))
        _store_row_tiles(ys_ref, _bdot(h, w2b[...]) + b2_ref[0])

    @pl.when(jnp.logical_not(active))
    def _():
        ys_ref[...] = jnp.zeros(ys_ref.shape, F32)


def _experts(blk_e, nact, xs, w1, b1, w2, b2):
    nb = xs.shape[0] // (MOE_BLOCK * ROW_TILE)
    blk = lambda j, be, na: (jnp.minimum(j, na[0] - 1), 0)
    exp = lambda j, be, na: (be[jnp.minimum(j, na[0] - 1)], 0, 0)
    rows = (MOE_BLOCK * ROW_TILE, LANES)
    return pl.pallas_call(
        _experts_kernel,
        grid_spec=pltpu.PrefetchScalarGridSpec(
            num_scalar_prefetch=2,
            grid=(nb,),
            in_specs=[
                pl.BlockSpec(rows, blk),
                pl.BlockSpec(memory_space=pl.ANY),
                pl.BlockSpec((1, 1, 2 * D_EXPERT), exp),
                pl.BlockSpec(memory_space=pl.ANY),
                pl.BlockSpec((1, 1, D_MODEL), exp),
            ],
            out_specs=pl.BlockSpec(rows, lambda j, be, na: (j, 0)),
            scratch_shapes=[
                pltpu.VMEM((2, D_MODEL, 2 * D_EXPERT), F32), pltpu.VMEM((2, D_EXPERT, D_MODEL), F32),
                pltpu.VMEM((D_MODEL, 2 * D_EXPERT), BF16), pltpu.VMEM((D_EXPERT, D_MODEL), BF16),
                pltpu.SemaphoreType.DMA((2,)), pltpu.SMEM((1,), I32),
            ],
        ),
        out_shape=jax.ShapeDtypeStruct(xs.shape, F32),
        compiler_params=_cparams(("arbitrary",)),
        name="moe_experts",
    )(blk_e, nact, xs, w1, b1, w2, b2)


def _block_diag(w):
    h, d, _ = w.shape
    eye = jnp.eye(h, dtype=w.dtype)
    return (eye[:, None, :, None] * w[:, :, None, :]).reshape(h * d, h * d)


def kernel(x_prompt, x_sample, state_lru_h, cache_lru_conv, cache_cf_conv, cache_mem_k, cache_mem_v,
           mem_prompt, g_mix, w_in, w_lru_conv, b_lru_conv, w_lru_a, b_lru_a, w_lru_x, b_lru_x,
           lru_lambda, w_cf_conv, b_cf_conv, g_cf_ln, b_cf_ln, w_out, g_xa, g_mem, w_q, w_k, w_v, w_o,
           g_moe, w_router, b_router, w_e1, b_e1, w_e2, b_e2, g_final):
    depth = g_mix.shape[0]
    assert depth == 1, "single-layer stack"
    batch, seq, _ = x_prompt.shape
    n_s = x_sample.shape[0]
    assert x_sample.shape[1] == 1
    n_p = batch * seq
    l = 0

    mix_w = (
        g_mix[l][None], w_in[l].astype(BF16), w_lru_conv[l], b_lru_conv[l][None],
        jnp.concatenate([_block_diag(w_lru_a[l]), _block_diag(w_lru_x[l])], axis=1).astype(BF16),
        jnp.concatenate([b_lru_a[l], b_lru_x[l]])[None], lru_lambda[l][None],
        w_cf_conv[l], b_cf_conv[l][None], g_cf_ln[l][None], b_cf_ln[l][None], w_out[l].astype(BF16),
    )
    router_w = (g_moe[l][None], w_router[l].T.astype(BF16), b_router[l][:, None])
    wq, wo = w_q[l].astype(BF16), w_o[l].astype(BF16)
    gxa = g_xa[l][None]

    xp = x_prompt.reshape(n_p, D_MODEL)
    xs_ = x_sample.reshape(n_s, D_MODEL)
    x1p, hp, lcp, ccp = _mixer_prompt(xp, mix_w, batch, seq)
    x1s, hs, zx_s, glu_s = _mixer_sample(xs_, mix_w, state_lru_h[l], cache_lru_conv[l], cache_cf_conv[l])

    mk, mv, mkb, mvb = _memkv(mem_prompt.reshape(batch * N_MEM, D_MODEL), g_mem[l][None],
                              w_k[l].astype(BF16), w_v[l].astype(BF16))
    x2p, xnp_, top_p, gates_p = _attn_prompt(x1p, mkb, mvb, gxa, wq, wo, router_w, batch, seq)
    x2s, xns, top_s, gates_s = _attn_sample(x1s, cache_mem_k[l], cache_mem_v[l], gxa, wq, wo, router_w)

    dest_p, dest_s, blk_e, nact, cnt, pstart = _slots(top_p, top_s)
    nb = _moe_num_blocks((n_p + n_s) * TOP_K)
    nact = nact.reshape(1)
    xs = _dispatch(cnt.reshape(N_EXPERTS), pstart.reshape(N_EXPERTS), nact, dest_p, dest_s, xnp_, xns,
                   nb * MOE_BLOCK)
    ys = _experts(blk_e.reshape(nb), nact, xs, w_e1[l], b_e1[l][:, None, :], w_e2[l], b_e2[l][:, None, :])
    gfin = g_final[None]
    y_p = _combine(dest_p, gates_p.T, x2p, gfin, ys)
    y_s = _combine(dest_s, gates_s.T, x2s, gfin, ys)

    lru_conv_s = jnp.concatenate([cache_lru_conv[l][:, 1:], zx_s[:, None]], axis=1)
    cf_conv_s = jnp.concatenate([cache_cf_conv[l][:, 1:], glu_s[:, None]], axis=1)
    return (
        y_p.reshape(batch, seq, D_MODEL),
        y_s.reshape(n_s, 1, D_MODEL),
        hp.reshape(depth, batch, LRU_WIDTH),
        lcp[None],
        ccp[None],
        mk.reshape(depth, batch, N_MEM, XA_HEADS, XA_HEAD_DIM),
        mv.reshape(depth, batch, N_MEM, XA_HEADS, XA_HEAD_DIM),
        hs[None],
        lru_conv_s[None],
        cf_conv_s[None],
    )
```

```python
import functools

import jax
import jax.numpy as jnp
from jax import lax
from jax.experimental import pallas as pl
from jax.experimental.pallas import tpu as pltpu

F32 = jnp.float32
BF16 = jnp.bfloat16
I32 = jnp.int32

D_MODEL = 1024
LRU_WIDTH = 512
CF_WIDTH = 512
LRU_HEADS = 8
LRU_CONV = 4
LRU_C = 8.0
CF_CONV = 31
IN_WIDTH = 2 * LRU_WIDTH + 2 * CF_WIDTH
N_MEM = 256
XA_HEADS = 4
XA_HEAD_DIM = D_MODEL // XA_HEADS
N_EXPERTS = 32
TOP_K = 4
D_EXPERT = D_MODEL
SWIGLU_LIMIT = 7.0
SWIGLU_ALPHA = 1.702
EPS = 1e-6

LANES = 128
SUBLANES = 8
ROW_TILE = D_MODEL // LANES
assert ROW_TILE == SUBLANES
VMEM_LIMIT = 56 * 1024 * 1024

MIX_TILE = 256
LRU_TAIL_ROWS = 8
CF_TAIL_ROWS = 32
ATT_TILE = 512
SAMPLE_GROUP = 2
SAMPLE_MIX_GROUP = 32
MOE_BLOCK = 256
DMA_UNROLL = 8
DISPATCH_TILE = 512
COMBINE_TILE = 256


def _cparams(sem):
    return pltpu.CompilerParams(dimension_semantics=sem, vmem_limit_bytes=VMEM_LIMIT)


def _full(shape):
    n = len(shape)
    return pl.BlockSpec(shape, lambda *_: (0,) * n)


def _rms(x, g):
    return x * lax.rsqrt(jnp.mean(x * x, axis=-1, keepdims=True) + EPS) * g


def _bdot(a, b):
    return jnp.dot(a.astype(BF16), b, preferred_element_type=F32)


def _store_row_tiles(ref, x):
    m = x.shape[0]
    for s in range(ROW_TILE):
        ref[pl.ds(s, m, stride=ROW_TILE), :] = x[:, s * LANES:(s + 1) * LANES]


def _load_row_tiles(ref, m):
    return jnp.concatenate([ref[pl.ds(s, m, stride=ROW_TILE), :] for s in range(ROW_TILE)], axis=-1)


def _row_tile_copy(src_ref, src_row, dst_ref, dst_row, sem, n_rows=1):
    src = src_ref.at[pl.ds(pl.multiple_of(src_row * ROW_TILE, ROW_TILE), n_rows * ROW_TILE)]
    dst = dst_ref.at[pl.ds(pl.multiple_of(dst_row * ROW_TILE, ROW_TILE), n_rows * ROW_TILE)]
    return pltpu.make_async_copy(src, dst, sem)


def _neg_expm1(y):
    u = jnp.exp(y)
    d = 1.0 - u
    return jnp.where(d == 0.0, -y, jnp.where(u == 0.0, 1.0, d * y / jnp.log(u)))


def _lru_coeffs(xc, wg_ref, bg_ref, lam_ref):
    gl = _bdot(xc, wg_ref[...]) + bg_ref[...]
    r = jax.nn.sigmoid(gl[:, :LRU_WIDTH])
    i = jax.nn.sigmoid(gl[:, LRU_WIDTH:])
    log_a = -LRU_C * r * jax.nn.softplus(-lam_ref[...])
    a = jnp.exp(log_a)
    u = jnp.sqrt(_neg_expm1(2.0 * log_a)) * (i * xc)
    return a, u


def _cf_post(gc, gln_ref, bln_ref):
    mu = jnp.mean(gc, axis=-1, keepdims=True)
    c = gc - mu
    y = c * lax.rsqrt(jnp.mean(c * c, axis=-1, keepdims=True) + EPS)
    return jax.nn.silu(y * gln_ref[...] + bln_ref[...])


def _mixer_out(x, y_lru, y_cf, wout_ref):
    y = _bdot(y_lru, wout_ref[:LRU_WIDTH, :]) + _bdot(y_cf, wout_ref[LRU_WIDTH:, :])
    return x + y


def _mixer_prompt_kernel(x_ref, gmix_ref, win_ref, wlc_ref, blc_ref, wg_ref, bg_ref, lam_ref,
                         wcc_ref, bcc_ref, gln_ref, bln_ref, wout_ref,
                         x1_ref, h_ref, ltail_ref, ctail_ref,
                         zx_ext, glu_ext, h_carry):
    tt = MIX_TILE
    j = pl.program_id(1)

    @pl.when(j == 0)
    def _():
        zx_ext[0:LRU_TAIL_ROWS, :] = jnp.zeros((LRU_TAIL_ROWS, LRU_WIDTH), F32)
        glu_ext[0:CF_TAIL_ROWS, :] = jnp.zeros((CF_TAIL_ROWS, CF_WIDTH), F32)
        h_carry[...] = jnp.zeros((1, LRU_WIDTH), F32)

    x = x_ref[...]
    z = _bdot(_rms(x, gmix_ref[...]), win_ref[...])
    zx = z[:, :LRU_WIDTH]
    zg = z[:, LRU_WIDTH:2 * LRU_WIDTH]
    za = z[:, 2 * LRU_WIDTH:2 * LRU_WIDTH + CF_WIDTH]
    zb = z[:, 2 * LRU_WIDTH + CF_WIDTH:]

    zx_ext[LRU_TAIL_ROWS:LRU_TAIL_ROWS + tt, :] = zx
    off = LRU_TAIL_ROWS - (LRU_CONV - 1)
    xc = blc_ref[...]
    for k in range(LRU_CONV):
        xc = xc + wlc_ref[k:k + 1, :] * zx_ext[off + k:off + k + tt, :]
    a, u = _lru_coeffs(xc, wg_ref, bg_ref, lam_ref)
    sub = lax.broadcasted_iota(I32, (tt, LRU_WIDTH), 0) & (SUBLANES - 1)
    d = 1
    while d < SUBLANES:
        keep = sub >= d
        a_sh = jnp.where(keep, pltpu.roll(a, d, 0), 1.0)
        u_sh = jnp.where(keep, pltpu.roll(u, d, 0), 0.0)
        u = u + a * u_sh
        a = a * a_sh
        d *= 2
    h_last = h_carry[...]
    groups = []
    for r in range(0, tt, SUBLANES):
        hg = a[r:r + SUBLANES, :] * h_last + u[r:r + SUBLANES, :]
        groups.append(hg)
        h_last = hg[SUBLANES - 1:SUBLANES, :]
    hs = jnp.concatenate(groups, axis=0)
    h_carry[...] = h_last
    y_lru = hs * jax.nn.gelu(zg)

    glu = za * jax.nn.sigmoid(zb)
    glu_ext[CF_TAIL_ROWS:CF_TAIL_ROWS + tt, :] = glu
    off = CF_TAIL_ROWS - (CF_CONV - 1)
    gc = bcc_ref[...]
    for s in range(SUBLANES):
        rows = tt if s == 0 else tt + SUBLANES
        q = None
        for k in range(CF_CONV):
            if (off + k) % SUBLANES == s:
                base = off + k - s
                term = wcc_ref[k:k + 1, :] * glu_ext[base:base + rows, :]
                q = term if q is None else q + term
        gc = gc + (q if s == 0 else q[s:s + tt, :])
    y_cf = _cf_post(gc, gln_ref, bln_ref)

    x1_ref[...] = _mixer_out(x, y_lru, y_cf, wout_ref)

    h_ref[0] = h_last
    ltail_ref[0] = zx_ext[LRU_TAIL_ROWS + tt - (LRU_CONV - 1):LRU_TAIL_ROWS + tt, :]
    ctail_ref[0] = glu_ext[CF_TAIL_ROWS + tt - (CF_CONV - 1):CF_TAIL_ROWS + tt, :]
    zx_ext[0:LRU_TAIL_ROWS, :] = zx_ext[tt:tt + LRU_TAIL_ROWS, :]
    glu_ext[0:CF_TAIL_ROWS, :] = glu_ext[tt:tt + CF_TAIL_ROWS, :]


def _mixer_weight_specs():
    return [
        _full((1, D_MODEL)),
        _full((D_MODEL, IN_WIDTH)),
        _full((LRU_CONV, LRU_WIDTH)), _full((1, LRU_WIDTH)),
        _full((LRU_WIDTH, 2 * LRU_WIDTH)), _full((1, 2 * LRU_WIDTH)),
        _full((1, LRU_WIDTH)),
        _full((CF_CONV, CF_WIDTH)), _full((1, CF_WIDTH)),
        _full((1, CF_WIDTH)), _full((1, CF_WIDTH)),
        _full((D_MODEL, D_MODEL)),
    ]


def _mixer_prompt(x, mix_w, batch, seq):
    tt = MIX_TILE
    nt = seq // tt
    return pl.pallas_call(
        _mixer_prompt_kernel,
        grid=(batch, nt),
        in_specs=[pl.BlockSpec((tt, D_MODEL), lambda b, j: (b * nt + j, 0))] + _mixer_weight_specs(),
        out_specs=[
            pl.BlockSpec((tt, D_MODEL), lambda b, j: (b * nt + j, 0)),
            pl.BlockSpec((1, 1, LRU_WIDTH), lambda b, j: (b, 0, 0)),
            pl.BlockSpec((1, LRU_CONV - 1, LRU_WIDTH), lambda b, j: (b, 0, 0)),
            pl.BlockSpec((1, CF_CONV - 1, CF_WIDTH), lambda b, j: (b, 0, 0)),
        ],
        out_shape=[
            jax.ShapeDtypeStruct((batch * seq, D_MODEL), F32),
            jax.ShapeDtypeStruct((batch, 1, LRU_WIDTH), F32),
            jax.ShapeDtypeStruct((batch, LRU_CONV - 1, LRU_WIDTH), F32),
            jax.ShapeDtypeStruct((batch, CF_CONV - 1, CF_WIDTH), F32),
        ],
        scratch_shapes=[
            pltpu.VMEM((LRU_TAIL_ROWS + tt, LRU_WIDTH), F32),
            pltpu.VMEM((CF_TAIL_ROWS + tt, CF_WIDTH), F32),
            pltpu.VMEM((1, LRU_WIDTH), F32),
        ],
        compiler_params=_cparams(("arbitrary", "arbitrary")),
        name="mixer_prompt",
    )(x, *mix_w)


def _mixer_sample_kernel(x_ref, gmix_ref, win_ref, wlc_ref, blc_ref, wg_ref, bg_ref, lam_ref,
                         wcc_ref, bcc_ref, gln_ref, bln_ref, wout_ref,
                         h0_ref, lbuf_ref, cbuf_ref,
                         x1_ref, h_ref, zx_ref, glu_ref):
    x = x_ref[...]
    z = _bdot(_rms(x, gmix_ref[...]), win_ref[...])
    zx = z[:, :LRU_WIDTH]
    zg = z[:, LRU_WIDTH:2 * LRU_WIDTH]
    za = z[:, 2 * LRU_WIDTH:2 * LRU_WIDTH + CF_WIDTH]
    zb = z[:, 2 * LRU_WIDTH + CF_WIDTH:]

    xc = blc_ref[...] + wlc_ref[LRU_CONV - 1:LRU_CONV, :] * zx
    xc = xc + jnp.sum(lbuf_ref[...] * wlc_ref[0:LRU_CONV - 1, :][None], axis=1)
    a, u = _lru_coeffs(xc, wg_ref, bg_ref, lam_ref)
    h = a * h0_ref[...] + u
    y_lru = h * jax.nn.gelu(zg)

    glu = za * jax.nn.sigmoid(zb)
    gc = bcc_ref[...] + wcc_ref[CF_CONV - 1:CF_CONV, :] * glu
    gc = gc + jnp.sum(cbuf_ref[...] * wcc_ref[0:CF_CONV - 1, :][None], axis=1)
    y_cf = _cf_post(gc, gln_ref, bln_ref)

    x1_ref[...] = _mixer_out(x, y_lru, y_cf, wout_ref)
    h_ref[...] = h
    zx_ref[...] = zx
    glu_ref[...] = glu


def _mixer_sample(x, mix_w, h0, lbuf, cbuf):
    n = x.shape[0]
    g = SAMPLE_MIX_GROUP
    row = lambda w: pl.BlockSpec((g, w), lambda i: (i, 0))
    return pl.pallas_call(
        _mixer_sample_kernel,
        grid=(n // g,),
        in_specs=[row(D_MODEL)] + _mixer_weight_specs() + [
            row(LRU_WIDTH),
            pl.BlockSpec((g, LRU_CONV - 1, LRU_WIDTH), lambda i: (i, 0, 0)),
            pl.BlockSpec((g, CF_CONV - 1, CF_WIDTH), lambda i: (i, 0, 0)),
        ],
        out_specs=[row(D_MODEL), row(LRU_WIDTH), row(LRU_WIDTH), row(CF_WIDTH)],
        out_shape=[
            jax.ShapeDtypeStruct((n, D_MODEL), F32),
            jax.ShapeDtypeStruct((n, LRU_WIDTH), F32),
            jax.ShapeDtypeStruct((n, LRU_WIDTH), F32),
            jax.ShapeDtypeStruct((n, CF_WIDTH), F32),
        ],
        compiler_params=_cparams(("arbitrary",)),
        name="mixer_sample",
    )(x, *mix_w, h0, lbuf, cbuf)


def _memkv_kernel(mem_ref, gmem_ref, wk_ref, wv_ref, k_ref, v_ref, kb_ref, vb_ref):
    mn = _rms(mem_ref[...], gmem_ref[...]).astype(BF16)
    k = jnp.dot(mn, wk_ref[...], preferred_element_type=F32)
    v = jnp.dot(mn, wv_ref[...], preferred_element_type=F32)
    for h in range(XA_HEADS):
        sl = slice(h * XA_HEAD_DIM, (h + 1) * XA_HEAD_DIM)
        k_ref[0, :, h, :] = k[:, sl]
        v_ref[0, :, h, :] = v[:, sl]
    kb_ref[...] = k.astype(BF16)
    vb_ref[...] = v.astype(BF16)


def _memkv(mem, g_mem, wk, wv):
    n = mem.shape[0]
    t = N_MEM
    row = pl.BlockSpec((t, D_MODEL), lambda i: (i, 0))
    state = pl.BlockSpec((1, N_MEM, XA_HEADS, XA_HEAD_DIM), lambda i: (i, 0, 0, 0))
    state_shape = jax.ShapeDtypeStruct((n // t, N_MEM, XA_HEADS, XA_HEAD_DIM), F32)
    return pl.pallas_call(
        _memkv_kernel,
        grid=(n // t,),
        in_specs=[row, _full((1, D_MODEL)), _full((D_MODEL, D_MODEL)), _full((D_MODEL, D_MODEL))],
        out_specs=[state, state, row, row],
        out_shape=[state_shape] * 2 + [jax.ShapeDtypeStruct((n, D_MODEL), BF16)] * 2,
        compiler_params=_cparams(("arbitrary",)),
        name="memory_kv",
    )(mem, g_mem, wk, wv)


def _router(x2, gmoe_ref, wrt_ref, brt_ref, xn_ref, tope_ref, gates_ref):
    xn = _rms(x2, gmoe_ref[...])
    _store_row_tiles(xn_ref, xn)
    logits = lax.dot_general(wrt_ref[...], xn.astype(BF16), (((1,), (1,)), ((), ())),
                             preferred_element_type=F32) + brt_ref[...]
    e_iota = lax.broadcasted_iota(I32, logits.shape, 0)
    work = logits
    vals, idxs = [], []
    for _ in range(TOP_K):
        m = jnp.max(work, axis=0, keepdims=True)
        idx = jnp.min(jnp.where(work == m, e_iota, N_EXPERTS), axis=0, keepdims=True)
        vals.append(m)
        idxs.append(idx)
        work = jnp.where(e_iota == idx, -jnp.inf, work)
    ex = [jnp.exp(v - vals[0]) for v in vals]
    den = ex[0] + ex[1] + ex[2] + ex[3]
    tope_ref[...] = jnp.concatenate(idxs, axis=0)
    gates_ref[...] = jnp.concatenate([e / den for e in ex], axis=0)


def _router_specs():
    return [_full((1, D_MODEL)), _full((N_EXPERTS, D_MODEL)), _full((N_EXPERTS, 1))]


def _router_out(n, tile, index):
    specs = [
        pl.BlockSpec((tile, D_MODEL), lambda *a: (index(*a), 0)),
        pl.BlockSpec((tile * ROW_TILE, LANES), lambda *a: (index(*a), 0)),
        pl.BlockSpec((TOP_K, tile), lambda *a: (0, index(*a))),
        pl.BlockSpec((TOP_K, tile), lambda *a: (0, index(*a))),
    ]
    shapes = [
        jax.ShapeDtypeStruct((n, D_MODEL), F32),
        jax.ShapeDtypeStruct((n * ROW_TILE, LANES), F32),
        jax.ShapeDtypeStruct((TOP_K, n), I32),
        jax.ShapeDtypeStruct((TOP_K, n), F32),
    ]
    return specs, shapes


def _attn_prompt_kernel(x1_ref, k_ref, v_ref, gxa_ref, wq_ref, wo_ref, gmoe_ref, wrt_ref, brt_ref,
                        x2_ref, xn_ref, tope_ref, gates_ref):
    x1 = x1_ref[...]
    q = (_bdot(_rms(x1, gxa_ref[...]), wq_ref[...]) * (XA_HEAD_DIM ** -0.5)).astype(BF16)
    outs = []
    for h in range(XA_HEADS):
        sl = slice(h * XA_HEAD_DIM, (h + 1) * XA_HEAD_DIM)
        s = lax.dot_general(q[:, sl], k_ref[:, sl], (((1,), (1,)), ((), ())), preferred_element_type=F32)
        p = jnp.exp(s - jnp.max(s, axis=-1, keepdims=True))
        p = p / jnp.sum(p, axis=-1, keepdims=True)
        outs.append(_bdot(p, v_ref[:, sl]))
    o = jnp.concatenate(outs, axis=-1)
    x2 = x1 + _bdot(o, wo_ref[...])
    x2_ref[...] = x2
    _router(x2, gmoe_ref, wrt_ref, brt_ref, xn_ref, tope_ref, gates_ref)


def _attn_prompt(x1, k, v, gxa, wq, wo, router_w, batch, seq):
    tq = ATT_TILE
    nt = seq // tq
    out_specs, out_shapes = _router_out(batch * seq, tq, lambda b, j: b * nt + j)
    kv = pl.BlockSpec((N_MEM, D_MODEL), lambda b, j: (b, 0))
    return pl.pallas_call(
        _attn_prompt_kernel,
        grid=(batch, nt),
        in_specs=[pl.BlockSpec((tq, D_MODEL), lambda b, j: (b * nt + j, 0)), kv, kv,
                  _full((1, D_MODEL)), _full((D_MODEL, D_MODEL)), _full((D_MODEL, D_MODEL))] + _router_specs(),
        out_specs=out_specs,
        out_shape=out_shapes,
        compiler_params=_cparams(("arbitrary", "arbitrary")),
        name="attn_prompt",
    )(x1, k, v, gxa, wq, wo, *router_w)


def _qproj_kernel(x1_ref, gxa_ref, wq_ref, q_ref):
    q_ref[...] = _bdot(_rms(x1_ref[...], gxa_ref[...]), wq_ref[...]) * (XA_HEAD_DIM ** -0.5)


def _attn_sample_core_kernel(q_ref, k_ref, v_ref, o_ref):
    q = q_ref[...]
    s = jnp.sum(k_ref[...] * q[:, None], axis=-1, keepdims=True)
    p = jnp.exp(s - jnp.max(s, axis=1, keepdims=True))
    o_ref[...] = jnp.sum(p * v_ref[...], axis=1) / jnp.sum(p, axis=1)


def _oproj_router_kernel(x1_ref, o_ref, wo_ref, gmoe_ref, wrt_ref, brt_ref,
                         x2_ref, xn_ref, tope_ref, gates_ref):
    x2 = x1_ref[...] + _bdot(o_ref[...], wo_ref[...])
    x2_ref[...] = x2
    _router(x2, gmoe_ref, wrt_ref, brt_ref, xn_ref, tope_ref, gates_ref)


def _attn_sample(x1, k, v, gxa, wq, wo, router_w):
    n = x1.shape[0]
    g = SAMPLE_GROUP
    q = pl.pallas_call(
        _qproj_kernel,
        grid=(1,),
        in_specs=[_full((n, D_MODEL)), _full((1, D_MODEL)), _full((D_MODEL, D_MODEL))],
        out_specs=_full((n, D_MODEL)),
        out_shape=jax.ShapeDtypeStruct((n, D_MODEL), F32),
        compiler_params=_cparams(("arbitrary",)),
        name="qproj_sample",
    )(x1, gxa, wq)
    kv = pl.BlockSpec((g, N_MEM, XA_HEADS, XA_HEAD_DIM), lambda i: (i, 0, 0, 0))
    row = pl.BlockSpec((g, XA_HEADS, XA_HEAD_DIM), lambda i: (i, 0, 0))
    o = pl.pallas_call(
        _attn_sample_core_kernel,
        grid=(n // g,),
        in_specs=[row, kv, kv],
        out_specs=row,
        out_shape=jax.ShapeDtypeStruct((n, XA_HEADS, XA_HEAD_DIM), F32),
        compiler_params=_cparams(("arbitrary",)),
        name="attn_sample_core",
    )(q.reshape(n, XA_HEADS, XA_HEAD_DIM), k, v).reshape(n, D_MODEL)
    out_specs, out_shapes = _router_out(n, n, lambda i: 0)
    return pl.pallas_call(
        _oproj_router_kernel,
        grid=(1,),
        in_specs=[_full((n, D_MODEL)), _full((n, D_MODEL)), _full((D_MODEL, D_MODEL))] + _router_specs(),
        out_specs=out_specs,
        out_shape=out_shapes,
        compiler_params=_cparams(("arbitrary",)),
        name="oproj_router_sample",
    )(x1, o, wo, *router_w)


def _moe_num_blocks(n_assign):
    return pl.cdiv(n_assign, MOE_BLOCK) + N_EXPERTS


def _slots_kernel(tp_ref, ts_ref, destp_ref, dests_ref, blke_ref, nact_ref, cnt_ref, pstart_ref,
                  rankp, ranks):
    c = LANES
    r_i = lax.broadcasted_iota(I32, (c, c), 0)
    c_i = lax.broadcasted_iota(I32, (c, c), 1)
    upper = (r_i < c_i).astype(BF16)
    e_iota = lax.broadcasted_iota(I32, (N_EXPERTS, c), 0)

    def make_body(top_ref, rank_ref):
        def body(ci, carry):
            lo = pl.multiple_of(ci * c, c)
            top = top_ref[:, pl.ds(lo, c)]
            hit = [e_iota == top[k:k + 1, :] for k in range(TOP_K)]
            cnt = sum(m.astype(F32) for m in hit)
            base = carry + jnp.dot(cnt.astype(BF16), upper, preferred_element_type=F32)
            rank_ref[:, pl.ds(lo, c)] = jnp.concatenate(
                [jnp.sum(jnp.where(m, base, 0.0), axis=0, keepdims=True) for m in hit], axis=0)
            return carry + jnp.sum(cnt, axis=1, keepdims=True)
        return body

    counts = jnp.zeros((N_EXPERTS, 1), F32)
    counts = lax.fori_loop(0, tp_ref.shape[1] // c, make_body(tp_ref, rankp), counts)
    counts = lax.fori_loop(0, ts_ref.shape[1] // c, make_body(ts_ref, ranks), counts)

    padded = jnp.floor((counts + (MOE_BLOCK - 1)) / MOE_BLOCK) * MOE_BLOCK
    er = lax.broadcasted_iota(I32, (N_EXPERTS, N_EXPERTS), 0)
    ec = lax.broadcasted_iota(I32, (N_EXPERTS, N_EXPERTS), 1)
    padded_row = jnp.sum(jnp.where(er == ec, padded, 0.0), axis=0, keepdims=True)
    cum = jnp.sum(jnp.where(ec <= er, padded_row, 0.0), axis=1, keepdims=True)
    pstart = cum - padded
    cnt_ref[...] = jnp.sum(jnp.where(er == ec, counts, 0.0), axis=0, keepdims=True).astype(I32)
    pstart_ref[...] = jnp.sum(jnp.where(er == ec, pstart, 0.0), axis=0, keepdims=True).astype(I32)

    def dest(top_ref, rank_ref, out_ref):
        n = top_ref.shape[1]
        ei = lax.broadcasted_iota(I32, (N_EXPERTS, n), 0)
        rows = []
        for k in range(TOP_K):
            start = jnp.sum(jnp.where(ei == top_ref[k:k + 1, :], pstart, 0.0), axis=0, keepdims=True)
            rows.append(start + rank_ref[k:k + 1, :])
        out_ref[...] = jnp.concatenate(rows, axis=0).astype(I32)

    dest(tp_ref, rankp, destp_ref)
    dest(ts_ref, ranks, dests_ref)

    nb = blke_ref.shape[1]
    blk_lo = lax.broadcasted_iota(I32, (N_EXPERTS, nb), 1).astype(F32) * MOE_BLOCK
    blk = jnp.sum((cum <= blk_lo).astype(F32), axis=0, keepdims=True)
    blke_ref[...] = jnp.minimum(blk, N_EXPERTS - 1).astype(I32)
    nact_ref[...] = (cum[N_EXPERTS - 1:N_EXPERTS, :] / MOE_BLOCK).astype(I32)


def _slots(top_p, top_s):
    n_p, n_s = top_p.shape[1], top_s.shape[1]
    nb = _moe_num_blocks((n_p + n_s) * TOP_K)
    return pl.pallas_call(
        _slots_kernel,
        grid=(1,),
        in_specs=[_full((TOP_K, n_p)), _full((TOP_K, n_s))],
        out_specs=[_full((TOP_K, n_p)), _full((TOP_K, n_s)), _full((1, nb)), _full((1, 1)),
                   _full((1, N_EXPERTS)), _full((1, N_EXPERTS))],
        out_shape=[
            jax.ShapeDtypeStruct((TOP_K, n_p), I32),
            jax.ShapeDtypeStruct((TOP_K, n_s), I32),
            jax.ShapeDtypeStruct((1, nb), I32),
            jax.ShapeDtypeStruct((1, 1), I32),
            jax.ShapeDtypeStruct((1, N_EXPERTS), I32),
            jax.ShapeDtypeStruct((1, N_EXPERTS), I32),
        ],
        scratch_shapes=[pltpu.VMEM((TOP_K, n_p), F32), pltpu.VMEM((TOP_K, n_s), F32)],
        compiler_params=_cparams(("arbitrary",)),
        name="moe_slots",
    )(top_p, top_s)


def _for_rows(n_rows, fn):
    def body(i, c):
        for u in range(DMA_UNROLL):
            fn(i * DMA_UNROLL + u)
        return c
    lax.fori_loop(0, n_rows // DMA_UNROLL, body, 0)


def _dispatch_kernel(cnt_ref, pstart_ref, nact_ref, destp_ref, dests_ref, xnp_ref, xns_ref, xs_ref,
                     zeros, sem, zsem):
    i = pl.program_id(0)
    last = pl.num_programs(0) - 1

    def scatter(x_ref, dest_ref):
        n = x_ref.shape[0] // ROW_TILE

        def one(t):
            for k in range(TOP_K):
                _row_tile_copy(x_ref, t, xs_ref, dest_ref[t * TOP_K + k], sem).start(priority=k % 2)

        _for_rows(n, one)
        _row_tile_copy(xs_ref, 0, xs_ref, 0, sem, n * TOP_K).wait()

    @pl.when(i < last)
    def _():
        scatter(xnp_ref, destp_ref)

    @pl.when(i == last)
    def _():
        scatter(xns_ref, dests_ref)
        zeros[...] = jnp.zeros(zeros.shape, F32)

        def pad_copies(e, act):
            c = cnt_ref[e]
            npad = (-c) & (MOE_BLOCK - 1)
            base = pstart_ref[e] + c
            for bit in range(MOE_BLOCK.bit_length() - 1):
                size = 1 << bit

                @pl.when((npad & size) != 0)
                def _():
                    act(_row_tile_copy(zeros, 0, xs_ref, base + (npad & (size - 1)), zsem, size))

        def tail_copies(act):
            def body(j, c):
                act(_row_tile_copy(zeros, 0, xs_ref, j * MOE_BLOCK, zsem, MOE_BLOCK))
                return c
            lax.fori_loop(nact_ref[0], xs_ref.shape[0] // (MOE_BLOCK * ROW_TILE), body, 0)

        for act in (lambda d: d.start(), lambda d: d.wait()):
            lax.fori_loop(0, N_EXPERTS, lambda e, c, act=act: (pad_copies(e, act), c)[1], 0)
            tail_copies(act)


def _dispatch(cnt, pstart, nact, dest_p, dest_s, xn_p, xn_s, n_rows):
    n_p, n_s = xn_p.shape[0] // ROW_TILE, xn_s.shape[0] // ROW_TILE
    tile = DISPATCH_TILE
    nt = n_p // tile
    smem = pl.BlockSpec(memory_space=pltpu.SMEM)
    return pl.pallas_call(
        _dispatch_kernel,
        grid=(nt + 1,),
        in_specs=[
            smem, smem, smem,
            pl.BlockSpec((TOP_K * tile,), lambda i: (jnp.minimum(i, nt - 1),), memory_space=pltpu.SMEM),
            smem,
            pl.BlockSpec((tile * ROW_TILE, LANES), lambda i: (jnp.minimum(i, nt - 1), 0)),
            _full((n_s * ROW_TILE, LANES)),
        ],
        out_specs=pl.BlockSpec(memory_space=pl.ANY),
        out_shape=jax.ShapeDtypeStruct((n_rows * ROW_TILE, LANES), F32),
        scratch_shapes=[pltpu.VMEM((MOE_BLOCK * ROW_TILE, LANES), F32), pltpu.SemaphoreType.DMA,
                        pltpu.SemaphoreType.DMA],
        compiler_params=_cparams(("arbitrary",)),
        name="moe_dispatch",
    )(cnt, pstart, nact, dest_p, dest_s, xn_p, xn_s)


def _combine_kernel(dest_ref, dest_next_ref, gates_ref, x2_ref, gfin_ref, ys_ref, y_ref, bufs, sems):
    tile = x2_ref.shape[0]
    i = pl.program_id(0)
    slot = i % 2

    def gather(d_ref, b):
        def one(t):
            for k in range(TOP_K):
                _row_tile_copy(ys_ref, d_ref[t * TOP_K + k], bufs.at[b, k], t, sems.at[b]).start(priority=k % 2)
        _for_rows(tile, one)

    @pl.when(i == 0)
    def _():
        gather(dest_ref, 0)

    for b in range(2):
        @pl.when(slot == b)
        def _():
            pltpu.make_async_copy(bufs.at[b], bufs.at[b], sems.at[b]).wait()

            @pl.when(i + 1 < pl.num_programs(0))
            def _():
                gather(dest_next_ref, 1 - b)

    buf = bufs.at[slot]
    sumsq = jnp.zeros((tile, 1), F32)
    gate = [jnp.broadcast_to(gates_ref[:, k:k + 1], (tile, LANES)) for k in range(TOP_K)]
    for s in range(ROW_TILE):
        cols = slice(s * LANES, (s + 1) * LANES)
        acc = x2_ref[:, cols]
        for k in range(TOP_K):
            acc = acc + gate[k] * buf[k, pl.ds(s, tile, stride=ROW_TILE), :]
        y_ref[:, cols] = acc
        sumsq = sumsq + jnp.sum(acc * acc, axis=-1, keepdims=True)
    scale = lax.rsqrt(sumsq / D_MODEL + EPS)
    for s in range(ROW_TILE):
        cols = slice(s * LANES, (s + 1) * LANES)
        y_ref[:, cols] = y_ref[:, cols] * scale * gfin_ref[:, cols]


def _combine(dest, gates_t, x2, g_final, ys):
    n = x2.shape[0]
    tile = min(COMBINE_TILE, n)
    row = pl.BlockSpec((tile, D_MODEL), lambda i: (i, 0))
    nt = n // tile
    return pl.pallas_call(
        _combine_kernel,
        grid=(nt,),
        in_specs=[
            pl.BlockSpec((TOP_K * tile,), lambda i: (i,), memory_space=pltpu.SMEM),
            pl.BlockSpec((TOP_K * tile,), lambda i: (jnp.minimum(i + 1, nt - 1),), memory_space=pltpu.SMEM),
            pl.BlockSpec((tile, TOP_K), lambda i: (i, 0)),
            row,
            _full((1, D_MODEL)),
            pl.BlockSpec(memory_space=pl.ANY),
        ],
        out_specs=row,
        out_shape=jax.ShapeDtypeStruct((n, D_MODEL), F32),
        scratch_shapes=[pltpu.VMEM((2, TOP_K, tile * ROW_TILE, LANES), F32), pltpu.SemaphoreType.DMA((2,))],
        compiler_params=_cparams(("arbitrary",)),
        name="moe_combine",
    )(dest, dest, gates_t, x2, g_final, ys)


def _experts_kernel(blke_ref, nact_ref, xs_ref, w1_hbm, b1_ref, w2_hbm, b2_ref, ys_ref,
                    w1f, w2f, w1b, w2b, wsem, started):
    j = pl.program_id(0)
    nact = nact_ref[0]
    active = j < nact
    e = blke_ref[j]
    new_expert = jnp.logical_or(j == 0, e != blke_ref[jnp.maximum(j - 1, 0)])

    def fetch(expert, slot):
        return (pltpu.make_async_copy(w1_hbm.at[expert], w1f.at[slot], wsem.at[slot]),
                pltpu.make_async_copy(w2_hbm.at[expert], w2f.at[slot], wsem.at[slot]))

    @pl.when(j == 0)
    def _():
        started[0] = 0
        for d in fetch(e, 0):
            d.start()

    @pl.when(jnp.logical_and(active, new_expert))
    def _():
        n = started[0]
        last = blke_ref.shape[0] - 1
        jn = lax.while_loop(lambda jj: jnp.logical_and(jj < nact, blke_ref[jnp.minimum(jj, last)] == e),
                            lambda jj: jj + 1, j + 1)
        for b in range(2):
            @pl.when(n % 2 == b)
            def _():
                for d in fetch(e, b):
                    d.wait()

                @pl.when(jn < nact)
                def _():
                    for d in fetch(blke_ref[jnp.minimum(jn, last)], 1 - b):
                        d.start()

                w1b[...] = w1f[b].astype(BF16)
                w2b[...] = w2f[b].astype(BF16)
        started[0] = n + 1

    @pl.when(active)
    def _():
        x = _load_row_tiles(xs_ref, MOE_BLOCK).astype(BF16)
        gu = jnp.dot(x, w1b[...], preferred_element_type=F32) + b1_ref[0]
        g = jnp.minimum(gu[:, :D_EXPERT], SWIGLU_LIMIT)
        u = jnp.clip(gu[:, D_EXPERT:], -SWIGLU_LIMIT, SWIGLU_LIMIT)
        h = (u + 1.0) * (g * jax.nn.sigmoid(SWIGLU_ALPHA * g))
        _store_row_tiles(ys_ref, _bdot(h, w2b[...]) + b2_ref[0])

    @pl.when(jnp.logical_not(active))
    def _():
        ys_ref[...] = jnp.zeros(ys_ref.shape, F32)


def _experts(blk_e, nact, xs, w1, b1, w2, b2):
    nb = xs.shape[0] // (MOE_BLOCK * ROW_TILE)
    blk = lambda j, be, na: (jnp.minimum(j, na[0] - 1), 0)
    exp = lambda j, be, na: (be[jnp.minimum(j, na[0] - 1)], 0, 0)
    rows = (MOE_BLOCK * ROW_TILE, LANES)
    return pl.pallas_call(
        _experts_kernel,
        grid_spec=pltpu.PrefetchScalarGridSpec(
            num_scalar_prefetch=2,
            grid=(nb,),
            in_specs=[
                pl.BlockSpec(rows, blk),
                pl.BlockSpec(memory_space=pl.ANY),
                pl.BlockSpec((1, 1, 2 * D_EXPERT), exp),
                pl.BlockSpec(memory_space=pl.ANY),
                pl.BlockSpec((1, 1, D_MODEL), exp),
            ],
            out_specs=pl.BlockSpec(rows, lambda j, be, na: (j, 0)),
            scratch_shapes=[
                pltpu.VMEM((2, D_MODEL, 2 * D_EXPERT), F32), pltpu.VMEM((2, D_EXPERT, D_MODEL), F32),
                pltpu.VMEM((D_MODEL, 2 * D_EXPERT), BF16), pltpu.VMEM((D_EXPERT, D_MODEL), BF16),
                pltpu.SemaphoreType.DMA((2,)), pltpu.SMEM((1,), I32),
            ],
        ),
        out_shape=jax.ShapeDtypeStruct(xs.shape, F32),
        compiler_params=_cparams(("arbitrary",)),
        name="moe_experts",
    )(blk_e, nact, xs, w1, b1, w2, b2)


def _block_diag(w):
    h, d, _ = w.shape
    eye = jnp.eye(h, dtype=w.dtype)
    return (eye[:, None, :, None] * w[:, :, None, :]).reshape(h * d, h * d)


def kernel(x_prompt, x_sample, state_lru_h, cache_lru_conv, cache_cf_conv, cache_mem_k, cache_mem_v,
           mem_prompt, g_mix, w_in, w_lru_conv, b_lru_conv, w_lru_a, b_lru_a, w_lru_x, b_lru_x,
           lru_lambda, w_cf_conv, b_cf_conv, g_cf_ln, b_cf_ln, w_out, g_xa, g_mem, w_q, w_k, w_v, w_o,
           g_moe, w_router, b_router, w_e1, b_e1, w_e2, b_e2, g_final):
    depth = g_mix.shape[0]
    assert depth == 1, "single-layer stack"
    batch, seq, _ = x_prompt.shape
    n_s = x_sample.shape[0]
    assert x_sample.shape[1] == 1
    n_p = batch * seq
    l = 0

    mix_w = (
        g_mix[l][None], w_in[l].astype(BF16), w_lru_conv[l], b_lru_conv[l][None],
        jnp.concatenate([_block_diag(w_lru_a[l]), _block_diag(w_lru_x[l])], axis=1).astype(BF16),
        jnp.concatenate([b_lru_a[l], b_lru_x[l]])[None], lru_lambda[l][None],
        w_cf_conv[l], b_cf_conv[l][None], g_cf_ln[l][None], b_cf_ln[l][None], w_out[l].astype(BF16),
    )
    router_w = (g_moe[l][None], w_router[l].T.astype(BF16), b_router[l][:, None])
    wq, wo = w_q[l].astype(BF16), w_o[l].astype(BF16)
    gxa = g_xa[l][None]

    xp = x_prompt.reshape(n_p, D_MODEL)
    xs_ = x_sample.reshape(n_s, D_MODEL)
    x1p, hp, lcp, ccp = _mixer_prompt(xp, mix_w, batch, seq)
    x1s, hs, zx_s, glu_s = _mixer_sample(xs_, mix_w, state_lru_h[l], cache_lru_conv[l], cache_cf_conv[l])

    mk, mv, mkb, mvb = _memkv(mem_prompt.reshape(batch * N_MEM, D_MODEL), g_mem[l][None],
                              w_k[l].astype(BF16), w_v[l].astype(BF16))
    x2p, xnp_, top_p, gates_p = _attn_prompt(x1p, mkb, mvb, gxa, wq, wo, router_w, batch, seq)
    x2s, xns, top_s, gates_s = _attn_sample(x1s, cache_mem_k[l], cache_mem_v[l], gxa, wq, wo, router_w)

    dest_p, dest_s, blk_e, nact, cnt, pstart = _slots(top_p, top_s)
    dest_p, dest_s = dest_p.T.reshape(-1), dest_s.T.reshape(-1)
    nb = _moe_num_blocks((n_p + n_s) * TOP_K)
    nact = nact.reshape(1)
    xs = _dispatch(cnt.reshape(N_EXPERTS), pstart.reshape(N_EXPERTS), nact, dest_p, dest_s, xnp_, xns,
                   nb * MOE_BLOCK)
    ys = _experts(blk_e.reshape(nb), nact, xs, w_e1[l], b_e1[l][:, None, :], w_e2[l], b_e2[l][:, None, :])
    gfin = g_final[None]
    y_p = _combine(dest_p, gates_p.T, x2p, gfin, ys)
    y_s = _combine(dest_s, gates_s.T, x2s, gfin, ys)

    lru_conv_s = jnp.concatenate([cache_lru_conv[l][:, 1:], zx_s[:, None]], axis=1)
    cf_conv_s = jnp.concatenate([cache_cf_conv[l][:, 1:], glu_s[:, None]], axis=1)
    return (
        y_p.reshape(batch, seq, D_MODEL),
        y_s.reshape(n_s, 1, D_MODEL),
        hp.reshape(depth, batch, LRU_WIDTH),
        lcp[None],
        ccp[None],
        mk[None],
        mv[None],
        hs[None],
        lru_conv_s[None],
        cf_conv_s[None],
    )
```

```python
import functools

import jax
import jax.numpy as jnp
from jax import lax
from jax.experimental import pallas as pl
from jax.experimental.pallas import tpu as pltpu

F32 = jnp.float32
BF16 = jnp.bfloat16
I32 = jnp.int32

D_MODEL = 1024
LRU_WIDTH = 512
CF_WIDTH = 512
LRU_HEADS = 8
LRU_CONV = 4
LRU_C = 8.0
CF_CONV = 31
IN_WIDTH = 2 * LRU_WIDTH + 2 * CF_WIDTH
N_MEM = 256
XA_HEADS = 4
XA_HEAD_DIM = D_MODEL // XA_HEADS
N_EXPERTS = 32
TOP_K = 4
D_EXPERT = D_MODEL
SWIGLU_LIMIT = 7.0
SWIGLU_ALPHA = 1.702
EPS = 1e-6

LANES = 128
SUBLANES = 8
ROW_TILE = D_MODEL // LANES
assert ROW_TILE == SUBLANES
VMEM_LIMIT = 56 * 1024 * 1024

MIX_TILE = 256
LRU_TAIL_ROWS = 8
CF_TAIL_ROWS = 32
ATT_TILE = 512
SAMPLE_GROUP = 4
SAMPLE_MIX_GROUP = 32
MOE_BLOCK = 256
DMA_UNROLL = 8
DISPATCH_TILE = 512
COMBINE_TILE = 256


def _cparams(sem):
    return pltpu.CompilerParams(dimension_semantics=sem, vmem_limit_bytes=VMEM_LIMIT)


def _full(shape):
    n = len(shape)
    return pl.BlockSpec(shape, lambda *_: (0,) * n)


def _rms(x, g):
    return x * lax.rsqrt(jnp.mean(x * x, axis=-1, keepdims=True) + EPS) * g


def _bdot(a, b):
    return jnp.dot(a.astype(BF16), b, preferred_element_type=F32)


def _store_row_tiles(ref, x):
    m = x.shape[0]
    for s in range(ROW_TILE):
        ref[pl.ds(s, m, stride=ROW_TILE), :] = x[:, s * LANES:(s + 1) * LANES]


def _load_row_tiles(ref, m):
    return jnp.concatenate([ref[pl.ds(s, m, stride=ROW_TILE), :] for s in range(ROW_TILE)], axis=-1)


def _row_tile_copy(src_ref, src_row, dst_ref, dst_row, sem, n_rows=1):
    src = src_ref.at[pl.ds(pl.multiple_of(src_row * ROW_TILE, ROW_TILE), n_rows * ROW_TILE)]
    dst = dst_ref.at[pl.ds(pl.multiple_of(dst_row * ROW_TILE, ROW_TILE), n_rows * ROW_TILE)]
    return pltpu.make_async_copy(src, dst, sem)


def _neg_expm1(y):
    u = jnp.exp(y)
    d = 1.0 - u
    return jnp.where(d == 0.0, -y, jnp.where(u == 0.0, 1.0, d * y / jnp.log(u)))


def _lru_coeffs(xc, wg_ref, bg_ref, lam_ref):
    gl = _bdot(xc, wg_ref[...]) + bg_ref[...]
    r = jax.nn.sigmoid(gl[:, :LRU_WIDTH])
    i = jax.nn.sigmoid(gl[:, LRU_WIDTH:])
    log_a = -LRU_C * r * jax.nn.softplus(-lam_ref[...])
    a = jnp.exp(log_a)
    u = jnp.sqrt(_neg_expm1(2.0 * log_a)) * (i * xc)
    return a, u


def _cf_post(gc, gln_ref, bln_ref):
    mu = jnp.mean(gc, axis=-1, keepdims=True)
    c = gc - mu
    y = c * lax.rsqrt(jnp.mean(c * c, axis=-1, keepdims=True) + EPS)
    return jax.nn.silu(y * gln_ref[...] + bln_ref[...])


def _mixer_out(x, y_lru, y_cf, wout_ref):
    y = _bdot(y_lru, wout_ref[:LRU_WIDTH, :]) + _bdot(y_cf, wout_ref[LRU_WIDTH:, :])
    return x + y


def _mixer_prompt_kernel(x_ref, gmix_ref, win_ref, wlc_ref, blc_ref, wg_ref, bg_ref, lam_ref,
                         wcc_ref, bcc_ref, gln_ref, bln_ref, wout_ref,
                         x1_ref, h_ref, ltail_ref, ctail_ref,
                         zx_ext, glu_ext, h_carry):
    tt = MIX_TILE
    j = pl.program_id(1)

    @pl.when(j == 0)
    def _():
        zx_ext[0:LRU_TAIL_ROWS, :] = jnp.zeros((LRU_TAIL_ROWS, LRU_WIDTH), F32)
        glu_ext[0:CF_TAIL_ROWS, :] = jnp.zeros((CF_TAIL_ROWS, CF_WIDTH), F32)
        h_carry[...] = jnp.zeros((1, LRU_WIDTH), F32)

    x = x_ref[...]
    z = _bdot(_rms(x, gmix_ref[...]), win_ref[...])
    zx = z[:, :LRU_WIDTH]
    zg = z[:, LRU_WIDTH:2 * LRU_WIDTH]
    za = z[:, 2 * LRU_WIDTH:2 * LRU_WIDTH + CF_WIDTH]
    zb = z[:, 2 * LRU_WIDTH + CF_WIDTH:]

    zx_ext[LRU_TAIL_ROWS:LRU_TAIL_ROWS + tt, :] = zx
    off = LRU_TAIL_ROWS - (LRU_CONV - 1)
    xc = blc_ref[...]
    for k in range(LRU_CONV):
        xc = xc + wlc_ref[k:k + 1, :] * zx_ext[off + k:off + k + tt, :]
    a, u = _lru_coeffs(xc, wg_ref, bg_ref, lam_ref)
    sub = lax.broadcasted_iota(I32, (tt, LRU_WIDTH), 0) & (SUBLANES - 1)
    d = 1
    while d < SUBLANES:
        keep = sub >= d
        a_sh = jnp.where(keep, pltpu.roll(a, d, 0), 1.0)
        u_sh = jnp.where(keep, pltpu.roll(u, d, 0), 0.0)
        u = u + a * u_sh
        a = a * a_sh
        d *= 2
    h_last = h_carry[...]
    groups = []
    for r in range(0, tt, SUBLANES):
        hg = a[r:r + SUBLANES, :] * h_last + u[r:r + SUBLANES, :]
        groups.append(hg)
        h_last = hg[SUBLANES - 1:SUBLANES, :]
    hs = jnp.concatenate(groups, axis=0)
    h_carry[...] = h_last
    y_lru = hs * jax.nn.gelu(zg)

    glu = za * jax.nn.sigmoid(zb)
    glu_ext[CF_TAIL_ROWS:CF_TAIL_ROWS + tt, :] = glu
    off = CF_TAIL_ROWS - (CF_CONV - 1)
    gc = bcc_ref[...]
    for s in range(SUBLANES):
        rows = tt if s == 0 else tt + SUBLANES
        q = None
        for k in range(CF_CONV):
            if (off + k) % SUBLANES == s:
                base = off + k - s
                term = wcc_ref[k:k + 1, :] * glu_ext[base:base + rows, :]
                q = term if q is None else q + term
        gc = gc + (q if s == 0 else q[s:s + tt, :])
    y_cf = _cf_post(gc, gln_ref, bln_ref)

    x1_ref[...] = _mixer_out(x, y_lru, y_cf, wout_ref)

    h_ref[0] = h_last
    ltail_ref[0] = zx_ext[LRU_TAIL_ROWS + tt - (LRU_CONV - 1):LRU_TAIL_ROWS + tt, :]
    ctail_ref[0] = glu_ext[CF_TAIL_ROWS + tt - (CF_CONV - 1):CF_TAIL_ROWS + tt, :]
    zx_ext[0:LRU_TAIL_ROWS, :] = zx_ext[tt:tt + LRU_TAIL_ROWS, :]
    glu_ext[0:CF_TAIL_ROWS, :] = glu_ext[tt:tt + CF_TAIL_ROWS, :]


def _mixer_weight_specs():
    return [
        _full((1, D_MODEL)),
        _full((D_MODEL, IN_WIDTH)),
        _full((LRU_CONV, LRU_WIDTH)), _full((1, LRU_WIDTH)),
        _full((LRU_WIDTH, 2 * LRU_WIDTH)), _full((1, 2 * LRU_WIDTH)),
        _full((1, LRU_WIDTH)),
        _full((CF_CONV, CF_WIDTH)), _full((1, CF_WIDTH)),
        _full((1, CF_WIDTH)), _full((1, CF_WIDTH)),
        _full((D_MODEL, D_MODEL)),
    ]


def _mixer_prompt(x, mix_w, batch, seq):
    tt = MIX_TILE
    nt = seq // tt
    return pl.pallas_call(
        _mixer_prompt_kernel,
        grid=(batch, nt),
        in_specs=[pl.BlockSpec((tt, D_MODEL), lambda b, j: (b * nt + j, 0))] + _mixer_weight_specs(),
        out_specs=[
            pl.BlockSpec((tt, D_MODEL), lambda b, j: (b * nt + j, 0)),
            pl.BlockSpec((1, 1, LRU_WIDTH), lambda b, j: (b, 0, 0)),
            pl.BlockSpec((1, LRU_CONV - 1, LRU_WIDTH), lambda b, j: (b, 0, 0)),
            pl.BlockSpec((1, CF_CONV - 1, CF_WIDTH), lambda b, j: (b, 0, 0)),
        ],
        out_shape=[
            jax.ShapeDtypeStruct((batch * seq, D_MODEL), F32),
            jax.ShapeDtypeStruct((batch, 1, LRU_WIDTH), F32),
            jax.ShapeDtypeStruct((batch, LRU_CONV - 1, LRU_WIDTH), F32),
            jax.ShapeDtypeStruct((batch, CF_CONV - 1, CF_WIDTH), F32),
        ],
        scratch_shapes=[
            pltpu.VMEM((LRU_TAIL_ROWS + tt, LRU_WIDTH), F32),
            pltpu.VMEM((CF_TAIL_ROWS + tt, CF_WIDTH), F32),
            pltpu.VMEM((1, LRU_WIDTH), F32),
        ],
        compiler_params=_cparams(("arbitrary", "arbitrary")),
        name="mixer_prompt",
    )(x, *mix_w)


def _mixer_sample_kernel(x_ref, gmix_ref, win_ref, wlc_ref, blc_ref, wg_ref, bg_ref, lam_ref,
                         wcc_ref, bcc_ref, gln_ref, bln_ref, wout_ref,
                         h0_ref, lbuf_ref, cbuf_ref,
                         x1_ref, h_ref, zx_ref, glu_ref):
    x = x_ref[...]
    z = _bdot(_rms(x, gmix_ref[...]), win_ref[...])
    zx = z[:, :LRU_WIDTH]
    zg = z[:, LRU_WIDTH:2 * LRU_WIDTH]
    za = z[:, 2 * LRU_WIDTH:2 * LRU_WIDTH + CF_WIDTH]
    zb = z[:, 2 * LRU_WIDTH + CF_WIDTH:]

    xc = blc_ref[...] + wlc_ref[LRU_CONV - 1:LRU_CONV, :] * zx
    xc = xc + jnp.sum(lbuf_ref[...] * wlc_ref[0:LRU_CONV - 1, :][None], axis=1)
    a, u = _lru_coeffs(xc, wg_ref, bg_ref, lam_ref)
    h = a * h0_ref[...] + u
    y_lru = h * jax.nn.gelu(zg)

    glu = za * jax.nn.sigmoid(zb)
    gc = bcc_ref[...] + wcc_ref[CF_CONV - 1:CF_CONV, :] * glu
    gc = gc + jnp.sum(cbuf_ref[...] * wcc_ref[0:CF_CONV - 1, :][None], axis=1)
    y_cf = _cf_post(gc, gln_ref, bln_ref)

    x1_ref[...] = _mixer_out(x, y_lru, y_cf, wout_ref)
    h_ref[...] = h
    zx_ref[...] = zx
    glu_ref[...] = glu


def _mixer_sample(x, mix_w, h0, lbuf, cbuf):
    n = x.shape[0]
    g = SAMPLE_MIX_GROUP
    row = lambda w: pl.BlockSpec((g, w), lambda i: (i, 0))
    return pl.pallas_call(
        _mixer_sample_kernel,
        grid=(n // g,),
        in_specs=[row(D_MODEL)] + _mixer_weight_specs() + [
            row(LRU_WIDTH),
            pl.BlockSpec((g, LRU_CONV - 1, LRU_WIDTH), lambda i: (i, 0, 0)),
            pl.BlockSpec((g, CF_CONV - 1, CF_WIDTH), lambda i: (i, 0, 0)),
        ],
        out_specs=[row(D_MODEL), row(LRU_WIDTH), row(LRU_WIDTH), row(CF_WIDTH)],
        out_shape=[
            jax.ShapeDtypeStruct((n, D_MODEL), F32),
            jax.ShapeDtypeStruct((n, LRU_WIDTH), F32),
            jax.ShapeDtypeStruct((n, LRU_WIDTH), F32),
            jax.ShapeDtypeStruct((n, CF_WIDTH), F32),
        ],
        compiler_params=_cparams(("arbitrary",)),
        name="mixer_sample",
    )(x, *mix_w, h0, lbuf, cbuf)


def _memkv_kernel(mem_ref, gmem_ref, wk_ref, wv_ref, k_ref, v_ref, kb_ref, vb_ref):
    mn = _rms(mem_ref[...], gmem_ref[...]).astype(BF16)
    k = jnp.dot(mn, wk_ref[...], preferred_element_type=F32)
    v = jnp.dot(mn, wv_ref[...], preferred_element_type=F32)
    for h in range(XA_HEADS):
        sl = slice(h * XA_HEAD_DIM, (h + 1) * XA_HEAD_DIM)
        k_ref[0, :, h, :] = k[:, sl]
        v_ref[0, :, h, :] = v[:, sl]
    kb_ref[...] = k.astype(BF16)
    vb_ref[...] = v.astype(BF16)


def _memkv(mem, g_mem, wk, wv):
    n = mem.shape[0]
    t = N_MEM
    row = pl.BlockSpec((t, D_MODEL), lambda i: (i, 0))
    state = pl.BlockSpec((1, N_MEM, XA_HEADS, XA_HEAD_DIM), lambda i: (i, 0, 0, 0))
    state_shape = jax.ShapeDtypeStruct((n // t, N_MEM, XA_HEADS, XA_HEAD_DIM), F32)
    return pl.pallas_call(
        _memkv_kernel,
        grid=(n // t,),
        in_specs=[row, _full((1, D_MODEL)), _full((D_MODEL, D_MODEL)), _full((D_MODEL, D_MODEL))],
        out_specs=[state, state, row, row],
        out_shape=[state_shape] * 2 + [jax.ShapeDtypeStruct((n, D_MODEL), BF16)] * 2,
        compiler_params=_cparams(("arbitrary",)),
        name="memory_kv",
    )(mem, g_mem, wk, wv)


def _router(x2, gmoe_ref, wrt_ref, brt_ref, xn_ref, tope_ref, gates_ref):
    xn = _rms(x2, gmoe_ref[...])
    _store_row_tiles(xn_ref, xn)
    logits = lax.dot_general(wrt_ref[...], xn.astype(BF16), (((1,), (1,)), ((), ())),
                             preferred_element_type=F32) + brt_ref[...]
    e_iota = lax.broadcasted_iota(I32, logits.shape, 0)
    work = logits
    vals, idxs = [], []
    for _ in range(TOP_K):
        m = jnp.max(work, axis=0, keepdims=True)
        idx = jnp.min(jnp.where(work == m, e_iota, N_EXPERTS), axis=0, keepdims=True)
        vals.append(m)
        idxs.append(idx)
        work = jnp.where(e_iota == idx, -jnp.inf, work)
    ex = [jnp.exp(v - vals[0]) for v in vals]
    den = ex[0] + ex[1] + ex[2] + ex[3]
    tope_ref[...] = jnp.concatenate(idxs, axis=0)
    gates_ref[...] = jnp.concatenate([e / den for e in ex], axis=0)


def _router_specs():
    return [_full((1, D_MODEL)), _full((N_EXPERTS, D_MODEL)), _full((N_EXPERTS, 1))]


def _router_out(n, tile, index):
    specs = [
        pl.BlockSpec((tile, D_MODEL), lambda *a: (index(*a), 0)),
        pl.BlockSpec((tile * ROW_TILE, LANES), lambda *a: (index(*a), 0)),
        pl.BlockSpec((TOP_K, tile), lambda *a: (0, index(*a))),
        pl.BlockSpec((TOP_K, tile), lambda *a: (0, index(*a))),
    ]
    shapes = [
        jax.ShapeDtypeStruct((n, D_MODEL), F32),
        jax.ShapeDtypeStruct((n * ROW_TILE, LANES), F32),
        jax.ShapeDtypeStruct((TOP_K, n), I32),
        jax.ShapeDtypeStruct((TOP_K, n), F32),
    ]
    return specs, shapes


def _attn_prompt_kernel(x1_ref, k_ref, v_ref, gxa_ref, wq_ref, wo_ref, gmoe_ref, wrt_ref, brt_ref,
                        x2_ref, xn_ref, tope_ref, gates_ref):
    x1 = x1_ref[...]
    q = (_bdot(_rms(x1, gxa_ref[...]), wq_ref[...]) * (XA_HEAD_DIM ** -0.5)).astype(BF16)
    outs = []
    for h in range(XA_HEADS):
        sl = slice(h * XA_HEAD_DIM, (h + 1) * XA_HEAD_DIM)
        s = lax.dot_general(q[:, sl], k_ref[:, sl], (((1,), (1,)), ((), ())), preferred_element_type=F32)
        p = jnp.exp(s - jnp.max(s, axis=-1, keepdims=True))
        p = p / jnp.sum(p, axis=-1, keepdims=True)
        outs.append(_bdot(p, v_ref[:, sl]))
    o = jnp.concatenate(outs, axis=-1)
    x2 = x1 + _bdot(o, wo_ref[...])
    x2_ref[...] = x2
    _router(x2, gmoe_ref, wrt_ref, brt_ref, xn_ref, tope_ref, gates_ref)


def _attn_prompt(x1, k, v, gxa, wq, wo, router_w, batch, seq):
    tq = ATT_TILE
    nt = seq // tq
    out_specs, out_shapes = _router_out(batch * seq, tq, lambda b, j: b * nt + j)
    kv = pl.BlockSpec((N_MEM, D_MODEL), lambda b, j: (b, 0))
    return pl.pallas_call(
        _attn_prompt_kernel,
        grid=(batch, nt),
        in_specs=[pl.BlockSpec((tq, D_MODEL), lambda b, j: (b * nt + j, 0)), kv, kv,
                  _full((1, D_MODEL)), _full((D_MODEL, D_MODEL)), _full((D_MODEL, D_MODEL))] + _router_specs(),
        out_specs=out_specs,
        out_shape=out_shapes,
        compiler_params=_cparams(("arbitrary", "arbitrary")),
        name="attn_prompt",
    )(x1, k, v, gxa, wq, wo, *router_w)


def _qproj_kernel(x1_ref, gxa_ref, wq_ref, q_ref):
    q_ref[...] = _bdot(_rms(x1_ref[...], gxa_ref[...]), wq_ref[...]) * (XA_HEAD_DIM ** -0.5)


def _attn_sample_core_kernel(q_ref, k_ref, v_ref, o_ref):
    n = N_MEM * SUBLANES
    col_head = lax.broadcasted_iota(I32, (SUBLANES, n), 1) & (SUBLANES - 1)
    same_head = col_head == lax.broadcasted_iota(I32, (SUBLANES, n), 0)
    kv_pad = jnp.zeros((N_MEM, SUBLANES - XA_HEADS, XA_HEAD_DIM), F32)
    q_pad = jnp.zeros((SUBLANES - XA_HEADS, XA_HEAD_DIM), F32)
    for g in range(q_ref.shape[0]):
        k8 = jnp.concatenate([k_ref[g], kv_pad], axis=1).reshape(n, XA_HEAD_DIM).astype(BF16)
        v8 = jnp.concatenate([v_ref[g], kv_pad], axis=1).reshape(n, XA_HEAD_DIM).astype(BF16)
        q8 = jnp.concatenate([q_ref[g], q_pad], axis=0).astype(BF16)
        s = lax.dot_general(q8, k8, (((1,), (1,)), ((), ())), preferred_element_type=F32)
        m = jnp.max(jnp.where(same_head, s, -jnp.inf), axis=-1, keepdims=True)
        p = jnp.where(same_head, jnp.exp(s - m), 0.0)
        den = jnp.sum(p, axis=-1, keepdims=True)
        o8 = jnp.dot(p.astype(BF16), v8, preferred_element_type=F32)
        o_ref[g] = (o8 / den)[0:XA_HEADS, :]


def _oproj_router_kernel(x1_ref, o_ref, wo_ref, gmoe_ref, wrt_ref, brt_ref,
                         x2_ref, xn_ref, tope_ref, gates_ref):
    x2 = x1_ref[...] + _bdot(o_ref[...], wo_ref[...])
    x2_ref[...] = x2
    _router(x2, gmoe_ref, wrt_ref, brt_ref, xn_ref, tope_ref, gates_ref)


def _attn_sample(x1, k, v, gxa, wq, wo, router_w):
    n = x1.shape[0]
    g = SAMPLE_GROUP
    q = pl.pallas_call(
        _qproj_kernel,
        grid=(1,),
        in_specs=[_full((n, D_MODEL)), _full((1, D_MODEL)), _full((D_MODEL, D_MODEL))],
        out_specs=_full((n, D_MODEL)),
        out_shape=jax.ShapeDtypeStruct((n, D_MODEL), F32),
        compiler_params=_cparams(("arbitrary",)),
        name="qproj_sample",
    )(x1, gxa, wq)
    kv = pl.BlockSpec((g, N_MEM, XA_HEADS, XA_HEAD_DIM), lambda i: (i, 0, 0, 0))
    row = pl.BlockSpec((g, XA_HEADS, XA_HEAD_DIM), lambda i: (i, 0, 0))
    o = pl.pallas_call(
        _attn_sample_core_kernel,
        grid=(n // g,),
        in_specs=[row, kv, kv],
        out_specs=row,
        out_shape=jax.ShapeDtypeStruct((n, XA_HEADS, XA_HEAD_DIM), F32),
        compiler_params=_cparams(("arbitrary",)),
        name="attn_sample_core",
    )(q.reshape(n, XA_HEADS, XA_HEAD_DIM), k, v).reshape(n, D_MODEL)
    out_specs, out_shapes = _router_out(n, n, lambda i: 0)
    return pl.pallas_call(
        _oproj_router_kernel,
        grid=(1,),
        in_specs=[_full((n, D_MODEL)), _full((n, D_MODEL)), _full((D_MODEL, D_MODEL))] + _router_specs(),
        out_specs=out_specs,
        out_shape=out_shapes,
        compiler_params=_cparams(("arbitrary",)),
        name="oproj_router_sample",
    )(x1, o, wo, *router_w)


def _moe_num_blocks(n_assign):
    return pl.cdiv(n_assign, MOE_BLOCK) + N_EXPERTS


def _slots_kernel(tp_ref, ts_ref, destp_ref, dests_ref, blke_ref, nact_ref, cnt_ref, pstart_ref,
                  rankp, ranks):
    c = LANES
    r_i = lax.broadcasted_iota(I32, (c, c), 0)
    c_i = lax.broadcasted_iota(I32, (c, c), 1)
    upper = (r_i < c_i).astype(BF16)
    e_iota = lax.broadcasted_iota(I32, (N_EXPERTS, c), 0)

    def make_body(top_ref, rank_ref):
        def body(ci, carry):
            lo = pl.multiple_of(ci * c, c)
            top = top_ref[:, pl.ds(lo, c)]
            hit = [e_iota == top[k:k + 1, :] for k in range(TOP_K)]
            cnt = sum(m.astype(F32) for m in hit)
            base = carry + jnp.dot(cnt.astype(BF16), upper, preferred_element_type=F32)
            rank_ref[:, pl.ds(lo, c)] = jnp.concatenate(
                [jnp.sum(jnp.where(m, base, 0.0), axis=0, keepdims=True) for m in hit], axis=0)
            return carry + jnp.sum(cnt, axis=1, keepdims=True)
        return body

    counts = jnp.zeros((N_EXPERTS, 1), F32)
    counts = lax.fori_loop(0, tp_ref.shape[1] // c, make_body(tp_ref, rankp), counts)
    counts = lax.fori_loop(0, ts_ref.shape[1] // c, make_body(ts_ref, ranks), counts)

    padded = jnp.floor((counts + (MOE_BLOCK - 1)) / MOE_BLOCK) * MOE_BLOCK
    er = lax.broadcasted_iota(I32, (N_EXPERTS, N_EXPERTS), 0)
    ec = lax.broadcasted_iota(I32, (N_EXPERTS, N_EXPERTS), 1)
    padded_row = jnp.sum(jnp.where(er == ec, padded, 0.0), axis=0, keepdims=True)
    cum = jnp.sum(jnp.where(ec <= er, padded_row, 0.0), axis=1, keepdims=True)
    pstart = cum - padded
    cnt_ref[...] = jnp.sum(jnp.where(er == ec, counts, 0.0), axis=0, keepdims=True).astype(I32)
    pstart_ref[...] = jnp.sum(jnp.where(er == ec, pstart, 0.0), axis=0, keepdims=True).astype(I32)

    def dest(top_ref, rank_ref, out_ref):
        n = top_ref.shape[1]
        ei = lax.broadcasted_iota(I32, (N_EXPERTS, n), 0)
        rows = []
        for k in range(TOP_K):
            start = jnp.sum(jnp.where(ei == top_ref[k:k + 1, :], pstart, 0.0), axis=0, keepdims=True)
            rows.append(start + rank_ref[k:k + 1, :])
        out_ref[...] = jnp.concatenate(rows, axis=0).astype(I32)

    dest(tp_ref, rankp, destp_ref)
    dest(ts_ref, ranks, dests_ref)

    nb = blke_ref.shape[1]
    blk_lo = lax.broadcasted_iota(I32, (N_EXPERTS, nb), 1).astype(F32) * MOE_BLOCK
    blk = jnp.sum((cum <= blk_lo).astype(F32), axis=0, keepdims=True)
    blke_ref[...] = jnp.minimum(blk, N_EXPERTS - 1).astype(I32)
    nact_ref[...] = (cum[N_EXPERTS - 1:N_EXPERTS, :] / MOE_BLOCK).astype(I32)


def _slots(top_p, top_s):
    n_p, n_s = top_p.shape[1], top_s.shape[1]
    nb = _moe_num_blocks((n_p + n_s) * TOP_K)
    return pl.pallas_call(
        _slots_kernel,
        grid=(1,),
        in_specs=[_full((TOP_K, n_p)), _full((TOP_K, n_s))],
        out_specs=[_full((TOP_K, n_p)), _full((TOP_K, n_s)), _full((1, nb)), _full((1, 1)),
                   _full((1, N_EXPERTS)), _full((1, N_EXPERTS))],
        out_shape=[
            jax.ShapeDtypeStruct((TOP_K, n_p), I32),
            jax.ShapeDtypeStruct((TOP_K, n_s), I32),
            jax.ShapeDtypeStruct((1, nb), I32),
            jax.ShapeDtypeStruct((1, 1), I32),
            jax.ShapeDtypeStruct((1, N_EXPERTS), I32),
            jax.ShapeDtypeStruct((1, N_EXPERTS), I32),
        ],
        scratch_shapes=[pltpu.VMEM((TOP_K, n_p), F32), pltpu.VMEM((TOP_K, n_s), F32)],
        compiler_params=_cparams(("arbitrary",)),
        name="moe_slots",
    )(top_p, top_s)


def _for_rows(n_rows, fn):
    def body(i, c):
        for u in range(DMA_UNROLL):
            fn(i * DMA_UNROLL + u)
        return c
    lax.fori_loop(0, n_rows // DMA_UNROLL, body, 0)


def _dispatch_kernel(cnt_ref, pstart_ref, nact_ref, destp_ref, dests_ref, xnp_ref, xns_ref, xs_ref,
                     zeros, sem, zsem):
    i = pl.program_id(0)
    last = pl.num_programs(0) - 1

    def scatter(x_ref, dest_ref):
        n = x_ref.shape[0] // ROW_TILE

        def one(t):
            for k in range(TOP_K):
                _row_tile_copy(x_ref, t, xs_ref, dest_ref[t * TOP_K + k], sem).start(priority=k % 2)

        _for_rows(n, one)
        _row_tile_copy(xs_ref, 0, xs_ref, 0, sem, n * TOP_K).wait()

    @pl.when(i < last)
    def _():
        scatter(xnp_ref, destp_ref)

    @pl.when(i == last)
    def _():
        scatter(xns_ref, dests_ref)
        zeros[...] = jnp.zeros(zeros.shape, F32)

        def pad_copies(e, act):
            c = cnt_ref[e]
            npad = (-c) & (MOE_BLOCK - 1)
            base = pstart_ref[e] + c
            for bit in range(MOE_BLOCK.bit_length() - 1):
                size = 1 << bit

                @pl.when((npad & size) != 0)
                def _():
                    act(_row_tile_copy(zeros, 0, xs_ref, base + (npad & (size - 1)), zsem, size))

        def tail_copies(act):
            def body(j, c):
                act(_row_tile_copy(zeros, 0, xs_ref, j * MOE_BLOCK, zsem, MOE_BLOCK))
                return c
            lax.fori_loop(nact_ref[0], xs_ref.shape[0] // (MOE_BLOCK * ROW_TILE), body, 0)

        for act in (lambda d: d.start(), lambda d: d.wait()):
            lax.fori_loop(0, N_EXPERTS, lambda e, c, act=act: (pad_copies(e, act), c)[1], 0)
            tail_copies(act)


def _dispatch(cnt, pstart, nact, dest_p, dest_s, xn_p, xn_s, n_rows):
    n_p, n_s = xn_p.shape[0] // ROW_TILE, xn_s.shape[0] // ROW_TILE
    tile = DISPATCH_TILE
    nt = n_p // tile
    smem = pl.BlockSpec(memory_space=pltpu.SMEM)
    return pl.pallas_call(
        _dispatch_kernel,
        grid=(nt + 1,),
        in_specs=[
            smem, smem, smem,
            pl.BlockSpec((TOP_K * tile,), lambda i: (jnp.minimum(i, nt - 1),), memory_space=pltpu.SMEM),
            smem,
            pl.BlockSpec((tile * ROW_TILE, LANES), lambda i: (jnp.minimum(i, nt - 1), 0)),
            _full((n_s * ROW_TILE, LANES)),
        ],
        out_specs=pl.BlockSpec(memory_space=pl.ANY),
        out_shape=jax.ShapeDtypeStruct((n_rows * ROW_TILE, LANES), F32),
        scratch_shapes=[pltpu.VMEM((MOE_BLOCK * ROW_TILE, LANES), F32), pltpu.SemaphoreType.DMA,
                        pltpu.SemaphoreType.DMA],
        compiler_params=_cparams(("arbitrary",)),
        name="moe_dispatch",
    )(cnt, pstart, nact, dest_p, dest_s, xn_p, xn_s)


def _combine_kernel(dest_ref, dest_next_ref, gates_ref, x2_ref, gfin_ref, ys_ref, y_ref, bufs, sems):
    tile = x2_ref.shape[0]
    i = pl.program_id(0)
    slot = i % 2

    def gather(d_ref, b):
        def one(t):
            for k in range(TOP_K):
                _row_tile_copy(ys_ref, d_ref[t * TOP_K + k], bufs.at[b, k], t, sems.at[b]).start(priority=k % 2)
        _for_rows(tile, one)

    @pl.when(i == 0)
    def _():
        gather(dest_ref, 0)

    for b in range(2):
        @pl.when(slot == b)
        def _():
            pltpu.make_async_copy(bufs.at[b], bufs.at[b], sems.at[b]).wait()

            @pl.when(i + 1 < pl.num_programs(0))
            def _():
                gather(dest_next_ref, 1 - b)

    buf = bufs.at[slot]
    sumsq = jnp.zeros((tile, 1), F32)
    gate = [jnp.broadcast_to(gates_ref[:, k:k + 1], (tile, LANES)) for k in range(TOP_K)]
    for s in range(ROW_TILE):
        cols = slice(s * LANES, (s + 1) * LANES)
        acc = x2_ref[:, cols]
        for k in range(TOP_K):
            acc = acc + gate[k] * buf[k, pl.ds(s, tile, stride=ROW_TILE), :]
        y_ref[:, cols] = acc
        sumsq = sumsq + jnp.sum(acc * acc, axis=-1, keepdims=True)
    scale = lax.rsqrt(sumsq / D_MODEL + EPS)
    for s in range(ROW_TILE):
        cols = slice(s * LANES, (s + 1) * LANES)
        y_ref[:, cols] = y_ref[:, cols] * scale * gfin_ref[:, cols]


def _combine(dest, gates_t, x2, g_final, ys):
    n = x2.shape[0]
    tile = min(COMBINE_TILE, n)
    row = pl.BlockSpec((tile, D_MODEL), lambda i: (i, 0))
    nt = n // tile
    return pl.pallas_call(
        _combine_kernel,
        grid=(nt,),
        in_specs=[
            pl.BlockSpec((TOP_K * tile,), lambda i: (i,), memory_space=pltpu.SMEM),
            pl.BlockSpec((TOP_K * tile,), lambda i: (jnp.minimum(i + 1, nt - 1),), memory_space=pltpu.SMEM),
            pl.BlockSpec((tile, TOP_K), lambda i: (i, 0)),
            row,
            _full((1, D_MODEL)),
            pl.BlockSpec(memory_space=pl.ANY),
        ],
        out_specs=row,
        out_shape=jax.ShapeDtypeStruct((n, D_MODEL), F32),
        scratch_shapes=[pltpu.VMEM((2, TOP_K, tile * ROW_TILE, LANES), F32), pltpu.SemaphoreType.DMA((2,))],
        compiler_params=_cparams(("arbitrary",)),
        name="moe_combine",
    )(dest, dest, gates_t, x2, g_final, ys)


def _experts_kernel(blke_ref, nact_ref, xs_ref, w1_hbm, b1_ref, w2_hbm, b2_ref, ys_ref,
                    w1f, w2f, w1b, w2b, wsem, started):
    j = pl.program_id(0)
    nact = nact_ref[0]
    active = j < nact
    e = blke_ref[j]
    new_expert = jnp.logical_or(j == 0, e != blke_ref[jnp.maximum(j - 1, 0)])

    def fetch(expert, slot):
        return (pltpu.make_async_copy(w1_hbm.at[expert], w1f.at[slot], wsem.at[slot]),
                pltpu.make_async_copy(w2_hbm.at[expert], w2f.at[slot], wsem.at[slot]))

    @pl.when(j == 0)
    def _():
        started[0] = 0
        for d in fetch(e, 0):
            d.start()

    @pl.when(jnp.logical_and(active, new_expert))
    def _():
        n = started[0]
        last = blke_ref.shape[0] - 1
        jn = lax.while_loop(lambda jj: jnp.logical_and(jj < nact, blke_ref[jnp.minimum(jj, last)] == e),
                            lambda jj: jj + 1, j + 1)
        for b in range(2):
            @pl.when(n % 2 == b)
            def _():
                for d in fetch(e, b):
                    d.wait()

                @pl.when(jn < nact)
                def _():
                    for d in fetch(blke_ref[jnp.minimum(jn, last)], 1 - b):
                        d.start(priority=1)

                w1b[...] = w1f[b].astype(BF16)
                w2b[...] = w2f[b].astype(BF16)
        started[0] = n + 1

    @pl.when(active)
    def _():
        x = _load_row_tiles(xs_ref, MOE_BLOCK).astype(BF16)
        gu = jnp.dot(x, w1b[...], preferred_element_type=F32) + b1_ref[0]
        g = jnp.minimum(gu[:, :D_EXPERT], SWIGLU_LIMIT)
        u = jnp.clip(gu[:, D_EXPERT:], -SWIGLU_LIMIT, SWIGLU_LIMIT)
        h = (u + 1.0) * (g * jax.nn.sigmoid(SWIGLU_ALPHA * g))
        _store_row_tiles(ys_ref, _bdot(h, w2b[...]) + b2_ref[0])

    @pl.when(jnp.logical_not(active))
    def _():
        ys_ref[...] = jnp.zeros(ys_ref.shape, F32)


def _experts(blk_e, nact, xs, w1, b1, w2, b2):
    nb = xs.shape[0] // (MOE_BLOCK * ROW_TILE)
    blk = lambda j, be, na: (jnp.minimum(j, na[0] - 1), 0)
    exp = lambda j, be, na: (be[jnp.minimum(j, na[0] - 1)], 0, 0)
    rows = (MOE_BLOCK * ROW_TILE, LANES)
    return pl.pallas_call(
        _experts_kernel,
        grid_spec=pltpu.PrefetchScalarGridSpec(
            num_scalar_prefetch=2,
            grid=(nb,),
            in_specs=[
                pl.BlockSpec(rows, blk),
                pl.BlockSpec(memory_space=pl.ANY),
                pl.BlockSpec((1, 1, 2 * D_EXPERT), exp),
                pl.BlockSpec(memory_space=pl.ANY),
                pl.BlockSpec((1, 1, D_MODEL), exp),
            ],
            out_specs=pl.BlockSpec(rows, lambda j, be, na: (j, 0)),
            scratch_shapes=[
                pltpu.VMEM((2, D_MODEL, 2 * D_EXPERT), F32), pltpu.VMEM((2, D_EXPERT, D_MODEL), F32),
                pltpu.VMEM((D_MODEL, 2 * D_EXPERT), BF16), pltpu.VMEM((D_EXPERT, D_MODEL), BF16),
                pltpu.SemaphoreType.DMA((2,)), pltpu.SMEM((1,), I32),
            ],
        ),
        out_shape=jax.ShapeDtypeStruct(xs.shape, F32),
        compiler_params=_cparams(("arbitrary",)),
        name="moe_experts",
    )(blk_e, nact, xs, w1, b1, w2, b2)


def _block_diag(w):
    h, d, _ = w.shape
    eye = jnp.eye(h, dtype=w.dtype)
    return (eye[:, None, :, None] * w[:, :, None, :]).reshape(h * d, h * d)


def kernel(x_prompt, x_sample, state_lru_h, cache_lru_conv, cache_cf_conv, cache_mem_k, cache_mem_v,
           mem_prompt, g_mix, w_in, w_lru_conv, b_lru_conv, w_lru_a, b_lru_a, w_lru_x, b_lru_x,
           lru_lambda, w_cf_conv, b_cf_conv, g_cf_ln, b_cf_ln, w_out, g_xa, g_mem, w_q, w_k, w_v, w_o,
           g_moe, w_router, b_router, w_e1, b_e1, w_e2, b_e2, g_final):
    depth = g_mix.shape[0]
    assert depth == 1, "single-layer stack"
    batch, seq, _ = x_prompt.shape
    n_s = x_sample.shape[0]
    assert x_sample.shape[1] == 1
    n_p = batch * seq
    l = 0

    mix_w = (
        g_mix[l][None], w_in[l].astype(BF16), w_lru_conv[l], b_lru_conv[l][None],
        jnp.concatenate([_block_diag(w_lru_a[l]), _block_diag(w_lru_x[l])], axis=1).astype(BF16),
        jnp.concatenate([b_lru_a[l], b_lru_x[l]])[None], lru_lambda[l][None],
        w_cf_conv[l], b_cf_conv[l][None], g_cf_ln[l][None], b_cf_ln[l][None], w_out[l].astype(BF16),
    )
    router_w = (g_moe[l][None], w_router[l].T.astype(BF16), b_router[l][:, None])
    wq, wo = w_q[l].astype(BF16), w_o[l].astype(BF16)
    gxa = g_xa[l][None]

    xp = x_prompt.reshape(n_p, D_MODEL)
    xs_ = x_sample.reshape(n_s, D_MODEL)
    x1p, hp, lcp, ccp = _mixer_prompt(xp, mix_w, batch, seq)
    x1s, hs, zx_s, glu_s = _mixer_sample(xs_, mix_w, state_lru_h[l], cache_lru_conv[l], cache_cf_conv[l])

    mk, mv, mkb, mvb = _memkv(mem_prompt.reshape(batch * N_MEM, D_MODEL), g_mem[l][None],
                              w_k[l].astype(BF16), w_v[l].astype(BF16))
    x2p, xnp_, top_p, gates_p = _attn_prompt(x1p, mkb, mvb, gxa, wq, wo, router_w, batch, seq)
    x2s, xns, top_s, gates_s = _attn_sample(x1s, cache_mem_k[l], cache_mem_v[l], gxa, wq, wo, router_w)

    dest_p, dest_s, blk_e, nact, cnt, pstart = _slots(top_p, top_s)
    dest_p, dest_s = dest_p.T.reshape(-1), dest_s.T.reshape(-1)
    nb = _moe_num_blocks((n_p + n_s) * TOP_K)
    nact = nact.reshape(1)
    xs = _dispatch(cnt.reshape(N_EXPERTS), pstart.reshape(N_EXPERTS), nact, dest_p, dest_s, xnp_, xns,
                   nb * MOE_BLOCK)
    ys = _experts(blk_e.reshape(nb), nact, xs, w_e1[l], b_e1[l][:, None, :], w_e2[l], b_e2[l][:, None, :])
    gfin = g_final[None]
    y_p = _combine(dest_p, gates_p.T, x2p, gfin, ys)
    y_s = _combine(dest_s, gates_s.T, x2s, gfin, ys)

    lru_conv_s = jnp.concatenate([cache_lru_conv[l][:, 1:], zx_s[:, None]], axis=1)
    cf_conv_s = jnp.concatenate([cache_cf_conv[l][:, 1:], glu_s[:, None]], axis=1)
    return (
        y_p.reshape(batch, seq, D_MODEL),
        y_s.reshape(n_s, 1, D_MODEL),
        hp.reshape(depth, batch, LRU_WIDTH),
        lcp[None],
        ccp[None],
        mk[None],
        mv[None],
        hs[None],
        lru_conv_s[None],
        cf_conv_s[None],
    )
```

```python
import functools

import jax
import jax.numpy as jnp
from jax import lax
from jax.experimental import pallas as pl
from jax.experimental.pallas import tpu as pltpu

F32 = jnp.float32
BF16 = jnp.bfloat16
I32 = jnp.int32

D_MODEL = 1024
LRU_WIDTH = 512
CF_WIDTH = 512
LRU_HEADS = 8
LRU_CONV = 4
LRU_C = 8.0
CF_CONV = 31
IN_WIDTH = 2 * LRU_WIDTH + 2 * CF_WIDTH
N_MEM = 256
XA_HEADS = 4
XA_HEAD_DIM = D_MODEL // XA_HEADS
N_EXPERTS = 32
TOP_K = 4
D_EXPERT = D_MODEL
SWIGLU_LIMIT = 7.0
SWIGLU_ALPHA = 1.702
EPS = 1e-6

LANES = 128
SUBLANES = 8
ROW_TILE = D_MODEL // LANES
assert ROW_TILE == SUBLANES
VMEM_LIMIT = 56 * 1024 * 1024

MIX_TILE = 256
LRU_TAIL_ROWS = 8
CF_TAIL_ROWS = 32
ATT_TILE = 512
SAMPLE_GROUP = 4
SAMPLE_MIX_GROUP = 32
MOE_BLOCK = 256
DMA_UNROLL = 8
DISPATCH_TILE = 512
COMBINE_TILE = 256


def _cparams(sem):
    return pltpu.CompilerParams(dimension_semantics=sem, vmem_limit_bytes=VMEM_LIMIT)


def _full(shape):
    n = len(shape)
    return pl.BlockSpec(shape, lambda *_: (0,) * n)


def _rms(x, g):
    return x * lax.rsqrt(jnp.mean(x * x, axis=-1, keepdims=True) + EPS) * g


def _bdot(a, b):
    return jnp.dot(a.astype(BF16), b, preferred_element_type=F32)


def _store_row_tiles(ref, x):
    m = x.shape[0]
    for s in range(ROW_TILE):
        ref[pl.ds(s, m, stride=ROW_TILE), :] = x[:, s * LANES:(s + 1) * LANES]


def _load_row_tiles(ref, m):
    return jnp.concatenate([ref[pl.ds(s, m, stride=ROW_TILE), :] for s in range(ROW_TILE)], axis=-1)


def _row_tile_copy(src_ref, src_row, dst_ref, dst_row, sem, n_rows=1):
    src = src_ref.at[pl.ds(pl.multiple_of(src_row * ROW_TILE, ROW_TILE), n_rows * ROW_TILE)]
    dst = dst_ref.at[pl.ds(pl.multiple_of(dst_row * ROW_TILE, ROW_TILE), n_rows * ROW_TILE)]
    return pltpu.make_async_copy(src, dst, sem)


def _neg_expm1(y):
    u = jnp.exp(y)
    d = 1.0 - u
    return jnp.where(d == 0.0, -y, jnp.where(u == 0.0, 1.0, d * y / jnp.log(u)))


def _lru_coeffs(xc, wg_ref, bg_ref, lam_ref):
    gl = _bdot(xc, wg_ref[...]) + bg_ref[...]
    r = jax.nn.sigmoid(gl[:, :LRU_WIDTH])
    i = jax.nn.sigmoid(gl[:, LRU_WIDTH:])
    log_a = -LRU_C * r * jax.nn.softplus(-lam_ref[...])
    a = jnp.exp(log_a)
    u = jnp.sqrt(_neg_expm1(2.0 * log_a)) * (i * xc)
    return a, u


def _cf_post(gc, gln_ref, bln_ref):
    mu = jnp.mean(gc, axis=-1, keepdims=True)
    c = gc - mu
    y = c * lax.rsqrt(jnp.mean(c * c, axis=-1, keepdims=True) + EPS)
    return jax.nn.silu(y * gln_ref[...] + bln_ref[...])


def _mixer_out(x, y_lru, y_cf, wout_ref):
    y = _bdot(y_lru, wout_ref[:LRU_WIDTH, :]) + _bdot(y_cf, wout_ref[LRU_WIDTH:, :])
    return x + y


def _mixer_prompt_kernel(x_ref, gmix_ref, win_ref, wlc_ref, blc_ref, wg_ref, bg_ref, lam_ref,
                         wcc_ref, bcc_ref, gln_ref, bln_ref, wout_ref,
                         x1_ref, h_ref, ltail_ref, ctail_ref,
                         zx_ext, glu_ext, h_carry):
    tt = MIX_TILE
    j = pl.program_id(1)

    @pl.when(j == 0)
    def _():
        zx_ext[0:LRU_TAIL_ROWS, :] = jnp.zeros((LRU_TAIL_ROWS, LRU_WIDTH), F32)
        glu_ext[0:CF_TAIL_ROWS, :] = jnp.zeros((CF_TAIL_ROWS, CF_WIDTH), F32)
        h_carry[...] = jnp.zeros((1, LRU_WIDTH), F32)

    x = x_ref[...]
    z = _bdot(_rms(x, gmix_ref[...]), win_ref[...])
    zx = z[:, :LRU_WIDTH]
    zg = z[:, LRU_WIDTH:2 * LRU_WIDTH]
    za = z[:, 2 * LRU_WIDTH:2 * LRU_WIDTH + CF_WIDTH]
    zb = z[:, 2 * LRU_WIDTH + CF_WIDTH:]

    zx_ext[LRU_TAIL_ROWS:LRU_TAIL_ROWS + tt, :] = zx
    off = LRU_TAIL_ROWS - (LRU_CONV - 1)
    xc = blc_ref[...]
    for k in range(LRU_CONV):
        xc = xc + wlc_ref[k:k + 1, :] * zx_ext[off + k:off + k + tt, :]
    a, u = _lru_coeffs(xc, wg_ref, bg_ref, lam_ref)
    sub = lax.broadcasted_iota(I32, (tt, LRU_WIDTH), 0) & (SUBLANES - 1)
    d = 1
    while d < SUBLANES:
        keep = sub >= d
        a_sh = jnp.where(keep, pltpu.roll(a, d, 0), 1.0)
        u_sh = jnp.where(keep, pltpu.roll(u, d, 0), 0.0)
        u = u + a * u_sh
        a = a * a_sh
        d *= 2
    h_last = h_carry[...]
    groups = []
    for r in range(0, tt, SUBLANES):
        hg = a[r:r + SUBLANES, :] * h_last + u[r:r + SUBLANES, :]
        groups.append(hg)
        h_last = hg[SUBLANES - 1:SUBLANES, :]
    hs = jnp.concatenate(groups, axis=0)
    h_carry[...] = h_last
    y_lru = hs * jax.nn.gelu(zg)

    glu = za * jax.nn.sigmoid(zb)
    glu_ext[CF_TAIL_ROWS:CF_TAIL_ROWS + tt, :] = glu
    off = CF_TAIL_ROWS - (CF_CONV - 1)
    gc = bcc_ref[...]
    for s in range(SUBLANES):
        rows = tt if s == 0 else tt + SUBLANES
        q = None
        for k in range(CF_CONV):
            if (off + k) % SUBLANES == s:
                base = off + k - s
                term = wcc_ref[k:k + 1, :] * glu_ext[base:base + rows, :]
                q = term if q is None else q + term
        gc = gc + (q if s == 0 else q[s:s + tt, :])
    y_cf = _cf_post(gc, gln_ref, bln_ref)

    x1_ref[...] = _mixer_out(x, y_lru, y_cf, wout_ref)

    h_ref[0] = h_last
    ltail_ref[0] = zx_ext[LRU_TAIL_ROWS + tt - (LRU_CONV - 1):LRU_TAIL_ROWS + tt, :]
    ctail_ref[0] = glu_ext[CF_TAIL_ROWS + tt - (CF_CONV - 1):CF_TAIL_ROWS + tt, :]
    zx_ext[0:LRU_TAIL_ROWS, :] = zx_ext[tt:tt + LRU_TAIL_ROWS, :]
    glu_ext[0:CF_TAIL_ROWS, :] = glu_ext[tt:tt + CF_TAIL_ROWS, :]


def _mixer_weight_specs():
    return [
        _full((1, D_MODEL)),
        _full((D_MODEL, IN_WIDTH)),
        _full((LRU_CONV, LRU_WIDTH)), _full((1, LRU_WIDTH)),
        _full((LRU_WIDTH, 2 * LRU_WIDTH)), _full((1, 2 * LRU_WIDTH)),
        _full((1, LRU_WIDTH)),
        _full((CF_CONV, CF_WIDTH)), _full((1, CF_WIDTH)),
        _full((1, CF_WIDTH)), _full((1, CF_WIDTH)),
        _full((D_MODEL, D_MODEL)),
    ]


def _mixer_prompt(x, mix_w, batch, seq):
    tt = MIX_TILE
    nt = seq // tt
    return pl.pallas_call(
        _mixer_prompt_kernel,
        grid=(batch, nt),
        in_specs=[pl.BlockSpec((tt, D_MODEL), lambda b, j: (b * nt + j, 0))] + _mixer_weight_specs(),
        out_specs=[
            pl.BlockSpec((tt, D_MODEL), lambda b, j: (b * nt + j, 0)),
            pl.BlockSpec((1, 1, LRU_WIDTH), lambda b, j: (b, 0, 0)),
            pl.BlockSpec((1, LRU_CONV - 1, LRU_WIDTH), lambda b, j: (b, 0, 0)),
            pl.BlockSpec((1, CF_CONV - 1, CF_WIDTH), lambda b, j: (b, 0, 0)),
        ],
        out_shape=[
            jax.ShapeDtypeStruct((batch * seq, D_MODEL), F32),
            jax.ShapeDtypeStruct((batch, 1, LRU_WIDTH), F32),
            jax.ShapeDtypeStruct((batch, LRU_CONV - 1, LRU_WIDTH), F32),
            jax.ShapeDtypeStruct((batch, CF_CONV - 1, CF_WIDTH), F32),
        ],
        scratch_shapes=[
            pltpu.VMEM((LRU_TAIL_ROWS + tt, LRU_WIDTH), F32),
            pltpu.VMEM((CF_TAIL_ROWS + tt, CF_WIDTH), F32),
            pltpu.VMEM((1, LRU_WIDTH), F32),
        ],
        compiler_params=_cparams(("arbitrary", "arbitrary")),
        name="mixer_prompt",
    )(x, *mix_w)


def _mixer_sample_kernel(x_ref, gmix_ref, win_ref, wlc_ref, blc_ref, wg_ref, bg_ref, lam_ref,
                         wcc_ref, bcc_ref, gln_ref, bln_ref, wout_ref,
                         h0_ref, lbuf_ref, cbuf_ref,
                         x1_ref, h_ref, zx_ref, glu_ref):
    x = x_ref[...]
    z = _bdot(_rms(x, gmix_ref[...]), win_ref[...])
    zx = z[:, :LRU_WIDTH]
    zg = z[:, LRU_WIDTH:2 * LRU_WIDTH]
    za = z[:, 2 * LRU_WIDTH:2 * LRU_WIDTH + CF_WIDTH]
    zb = z[:, 2 * LRU_WIDTH + CF_WIDTH:]

    xc = blc_ref[...] + wlc_ref[LRU_CONV - 1:LRU_CONV, :] * zx
    xc = xc + jnp.sum(lbuf_ref[...] * wlc_ref[0:LRU_CONV - 1, :][None], axis=1)
    a, u = _lru_coeffs(xc, wg_ref, bg_ref, lam_ref)
    h = a * h0_ref[...] + u
    y_lru = h * jax.nn.gelu(zg)

    glu = za * jax.nn.sigmoid(zb)
    gc = bcc_ref[...] + wcc_ref[CF_CONV - 1:CF_CONV, :] * glu
    gc = gc + jnp.sum(cbuf_ref[...] * wcc_ref[0:CF_CONV - 1, :][None], axis=1)
    y_cf = _cf_post(gc, gln_ref, bln_ref)

    x1_ref[...] = _mixer_out(x, y_lru, y_cf, wout_ref)
    h_ref[...] = h
    zx_ref[...] = zx
    glu_ref[...] = glu


def _mixer_sample(x, mix_w, h0, lbuf, cbuf):
    n = x.shape[0]
    g = SAMPLE_MIX_GROUP
    row = lambda w: pl.BlockSpec((g, w), lambda i: (i, 0))
    return pl.pallas_call(
        _mixer_sample_kernel,
        grid=(n // g,),
        in_specs=[row(D_MODEL)] + _mixer_weight_specs() + [
            row(LRU_WIDTH),
            pl.BlockSpec((g, LRU_CONV - 1, LRU_WIDTH), lambda i: (i, 0, 0)),
            pl.BlockSpec((g, CF_CONV - 1, CF_WIDTH), lambda i: (i, 0, 0)),
        ],
        out_specs=[row(D_MODEL), row(LRU_WIDTH), row(LRU_WIDTH), row(CF_WIDTH)],
        out_shape=[
            jax.ShapeDtypeStruct((n, D_MODEL), F32),
            jax.ShapeDtypeStruct((n, LRU_WIDTH), F32),
            jax.ShapeDtypeStruct((n, LRU_WIDTH), F32),
            jax.ShapeDtypeStruct((n, CF_WIDTH), F32),
        ],
        compiler_params=_cparams(("arbitrary",)),
        name="mixer_sample",
    )(x, *mix_w, h0, lbuf, cbuf)


def _memkv_kernel(mem_ref, gmem_ref, wk_ref, wv_ref, k_ref, v_ref, kb_ref, vb_ref):
    mn = _rms(mem_ref[...], gmem_ref[...]).astype(BF16)
    k = jnp.dot(mn, wk_ref[...], preferred_element_type=F32)
    v = jnp.dot(mn, wv_ref[...], preferred_element_type=F32)
    for h in range(XA_HEADS):
        sl = slice(h * XA_HEAD_DIM, (h + 1) * XA_HEAD_DIM)
        k_ref[0, :, h, :] = k[:, sl]
        v_ref[0, :, h, :] = v[:, sl]
    kb_ref[...] = k.astype(BF16)
    vb_ref[...] = v.astype(BF16)


def _memkv(mem, g_mem, wk, wv):
    n = mem.shape[0]
    t = N_MEM
    row = pl.BlockSpec((t, D_MODEL), lambda i: (i, 0))
    state = pl.BlockSpec((1, N_MEM, XA_HEADS, XA_HEAD_DIM), lambda i: (i, 0, 0, 0))
    state_shape = jax.ShapeDtypeStruct((n // t, N_MEM, XA_HEADS, XA_HEAD_DIM), F32)
    return pl.pallas_call(
        _memkv_kernel,
        grid=(n // t,),
        in_specs=[row, _full((1, D_MODEL)), _full((D_MODEL, D_MODEL)), _full((D_MODEL, D_MODEL))],
        out_specs=[state, state, row, row],
        out_shape=[state_shape] * 2 + [jax.ShapeDtypeStruct((n, D_MODEL), BF16)] * 2,
        compiler_params=_cparams(("arbitrary",)),
        name="memory_kv",
    )(mem, g_mem, wk, wv)


def _router(x2, gmoe_ref, wrt_ref, brt_ref, xn_ref, tope_ref, gates_ref):
    xn = _rms(x2, gmoe_ref[...])
    _store_row_tiles(xn_ref, xn)
    logits = lax.dot_general(wrt_ref[...], xn.astype(BF16), (((1,), (1,)), ((), ())),
                             preferred_element_type=F32) + brt_ref[...]
    e_iota = lax.broadcasted_iota(I32, logits.shape, 0)
    work = logits
    vals, idxs = [], []
    for _ in range(TOP_K):
        m = jnp.max(work, axis=0, keepdims=True)
        idx = jnp.min(jnp.where(work == m, e_iota, N_EXPERTS), axis=0, keepdims=True)
        vals.append(m)
        idxs.append(idx)
        work = jnp.where(e_iota == idx, -jnp.inf, work)
    ex = [jnp.exp(v - vals[0]) for v in vals]
    den = ex[0] + ex[1] + ex[2] + ex[3]
    tope_ref[...] = jnp.concatenate(idxs, axis=0)
    gates_ref[...] = jnp.concatenate([e / den for e in ex], axis=0)


def _router_specs():
    return [_full((1, D_MODEL)), _full((N_EXPERTS, D_MODEL)), _full((N_EXPERTS, 1))]


def _router_out(n, tile, index):
    specs = [
        pl.BlockSpec((tile, D_MODEL), lambda *a: (index(*a), 0)),
        pl.BlockSpec((tile * ROW_TILE, LANES), lambda *a: (index(*a), 0)),
        pl.BlockSpec((TOP_K, tile), lambda *a: (0, index(*a))),
        pl.BlockSpec((TOP_K, tile), lambda *a: (0, index(*a))),
    ]
    shapes = [
        jax.ShapeDtypeStruct((n, D_MODEL), F32),
        jax.ShapeDtypeStruct((n * ROW_TILE, LANES), F32),
        jax.ShapeDtypeStruct((TOP_K, n), I32),
        jax.ShapeDtypeStruct((TOP_K, n), F32),
    ]
    return specs, shapes


def _attn_prompt_kernel(x1_ref, k_ref, v_ref, gxa_ref, wq_ref, wo_ref, gmoe_ref, wrt_ref, brt_ref,
                        x2_ref, xn_ref, tope_ref, gates_ref):
    x1 = x1_ref[...]
    q = (_bdot(_rms(x1, gxa_ref[...]), wq_ref[...]) * (XA_HEAD_DIM ** -0.5)).astype(BF16)
    outs = []
    for h in range(XA_HEADS):
        sl = slice(h * XA_HEAD_DIM, (h + 1) * XA_HEAD_DIM)
        s = lax.dot_general(q[:, sl], k_ref[:, sl], (((1,), (1,)), ((), ())), preferred_element_type=F32)
        p = jnp.exp(s - jnp.max(s, axis=-1, keepdims=True))
        p = p / jnp.sum(p, axis=-1, keepdims=True)
        outs.append(_bdot(p, v_ref[:, sl]))
    o = jnp.concatenate(outs, axis=-1)
    x2 = x1 + _bdot(o, wo_ref[...])
    x2_ref[...] = x2
    _router(x2, gmoe_ref, wrt_ref, brt_ref, xn_ref, tope_ref, gates_ref)


def _attn_prompt(x1, k, v, gxa, wq, wo, router_w, batch, seq):
    tq = ATT_TILE
    nt = seq // tq
    out_specs, out_shapes = _router_out(batch * seq, tq, lambda b, j: b * nt + j)
    kv = pl.BlockSpec((N_MEM, D_MODEL), lambda b, j: (b, 0))
    return pl.pallas_call(
        _attn_prompt_kernel,
        grid=(batch, nt),
        in_specs=[pl.BlockSpec((tq, D_MODEL), lambda b, j: (b * nt + j, 0)), kv, kv,
                  _full((1, D_MODEL)), _full((D_MODEL, D_MODEL)), _full((D_MODEL, D_MODEL))] + _router_specs(),
        out_specs=out_specs,
        out_shape=out_shapes,
        compiler_params=_cparams(("arbitrary", "arbitrary")),
        name="attn_prompt",
    )(x1, k, v, gxa, wq, wo, *router_w)


def _qproj_kernel(x1_ref, gxa_ref, wq_ref, q_ref):
    q_ref[...] = _bdot(_rms(x1_ref[...], gxa_ref[...]), wq_ref[...]) * (XA_HEAD_DIM ** -0.5)


def _attn_sample_core_kernel(q_ref, k_ref, v_ref, o_ref):
    n = N_MEM * SUBLANES
    col_head = lax.broadcasted_iota(I32, (SUBLANES, n), 1) & (SUBLANES - 1)
    same_head = col_head == lax.broadcasted_iota(I32, (SUBLANES, n), 0)
    kv_pad = jnp.zeros((N_MEM, SUBLANES - XA_HEADS, XA_HEAD_DIM), F32)
    q_pad = jnp.zeros((SUBLANES - XA_HEADS, XA_HEAD_DIM), F32)
    for g in range(q_ref.shape[0]):
        k8 = jnp.concatenate([k_ref[g], kv_pad], axis=1).reshape(n, XA_HEAD_DIM).astype(BF16)
        v8 = jnp.concatenate([v_ref[g], kv_pad], axis=1).reshape(n, XA_HEAD_DIM).astype(BF16)
        q8 = jnp.concatenate([q_ref[g], q_pad], axis=0).astype(BF16)
        s = lax.dot_general(q8, k8, (((1,), (1,)), ((), ())), preferred_element_type=F32)
        m = jnp.max(jnp.where(same_head, s, -jnp.inf), axis=-1, keepdims=True)
        p = jnp.where(same_head, jnp.exp(s - m), 0.0)
        den = jnp.sum(p, axis=-1, keepdims=True)
        o8 = jnp.dot(p.astype(BF16), v8, preferred_element_type=F32)
        o_ref[g] = (o8 / den)[0:XA_HEADS, :]


def _oproj_router_kernel(x1_ref, o_ref, wo_ref, gmoe_ref, wrt_ref, brt_ref,
                         x2_ref, xn_ref, tope_ref, gates_ref):
    x2 = x1_ref[...] + _bdot(o_ref[...], wo_ref[...])
    x2_ref[...] = x2
    _router(x2, gmoe_ref, wrt_ref, brt_ref, xn_ref, tope_ref, gates_ref)


def _attn_sample(x1, k, v, gxa, wq, wo, router_w):
    n = x1.shape[0]
    g = SAMPLE_GROUP
    q = pl.pallas_call(
        _qproj_kernel,
        grid=(1,),
        in_specs=[_full((n, D_MODEL)), _full((1, D_MODEL)), _full((D_MODEL, D_MODEL))],
        out_specs=_full((n, D_MODEL)),
        out_shape=jax.ShapeDtypeStruct((n, D_MODEL), F32),
        compiler_params=_cparams(("arbitrary",)),
        name="qproj_sample",
    )(x1, gxa, wq)
    kv = pl.BlockSpec((g, N_MEM, XA_HEADS, XA_HEAD_DIM), lambda i: (i, 0, 0, 0))
    row = pl.BlockSpec((g, XA_HEADS, XA_HEAD_DIM), lambda i: (i, 0, 0))
    o = pl.pallas_call(
        _attn_sample_core_kernel,
        grid=(n // g,),
        in_specs=[row, kv, kv],
        out_specs=row,
        out_shape=jax.ShapeDtypeStruct((n, XA_HEADS, XA_HEAD_DIM), F32),
        compiler_params=_cparams(("arbitrary",)),
        name="attn_sample_core",
    )(q.reshape(n, XA_HEADS, XA_HEAD_DIM), k, v).reshape(n, D_MODEL)
    out_specs, out_shapes = _router_out(n, n, lambda i: 0)
    return pl.pallas_call(
        _oproj_router_kernel,
        grid=(1,),
        in_specs=[_full((n, D_MODEL)), _full((n, D_MODEL)), _full((D_MODEL, D_MODEL))] + _router_specs(),
        out_specs=out_specs,
        out_shape=out_shapes,
        compiler_params=_cparams(("arbitrary",)),
        name="oproj_router_sample",
    )(x1, o, wo, *router_w)


def _moe_num_blocks(n_assign):
    return pl.cdiv(n_assign, MOE_BLOCK) + N_EXPERTS


def _slots_kernel(tp_ref, ts_ref, destp_ref, dests_ref, blke_ref, nact_ref, cnt_ref, pstart_ref,
                  rankp, ranks):
    c = LANES
    r_i = lax.broadcasted_iota(I32, (c, c), 0)
    c_i = lax.broadcasted_iota(I32, (c, c), 1)
    upper = (r_i < c_i).astype(BF16)
    e_iota = lax.broadcasted_iota(I32, (N_EXPERTS, c), 0)

    def make_body(top_ref, rank_ref):
        def body(ci, carry):
            lo = pl.multiple_of(ci * c, c)
            top = top_ref[:, pl.ds(lo, c)]
            hit = [e_iota == top[k:k + 1, :] for k in range(TOP_K)]
            cnt = sum(m.astype(F32) for m in hit)
            base = carry + jnp.dot(cnt.astype(BF16), upper, preferred_element_type=F32)
            rank_ref[:, pl.ds(lo, c)] = jnp.concatenate(
                [jnp.sum(jnp.where(m, base, 0.0), axis=0, keepdims=True) for m in hit], axis=0)
            return carry + jnp.sum(cnt, axis=1, keepdims=True)
        return body

    counts = jnp.zeros((N_EXPERTS, 1), F32)
    counts = lax.fori_loop(0, tp_ref.shape[1] // c, make_body(tp_ref, rankp), counts)
    counts = lax.fori_loop(0, ts_ref.shape[1] // c, make_body(ts_ref, ranks), counts)

    padded = jnp.floor((counts + (MOE_BLOCK - 1)) / MOE_BLOCK) * MOE_BLOCK
    er = lax.broadcasted_iota(I32, (N_EXPERTS, N_EXPERTS), 0)
    ec = lax.broadcasted_iota(I32, (N_EXPERTS, N_EXPERTS), 1)
    padded_row = jnp.sum(jnp.where(er == ec, padded, 0.0), axis=0, keepdims=True)
    cum = jnp.sum(jnp.where(ec <= er, padded_row, 0.0), axis=1, keepdims=True)
    pstart = cum - padded
    cnt_ref[...] = jnp.sum(jnp.where(er == ec, counts, 0.0), axis=0, keepdims=True).astype(I32)
    pstart_ref[...] = jnp.sum(jnp.where(er == ec, pstart, 0.0), axis=0, keepdims=True).astype(I32)

    def dest(top_ref, rank_ref, out_ref):
        n = top_ref.shape[1]
        ei = lax.broadcasted_iota(I32, (N_EXPERTS, n), 0)
        rows = []
        for k in range(TOP_K):
            start = jnp.sum(jnp.where(ei == top_ref[k:k + 1, :], pstart, 0.0), axis=0, keepdims=True)
            rows.append(start + rank_ref[k:k + 1, :])
        out_ref[...] = jnp.concatenate(rows, axis=0).astype(I32)

    dest(tp_ref, rankp, destp_ref)
    dest(ts_ref, ranks, dests_ref)

    nb = blke_ref.shape[1]
    blk_lo = lax.broadcasted_iota(I32, (N_EXPERTS, nb), 1).astype(F32) * MOE_BLOCK
    blk = jnp.sum((cum <= blk_lo).astype(F32), axis=0, keepdims=True)
    blke_ref[...] = jnp.minimum(blk, N_EXPERTS - 1).astype(I32)
    nact_ref[...] = (cum[N_EXPERTS - 1:N_EXPERTS, :] / MOE_BLOCK).astype(I32)


def _slots(top_p, top_s):
    n_p, n_s = top_p.shape[1], top_s.shape[1]
    nb = _moe_num_blocks((n_p + n_s) * TOP_K)
    return pl.pallas_call(
        _slots_kernel,
        grid=(1,),
        in_specs=[_full((TOP_K, n_p)), _full((TOP_K, n_s))],
        out_specs=[_full((TOP_K, n_p)), _full((TOP_K, n_s)), _full((1, nb)), _full((1, 1)),
                   _full((1, N_EXPERTS)), _full((1, N_EXPERTS))],
        out_shape=[
            jax.ShapeDtypeStruct((TOP_K, n_p), I32),
            jax.ShapeDtypeStruct((TOP_K, n_s), I32),
            jax.ShapeDtypeStruct((1, nb), I32),
            jax.ShapeDtypeStruct((1, 1), I32),
            jax.ShapeDtypeStruct((1, N_EXPERTS), I32),
            jax.ShapeDtypeStruct((1, N_EXPERTS), I32),
        ],
        scratch_shapes=[pltpu.VMEM((TOP_K, n_p), F32), pltpu.VMEM((TOP_K, n_s), F32)],
        compiler_params=_cparams(("arbitrary",)),
        name="moe_slots",
    )(top_p, top_s)


def _for_rows(n_rows, fn):
    def body(i, c):
        for u in range(DMA_UNROLL):
            fn(i * DMA_UNROLL + u)
        return c
    lax.fori_loop(0, n_rows // DMA_UNROLL, body, 0)


def _dispatch_kernel(cnt_ref, pstart_ref, nact_ref, destp_ref, dests_ref, xnp_ref, xns_ref, xs_ref,
                     zeros, sem, zsem):
    i = pl.program_id(0)
    last = pl.num_programs(0) - 1

    def scatter(x_ref, dest_ref):
        n = x_ref.shape[0] // ROW_TILE

        def one(t):
            for k in range(TOP_K):
                _row_tile_copy(x_ref, t, xs_ref, dest_ref[t * TOP_K + k], sem).start(priority=k % 2)

        _for_rows(n, one)
        _row_tile_copy(xs_ref, 0, xs_ref, 0, sem, n * TOP_K).wait()

    @pl.when(i < last)
    def _():
        scatter(xnp_ref, destp_ref)

    @pl.when(i == last)
    def _():
        scatter(xns_ref, dests_ref)
        zeros[...] = jnp.zeros(zeros.shape, F32)

        def pad_copies(e, act):
            c = cnt_ref[e]
            npad = (-c) & (MOE_BLOCK - 1)
            base = pstart_ref[e] + c
            for bit in range(MOE_BLOCK.bit_length() - 1):
                size = 1 << bit

                @pl.when((npad & size) != 0)
                def _():
                    act(_row_tile_copy(zeros, 0, xs_ref, base + (npad & (size - 1)), zsem, size))

        def tail_copies(act):
            def body(j, c):
                act(_row_tile_copy(zeros, 0, xs_ref, j * MOE_BLOCK, zsem, MOE_BLOCK))
                return c
            lax.fori_loop(nact_ref[0], xs_ref.shape[0] // (MOE_BLOCK * ROW_TILE), body, 0)

        for act in (lambda d: d.start(), lambda d: d.wait()):
            lax.fori_loop(0, N_EXPERTS, lambda e, c, act=act: (pad_copies(e, act), c)[1], 0)
            tail_copies(act)


def _dispatch(cnt, pstart, nact, dest_p, dest_s, xn_p, xn_s, n_rows):
    n_p, n_s = xn_p.shape[0] // ROW_TILE, xn_s.shape[0] // ROW_TILE
    tile = DISPATCH_TILE
    nt = n_p // tile
    smem = pl.BlockSpec(memory_space=pltpu.SMEM)
    return pl.pallas_call(
        _dispatch_kernel,
        grid=(nt + 1,),
        in_specs=[
            smem, smem, smem,
            pl.BlockSpec((TOP_K * tile,), lambda i: (jnp.minimum(i, nt - 1),), memory_space=pltpu.SMEM),
            smem,
            pl.BlockSpec((tile * ROW_TILE, LANES), lambda i: (jnp.minimum(i, nt - 1), 0)),
            _full((n_s * ROW_TILE, LANES)),
        ],
        out_specs=pl.BlockSpec(memory_space=pl.ANY),
        out_shape=jax.ShapeDtypeStruct((n_rows * ROW_TILE, LANES), F32),
        scratch_shapes=[pltpu.VMEM((MOE_BLOCK * ROW_TILE, LANES), F32), pltpu.SemaphoreType.DMA,
                        pltpu.SemaphoreType.DMA],
        compiler_params=_cparams(("arbitrary",)),
        name="moe_dispatch",
    )(cnt, pstart, nact, dest_p, dest_s, xn_p, xn_s)


def _combine_kernel(dest_ref, dest_next_ref, gates_ref, x2_ref, gfin_ref, ys_ref, y_ref, bufs, sems):
    tile = x2_ref.shape[0]
    i = pl.program_id(0)
    slot = i % 2

    def gather(d_ref, b):
        def one(t):
            for k in range(TOP_K):
                _row_tile_copy(ys_ref, d_ref[t * TOP_K + k], bufs.at[b, k], t, sems.at[b]).start(priority=k % 2)
        _for_rows(tile, one)

    @pl.when(i == 0)
    def _():
        gather(dest_ref, 0)

    for b in range(2):
        @pl.when(slot == b)
        def _():
            pltpu.make_async_copy(bufs.at[b], bufs.at[b], sems.at[b]).wait()

            @pl.when(i + 1 < pl.num_programs(0))
            def _():
                gather(dest_next_ref, 1 - b)

    buf = bufs.at[slot]
    sumsq = jnp.zeros((tile, 1), F32)
    gate = [jnp.broadcast_to(gates_ref[:, k:k + 1], (tile, LANES)) for k in range(TOP_K)]
    for s in range(ROW_TILE):
        cols = slice(s * LANES, (s + 1) * LANES)
        acc = x2_ref[:, cols]
        for k in range(TOP_K):
            acc = acc + gate[k] * buf[k, pl.ds(s, tile, stride=ROW_TILE), :]
        y_ref[:, cols] = acc
        sumsq = sumsq + jnp.sum(acc * acc, axis=-1, keepdims=True)
    scale = lax.rsqrt(sumsq / D_MODEL + EPS)
    for s in range(ROW_TILE):
        cols = slice(s * LANES, (s + 1) * LANES)
        y_ref[:, cols] = y_ref[:, cols] * scale * gfin_ref[:, cols]


def _combine(dest, gates_t, x2, g_final, ys):
    n = x2.shape[0]
    tile = min(COMBINE_TILE, n)
    row = pl.BlockSpec((tile, D_MODEL), lambda i: (i, 0))
    nt = n // tile
    return pl.pallas_call(
        _combine_kernel,
        grid=(nt,),
        in_specs=[
            pl.BlockSpec((TOP_K * tile,), lambda i: (i,), memory_space=pltpu.SMEM),
            pl.BlockSpec((TOP_K * tile,), lambda i: (jnp.minimum(i + 1, nt - 1),), memory_space=pltpu.SMEM),
            pl.BlockSpec((tile, TOP_K), lambda i: (i, 0)),
            row,
            _full((1, D_MODEL)),
            pl.BlockSpec(memory_space=pl.ANY),
        ],
        out_specs=row,
        out_shape=jax.ShapeDtypeStruct((n, D_MODEL), F32),
        scratch_shapes=[pltpu.VMEM((2, TOP_K, tile * ROW_TILE, LANES), F32), pltpu.SemaphoreType.DMA((2,))],
        compiler_params=_cparams(("arbitrary",)),
        name="moe_combine",
    )(dest, dest, gates_t, x2, g_final, ys)


def _experts_kernel(blke_ref, nact_ref, xs_ref, w1_hbm, b1_ref, w2_hbm, b2_ref, ys_ref,
                    w1f, w2f, w1b, w2b, wsem, started):
    j = pl.program_id(0)
    nact = nact_ref[0]
    active = j < nact
    e = blke_ref[j]
    new_expert = jnp.logical_or(j == 0, e != blke_ref[jnp.maximum(j - 1, 0)])

    def fetch(expert, slot):
        return (pltpu.make_async_copy(w1_hbm.at[expert], w1f.at[slot], wsem.at[slot]),
                pltpu.make_async_copy(w2_hbm.at[expert], w2f.at[slot], wsem.at[slot]))

    @pl.when(j == 0)
    def _():
        started[0] = 0
        for d in fetch(e, 0):
            d.start()

    @pl.when(jnp.logical_and(active, new_expert))
    def _():
        n = started[0]
        last = blke_ref.shape[0] - 1
        jn = lax.while_loop(lambda jj: jnp.logical_and(jj < nact, blke_ref[jnp.minimum(jj, last)] == e),
                            lambda jj: jj + 1, j + 1)
        for b in range(2):
            @pl.when(n % 2 == b)
            def _():
                for d in fetch(e, b):
                    d.wait()

                @pl.when(jn < nact)
                def _():
                    for d in fetch(blke_ref[jnp.minimum(jn, last)], 1 - b):
                        d.start(priority=1)

                w1b[...] = w1f[b].astype(BF16)
                w2b[...] = w2f[b].astype(BF16)
        started[0] = n + 1

    @pl.when(active)
    def _():
        x = _load_row_tiles(xs_ref, MOE_BLOCK).astype(BF16)
        y = b2_ref[0]
        half = D_EXPERT // 2
        for c in (0, half):
            g = jnp.dot(x, w1b[:, c:c + half], preferred_element_type=F32) + b1_ref[0, :, c:c + half]
            u = (jnp.dot(x, w1b[:, D_EXPERT + c:D_EXPERT + c + half], preferred_element_type=F32)
                 + b1_ref[0, :, D_EXPERT + c:D_EXPERT + c + half])
            g = jnp.minimum(g, SWIGLU_LIMIT)
            u = jnp.clip(u, -SWIGLU_LIMIT, SWIGLU_LIMIT)
            h = (u + 1.0) * (g * jax.nn.sigmoid(SWIGLU_ALPHA * g))
            y = y + _bdot(h, w2b[c:c + half, :])
        _store_row_tiles(ys_ref, y)

    @pl.when(jnp.logical_not(active))
    def _():
        ys_ref[...] = jnp.zeros(ys_ref.shape, F32)


def _experts(blk_e, nact, xs, w1, b1, w2, b2):
    nb = xs.shape[0] // (MOE_BLOCK * ROW_TILE)
    blk = lambda j, be, na: (jnp.minimum(j, na[0] - 1), 0)
    exp = lambda j, be, na: (be[jnp.minimum(j, na[0] - 1)], 0, 0)
    rows = (MOE_BLOCK * ROW_TILE, LANES)
    return pl.pallas_call(
        _experts_kernel,
        grid_spec=pltpu.PrefetchScalarGridSpec(
            num_scalar_prefetch=2,
            grid=(nb,),
            in_specs=[
                pl.BlockSpec(rows, blk),
                pl.BlockSpec(memory_space=pl.ANY),
                pl.BlockSpec((1, 1, 2 * D_EXPERT), exp),
                pl.BlockSpec(memory_space=pl.ANY),
                pl.BlockSpec((1, 1, D_MODEL), exp),
            ],
            out_specs=pl.BlockSpec(rows, lambda j, be, na: (j, 0)),
            scratch_shapes=[
                pltpu.VMEM((2, D_MODEL, 2 * D_EXPERT), F32), pltpu.VMEM((2, D_EXPERT, D_MODEL), F32),
                pltpu.VMEM((D_MODEL, 2 * D_EXPERT), BF16), pltpu.VMEM((D_EXPERT, D_MODEL), BF16),
                pltpu.SemaphoreType.DMA((2,)), pltpu.SMEM((1,), I32),
            ],
        ),
        out_shape=jax.ShapeDtypeStruct(xs.shape, F32),
        compiler_params=_cparams(("arbitrary",)),
        name="moe_experts",
    )(blk_e, nact, xs, w1, b1, w2, b2)


def _block_diag(w):
    h, d, _ = w.shape
    eye = jnp.eye(h, dtype=w.dtype)
    return (eye[:, None, :, None] * w[:, :, None, :]).reshape(h * d, h * d)


def kernel(x_prompt, x_sample, state_lru_h, cache_lru_conv, cache_cf_conv, cache_mem_k, cache_mem_v,
           mem_prompt, g_mix, w_in, w_lru_conv, b_lru_conv, w_lru_a, b_lru_a, w_lru_x, b_lru_x,
           lru_lambda, w_cf_conv, b_cf_conv, g_cf_ln, b_cf_ln, w_out, g_xa, g_mem, w_q, w_k, w_v, w_o,
           g_moe, w_router, b_router, w_e1, b_e1, w_e2, b_e2, g_final):
    depth = g_mix.shape[0]
    assert depth == 1, "single-layer stack"
    batch, seq, _ = x_prompt.shape
    n_s = x_sample.shape[0]
    assert x_sample.shape[1] == 1
    n_p = batch * seq
    l = 0

    mix_w = (
        g_mix[l][None], w_in[l].astype(BF16), w_lru_conv[l], b_lru_conv[l][None],
        jnp.concatenate([_block_diag(w_lru_a[l]), _block_diag(w_lru_x[l])], axis=1).astype(BF16),
        jnp.concatenate([b_lru_a[l], b_lru_x[l]])[None], lru_lambda[l][None],
        w_cf_conv[l], b_cf_conv[l][None], g_cf_ln[l][None], b_cf_ln[l][None], w_out[l].astype(BF16),
    )
    router_w = (g_moe[l][None], w_router[l].T.astype(BF16), b_router[l][:, None])
    wq, wo = w_q[l].astype(BF16), w_o[l].astype(BF16)
    gxa = g_xa[l][None]

    xp = x_prompt.reshape(n_p, D_MODEL)
    xs_ = x_sample.reshape(n_s, D_MODEL)
    x1p, hp, lcp, ccp = _mixer_prompt(xp, mix_w, batch, seq)
    x1s, hs, zx_s, glu_s = _mixer_sample(xs_, mix_w, state_lru_h[l], cache_lru_conv[l], cache_cf_conv[l])

    mk, mv, mkb, mvb = _memkv(mem_prompt.reshape(batch * N_MEM, D_MODEL), g_mem[l][None],
                              w_k[l].astype(BF16), w_v[l].astype(BF16))
    x2p, xnp_, top_p, gates_p = _attn_prompt(x1p, mkb, mvb, gxa, wq, wo, router_w, batch, seq)
    x2s, xns, top_s, gates_s = _attn_sample(x1s, cache_mem_k[l], cache_mem_v[l], gxa, wq, wo, router_w)

    dest_p, dest_s, blk_e, nact, cnt, pstart = _slots(top_p, top_s)
    dest_p, dest_s = dest_p.T.reshape(-1), dest_s.T.reshape(-1)
    nb = _moe_num_blocks((n_p + n_s) * TOP_K)
    nact = nact.reshape(1)
    xs = _dispatch(cnt.reshape(N_EXPERTS), pstart.reshape(N_EXPERTS), nact, dest_p, dest_s, xnp_, xns,
                   nb * MOE_BLOCK)
    ys = _experts(blk_e.reshape(nb), nact, xs, w_e1[l], b_e1[l][:, None, :], w_e2[l], b_e2[l][:, None, :])
    gfin = g_final[None]
    y_p = _combine(dest_p, gates_p.T, x2p, gfin, ys)
    y_s = _combine(dest_s, gates_s.T, x2s, gfin, ys)

    lru_conv_s = jnp.concatenate([cache_lru_conv[l][:, 1:], zx_s[:, None]], axis=1)
    cf_conv_s = jnp.concatenate([cache_cf_conv[l][:, 1:], glu_s[:, None]], axis=1)
    return (
        y_p.reshape(batch, seq, D_MODEL),
        y_s.reshape(n_s, 1, D_MODEL),
        hp.reshape(depth, batch, LRU_WIDTH),
        lcp[None],
        ccp[None],
        mk[None],
        mv[None],
        hs[None],
        lru_conv_s[None],
        cf_conv_s[None],
    )
```

```python
import functools

import jax
import jax.numpy as jnp
from jax import lax
from jax.experimental import pallas as pl
from jax.experimental.pallas import tpu as pltpu

F32 = jnp.float32
BF16 = jnp.bfloat16
I32 = jnp.int32

D_MODEL = 1024
LRU_WIDTH = 512
CF_WIDTH = 512
LRU_HEADS = 8
LRU_CONV = 4
LRU_C = 8.0
CF_CONV = 31
IN_WIDTH = 2 * LRU_WIDTH + 2 * CF_WIDTH
N_MEM = 256
XA_HEADS = 4
XA_HEAD_DIM = D_MODEL // XA_HEADS
N_EXPERTS = 32
TOP_K = 4
D_EXPERT = D_MODEL
SWIGLU_LIMIT = 7.0
SWIGLU_ALPHA = 1.702
EPS = 1e-6

LANES = 128
SUBLANES = 8
ROW_TILE = D_MODEL // LANES
assert ROW_TILE == SUBLANES
VMEM_LIMIT = 56 * 1024 * 1024

MIX_TILE = 256
LRU_TAIL_ROWS = 8
CF_TAIL_ROWS = 32
ATT_TILE = 512
SAMPLE_GROUP = 4
SAMPLE_MIX_GROUP = 32
MOE_BLOCK = 256
DMA_UNROLL = 8
DISPATCH_TILE = 512
COMBINE_TILE = 256


def _cparams(sem):
    return pltpu.CompilerParams(dimension_semantics=sem, vmem_limit_bytes=VMEM_LIMIT)


def _full(shape):
    n = len(shape)
    return pl.BlockSpec(shape, lambda *_: (0,) * n)


def _rms(x, g):
    return x * lax.rsqrt(jnp.mean(x * x, axis=-1, keepdims=True) + EPS) * g


def _bdot(a, b):
    return jnp.dot(a.astype(BF16), b, preferred_element_type=F32)


def _store_row_tiles(ref, x):
    m = x.shape[0]
    for s in range(ROW_TILE):
        ref[pl.ds(s, m, stride=ROW_TILE), :] = x[:, s * LANES:(s + 1) * LANES]


def _load_row_tiles(ref, m):
    return jnp.concatenate([ref[pl.ds(s, m, stride=ROW_TILE), :] for s in range(ROW_TILE)], axis=-1)


def _row_tile_copy(src_ref, src_row, dst_ref, dst_row, sem, n_rows=1):
    def rows(ref, row):
        start = row * ROW_TILE
        if not isinstance(start, int):
            start = pl.multiple_of(start, ROW_TILE)
        return ref.at[pl.ds(start, n_rows * ROW_TILE)]
    return pltpu.make_async_copy(rows(src_ref, src_row), rows(dst_ref, dst_row), sem)


def _neg_expm1(y):
    u = jnp.exp(y)
    d = 1.0 - u
    return jnp.where(d == 0.0, -y, jnp.where(u == 0.0, 1.0, d * y / jnp.log(u)))


def _lru_coeffs(xc, wg_ref, bg_ref, lam_ref):
    gl = _bdot(xc, wg_ref[...]) + bg_ref[...]
    r = jax.nn.sigmoid(gl[:, :LRU_WIDTH])
    i = jax.nn.sigmoid(gl[:, LRU_WIDTH:])
    log_a = -LRU_C * r * jax.nn.softplus(-lam_ref[...])
    a = jnp.exp(log_a)
    u = jnp.sqrt(_neg_expm1(2.0 * log_a)) * (i * xc)
    return a, u


def _cf_post(gc, gln_ref, bln_ref):
    mu = jnp.mean(gc, axis=-1, keepdims=True)
    c = gc - mu
    y = c * lax.rsqrt(jnp.mean(c * c, axis=-1, keepdims=True) + EPS)
    return jax.nn.silu(y * gln_ref[...] + bln_ref[...])


def _mixer_out(x, y_lru, y_cf, wout_ref):
    y = _bdot(y_lru, wout_ref[:LRU_WIDTH, :]) + _bdot(y_cf, wout_ref[LRU_WIDTH:, :])
    return x + y


def _mixer_prompt_kernel(x_ref, gmix_ref, win_ref, wlc_ref, blc_ref, wg_ref, bg_ref, lam_ref,
                         wcc_ref, bcc_ref, gln_ref, bln_ref, wout_ref,
                         x1_ref, h_ref, ltail_ref, ctail_ref,
                         zx_ext, glu_ext, h_carry):
    tt = MIX_TILE
    j = pl.program_id(1)

    @pl.when(j == 0)
    def _():
        zx_ext[0:LRU_TAIL_ROWS, :] = jnp.zeros((LRU_TAIL_ROWS, LRU_WIDTH), F32)
        glu_ext[0:CF_TAIL_ROWS, :] = jnp.zeros((CF_TAIL_ROWS, CF_WIDTH), F32)
        h_carry[...] = jnp.zeros((1, LRU_WIDTH), F32)

    x = x_ref[...]
    z = _bdot(_rms(x, gmix_ref[...]), win_ref[...])
    zx = z[:, :LRU_WIDTH]
    zg = z[:, LRU_WIDTH:2 * LRU_WIDTH]
    za = z[:, 2 * LRU_WIDTH:2 * LRU_WIDTH + CF_WIDTH]
    zb = z[:, 2 * LRU_WIDTH + CF_WIDTH:]

    zx_ext[LRU_TAIL_ROWS:LRU_TAIL_ROWS + tt, :] = zx
    off = LRU_TAIL_ROWS - (LRU_CONV - 1)
    xc = blc_ref[...]
    for k in range(LRU_CONV):
        xc = xc + wlc_ref[k:k + 1, :] * zx_ext[off + k:off + k + tt, :]
    a, u = _lru_coeffs(xc, wg_ref, bg_ref, lam_ref)
    sub = lax.broadcasted_iota(I32, (tt, LRU_WIDTH), 0) & (SUBLANES - 1)
    d = 1
    while d < SUBLANES:
        keep = sub >= d
        a_sh = jnp.where(keep, pltpu.roll(a, d, 0), 1.0)
        u_sh = jnp.where(keep, pltpu.roll(u, d, 0), 0.0)
        u = u + a * u_sh
        a = a * a_sh
        d *= 2
    h_last = h_carry[...]
    groups = []
    for r in range(0, tt, SUBLANES):
        hg = a[r:r + SUBLANES, :] * h_last + u[r:r + SUBLANES, :]
        groups.append(hg)
        h_last = hg[SUBLANES - 1:SUBLANES, :]
    hs = jnp.concatenate(groups, axis=0)
    h_carry[...] = h_last
    y_lru = hs * jax.nn.gelu(zg)

    glu = za * jax.nn.sigmoid(zb)
    glu_ext[CF_TAIL_ROWS:CF_TAIL_ROWS + tt, :] = glu
    off = CF_TAIL_ROWS - (CF_CONV - 1)
    gc = bcc_ref[...]
    for s in range(SUBLANES):
        rows = tt if s == 0 else tt + SUBLANES
        q = None
        for k in range(CF_CONV):
            if (off + k) % SUBLANES == s:
                base = off + k - s
                term = wcc_ref[k:k + 1, :] * glu_ext[base:base + rows, :]
                q = term if q is None else q + term
        gc = gc + (q if s == 0 else q[s:s + tt, :])
    y_cf = _cf_post(gc, gln_ref, bln_ref)

    x1_ref[...] = _mixer_out(x, y_lru, y_cf, wout_ref)

    h_ref[0] = h_last
    ltail_ref[0] = zx_ext[LRU_TAIL_ROWS + tt - (LRU_CONV - 1):LRU_TAIL_ROWS + tt, :]
    ctail_ref[0] = glu_ext[CF_TAIL_ROWS + tt - (CF_CONV - 1):CF_TAIL_ROWS + tt, :]
    zx_ext[0:LRU_TAIL_ROWS, :] = zx_ext[tt:tt + LRU_TAIL_ROWS, :]
    glu_ext[0:CF_TAIL_ROWS, :] = glu_ext[tt:tt + CF_TAIL_ROWS, :]


def _mixer_weight_specs():
    return [
        _full((1, D_MODEL)),
        _full((D_MODEL, IN_WIDTH)),
        _full((LRU_CONV, LRU_WIDTH)), _full((1, LRU_WIDTH)),
        _full((LRU_WIDTH, 2 * LRU_WIDTH)), _full((1, 2 * LRU_WIDTH)),
        _full((1, LRU_WIDTH)),
        _full((CF_CONV, CF_WIDTH)), _full((1, CF_WIDTH)),
        _full((1, CF_WIDTH)), _full((1, CF_WIDTH)),
        _full((D_MODEL, D_MODEL)),
    ]


def _mixer_prompt(x, mix_w, batch, seq):
    tt = MIX_TILE
    nt = seq // tt
    return pl.pallas_call(
        _mixer_prompt_kernel,
        grid=(batch, nt),
        in_specs=[pl.BlockSpec((tt, D_MODEL), lambda b, j: (b * nt + j, 0))] + _mixer_weight_specs(),
        out_specs=[
            pl.BlockSpec((tt, D_MODEL), lambda b, j: (b * nt + j, 0)),
            pl.BlockSpec((1, 1, LRU_WIDTH), lambda b, j: (b, 0, 0)),
            pl.BlockSpec((1, LRU_CONV - 1, LRU_WIDTH), lambda b, j: (b, 0, 0)),
            pl.BlockSpec((1, CF_CONV - 1, CF_WIDTH), lambda b, j: (b, 0, 0)),
        ],
        out_shape=[
            jax.ShapeDtypeStruct((batch * seq, D_MODEL), F32),
            jax.ShapeDtypeStruct((batch, 1, LRU_WIDTH), F32),
            jax.ShapeDtypeStruct((batch, LRU_CONV - 1, LRU_WIDTH), F32),
            jax.ShapeDtypeStruct((batch, CF_CONV - 1, CF_WIDTH), F32),
        ],
        scratch_shapes=[
            pltpu.VMEM((LRU_TAIL_ROWS + tt, LRU_WIDTH), F32),
            pltpu.VMEM((CF_TAIL_ROWS + tt, CF_WIDTH), F32),
            pltpu.VMEM((1, LRU_WIDTH), F32),
        ],
        compiler_params=_cparams(("arbitrary", "arbitrary")),
        name="mixer_prompt",
    )(x, *mix_w)


def _mixer_sample_kernel(x_ref, gmix_ref, win_ref, wlc_ref, blc_ref, wg_ref, bg_ref, lam_ref,
                         wcc_ref, bcc_ref, gln_ref, bln_ref, wout_ref,
                         h0_ref, lbuf_ref, cbuf_ref,
                         x1_ref, h_ref, zx_ref, glu_ref):
    x = x_ref[...]
    z = _bdot(_rms(x, gmix_ref[...]), win_ref[...])
    zx = z[:, :LRU_WIDTH]
    zg = z[:, LRU_WIDTH:2 * LRU_WIDTH]
    za = z[:, 2 * LRU_WIDTH:2 * LRU_WIDTH + CF_WIDTH]
    zb = z[:, 2 * LRU_WIDTH + CF_WIDTH:]

    xc = blc_ref[...] + wlc_ref[LRU_CONV - 1:LRU_CONV, :] * zx
    xc = xc + jnp.sum(lbuf_ref[...] * wlc_ref[0:LRU_CONV - 1, :][None], axis=1)
    a, u = _lru_coeffs(xc, wg_ref, bg_ref, lam_ref)
    h = a * h0_ref[...] + u
    y_lru = h * jax.nn.gelu(zg)

    glu = za * jax.nn.sigmoid(zb)
    gc = bcc_ref[...] + wcc_ref[CF_CONV - 1:CF_CONV, :] * glu
    gc = gc + jnp.sum(cbuf_ref[...] * wcc_ref[0:CF_CONV - 1, :][None], axis=1)
    y_cf = _cf_post(gc, gln_ref, bln_ref)

    x1_ref[...] = _mixer_out(x, y_lru, y_cf, wout_ref)
    h_ref[...] = h
    zx_ref[...] = zx
    glu_ref[...] = glu


def _mixer_sample(x, mix_w, h0, lbuf, cbuf):
    n = x.shape[0]
    g = SAMPLE_MIX_GROUP
    row = lambda w: pl.BlockSpec((g, w), lambda i: (i, 0))
    return pl.pallas_call(
        _mixer_sample_kernel,
        grid=(n // g,),
        in_specs=[row(D_MODEL)] + _mixer_weight_specs() + [
            row(LRU_WIDTH),
            pl.BlockSpec((g, LRU_CONV - 1, LRU_WIDTH), lambda i: (i, 0, 0)),
            pl.BlockSpec((g, CF_CONV - 1, CF_WIDTH), lambda i: (i, 0, 0)),
        ],
        out_specs=[row(D_MODEL), row(LRU_WIDTH), row(LRU_WIDTH), row(CF_WIDTH)],
        out_shape=[
            jax.ShapeDtypeStruct((n, D_MODEL), F32),
            jax.ShapeDtypeStruct((n, LRU_WIDTH), F32),
            jax.ShapeDtypeStruct((n, LRU_WIDTH), F32),
            jax.ShapeDtypeStruct((n, CF_WIDTH), F32),
        ],
        compiler_params=_cparams(("arbitrary",)),
        name="mixer_sample",
    )(x, *mix_w, h0, lbuf, cbuf)


def _memkv_kernel(mem_ref, gmem_ref, wk_ref, wv_ref, k_ref, v_ref, kb_ref, vb_ref):
    mn = _rms(mem_ref[...], gmem_ref[...]).astype(BF16)
    k = jnp.dot(mn, wk_ref[...], preferred_element_type=F32)
    v = jnp.dot(mn, wv_ref[...], preferred_element_type=F32)
    for h in range(XA_HEADS):
        sl = slice(h * XA_HEAD_DIM, (h + 1) * XA_HEAD_DIM)
        k_ref[0, :, h, :] = k[:, sl]
        v_ref[0, :, h, :] = v[:, sl]
    kb_ref[...] = k.astype(BF16)
    vb_ref[...] = v.astype(BF16)


def _memkv(mem, g_mem, wk, wv):
    n = mem.shape[0]
    t = N_MEM
    row = pl.BlockSpec((t, D_MODEL), lambda i: (i, 0))
    state = pl.BlockSpec((1, N_MEM, XA_HEADS, XA_HEAD_DIM), lambda i: (i, 0, 0, 0))
    state_shape = jax.ShapeDtypeStruct((n // t, N_MEM, XA_HEADS, XA_HEAD_DIM), F32)
    return pl.pallas_call(
        _memkv_kernel,
        grid=(n // t,),
        in_specs=[row, _full((1, D_MODEL)), _full((D_MODEL, D_MODEL)), _full((D_MODEL, D_MODEL))],
        out_specs=[state, state, row, row],
        out_shape=[state_shape] * 2 + [jax.ShapeDtypeStruct((n, D_MODEL), BF16)] * 2,
        compiler_params=_cparams(("arbitrary",)),
        name="memory_kv",
    )(mem, g_mem, wk, wv)


def _router(x2, gmoe_ref, wrt_ref, brt_ref, xn_ref, tope_ref, gates_ref):
    xn = _rms(x2, gmoe_ref[...])
    _store_row_tiles(xn_ref, xn)
    logits = lax.dot_general(wrt_ref[...], xn.astype(BF16), (((1,), (1,)), ((), ())),
                             preferred_element_type=F32) + brt_ref[...]
    e_iota = lax.broadcasted_iota(I32, logits.shape, 0)
    work = logits
    vals, idxs = [], []
    for _ in range(TOP_K):
        m = jnp.max(work, axis=0, keepdims=True)
        idx = jnp.min(jnp.where(work == m, e_iota, N_EXPERTS), axis=0, keepdims=True)
        vals.append(m)
        idxs.append(idx)
        work = jnp.where(e_iota == idx, -jnp.inf, work)
    ex = [jnp.exp(v - vals[0]) for v in vals]
    den = ex[0] + ex[1] + ex[2] + ex[3]
    tope_ref[...] = jnp.concatenate(idxs, axis=0)
    gates_ref[...] = jnp.concatenate([e / den for e in ex], axis=0)


def _router_specs():
    return [_full((1, D_MODEL)), _full((N_EXPERTS, D_MODEL)), _full((N_EXPERTS, 1))]


def _router_out(n, tile, index):
    specs = [
        pl.BlockSpec((tile, D_MODEL), lambda *a: (index(*a), 0)),
        pl.BlockSpec((tile * ROW_TILE, LANES), lambda *a: (index(*a), 0)),
        pl.BlockSpec((TOP_K, tile), lambda *a: (0, index(*a))),
        pl.BlockSpec((TOP_K, tile), lambda *a: (0, index(*a))),
    ]
    shapes = [
        jax.ShapeDtypeStruct((n, D_MODEL), F32),
        jax.ShapeDtypeStruct((n * ROW_TILE, LANES), F32),
        jax.ShapeDtypeStruct((TOP_K, n), I32),
        jax.ShapeDtypeStruct((TOP_K, n), F32),
    ]
    return specs, shapes


def _attn_prompt_kernel(x1_ref, k_ref, v_ref, xns_ref, gxa_ref, wq_ref, wo_ref, gmoe_ref, wrt_ref, brt_ref,
                        x2_ref, xn_ref, tope_ref, gates_ref):
    step = pl.program_id(0)
    last = pl.num_programs(0) - 1

    @pl.when(step < last)
    def _():
        x1 = x1_ref[...]
        q = (_bdot(_rms(x1, gxa_ref[...]), wq_ref[...]) * (XA_HEAD_DIM ** -0.5)).astype(BF16)
        outs = []
        for h in range(XA_HEADS):
            sl = slice(h * XA_HEAD_DIM, (h + 1) * XA_HEAD_DIM)
            s = lax.dot_general(q[:, sl], k_ref[:, sl], (((1,), (1,)), ((), ())), preferred_element_type=F32)
            p = jnp.exp(s - jnp.max(s, axis=-1, keepdims=True))
            p = p / jnp.sum(p, axis=-1, keepdims=True)
            outs.append(_bdot(p, v_ref[:, sl]))
        o = jnp.concatenate(outs, axis=-1)
        x2 = x1 + _bdot(o, wo_ref[...])
        x2_ref[...] = x2
        _router(x2, gmoe_ref, wrt_ref, brt_ref, xn_ref, tope_ref, gates_ref)

    @pl.when(step == last)
    def _():
        n = xns_ref.shape[0]
        xn_ref[0:n, :] = xns_ref[...]
        xn_ref[n:, :] = jnp.zeros((xn_ref.shape[0] - n, LANES), F32)


def _attn_prompt(x1, k, v, xn_s, gxa, wq, wo, router_w, batch, seq):
    tq = ATT_TILE
    nt = seq // tq
    nblk = batch * nt
    n = batch * seq
    blk = lambda s: jnp.minimum(s, nblk - 1)
    kv = pl.BlockSpec((N_MEM, D_MODEL), lambda s: (jnp.minimum(s // nt, batch - 1), 0))
    return pl.pallas_call(
        _attn_prompt_kernel,
        grid=(nblk + 1,),
        in_specs=[pl.BlockSpec((tq, D_MODEL), lambda s: (blk(s), 0)), kv, kv, _full(xn_s.shape),
                  _full((1, D_MODEL)), _full((D_MODEL, D_MODEL)), _full((D_MODEL, D_MODEL))] + _router_specs(),
        out_specs=[
            pl.BlockSpec((tq, D_MODEL), lambda s: (blk(s), 0)),
            pl.BlockSpec((tq * ROW_TILE, LANES), lambda s: (s, 0)),
            pl.BlockSpec((TOP_K, tq), lambda s: (0, blk(s))),
            pl.BlockSpec((TOP_K, tq), lambda s: (0, blk(s))),
        ],
        out_shape=[
            jax.ShapeDtypeStruct((n, D_MODEL), F32),
            jax.ShapeDtypeStruct(((n + tq) * ROW_TILE, LANES), F32),
            jax.ShapeDtypeStruct((TOP_K, n), I32),
            jax.ShapeDtypeStruct((TOP_K, n), F32),
        ],
        compiler_params=_cparams(("arbitrary",)),
        name="attn_prompt",
    )(x1, k, v, xn_s, gxa, wq, wo, *router_w)


def _qproj_kernel(x1_ref, gxa_ref, wq_ref, q_ref):
    q_ref[...] = _bdot(_rms(x1_ref[...], gxa_ref[...]), wq_ref[...]) * (XA_HEAD_DIM ** -0.5)


def _attn_sample_core_kernel(q_ref, k_ref, v_ref, o_ref):
    n = N_MEM * SUBLANES
    col_head = lax.broadcasted_iota(I32, (SUBLANES, n), 1) & (SUBLANES - 1)
    same_head = col_head == lax.broadcasted_iota(I32, (SUBLANES, n), 0)
    kv_pad = jnp.zeros((N_MEM, SUBLANES - XA_HEADS, XA_HEAD_DIM), F32)
    q_pad = jnp.zeros((SUBLANES - XA_HEADS, XA_HEAD_DIM), F32)
    for g in range(q_ref.shape[0]):
        k8 = jnp.concatenate([k_ref[g], kv_pad], axis=1).reshape(n, XA_HEAD_DIM).astype(BF16)
        v8 = jnp.concatenate([v_ref[g], kv_pad], axis=1).reshape(n, XA_HEAD_DIM).astype(BF16)
        q8 = jnp.concatenate([q_ref[g], q_pad], axis=0).astype(BF16)
        s = lax.dot_general(q8, k8, (((1,), (1,)), ((), ())), preferred_element_type=F32)
        m = jnp.max(jnp.where(same_head, s, -jnp.inf), axis=-1, keepdims=True)
        p = jnp.where(same_head, jnp.exp(s - m), 0.0)
        den = jnp.sum(p, axis=-1, keepdims=True)
        o8 = jnp.dot(p.astype(BF16), v8, preferred_element_type=F32)
        o_ref[g] = (o8 / den)[0:XA_HEADS, :]


def _oproj_router_kernel(x1_ref, o_ref, wo_ref, gmoe_ref, wrt_ref, brt_ref,
                         x2_ref, xn_ref, tope_ref, gates_ref):
    x2 = x1_ref[...] + _bdot(o_ref[...], wo_ref[...])
    x2_ref[...] = x2
    _router(x2, gmoe_ref, wrt_ref, brt_ref, xn_ref, tope_ref, gates_ref)


def _attn_sample(x1, k, v, gxa, wq, wo, router_w):
    n = x1.shape[0]
    g = SAMPLE_GROUP
    q = pl.pallas_call(
        _qproj_kernel,
        grid=(1,),
        in_specs=[_full((n, D_MODEL)), _full((1, D_MODEL)), _full((D_MODEL, D_MODEL))],
        out_specs=_full((n, D_MODEL)),
        out_shape=jax.ShapeDtypeStruct((n, D_MODEL), F32),
        compiler_params=_cparams(("arbitrary",)),
        name="qproj_sample",
    )(x1, gxa, wq)
    kv = pl.BlockSpec((g, N_MEM, XA_HEADS, XA_HEAD_DIM), lambda i: (i, 0, 0, 0))
    row = pl.BlockSpec((g, XA_HEADS, XA_HEAD_DIM), lambda i: (i, 0, 0))
    o = pl.pallas_call(
        _attn_sample_core_kernel,
        grid=(n // g,),
        in_specs=[row, kv, kv],
        out_specs=row,
        out_shape=jax.ShapeDtypeStruct((n, XA_HEADS, XA_HEAD_DIM), F32),
        compiler_params=_cparams(("arbitrary",)),
        name="attn_sample_core",
    )(q.reshape(n, XA_HEADS, XA_HEAD_DIM), k, v).reshape(n, D_MODEL)
    out_specs, out_shapes = _router_out(n, n, lambda i: 0)
    return pl.pallas_call(
        _oproj_router_kernel,
        grid=(1,),
        in_specs=[_full((n, D_MODEL)), _full((n, D_MODEL)), _full((D_MODEL, D_MODEL))] + _router_specs(),
        out_specs=out_specs,
        out_shape=out_shapes,
        compiler_params=_cparams(("arbitrary",)),
        name="oproj_router_sample",
    )(x1, o, wo, *router_w)


def _moe_num_blocks(n_assign):
    return pl.cdiv(n_assign, MOE_BLOCK) + N_EXPERTS


def _slots_kernel(tp_ref, ts_ref, destp_ref, dests_ref, blke_ref, nact_ref, cnt_ref, pstart_ref,
                  rankp, ranks):
    c = LANES
    r_i = lax.broadcasted_iota(I32, (c, c), 0)
    c_i = lax.broadcasted_iota(I32, (c, c), 1)
    upper = (r_i < c_i).astype(BF16)
    e_iota = lax.broadcasted_iota(I32, (N_EXPERTS, c), 0)

    def make_body(top_ref, rank_ref):
        def body(ci, carry):
            lo = pl.multiple_of(ci * c, c)
            top = top_ref[:, pl.ds(lo, c)]
            hit = [e_iota == top[k:k + 1, :] for k in range(TOP_K)]
            cnt = sum(m.astype(F32) for m in hit)
            base = carry + jnp.dot(cnt.astype(BF16), upper, preferred_element_type=F32)
            rank_ref[:, pl.ds(lo, c)] = jnp.concatenate(
                [jnp.sum(jnp.where(m, base, 0.0), axis=0, keepdims=True) for m in hit], axis=0)
            return carry + jnp.sum(cnt, axis=1, keepdims=True)
        return body

    counts = jnp.zeros((N_EXPERTS, 1), F32)
    counts = lax.fori_loop(0, tp_ref.shape[1] // c, make_body(tp_ref, rankp), counts)
    counts = lax.fori_loop(0, ts_ref.shape[1] // c, make_body(ts_ref, ranks), counts)

    padded = jnp.floor((counts + (MOE_BLOCK - 1)) / MOE_BLOCK) * MOE_BLOCK
    er = lax.broadcasted_iota(I32, (N_EXPERTS, N_EXPERTS), 0)
    ec = lax.broadcasted_iota(I32, (N_EXPERTS, N_EXPERTS), 1)
    padded_row = jnp.sum(jnp.where(er == ec, padded, 0.0), axis=0, keepdims=True)
    cum = jnp.sum(jnp.where(ec <= er, padded_row, 0.0), axis=1, keepdims=True)
    pstart = cum - padded
    cnt_ref[...] = jnp.sum(jnp.where(er == ec, counts, 0.0), axis=0, keepdims=True).astype(I32)
    pstart_ref[...] = jnp.sum(jnp.where(er == ec, pstart, 0.0), axis=0, keepdims=True).astype(I32)

    def dest(top_ref, rank_ref, out_ref):
        n = top_ref.shape[1]
        ei = lax.broadcasted_iota(I32, (N_EXPERTS, n), 0)
        rows = []
        for k in range(TOP_K):
            start = jnp.sum(jnp.where(ei == top_ref[k:k + 1, :], pstart, 0.0), axis=0, keepdims=True)
            rows.append(start + rank_ref[k:k + 1, :])
        out_ref[...] = jnp.concatenate(rows, axis=0).astype(I32)

    dest(tp_ref, rankp, destp_ref)
    dest(ts_ref, ranks, dests_ref)

    nb = blke_ref.shape[1]
    blk_lo = lax.broadcasted_iota(I32, (N_EXPERTS, nb), 1).astype(F32) * MOE_BLOCK
    blk = jnp.sum((cum <= blk_lo).astype(F32), axis=0, keepdims=True)
    blke_ref[...] = jnp.minimum(blk, N_EXPERTS - 1).astype(I32)
    nact_ref[...] = (cum[N_EXPERTS - 1:N_EXPERTS, :] / MOE_BLOCK).astype(I32)


def _slots(top_p, top_s):
    n_p, n_s = top_p.shape[1], top_s.shape[1]
    nb = _moe_num_blocks((n_p + n_s) * TOP_K)
    return pl.pallas_call(
        _slots_kernel,
        grid=(1,),
        in_specs=[_full((TOP_K, n_p)), _full((TOP_K, n_s))],
        out_specs=[_full((TOP_K, n_p)), _full((TOP_K, n_s)), _full((1, nb)), _full((1, 1)),
                   _full((1, N_EXPERTS)), _full((1, N_EXPERTS))],
        out_shape=[
            jax.ShapeDtypeStruct((TOP_K, n_p), I32),
            jax.ShapeDtypeStruct((TOP_K, n_s), I32),
            jax.ShapeDtypeStruct((1, nb), I32),
            jax.ShapeDtypeStruct((1, 1), I32),
            jax.ShapeDtypeStruct((1, N_EXPERTS), I32),
            jax.ShapeDtypeStruct((1, N_EXPERTS), I32),
        ],
        scratch_shapes=[pltpu.VMEM((TOP_K, n_p), F32), pltpu.VMEM((TOP_K, n_s), F32)],
        compiler_params=_cparams(("arbitrary",)),
        name="moe_slots",
    )(top_p, top_s)


SRC_BITS = 15
U32 = jnp.uint32


def _pack_rows(dst, src):
    return (jnp.asarray(dst, I32).astype(U32) << SRC_BITS) | jnp.asarray(src, I32).astype(U32)


def _src_row(entry):
    return (entry & ((1 << SRC_BITS) - 1)).astype(I32)


def _dst_row(entry):
    return (entry >> SRC_BITS).astype(I32)


def _invert_kernel(cnt_ref, pstart_ref, nact_ref, destp_ref, dests_ref, rows_ref, *, n_tokens, k_stride):
    assert k_stride < (1 << SRC_BITS) and TOP_K * k_stride + 2 * MOE_BLOCK < (1 << (32 - SRC_BITS))
    i = pl.program_id(0)
    last = pl.num_programs(0) - 1
    spare_base = TOP_K * k_stride

    def fill(dest_ref, first_token):
        n = dest_ref.shape[0] // TOP_K

        def body(c, carry):
            for u in range(DMA_UNROLL):
                t = first_token + c * DMA_UNROLL + u
                for k in range(TOP_K):
                    rows_ref[dest_ref[(c * DMA_UNROLL + u) * TOP_K + k]] = _pack_rows(k * k_stride + t, t)
            return carry

        lax.fori_loop(0, n // DMA_UNROLL, body, 0)

    def clear(lo, hi):
        def body(s, carry):
            spare = spare_base + (s & (2 * MOE_BLOCK - 1))
            rows_ref[s] = _pack_rows(spare, n_tokens)
            return carry
        lax.fori_loop(lo, hi, body, 0)

    @pl.when(i < last)
    def _():
        fill(destp_ref, i * (destp_ref.shape[0] // TOP_K))

    @pl.when(i == last)
    def _():
        fill(dests_ref, last * (destp_ref.shape[0] // TOP_K))

        def pads(e, carry):
            c = cnt_ref[e]
            base = pstart_ref[e] + c
            clear(base, base + ((-c) & (MOE_BLOCK - 1)))
            return carry

        lax.fori_loop(0, N_EXPERTS, pads, 0)
        clear(nact_ref[0] * MOE_BLOCK, rows_ref.shape[0])


def _invert(cnt, pstart, nact, dest_p, dest_s, n_slots, n_tokens, k_stride):
    tile = DISPATCH_TILE
    nt = dest_p.shape[0] // (tile * TOP_K)
    smem = pl.BlockSpec(memory_space=pltpu.SMEM)
    return pl.pallas_call(
        functools.partial(_invert_kernel, n_tokens=n_tokens, k_stride=k_stride),
        grid=(nt + 1,),
        in_specs=[smem, smem, smem,
                  pl.BlockSpec((TOP_K * tile,), lambda i: (jnp.minimum(i, nt - 1),), memory_space=pltpu.SMEM),
                  smem],
        out_specs=smem,
        out_shape=jax.ShapeDtypeStruct((n_slots,), U32),
        compiler_params=_cparams(("arbitrary",)),
        name="moe_invert",
    )(cnt, pstart, nact, dest_p, dest_s)


def _combine_kernel(gates_ref, x2_ref, gfin_ref, y0_ref, y1_ref, y2_ref, y3_ref, y_ref):
    tile = x2_ref.shape[0]
    rows = (y0_ref, y1_ref, y2_ref, y3_ref)
    sumsq = jnp.zeros((tile, 1), F32)
    gate = [jnp.broadcast_to(gates_ref[:, k:k + 1], (tile, LANES)) for k in range(TOP_K)]
    for s in range(ROW_TILE):
        cols = slice(s * LANES, (s + 1) * LANES)
        acc = x2_ref[:, cols]
        for k in range(TOP_K):
            acc = acc + gate[k] * rows[k][pl.ds(s, tile, stride=ROW_TILE), :]
        y_ref[:, cols] = acc
        sumsq = sumsq + jnp.sum(acc * acc, axis=-1, keepdims=True)
    scale = lax.rsqrt(sumsq / D_MODEL + EPS)
    for s in range(ROW_TILE):
        cols = slice(s * LANES, (s + 1) * LANES)
        y_ref[:, cols] = y_ref[:, cols] * scale * gfin_ref[:, cols]


def _combine(gates_t, x2, g_final, y4, first_token, k_stride):
    n = x2.shape[0]
    tile = min(COMBINE_TILE, n)
    row = pl.BlockSpec((tile, D_MODEL), lambda i: (i, 0))
    y4_specs = [pl.BlockSpec((tile * ROW_TILE, LANES),
                             lambda i, k=k: ((k * k_stride + first_token) // tile + i, 0)) for k in range(TOP_K)]
    return pl.pallas_call(
        _combine_kernel,
        grid=(n // tile,),
        in_specs=[pl.BlockSpec((tile, TOP_K), lambda i: (i, 0)), row, _full((1, D_MODEL))] + y4_specs,
        out_specs=row,
        out_shape=jax.ShapeDtypeStruct((n, D_MODEL), F32),
        compiler_params=_cparams(("arbitrary",)),
        name="moe_combine",
    )(gates_t, x2, g_final, y4, y4, y4, y4)


def _experts_kernel(blke_ref, nact_ref, rows_next_ref, rows_prev_ref, rows_ref, xn_hbm, w1_hbm, b1_ref, w2_hbm,
                    b2_ref, y4_hbm, xg0, xg1, yo0, yo1, zeros, w1f, w2f, w1b, w2b, gsem, ssem, wsem, zsem, started,
                    *, n_tokens, k_stride):
    j = pl.program_id(0)
    nact = nact_ref[0]
    active = j < nact
    slot = j % 2
    e = blke_ref[j]
    new_expert = jnp.logical_or(j == 0, e != blke_ref[jnp.maximum(j - 1, 0)])
    spare_base = TOP_K * k_stride
    xg = (xg0, xg1)
    yo = (yo0, yo1)

    def gather(r_ref, r, b):
        return _row_tile_copy(xn_hbm, _src_row(r_ref[0, 0, r]), xg[b], r, gsem.at[b])

    def scatter(r_ref, r, b):
        return _row_tile_copy(yo[b], r, y4_hbm, _dst_row(r_ref[0, 0, r]), ssem.at[b])

    def wait_rows(buf, sem, b):
        pltpu.make_async_copy(buf[b], buf[b], sem.at[b]).wait()

    def fetch(expert, b):
        return (pltpu.make_async_copy(w1_hbm.at[expert], w1f.at[b], wsem.at[b]),
                pltpu.make_async_copy(w2_hbm.at[expert], w2f.at[b], wsem.at[b]))

    @pl.when(j == 0)
    def _():
        started[0] = 0
        for d in fetch(e, 0):
            d.start()
        zeros[...] = jnp.zeros(zeros.shape, F32)
        yo1[...] = jnp.zeros(yo1.shape, F32)
        fills = []
        for k in range(TOP_K):
            lo, hi = k * k_stride + n_tokens, (k + 1) * k_stride
            while lo < hi:
                fills.append((lo, min(MOE_BLOCK, hi - lo)))
                lo += MOE_BLOCK
        fills += [(spare_base, MOE_BLOCK), (spare_base + MOE_BLOCK, MOE_BLOCK)]
        for act in ("start", "wait"):
            for lo, n in fills:
                getattr(_row_tile_copy(zeros, 0, y4_hbm, lo, zsem, n), act)()
        lax.fori_loop(0, MOE_BLOCK, lambda r, c: (gather(rows_ref, r, 0).start(), c)[1], 0)

    @pl.when(jnp.logical_and(active, new_expert))
    def _():
        n = started[0]
        last = blke_ref.shape[0] - 1
        jn = lax.while_loop(lambda jj: jnp.logical_and(jj < nact, blke_ref[jnp.minimum(jj, last)] == e),
                            lambda jj: jj + 1, j + 1)
        for b in range(2):
            @pl.when(n % 2 == b)
            def _():
                for d in fetch(e, b):
                    d.wait()

                @pl.when(jn < nact)
                def _():
                    for d in fetch(blke_ref[jnp.minimum(jn, last)], 1 - b):
                        d.start(priority=1)

                w1b[...] = w1f[b].astype(BF16)
                w2b[...] = w2f[b].astype(BF16)
        started[0] = n + 1

    for b in range(2):
        @pl.when(jnp.logical_and(active, slot == b))
        def _():
            wait_rows(xg, gsem, b)

            @pl.when(j > 0)
            def _():
                wait_rows(yo, ssem, b)

            for r in range(MOE_BLOCK):
                gather(rows_next_ref, r, 1 - b).start()
                scatter(rows_prev_ref, r, 1 - b).start(priority=1)
            x = _load_row_tiles(xg[b], MOE_BLOCK).astype(BF16)
            gu = jnp.dot(x, w1b[...], preferred_element_type=F32) + b1_ref[0]
            g = jnp.minimum(gu[:, :D_EXPERT], SWIGLU_LIMIT)
            u = jnp.clip(gu[:, D_EXPERT:], -SWIGLU_LIMIT, SWIGLU_LIMIT)
            h = (u + 1.0) * (g * jax.nn.sigmoid(SWIGLU_ALPHA * g))
            _store_row_tiles(yo[b], _bdot(h, w2b[...]) + b2_ref[0])

            @pl.when(j == nact - 1)
            def _():
                lax.fori_loop(0, MOE_BLOCK, lambda r, c: (scatter(rows_ref, r, b).start(priority=1), c)[1], 0)
                wait_rows(yo, ssem, 1 - b)
                wait_rows(yo, ssem, b)
                wait_rows(xg, gsem, 1 - b)


def _experts(blk_e, nact, rows, xn, w1, b1, w2, b2, n_tokens):
    nb = rows.shape[0] - 1
    k_stride = xn.shape[0] // ROW_TILE
    exp = lambda j, be, na: (be[jnp.minimum(j, na[0] - 1)], 0, 0)
    rows_spec = lambda f: pl.BlockSpec((1, 1, MOE_BLOCK), lambda j, be, na: (f(j), 0, 0), memory_space=pltpu.SMEM)
    block = (MOE_BLOCK * ROW_TILE, LANES)
    any_ = pl.BlockSpec(memory_space=pl.ANY)
    return pl.pallas_call(
        functools.partial(_experts_kernel, n_tokens=n_tokens, k_stride=k_stride),
        grid_spec=pltpu.PrefetchScalarGridSpec(
            num_scalar_prefetch=2,
            grid=(nb,),
            in_specs=[
                rows_spec(lambda j: j + 1),
                rows_spec(lambda j: jnp.where(j == 0, nb, j - 1)),
                rows_spec(lambda j: j),
                any_,
                any_,
                pl.BlockSpec((1, 1, 2 * D_EXPERT), exp),
                any_,
                pl.BlockSpec((1, 1, D_MODEL), exp),
            ],
            out_specs=any_,
            scratch_shapes=[
                pltpu.VMEM(block, F32), pltpu.VMEM(block, F32), pltpu.VMEM(block, F32), pltpu.VMEM(block, F32),
                pltpu.VMEM(block, F32),
                pltpu.VMEM((2, D_MODEL, 2 * D_EXPERT), F32), pltpu.VMEM((2, D_EXPERT, D_MODEL), F32),
                pltpu.VMEM((D_MODEL, 2 * D_EXPERT), BF16), pltpu.VMEM((D_EXPERT, D_MODEL), BF16),
                pltpu.SemaphoreType.DMA((2,)), pltpu.SemaphoreType.DMA((2,)), pltpu.SemaphoreType.DMA((2,)),
                pltpu.SemaphoreType.DMA, pltpu.SMEM((1,), I32),
            ],
        ),
        out_shape=jax.ShapeDtypeStruct(((TOP_K * k_stride + 2 * MOE_BLOCK) * ROW_TILE, LANES), F32),
        compiler_params=_cparams(("arbitrary",)),
        name="moe_experts",
    )(blk_e, nact, rows, rows, rows, xn, w1, b1, w2, b2)


def _block_diag(w):
    h, d, _ = w.shape
    eye = jnp.eye(h, dtype=w.dtype)
    return (eye[:, None, :, None] * w[:, :, None, :]).reshape(h * d, h * d)


def kernel(x_prompt, x_sample, state_lru_h, cache_lru_conv, cache_cf_conv, cache_mem_k, cache_mem_v,
           mem_prompt, g_mix, w_in, w_lru_conv, b_lru_conv, w_lru_a, b_lru_a, w_lru_x, b_lru_x,
           lru_lambda, w_cf_conv, b_cf_conv, g_cf_ln, b_cf_ln, w_out, g_xa, g_mem, w_q, w_k, w_v, w_o,
           g_moe, w_router, b_router, w_e1, b_e1, w_e2, b_e2, g_final):
    depth = g_mix.shape[0]
    assert depth == 1, "single-layer stack"
    batch, seq, _ = x_prompt.shape
    n_s = x_sample.shape[0]
    assert x_sample.shape[1] == 1
    n_p = batch * seq
    l = 0

    mix_w = (
        g_mix[l][None], w_in[l].astype(BF16), w_lru_conv[l], b_lru_conv[l][None],
        jnp.concatenate([_block_diag(w_lru_a[l]), _block_diag(w_lru_x[l])], axis=1).astype(BF16),
        jnp.concatenate([b_lru_a[l], b_lru_x[l]])[None], lru_lambda[l][None],
        w_cf_conv[l], b_cf_conv[l][None], g_cf_ln[l][None], b_cf_ln[l][None], w_out[l].astype(BF16),
    )
    router_w = (g_moe[l][None], w_router[l].T.astype(BF16), b_router[l][:, None])
    wq, wo = w_q[l].astype(BF16), w_o[l].astype(BF16)
    gxa = g_xa[l][None]

    xp = x_prompt.reshape(n_p, D_MODEL)
    xs_ = x_sample.reshape(n_s, D_MODEL)
    x1p, hp, lcp, ccp = _mixer_prompt(xp, mix_w, batch, seq)
    x1s, hs, zx_s, glu_s = _mixer_sample(xs_, mix_w, state_lru_h[l], cache_lru_conv[l], cache_cf_conv[l])

    mk, mv, mkb, mvb = _memkv(mem_prompt.reshape(batch * N_MEM, D_MODEL), g_mem[l][None],
                              w_k[l].astype(BF16), w_v[l].astype(BF16))
    x2s, xns, top_s, gates_s = _attn_sample(x1s, cache_mem_k[l], cache_mem_v[l], gxa, wq, wo, router_w)
    x2p, xn, top_p, gates_p = _attn_prompt(x1p, mkb, mvb, xns, gxa, wq, wo, router_w, batch, seq)

    dest_p, dest_s, blk_e, nact, cnt, pstart = _slots(top_p, top_s)
    dest_p, dest_s = dest_p.T.reshape(-1), dest_s.T.reshape(-1)
    nb = _moe_num_blocks((n_p + n_s) * TOP_K)
    nact = nact.reshape(1)
    k_stride = xn.shape[0] // ROW_TILE
    rows = _invert(cnt.reshape(N_EXPERTS), pstart.reshape(N_EXPERTS), nact, dest_p, dest_s, (nb + 1) * MOE_BLOCK,
                   n_p + n_s, k_stride)
    y4 = _experts(blk_e.reshape(nb), nact, rows.reshape(nb + 1, 1, MOE_BLOCK), xn, w_e1[l], b_e1[l][:, None, :],
                  w_e2[l], b_e2[l][:, None, :], n_p + n_s)
    gfin = g_final[None]
    y_p = _combine(gates_p.T, x2p, gfin, y4, 0, k_stride)
    y_s = _combine(gates_s.T, x2s, gfin, y4, n_p, k_stride)

    lru_conv_s = jnp.concatenate([cache_lru_conv[l][:, 1:], zx_s[:, None]], axis=1)
    cf_conv_s = jnp.concatenate([cache_cf_conv[l][:, 1:], glu_s[:, None]], axis=1)
    return (
        y_p.reshape(batch, seq, D_MODEL),
        y_s.reshape(n_s, 1, D_MODEL),
        hp.reshape(depth, batch, LRU_WIDTH),
        lcp[None],
        ccp[None],
        mk[None],
        mv[None],
        hs[None],
        lru_conv_s[None],
        cf_conv_s[None],
    )
```

```python
import functools

import jax
import jax.numpy as jnp
from jax import lax
from jax.experimental import pallas as pl
from jax.experimental.pallas import tpu as pltpu

F32 = jnp.float32
BF16 = jnp.bfloat16
I32 = jnp.int32

D_MODEL = 1024
LRU_WIDTH = 512
CF_WIDTH = 512
LRU_HEADS = 8
LRU_CONV = 4
LRU_C = 8.0
CF_CONV = 31
IN_WIDTH = 2 * LRU_WIDTH + 2 * CF_WIDTH
N_MEM = 256
XA_HEADS = 4
XA_HEAD_DIM = D_MODEL // XA_HEADS
N_EXPERTS = 32
TOP_K = 4
D_EXPERT = D_MODEL
SWIGLU_LIMIT = 7.0
SWIGLU_ALPHA = 1.702
EPS = 1e-6

LANES = 128
SUBLANES = 8
ROW_TILE = D_MODEL // LANES
assert ROW_TILE == SUBLANES
VMEM_LIMIT = 56 * 1024 * 1024

MIX_TILE = 256
LRU_TAIL_ROWS = 8
CF_TAIL_ROWS = 32
ATT_TILE = 1024
SAMPLE_GROUP = 4
SAMPLE_MIX_GROUP = 32
MOE_BLOCK = 256
DMA_UNROLL = 8
DISPATCH_TILE = 512
COMBINE_TILE = 256


def _cparams(sem):
    return pltpu.CompilerParams(dimension_semantics=sem, vmem_limit_bytes=VMEM_LIMIT)


def _full(shape):
    n = len(shape)
    return pl.BlockSpec(shape, lambda *_: (0,) * n)


def _rms(x, g):
    return x * lax.rsqrt(jnp.mean(x * x, axis=-1, keepdims=True) + EPS) * g


def _bdot(a, b):
    return jnp.dot(a.astype(BF16), b, preferred_element_type=F32)


def _store_row_tiles(ref, x):
    m = x.shape[0]
    for s in range(ROW_TILE):
        ref[pl.ds(s, m, stride=ROW_TILE), :] = x[:, s * LANES:(s + 1) * LANES]


def _load_row_tiles(ref, m):
    return jnp.concatenate([ref[pl.ds(s, m, stride=ROW_TILE), :] for s in range(ROW_TILE)], axis=-1)


def _row_tile_copy(src_ref, src_row, dst_ref, dst_row, sem, n_rows=1):
    src = src_ref.at[pl.ds(pl.multiple_of(src_row * ROW_TILE, ROW_TILE), n_rows * ROW_TILE)]
    dst = dst_ref.at[pl.ds(pl.multiple_of(dst_row * ROW_TILE, ROW_TILE), n_rows * ROW_TILE)]
    return pltpu.make_async_copy(src, dst, sem)


def _lru_coeffs(xc, wg_ref, bg_ref, lam_ref):
    gl = _bdot(xc, wg_ref[...]) + bg_ref[...]
    r = jax.nn.sigmoid(gl[:, :LRU_WIDTH])
    i = jax.nn.sigmoid(gl[:, LRU_WIDTH:])
    log_a = -LRU_C * r * jax.nn.softplus(-lam_ref[...])
    a = jnp.exp(log_a)
    u = jnp.sqrt(-jnp.tanh(log_a) * (1.0 + a * a)) * (i * xc)
    return a, u


def _cf_post(gc, gln_ref, bln_ref):
    mu = jnp.mean(gc, axis=-1, keepdims=True)
    c = gc - mu
    y = c * lax.rsqrt(jnp.mean(c * c, axis=-1, keepdims=True) + EPS)
    return jax.nn.silu(y * gln_ref[...] + bln_ref[...])


def _mixer_out(x, y_lru, y_cf, wout_ref):
    y = _bdot(y_lru, wout_ref[:LRU_WIDTH, :]) + _bdot(y_cf, wout_ref[LRU_WIDTH:, :])
    return x + y


def _mixer_prompt_kernel(x_ref, gmix_ref, win_ref, wlc_ref, blc_ref, wg_ref, bg_ref, lam_ref,
                         wcc_ref, bcc_ref, gln_ref, bln_ref, wout_ref,
                         x1_ref, h_ref, ltail_ref, ctail_ref,
                         zx_ext, glu_ext, h_carry):
    tt = MIX_TILE
    j = pl.program_id(1)

    @pl.when(j == 0)
    def _():
        zx_ext[0:LRU_TAIL_ROWS, :] = jnp.zeros((LRU_TAIL_ROWS, LRU_WIDTH), F32)
        glu_ext[0:CF_TAIL_ROWS, :] = jnp.zeros((CF_TAIL_ROWS, CF_WIDTH), F32)
        h_carry[...] = jnp.zeros((1, LRU_WIDTH), F32)

    x = x_ref[...]
    z = _bdot(_rms(x, gmix_ref[...]), win_ref[...])
    zx = z[:, :LRU_WIDTH]
    zg = z[:, LRU_WIDTH:2 * LRU_WIDTH]
    za = z[:, 2 * LRU_WIDTH:2 * LRU_WIDTH + CF_WIDTH]
    zb = z[:, 2 * LRU_WIDTH + CF_WIDTH:]

    zx_ext[LRU_TAIL_ROWS:LRU_TAIL_ROWS + tt, :] = zx
    off = LRU_TAIL_ROWS - (LRU_CONV - 1)
    xc = blc_ref[...]
    for k in range(LRU_CONV):
        xc = xc + wlc_ref[k:k + 1, :] * zx_ext[off + k:off + k + tt, :]
    a, u = _lru_coeffs(xc, wg_ref, bg_ref, lam_ref)
    sub = lax.broadcasted_iota(I32, (tt, LRU_WIDTH), 0) & (SUBLANES - 1)
    d = 1
    while d < SUBLANES:
        keep = sub >= d
        a_sh = jnp.where(keep, pltpu.roll(a, d, 0), 1.0)
        u_sh = jnp.where(keep, pltpu.roll(u, d, 0), 0.0)
        u = u + a * u_sh
        a = a * a_sh
        d *= 2
    h_last = h_carry[...]
    groups = []
    for r in range(0, tt, SUBLANES):
        hg = a[r:r + SUBLANES, :] * h_last + u[r:r + SUBLANES, :]
        groups.append(hg)
        h_last = hg[SUBLANES - 1:SUBLANES, :]
    hs = jnp.concatenate(groups, axis=0)
    h_carry[...] = h_last
    y_lru = hs * jax.nn.gelu(zg)

    glu = za * jax.nn.sigmoid(zb)
    glu_ext[CF_TAIL_ROWS:CF_TAIL_ROWS + tt, :] = glu
    off = CF_TAIL_ROWS - (CF_CONV - 1)
    gc = bcc_ref[...]
    for s in range(SUBLANES):
        rows = tt if s == 0 else tt + SUBLANES
        q = None
        for k in range(CF_CONV):
            if (off + k) % SUBLANES == s:
                base = off + k - s
                term = wcc_ref[k:k + 1, :] * glu_ext[base:base + rows, :]
                q = term if q is None else q + term
        gc = gc + (q if s == 0 else q[s:s + tt, :])
    y_cf = _cf_post(gc, gln_ref, bln_ref)

    x1_ref[...] = _mixer_out(x, y_lru, y_cf, wout_ref)

    h_ref[0] = h_last
    ltail_ref[0] = zx_ext[LRU_TAIL_ROWS + tt - (LRU_CONV - 1):LRU_TAIL_ROWS + tt, :]
    ctail_ref[0] = glu_ext[CF_TAIL_ROWS + tt - (CF_CONV - 1):CF_TAIL_ROWS + tt, :]
    zx_ext[0:LRU_TAIL_ROWS, :] = zx_ext[tt:tt + LRU_TAIL_ROWS, :]
    glu_ext[0:CF_TAIL_ROWS, :] = glu_ext[tt:tt + CF_TAIL_ROWS, :]


def _mixer_weight_specs():
    return [
        _full((1, D_MODEL)),
        _full((D_MODEL, IN_WIDTH)),
        _full((LRU_CONV, LRU_WIDTH)), _full((1, LRU_WIDTH)),
        _full((LRU_WIDTH, 2 * LRU_WIDTH)), _full((1, 2 * LRU_WIDTH)),
        _full((1, LRU_WIDTH)),
        _full((CF_CONV, CF_WIDTH)), _full((1, CF_WIDTH)),
        _full((1, CF_WIDTH)), _full((1, CF_WIDTH)),
        _full((D_MODEL, D_MODEL)),
    ]


def _mixer_prompt(x, mix_w, batch, seq):
    tt = MIX_TILE
    nt = seq // tt
    return pl.pallas_call(
        _mixer_prompt_kernel,
        grid=(batch, nt),
        in_specs=[pl.BlockSpec((tt, D_MODEL), lambda b, j: (b * nt + j, 0))] + _mixer_weight_specs(),
        out_specs=[
            pl.BlockSpec((tt, D_MODEL), lambda b, j: (b * nt + j, 0)),
            pl.BlockSpec((1, 1, LRU_WIDTH), lambda b, j: (b, 0, 0)),
            pl.BlockSpec((1, LRU_CONV - 1, LRU_WIDTH), lambda b, j: (b, 0, 0)),
            pl.BlockSpec((1, CF_CONV - 1, CF_WIDTH), lambda b, j: (b, 0, 0)),
        ],
        out_shape=[
            jax.ShapeDtypeStruct((batch * seq, D_MODEL), F32),
            jax.ShapeDtypeStruct((batch, 1, LRU_WIDTH), F32),
            jax.ShapeDtypeStruct((batch, LRU_CONV - 1, LRU_WIDTH), F32),
            jax.ShapeDtypeStruct((batch, CF_CONV - 1, CF_WIDTH), F32),
        ],
        scratch_shapes=[
            pltpu.VMEM((LRU_TAIL_ROWS + tt, LRU_WIDTH), F32),
            pltpu.VMEM((CF_TAIL_ROWS + tt, CF_WIDTH), F32),
            pltpu.VMEM((1, LRU_WIDTH), F32),
        ],
        compiler_params=_cparams(("arbitrary", "arbitrary")),
        name="mixer_prompt",
    )(x, *mix_w)


def _mixer_sample_kernel(x_ref, gmix_ref, win_ref, wlc_ref, blc_ref, wg_ref, bg_ref, lam_ref,
                         wcc_ref, bcc_ref, gln_ref, bln_ref, wout_ref,
                         h0_ref, lbuf_ref, cbuf_ref,
                         x1_ref, h_ref, ltail_ref, ctail_ref):
    x = x_ref[...]
    z = _bdot(_rms(x, gmix_ref[...]), win_ref[...])
    zx = z[:, :LRU_WIDTH]
    zg = z[:, LRU_WIDTH:2 * LRU_WIDTH]
    za = z[:, 2 * LRU_WIDTH:2 * LRU_WIDTH + CF_WIDTH]
    zb = z[:, 2 * LRU_WIDTH + CF_WIDTH:]

    xc = blc_ref[...] + wlc_ref[LRU_CONV - 1:LRU_CONV, :] * zx
    xc = xc + jnp.sum(lbuf_ref[...] * wlc_ref[0:LRU_CONV - 1, :][None], axis=1)
    a, u = _lru_coeffs(xc, wg_ref, bg_ref, lam_ref)
    h = a * h0_ref[...] + u
    y_lru = h * jax.nn.gelu(zg)

    glu = za * jax.nn.sigmoid(zb)
    gc = bcc_ref[...] + wcc_ref[CF_CONV - 1:CF_CONV, :] * glu
    gc = gc + jnp.sum(cbuf_ref[...] * wcc_ref[0:CF_CONV - 1, :][None], axis=1)
    y_cf = _cf_post(gc, gln_ref, bln_ref)

    x1_ref[...] = _mixer_out(x, y_lru, y_cf, wout_ref)
    h_ref[...] = h
    ltail_ref[:, 0:LRU_CONV - 2, :] = lbuf_ref[:, 1:LRU_CONV - 1, :]
    ctail_ref[:, 0:CF_CONV - 2, :] = cbuf_ref[:, 1:CF_CONV - 1, :]
    for i in range(x.shape[0]):
        ltail_ref[i, LRU_CONV - 2:LRU_CONV - 1, :] = zx[i:i + 1, :]
        ctail_ref[i, CF_CONV - 2:CF_CONV - 1, :] = glu[i:i + 1, :]


def _mixer_sample(x, mix_w, h0, lbuf, cbuf):
    n = x.shape[0]
    g = SAMPLE_MIX_GROUP
    row = lambda w: pl.BlockSpec((g, w), lambda i: (i, 0))
    return pl.pallas_call(
        _mixer_sample_kernel,
        grid=(n // g,),
        in_specs=[row(D_MODEL)] + _mixer_weight_specs() + [
            row(LRU_WIDTH),
            pl.BlockSpec((g, LRU_CONV - 1, LRU_WIDTH), lambda i: (i, 0, 0)),
            pl.BlockSpec((g, CF_CONV - 1, CF_WIDTH), lambda i: (i, 0, 0)),
        ],
        out_specs=[row(D_MODEL), row(LRU_WIDTH),
                   pl.BlockSpec((g, LRU_CONV - 1, LRU_WIDTH), lambda i: (i, 0, 0)),
                   pl.BlockSpec((g, CF_CONV - 1, CF_WIDTH), lambda i: (i, 0, 0))],
        out_shape=[
            jax.ShapeDtypeStruct((n, D_MODEL), F32),
            jax.ShapeDtypeStruct((n, LRU_WIDTH), F32),
            jax.ShapeDtypeStruct((n, LRU_CONV - 1, LRU_WIDTH), F32),
            jax.ShapeDtypeStruct((n, CF_CONV - 1, CF_WIDTH), F32),
        ],
        compiler_params=_cparams(("arbitrary",)),
        name="mixer_sample",
    )(x, *mix_w, h0, lbuf, cbuf)


def _memkv_kernel(mem_ref, gmem_ref, wk_ref, wv_ref, k_ref, v_ref, kb_ref, vb_ref):
    mn = _rms(mem_ref[...], gmem_ref[...]).astype(BF16)
    k = jnp.dot(mn, wk_ref[...], preferred_element_type=F32)
    v = jnp.dot(mn, wv_ref[...], preferred_element_type=F32)
    for h in range(XA_HEADS):
        sl = slice(h * XA_HEAD_DIM, (h + 1) * XA_HEAD_DIM)
        k_ref[0, :, h, :] = k[:, sl]
        v_ref[0, :, h, :] = v[:, sl]
    kb_ref[...] = k.astype(BF16)
    vb_ref[...] = v.astype(BF16)


def _memkv(mem, g_mem, wk, wv):
    n = mem.shape[0]
    t = N_MEM
    row = pl.BlockSpec((t, D_MODEL), lambda i: (i, 0))
    state = pl.BlockSpec((1, N_MEM, XA_HEADS, XA_HEAD_DIM), lambda i: (i, 0, 0, 0))
    state_shape = jax.ShapeDtypeStruct((n // t, N_MEM, XA_HEADS, XA_HEAD_DIM), F32)
    return pl.pallas_call(
        _memkv_kernel,
        grid=(n // t,),
        in_specs=[row, _full((1, D_MODEL)), _full((D_MODEL, D_MODEL)), _full((D_MODEL, D_MODEL))],
        out_specs=[state, state, row, row],
        out_shape=[state_shape] * 2 + [jax.ShapeDtypeStruct((n, D_MODEL), BF16)] * 2,
        compiler_params=_cparams(("arbitrary",)),
        name="memory_kv",
    )(mem, g_mem, wk, wv)


def _router(x2, gmoe_ref, wrt_ref, brt_ref, xn_ref, tope_ref, gates_ref):
    xn = _rms(x2, gmoe_ref[...])
    _store_row_tiles(xn_ref, xn)
    logits = lax.dot_general(wrt_ref[...], xn.astype(BF16), (((1,), (1,)), ((), ())),
                             preferred_element_type=F32) + brt_ref[...]
    e_iota = lax.broadcasted_iota(I32, logits.shape, 0)
    work = logits
    vals, idxs = [], []
    for _ in range(TOP_K):
        m = jnp.max(work, axis=0, keepdims=True)
        idx = jnp.min(jnp.where(work == m, e_iota, N_EXPERTS), axis=0, keepdims=True)
        vals.append(m)
        idxs.append(idx)
        work = jnp.where(e_iota == idx, -jnp.inf, work)
    ex = [jnp.exp(v - vals[0]) for v in vals]
    den = ex[0] + ex[1] + ex[2] + ex[3]
    tope_ref[...] = jnp.concatenate(idxs, axis=0)
    gates_ref[...] = jnp.concatenate([e / den for e in ex], axis=0)


def _router_specs():
    return [_full((1, D_MODEL)), _full((N_EXPERTS, D_MODEL)), _full((N_EXPERTS, 1))]


def _router_out(n, tile, index):
    specs = [
        pl.BlockSpec((tile, D_MODEL), lambda *a: (index(*a), 0)),
        pl.BlockSpec((tile * ROW_TILE, LANES), lambda *a: (index(*a), 0)),
        pl.BlockSpec((TOP_K, tile), lambda *a: (0, index(*a))),
        pl.BlockSpec((TOP_K, tile), lambda *a: (0, index(*a))),
    ]
    shapes = [
        jax.ShapeDtypeStruct((n, D_MODEL), F32),
        jax.ShapeDtypeStruct((n * ROW_TILE, LANES), F32),
        jax.ShapeDtypeStruct((TOP_K, n), I32),
        jax.ShapeDtypeStruct((TOP_K, n), F32),
    ]
    return specs, shapes


def _attn_prompt_kernel(x1_ref, k_ref, v_ref, gxa_ref, wq_ref, wo_ref, gmoe_ref, wrt_ref, brt_ref,
                        x2_ref, xn_ref, tope_ref, gates_ref):
    x1 = x1_ref[...]
    q = (_bdot(_rms(x1, gxa_ref[...]), wq_ref[...]) * (XA_HEAD_DIM ** -0.5)).astype(BF16)
    outs = []
    for h in range(XA_HEADS):
        sl = slice(h * XA_HEAD_DIM, (h + 1) * XA_HEAD_DIM)
        s = lax.dot_general(q[:, sl], k_ref[:, sl], (((1,), (1,)), ((), ())), preferred_element_type=F32)
        p = jnp.exp(s - jnp.max(s, axis=-1, keepdims=True))
        p = p / jnp.sum(p, axis=-1, keepdims=True)
        outs.append(_bdot(p, v_ref[:, sl]))
    o = jnp.concatenate(outs, axis=-1)
    x2 = x1 + _bdot(o, wo_ref[...])
    x2_ref[...] = x2
    _router(x2, gmoe_ref, wrt_ref, brt_ref, xn_ref, tope_ref, gates_ref)


def _attn_prompt(x1, k, v, gxa, wq, wo, router_w, batch, seq):
    tq = ATT_TILE
    nt = seq // tq
    out_specs, out_shapes = _router_out(batch * seq, tq, lambda b, j: b * nt + j)
    kv = pl.BlockSpec((N_MEM, D_MODEL), lambda b, j: (b, 0))
    return pl.pallas_call(
        _attn_prompt_kernel,
        grid=(batch, nt),
        in_specs=[pl.BlockSpec((tq, D_MODEL), lambda b, j: (b * nt + j, 0)), kv, kv,
                  _full((1, D_MODEL)), _full((D_MODEL, D_MODEL)), _full((D_MODEL, D_MODEL))] + _router_specs(),
        out_specs=out_specs,
        out_shape=out_shapes,
        compiler_params=_cparams(("arbitrary", "arbitrary")),
        name="attn_prompt",
    )(x1, k, v, gxa, wq, wo, *router_w)


def _qproj_kernel(x1_ref, gxa_ref, wq_ref, q_ref):
    q_ref[...] = _bdot(_rms(x1_ref[...], gxa_ref[...]), wq_ref[...]) * (XA_HEAD_DIM ** -0.5)


def _attn_sample_core_kernel(q_ref, k_ref, v_ref, o_ref):
    n = N_MEM * SUBLANES
    col_head = lax.broadcasted_iota(I32, (SUBLANES, n), 1) & (SUBLANES - 1)
    same_head = col_head == lax.broadcasted_iota(I32, (SUBLANES, n), 0)
    kv_pad = jnp.zeros((N_MEM, SUBLANES - XA_HEADS, XA_HEAD_DIM), F32)
    q_pad = jnp.zeros((SUBLANES - XA_HEADS, XA_HEAD_DIM), F32)
    for g in range(q_ref.shape[0]):
        k8 = jnp.concatenate([k_ref[g], kv_pad], axis=1).reshape(n, XA_HEAD_DIM).astype(BF16)
        v8 = jnp.concatenate([v_ref[g], kv_pad], axis=1).reshape(n, XA_HEAD_DIM).astype(BF16)
        q8 = jnp.concatenate([q_ref[g], q_pad], axis=0).astype(BF16)
        s = lax.dot_general(q8, k8, (((1,), (1,)), ((), ())), preferred_element_type=F32)
        m = jnp.max(jnp.where(same_head, s, -jnp.inf), axis=-1, keepdims=True)
        p = jnp.where(same_head, jnp.exp(s - m), 0.0)
        den = jnp.sum(p, axis=-1, keepdims=True)
        o8 = jnp.dot(p.astype(BF16), v8, preferred_element_type=F32)
        o_ref[g] = (o8 / den)[0:XA_HEADS, :]


def _oproj_router_kernel(x1_ref, o_ref, wo_ref, gmoe_ref, wrt_ref, brt_ref,
                         x2_ref, xn_ref, tope_ref, gates_ref):
    x2 = x1_ref[...] + _bdot(o_ref[...], wo_ref[...])
    x2_ref[...] = x2
    _router(x2, gmoe_ref, wrt_ref, brt_ref, xn_ref, tope_ref, gates_ref)


def _attn_sample(x1, k, v, gxa, wq, wo, router_w):
    n = x1.shape[0]
    g = SAMPLE_GROUP
    q = pl.pallas_call(
        _qproj_kernel,
        grid=(1,),
        in_specs=[_full((n, D_MODEL)), _full((1, D_MODEL)), _full((D_MODEL, D_MODEL))],
        out_specs=_full((n, D_MODEL)),
        out_shape=jax.ShapeDtypeStruct((n, D_MODEL), F32),
        compiler_params=_cparams(("arbitrary",)),
        name="qproj_sample",
    )(x1, gxa, wq)
    kv = pl.BlockSpec((g, N_MEM, XA_HEADS, XA_HEAD_DIM), lambda i: (i, 0, 0, 0))
    row = pl.BlockSpec((g, XA_HEADS, XA_HEAD_DIM), lambda i: (i, 0, 0))
    o = pl.pallas_call(
        _attn_sample_core_kernel,
        grid=(n // g,),
        in_specs=[row, kv, kv],
        out_specs=row,
        out_shape=jax.ShapeDtypeStruct((n, XA_HEADS, XA_HEAD_DIM), F32),
        compiler_params=_cparams(("arbitrary",)),
        name="attn_sample_core",
    )(q.reshape(n, XA_HEADS, XA_HEAD_DIM), k, v).reshape(n, D_MODEL)
    out_specs, out_shapes = _router_out(n, n, lambda i: 0)
    return pl.pallas_call(
        _oproj_router_kernel,
        grid=(1,),
        in_specs=[_full((n, D_MODEL)), _full((n, D_MODEL)), _full((D_MODEL, D_MODEL))] + _router_specs(),
        out_specs=out_specs,
        out_shape=out_shapes,
        compiler_params=_cparams(("arbitrary",)),
        name="oproj_router_sample",
    )(x1, o, wo, *router_w)


def _moe_num_blocks(n_assign):
    return pl.cdiv(n_assign, MOE_BLOCK) + N_EXPERTS


def _slots_kernel(tp_ref, ts_ref, destp_ref, dests_ref, blke_ref, nact_ref, cnt_ref, pstart_ref,
                  rankp, ranks):
    c = LANES
    r_i = lax.broadcasted_iota(I32, (c, c), 0)
    c_i = lax.broadcasted_iota(I32, (c, c), 1)
    upper = (r_i < c_i).astype(BF16)
    e_iota = lax.broadcasted_iota(I32, (N_EXPERTS, c), 0)

    def make_body(top_ref, rank_ref):
        def body(ci, carry):
            lo = pl.multiple_of(ci * c, c)
            top = top_ref[:, pl.ds(lo, c)]
            hit = [e_iota == top[k:k + 1, :] for k in range(TOP_K)]
            cnt = sum(m.astype(F32) for m in hit)
            base = carry + jnp.dot(cnt.astype(BF16), upper, preferred_element_type=F32)
            rank_ref[:, pl.ds(lo, c)] = jnp.concatenate(
                [jnp.sum(jnp.where(m, base, 0.0), axis=0, keepdims=True) for m in hit], axis=0)
            return carry + jnp.sum(cnt, axis=1, keepdims=True)
        return body

    counts = jnp.zeros((N_EXPERTS, 1), F32)
    counts = lax.fori_loop(0, tp_ref.shape[1] // c, make_body(tp_ref, rankp), counts)
    counts = lax.fori_loop(0, ts_ref.shape[1] // c, make_body(ts_ref, ranks), counts)

    padded = jnp.floor((counts + (MOE_BLOCK - 1)) / MOE_BLOCK) * MOE_BLOCK
    er = lax.broadcasted_iota(I32, (N_EXPERTS, N_EXPERTS), 0)
    ec = lax.broadcasted_iota(I32, (N_EXPERTS, N_EXPERTS), 1)
    padded_row = jnp.sum(jnp.where(er == ec, padded, 0.0), axis=0, keepdims=True)
    cum = jnp.sum(jnp.where(ec <= er, padded_row, 0.0), axis=1, keepdims=True)
    pstart = cum - padded
    cnt_ref[...] = jnp.sum(jnp.where(er == ec, counts, 0.0), axis=0, keepdims=True).astype(I32)
    pstart_ref[...] = jnp.sum(jnp.where(er == ec, pstart, 0.0), axis=0, keepdims=True).astype(I32)

    def dest(top_ref, rank_ref, out_ref):
        n = top_ref.shape[1]
        ei = lax.broadcasted_iota(I32, (N_EXPERTS, n), 0)
        rows = []
        for k in range(TOP_K):
            start = jnp.sum(jnp.where(ei == top_ref[k:k + 1, :], pstart, 0.0), axis=0, keepdims=True)
            rows.append(start + rank_ref[k:k + 1, :])
        out_ref[...] = jnp.concatenate(rows, axis=0).astype(I32)

    dest(tp_ref, rankp, destp_ref)
    dest(ts_ref, ranks, dests_ref)

    nb = blke_ref.shape[1]
    blk_lo = lax.broadcasted_iota(I32, (N_EXPERTS, nb), 1).astype(F32) * MOE_BLOCK
    blk = jnp.sum((cum <= blk_lo).astype(F32), axis=0, keepdims=True)
    blke_ref[...] = jnp.minimum(blk, N_EXPERTS - 1).astype(I32)
    nact_ref[...] = (cum[N_EXPERTS - 1:N_EXPERTS, :] / MOE_BLOCK).astype(I32)


def _slots(top_p, top_s):
    n_p, n_s = top_p.shape[1], top_s.shape[1]
    nb = _moe_num_blocks((n_p + n_s) * TOP_K)
    return pl.pallas_call(
        _slots_kernel,
        grid=(1,),
        in_specs=[_full((TOP_K, n_p)), _full((TOP_K, n_s))],
        out_specs=[_full((TOP_K, n_p)), _full((TOP_K, n_s)), _full((1, nb)), _full((1, 1)),
                   _full((1, N_EXPERTS)), _full((1, N_EXPERTS))],
        out_shape=[
            jax.ShapeDtypeStruct((TOP_K, n_p), I32),
            jax.ShapeDtypeStruct((TOP_K, n_s), I32),
            jax.ShapeDtypeStruct((1, nb), I32),
            jax.ShapeDtypeStruct((1, 1), I32),
            jax.ShapeDtypeStruct((1, N_EXPERTS), I32),
            jax.ShapeDtypeStruct((1, N_EXPERTS), I32),
        ],
        scratch_shapes=[pltpu.VMEM((TOP_K, n_p), F32), pltpu.VMEM((TOP_K, n_s), F32)],
        compiler_params=_cparams(("arbitrary",)),
        name="moe_slots",
    )(top_p, top_s)


def _for_rows(n_rows, fn):
    def body(i, c):
        for u in range(DMA_UNROLL):
            fn(i * DMA_UNROLL + u)
        return c
    lax.fori_loop(0, n_rows // DMA_UNROLL, body, 0)


def _dispatch_kernel(cnt_ref, pstart_ref, nact_ref, destp_ref, dests_ref, xnp_ref, xns_ref, xs_ref,
                     zeros, sem, zsem):
    i = pl.program_id(0)
    last = pl.num_programs(0) - 1

    def scatter(x_ref, dest_ref):
        n = x_ref.shape[0] // ROW_TILE

        def one(t):
            for k in range(TOP_K):
                _row_tile_copy(x_ref, t, xs_ref, dest_ref[t * TOP_K + k], sem).start(priority=k % 2)

        _for_rows(n, one)
        _row_tile_copy(xs_ref, 0, xs_ref, 0, sem, n * TOP_K).wait()

    @pl.when(i < last)
    def _():
        scatter(xnp_ref, destp_ref)

    @pl.when(i == last)
    def _():
        scatter(xns_ref, dests_ref)
        zeros[...] = jnp.zeros(zeros.shape, F32)

        def pad_copies(e, act):
            c = cnt_ref[e]
            npad = (-c) & (MOE_BLOCK - 1)
            base = pstart_ref[e] + c
            for bit in range(MOE_BLOCK.bit_length() - 1):
                size = 1 << bit

                @pl.when((npad & size) != 0)
                def _():
                    act(_row_tile_copy(zeros, 0, xs_ref, base + (npad & (size - 1)), zsem, size))

        def tail_copies(act):
            def body(j, c):
                act(_row_tile_copy(zeros, 0, xs_ref, j * MOE_BLOCK, zsem, MOE_BLOCK))
                return c
            lax.fori_loop(nact_ref[0], xs_ref.shape[0] // (MOE_BLOCK * ROW_TILE), body, 0)

        for act in (lambda d: d.start(), lambda d: d.wait()):
            lax.fori_loop(0, N_EXPERTS, lambda e, c, act=act: (pad_copies(e, act), c)[1], 0)
            tail_copies(act)


def _dispatch(cnt, pstart, nact, dest_p, dest_s, xn_p, xn_s, n_rows):
    n_p, n_s = xn_p.shape[0] // ROW_TILE, xn_s.shape[0] // ROW_TILE
    tile = DISPATCH_TILE
    nt = n_p // tile
    smem = pl.BlockSpec(memory_space=pltpu.SMEM)
    return pl.pallas_call(
        _dispatch_kernel,
        grid=(nt + 1,),
        in_specs=[
            smem, smem, smem,
            pl.BlockSpec((TOP_K * tile,), lambda i: (jnp.minimum(i, nt - 1),), memory_space=pltpu.SMEM),
            smem,
            pl.BlockSpec((tile * ROW_TILE, LANES), lambda i: (jnp.minimum(i, nt - 1), 0)),
            _full((n_s * ROW_TILE, LANES)),
        ],
        out_specs=pl.BlockSpec(memory_space=pl.ANY),
        out_shape=jax.ShapeDtypeStruct((n_rows * ROW_TILE, LANES), F32),
        scratch_shapes=[pltpu.VMEM((MOE_BLOCK * ROW_TILE, LANES), F32), pltpu.SemaphoreType.DMA,
                        pltpu.SemaphoreType.DMA],
        compiler_params=_cparams(("arbitrary",)),
        name="moe_dispatch",
    )(cnt, pstart, nact, dest_p, dest_s, xn_p, xn_s)


def _combine_kernel(dest_ref, dest_next_ref, gates_ref, x2_ref, gfin_ref, ys_ref, y_ref, bufs, sems):
    tile = x2_ref.shape[0]
    i = pl.program_id(0)
    slot = i % 2

    def gather(d_ref, b):
        def one(t):
            for k in range(TOP_K):
                _row_tile_copy(ys_ref, d_ref[t * TOP_K + k], bufs.at[b, k], t, sems.at[b]).start(priority=k % 2)
        _for_rows(tile, one)

    @pl.when(i == 0)
    def _():
        gather(dest_ref, 0)

    for b in range(2):
        @pl.when(slot == b)
        def _():
            pltpu.make_async_copy(bufs.at[b], bufs.at[b], sems.at[b]).wait()

            @pl.when(i + 1 < pl.num_programs(0))
            def _():
                gather(dest_next_ref, 1 - b)

    buf = bufs.at[slot]
    sumsq = jnp.zeros((tile, 1), F32)
    gate = [jnp.broadcast_to(gates_ref[:, k:k + 1], (tile, LANES)) for k in range(TOP_K)]
    for s in range(ROW_TILE):
        cols = slice(s * LANES, (s + 1) * LANES)
        acc = x2_ref[:, cols]
        for k in range(TOP_K):
            acc = acc + gate[k] * buf[k, pl.ds(s, tile, stride=ROW_TILE), :]
        y_ref[:, cols] = acc
        sumsq = sumsq + jnp.sum(acc * acc, axis=-1, keepdims=True)
    scale = lax.rsqrt(sumsq / D_MODEL + EPS)
    for s in range(ROW_TILE):
        cols = slice(s * LANES, (s + 1) * LANES)
        y_ref[:, cols] = y_ref[:, cols] * scale * gfin_ref[:, cols]


def _combine(dest, gates_t, x2, g_final, ys):
    n = x2.shape[0]
    tile = min(COMBINE_TILE, n)
    row = pl.BlockSpec((tile, D_MODEL), lambda i: (i, 0))
    nt = n // tile
    return pl.pallas_call(
        _combine_kernel,
        grid=(nt,),
        in_specs=[
            pl.BlockSpec((TOP_K * tile,), lambda i: (i,), memory_space=pltpu.SMEM),
            pl.BlockSpec((TOP_K * tile,), lambda i: (jnp.minimum(i + 1, nt - 1),), memory_space=pltpu.SMEM),
            pl.BlockSpec((tile, TOP_K), lambda i: (i, 0)),
            row,
            _full((1, D_MODEL)),
            pl.BlockSpec(memory_space=pl.ANY),
        ],
        out_specs=row,
        out_shape=jax.ShapeDtypeStruct((n, D_MODEL), F32),
        scratch_shapes=[pltpu.VMEM((2, TOP_K, tile * ROW_TILE, LANES), F32), pltpu.SemaphoreType.DMA((2,))],
        compiler_params=_cparams(("arbitrary",)),
        name="moe_combine",
    )(dest, dest, gates_t, x2, g_final, ys)


def _experts_kernel(blke_ref, nact_ref, xs_ref, w1_hbm, b1_ref, w2_hbm, b2_ref, ys_ref,
                    w1f, w2f, w1b, w2b, wsem, started):
    j = pl.program_id(0)
    nact = nact_ref[0]
    active = j < nact
    e = blke_ref[j]
    new_expert = jnp.logical_or(j == 0, e != blke_ref[jnp.maximum(j - 1, 0)])

    def fetch(expert, slot):
        return (pltpu.make_async_copy(w1_hbm.at[expert], w1f.at[slot], wsem.at[slot]),
                pltpu.make_async_copy(w2_hbm.at[expert], w2f.at[slot], wsem.at[slot]))

    @pl.when(j == 0)
    def _():
        started[0] = 0
        for d in fetch(e, 0):
            d.start()

    @pl.when(jnp.logical_and(active, new_expert))
    def _():
        n = started[0]
        last = blke_ref.shape[0] - 1
        jn = lax.while_loop(lambda jj: jnp.logical_and(jj < nact, blke_ref[jnp.minimum(jj, last)] == e),
                            lambda jj: jj + 1, j + 1)
        for b in range(2):
            @pl.when(n % 2 == b)
            def _():
                for d in fetch(e, b):
                    d.wait()

                @pl.when(jn < nact)
                def _():
                    for d in fetch(blke_ref[jnp.minimum(jn, last)], 1 - b):
                        d.start(priority=1)

                w1b[...] = w1f[b].astype(BF16)
                w2b[...] = w2f[b].astype(BF16)
        started[0] = n + 1

    @pl.when(active)
    def _():
        x = _load_row_tiles(xs_ref, MOE_BLOCK).astype(BF16)
        gu = jnp.dot(x, w1b[...], preferred_element_type=F32) + b1_ref[0]
        g = jnp.minimum(gu[:, :D_EXPERT], SWIGLU_LIMIT)
        u = jnp.clip(gu[:, D_EXPERT:], -SWIGLU_LIMIT, SWIGLU_LIMIT)
        h = (u + 1.0) * (g * jax.nn.sigmoid(SWIGLU_ALPHA * g))
        _store_row_tiles(ys_ref, _bdot(h, w2b[...]) + b2_ref[0])

    @pl.when(jnp.logical_not(active))
    def _():
        ys_ref[...] = jnp.zeros(ys_ref.shape, F32)


def _experts(blk_e, nact, xs, w1, b1, w2, b2):
    nb = xs.shape[0] // (MOE_BLOCK * ROW_TILE)
    blk = lambda j, be, na: (jnp.minimum(j, na[0] - 1), 0)
    exp = lambda j, be, na: (be[jnp.minimum(j, na[0] - 1)], 0, 0)
    rows = (MOE_BLOCK * ROW_TILE, LANES)
    return pl.pallas_call(
        _experts_kernel,
        grid_spec=pltpu.PrefetchScalarGridSpec(
            num_scalar_prefetch=2,
            grid=(nb,),
            in_specs=[
                pl.BlockSpec(rows, blk),
                pl.BlockSpec(memory_space=pl.ANY),
                pl.BlockSpec((1, 1, 2 * D_EXPERT), exp),
                pl.BlockSpec(memory_space=pl.ANY),
                pl.BlockSpec((1, 1, D_MODEL), exp),
            ],
            out_specs=pl.BlockSpec(rows, lambda j, be, na: (j, 0)),
            scratch_shapes=[
                pltpu.VMEM((2, D_MODEL, 2 * D_EXPERT), F32), pltpu.VMEM((2, D_EXPERT, D_MODEL), F32),
                pltpu.VMEM((D_MODEL, 2 * D_EXPERT), BF16), pltpu.VMEM((D_EXPERT, D_MODEL), BF16),
                pltpu.SemaphoreType.DMA((2,)), pltpu.SMEM((1,), I32),
            ],
        ),
        out_shape=jax.ShapeDtypeStruct(xs.shape, F32),
        compiler_params=_cparams(("arbitrary",)),
        name="moe_experts",
    )(blk_e, nact, xs, w1, b1, w2, b2)


def _block_diag(w):
    h, d, _ = w.shape
    eye = jnp.eye(h, dtype=w.dtype)
    return (eye[:, None, :, None] * w[:, :, None, :]).reshape(h * d, h * d)


def kernel(x_prompt, x_sample, state_lru_h, cache_lru_conv, cache_cf_conv, cache_mem_k, cache_mem_v,
           mem_prompt, g_mix, w_in, w_lru_conv, b_lru_conv, w_lru_a, b_lru_a, w_lru_x, b_lru_x,
           lru_lambda, w_cf_conv, b_cf_conv, g_cf_ln, b_cf_ln, w_out, g_xa, g_mem, w_q, w_k, w_v, w_o,
           g_moe, w_router, b_router, w_e1, b_e1, w_e2, b_e2, g_final):
    depth = g_mix.shape[0]
    assert depth == 1, "single-layer stack"
    batch, seq, _ = x_prompt.shape
    n_s = x_sample.shape[0]
    assert x_sample.shape[1] == 1
    n_p = batch * seq
    l = 0

    mix_w = (
        g_mix[l][None], w_in[l].astype(BF16), w_lru_conv[l], b_lru_conv[l][None],
        jnp.concatenate([_block_diag(w_lru_a[l]), _block_diag(w_lru_x[l])], axis=1).astype(BF16),
        jnp.concatenate([b_lru_a[l], b_lru_x[l]])[None], lru_lambda[l][None],
        w_cf_conv[l], b_cf_conv[l][None], g_cf_ln[l][None], b_cf_ln[l][None], w_out[l].astype(BF16),
    )
    router_w = (g_moe[l][None], w_router[l].T.astype(BF16), b_router[l][:, None])
    wq, wo = w_q[l].astype(BF16), w_o[l].astype(BF16)
    gxa = g_xa[l][None]

    xp = x_prompt.reshape(n_p, D_MODEL)
    xs_ = x_sample.reshape(n_s, D_MODEL)
    x1p, hp, lcp, ccp = _mixer_prompt(xp, mix_w, batch, seq)
    x1s, hs, lru_conv_s, cf_conv_s = _mixer_sample(xs_, mix_w, state_lru_h[l], cache_lru_conv[l], cache_cf_conv[l])

    mk, mv, mkb, mvb = _memkv(mem_prompt.reshape(batch * N_MEM, D_MODEL), g_mem[l][None],
                              w_k[l].astype(BF16), w_v[l].astype(BF16))
    x2p, xnp_, top_p, gates_p = _attn_prompt(x1p, mkb, mvb, gxa, wq, wo, router_w, batch, seq)
    x2s, xns, top_s, gates_s = _attn_sample(x1s, cache_mem_k[l], cache_mem_v[l], gxa, wq, wo, router_w)

    dest_p, dest_s, blk_e, nact, cnt, pstart = _slots(top_p, top_s)
    dest_p, dest_s = dest_p.T.reshape(-1), dest_s.T.reshape(-1)
    nb = _moe_num_blocks((n_p + n_s) * TOP_K)
    nact = nact.reshape(1)
    xs = _dispatch(cnt.reshape(N_EXPERTS), pstart.reshape(N_EXPERTS), nact, dest_p, dest_s, xnp_, xns,
                   nb * MOE_BLOCK)
    ys = _experts(blk_e.reshape(nb), nact, xs, w_e1[l], b_e1[l][:, None, :], w_e2[l], b_e2[l][:, None, :])
    gfin = g_final[None]
    y_p = _combine(dest_p, gates_p.T, x2p, gfin, ys)
    y_s = _combine(dest_s, gates_s.T, x2s, gfin, ys)

    return (
        y_p.reshape(batch, seq, D_MODEL),
        y_s.reshape(n_s, 1, D_MODEL),
        hp.reshape(depth, batch, LRU_WIDTH),
        lcp[None],
        ccp[None],
        mk[None],
        mv[None],
        hs[None],
        lru_conv_s[None],
        cf_conv_s[None],
    )
```

```python
import functools

import jax
import jax.numpy as jnp
from jax import lax
from jax.experimental import pallas as pl
from jax.experimental.pallas import tpu as pltpu

F32 = jnp.float32
BF16 = jnp.bfloat16
I32 = jnp.int32

D_MODEL = 1024
LRU_WIDTH = 512
CF_WIDTH = 512
LRU_HEADS = 8
LRU_CONV = 4
LRU_C = 8.0
CF_CONV = 31
IN_WIDTH = 2 * LRU_WIDTH + 2 * CF_WIDTH
N_MEM = 256
XA_HEADS = 4
XA_HEAD_DIM = D_MODEL // XA_HEADS
N_EXPERTS = 32
TOP_K = 4
D_EXPERT = D_MODEL
SWIGLU_LIMIT = 7.0
SWIGLU_ALPHA = 1.702
EPS = 1e-6

LANES = 128
SUBLANES = 8
ROW_TILE = D_MODEL // LANES
assert ROW_TILE == SUBLANES
VMEM_LIMIT = 56 * 1024 * 1024

MIX_TILE = 512
LRU_TAIL_ROWS = 8
CF_TAIL_ROWS = 32
ATT_TILE = 1024
SAMPLE_GROUP = 4
SAMPLE_MIX_GROUP = 32
MOE_BLOCK = 256
DMA_UNROLL = 8
DISPATCH_TILE = 1024
COMBINE_TILE = 512


def _cparams(sem):
    return pltpu.CompilerParams(dimension_semantics=sem, vmem_limit_bytes=VMEM_LIMIT)


def _full(shape):
    n = len(shape)
    return pl.BlockSpec(shape, lambda *_: (0,) * n)


def _rms(x, g):
    return x * lax.rsqrt(jnp.mean(x * x, axis=-1, keepdims=True) + EPS) * g


def _bdot(a, b):
    return jnp.dot(a.astype(BF16), b, preferred_element_type=F32)


def _store_row_tiles(ref, x):
    m = x.shape[0]
    for s in range(ROW_TILE):
        ref[pl.ds(s, m, stride=ROW_TILE), :] = x[:, s * LANES:(s + 1) * LANES]


def _load_row_tiles(ref, m):
    return jnp.concatenate([ref[pl.ds(s, m, stride=ROW_TILE), :] for s in range(ROW_TILE)], axis=-1)


def _row_tile_copy(src_ref, src_row, dst_ref, dst_row, sem, n_rows=1):
    src = src_ref.at[pl.ds(pl.multiple_of(src_row * ROW_TILE, ROW_TILE), n_rows * ROW_TILE)]
    dst = dst_ref.at[pl.ds(pl.multiple_of(dst_row * ROW_TILE, ROW_TILE), n_rows * ROW_TILE)]
    return pltpu.make_async_copy(src, dst, sem)


def _lru_coeffs(xc, wg_ref, bg_ref, lam_ref):
    gl = _bdot(xc, wg_ref[...]) + bg_ref[...]
    r = jax.nn.sigmoid(gl[:, :LRU_WIDTH])
    i = jax.nn.sigmoid(gl[:, LRU_WIDTH:])
    log_a = -LRU_C * r * jax.nn.softplus(-lam_ref[...])
    a = jnp.exp(log_a)
    u = jnp.sqrt(-jnp.tanh(log_a) * (1.0 + a * a)) * (i * xc)
    return a, u


def _cf_post(gc, gln_ref, bln_ref):
    mu = jnp.mean(gc, axis=-1, keepdims=True)
    c = gc - mu
    y = c * lax.rsqrt(jnp.mean(c * c, axis=-1, keepdims=True) + EPS)
    return jax.nn.silu(y * gln_ref[...] + bln_ref[...])


def _mixer_out(x, y_lru, y_cf, wout_ref):
    y = _bdot(y_lru, wout_ref[:LRU_WIDTH, :]) + _bdot(y_cf, wout_ref[LRU_WIDTH:, :])
    return x + y


def _mixer_prompt_kernel(x_ref, gmix_ref, win_ref, wlc_ref, blc_ref, wg_ref, bg_ref, lam_ref,
                         wcc_ref, bcc_ref, gln_ref, bln_ref, wout_ref,
                         x1_ref, h_ref, ltail_ref, ctail_ref,
                         zx_ext, glu_ext, h_carry):
    tt = MIX_TILE
    j = pl.program_id(1)

    @pl.when(j == 0)
    def _():
        zx_ext[0:LRU_TAIL_ROWS, :] = jnp.zeros((LRU_TAIL_ROWS, LRU_WIDTH), F32)
        glu_ext[0:CF_TAIL_ROWS, :] = jnp.zeros((CF_TAIL_ROWS, CF_WIDTH), F32)
        h_carry[...] = jnp.zeros((1, LRU_WIDTH), F32)

    x = x_ref[...]
    z = _bdot(_rms(x, gmix_ref[...]), win_ref[...])
    zx = z[:, :LRU_WIDTH]
    zg = z[:, LRU_WIDTH:2 * LRU_WIDTH]
    za = z[:, 2 * LRU_WIDTH:2 * LRU_WIDTH + CF_WIDTH]
    zb = z[:, 2 * LRU_WIDTH + CF_WIDTH:]

    zx_ext[LRU_TAIL_ROWS:LRU_TAIL_ROWS + tt, :] = zx
    off = LRU_TAIL_ROWS - (LRU_CONV - 1)
    xc = blc_ref[...]
    for k in range(LRU_CONV):
        xc = xc + wlc_ref[k:k + 1, :] * zx_ext[off + k:off + k + tt, :]
    a, u = _lru_coeffs(xc, wg_ref, bg_ref, lam_ref)
    sub = lax.broadcasted_iota(I32, (tt, LRU_WIDTH), 0) & (SUBLANES - 1)
    d = 1
    while d < SUBLANES:
        keep = sub >= d
        a_sh = jnp.where(keep, pltpu.roll(a, d, 0), 1.0)
        u_sh = jnp.where(keep, pltpu.roll(u, d, 0), 0.0)
        u = u + a * u_sh
        a = a * a_sh
        d *= 2
    h_last = h_carry[...]
    groups = []
    for r in range(0, tt, SUBLANES):
        hg = a[r:r + SUBLANES, :] * h_last + u[r:r + SUBLANES, :]
        groups.append(hg)
        h_last = hg[SUBLANES - 1:SUBLANES, :]
    hs = jnp.concatenate(groups, axis=0)
    h_carry[...] = h_last
    y_lru = hs * jax.nn.gelu(zg)

    glu = za * jax.nn.sigmoid(zb)
    glu_ext[CF_TAIL_ROWS:CF_TAIL_ROWS + tt, :] = glu
    off = CF_TAIL_ROWS - (CF_CONV - 1)
    gc = bcc_ref[...]
    for s in range(SUBLANES):
        rows = tt if s == 0 else tt + SUBLANES
        q = None
        for k in range(CF_CONV):
            if (off + k) % SUBLANES == s:
                base = off + k - s
                term = wcc_ref[k:k + 1, :] * glu_ext[base:base + rows, :]
                q = term if q is None else q + term
        gc = gc + (q if s == 0 else q[s:s + tt, :])
    y_cf = _cf_post(gc, gln_ref, bln_ref)

    x1_ref[...] = _mixer_out(x, y_lru, y_cf, wout_ref)

    h_ref[0] = h_last
    ltail_ref[0] = zx_ext[LRU_TAIL_ROWS + tt - (LRU_CONV - 1):LRU_TAIL_ROWS + tt, :]
    ctail_ref[0] = glu_ext[CF_TAIL_ROWS + tt - (CF_CONV - 1):CF_TAIL_ROWS + tt, :]
    zx_ext[0:LRU_TAIL_ROWS, :] = zx_ext[tt:tt + LRU_TAIL_ROWS, :]
    glu_ext[0:CF_TAIL_ROWS, :] = glu_ext[tt:tt + CF_TAIL_ROWS, :]


def _mixer_weight_specs():
    return [
        _full((1, D_MODEL)),
        _full((D_MODEL, IN_WIDTH)),
        _full((LRU_CONV, LRU_WIDTH)), _full((1, LRU_WIDTH)),
        _full((LRU_WIDTH, 2 * LRU_WIDTH)), _full((1, 2 * LRU_WIDTH)),
        _full((1, LRU_WIDTH)),
        _full((CF_CONV, CF_WIDTH)), _full((1, CF_WIDTH)),
        _full((1, CF_WIDTH)), _full((1, CF_WIDTH)),
        _full((D_MODEL, D_MODEL)),
    ]


def _mixer_prompt(x, mix_w, batch, seq):
    tt = MIX_TILE
    nt = seq // tt
    return pl.pallas_call(
        _mixer_prompt_kernel,
        grid=(batch, nt),
        in_specs=[pl.BlockSpec((tt, D_MODEL), lambda b, j: (b * nt + j, 0))] + _mixer_weight_specs(),
        out_specs=[
            pl.BlockSpec((tt, D_MODEL), lambda b, j: (b * nt + j, 0)),
            pl.BlockSpec((1, 1, LRU_WIDTH), lambda b, j: (b, 0, 0)),
            pl.BlockSpec((1, LRU_CONV - 1, LRU_WIDTH), lambda b, j: (b, 0, 0)),
            pl.BlockSpec((1, CF_CONV - 1, CF_WIDTH), lambda b, j: (b, 0, 0)),
        ],
        out_shape=[
            jax.ShapeDtypeStruct((batch * seq, D_MODEL), F32),
            jax.ShapeDtypeStruct((batch, 1, LRU_WIDTH), F32),
            jax.ShapeDtypeStruct((batch, LRU_CONV - 1, LRU_WIDTH), F32),
            jax.ShapeDtypeStruct((batch, CF_CONV - 1, CF_WIDTH), F32),
        ],
        scratch_shapes=[
            pltpu.VMEM((LRU_TAIL_ROWS + tt, LRU_WIDTH), F32),
            pltpu.VMEM((CF_TAIL_ROWS + tt, CF_WIDTH), F32),
            pltpu.VMEM((1, LRU_WIDTH), F32),
        ],
        compiler_params=_cparams(("arbitrary", "arbitrary")),
        name="mixer_prompt",
    )(x, *mix_w)


def _mixer_sample_kernel(x_ref, gmix_ref, win_ref, wlc_ref, blc_ref, wg_ref, bg_ref, lam_ref,
                         wcc_ref, bcc_ref, gln_ref, bln_ref, wout_ref,
                         h0_ref, lbuf_ref, cbuf_ref,
                         x1_ref, h_ref, ltail_ref, ctail_ref):
    x = x_ref[...]
    z = _bdot(_rms(x, gmix_ref[...]), win_ref[...])
    zx = z[:, :LRU_WIDTH]
    zg = z[:, LRU_WIDTH:2 * LRU_WIDTH]
    za = z[:, 2 * LRU_WIDTH:2 * LRU_WIDTH + CF_WIDTH]
    zb = z[:, 2 * LRU_WIDTH + CF_WIDTH:]

    xc = blc_ref[...] + wlc_ref[LRU_CONV - 1:LRU_CONV, :] * zx
    xc = xc + jnp.sum(lbuf_ref[...] * wlc_ref[0:LRU_CONV - 1, :][None], axis=1)
    a, u = _lru_coeffs(xc, wg_ref, bg_ref, lam_ref)
    h = a * h0_ref[...] + u
    y_lru = h * jax.nn.gelu(zg)

    glu = za * jax.nn.sigmoid(zb)
    gc = bcc_ref[...] + wcc_ref[CF_CONV - 1:CF_CONV, :] * glu
    gc = gc + jnp.sum(cbuf_ref[...] * wcc_ref[0:CF_CONV - 1, :][None], axis=1)
    y_cf = _cf_post(gc, gln_ref, bln_ref)

    x1_ref[...] = _mixer_out(x, y_lru, y_cf, wout_ref)
    h_ref[...] = h
    ltail_ref[:, 0:LRU_CONV - 2, :] = lbuf_ref[:, 1:LRU_CONV - 1, :]
    ctail_ref[:, 0:CF_CONV - 2, :] = cbuf_ref[:, 1:CF_CONV - 1, :]
    for i in range(x.shape[0]):
        ltail_ref[i, LRU_CONV - 2:LRU_CONV - 1, :] = zx[i:i + 1, :]
        ctail_ref[i, CF_CONV - 2:CF_CONV - 1, :] = glu[i:i + 1, :]


def _mixer_sample(x, mix_w, h0, lbuf, cbuf):
    n = x.shape[0]
    g = SAMPLE_MIX_GROUP
    row = lambda w: pl.BlockSpec((g, w), lambda i: (i, 0))
    return pl.pallas_call(
        _mixer_sample_kernel,
        grid=(n // g,),
        in_specs=[row(D_MODEL)] + _mixer_weight_specs() + [
            row(LRU_WIDTH),
            pl.BlockSpec((g, LRU_CONV - 1, LRU_WIDTH), lambda i: (i, 0, 0)),
            pl.BlockSpec((g, CF_CONV - 1, CF_WIDTH), lambda i: (i, 0, 0)),
        ],
        out_specs=[row(D_MODEL), row(LRU_WIDTH),
                   pl.BlockSpec((g, LRU_CONV - 1, LRU_WIDTH), lambda i: (i, 0, 0)),
                   pl.BlockSpec((g, CF_CONV - 1, CF_WIDTH), lambda i: (i, 0, 0))],
        out_shape=[
            jax.ShapeDtypeStruct((n, D_MODEL), F32),
            jax.ShapeDtypeStruct((n, LRU_WIDTH), F32),
            jax.ShapeDtypeStruct((n, LRU_CONV - 1, LRU_WIDTH), F32),
            jax.ShapeDtypeStruct((n, CF_CONV - 1, CF_WIDTH), F32),
        ],
        compiler_params=_cparams(("arbitrary",)),
        name="mixer_sample",
    )(x, *mix_w, h0, lbuf, cbuf)


def _memkv_kernel(mem_ref, gmem_ref, wk_ref, wv_ref, k_ref, v_ref, kb_ref, vb_ref):
    mn = _rms(mem_ref[...], gmem_ref[...]).astype(BF16)
    k = jnp.dot(mn, wk_ref[...], preferred_element_type=F32)
    v = jnp.dot(mn, wv_ref[...], preferred_element_type=F32)
    for h in range(XA_HEADS):
        sl = slice(h * XA_HEAD_DIM, (h + 1) * XA_HEAD_DIM)
        k_ref[0, :, h, :] = k[:, sl]
        v_ref[0, :, h, :] = v[:, sl]
    kb_ref[...] = k.astype(BF16)
    vb_ref[...] = v.astype(BF16)


def _memkv(mem, g_mem, wk, wv):
    n = mem.shape[0]
    t = N_MEM
    row = pl.BlockSpec((t, D_MODEL), lambda i: (i, 0))
    state = pl.BlockSpec((1, N_MEM, XA_HEADS, XA_HEAD_DIM), lambda i: (i, 0, 0, 0))
    state_shape = jax.ShapeDtypeStruct((n // t, N_MEM, XA_HEADS, XA_HEAD_DIM), F32)
    return pl.pallas_call(
        _memkv_kernel,
        grid=(n // t,),
        in_specs=[row, _full((1, D_MODEL)), _full((D_MODEL, D_MODEL)), _full((D_MODEL, D_MODEL))],
        out_specs=[state, state, row, row],
        out_shape=[state_shape] * 2 + [jax.ShapeDtypeStruct((n, D_MODEL), BF16)] * 2,
        compiler_params=_cparams(("arbitrary",)),
        name="memory_kv",
    )(mem, g_mem, wk, wv)


def _router(x2, gmoe_ref, wrt_ref, brt_ref, xn_ref, tope_ref, gates_ref):
    xn = _rms(x2, gmoe_ref[...])
    _store_row_tiles(xn_ref, xn)
    logits = lax.dot_general(wrt_ref[...], xn.astype(BF16), (((1,), (1,)), ((), ())),
                             preferred_element_type=F32) + brt_ref[...]
    e_iota = lax.broadcasted_iota(I32, logits.shape, 0)
    work = logits
    vals, idxs = [], []
    for _ in range(TOP_K):
        m = jnp.max(work, axis=0, keepdims=True)
        idx = jnp.min(jnp.where(work == m, e_iota, N_EXPERTS), axis=0, keepdims=True)
        vals.append(m)
        idxs.append(idx)
        work = jnp.where(e_iota == idx, -jnp.inf, work)
    ex = [jnp.exp(v - vals[0]) for v in vals]
    den = ex[0] + ex[1] + ex[2] + ex[3]
    tope_ref[...] = jnp.concatenate(idxs, axis=0)
    gates_ref[...] = jnp.concatenate([e / den for e in ex], axis=0)


def _router_specs():
    return [_full((1, D_MODEL)), _full((N_EXPERTS, D_MODEL)), _full((N_EXPERTS, 1))]


def _router_out(n, tile, index):
    specs = [
        pl.BlockSpec((tile, D_MODEL), lambda *a: (index(*a), 0)),
        pl.BlockSpec((tile * ROW_TILE, LANES), lambda *a: (index(*a), 0)),
        pl.BlockSpec((TOP_K, tile), lambda *a: (0, index(*a))),
        pl.BlockSpec((TOP_K, tile), lambda *a: (0, index(*a))),
    ]
    shapes = [
        jax.ShapeDtypeStruct((n, D_MODEL), F32),
        jax.ShapeDtypeStruct((n * ROW_TILE, LANES), F32),
        jax.ShapeDtypeStruct((TOP_K, n), I32),
        jax.ShapeDtypeStruct((TOP_K, n), F32),
    ]
    return specs, shapes


def _attn_prompt_kernel(x1_ref, k_ref, v_ref, gxa_ref, wq_ref, wo_ref, gmoe_ref, wrt_ref, brt_ref,
                        x2_ref, xn_ref, tope_ref, gates_ref):
    x1 = x1_ref[...]
    q = (_bdot(_rms(x1, gxa_ref[...]), wq_ref[...]) * (XA_HEAD_DIM ** -0.5)).astype(BF16)
    outs = []
    for h in range(XA_HEADS):
        sl = slice(h * XA_HEAD_DIM, (h + 1) * XA_HEAD_DIM)
        s = lax.dot_general(q[:, sl], k_ref[:, sl], (((1,), (1,)), ((), ())), preferred_element_type=F32)
        p = jnp.exp(s - jnp.max(s, axis=-1, keepdims=True))
        p = p / jnp.sum(p, axis=-1, keepdims=True)
        outs.append(_bdot(p, v_ref[:, sl]))
    o = jnp.concatenate(outs, axis=-1)
    x2 = x1 + _bdot(o, wo_ref[...])
    x2_ref[...] = x2
    _router(x2, gmoe_ref, wrt_ref, brt_ref, xn_ref, tope_ref, gates_ref)


def _attn_prompt(x1, k, v, gxa, wq, wo, router_w, batch, seq):
    tq = ATT_TILE
    nt = seq // tq
    out_specs, out_shapes = _router_out(batch * seq, tq, lambda b, j: b * nt + j)
    kv = pl.BlockSpec((N_MEM, D_MODEL), lambda b, j: (b, 0))
    return pl.pallas_call(
        _attn_prompt_kernel,
        grid=(batch, nt),
        in_specs=[pl.BlockSpec((tq, D_MODEL), lambda b, j: (b * nt + j, 0)), kv, kv,
                  _full((1, D_MODEL)), _full((D_MODEL, D_MODEL)), _full((D_MODEL, D_MODEL))] + _router_specs(),
        out_specs=out_specs,
        out_shape=out_shapes,
        compiler_params=_cparams(("arbitrary", "arbitrary")),
        name="attn_prompt",
    )(x1, k, v, gxa, wq, wo, *router_w)


def _qproj_kernel(x1_ref, gxa_ref, wq_ref, q_ref):
    q_ref[...] = _bdot(_rms(x1_ref[...], gxa_ref[...]), wq_ref[...]) * (XA_HEAD_DIM ** -0.5)


def _attn_sample_core_kernel(q_ref, k_ref, v_ref, o_ref):
    n = N_MEM * SUBLANES
    col_head = lax.broadcasted_iota(I32, (SUBLANES, n), 1) & (SUBLANES - 1)
    same_head = col_head == lax.broadcasted_iota(I32, (SUBLANES, n), 0)
    kv_pad = jnp.zeros((N_MEM, SUBLANES - XA_HEADS, XA_HEAD_DIM), F32)
    q_pad = jnp.zeros((SUBLANES - XA_HEADS, XA_HEAD_DIM), F32)
    for g in range(q_ref.shape[0]):
        k8 = jnp.concatenate([k_ref[g], kv_pad], axis=1).reshape(n, XA_HEAD_DIM).astype(BF16)
        v8 = jnp.concatenate([v_ref[g], kv_pad], axis=1).reshape(n, XA_HEAD_DIM).astype(BF16)
        q8 = jnp.concatenate([q_ref[g], q_pad], axis=0).astype(BF16)
        s = lax.dot_general(q8, k8, (((1,), (1,)), ((), ())), preferred_element_type=F32)
        m = jnp.max(jnp.where(same_head, s, -jnp.inf), axis=-1, keepdims=True)
        p = jnp.where(same_head, jnp.exp(s - m), 0.0)
        den = jnp.sum(p, axis=-1, keepdims=True)
        o8 = jnp.dot(p.astype(BF16), v8, preferred_element_type=F32)
        o_ref[g] = (o8 / den)[0:XA_HEADS, :]


def _oproj_router_kernel(x1_ref, o_ref, wo_ref, gmoe_ref, wrt_ref, brt_ref,
                         x2_ref, xn_ref, tope_ref, gates_ref):
    x2 = x1_ref[...] + _bdot(o_ref[...], wo_ref[...])
    x2_ref[...] = x2
    _router(x2, gmoe_ref, wrt_ref, brt_ref, xn_ref, tope_ref, gates_ref)


def _attn_sample(x1, k, v, gxa, wq, wo, router_w):
    n = x1.shape[0]
    g = SAMPLE_GROUP
    q = pl.pallas_call(
        _qproj_kernel,
        grid=(1,),
        in_specs=[_full((n, D_MODEL)), _full((1, D_MODEL)), _full((D_MODEL, D_MODEL))],
        out_specs=_full((n, D_MODEL)),
        out_shape=jax.ShapeDtypeStruct((n, D_MODEL), F32),
        compiler_params=_cparams(("arbitrary",)),
        name="qproj_sample",
    )(x1, gxa, wq)
    kv = pl.BlockSpec((g, N_MEM, XA_HEADS, XA_HEAD_DIM), lambda i: (i, 0, 0, 0))
    row = pl.BlockSpec((g, XA_HEADS, XA_HEAD_DIM), lambda i: (i, 0, 0))
    o = pl.pallas_call(
        _attn_sample_core_kernel,
        grid=(n // g,),
        in_specs=[row, kv, kv],
        out_specs=row,
        out_shape=jax.ShapeDtypeStruct((n, XA_HEADS, XA_HEAD_DIM), F32),
        compiler_params=_cparams(("arbitrary",)),
        name="attn_sample_core",
    )(q.reshape(n, XA_HEADS, XA_HEAD_DIM), k, v).reshape(n, D_MODEL)
    out_specs, out_shapes = _router_out(n, n, lambda i: 0)
    return pl.pallas_call(
        _oproj_router_kernel,
        grid=(1,),
        in_specs=[_full((n, D_MODEL)), _full((n, D_MODEL)), _full((D_MODEL, D_MODEL))] + _router_specs(),
        out_specs=out_specs,
        out_shape=out_shapes,
        compiler_params=_cparams(("arbitrary",)),
        name="oproj_router_sample",
    )(x1, o, wo, *router_w)


def _moe_num_blocks(n_assign):
    return pl.cdiv(n_assign, MOE_BLOCK) + N_EXPERTS


def _slots_kernel(tp_ref, ts_ref, destp_ref, dests_ref, blke_ref, nact_ref, cnt_ref, pstart_ref,
                  rankp, ranks):
    c = LANES
    r_i = lax.broadcasted_iota(I32, (c, c), 0)
    c_i = lax.broadcasted_iota(I32, (c, c), 1)
    upper = (r_i < c_i).astype(BF16)
    e_iota = lax.broadcasted_iota(I32, (N_EXPERTS, c), 0)

    def make_body(top_ref, rank_ref):
        def body(ci, carry):
            lo = pl.multiple_of(ci * c, c)
            top = top_ref[:, pl.ds(lo, c)]
            hit = [e_iota == top[k:k + 1, :] for k in range(TOP_K)]
            cnt = sum(m.astype(F32) for m in hit)
            base = carry + jnp.dot(cnt.astype(BF16), upper, preferred_element_type=F32)
            rank_ref[:, pl.ds(lo, c)] = jnp.concatenate(
                [jnp.sum(jnp.where(m, base, 0.0), axis=0, keepdims=True) for m in hit], axis=0)
            return carry + jnp.sum(cnt, axis=1, keepdims=True)
        return body

    counts = jnp.zeros((N_EXPERTS, 1), F32)
    counts = lax.fori_loop(0, tp_ref.shape[1] // c, make_body(tp_ref, rankp), counts)
    counts = lax.fori_loop(0, ts_ref.shape[1] // c, make_body(ts_ref, ranks), counts)

    padded = jnp.floor((counts + (MOE_BLOCK - 1)) / MOE_BLOCK) * MOE_BLOCK
    er = lax.broadcasted_iota(I32, (N_EXPERTS, N_EXPERTS), 0)
    ec = lax.broadcasted_iota(I32, (N_EXPERTS, N_EXPERTS), 1)
    padded_row = jnp.sum(jnp.where(er == ec, padded, 0.0), axis=0, keepdims=True)
    cum = jnp.sum(jnp.where(ec <= er, padded_row, 0.0), axis=1, keepdims=True)
    pstart = cum - padded
    cnt_ref[...] = jnp.sum(jnp.where(er == ec, counts, 0.0), axis=0, keepdims=True).astype(I32)
    pstart_ref[...] = jnp.sum(jnp.where(er == ec, pstart, 0.0), axis=0, keepdims=True).astype(I32)

    def dest(top_ref, rank_ref, out_ref):
        n = top_ref.shape[1]
        ei = lax.broadcasted_iota(I32, (N_EXPERTS, n), 0)
        rows = []
        for k in range(TOP_K):
            start = jnp.sum(jnp.where(ei == top_ref[k:k + 1, :], pstart, 0.0), axis=0, keepdims=True)
            rows.append(start + rank_ref[k:k + 1, :])
        out_ref[...] = jnp.concatenate(rows, axis=0).astype(I32)

    dest(tp_ref, rankp, destp_ref)
    dest(ts_ref, ranks, dests_ref)

    nb = blke_ref.shape[1]
    blk_lo = lax.broadcasted_iota(I32, (N_EXPERTS, nb), 1).astype(F32) * MOE_BLOCK
    blk = jnp.sum((cum <= blk_lo).astype(F32), axis=0, keepdims=True)
    blke_ref[...] = jnp.minimum(blk, N_EXPERTS - 1).astype(I32)
    nact_ref[...] = (cum[N_EXPERTS - 1:N_EXPERTS, :] / MOE_BLOCK).astype(I32)


def _slots(top_p, top_s):
    n_p, n_s = top_p.shape[1], top_s.shape[1]
    nb = _moe_num_blocks((n_p + n_s) * TOP_K)
    return pl.pallas_call(
        _slots_kernel,
        grid=(1,),
        in_specs=[_full((TOP_K, n_p)), _full((TOP_K, n_s))],
        out_specs=[_full((TOP_K, n_p)), _full((TOP_K, n_s)), _full((1, nb)), _full((1, 1)),
                   _full((1, N_EXPERTS)), _full((1, N_EXPERTS))],
        out_shape=[
            jax.ShapeDtypeStruct((TOP_K, n_p), I32),
            jax.ShapeDtypeStruct((TOP_K, n_s), I32),
            jax.ShapeDtypeStruct((1, nb), I32),
            jax.ShapeDtypeStruct((1, 1), I32),
            jax.ShapeDtypeStruct((1, N_EXPERTS), I32),
            jax.ShapeDtypeStruct((1, N_EXPERTS), I32),
        ],
        scratch_shapes=[pltpu.VMEM((TOP_K, n_p), F32), pltpu.VMEM((TOP_K, n_s), F32)],
        compiler_params=_cparams(("arbitrary",)),
        name="moe_slots",
    )(top_p, top_s)


def _for_rows(n_rows, fn):
    def body(i, c):
        for u in range(DMA_UNROLL):
            fn(i * DMA_UNROLL + u)
        return c
    lax.fori_loop(0, n_rows // DMA_UNROLL, body, 0)


def _dispatch_kernel(cnt_ref, pstart_ref, nact_ref, destp_ref, dests_ref, xnp_ref, xns_ref, xs_ref,
                     zeros, sem, zsem):
    i = pl.program_id(0)
    last = pl.num_programs(0) - 1

    def scatter(x_ref, dest_ref):
        n = x_ref.shape[0] // ROW_TILE

        def one(t):
            for k in range(TOP_K):
                _row_tile_copy(x_ref, t, xs_ref, dest_ref[t * TOP_K + k], sem).start(priority=k % 2)

        _for_rows(n, one)
        _row_tile_copy(xs_ref, 0, xs_ref, 0, sem, n * TOP_K).wait()

    @pl.when(i < last)
    def _():
        scatter(xnp_ref, destp_ref)

    @pl.when(i == last)
    def _():
        scatter(xns_ref, dests_ref)
        zeros[...] = jnp.zeros(zeros.shape, F32)

        def pad_copies(e, act):
            c = cnt_ref[e]
            npad = (-c) & (MOE_BLOCK - 1)
            base = pstart_ref[e] + c
            for bit in range(MOE_BLOCK.bit_length() - 1):
                size = 1 << bit

                @pl.when((npad & size) != 0)
                def _():
                    act(_row_tile_copy(zeros, 0, xs_ref, base + (npad & (size - 1)), zsem, size))

        def tail_copies(act):
            def body(j, c):
                act(_row_tile_copy(zeros, 0, xs_ref, j * MOE_BLOCK, zsem, MOE_BLOCK))
                return c
            lax.fori_loop(nact_ref[0], xs_ref.shape[0] // (MOE_BLOCK * ROW_TILE), body, 0)

        for act in (lambda d: d.start(), lambda d: d.wait()):
            lax.fori_loop(0, N_EXPERTS, lambda e, c, act=act: (pad_copies(e, act), c)[1], 0)
            tail_copies(act)


def _dispatch(cnt, pstart, nact, dest_p, dest_s, xn_p, xn_s, n_rows):
    n_p, n_s = xn_p.shape[0] // ROW_TILE, xn_s.shape[0] // ROW_TILE
    tile = DISPATCH_TILE
    nt = n_p // tile
    smem = pl.BlockSpec(memory_space=pltpu.SMEM)
    return pl.pallas_call(
        _dispatch_kernel,
        grid=(nt + 1,),
        in_specs=[
            smem, smem, smem,
            pl.BlockSpec((TOP_K * tile,), lambda i: (jnp.minimum(i, nt - 1),), memory_space=pltpu.SMEM),
            smem,
            pl.BlockSpec((tile * ROW_TILE, LANES), lambda i: (jnp.minimum(i, nt - 1), 0)),
            _full((n_s * ROW_TILE, LANES)),
        ],
        out_specs=pl.BlockSpec(memory_space=pl.ANY),
        out_shape=jax.ShapeDtypeStruct((n_rows * ROW_TILE, LANES), F32),
        scratch_shapes=[pltpu.VMEM((MOE_BLOCK * ROW_TILE, LANES), F32), pltpu.SemaphoreType.DMA,
                        pltpu.SemaphoreType.DMA],
        compiler_params=_cparams(("arbitrary",)),
        name="moe_dispatch",
    )(cnt, pstart, nact, dest_p, dest_s, xn_p, xn_s)


def _combine_kernel(dest_ref, dest_next_ref, gates_ref, x2_ref, gfin_ref, ys_ref, y_ref, bufs, sems):
    tile = x2_ref.shape[0]
    i = pl.program_id(0)
    slot = i % 2

    def gather(d_ref, b):
        def one(t):
            for k in range(TOP_K):
                _row_tile_copy(ys_ref, d_ref[t * TOP_K + k], bufs.at[b, k], t, sems.at[b]).start(priority=k % 2)
        _for_rows(tile, one)

    @pl.when(i == 0)
    def _():
        gather(dest_ref, 0)

    for b in range(2):
        @pl.when(slot == b)
        def _():
            pltpu.make_async_copy(bufs.at[b], bufs.at[b], sems.at[b]).wait()

            @pl.when(i + 1 < pl.num_programs(0))
            def _():
                gather(dest_next_ref, 1 - b)

    buf = bufs.at[slot]
    sumsq = jnp.zeros((tile, 1), F32)
    gate = [jnp.broadcast_to(gates_ref[:, k:k + 1], (tile, LANES)) for k in range(TOP_K)]
    for s in range(ROW_TILE):
        cols = slice(s * LANES, (s + 1) * LANES)
        acc = x2_ref[:, cols]
        for k in range(TOP_K):
            acc = acc + gate[k] * buf[k, pl.ds(s, tile, stride=ROW_TILE), :]
        y_ref[:, cols] = acc
        sumsq = sumsq + jnp.sum(acc * acc, axis=-1, keepdims=True)
    scale = lax.rsqrt(sumsq / D_MODEL + EPS)
    for s in range(ROW_TILE):
        cols = slice(s * LANES, (s + 1) * LANES)
        y_ref[:, cols] = y_ref[:, cols] * scale * gfin_ref[:, cols]


def _combine(dest, gates_t, x2, g_final, ys):
    n = x2.shape[0]
    tile = min(COMBINE_TILE, n)
    row = pl.BlockSpec((tile, D_MODEL), lambda i: (i, 0))
    nt = n // tile
    return pl.pallas_call(
        _combine_kernel,
        grid=(nt,),
        in_specs=[
            pl.BlockSpec((TOP_K * tile,), lambda i: (i,), memory_space=pltpu.SMEM),
            pl.BlockSpec((TOP_K * tile,), lambda i: (jnp.minimum(i + 1, nt - 1),), memory_space=pltpu.SMEM),
            pl.BlockSpec((tile, TOP_K), lambda i: (i, 0)),
            row,
            _full((1, D_MODEL)),
            pl.BlockSpec(memory_space=pl.ANY),
        ],
        out_specs=row,
        out_shape=jax.ShapeDtypeStruct((n, D_MODEL), F32),
        scratch_shapes=[pltpu.VMEM((2, TOP_K, tile * ROW_TILE, LANES), F32), pltpu.SemaphoreType.DMA((2,))],
        compiler_params=_cparams(("arbitrary",)),
        name="moe_combine",
    )(dest, dest, gates_t, x2, g_final, ys)


def _experts_kernel(blke_ref, nact_ref, xs_ref, w1_hbm, b1_ref, w2_hbm, b2_ref, ys_ref,
                    w1f, w2f, w1b, w2b, wsem, started):
    j = pl.program_id(0)
    nact = nact_ref[0]
    active = j < nact
    e = blke_ref[j]
    new_expert = jnp.logical_or(j == 0, e != blke_ref[jnp.maximum(j - 1, 0)])

    def fetch(expert, slot):
        return (pltpu.make_async_copy(w1_hbm.at[expert], w1f.at[slot], wsem.at[slot]),
                pltpu.make_async_copy(w2_hbm.at[expert], w2f.at[slot], wsem.at[slot]))

    @pl.when(j == 0)
    def _():
        started[0] = 0
        for d in fetch(e, 0):
            d.start()

    @pl.when(jnp.logical_and(active, new_expert))
    def _():
        n = started[0]
        last = blke_ref.shape[0] - 1
        jn = lax.while_loop(lambda jj: jnp.logical_and(jj < nact, blke_ref[jnp.minimum(jj, last)] == e),
                            lambda jj: jj + 1, j + 1)
        for b in range(2):
            @pl.when(n % 2 == b)
            def _():
                for d in fetch(e, b):
                    d.wait()

                @pl.when(jn < nact)
                def _():
                    for d in fetch(blke_ref[jnp.minimum(jn, last)], 1 - b):
                        d.start(priority=1)

                w1b[...] = w1f[b].astype(BF16)
                w2b[...] = w2f[b].astype(BF16)
        started[0] = n + 1

    @pl.when(active)
    def _():
        x = _load_row_tiles(xs_ref, MOE_BLOCK).astype(BF16)
        gu = jnp.dot(x, w1b[...], preferred_element_type=F32) + b1_ref[0]
        g = jnp.minimum(gu[:, :D_EXPERT], SWIGLU_LIMIT)
        u = jnp.clip(gu[:, D_EXPERT:], -SWIGLU_LIMIT, SWIGLU_LIMIT)
        h = (u + 1.0) * (g * jax.nn.sigmoid(SWIGLU_ALPHA * g))
        _store_row_tiles(ys_ref, _bdot(h, w2b[...]) + b2_ref[0])

    @pl.when(jnp.logical_not(active))
    def _():
        ys_ref[...] = jnp.zeros(ys_ref.shape, F32)


def _experts(blk_e, nact, xs, w1, b1, w2, b2):
    nb = xs.shape[0] // (MOE_BLOCK * ROW_TILE)
    blk = lambda j, be, na: (jnp.minimum(j, na[0] - 1), 0)
    exp = lambda j, be, na: (be[jnp.minimum(j, na[0] - 1)], 0, 0)
    rows = (MOE_BLOCK * ROW_TILE, LANES)
    return pl.pallas_call(
        _experts_kernel,
        grid_spec=pltpu.PrefetchScalarGridSpec(
            num_scalar_prefetch=2,
            grid=(nb,),
            in_specs=[
                pl.BlockSpec(rows, blk),
                pl.BlockSpec(memory_space=pl.ANY),
                pl.BlockSpec((1, 1, 2 * D_EXPERT), exp),
                pl.BlockSpec(memory_space=pl.ANY),
                pl.BlockSpec((1, 1, D_MODEL), exp),
            ],
            out_specs=pl.BlockSpec(rows, lambda j, be, na: (j, 0)),
            scratch_shapes=[
                pltpu.VMEM((2, D_MODEL, 2 * D_EXPERT), F32), pltpu.VMEM((2, D_EXPERT, D_MODEL), F32),
                pltpu.VMEM((D_MODEL, 2 * D_EXPERT), BF16), pltpu.VMEM((D_EXPERT, D_MODEL), BF16),
                pltpu.SemaphoreType.DMA((2,)), pltpu.SMEM((1,), I32),
            ],
        ),
        out_shape=jax.ShapeDtypeStruct(xs.shape, F32),
        compiler_params=_cparams(("arbitrary",)),
        name="moe_experts",
    )(blk_e, nact, xs, w1, b1, w2, b2)


def _block_diag(w):
    h, d, _ = w.shape
    eye = jnp.eye(h, dtype=w.dtype)
    return (eye[:, None, :, None] * w[:, :, None, :]).reshape(h * d, h * d)


def kernel(x_prompt, x_sample, state_lru_h, cache_lru_conv, cache_cf_conv, cache_mem_k, cache_mem_v,
           mem_prompt, g_mix, w_in, w_lru_conv, b_lru_conv, w_lru_a, b_lru_a, w_lru_x, b_lru_x,
           lru_lambda, w_cf_conv, b_cf_conv, g_cf_ln, b_cf_ln, w_out, g_xa, g_mem, w_q, w_k, w_v, w_o,
           g_moe, w_router, b_router, w_e1, b_e1, w_e2, b_e2, g_final):
    depth = g_mix.shape[0]
    assert depth == 1, "single-layer stack"
    batch, seq, _ = x_prompt.shape
    n_s = x_sample.shape[0]
    assert x_sample.shape[1] == 1
    n_p = batch * seq
    l = 0

    mix_w = (
        g_mix[l][None], w_in[l].astype(BF16), w_lru_conv[l], b_lru_conv[l][None],
        jnp.concatenate([_block_diag(w_lru_a[l]), _block_diag(w_lru_x[l])], axis=1).astype(BF16),
        jnp.concatenate([b_lru_a[l], b_lru_x[l]])[None], lru_lambda[l][None],
        w_cf_conv[l], b_cf_conv[l][None], g_cf_ln[l][None], b_cf_ln[l][None], w_out[l].astype(BF16),
    )
    router_w = (g_moe[l][None], w_router[l].T.astype(BF16), b_router[l][:, None])
    wq, wo = w_q[l].astype(BF16), w_o[l].astype(BF16)
    gxa = g_xa[l][None]

    xp = x_prompt.reshape(n_p, D_MODEL)
    xs_ = x_sample.reshape(n_s, D_MODEL)
    x1p, hp, lcp, ccp = _mixer_prompt(xp, mix_w, batch, seq)
    x1s, hs, lru_conv_s, cf_conv_s = _mixer_sample(xs_, mix_w, state_lru_h[l], cache_lru_conv[l], cache_cf_conv[l])

    mk, mv, mkb, mvb = _memkv(mem_prompt.reshape(batch * N_MEM, D_MODEL), g_mem[l][None],
                              w_k[l].astype(BF16), w_v[l].astype(BF16))
    x2p, xnp_, top_p, gates_p = _attn_prompt(x1p, mkb, mvb, gxa, wq, wo, router_w, batch, seq)
    x2s, xns, top_s, gates_s = _attn_sample(x1s, cache_mem_k[l], cache_mem_v[l], gxa, wq, wo, router_w)

    dest_p, dest_s, blk_e, nact, cnt, pstart = _slots(top_p, top_s)
    dest_p, dest_s = dest_p.T.reshape(-1), dest_s.T.reshape(-1)
    nb = _moe_num_blocks((n_p + n_s) * TOP_K)
    nact = nact.reshape(1)
    xs = _dispatch(cnt.reshape(N_EXPERTS), pstart.reshape(N_EXPERTS), nact, dest_p, dest_s, xnp_, xns,
                   nb * MOE_BLOCK)
    ys = _experts(blk_e.reshape(nb), nact, xs, w_e1[l], b_e1[l][:, None, :], w_e2[l], b_e2[l][:, None, :])
    gfin = g_final[None]
    y_p = _combine(dest_p, gates_p.T, x2p, gfin, ys)
    y_s = _combine(dest_s, gates_s.T, x2s, gfin, ys)

    return (
        y_p.reshape(batch, seq, D_MODEL),
        y_s.reshape(n_s, 1, D_MODEL),
        hp.reshape(depth, batch, LRU_WIDTH),
        lcp[None],
        ccp[None],
        mk[None],
        mv[None],
        hs[None],
        lru_conv_s[None],
        cf_conv_s[None],
    )
```

```python
import functools

import jax
import jax.numpy as jnp
from jax import lax
from jax.experimental import pallas as pl
from jax.experimental.pallas import tpu as pltpu

F32 = jnp.float32
BF16 = jnp.bfloat16
I32 = jnp.int32

D_MODEL = 1024
LRU_WIDTH = 512
CF_WIDTH = 512
LRU_HEADS = 8
LRU_CONV = 4
LRU_C = 8.0
CF_CONV = 31
IN_WIDTH = 2 * LRU_WIDTH + 2 * CF_WIDTH
N_MEM = 256
XA_HEADS = 4
XA_HEAD_DIM = D_MODEL // XA_HEADS
N_EXPERTS = 32
TOP_K = 4
D_EXPERT = D_MODEL
SWIGLU_LIMIT = 7.0
SWIGLU_ALPHA = 1.702
EPS = 1e-6

LANES = 128
SUBLANES = 8
ROW_TILE = D_MODEL // LANES
assert ROW_TILE == SUBLANES
VMEM_LIMIT = 56 * 1024 * 1024

MIX_TILE = 1024
LRU_TAIL_ROWS = 8
CF_TAIL_ROWS = 32
ATT_TILE = 1024
SAMPLE_GROUP = 4
SAMPLE_MIX_GROUP = 32
MOE_BLOCK = 256
DMA_UNROLL = 8
DISPATCH_TILE = 2048
COMBINE_TILE = 256


def _cparams(sem):
    return pltpu.CompilerParams(dimension_semantics=sem, vmem_limit_bytes=VMEM_LIMIT)


def _full(shape):
    n = len(shape)
    return pl.BlockSpec(shape, lambda *_: (0,) * n)


def _rms(x, g):
    return x * lax.rsqrt(jnp.mean(x * x, axis=-1, keepdims=True) + EPS) * g


def _bdot(a, b):
    return jnp.dot(a.astype(BF16), b, preferred_element_type=F32)


def _store_row_tiles(ref, x):
    m = x.shape[0]
    for s in range(ROW_TILE):
        ref[pl.ds(s, m, stride=ROW_TILE), :] = x[:, s * LANES:(s + 1) * LANES]


def _load_row_tiles(ref, m):
    return jnp.concatenate([ref[pl.ds(s, m, stride=ROW_TILE), :] for s in range(ROW_TILE)], axis=-1)


def _row_tile_copy(src_ref, src_row, dst_ref, dst_row, sem, n_rows=1):
    src = src_ref.at[pl.ds(pl.multiple_of(src_row * ROW_TILE, ROW_TILE), n_rows * ROW_TILE)]
    dst = dst_ref.at[pl.ds(pl.multiple_of(dst_row * ROW_TILE, ROW_TILE), n_rows * ROW_TILE)]
    return pltpu.make_async_copy(src, dst, sem)


def _lru_coeffs(xc, wg_ref, bg_ref, lam_ref):
    gl = _bdot(xc, wg_ref[...]) + bg_ref[...]
    r = jax.nn.sigmoid(gl[:, :LRU_WIDTH])
    i = jax.nn.sigmoid(gl[:, LRU_WIDTH:])
    log_a = -LRU_C * r * jax.nn.softplus(-lam_ref[...])
    a = jnp.exp(log_a)
    u = jnp.sqrt(-jnp.tanh(log_a) * (1.0 + a * a)) * (i * xc)
    return a, u


def _cf_post(gc, gln_ref, bln_ref):
    mu = jnp.mean(gc, axis=-1, keepdims=True)
    c = gc - mu
    y = c * lax.rsqrt(jnp.mean(c * c, axis=-1, keepdims=True) + EPS)
    return jax.nn.silu(y * gln_ref[...] + bln_ref[...])


def _mixer_out(x, y_lru, y_cf, wout_ref):
    y = _bdot(y_lru, wout_ref[:LRU_WIDTH, :]) + _bdot(y_cf, wout_ref[LRU_WIDTH:, :])
    return x + y


def _mixer_prompt_kernel(x_ref, gmix_ref, win_ref, wlc_ref, blc_ref, wg_ref, bg_ref, lam_ref,
                         wcc_ref, bcc_ref, gln_ref, bln_ref, wout_ref,
                         x1_ref, h_ref, ltail_ref, ctail_ref,
                         zx_ext, glu_ext, h_carry):
    tt = MIX_TILE
    j = pl.program_id(1)

    @pl.when(j == 0)
    def _():
        zx_ext[0:LRU_TAIL_ROWS, :] = jnp.zeros((LRU_TAIL_ROWS, LRU_WIDTH), F32)
        glu_ext[0:CF_TAIL_ROWS, :] = jnp.zeros((CF_TAIL_ROWS, CF_WIDTH), F32)
        h_carry[...] = jnp.zeros((1, LRU_WIDTH), F32)

    x = x_ref[...]
    z = _bdot(_rms(x, gmix_ref[...]), win_ref[...])
    zx = z[:, :LRU_WIDTH]
    zg = z[:, LRU_WIDTH:2 * LRU_WIDTH]
    za = z[:, 2 * LRU_WIDTH:2 * LRU_WIDTH + CF_WIDTH]
    zb = z[:, 2 * LRU_WIDTH + CF_WIDTH:]

    zx_ext[LRU_TAIL_ROWS:LRU_TAIL_ROWS + tt, :] = zx
    off = LRU_TAIL_ROWS - (LRU_CONV - 1)
    xc = blc_ref[...]
    for k in range(LRU_CONV):
        xc = xc + wlc_ref[k:k + 1, :] * zx_ext[off + k:off + k + tt, :]
    a, u = _lru_coeffs(xc, wg_ref, bg_ref, lam_ref)
    sub = lax.broadcasted_iota(I32, (tt, LRU_WIDTH), 0) & (SUBLANES - 1)
    d = 1
    while d < SUBLANES:
        keep = sub >= d
        a_sh = jnp.where(keep, pltpu.roll(a, d, 0), 1.0)
        u_sh = jnp.where(keep, pltpu.roll(u, d, 0), 0.0)
        u = u + a * u_sh
        a = a * a_sh
        d *= 2
    h_last = h_carry[...]
    groups = []
    for r in range(0, tt, SUBLANES):
        hg = a[r:r + SUBLANES, :] * h_last + u[r:r + SUBLANES, :]
        groups.append(hg)
        h_last = hg[SUBLANES - 1:SUBLANES, :]
    hs = jnp.concatenate(groups, axis=0)
    h_carry[...] = h_last
    y_lru = hs * jax.nn.gelu(zg)

    glu = za * jax.nn.sigmoid(zb)
    glu_ext[CF_TAIL_ROWS:CF_TAIL_ROWS + tt, :] = glu
    off = CF_TAIL_ROWS - (CF_CONV - 1)
    gc = bcc_ref[...]
    for s in range(SUBLANES):
        rows = tt if s == 0 else tt + SUBLANES
        q = None
        for k in range(CF_CONV):
            if (off + k) % SUBLANES == s:
                base = off + k - s
                term = wcc_ref[k:k + 1, :] * glu_ext[base:base + rows, :]
                q = term if q is None else q + term
        gc = gc + (q if s == 0 else q[s:s + tt, :])
    y_cf = _cf_post(gc, gln_ref, bln_ref)

    x1_ref[...] = _mixer_out(x, y_lru, y_cf, wout_ref)

    h_ref[0] = h_last
    ltail_ref[0] = zx_ext[LRU_TAIL_ROWS + tt - (LRU_CONV - 1):LRU_TAIL_ROWS + tt, :]
    ctail_ref[0] = glu_ext[CF_TAIL_ROWS + tt - (CF_CONV - 1):CF_TAIL_ROWS + tt, :]
    zx_ext[0:LRU_TAIL_ROWS, :] = zx_ext[tt:tt + LRU_TAIL_ROWS, :]
    glu_ext[0:CF_TAIL_ROWS, :] = glu_ext[tt:tt + CF_TAIL_ROWS, :]


def _mixer_weight_specs():
    return [
        _full((1, D_MODEL)),
        _full((D_MODEL, IN_WIDTH)),
        _full((LRU_CONV, LRU_WIDTH)), _full((1, LRU_WIDTH)),
        _full((LRU_WIDTH, 2 * LRU_WIDTH)), _full((1, 2 * LRU_WIDTH)),
        _full((1, LRU_WIDTH)),
        _full((CF_CONV, CF_WIDTH)), _full((1, CF_WIDTH)),
        _full((1, CF_WIDTH)), _full((1, CF_WIDTH)),
        _full((D_MODEL, D_MODEL)),
    ]


def _mixer_prompt(x, mix_w, batch, seq):
    tt = MIX_TILE
    nt = seq // tt
    return pl.pallas_call(
        _mixer_prompt_kernel,
        grid=(batch, nt),
        in_specs=[pl.BlockSpec((tt, D_MODEL), lambda b, j: (b * nt + j, 0))] + _mixer_weight_specs(),
        out_specs=[
            pl.BlockSpec((tt, D_MODEL), lambda b, j: (b * nt + j, 0)),
            pl.BlockSpec((1, 1, LRU_WIDTH), lambda b, j: (b, 0, 0)),
            pl.BlockSpec((1, LRU_CONV - 1, LRU_WIDTH), lambda b, j: (b, 0, 0)),
            pl.BlockSpec((1, CF_CONV - 1, CF_WIDTH), lambda b, j: (b, 0, 0)),
        ],
        out_shape=[
            jax.ShapeDtypeStruct((batch * seq, D_MODEL), F32),
            jax.ShapeDtypeStruct((batch, 1, LRU_WIDTH), F32),
            jax.ShapeDtypeStruct((batch, LRU_CONV - 1, LRU_WIDTH), F32),
            jax.ShapeDtypeStruct((batch, CF_CONV - 1, CF_WIDTH), F32),
        ],
        scratch_shapes=[
            pltpu.VMEM((LRU_TAIL_ROWS + tt, LRU_WIDTH), F32),
            pltpu.VMEM((CF_TAIL_ROWS + tt, CF_WIDTH), F32),
            pltpu.VMEM((1, LRU_WIDTH), F32),
        ],
        compiler_params=_cparams(("arbitrary", "arbitrary")),
        name="mixer_prompt",
    )(x, *mix_w)


def _mixer_sample_kernel(x_ref, gmix_ref, win_ref, wlc_ref, blc_ref, wg_ref, bg_ref, lam_ref,
                         wcc_ref, bcc_ref, gln_ref, bln_ref, wout_ref,
                         h0_ref, lbuf_ref, cbuf_ref,
                         x1_ref, h_ref, ltail_ref, ctail_ref):
    x = x_ref[...]
    z = _bdot(_rms(x, gmix_ref[...]), win_ref[...])
    zx = z[:, :LRU_WIDTH]
    zg = z[:, LRU_WIDTH:2 * LRU_WIDTH]
    za = z[:, 2 * LRU_WIDTH:2 * LRU_WIDTH + CF_WIDTH]
    zb = z[:, 2 * LRU_WIDTH + CF_WIDTH:]

    xc = blc_ref[...] + wlc_ref[LRU_CONV - 1:LRU_CONV, :] * zx
    xc = xc + jnp.sum(lbuf_ref[...] * wlc_ref[0:LRU_CONV - 1, :][None], axis=1)
    a, u = _lru_coeffs(xc, wg_ref, bg_ref, lam_ref)
    h = a * h0_ref[...] + u
    y_lru = h * jax.nn.gelu(zg)

    glu = za * jax.nn.sigmoid(zb)
    gc = bcc_ref[...] + wcc_ref[CF_CONV - 1:CF_CONV, :] * glu
    gc = gc + jnp.sum(cbuf_ref[...] * wcc_ref[0:CF_CONV - 1, :][None], axis=1)
    y_cf = _cf_post(gc, gln_ref, bln_ref)

    x1_ref[...] = _mixer_out(x, y_lru, y_cf, wout_ref)
    h_ref[...] = h
    ltail_ref[:, 0:LRU_CONV - 2, :] = lbuf_ref[:, 1:LRU_CONV - 1, :]
    ctail_ref[:, 0:CF_CONV - 2, :] = cbuf_ref[:, 1:CF_CONV - 1, :]
    for i in range(x.shape[0]):
        ltail_ref[i, LRU_CONV - 2:LRU_CONV - 1, :] = zx[i:i + 1, :]
        ctail_ref[i, CF_CONV - 2:CF_CONV - 1, :] = glu[i:i + 1, :]


def _mixer_sample(x, mix_w, h0, lbuf, cbuf):
    n = x.shape[0]
    g = SAMPLE_MIX_GROUP
    row = lambda w: pl.BlockSpec((g, w), lambda i: (i, 0))
    return pl.pallas_call(
        _mixer_sample_kernel,
        grid=(n // g,),
        in_specs=[row(D_MODEL)] + _mixer_weight_specs() + [
            row(LRU_WIDTH),
            pl.BlockSpec((g, LRU_CONV - 1, LRU_WIDTH), lambda i: (i, 0, 0)),
            pl.BlockSpec((g, CF_CONV - 1, CF_WIDTH), lambda i: (i, 0, 0)),
        ],
        out_specs=[row(D_MODEL), row(LRU_WIDTH),
                   pl.BlockSpec((g, LRU_CONV - 1, LRU_WIDTH), lambda i: (i, 0, 0)),
                   pl.BlockSpec((g, CF_CONV - 1, CF_WIDTH), lambda i: (i, 0, 0))],
        out_shape=[
            jax.ShapeDtypeStruct((n, D_MODEL), F32),
            jax.ShapeDtypeStruct((n, LRU_WIDTH), F32),
            jax.ShapeDtypeStruct((n, LRU_CONV - 1, LRU_WIDTH), F32),
            jax.ShapeDtypeStruct((n, CF_CONV - 1, CF_WIDTH), F32),
        ],
        compiler_params=_cparams(("arbitrary",)),
        name="mixer_sample",
    )(x, *mix_w, h0, lbuf, cbuf)


def _memkv_kernel(mem_ref, gmem_ref, wk_ref, wv_ref, k_ref, v_ref, kb_ref, vb_ref):
    mn = _rms(mem_ref[...], gmem_ref[...]).astype(BF16)
    k = jnp.dot(mn, wk_ref[...], preferred_element_type=F32)
    v = jnp.dot(mn, wv_ref[...], preferred_element_type=F32)
    for h in range(XA_HEADS):
        sl = slice(h * XA_HEAD_DIM, (h + 1) * XA_HEAD_DIM)
        k_ref[0, :, h, :] = k[:, sl]
        v_ref[0, :, h, :] = v[:, sl]
    kb_ref[...] = k.astype(BF16)
    vb_ref[...] = v.astype(BF16)


def _memkv(mem, g_mem, wk, wv):
    n = mem.shape[0]
    t = N_MEM
    row = pl.BlockSpec((t, D_MODEL), lambda i: (i, 0))
    state = pl.BlockSpec((1, N_MEM, XA_HEADS, XA_HEAD_DIM), lambda i: (i, 0, 0, 0))
    state_shape = jax.ShapeDtypeStruct((n // t, N_MEM, XA_HEADS, XA_HEAD_DIM), F32)
    return pl.pallas_call(
        _memkv_kernel,
        grid=(n // t,),
        in_specs=[row, _full((1, D_MODEL)), _full((D_MODEL, D_MODEL)), _full((D_MODEL, D_MODEL))],
        out_specs=[state, state, row, row],
        out_shape=[state_shape] * 2 + [jax.ShapeDtypeStruct((n, D_MODEL), BF16)] * 2,
        compiler_params=_cparams(("arbitrary",)),
        name="memory_kv",
    )(mem, g_mem, wk, wv)


def _router(x2, gmoe_ref, wrt_ref, brt_ref, xn_ref, tope_ref, gates_ref):
    xn = _rms(x2, gmoe_ref[...])
    _store_row_tiles(xn_ref, xn)
    logits = lax.dot_general(wrt_ref[...], xn.astype(BF16), (((1,), (1,)), ((), ())),
                             preferred_element_type=F32) + brt_ref[...]
    e_iota = lax.broadcasted_iota(I32, logits.shape, 0)
    work = logits
    vals, idxs = [], []
    for _ in range(TOP_K):
        m = jnp.max(work, axis=0, keepdims=True)
        idx = jnp.min(jnp.where(work == m, e_iota, N_EXPERTS), axis=0, keepdims=True)
        vals.append(m)
        idxs.append(idx)
        work = jnp.where(e_iota == idx, -jnp.inf, work)
    ex = [jnp.exp(v - vals[0]) for v in vals]
    den = ex[0] + ex[1] + ex[2] + ex[3]
    tope_ref[...] = jnp.concatenate(idxs, axis=0)
    gates_ref[...] = jnp.concatenate([e / den for e in ex], axis=0)


def _router_specs():
    return [_full((1, D_MODEL)), _full((N_EXPERTS, D_MODEL)), _full((N_EXPERTS, 1))]


def _router_out(n, tile, index):
    specs = [
        pl.BlockSpec((tile, D_MODEL), lambda *a: (index(*a), 0)),
        pl.BlockSpec((tile * ROW_TILE, LANES), lambda *a: (index(*a), 0)),
        pl.BlockSpec((TOP_K, tile), lambda *a: (0, index(*a))),
        pl.BlockSpec((TOP_K, tile), lambda *a: (0, index(*a))),
    ]
    shapes = [
        jax.ShapeDtypeStruct((n, D_MODEL), F32),
        jax.ShapeDtypeStruct((n * ROW_TILE, LANES), F32),
        jax.ShapeDtypeStruct((TOP_K, n), I32),
        jax.ShapeDtypeStruct((TOP_K, n), F32),
    ]
    return specs, shapes


def _attn_prompt_kernel(x1_ref, k_ref, v_ref, gxa_ref, wq_ref, wo_ref, gmoe_ref, wrt_ref, brt_ref,
                        x2_ref, xn_ref, tope_ref, gates_ref):
    x1 = x1_ref[...]
    q = (_bdot(_rms(x1, gxa_ref[...]), wq_ref[...]) * (XA_HEAD_DIM ** -0.5)).astype(BF16)
    outs = []
    for h in range(XA_HEADS):
        sl = slice(h * XA_HEAD_DIM, (h + 1) * XA_HEAD_DIM)
        s = lax.dot_general(q[:, sl], k_ref[:, sl], (((1,), (1,)), ((), ())), preferred_element_type=F32)
        p = jnp.exp(s - jnp.max(s, axis=-1, keepdims=True))
        p = p / jnp.sum(p, axis=-1, keepdims=True)
        outs.append(_bdot(p, v_ref[:, sl]))
    o = jnp.concatenate(outs, axis=-1)
    x2 = x1 + _bdot(o, wo_ref[...])
    x2_ref[...] = x2
    _router(x2, gmoe_ref, wrt_ref, brt_ref, xn_ref, tope_ref, gates_ref)


def _attn_prompt(x1, k, v, gxa, wq, wo, router_w, batch, seq):
    tq = ATT_TILE
    nt = seq // tq
    out_specs, out_shapes = _router_out(batch * seq, tq, lambda b, j: b * nt + j)
    kv = pl.BlockSpec((N_MEM, D_MODEL), lambda b, j: (b, 0))
    return pl.pallas_call(
        _attn_prompt_kernel,
        grid=(batch, nt),
        in_specs=[pl.BlockSpec((tq, D_MODEL), lambda b, j: (b * nt + j, 0)), kv, kv,
                  _full((1, D_MODEL)), _full((D_MODEL, D_MODEL)), _full((D_MODEL, D_MODEL))] + _router_specs(),
        out_specs=out_specs,
        out_shape=out_shapes,
        compiler_params=_cparams(("arbitrary", "arbitrary")),
        name="attn_prompt",
    )(x1, k, v, gxa, wq, wo, *router_w)


def _qproj_kernel(x1_ref, gxa_ref, wq_ref, q_ref):
    q_ref[...] = _bdot(_rms(x1_ref[...], gxa_ref[...]), wq_ref[...]) * (XA_HEAD_DIM ** -0.5)


def _attn_sample_core_kernel(q_ref, k_ref, v_ref, o_ref):
    n = N_MEM * SUBLANES
    col_head = lax.broadcasted_iota(I32, (SUBLANES, n), 1) & (SUBLANES - 1)
    same_head = col_head == lax.broadcasted_iota(I32, (SUBLANES, n), 0)
    kv_pad = jnp.zeros((N_MEM, SUBLANES - XA_HEADS, XA_HEAD_DIM), F32)
    q_pad = jnp.zeros((SUBLANES - XA_HEADS, XA_HEAD_DIM), F32)
    for g in range(q_ref.shape[0]):
        k8 = jnp.concatenate([k_ref[g], kv_pad], axis=1).reshape(n, XA_HEAD_DIM).astype(BF16)
        v8 = jnp.concatenate([v_ref[g], kv_pad], axis=1).reshape(n, XA_HEAD_DIM).astype(BF16)
        q8 = jnp.concatenate([q_ref[g], q_pad], axis=0).astype(BF16)
        s = lax.dot_general(q8, k8, (((1,), (1,)), ((), ())), preferred_element_type=F32)
        m = jnp.max(jnp.where(same_head, s, -jnp.inf), axis=-1, keepdims=True)
        p = jnp.where(same_head, jnp.exp(s - m), 0.0)
        den = jnp.sum(p, axis=-1, keepdims=True)
        o8 = jnp.dot(p.astype(BF16), v8, preferred_element_type=F32)
        o_ref[g] = (o8 / den)[0:XA_HEADS, :]


def _oproj_router_kernel(x1_ref, o_ref, wo_ref, gmoe_ref, wrt_ref, brt_ref,
                         x2_ref, xn_ref, tope_ref, gates_ref):
    x2 = x1_ref[...] + _bdot(o_ref[...], wo_ref[...])
    x2_ref[...] = x2
    _router(x2, gmoe_ref, wrt_ref, brt_ref, xn_ref, tope_ref, gates_ref)


def _attn_sample(x1, k, v, gxa, wq, wo, router_w):
    n = x1.shape[0]
    g = SAMPLE_GROUP
    q = pl.pallas_call(
        _qproj_kernel,
        grid=(1,),
        in_specs=[_full((n, D_MODEL)), _full((1, D_MODEL)), _full((D_MODEL, D_MODEL))],
        out_specs=_full((n, D_MODEL)),
        out_shape=jax.ShapeDtypeStruct((n, D_MODEL), F32),
        compiler_params=_cparams(("arbitrary",)),
        name="qproj_sample",
    )(x1, gxa, wq)
    kv = pl.BlockSpec((g, N_MEM, XA_HEADS, XA_HEAD_DIM), lambda i: (i, 0, 0, 0))
    row = pl.BlockSpec((g, XA_HEADS, XA_HEAD_DIM), lambda i: (i, 0, 0))
    o = pl.pallas_call(
        _attn_sample_core_kernel,
        grid=(n // g,),
        in_specs=[row, kv, kv],
        out_specs=row,
        out_shape=jax.ShapeDtypeStruct((n, XA_HEADS, XA_HEAD_DIM), F32),
        compiler_params=_cparams(("arbitrary",)),
        name="attn_sample_core",
    )(q.reshape(n, XA_HEADS, XA_HEAD_DIM), k, v).reshape(n, D_MODEL)
    out_specs, out_shapes = _router_out(n, n, lambda i: 0)
    return pl.pallas_call(
        _oproj_router_kernel,
        grid=(1,),
        in_specs=[_full((n, D_MODEL)), _full((n, D_MODEL)), _full((D_MODEL, D_MODEL))] + _router_specs(),
        out_specs=out_specs,
        out_shape=out_shapes,
        compiler_params=_cparams(("arbitrary",)),
        name="oproj_router_sample",
    )(x1, o, wo, *router_w)


def _moe_num_blocks(n_assign):
    return pl.cdiv(n_assign, MOE_BLOCK) + N_EXPERTS


def _slots_kernel(tp_ref, ts_ref, destp_ref, dests_ref, blke_ref, nact_ref, cnt_ref, pstart_ref,
                  rankp, ranks):
    c = LANES
    r_i = lax.broadcasted_iota(I32, (c, c), 0)
    c_i = lax.broadcasted_iota(I32, (c, c), 1)
    upper = (r_i < c_i).astype(BF16)
    e_iota = lax.broadcasted_iota(I32, (N_EXPERTS, c), 0)

    def make_body(top_ref, rank_ref):
        def body(ci, carry):
            lo = pl.multiple_of(ci * c, c)
            top = top_ref[:, pl.ds(lo, c)]
            hit = [e_iota == top[k:k + 1, :] for k in range(TOP_K)]
            cnt = sum(m.astype(F32) for m in hit)
            base = carry + jnp.dot(cnt.astype(BF16), upper, preferred_element_type=F32)
            rank_ref[:, pl.ds(lo, c)] = jnp.concatenate(
                [jnp.sum(jnp.where(m, base, 0.0), axis=0, keepdims=True) for m in hit], axis=0)
            return carry + jnp.sum(cnt, axis=1, keepdims=True)
        return body

    counts = jnp.zeros((N_EXPERTS, 1), F32)
    counts = lax.fori_loop(0, tp_ref.shape[1] // c, make_body(tp_ref, rankp), counts)
    counts = lax.fori_loop(0, ts_ref.shape[1] // c, make_body(ts_ref, ranks), counts)

    padded = jnp.floor((counts + (MOE_BLOCK - 1)) / MOE_BLOCK) * MOE_BLOCK
    er = lax.broadcasted_iota(I32, (N_EXPERTS, N_EXPERTS), 0)
    ec = lax.broadcasted_iota(I32, (N_EXPERTS, N_EXPERTS), 1)
    padded_row = jnp.sum(jnp.where(er == ec, padded, 0.0), axis=0, keepdims=True)
    cum = jnp.sum(jnp.where(ec <= er, padded_row, 0.0), axis=1, keepdims=True)
    pstart = cum - padded
    cnt_ref[...] = jnp.sum(jnp.where(er == ec, counts, 0.0), axis=0, keepdims=True).astype(I32)
    pstart_ref[...] = jnp.sum(jnp.where(er == ec, pstart, 0.0), axis=0, keepdims=True).astype(I32)

    def dest(top_ref, rank_ref, out_ref):
        n = top_ref.shape[1]
        ei = lax.broadcasted_iota(I32, (N_EXPERTS, n), 0)
        rows = []
        for k in range(TOP_K):
            start = jnp.sum(jnp.where(ei == top_ref[k:k + 1, :], pstart, 0.0), axis=0, keepdims=True)
            rows.append(start + rank_ref[k:k + 1, :])
        out_ref[...] = jnp.concatenate(rows, axis=0).astype(I32)

    dest(tp_ref, rankp, destp_ref)
    dest(ts_ref, ranks, dests_ref)

    nb = blke_ref.shape[1]
    blk_lo = lax.broadcasted_iota(I32, (N_EXPERTS, nb), 1).astype(F32) * MOE_BLOCK
    blk = jnp.sum((cum <= blk_lo).astype(F32), axis=0, keepdims=True)
    blke_ref[...] = jnp.minimum(blk, N_EXPERTS - 1).astype(I32)
    nact_ref[...] = (cum[N_EXPERTS - 1:N_EXPERTS, :] / MOE_BLOCK).astype(I32)


def _slots(top_p, top_s):
    n_p, n_s = top_p.shape[1], top_s.shape[1]
    nb = _moe_num_blocks((n_p + n_s) * TOP_K)
    return pl.pallas_call(
        _slots_kernel,
        grid=(1,),
        in_specs=[_full((TOP_K, n_p)), _full((TOP_K, n_s))],
        out_specs=[_full((TOP_K, n_p)), _full((TOP_K, n_s)), _full((1, nb)), _full((1, 1)),
                   _full((1, N_EXPERTS)), _full((1, N_EXPERTS))],
        out_shape=[
            jax.ShapeDtypeStruct((TOP_K, n_p), I32),
            jax.ShapeDtypeStruct((TOP_K, n_s), I32),
            jax.ShapeDtypeStruct((1, nb), I32),
            jax.ShapeDtypeStruct((1, 1), I32),
            jax.ShapeDtypeStruct((1, N_EXPERTS), I32),
            jax.ShapeDtypeStruct((1, N_EXPERTS), I32),
        ],
        scratch_shapes=[pltpu.VMEM((TOP_K, n_p), F32), pltpu.VMEM((TOP_K, n_s), F32)],
        compiler_params=_cparams(("arbitrary",)),
        name="moe_slots",
    )(top_p, top_s)


def _for_rows(n_rows, fn):
    def body(i, c):
        for u in range(DMA_UNROLL):
            fn(i * DMA_UNROLL + u)
        return c
    lax.fori_loop(0, n_rows // DMA_UNROLL, body, 0)


def _dispatch_kernel(cnt_ref, pstart_ref, nact_ref, destp_ref, dests_ref, xnp_ref, xns_ref, xs_ref,
                     zeros, sem, zsem):
    i = pl.program_id(0)
    last = pl.num_programs(0) - 1

    def scatter(x_ref, dest_ref):
        n = x_ref.shape[0] // ROW_TILE

        def one(t):
            for k in range(TOP_K):
                _row_tile_copy(x_ref, t, xs_ref, dest_ref[t * TOP_K + k], sem).start(priority=k % 2)

        _for_rows(n, one)
        _row_tile_copy(xs_ref, 0, xs_ref, 0, sem, n * TOP_K).wait()

    @pl.when(i < last)
    def _():
        scatter(xnp_ref, destp_ref)

    @pl.when(i == last)
    def _():
        scatter(xns_ref, dests_ref)
        zeros[...] = jnp.zeros(zeros.shape, F32)

        def pad_copies(e, act):
            c = cnt_ref[e]
            npad = (-c) & (MOE_BLOCK - 1)
            base = pstart_ref[e] + c
            for bit in range(MOE_BLOCK.bit_length() - 1):
                size = 1 << bit

                @pl.when((npad & size) != 0)
                def _():
                    act(_row_tile_copy(zeros, 0, xs_ref, base + (npad & (size - 1)), zsem, size))

        def tail_copies(act):
            def body(j, c):
                act(_row_tile_copy(zeros, 0, xs_ref, j * MOE_BLOCK, zsem, MOE_BLOCK))
                return c
            lax.fori_loop(nact_ref[0], xs_ref.shape[0] // (MOE_BLOCK * ROW_TILE), body, 0)

        for act in (lambda d: d.start(), lambda d: d.wait()):
            lax.fori_loop(0, N_EXPERTS, lambda e, c, act=act: (pad_copies(e, act), c)[1], 0)
            tail_copies(act)


def _dispatch(cnt, pstart, nact, dest_p, dest_s, xn_p, xn_s, n_rows):
    n_p, n_s = xn_p.shape[0] // ROW_TILE, xn_s.shape[0] // ROW_TILE
    tile = DISPATCH_TILE
    nt = n_p // tile
    smem = pl.BlockSpec(memory_space=pltpu.SMEM)
    return pl.pallas_call(
        _dispatch_kernel,
        grid=(nt + 1,),
        in_specs=[
            smem, smem, smem,
            pl.BlockSpec((TOP_K * tile,), lambda i: (jnp.minimum(i, nt - 1),), memory_space=pltpu.SMEM),
            smem,
            pl.BlockSpec((tile * ROW_TILE, LANES), lambda i: (jnp.minimum(i, nt - 1), 0)),
            _full((n_s * ROW_TILE, LANES)),
        ],
        out_specs=pl.BlockSpec(memory_space=pl.ANY),
        out_shape=jax.ShapeDtypeStruct((n_rows * ROW_TILE, LANES), F32),
        scratch_shapes=[pltpu.VMEM((MOE_BLOCK * ROW_TILE, LANES), F32), pltpu.SemaphoreType.DMA,
                        pltpu.SemaphoreType.DMA],
        compiler_params=_cparams(("arbitrary",)),
        name="moe_dispatch",
    )(cnt, pstart, nact, dest_p, dest_s, xn_p, xn_s)


def _combine_kernel(dest_ref, dest_next_ref, gates_ref, x2_ref, gfin_ref, ys_ref, y_ref, bufs, sems):
    tile = x2_ref.shape[0]
    i = pl.program_id(0)
    slot = i % 2

    def gather(d_ref, b):
        def one(t):
            for k in range(TOP_K):
                _row_tile_copy(ys_ref, d_ref[t * TOP_K + k], bufs.at[b, k], t, sems.at[b]).start(priority=k % 2)
        _for_rows(tile, one)

    @pl.when(i == 0)
    def _():
        gather(dest_ref, 0)

    for b in range(2):
        @pl.when(slot == b)
        def _():
            pltpu.make_async_copy(bufs.at[b], bufs.at[b], sems.at[b]).wait()

            @pl.when(i + 1 < pl.num_programs(0))
            def _():
                gather(dest_next_ref, 1 - b)

    buf = bufs.at[slot]
    sumsq = jnp.zeros((tile, 1), F32)
    gate = [jnp.broadcast_to(gates_ref[:, k:k + 1], (tile, LANES)) for k in range(TOP_K)]
    for s in range(ROW_TILE):
        cols = slice(s * LANES, (s + 1) * LANES)
        acc = x2_ref[:, cols]
        for k in range(TOP_K):
            acc = acc + gate[k] * buf[k, pl.ds(s, tile, stride=ROW_TILE), :]
        y_ref[:, cols] = acc
        sumsq = sumsq + jnp.sum(acc * acc, axis=-1, keepdims=True)
    scale = lax.rsqrt(sumsq / D_MODEL + EPS)
    for s in range(ROW_TILE):
        cols = slice(s * LANES, (s + 1) * LANES)
        y_ref[:, cols] = y_ref[:, cols] * scale * gfin_ref[:, cols]


def _combine(dest, gates_t, x2, g_final, ys):
    n = x2.shape[0]
    tile = min(COMBINE_TILE, n)
    row = pl.BlockSpec((tile, D_MODEL), lambda i: (i, 0))
    nt = n // tile
    return pl.pallas_call(
        _combine_kernel,
        grid=(nt,),
        in_specs=[
            pl.BlockSpec((TOP_K * tile,), lambda i: (i,), memory_space=pltpu.SMEM),
            pl.BlockSpec((TOP_K * tile,), lambda i: (jnp.minimum(i + 1, nt - 1),), memory_space=pltpu.SMEM),
            pl.BlockSpec((tile, TOP_K), lambda i: (i, 0)),
            row,
            _full((1, D_MODEL)),
            pl.BlockSpec(memory_space=pl.ANY),
        ],
        out_specs=row,
        out_shape=jax.ShapeDtypeStruct((n, D_MODEL), F32),
        scratch_shapes=[pltpu.VMEM((2, TOP_K, tile * ROW_TILE, LANES), F32), pltpu.SemaphoreType.DMA((2,))],
        compiler_params=_cparams(("arbitrary",)),
        name="moe_combine",
    )(dest, dest, gates_t, x2, g_final, ys)


def _experts_kernel(blke_ref, nact_ref, xs_ref, w1_hbm, b1_ref, w2_hbm, b2_ref, ys_ref,
                    w1f, w2f, w1b, w2b, wsem, started):
    j = pl.program_id(0)
    nact = nact_ref[0]
    active = j < nact
    e = blke_ref[j]
    new_expert = jnp.logical_or(j == 0, e != blke_ref[jnp.maximum(j - 1, 0)])

    def fetch(expert, slot):
        return (pltpu.make_async_copy(w1_hbm.at[expert], w1f.at[slot], wsem.at[slot]),
                pltpu.make_async_copy(w2_hbm.at[expert], w2f.at[slot], wsem.at[slot]))

    @pl.when(j == 0)
    def _():
        started[0] = 0
        for d in fetch(e, 0):
            d.start()

    @pl.when(jnp.logical_and(active, new_expert))
    def _():
        n = started[0]
        last = blke_ref.shape[0] - 1
        jn = lax.while_loop(lambda jj: jnp.logical_and(jj < nact, blke_ref[jnp.minimum(jj, last)] == e),
                            lambda jj: jj + 1, j + 1)
        for b in range(2):
            @pl.when(n % 2 == b)
            def _():
                for d in fetch(e, b):
                    d.wait()

                @pl.when(jn < nact)
                def _():
                    for d in fetch(blke_ref[jnp.minimum(jn, last)], 1 - b):
                        d.start(priority=1)

                w1b[...] = w1f[b].astype(BF16)
                w2b[...] = w2f[b].astype(BF16)
        started[0] = n + 1

    @pl.when(active)
    def _():
        x = _load_row_tiles(xs_ref, MOE_BLOCK).astype(BF16)
        gu = jnp.dot(x, w1b[...], preferred_element_type=F32) + b1_ref[0]
        g = jnp.minimum(gu[:, :D_EXPERT], SWIGLU_LIMIT)
        u = jnp.clip(gu[:, D_EXPERT:], -SWIGLU_LIMIT, SWIGLU_LIMIT)
        h = (u + 1.0) * (g * jax.nn.sigmoid(SWIGLU_ALPHA * g))
        _store_row_tiles(ys_ref, _bdot(h, w2b[...]) + b2_ref[0])

    @pl.when(jnp.logical_not(active))
    def _():
        ys_ref[...] = jnp.zeros(ys_ref.shape, F32)


def _experts(blk_e, nact, xs, w1, b1, w2, b2):
    nb = xs.shape[0] // (MOE_BLOCK * ROW_TILE)
    blk = lambda j, be, na: (jnp.minimum(j, na[0] - 1), 0)
    exp = lambda j, be, na: (be[jnp.minimum(j, na[0] - 1)], 0, 0)
    rows = (MOE_BLOCK * ROW_TILE, LANES)
    return pl.pallas_call(
        _experts_kernel,
        grid_spec=pltpu.PrefetchScalarGridSpec(
            num_scalar_prefetch=2,
            grid=(nb,),
            in_specs=[
                pl.BlockSpec(rows, blk),
                pl.BlockSpec(memory_space=pl.ANY),
                pl.BlockSpec((1, 1, 2 * D_EXPERT), exp),
                pl.BlockSpec(memory_space=pl.ANY),
                pl.BlockSpec((1, 1, D_MODEL), exp),
            ],
            out_specs=pl.BlockSpec(rows, lambda j, be, na: (j, 0)),
            scratch_shapes=[
                pltpu.VMEM((2, D_MODEL, 2 * D_EXPERT), F32), pltpu.VMEM((2, D_EXPERT, D_MODEL), F32),
                pltpu.VMEM((D_MODEL, 2 * D_EXPERT), BF16), pltpu.VMEM((D_EXPERT, D_MODEL), BF16),
                pltpu.SemaphoreType.DMA((2,)), pltpu.SMEM((1,), I32),
            ],
        ),
        out_shape=jax.ShapeDtypeStruct(xs.shape, F32),
        compiler_params=_cparams(("arbitrary",)),
        name="moe_experts",
    )(blk_e, nact, xs, w1, b1, w2, b2)


def _block_diag(w):
    h, d, _ = w.shape
    eye = jnp.eye(h, dtype=w.dtype)
    return (eye[:, None, :, None] * w[:, :, None, :]).reshape(h * d, h * d)


def kernel(x_prompt, x_sample, state_lru_h, cache_lru_conv, cache_cf_conv, cache_mem_k, cache_mem_v,
           mem_prompt, g_mix, w_in, w_lru_conv, b_lru_conv, w_lru_a, b_lru_a, w_lru_x, b_lru_x,
           lru_lambda, w_cf_conv, b_cf_conv, g_cf_ln, b_cf_ln, w_out, g_xa, g_mem, w_q, w_k, w_v, w_o,
           g_moe, w_router, b_router, w_e1, b_e1, w_e2, b_e2, g_final):
    depth = g_mix.shape[0]
    assert depth == 1, "single-layer stack"
    batch, seq, _ = x_prompt.shape
    n_s = x_sample.shape[0]
    assert x_sample.shape[1] == 1
    n_p = batch * seq
    l = 0

    mix_w = (
        g_mix[l][None], w_in[l].astype(BF16), w_lru_conv[l], b_lru_conv[l][None],
        jnp.concatenate([_block_diag(w_lru_a[l]), _block_diag(w_lru_x[l])], axis=1).astype(BF16),
        jnp.concatenate([b_lru_a[l], b_lru_x[l]])[None], lru_lambda[l][None],
        w_cf_conv[l], b_cf_conv[l][None], g_cf_ln[l][None], b_cf_ln[l][None], w_out[l].astype(BF16),
    )
    router_w = (g_moe[l][None], w_router[l].T.astype(BF16), b_router[l][:, None])
    wq, wo = w_q[l].astype(BF16), w_o[l].astype(BF16)
    gxa = g_xa[l][None]

    xp = x_prompt.reshape(n_p, D_MODEL)
    xs_ = x_sample.reshape(n_s, D_MODEL)
    x1p, hp, lcp, ccp = _mixer_prompt(xp, mix_w, batch, seq)
    x1s, hs, lru_conv_s, cf_conv_s = _mixer_sample(xs_, mix_w, state_lru_h[l], cache_lru_conv[l], cache_cf_conv[l])

    mk, mv, mkb, mvb = _memkv(mem_prompt.reshape(batch * N_MEM, D_MODEL), g_mem[l][None],
                              w_k[l].astype(BF16), w_v[l].astype(BF16))
    x2p, xnp_, top_p, gates_p = _attn_prompt(x1p, mkb, mvb, gxa, wq, wo, router_w, batch, seq)
    x2s, xns, top_s, gates_s = _attn_sample(x1s, cache_mem_k[l], cache_mem_v[l], gxa, wq, wo, router_w)

    dest_p, dest_s, blk_e, nact, cnt, pstart = _slots(top_p, top_s)
    dest_p, dest_s = dest_p.T.reshape(-1), dest_s.T.reshape(-1)
    nb = _moe_num_blocks((n_p + n_s) * TOP_K)
    nact = nact.reshape(1)
    xs = _dispatch(cnt.reshape(N_EXPERTS), pstart.reshape(N_EXPERTS), nact, dest_p, dest_s, xnp_, xns,
                   nb * MOE_BLOCK)
    ys = _experts(blk_e.reshape(nb), nact, xs, w_e1[l], b_e1[l][:, None, :], w_e2[l], b_e2[l][:, None, :])
    gfin = g_final[None]
    y_p = _combine(dest_p, gates_p.T, x2p, gfin, ys)
    y_s = _combine(dest_s, gates_s.T, x2s, gfin, ys)

    return (
        y_p.reshape(batch, seq, D_MODEL),
        y_s.reshape(n_s, 1, D_MODEL),
        hp.reshape(depth, batch, LRU_WIDTH),
        lcp[None],
        ccp[None],
        mk[None],
        mv[None],
        hs[None],
        lru_conv_s[None],
        cf_conv_s[None],
    )
```

```python
import functools

import jax
import jax.numpy as jnp
from jax import lax
from jax.experimental import pallas as pl
from jax.experimental.pallas import tpu as pltpu

F32 = jnp.float32
BF16 = jnp.bfloat16
I32 = jnp.int32

D_MODEL = 1024
LRU_WIDTH = 512
CF_WIDTH = 512
LRU_HEADS = 8
LRU_CONV = 4
LRU_C = 8.0
CF_CONV = 31
IN_WIDTH = 2 * LRU_WIDTH + 2 * CF_WIDTH
N_MEM = 256
XA_HEADS = 4
XA_HEAD_DIM = D_MODEL // XA_HEADS
N_EXPERTS = 32
TOP_K = 4
D_EXPERT = D_MODEL
SWIGLU_LIMIT = 7.0
SWIGLU_ALPHA = 1.702
EPS = 1e-6

LANES = 128
SUBLANES = 8
ROW_TILE = D_MODEL // LANES
assert ROW_TILE == SUBLANES
VMEM_LIMIT = 56 * 1024 * 1024

MIX_TILE = 1024
LRU_TAIL_ROWS = 8
CF_TAIL_ROWS = 32
ATT_TILE = 1024
SAMPLE_GROUP = 4
SAMPLE_MIX_GROUP = 32
MOE_BLOCK = 512
DMA_UNROLL = 8
DISPATCH_TILE = 2048
COMBINE_TILE = 256


def _cparams(sem):
    return pltpu.CompilerParams(dimension_semantics=sem, vmem_limit_bytes=VMEM_LIMIT)


def _full(shape):
    n = len(shape)
    return pl.BlockSpec(shape, lambda *_: (0,) * n)


def _rms(x, g):
    return x * lax.rsqrt(jnp.mean(x * x, axis=-1, keepdims=True) + EPS) * g


def _bdot(a, b):
    return jnp.dot(a.astype(BF16), b, preferred_element_type=F32)


def _store_row_tiles(ref, x):
    m = x.shape[0]
    for s in range(ROW_TILE):
        ref[pl.ds(s, m, stride=ROW_TILE), :] = x[:, s * LANES:(s + 1) * LANES]


def _load_row_tiles(ref, m):
    return jnp.concatenate([ref[pl.ds(s, m, stride=ROW_TILE), :] for s in range(ROW_TILE)], axis=-1)


def _row_tile_copy(src_ref, src_row, dst_ref, dst_row, sem, n_rows=1):
    src = src_ref.at[pl.ds(pl.multiple_of(src_row * ROW_TILE, ROW_TILE), n_rows * ROW_TILE)]
    dst = dst_ref.at[pl.ds(pl.multiple_of(dst_row * ROW_TILE, ROW_TILE), n_rows * ROW_TILE)]
    return pltpu.make_async_copy(src, dst, sem)


def _lru_coeffs(xc, wg_ref, bg_ref, lam_ref):
    gl = _bdot(xc, wg_ref[...]) + bg_ref[...]
    r = jax.nn.sigmoid(gl[:, :LRU_WIDTH])
    i = jax.nn.sigmoid(gl[:, LRU_WIDTH:])
    log_a = -LRU_C * r * jax.nn.softplus(-lam_ref[...])
    a = jnp.exp(log_a)
    u = jnp.sqrt(-jnp.tanh(log_a) * (1.0 + a * a)) * (i * xc)
    return a, u


def _cf_post(gc, gln_ref, bln_ref):
    mu = jnp.mean(gc, axis=-1, keepdims=True)
    c = gc - mu
    y = c * lax.rsqrt(jnp.mean(c * c, axis=-1, keepdims=True) + EPS)
    return jax.nn.silu(y * gln_ref[...] + bln_ref[...])


def _mixer_out(x, y_lru, y_cf, wout_ref):
    y = _bdot(y_lru, wout_ref[:LRU_WIDTH, :]) + _bdot(y_cf, wout_ref[LRU_WIDTH:, :])
    return x + y


def _mixer_prompt_kernel(x_ref, gmix_ref, win_ref, wlc_ref, blc_ref, wg_ref, bg_ref, lam_ref,
                         wcc_ref, bcc_ref, gln_ref, bln_ref, wout_ref,
                         x1_ref, h_ref, ltail_ref, ctail_ref,
                         zx_ext, glu_ext, h_carry):
    tt = MIX_TILE
    j = pl.program_id(1)

    @pl.when(j == 0)
    def _():
        zx_ext[0:LRU_TAIL_ROWS, :] = jnp.zeros((LRU_TAIL_ROWS, LRU_WIDTH), F32)
        glu_ext[0:CF_TAIL_ROWS, :] = jnp.zeros((CF_TAIL_ROWS, CF_WIDTH), F32)
        h_carry[...] = jnp.zeros((1, LRU_WIDTH), F32)

    x = x_ref[...]
    z = _bdot(_rms(x, gmix_ref[...]), win_ref[...])
    zx = z[:, :LRU_WIDTH]
    zg = z[:, LRU_WIDTH:2 * LRU_WIDTH]
    za = z[:, 2 * LRU_WIDTH:2 * LRU_WIDTH + CF_WIDTH]
    zb = z[:, 2 * LRU_WIDTH + CF_WIDTH:]

    zx_ext[LRU_TAIL_ROWS:LRU_TAIL_ROWS + tt, :] = zx
    off = LRU_TAIL_ROWS - (LRU_CONV - 1)
    xc = blc_ref[...]
    for k in range(LRU_CONV):
        xc = xc + wlc_ref[k:k + 1, :] * zx_ext[off + k:off + k + tt, :]
    a, u = _lru_coeffs(xc, wg_ref, bg_ref, lam_ref)
    sub = lax.broadcasted_iota(I32, (tt, LRU_WIDTH), 0) & (SUBLANES - 1)
    d = 1
    while d < SUBLANES:
        keep = sub >= d
        a_sh = jnp.where(keep, pltpu.roll(a, d, 0), 1.0)
        u_sh = jnp.where(keep, pltpu.roll(u, d, 0), 0.0)
        u = u + a * u_sh
        a = a * a_sh
        d *= 2
    h_last = h_carry[...]
    groups = []
    for r in range(0, tt, SUBLANES):
        hg = a[r:r + SUBLANES, :] * h_last + u[r:r + SUBLANES, :]
        groups.append(hg)
        h_last = hg[SUBLANES - 1:SUBLANES, :]
    hs = jnp.concatenate(groups, axis=0)
    h_carry[...] = h_last
    y_lru = hs * jax.nn.gelu(zg)

    glu = za * jax.nn.sigmoid(zb)
    glu_ext[CF_TAIL_ROWS:CF_TAIL_ROWS + tt, :] = glu
    off = CF_TAIL_ROWS - (CF_CONV - 1)
    gc = bcc_ref[...]
    for s in range(SUBLANES):
        rows = tt if s == 0 else tt + SUBLANES
        q = None
        for k in range(CF_CONV):
            if (off + k) % SUBLANES == s:
                base = off + k - s
                term = wcc_ref[k:k + 1, :] * glu_ext[base:base + rows, :]
                q = term if q is None else q + term
        gc = gc + (q if s == 0 else q[s:s + tt, :])
    y_cf = _cf_post(gc, gln_ref, bln_ref)

    x1_ref[...] = _mixer_out(x, y_lru, y_cf, wout_ref)

    h_ref[0] = h_last
    ltail_ref[0] = zx_ext[LRU_TAIL_ROWS + tt - (LRU_CONV - 1):LRU_TAIL_ROWS + tt, :]
    ctail_ref[0] = glu_ext[CF_TAIL_ROWS + tt - (CF_CONV - 1):CF_TAIL_ROWS + tt, :]
    zx_ext[0:LRU_TAIL_ROWS, :] = zx_ext[tt:tt + LRU_TAIL_ROWS, :]
    glu_ext[0:CF_TAIL_ROWS, :] = glu_ext[tt:tt + CF_TAIL_ROWS, :]


def _mixer_weight_specs():
    return [
        _full((1, D_MODEL)),
        _full((D_MODEL, IN_WIDTH)),
        _full((LRU_CONV, LRU_WIDTH)), _full((1, LRU_WIDTH)),
        _full((LRU_WIDTH, 2 * LRU_WIDTH)), _full((1, 2 * LRU_WIDTH)),
        _full((1, LRU_WIDTH)),
        _full((CF_CONV, CF_WIDTH)), _full((1, CF_WIDTH)),
        _full((1, CF_WIDTH)), _full((1, CF_WIDTH)),
        _full((D_MODEL, D_MODEL)),
    ]


def _mixer_prompt(x, mix_w, batch, seq):
    tt = MIX_TILE
    nt = seq // tt
    return pl.pallas_call(
        _mixer_prompt_kernel,
        grid=(batch, nt),
        in_specs=[pl.BlockSpec((tt, D_MODEL), lambda b, j: (b * nt + j, 0))] + _mixer_weight_specs(),
        out_specs=[
            pl.BlockSpec((tt, D_MODEL), lambda b, j: (b * nt + j, 0)),
            pl.BlockSpec((1, 1, LRU_WIDTH), lambda b, j: (b, 0, 0)),
            pl.BlockSpec((1, LRU_CONV - 1, LRU_WIDTH), lambda b, j: (b, 0, 0)),
            pl.BlockSpec((1, CF_CONV - 1, CF_WIDTH), lambda b, j: (b, 0, 0)),
        ],
        out_shape=[
            jax.ShapeDtypeStruct((batch * seq, D_MODEL), F32),
            jax.ShapeDtypeStruct((batch, 1, LRU_WIDTH), F32),
            jax.ShapeDtypeStruct((batch, LRU_CONV - 1, LRU_WIDTH), F32),
            jax.ShapeDtypeStruct((batch, CF_CONV - 1, CF_WIDTH), F32),
        ],
        scratch_shapes=[
            pltpu.VMEM((LRU_TAIL_ROWS + tt, LRU_WIDTH), F32),
            pltpu.VMEM((CF_TAIL_ROWS + tt, CF_WIDTH), F32),
            pltpu.VMEM((1, LRU_WIDTH), F32),
        ],
        compiler_params=_cparams(("arbitrary", "arbitrary")),
        name="mixer_prompt",
    )(x, *mix_w)


def _mixer_sample_kernel(x_ref, gmix_ref, win_ref, wlc_ref, blc_ref, wg_ref, bg_ref, lam_ref,
                         wcc_ref, bcc_ref, gln_ref, bln_ref, wout_ref,
                         h0_ref, lbuf_ref, cbuf_ref,
                         x1_ref, h_ref, ltail_ref, ctail_ref):
    x = x_ref[...]
    z = _bdot(_rms(x, gmix_ref[...]), win_ref[...])
    zx = z[:, :LRU_WIDTH]
    zg = z[:, LRU_WIDTH:2 * LRU_WIDTH]
    za = z[:, 2 * LRU_WIDTH:2 * LRU_WIDTH + CF_WIDTH]
    zb = z[:, 2 * LRU_WIDTH + CF_WIDTH:]

    xc = blc_ref[...] + wlc_ref[LRU_CONV - 1:LRU_CONV, :] * zx
    xc = xc + jnp.sum(lbuf_ref[...] * wlc_ref[0:LRU_CONV - 1, :][None], axis=1)
    a, u = _lru_coeffs(xc, wg_ref, bg_ref, lam_ref)
    h = a * h0_ref[...] + u
    y_lru = h * jax.nn.gelu(zg)

    glu = za * jax.nn.sigmoid(zb)
    gc = bcc_ref[...] + wcc_ref[CF_CONV - 1:CF_CONV, :] * glu
    gc = gc + jnp.sum(cbuf_ref[...] * wcc_ref[0:CF_CONV - 1, :][None], axis=1)
    y_cf = _cf_post(gc, gln_ref, bln_ref)

    x1_ref[...] = _mixer_out(x, y_lru, y_cf, wout_ref)
    h_ref[...] = h
    ltail_ref[:, 0:LRU_CONV - 2, :] = lbuf_ref[:, 1:LRU_CONV - 1, :]
    ctail_ref[:, 0:CF_CONV - 2, :] = cbuf_ref[:, 1:CF_CONV - 1, :]
    for i in range(x.shape[0]):
        ltail_ref[i, LRU_CONV - 2:LRU_CONV - 1, :] = zx[i:i + 1, :]
        ctail_ref[i, CF_CONV - 2:CF_CONV - 1, :] = glu[i:i + 1, :]


def _mixer_sample(x, mix_w, h0, lbuf, cbuf):
    n = x.shape[0]
    g = SAMPLE_MIX_GROUP
    row = lambda w: pl.BlockSpec((g, w), lambda i: (i, 0))
    return pl.pallas_call(
        _mixer_sample_kernel,
        grid=(n // g,),
        in_specs=[row(D_MODEL)] + _mixer_weight_specs() + [
            row(LRU_WIDTH),
            pl.BlockSpec((g, LRU_CONV - 1, LRU_WIDTH), lambda i: (i, 0, 0)),
            pl.BlockSpec((g, CF_CONV - 1, CF_WIDTH), lambda i: (i, 0, 0)),
        ],
        out_specs=[row(D_MODEL), row(LRU_WIDTH),
                   pl.BlockSpec((g, LRU_CONV - 1, LRU_WIDTH), lambda i: (i, 0, 0)),
                   pl.BlockSpec((g, CF_CONV - 1, CF_WIDTH), lambda i: (i, 0, 0))],
        out_shape=[
            jax.ShapeDtypeStruct((n, D_MODEL), F32),
            jax.ShapeDtypeStruct((n, LRU_WIDTH), F32),
            jax.ShapeDtypeStruct((n, LRU_CONV - 1, LRU_WIDTH), F32),
            jax.ShapeDtypeStruct((n, CF_CONV - 1, CF_WIDTH), F32),
        ],
        compiler_params=_cparams(("arbitrary",)),
        name="mixer_sample",
    )(x, *mix_w, h0, lbuf, cbuf)


def _memkv_kernel(mem_ref, gmem_ref, wk_ref, wv_ref, k_ref, v_ref, kb_ref, vb_ref):
    mn = _rms(mem_ref[...], gmem_ref[...]).astype(BF16)
    k = jnp.dot(mn, wk_ref[...], preferred_element_type=F32)
    v = jnp.dot(mn, wv_ref[...], preferred_element_type=F32)
    for h in range(XA_HEADS):
        sl = slice(h * XA_HEAD_DIM, (h + 1) * XA_HEAD_DIM)
        k_ref[0, :, h, :] = k[:, sl]
        v_ref[0, :, h, :] = v[:, sl]
    kb_ref[...] = k.astype(BF16)
    vb_ref[...] = v.astype(BF16)


def _memkv(mem, g_mem, wk, wv):
    n = mem.shape[0]
    t = N_MEM
    row = pl.BlockSpec((t, D_MODEL), lambda i: (i, 0))
    state = pl.BlockSpec((1, N_MEM, XA_HEADS, XA_HEAD_DIM), lambda i: (i, 0, 0, 0))
    state_shape = jax.ShapeDtypeStruct((n // t, N_MEM, XA_HEADS, XA_HEAD_DIM), F32)
    return pl.pallas_call(
        _memkv_kernel,
        grid=(n // t,),
        in_specs=[row, _full((1, D_MODEL)), _full((D_MODEL, D_MODEL)), _full((D_MODEL, D_MODEL))],
        out_specs=[state, state, row, row],
        out_shape=[state_shape] * 2 + [jax.ShapeDtypeStruct((n, D_MODEL), BF16)] * 2,
        compiler_params=_cparams(("arbitrary",)),
        name="memory_kv",
    )(mem, g_mem, wk, wv)


def _router(x2, gmoe_ref, wrt_ref, brt_ref, xn_ref, tope_ref, gates_ref):
    xn = _rms(x2, gmoe_ref[...])
    _store_row_tiles(xn_ref, xn)
    logits = lax.dot_general(wrt_ref[...], xn.astype(BF16), (((1,), (1,)), ((), ())),
                             preferred_element_type=F32) + brt_ref[...]
    e_iota = lax.broadcasted_iota(I32, logits.shape, 0)
    work = logits
    vals, idxs = [], []
    for _ in range(TOP_K):
        m = jnp.max(work, axis=0, keepdims=True)
        idx = jnp.min(jnp.where(work == m, e_iota, N_EXPERTS), axis=0, keepdims=True)
        vals.append(m)
        idxs.append(idx)
        work = jnp.where(e_iota == idx, -jnp.inf, work)
    ex = [jnp.exp(v - vals[0]) for v in vals]
    den = ex[0] + ex[1] + ex[2] + ex[3]
    tope_ref[...] = jnp.concatenate(idxs, axis=0)
    gates_ref[...] = jnp.concatenate([e / den for e in ex], axis=0)


def _router_specs():
    return [_full((1, D_MODEL)), _full((N_EXPERTS, D_MODEL)), _full((N_EXPERTS, 1))]


def _router_out(n, tile, index):
    specs = [
        pl.BlockSpec((tile, D_MODEL), lambda *a: (index(*a), 0)),
        pl.BlockSpec((tile * ROW_TILE, LANES), lambda *a: (index(*a), 0)),
        pl.BlockSpec((TOP_K, tile), lambda *a: (0, index(*a))),
        pl.BlockSpec((TOP_K, tile), lambda *a: (0, index(*a))),
    ]
    shapes = [
        jax.ShapeDtypeStruct((n, D_MODEL), F32),
        jax.ShapeDtypeStruct((n * ROW_TILE, LANES), F32),
        jax.ShapeDtypeStruct((TOP_K, n), I32),
        jax.ShapeDtypeStruct((TOP_K, n), F32),
    ]
    return specs, shapes


def _attn_prompt_kernel(x1_ref, k_ref, v_ref, gxa_ref, wq_ref, wo_ref, gmoe_ref, wrt_ref, brt_ref,
                        x2_ref, xn_ref, tope_ref, gates_ref):
    x1 = x1_ref[...]
    q = (_bdot(_rms(x1, gxa_ref[...]), wq_ref[...]) * (XA_HEAD_DIM ** -0.5)).astype(BF16)
    outs = []
    for h in range(XA_HEADS):
        sl = slice(h * XA_HEAD_DIM, (h + 1) * XA_HEAD_DIM)
        s = lax.dot_general(q[:, sl], k_ref[:, sl], (((1,), (1,)), ((), ())), preferred_element_type=F32)
        p = jnp.exp(s - jnp.max(s, axis=-1, keepdims=True))
        p = p / jnp.sum(p, axis=-1, keepdims=True)
        outs.append(_bdot(p, v_ref[:, sl]))
    o = jnp.concatenate(outs, axis=-1)
    x2 = x1 + _bdot(o, wo_ref[...])
    x2_ref[...] = x2
    _router(x2, gmoe_ref, wrt_ref, brt_ref, xn_ref, tope_ref, gates_ref)


def _attn_prompt(x1, k, v, gxa, wq, wo, router_w, batch, seq):
    tq = ATT_TILE
    nt = seq // tq
    out_specs, out_shapes = _router_out(batch * seq, tq, lambda b, j: b * nt + j)
    kv = pl.BlockSpec((N_MEM, D_MODEL), lambda b, j: (b, 0))
    return pl.pallas_call(
        _attn_prompt_kernel,
        grid=(batch, nt),
        in_specs=[pl.BlockSpec((tq, D_MODEL), lambda b, j: (b * nt + j, 0)), kv, kv,
                  _full((1, D_MODEL)), _full((D_MODEL, D_MODEL)), _full((D_MODEL, D_MODEL))] + _router_specs(),
        out_specs=out_specs,
        out_shape=out_shapes,
        compiler_params=_cparams(("arbitrary", "arbitrary")),
        name="attn_prompt",
    )(x1, k, v, gxa, wq, wo, *router_w)


def _qproj_kernel(x1_ref, gxa_ref, wq_ref, q_ref):
    q_ref[...] = _bdot(_rms(x1_ref[...], gxa_ref[...]), wq_ref[...]) * (XA_HEAD_DIM ** -0.5)


def _attn_sample_core_kernel(q_ref, k_ref, v_ref, o_ref):
    n = N_MEM * SUBLANES
    col_head = lax.broadcasted_iota(I32, (SUBLANES, n), 1) & (SUBLANES - 1)
    same_head = col_head == lax.broadcasted_iota(I32, (SUBLANES, n), 0)
    kv_pad = jnp.zeros((N_MEM, SUBLANES - XA_HEADS, XA_HEAD_DIM), F32)
    q_pad = jnp.zeros((SUBLANES - XA_HEADS, XA_HEAD_DIM), F32)
    for g in range(q_ref.shape[0]):
        k8 = jnp.concatenate([k_ref[g], kv_pad], axis=1).reshape(n, XA_HEAD_DIM).astype(BF16)
        v8 = jnp.concatenate([v_ref[g], kv_pad], axis=1).reshape(n, XA_HEAD_DIM).astype(BF16)
        q8 = jnp.concatenate([q_ref[g], q_pad], axis=0).astype(BF16)
        s = lax.dot_general(q8, k8, (((1,), (1,)), ((), ())), preferred_element_type=F32)
        m = jnp.max(jnp.where(same_head, s, -jnp.inf), axis=-1, keepdims=True)
        p = jnp.where(same_head, jnp.exp(s - m), 0.0)
        den = jnp.sum(p, axis=-1, keepdims=True)
        o8 = jnp.dot(p.astype(BF16), v8, preferred_element_type=F32)
        o_ref[g] = (o8 / den)[0:XA_HEADS, :]


def _oproj_router_kernel(x1_ref, o_ref, wo_ref, gmoe_ref, wrt_ref, brt_ref,
                         x2_ref, xn_ref, tope_ref, gates_ref):
    x2 = x1_ref[...] + _bdot(o_ref[...], wo_ref[...])
    x2_ref[...] = x2
    _router(x2, gmoe_ref, wrt_ref, brt_ref, xn_ref, tope_ref, gates_ref)


def _attn_sample(x1, k, v, gxa, wq, wo, router_w):
    n = x1.shape[0]
    g = SAMPLE_GROUP
    q = pl.pallas_call(
        _qproj_kernel,
        grid=(1,),
        in_specs=[_full((n, D_MODEL)), _full((1, D_MODEL)), _full((D_MODEL, D_MODEL))],
        out_specs=_full((n, D_MODEL)),
        out_shape=jax.ShapeDtypeStruct((n, D_MODEL), F32),
        compiler_params=_cparams(("arbitrary",)),
        name="qproj_sample",
    )(x1, gxa, wq)
    kv = pl.BlockSpec((g, N_MEM, XA_HEADS, XA_HEAD_DIM), lambda i: (i, 0, 0, 0))
    row = pl.BlockSpec((g, XA_HEADS, XA_HEAD_DIM), lambda i: (i, 0, 0))
    o = pl.pallas_call(
        _attn_sample_core_kernel,
        grid=(n // g,),
        in_specs=[row, kv, kv],
        out_specs=row,
        out_shape=jax.ShapeDtypeStruct((n, XA_HEADS, XA_HEAD_DIM), F32),
        compiler_params=_cparams(("arbitrary",)),
        name="attn_sample_core",
    )(q.reshape(n, XA_HEADS, XA_HEAD_DIM), k, v).reshape(n, D_MODEL)
    out_specs, out_shapes = _router_out(n, n, lambda i: 0)
    return pl.pallas_call(
        _oproj_router_kernel,
        grid=(1,),
        in_specs=[_full((n, D_MODEL)), _full((n, D_MODEL)), _full((D_MODEL, D_MODEL))] + _router_specs(),
        out_specs=out_specs,
        out_shape=out_shapes,
        compiler_params=_cparams(("arbitrary",)),
        name="oproj_router_sample",
    )(x1, o, wo, *router_w)


def _moe_num_blocks(n_assign):
    return pl.cdiv(n_assign, MOE_BLOCK) + N_EXPERTS


def _slots_kernel(tp_ref, ts_ref, destp_ref, dests_ref, blke_ref, nact_ref, cnt_ref, pstart_ref,
                  rankp, ranks):
    c = LANES
    r_i = lax.broadcasted_iota(I32, (c, c), 0)
    c_i = lax.broadcasted_iota(I32, (c, c), 1)
    upper = (r_i < c_i).astype(BF16)
    e_iota = lax.broadcasted_iota(I32, (N_EXPERTS, c), 0)

    def make_body(top_ref, rank_ref):
        def body(ci, carry):
            lo = pl.multiple_of(ci * c, c)
            top = top_ref[:, pl.ds(lo, c)]
            hit = [e_iota == top[k:k + 1, :] for k in range(TOP_K)]
            cnt = sum(m.astype(F32) for m in hit)
            base = carry + jnp.dot(cnt.astype(BF16), upper, preferred_element_type=F32)
            rank_ref[:, pl.ds(lo, c)] = jnp.concatenate(
                [jnp.sum(jnp.where(m, base, 0.0), axis=0, keepdims=True) for m in hit], axis=0)
            return carry + jnp.sum(cnt, axis=1, keepdims=True)
        return body

    counts = jnp.zeros((N_EXPERTS, 1), F32)
    counts = lax.fori_loop(0, tp_ref.shape[1] // c, make_body(tp_ref, rankp), counts)
    counts = lax.fori_loop(0, ts_ref.shape[1] // c, make_body(ts_ref, ranks), counts)

    padded = jnp.floor((counts + (MOE_BLOCK - 1)) / MOE_BLOCK) * MOE_BLOCK
    er = lax.broadcasted_iota(I32, (N_EXPERTS, N_EXPERTS), 0)
    ec = lax.broadcasted_iota(I32, (N_EXPERTS, N_EXPERTS), 1)
    padded_row = jnp.sum(jnp.where(er == ec, padded, 0.0), axis=0, keepdims=True)
    cum = jnp.sum(jnp.where(ec <= er, padded_row, 0.0), axis=1, keepdims=True)
    pstart = cum - padded
    cnt_ref[...] = jnp.sum(jnp.where(er == ec, counts, 0.0), axis=0, keepdims=True).astype(I32)
    pstart_ref[...] = jnp.sum(jnp.where(er == ec, pstart, 0.0), axis=0, keepdims=True).astype(I32)

    def dest(top_ref, rank_ref, out_ref):
        n = top_ref.shape[1]
        ei = lax.broadcasted_iota(I32, (N_EXPERTS, n), 0)
        rows = []
        for k in range(TOP_K):
            start = jnp.sum(jnp.where(ei == top_ref[k:k + 1, :], pstart, 0.0), axis=0, keepdims=True)
            rows.append(start + rank_ref[k:k + 1, :])
        out_ref[...] = jnp.concatenate(rows, axis=0).astype(I32)

    dest(tp_ref, rankp, destp_ref)
    dest(ts_ref, ranks, dests_ref)

    nb = blke_ref.shape[1]
    blk_lo = lax.broadcasted_iota(I32, (N_EXPERTS, nb), 1).astype(F32) * MOE_BLOCK
    blk = jnp.sum((cum <= blk_lo).astype(F32), axis=0, keepdims=True)
    blke_ref[...] = jnp.minimum(blk, N_EXPERTS - 1).astype(I32)
    nact_ref[...] = (cum[N_EXPERTS - 1:N_EXPERTS, :] / MOE_BLOCK).astype(I32)


def _slots(top_p, top_s):
    n_p, n_s = top_p.shape[1], top_s.shape[1]
    nb = _moe_num_blocks((n_p + n_s) * TOP_K)
    return pl.pallas_call(
        _slots_kernel,
        grid=(1,),
        in_specs=[_full((TOP_K, n_p)), _full((TOP_K, n_s))],
        out_specs=[_full((TOP_K, n_p)), _full((TOP_K, n_s)), _full((1, nb)), _full((1, 1)),
                   _full((1, N_EXPERTS)), _full((1, N_EXPERTS))],
        out_shape=[
            jax.ShapeDtypeStruct((TOP_K, n_p), I32),
            jax.ShapeDtypeStruct((TOP_K, n_s), I32),
            jax.ShapeDtypeStruct((1, nb), I32),
            jax.ShapeDtypeStruct((1, 1), I32),
            jax.ShapeDtypeStruct((1, N_EXPERTS), I32),
            jax.ShapeDtypeStruct((1, N_EXPERTS), I32),
        ],
        scratch_shapes=[pltpu.VMEM((TOP_K, n_p), F32), pltpu.VMEM((TOP_K, n_s), F32)],
        compiler_params=_cparams(("arbitrary",)),
        name="moe_slots",
    )(top_p, top_s)


def _for_rows(n_rows, fn):
    def body(i, c):
        for u in range(DMA_UNROLL):
            fn(i * DMA_UNROLL + u)
        return c
    lax.fori_loop(0, n_rows // DMA_UNROLL, body, 0)


def _dispatch_kernel(cnt_ref, pstart_ref, nact_ref, destp_ref, dests_ref, xnp_ref, xns_ref, xs_ref,
                     zeros, sem, zsem):
    i = pl.program_id(0)
    last = pl.num_programs(0) - 1

    def scatter(x_ref, dest_ref):
        n = x_ref.shape[0] // ROW_TILE

        def one(t):
            for k in range(TOP_K):
                _row_tile_copy(x_ref, t, xs_ref, dest_ref[t * TOP_K + k], sem).start(priority=k % 2)

        _for_rows(n, one)
        _row_tile_copy(xs_ref, 0, xs_ref, 0, sem, n * TOP_K).wait()

    @pl.when(i < last)
    def _():
        scatter(xnp_ref, destp_ref)

    @pl.when(i == last)
    def _():
        scatter(xns_ref, dests_ref)
        zeros[...] = jnp.zeros(zeros.shape, F32)

        def pad_copies(e, act):
            c = cnt_ref[e]
            npad = (-c) & (MOE_BLOCK - 1)
            base = pstart_ref[e] + c
            for bit in range(MOE_BLOCK.bit_length() - 1):
                size = 1 << bit

                @pl.when((npad & size) != 0)
                def _():
                    act(_row_tile_copy(zeros, 0, xs_ref, base + (npad & (size - 1)), zsem, size))

        def tail_copies(act):
            def body(j, c):
                act(_row_tile_copy(zeros, 0, xs_ref, j * MOE_BLOCK, zsem, MOE_BLOCK))
                return c
            lax.fori_loop(nact_ref[0], xs_ref.shape[0] // (MOE_BLOCK * ROW_TILE), body, 0)

        for act in (lambda d: d.start(), lambda d: d.wait()):
            lax.fori_loop(0, N_EXPERTS, lambda e, c, act=act: (pad_copies(e, act), c)[1], 0)
            tail_copies(act)


def _dispatch(cnt, pstart, nact, dest_p, dest_s, xn_p, xn_s, n_rows):
    n_p, n_s = xn_p.shape[0] // ROW_TILE, xn_s.shape[0] // ROW_TILE
    tile = DISPATCH_TILE
    nt = n_p // tile
    smem = pl.BlockSpec(memory_space=pltpu.SMEM)
    return pl.pallas_call(
        _dispatch_kernel,
        grid=(nt + 1,),
        in_specs=[
            smem, smem, smem,
            pl.BlockSpec((TOP_K * tile,), lambda i: (jnp.minimum(i, nt - 1),), memory_space=pltpu.SMEM),
            smem,
            pl.BlockSpec((tile * ROW_TILE, LANES), lambda i: (jnp.minimum(i, nt - 1), 0)),
            _full((n_s * ROW_TILE, LANES)),
        ],
        out_specs=pl.BlockSpec(memory_space=pl.ANY),
        out_shape=jax.ShapeDtypeStruct((n_rows * ROW_TILE, LANES), F32),
        scratch_shapes=[pltpu.VMEM((MOE_BLOCK * ROW_TILE, LANES), F32), pltpu.SemaphoreType.DMA,
                        pltpu.SemaphoreType.DMA],
        compiler_params=_cparams(("arbitrary",)),
        name="moe_dispatch",
    )(cnt, pstart, nact, dest_p, dest_s, xn_p, xn_s)


def _combine_kernel(dest_ref, dest_next_ref, gates_ref, x2_ref, gfin_ref, ys_ref, y_ref, bufs, sems):
    tile = x2_ref.shape[0]
    i = pl.program_id(0)
    slot = i % 2

    def gather(d_ref, b):
        def one(t):
            for k in range(TOP_K):
                _row_tile_copy(ys_ref, d_ref[t * TOP_K + k], bufs.at[b, k], t, sems.at[b]).start(priority=k % 2)
        _for_rows(tile, one)

    @pl.when(i == 0)
    def _():
        gather(dest_ref, 0)

    for b in range(2):
        @pl.when(slot == b)
        def _():
            pltpu.make_async_copy(bufs.at[b], bufs.at[b], sems.at[b]).wait()

            @pl.when(i + 1 < pl.num_programs(0))
            def _():
                gather(dest_next_ref, 1 - b)

    buf = bufs.at[slot]
    sumsq = jnp.zeros((tile, 1), F32)
    gate = [jnp.broadcast_to(gates_ref[:, k:k + 1], (tile, LANES)) for k in range(TOP_K)]
    for s in range(ROW_TILE):
        cols = slice(s * LANES, (s + 1) * LANES)
        acc = x2_ref[:, cols]
        for k in range(TOP_K):
            acc = acc + gate[k] * buf[k, pl.ds(s, tile, stride=ROW_TILE), :]
        y_ref[:, cols] = acc
        sumsq = sumsq + jnp.sum(acc * acc, axis=-1, keepdims=True)
    scale = lax.rsqrt(sumsq / D_MODEL + EPS)
    for s in range(ROW_TILE):
        cols = slice(s * LANES, (s + 1) * LANES)
        y_ref[:, cols] = y_ref[:, cols] * scale * gfin_ref[:, cols]


def _combine(dest, gates_t, x2, g_final, ys):
    n = x2.shape[0]
    tile = min(COMBINE_TILE, n)
    row = pl.BlockSpec((tile, D_MODEL), lambda i: (i, 0))
    nt = n // tile
    return pl.pallas_call(
        _combine_kernel,
        grid=(nt,),
        in_specs=[
            pl.BlockSpec((TOP_K * tile,), lambda i: (i,), memory_space=pltpu.SMEM),
            pl.BlockSpec((TOP_K * tile,), lambda i: (jnp.minimum(i + 1, nt - 1),), memory_space=pltpu.SMEM),
            pl.BlockSpec((tile, TOP_K), lambda i: (i, 0)),
            row,
            _full((1, D_MODEL)),
            pl.BlockSpec(memory_space=pl.ANY),
        ],
        out_specs=row,
        out_shape=jax.ShapeDtypeStruct((n, D_MODEL), F32),
        scratch_shapes=[pltpu.VMEM((2, TOP_K, tile * ROW_TILE, LANES), F32), pltpu.SemaphoreType.DMA((2,))],
        compiler_params=_cparams(("arbitrary",)),
        name="moe_combine",
    )(dest, dest, gates_t, x2, g_final, ys)


def _experts_kernel(blke_ref, nact_ref, xs_ref, w1_hbm, b1_ref, w2_hbm, b2_ref, ys_ref,
                    w1f, w2f, w1b, w2b, wsem, started):
    j = pl.program_id(0)
    nact = nact_ref[0]
    active = j < nact
    e = blke_ref[j]
    new_expert = jnp.logical_or(j == 0, e != blke_ref[jnp.maximum(j - 1, 0)])

    def fetch(expert, slot):
        return (pltpu.make_async_copy(w1_hbm.at[expert], w1f.at[slot], wsem.at[slot]),
                pltpu.make_async_copy(w2_hbm.at[expert], w2f.at[slot], wsem.at[slot]))

    @pl.when(j == 0)
    def _():
        started[0] = 0
        for d in fetch(e, 0):
            d.start()

    @pl.when(jnp.logical_and(active, new_expert))
    def _():
        n = started[0]
        last = blke_ref.shape[0] - 1
        jn = lax.while_loop(lambda jj: jnp.logical_and(jj < nact, blke_ref[jnp.minimum(jj, last)] == e),
                            lambda jj: jj + 1, j + 1)
        for b in range(2):
            @pl.when(n % 2 == b)
            def _():
                for d in fetch(e, b):
                    d.wait()

                @pl.when(jn < nact)
                def _():
                    for d in fetch(blke_ref[jnp.minimum(jn, last)], 1 - b):
                        d.start(priority=1)

                w1b[...] = w1f[b].astype(BF16)
                w2b[...] = w2f[b].astype(BF16)
        started[0] = n + 1

    @pl.when(active)
    def _():
        x = _load_row_tiles(xs_ref, MOE_BLOCK).astype(BF16)
        gu = jnp.dot(x, w1b[...], preferred_element_type=F32) + b1_ref[0]
        g = jnp.minimum(gu[:, :D_EXPERT], SWIGLU_LIMIT)
        u = jnp.clip(gu[:, D_EXPERT:], -SWIGLU_LIMIT, SWIGLU_LIMIT)
        h = (u + 1.0) * (g * jax.nn.sigmoid(SWIGLU_ALPHA * g))
        _store_row_tiles(ys_ref, _bdot(h, w2b[...]) + b2_ref[0])

    @pl.when(jnp.logical_not(active))
    def _():
        ys_ref[...] = jnp.zeros(ys_ref.shape, F32)


def _experts(blk_e, nact, xs, w1, b1, w2, b2):
    nb = xs.shape[0] // (MOE_BLOCK * ROW_TILE)
    blk = lambda j, be, na: (jnp.minimum(j, na[0] - 1), 0)
    exp = lambda j, be, na: (be[jnp.minimum(j, na[0] - 1)], 0, 0)
    rows = (MOE_BLOCK * ROW_TILE, LANES)
    return pl.pallas_call(
        _experts_kernel,
        grid_spec=pltpu.PrefetchScalarGridSpec(
            num_scalar_prefetch=2,
            grid=(nb,),
            in_specs=[
                pl.BlockSpec(rows, blk),
                pl.BlockSpec(memory_space=pl.ANY),
                pl.BlockSpec((1, 1, 2 * D_EXPERT), exp),
                pl.BlockSpec(memory_space=pl.ANY),
                pl.BlockSpec((1, 1, D_MODEL), exp),
            ],
            out_specs=pl.BlockSpec(rows, lambda j, be, na: (j, 0)),
            scratch_shapes=[
                pltpu.VMEM((2, D_MODEL, 2 * D_EXPERT), F32), pltpu.VMEM((2, D_EXPERT, D_MODEL), F32),
                pltpu.VMEM((D_MODEL, 2 * D_EXPERT), BF16), pltpu.VMEM((D_EXPERT, D_MODEL), BF16),
                pltpu.SemaphoreType.DMA((2,)), pltpu.SMEM((1,), I32),
            ],
        ),
        out_shape=jax.ShapeDtypeStruct(xs.shape, F32),
        compiler_params=_cparams(("arbitrary",)),
        name="moe_experts",
    )(blk_e, nact, xs, w1, b1, w2, b2)


def _block_diag(w):
    h, d, _ = w.shape
    eye = jnp.eye(h, dtype=w.dtype)
    return (eye[:, None, :, None] * w[:, :, None, :]).reshape(h * d, h * d)


def kernel(x_prompt, x_sample, state_lru_h, cache_lru_conv, cache_cf_conv, cache_mem_k, cache_mem_v,
           mem_prompt, g_mix, w_in, w_lru_conv, b_lru_conv, w_lru_a, b_lru_a, w_lru_x, b_lru_x,
           lru_lambda, w_cf_conv, b_cf_conv, g_cf_ln, b_cf_ln, w_out, g_xa, g_mem, w_q, w_k, w_v, w_o,
           g_moe, w_router, b_router, w_e1, b_e1, w_e2, b_e2, g_final):
    depth = g_mix.shape[0]
    assert depth == 1, "single-layer stack"
    batch, seq, _ = x_prompt.shape
    n_s = x_sample.shape[0]
    assert x_sample.shape[1] == 1
    n_p = batch * seq
    l = 0

    mix_w = (
        g_mix[l][None], w_in[l].astype(BF16), w_lru_conv[l], b_lru_conv[l][None],
        jnp.concatenate([_block_diag(w_lru_a[l]), _block_diag(w_lru_x[l])], axis=1).astype(BF16),
        jnp.concatenate([b_lru_a[l], b_lru_x[l]])[None], lru_lambda[l][None],
        w_cf_conv[l], b_cf_conv[l][None], g_cf_ln[l][None], b_cf_ln[l][None], w_out[l].astype(BF16),
    )
    router_w = (g_moe[l][None], w_router[l].T.astype(BF16), b_router[l][:, None])
    wq, wo = w_q[l].astype(BF16), w_o[l].astype(BF16)
    gxa = g_xa[l][None]

    xp = x_prompt.reshape(n_p, D_MODEL)
    xs_ = x_sample.reshape(n_s, D_MODEL)
    x1p, hp, lcp, ccp = _mixer_prompt(xp, mix_w, batch, seq)
    x1s, hs, lru_conv_s, cf_conv_s = _mixer_sample(xs_, mix_w, state_lru_h[l], cache_lru_conv[l], cache_cf_conv[l])

    mk, mv, mkb, mvb = _memkv(mem_prompt.reshape(batch * N_MEM, D_MODEL), g_mem[l][None],
                              w_k[l].astype(BF16), w_v[l].astype(BF16))
    x2p, xnp_, top_p, gates_p = _attn_prompt(x1p, mkb, mvb, gxa, wq, wo, router_w, batch, seq)
    x2s, xns, top_s, gates_s = _attn_sample(x1s, cache_mem_k[l], cache_mem_v[l], gxa, wq, wo, router_w)

    dest_p, dest_s, blk_e, nact, cnt, pstart = _slots(top_p, top_s)
    dest_p, dest_s = dest_p.T.reshape(-1), dest_s.T.reshape(-1)
    nb = _moe_num_blocks((n_p + n_s) * TOP_K)
    nact = nact.reshape(1)
    xs = _dispatch(cnt.reshape(N_EXPERTS), pstart.reshape(N_EXPERTS), nact, dest_p, dest_s, xnp_, xns,
                   nb * MOE_BLOCK)
    ys = _experts(blk_e.reshape(nb), nact, xs, w_e1[l], b_e1[l][:, None, :], w_e2[l], b_e2[l][:, None, :])
    gfin = g_final[None]
    y_p = _combine(dest_p, gates_p.T, x2p, gfin, ys)
    y_s = _combine(dest_s, gates_s.T, x2s, gfin, ys)

    return (
        y_p.reshape(batch, seq, D_MODEL),
        y_s.reshape(n_s, 1, D_MODEL),
        hp.reshape(depth, batch, LRU_WIDTH),
        lcp[None],
        ccp[None],
        mk[None],
        mv[None],
        hs[None],
        lru_conv_s[None],
        cf_conv_s[None],
    )
```

```python
import functools

import jax
import jax.numpy as jnp
from jax import lax
from jax.experimental import pallas as pl
from jax.experimental.pallas import tpu as pltpu

F32 = jnp.float32
BF16 = jnp.bfloat16
I32 = jnp.int32

D_MODEL = 1024
LRU_WIDTH = 512
CF_WIDTH = 512
LRU_HEADS = 8
LRU_CONV = 4
LRU_C = 8.0
CF_CONV = 31
IN_WIDTH = 2 * LRU_WIDTH + 2 * CF_WIDTH
N_MEM = 256
XA_HEADS = 4
XA_HEAD_DIM = D_MODEL // XA_HEADS
N_EXPERTS = 32
TOP_K = 4
D_EXPERT = D_MODEL
SWIGLU_LIMIT = 7.0
SWIGLU_ALPHA = 1.702
EPS = 1e-6

LANES = 128
SUBLANES = 8
ROW_TILE = D_MODEL // LANES
assert ROW_TILE == SUBLANES
VMEM_LIMIT = 56 * 1024 * 1024

MIX_TILE = 1024
LRU_TAIL_ROWS = 8
CF_TAIL_ROWS = 32
ATT_TILE = 1024
SAMPLE_GROUP = 4
SAMPLE_MIX_GROUP = 32
MOE_BLOCK = 256
DMA_UNROLL = 8
DISPATCH_TILE = 2048
COMBINE_TILE = 256


def _cparams(sem):
    return pltpu.CompilerParams(dimension_semantics=sem, vmem_limit_bytes=VMEM_LIMIT)


def _full(shape):
    n = len(shape)
    return pl.BlockSpec(shape, lambda *_: (0,) * n)


def _rms(x, g):
    return x * lax.rsqrt(jnp.mean(x * x, axis=-1, keepdims=True) + EPS) * g


def _bdot(a, b):
    return jnp.dot(a.astype(BF16), b, preferred_element_type=F32)


def _store_row_tiles(ref, x):
    m = x.shape[0]
    for s in range(ROW_TILE):
        ref[pl.ds(s, m, stride=ROW_TILE), :] = x[:, s * LANES:(s + 1) * LANES]


def _load_row_tiles(ref, m):
    return jnp.concatenate([ref[pl.ds(s, m, stride=ROW_TILE), :] for s in range(ROW_TILE)], axis=-1)


def _row_tile_copy(src_ref, src_row, dst_ref, dst_row, sem, n_rows=1):
    src = src_ref.at[pl.ds(pl.multiple_of(src_row * ROW_TILE, ROW_TILE), n_rows * ROW_TILE)]
    dst = dst_ref.at[pl.ds(pl.multiple_of(dst_row * ROW_TILE, ROW_TILE), n_rows * ROW_TILE)]
    return pltpu.make_async_copy(src, dst, sem)


def _lru_coeffs(xc, wg_ref, bg_ref, lam_ref):
    gl = _bdot(xc, wg_ref[...]) + bg_ref[...]
    r = jax.nn.sigmoid(gl[:, :LRU_WIDTH])
    i = jax.nn.sigmoid(gl[:, LRU_WIDTH:])
    log_a = -LRU_C * r * jax.nn.softplus(-lam_ref[...])
    a = jnp.exp(log_a)
    u = jnp.sqrt(-jnp.tanh(log_a) * (1.0 + a * a)) * (i * xc)
    return a, u


def _cf_post(gc, gln_ref, bln_ref):
    mu = jnp.mean(gc, axis=-1, keepdims=True)
    c = gc - mu
    y = c * lax.rsqrt(jnp.mean(c * c, axis=-1, keepdims=True) + EPS)
    return jax.nn.silu(y * gln_ref[...] + bln_ref[...])


def _mixer_out(x, y_lru, y_cf, wout_ref):
    y = _bdot(y_lru, wout_ref[:LRU_WIDTH, :]) + _bdot(y_cf, wout_ref[LRU_WIDTH:, :])
    return x + y


def _mixer_prompt_kernel(x_ref, gmix_ref, win_ref, wlc_ref, blc_ref, wg_ref, bg_ref, lam_ref,
                         wcc_ref, bcc_ref, gln_ref, bln_ref, wout_ref,
                         x1_ref, h_ref, ltail_ref, ctail_ref,
                         zx_ext, glu_ext, h_carry):
    tt = MIX_TILE
    j = pl.program_id(1)

    @pl.when(j == 0)
    def _():
        zx_ext[0:LRU_TAIL_ROWS, :] = jnp.zeros((LRU_TAIL_ROWS, LRU_WIDTH), F32)
        glu_ext[0:CF_TAIL_ROWS, :] = jnp.zeros((CF_TAIL_ROWS, CF_WIDTH), F32)
        h_carry[...] = jnp.zeros((1, LRU_WIDTH), F32)

    x = x_ref[...]
    z = _bdot(_rms(x, gmix_ref[...]), win_ref[...])
    zx = z[:, :LRU_WIDTH]
    zg = z[:, LRU_WIDTH:2 * LRU_WIDTH]
    za = z[:, 2 * LRU_WIDTH:2 * LRU_WIDTH + CF_WIDTH]
    zb = z[:, 2 * LRU_WIDTH + CF_WIDTH:]

    zx_ext[LRU_TAIL_ROWS:LRU_TAIL_ROWS + tt, :] = zx
    off = LRU_TAIL_ROWS - (LRU_CONV - 1)
    xc = blc_ref[...]
    for k in range(LRU_CONV):
        xc = xc + wlc_ref[k:k + 1, :] * zx_ext[off + k:off + k + tt, :]
    a, u = _lru_coeffs(xc, wg_ref, bg_ref, lam_ref)
    sub = lax.broadcasted_iota(I32, (tt, LRU_WIDTH), 0) & (SUBLANES - 1)
    d = 1
    while d < SUBLANES:
        keep = sub >= d
        a_sh = jnp.where(keep, pltpu.roll(a, d, 0), 1.0)
        u_sh = jnp.where(keep, pltpu.roll(u, d, 0), 0.0)
        u = u + a * u_sh
        a = a * a_sh
        d *= 2
    h_last = h_carry[...]
    groups = []
    for r in range(0, tt, SUBLANES):
        hg = a[r:r + SUBLANES, :] * h_last + u[r:r + SUBLANES, :]
        groups.append(hg)
        h_last = hg[SUBLANES - 1:SUBLANES, :]
    hs = jnp.concatenate(groups, axis=0)
    h_carry[...] = h_last
    y_lru = hs * jax.nn.gelu(zg)

    glu = za * jax.nn.sigmoid(zb)
    glu_ext[CF_TAIL_ROWS:CF_TAIL_ROWS + tt, :] = glu
    off = CF_TAIL_ROWS - (CF_CONV - 1)
    gc = bcc_ref[...]
    for s in range(SUBLANES):
        rows = tt if s == 0 else tt + SUBLANES
        q = None
        for k in range(CF_CONV):
            if (off + k) % SUBLANES == s:
                base = off + k - s
                term = wcc_ref[k:k + 1, :] * glu_ext[base:base + rows, :]
                q = term if q is None else q + term
        gc = gc + (q if s == 0 else q[s:s + tt, :])
    y_cf = _cf_post(gc, gln_ref, bln_ref)

    x1_ref[...] = _mixer_out(x, y_lru, y_cf, wout_ref)

    h_ref[0] = h_last
    ltail_ref[0] = zx_ext[LRU_TAIL_ROWS + tt - (LRU_CONV - 1):LRU_TAIL_ROWS + tt, :]
    ctail_ref[0] = glu_ext[CF_TAIL_ROWS + tt - (CF_CONV - 1):CF_TAIL_ROWS + tt, :]
    zx_ext[0:LRU_TAIL_ROWS, :] = zx_ext[tt:tt + LRU_TAIL_ROWS, :]
    glu_ext[0:CF_TAIL_ROWS, :] = glu_ext[tt:tt + CF_TAIL_ROWS, :]


def _mixer_weight_specs():
    return [
        _full((1, D_MODEL)),
        _full((D_MODEL, IN_WIDTH)),
        _full((LRU_CONV, LRU_WIDTH)), _full((1, LRU_WIDTH)),
        _full((LRU_WIDTH, 2 * LRU_WIDTH)), _full((1, 2 * LRU_WIDTH)),
        _full((1, LRU_WIDTH)),
        _full((CF_CONV, CF_WIDTH)), _full((1, CF_WIDTH)),
        _full((1, CF_WIDTH)), _full((1, CF_WIDTH)),
        _full((D_MODEL, D_MODEL)),
    ]


def _mixer_prompt(x, mix_w, batch, seq):
    tt = MIX_TILE
    nt = seq // tt
    return pl.pallas_call(
        _mixer_prompt_kernel,
        grid=(batch, nt),
        in_specs=[pl.BlockSpec((tt, D_MODEL), lambda b, j: (b * nt + j, 0))] + _mixer_weight_specs(),
        out_specs=[
            pl.BlockSpec((tt, D_MODEL), lambda b, j: (b * nt + j, 0)),
            pl.BlockSpec((1, 1, LRU_WIDTH), lambda b, j: (b, 0, 0)),
            pl.BlockSpec((1, LRU_CONV - 1, LRU_WIDTH), lambda b, j: (b, 0, 0)),
            pl.BlockSpec((1, CF_CONV - 1, CF_WIDTH), lambda b, j: (b, 0, 0)),
        ],
        out_shape=[
            jax.ShapeDtypeStruct((batch * seq, D_MODEL), F32),
            jax.ShapeDtypeStruct((batch, 1, LRU_WIDTH), F32),
            jax.ShapeDtypeStruct((batch, LRU_CONV - 1, LRU_WIDTH), F32),
            jax.ShapeDtypeStruct((batch, CF_CONV - 1, CF_WIDTH), F32),
        ],
        scratch_shapes=[
            pltpu.VMEM((LRU_TAIL_ROWS + tt, LRU_WIDTH), F32),
            pltpu.VMEM((CF_TAIL_ROWS + tt, CF_WIDTH), F32),
            pltpu.VMEM((1, LRU_WIDTH), F32),
        ],
        compiler_params=_cparams(("arbitrary", "arbitrary")),
        name="mixer_prompt",
    )(x, *mix_w)


def _mixer_sample_kernel(x_ref, gmix_ref, win_ref, wlc_ref, blc_ref, wg_ref, bg_ref, lam_ref,
                         wcc_ref, bcc_ref, gln_ref, bln_ref, wout_ref,
                         h0_ref, lbuf_ref, cbuf_ref,
                         x1_ref, h_ref, ltail_ref, ctail_ref):
    x = x_ref[...]
    z = _bdot(_rms(x, gmix_ref[...]), win_ref[...])
    zx = z[:, :LRU_WIDTH]
    zg = z[:, LRU_WIDTH:2 * LRU_WIDTH]
    za = z[:, 2 * LRU_WIDTH:2 * LRU_WIDTH + CF_WIDTH]
    zb = z[:, 2 * LRU_WIDTH + CF_WIDTH:]

    xc = blc_ref[...] + wlc_ref[LRU_CONV - 1:LRU_CONV, :] * zx
    xc = xc + jnp.sum(lbuf_ref[...] * wlc_ref[0:LRU_CONV - 1, :][None], axis=1)
    a, u = _lru_coeffs(xc, wg_ref, bg_ref, lam_ref)
    h = a * h0_ref[...] + u
    y_lru = h * jax.nn.gelu(zg)

    glu = za * jax.nn.sigmoid(zb)
    gc = bcc_ref[...] + wcc_ref[CF_CONV - 1:CF_CONV, :] * glu
    gc = gc + jnp.sum(cbuf_ref[...] * wcc_ref[0:CF_CONV - 1, :][None], axis=1)
    y_cf = _cf_post(gc, gln_ref, bln_ref)

    x1_ref[...] = _mixer_out(x, y_lru, y_cf, wout_ref)
    h_ref[...] = h
    ltail_ref[:, 0:LRU_CONV - 2, :] = lbuf_ref[:, 1:LRU_CONV - 1, :]
    ctail_ref[:, 0:CF_CONV - 2, :] = cbuf_ref[:, 1:CF_CONV - 1, :]
    for i in range(x.shape[0]):
        ltail_ref[i, LRU_CONV - 2:LRU_CONV - 1, :] = zx[i:i + 1, :]
        ctail_ref[i, CF_CONV - 2:CF_CONV - 1, :] = glu[i:i + 1, :]


def _mixer_sample(x, mix_w, h0, lbuf, cbuf):
    n = x.shape[0]
    g = SAMPLE_MIX_GROUP
    row = lambda w: pl.BlockSpec((g, w), lambda i: (i, 0))
    return pl.pallas_call(
        _mixer_sample_kernel,
        grid=(n // g,),
        in_specs=[row(D_MODEL)] + _mixer_weight_specs() + [
            row(LRU_WIDTH),
            pl.BlockSpec((g, LRU_CONV - 1, LRU_WIDTH), lambda i: (i, 0, 0)),
            pl.BlockSpec((g, CF_CONV - 1, CF_WIDTH), lambda i: (i, 0, 0)),
        ],
        out_specs=[row(D_MODEL), row(LRU_WIDTH),
                   pl.BlockSpec((g, LRU_CONV - 1, LRU_WIDTH), lambda i: (i, 0, 0)),
                   pl.BlockSpec((g, CF_CONV - 1, CF_WIDTH), lambda i: (i, 0, 0))],
        out_shape=[
            jax.ShapeDtypeStruct((n, D_MODEL), F32),
            jax.ShapeDtypeStruct((n, LRU_WIDTH), F32),
            jax.ShapeDtypeStruct((n, LRU_CONV - 1, LRU_WIDTH), F32),
            jax.ShapeDtypeStruct((n, CF_CONV - 1, CF_WIDTH), F32),
        ],
        compiler_params=_cparams(("arbitrary",)),
        name="mixer_sample",
    )(x, *mix_w, h0, lbuf, cbuf)


def _memkv_kernel(mem_ref, gmem_ref, wk_ref, wv_ref, k_ref, v_ref, kb_ref, vb_ref):
    mn = _rms(mem_ref[...], gmem_ref[...]).astype(BF16)
    k = jnp.dot(mn, wk_ref[...], preferred_element_type=F32)
    v = jnp.dot(mn, wv_ref[...], preferred_element_type=F32)
    for h in range(XA_HEADS):
        sl = slice(h * XA_HEAD_DIM, (h + 1) * XA_HEAD_DIM)
        k_ref[0, :, h, :] = k[:, sl]
        v_ref[0, :, h, :] = v[:, sl]
    kb_ref[...] = k.astype(BF16)
    vb_ref[...] = v.astype(BF16)


def _memkv(mem, g_mem, wk, wv):
    n = mem.shape[0]
    t = N_MEM
    row = pl.BlockSpec((t, D_MODEL), lambda i: (i, 0))
    state = pl.BlockSpec((1, N_MEM, XA_HEADS, XA_HEAD_DIM), lambda i: (i, 0, 0, 0))
    state_shape = jax.ShapeDtypeStruct((n // t, N_MEM, XA_HEADS, XA_HEAD_DIM), F32)
    return pl.pallas_call(
        _memkv_kernel,
        grid=(n // t,),
        in_specs=[row, _full((1, D_MODEL)), _full((D_MODEL, D_MODEL)), _full((D_MODEL, D_MODEL))],
        out_specs=[state, state, row, row],
        out_shape=[state_shape] * 2 + [jax.ShapeDtypeStruct((n, D_MODEL), BF16)] * 2,
        compiler_params=_cparams(("arbitrary",)),
        name="memory_kv",
    )(mem, g_mem, wk, wv)


def _router(x2, gmoe_ref, wrt_ref, brt_ref, xn_ref, tope_ref, gates_ref):
    xn = _rms(x2, gmoe_ref[...])
    _store_row_tiles(xn_ref, xn)
    logits = lax.dot_general(wrt_ref[...], xn.astype(BF16), (((1,), (1,)), ((), ())),
                             preferred_element_type=F32) + brt_ref[...]
    e_iota = lax.broadcasted_iota(I32, logits.shape, 0)
    work = logits
    vals, idxs = [], []
    for _ in range(TOP_K):
        m = jnp.max(work, axis=0, keepdims=True)
        idx = jnp.min(jnp.where(work == m, e_iota, N_EXPERTS), axis=0, keepdims=True)
        vals.append(m)
        idxs.append(idx)
        work = jnp.where(e_iota == idx, -jnp.inf, work)
    ex = [jnp.exp(v - vals[0]) for v in vals]
    den = ex[0] + ex[1] + ex[2] + ex[3]
    tope_ref[...] = jnp.concatenate(idxs, axis=0)
    gates_ref[...] = jnp.concatenate([e / den for e in ex], axis=0)


def _router_specs():
    return [_full((1, D_MODEL)), _full((N_EXPERTS, D_MODEL)), _full((N_EXPERTS, 1))]


def _router_out(n, tile, index):
    specs = [
        pl.BlockSpec((tile, D_MODEL), lambda *a: (index(*a), 0)),
        pl.BlockSpec((tile * ROW_TILE, LANES), lambda *a: (index(*a), 0)),
        pl.BlockSpec((TOP_K, tile), lambda *a: (0, index(*a))),
        pl.BlockSpec((TOP_K, tile), lambda *a: (0, index(*a))),
    ]
    shapes = [
        jax.ShapeDtypeStruct((n, D_MODEL), F32),
        jax.ShapeDtypeStruct((n * ROW_TILE, LANES), F32),
        jax.ShapeDtypeStruct((TOP_K, n), I32),
        jax.ShapeDtypeStruct((TOP_K, n), F32),
    ]
    return specs, shapes


def _attn_prompt_kernel(x1_ref, k_ref, v_ref, gxa_ref, wq_ref, wo_ref, gmoe_ref, wrt_ref, brt_ref,
                        x2_ref, xn_ref, tope_ref, gates_ref):
    x1 = x1_ref[...]
    q = (_bdot(_rms(x1, gxa_ref[...]), wq_ref[...]) * (XA_HEAD_DIM ** -0.5)).astype(BF16)
    outs = []
    for h in range(XA_HEADS):
        sl = slice(h * XA_HEAD_DIM, (h + 1) * XA_HEAD_DIM)
        s = lax.dot_general(q[:, sl], k_ref[:, sl], (((1,), (1,)), ((), ())), preferred_element_type=F32)
        p = jnp.exp(s - jnp.max(s, axis=-1, keepdims=True))
        p = p / jnp.sum(p, axis=-1, keepdims=True)
        outs.append(_bdot(p, v_ref[:, sl]))
    o = jnp.concatenate(outs, axis=-1)
    x2 = x1 + _bdot(o, wo_ref[...])
    x2_ref[...] = x2
    _router(x2, gmoe_ref, wrt_ref, brt_ref, xn_ref, tope_ref, gates_ref)


def _attn_prompt(x1, k, v, gxa, wq, wo, router_w, batch, seq):
    tq = ATT_TILE
    nt = seq // tq
    out_specs, out_shapes = _router_out(batch * seq, tq, lambda b, j: b * nt + j)
    kv = pl.BlockSpec((N_MEM, D_MODEL), lambda b, j: (b, 0))
    return pl.pallas_call(
        _attn_prompt_kernel,
        grid=(batch, nt),
        in_specs=[pl.BlockSpec((tq, D_MODEL), lambda b, j: (b * nt + j, 0)), kv, kv,
                  _full((1, D_MODEL)), _full((D_MODEL, D_MODEL)), _full((D_MODEL, D_MODEL))] + _router_specs(),
        out_specs=out_specs,
        out_shape=out_shapes,
        compiler_params=_cparams(("arbitrary", "arbitrary")),
        name="attn_prompt",
    )(x1, k, v, gxa, wq, wo, *router_w)


def _qproj_kernel(x1_ref, gxa_ref, wq_ref, q_ref):
    q_ref[...] = _bdot(_rms(x1_ref[...], gxa_ref[...]), wq_ref[...]) * (XA_HEAD_DIM ** -0.5)


def _attn_sample_core_kernel(q_ref, k_ref, v_ref, o_ref):
    n = N_MEM * SUBLANES
    col_head = lax.broadcasted_iota(I32, (SUBLANES, n), 1) & (SUBLANES - 1)
    same_head = col_head == lax.broadcasted_iota(I32, (SUBLANES, n), 0)
    kv_pad = jnp.zeros((N_MEM, SUBLANES - XA_HEADS, XA_HEAD_DIM), F32)
    q_pad = jnp.zeros((SUBLANES - XA_HEADS, XA_HEAD_DIM), F32)
    for g in range(q_ref.shape[0]):
        k8 = jnp.concatenate([k_ref[g], kv_pad], axis=1).reshape(n, XA_HEAD_DIM).astype(BF16)
        v8 = jnp.concatenate([v_ref[g], kv_pad], axis=1).reshape(n, XA_HEAD_DIM).astype(BF16)
        q8 = jnp.concatenate([q_ref[g], q_pad], axis=0).astype(BF16)
        s = lax.dot_general(q8, k8, (((1,), (1,)), ((), ())), preferred_element_type=F32)
        m = jnp.max(jnp.where(same_head, s, -jnp.inf), axis=-1, keepdims=True)
        p = jnp.where(same_head, jnp.exp(s - m), 0.0)
        den = jnp.sum(p, axis=-1, keepdims=True)
        o8 = jnp.dot(p.astype(BF16), v8, preferred_element_type=F32)
        o_ref[g] = (o8 / den)[0:XA_HEADS, :]


def _oproj_router_kernel(x1_ref, o_ref, wo_ref, gmoe_ref, wrt_ref, brt_ref,
                         x2_ref, xn_ref, tope_ref, gates_ref):
    x2 = x1_ref[...] + _bdot(o_ref[...], wo_ref[...])
    x2_ref[...] = x2
    _router(x2, gmoe_ref, wrt_ref, brt_ref, xn_ref, tope_ref, gates_ref)


def _attn_sample(x1, k, v, gxa, wq, wo, router_w):
    n = x1.shape[0]
    g = SAMPLE_GROUP
    q = pl.pallas_call(
        _qproj_kernel,
        grid=(1,),
        in_specs=[_full((n, D_MODEL)), _full((1, D_MODEL)), _full((D_MODEL, D_MODEL))],
        out_specs=_full((n, D_MODEL)),
        out_shape=jax.ShapeDtypeStruct((n, D_MODEL), F32),
        compiler_params=_cparams(("arbitrary",)),
        name="qproj_sample",
    )(x1, gxa, wq)
    kv = pl.BlockSpec((g, N_MEM, XA_HEADS, XA_HEAD_DIM), lambda i: (i, 0, 0, 0))
    row = pl.BlockSpec((g, XA_HEADS, XA_HEAD_DIM), lambda i: (i, 0, 0))
    o = pl.pallas_call(
        _attn_sample_core_kernel,
        grid=(n // g,),
        in_specs=[row, kv, kv],
        out_specs=row,
        out_shape=jax.ShapeDtypeStruct((n, XA_HEADS, XA_HEAD_DIM), F32),
        compiler_params=_cparams(("arbitrary",)),
        name="attn_sample_core",
    )(q.reshape(n, XA_HEADS, XA_HEAD_DIM), k, v).reshape(n, D_MODEL)
    out_specs, out_shapes = _router_out(n, n, lambda i: 0)
    return pl.pallas_call(
        _oproj_router_kernel,
        grid=(1,),
        in_specs=[_full((n, D_MODEL)), _full((n, D_MODEL)), _full((D_MODEL, D_MODEL))] + _router_specs(),
        out_specs=out_specs,
        out_shape=out_shapes,
        compiler_params=_cparams(("arbitrary",)),
        name="oproj_router_sample",
    )(x1, o, wo, *router_w)


def _moe_num_blocks(n_assign):
    return pl.cdiv(n_assign, MOE_BLOCK) + N_EXPERTS


def _slots_kernel(tp_ref, ts_ref, destp_ref, dests_ref, blke_ref, nact_ref, cnt_ref, pstart_ref,
                  rankp, ranks):
    c = LANES
    r_i = lax.broadcasted_iota(I32, (c, c), 0)
    c_i = lax.broadcasted_iota(I32, (c, c), 1)
    upper = (r_i < c_i).astype(BF16)
    e_iota = lax.broadcasted_iota(I32, (N_EXPERTS, c), 0)

    def make_body(top_ref, rank_ref):
        def body(ci, carry):
            lo = pl.multiple_of(ci * c, c)
            top = top_ref[:, pl.ds(lo, c)]
            hit = [e_iota == top[k:k + 1, :] for k in range(TOP_K)]
            cnt = sum(m.astype(F32) for m in hit)
            base = carry + jnp.dot(cnt.astype(BF16), upper, preferred_element_type=F32)
            rank_ref[:, pl.ds(lo, c)] = jnp.concatenate(
                [jnp.sum(jnp.where(m, base, 0.0), axis=0, keepdims=True) for m in hit], axis=0)
            return carry + jnp.sum(cnt, axis=1, keepdims=True)
        return body

    counts = jnp.zeros((N_EXPERTS, 1), F32)
    counts = lax.fori_loop(0, tp_ref.shape[1] // c, make_body(tp_ref, rankp), counts)
    counts = lax.fori_loop(0, ts_ref.shape[1] // c, make_body(ts_ref, ranks), counts)

    padded = jnp.floor((counts + (MOE_BLOCK - 1)) / MOE_BLOCK) * MOE_BLOCK
    er = lax.broadcasted_iota(I32, (N_EXPERTS, N_EXPERTS), 0)
    ec = lax.broadcasted_iota(I32, (N_EXPERTS, N_EXPERTS), 1)
    padded_row = jnp.sum(jnp.where(er == ec, padded, 0.0), axis=0, keepdims=True)
    cum = jnp.sum(jnp.where(ec <= er, padded_row, 0.0), axis=1, keepdims=True)
    pstart = cum - padded
    cnt_ref[...] = jnp.sum(jnp.where(er == ec, counts, 0.0), axis=0, keepdims=True).astype(I32)
    pstart_ref[...] = jnp.sum(jnp.where(er == ec, pstart, 0.0), axis=0, keepdims=True).astype(I32)

    def dest(top_ref, rank_ref, out_ref):
        n = top_ref.shape[1]
        ei = lax.broadcasted_iota(I32, (N_EXPERTS, n), 0)
        rows = []
        for k in range(TOP_K):
            start = jnp.sum(jnp.where(ei == top_ref[k:k + 1, :], pstart, 0.0), axis=0, keepdims=True)
            rows.append(start + rank_ref[k:k + 1, :])
        out_ref[...] = jnp.concatenate(rows, axis=0).astype(I32)

    dest(tp_ref, rankp, destp_ref)
    dest(ts_ref, ranks, dests_ref)

    nb = blke_ref.shape[1]
    blk_lo = lax.broadcasted_iota(I32, (N_EXPERTS, nb), 1).astype(F32) * MOE_BLOCK
    blk = jnp.sum((cum <= blk_lo).astype(F32), axis=0, keepdims=True)
    blke_ref[...] = jnp.minimum(blk, N_EXPERTS - 1).astype(I32)
    nact_ref[...] = (cum[N_EXPERTS - 1:N_EXPERTS, :] / MOE_BLOCK).astype(I32)


def _slots(top_p, top_s):
    n_p, n_s = top_p.shape[1], top_s.shape[1]
    nb = _moe_num_blocks((n_p + n_s) * TOP_K)
    return pl.pallas_call(
        _slots_kernel,
        grid=(1,),
        in_specs=[_full((TOP_K, n_p)), _full((TOP_K, n_s))],
        out_specs=[_full((TOP_K, n_p)), _full((TOP_K, n_s)), _full((1, nb)), _full((1, 1)),
                   _full((1, N_EXPERTS)), _full((1, N_EXPERTS))],
        out_shape=[
            jax.ShapeDtypeStruct((TOP_K, n_p), I32),
            jax.ShapeDtypeStruct((TOP_K, n_s), I32),
            jax.ShapeDtypeStruct((1, nb), I32),
            jax.ShapeDtypeStruct((1, 1), I32),
            jax.ShapeDtypeStruct((1, N_EXPERTS), I32),
            jax.ShapeDtypeStruct((1, N_EXPERTS), I32),
        ],
        scratch_shapes=[pltpu.VMEM((TOP_K, n_p), F32), pltpu.VMEM((TOP_K, n_s), F32)],
        compiler_params=_cparams(("arbitrary",)),
        name="moe_slots",
    )(top_p, top_s)


def _for_rows(n_rows, fn):
    def body(i, c):
        for u in range(DMA_UNROLL):
            fn(i * DMA_UNROLL + u)
        return c
    lax.fori_loop(0, n_rows // DMA_UNROLL, body, 0)


def _dispatch_kernel(cnt_ref, pstart_ref, nact_ref, destp_ref, dests_ref, xnp_ref, xns_ref, xs_ref,
                     zeros, sem, zsem):
    i = pl.program_id(0)
    last = pl.num_programs(0) - 1

    def scatter(x_ref, dest_ref):
        n = x_ref.shape[0] // ROW_TILE

        def one(t):
            for k in range(TOP_K):
                _row_tile_copy(x_ref, t, xs_ref, dest_ref[t * TOP_K + k], sem).start(priority=k % 2)

        _for_rows(n, one)
        _row_tile_copy(xs_ref, 0, xs_ref, 0, sem, n * TOP_K).wait()

    @pl.when(i < last)
    def _():
        scatter(xnp_ref, destp_ref)

    @pl.when(i == last)
    def _():
        scatter(xns_ref, dests_ref)
        zeros[...] = jnp.zeros(zeros.shape, F32)

        def pad_copies(e, act):
            c = cnt_ref[e]
            npad = (-c) & (MOE_BLOCK - 1)
            base = pstart_ref[e] + c
            for bit in range(MOE_BLOCK.bit_length() - 1):
                size = 1 << bit

                @pl.when((npad & size) != 0)
                def _():
                    act(_row_tile_copy(zeros, 0, xs_ref, base + (npad & (size - 1)), zsem, size))

        def tail_copies(act):
            def body(j, c):
                act(_row_tile_copy(zeros, 0, xs_ref, j * MOE_BLOCK, zsem, MOE_BLOCK))
                return c
            lax.fori_loop(nact_ref[0], xs_ref.shape[0] // (MOE_BLOCK * ROW_TILE), body, 0)

        for act in (lambda d: d.start(), lambda d: d.wait()):
            lax.fori_loop(0, N_EXPERTS, lambda e, c, act=act: (pad_copies(e, act), c)[1], 0)
            tail_copies(act)


def _dispatch(cnt, pstart, nact, dest_p, dest_s, xn_p, xn_s, n_rows):
    n_p, n_s = xn_p.shape[0] // ROW_TILE, xn_s.shape[0] // ROW_TILE
    tile = DISPATCH_TILE
    nt = n_p // tile
    smem = pl.BlockSpec(memory_space=pltpu.SMEM)
    return pl.pallas_call(
        _dispatch_kernel,
        grid=(nt + 1,),
        in_specs=[
            smem, smem, smem,
            pl.BlockSpec((TOP_K * tile,), lambda i: (jnp.minimum(i, nt - 1),), memory_space=pltpu.SMEM),
            smem,
            pl.BlockSpec((tile * ROW_TILE, LANES), lambda i: (jnp.minimum(i, nt - 1), 0)),
            _full((n_s * ROW_TILE, LANES)),
        ],
        out_specs=pl.BlockSpec(memory_space=pl.ANY),
        out_shape=jax.ShapeDtypeStruct((n_rows * ROW_TILE, LANES), F32),
        scratch_shapes=[pltpu.VMEM((MOE_BLOCK * ROW_TILE, LANES), F32), pltpu.SemaphoreType.DMA,
                        pltpu.SemaphoreType.DMA],
        compiler_params=_cparams(("arbitrary",)),
        name="moe_dispatch",
    )(cnt, pstart, nact, dest_p, dest_s, xn_p, xn_s)


def _combine_kernel(dest_ref, dest_next_ref, gates_ref, x2_ref, gfin_ref, ys_ref, y_ref, bufs, sems):
    tile = x2_ref.shape[0]
    i = pl.program_id(0)
    slot = i % 2

    def gather(d_ref, b):
        def one(t):
            for k in range(TOP_K):
                _row_tile_copy(ys_ref, d_ref[t * TOP_K + k], bufs.at[b, k], t, sems.at[b]).start(priority=k % 2)
        _for_rows(tile, one)

    @pl.when(i == 0)
    def _():
        gather(dest_ref, 0)

    for b in range(2):
        @pl.when(slot == b)
        def _():
            pltpu.make_async_copy(bufs.at[b], bufs.at[b], sems.at[b]).wait()

            @pl.when(i + 1 < pl.num_programs(0))
            def _():
                gather(dest_next_ref, 1 - b)

    buf = bufs.at[slot]
    sumsq = jnp.zeros((tile, 1), F32)
    gate = [jnp.broadcast_to(gates_ref[:, k:k + 1], (tile, LANES)) for k in range(TOP_K)]
    for s in range(ROW_TILE):
        cols = slice(s * LANES, (s + 1) * LANES)
        acc = x2_ref[:, cols]
        for k in range(TOP_K):
            acc = acc + gate[k] * buf[k, pl.ds(s, tile, stride=ROW_TILE), :]
        y_ref[:, cols] = acc
        sumsq = sumsq + jnp.sum(acc * acc, axis=-1, keepdims=True)
    scale = lax.rsqrt(sumsq / D_MODEL + EPS)
    for s in range(ROW_TILE):
        cols = slice(s * LANES, (s + 1) * LANES)
        y_ref[:, cols] = y_ref[:, cols] * scale * gfin_ref[:, cols]


def _combine(dest, gates_t, x2, g_final, ys):
    n = x2.shape[0]
    tile = min(COMBINE_TILE, n)
    row = pl.BlockSpec((tile, D_MODEL), lambda i: (i, 0))
    nt = n // tile
    return pl.pallas_call(
        _combine_kernel,
        grid=(nt,),
        in_specs=[
            pl.BlockSpec((TOP_K * tile,), lambda i: (i,), memory_space=pltpu.SMEM),
            pl.BlockSpec((TOP_K * tile,), lambda i: (jnp.minimum(i + 1, nt - 1),), memory_space=pltpu.SMEM),
            pl.BlockSpec((tile, TOP_K), lambda i: (i, 0)),
            row,
            _full((1, D_MODEL)),
            pl.BlockSpec(memory_space=pl.ANY),
        ],
        out_specs=row,
        out_shape=jax.ShapeDtypeStruct((n, D_MODEL), F32),
        scratch_shapes=[pltpu.VMEM((2, TOP_K, tile * ROW_TILE, LANES), F32), pltpu.SemaphoreType.DMA((2,))],
        compiler_params=_cparams(("arbitrary",)),
        name="moe_combine",
    )(dest, dest, gates_t, x2, g_final, ys)


def _experts_kernel(blke_ref, nact_ref, xs_ref, w1_hbm, b1a_ref, b1b_ref, w2_hbm, b2a_ref, b2b_ref, ys_ref,
                    w1f, w2f, w1b, w2b, wsem, started):
    j = pl.program_id(0)
    nact = nact_ref[0]
    last = blke_ref.shape[0] - 1
    first, second = 2 * j, 2 * j + 1
    e0, e1 = blke_ref[first], blke_ref[second]
    active0, active1 = first < nact, second < nact
    together = jnp.logical_and(active1, e1 == e0)
    half = MOE_BLOCK * ROW_TILE

    def fetch(expert, slot):
        return (pltpu.make_async_copy(w1_hbm.at[expert], w1f.at[slot], wsem.at[slot]),
                pltpu.make_async_copy(w2_hbm.at[expert], w2f.at[slot], wsem.at[slot]))

    def switch_to(blk):
        e = blke_ref[blk]
        n = started[0]
        jn = lax.while_loop(lambda jj: jnp.logical_and(jj < nact, blke_ref[jnp.minimum(jj, last)] == e),
                            lambda jj: jj + 1, blk + 1)
        for b in range(2):
            @pl.when(n % 2 == b)
            def _():
                for d in fetch(e, b):
                    d.wait()

                @pl.when(jn < nact)
                def _():
                    for d in fetch(blke_ref[jnp.minimum(jn, last)], 1 - b):
                        d.start(priority=1)

                w1b[...] = w1f[b].astype(BF16)
                w2b[...] = w2f[b].astype(BF16)
        started[0] = n + 1

    def ffn(x_ref, n_rows, b1_ref, b2_ref, y_ref):
        x = _load_row_tiles(x_ref, n_rows).astype(BF16)
        gu = jnp.dot(x, w1b[...], preferred_element_type=F32) + b1_ref[0]
        g = jnp.minimum(gu[:, :D_EXPERT], SWIGLU_LIMIT)
        u = jnp.clip(gu[:, D_EXPERT:], -SWIGLU_LIMIT, SWIGLU_LIMIT)
        h = (u + 1.0) * (g * jax.nn.sigmoid(SWIGLU_ALPHA * g))
        _store_row_tiles(y_ref, _bdot(h, w2b[...]) + b2_ref[0])

    @pl.when(j == 0)
    def _():
        started[0] = 0
        for d in fetch(e0, 0):
            d.start()

    @pl.when(jnp.logical_and(active0, jnp.logical_or(j == 0, e0 != blke_ref[jnp.maximum(first - 1, 0)])))
    def _():
        switch_to(first)

    @pl.when(together)
    def _():
        ffn(xs_ref, 2 * MOE_BLOCK, b1a_ref, b2a_ref, ys_ref)

    @pl.when(jnp.logical_and(active0, jnp.logical_not(together)))
    def _():
        ffn(xs_ref.at[pl.ds(0, half)], MOE_BLOCK, b1a_ref, b2a_ref, ys_ref.at[pl.ds(0, half)])

        @pl.when(active1)
        def _():
            switch_to(second)
            ffn(xs_ref.at[pl.ds(half, half)], MOE_BLOCK, b1b_ref, b2b_ref, ys_ref.at[pl.ds(half, half)])

        @pl.when(jnp.logical_not(active1))
        def _():
            ys_ref[pl.ds(half, half), :] = jnp.zeros((half, LANES), F32)

    @pl.when(jnp.logical_not(active0))
    def _():
        ys_ref[...] = jnp.zeros(ys_ref.shape, F32)


def _experts(blk_e, nact, xs, w1, b1, w2, b2):
    nb = xs.shape[0] // (MOE_BLOCK * ROW_TILE)
    assert nb % 2 == 0
    blk = lambda j, be, na: (jnp.minimum(j, (na[0] - 1) // 2), 0)
    exp = lambda h: (lambda j, be, na: (be[jnp.minimum(2 * j + h, na[0] - 1)], 0, 0))
    rows = (2 * MOE_BLOCK * ROW_TILE, LANES)
    return pl.pallas_call(
        _experts_kernel,
        grid_spec=pltpu.PrefetchScalarGridSpec(
            num_scalar_prefetch=2,
            grid=(nb // 2,),
            in_specs=[
                pl.BlockSpec(rows, blk),
                pl.BlockSpec(memory_space=pl.ANY),
                pl.BlockSpec((1, 1, 2 * D_EXPERT), exp(0)),
                pl.BlockSpec((1, 1, 2 * D_EXPERT), exp(1)),
                pl.BlockSpec(memory_space=pl.ANY),
                pl.BlockSpec((1, 1, D_MODEL), exp(0)),
                pl.BlockSpec((1, 1, D_MODEL), exp(1)),
            ],
            out_specs=pl.BlockSpec(rows, lambda j, be, na: (j, 0)),
            scratch_shapes=[
                pltpu.VMEM((2, D_MODEL, 2 * D_EXPERT), F32), pltpu.VMEM((2, D_EXPERT, D_MODEL), F32),
                pltpu.VMEM((D_MODEL, 2 * D_EXPERT), BF16), pltpu.VMEM((D_EXPERT, D_MODEL), BF16),
                pltpu.SemaphoreType.DMA((2,)), pltpu.SMEM((1,), I32),
            ],
        ),
        out_shape=jax.ShapeDtypeStruct(xs.shape, F32),
        compiler_params=_cparams(("arbitrary",)),
        name="moe_experts",
    )(blk_e, nact, xs, w1, b1, b1, w2, b2, b2)


def _block_diag(w):
    h, d, _ = w.shape
    eye = jnp.eye(h, dtype=w.dtype)
    return (eye[:, None, :, None] * w[:, :, None, :]).reshape(h * d, h * d)


def kernel(x_prompt, x_sample, state_lru_h, cache_lru_conv, cache_cf_conv, cache_mem_k, cache_mem_v,
           mem_prompt, g_mix, w_in, w_lru_conv, b_lru_conv, w_lru_a, b_lru_a, w_lru_x, b_lru_x,
           lru_lambda, w_cf_conv, b_cf_conv, g_cf_ln, b_cf_ln, w_out, g_xa, g_mem, w_q, w_k, w_v, w_o,
           g_moe, w_router, b_router, w_e1, b_e1, w_e2, b_e2, g_final):
    depth = g_mix.shape[0]
    assert depth == 1, "single-layer stack"
    batch, seq, _ = x_prompt.shape
    n_s = x_sample.shape[0]
    assert x_sample.shape[1] == 1
    n_p = batch * seq
    l = 0

    mix_w = (
        g_mix[l][None], w_in[l].astype(BF16), w_lru_conv[l], b_lru_conv[l][None],
        jnp.concatenate([_block_diag(w_lru_a[l]), _block_diag(w_lru_x[l])], axis=1).astype(BF16),
        jnp.concatenate([b_lru_a[l], b_lru_x[l]])[None], lru_lambda[l][None],
        w_cf_conv[l], b_cf_conv[l][None], g_cf_ln[l][None], b_cf_ln[l][None], w_out[l].astype(BF16),
    )
    router_w = (g_moe[l][None], w_router[l].T.astype(BF16), b_router[l][:, None])
    wq, wo = w_q[l].astype(BF16), w_o[l].astype(BF16)
    gxa = g_xa[l][None]

    xp = x_prompt.reshape(n_p, D_MODEL)
    xs_ = x_sample.reshape(n_s, D_MODEL)
    x1p, hp, lcp, ccp = _mixer_prompt(xp, mix_w, batch, seq)
    x1s, hs, lru_conv_s, cf_conv_s = _mixer_sample(xs_, mix_w, state_lru_h[l], cache_lru_conv[l], cache_cf_conv[l])

    mk, mv, mkb, mvb = _memkv(mem_prompt.reshape(batch * N_MEM, D_MODEL), g_mem[l][None],
                              w_k[l].astype(BF16), w_v[l].astype(BF16))
    x2p, xnp_, top_p, gates_p = _attn_prompt(x1p, mkb, mvb, gxa, wq, wo, router_w, batch, seq)
    x2s, xns, top_s, gates_s = _attn_sample(x1s, cache_mem_k[l], cache_mem_v[l], gxa, wq, wo, router_w)

    dest_p, dest_s, blk_e, nact, cnt, pstart = _slots(top_p, top_s)
    dest_p, dest_s = dest_p.T.reshape(-1), dest_s.T.reshape(-1)
    nb = _moe_num_blocks((n_p + n_s) * TOP_K)
    nact = nact.reshape(1)
    xs = _dispatch(cnt.reshape(N_EXPERTS), pstart.reshape(N_EXPERTS), nact, dest_p, dest_s, xnp_, xns,
                   nb * MOE_BLOCK)
    ys = _experts(blk_e.reshape(nb), nact, xs, w_e1[l], b_e1[l][:, None, :], w_e2[l], b_e2[l][:, None, :])
    gfin = g_final[None]
    y_p = _combine(dest_p, gates_p.T, x2p, gfin, ys)
    y_s = _combine(dest_s, gates_s.T, x2s, gfin, ys)

    return (
        y_p.reshape(batch, seq, D_MODEL),
        y_s.reshape(n_s, 1, D_MODEL),
        hp.reshape(depth, batch, LRU_WIDTH),
        lcp[None],
        ccp[None],
        mk[None],
        mv[None],
        hs[None],
        lru_conv_s[None],
        cf_conv_s[None],
    )
```

```python
import functools

import jax
import jax.numpy as jnp
from jax import lax
from jax.experimental import pallas as pl
from jax.experimental.pallas import tpu as pltpu

F32 = jnp.float32
BF16 = jnp.bfloat16
I32 = jnp.int32

D_MODEL = 1024
LRU_WIDTH = 512
CF_WIDTH = 512
LRU_HEADS = 8
LRU_CONV = 4
LRU_C = 8.0
CF_CONV = 31
IN_WIDTH = 2 * LRU_WIDTH + 2 * CF_WIDTH
N_MEM = 256
XA_HEADS = 4
XA_HEAD_DIM = D_MODEL // XA_HEADS
N_EXPERTS = 32
TOP_K = 4
D_EXPERT = D_MODEL
SWIGLU_LIMIT = 7.0
SWIGLU_ALPHA = 1.702
EPS = 1e-6

LANES = 128
SUBLANES = 8
ROW_TILE = D_MODEL // LANES
assert ROW_TILE == SUBLANES
VMEM_LIMIT = 56 * 1024 * 1024

MIX_TILE = 1024
LRU_TAIL_ROWS = 8
CF_TAIL_ROWS = 32
ATT_TILE = 1024
SAMPLE_GROUP = 4
SAMPLE_MIX_GROUP = 32
MOE_BLOCK = 256
DMA_UNROLL = 8
DISPATCH_TILE = 2048
COMBINE_TILE = 256


def _cparams(sem):
    return pltpu.CompilerParams(dimension_semantics=sem, vmem_limit_bytes=VMEM_LIMIT)


def _full(shape):
    n = len(shape)
    return pl.BlockSpec(shape, lambda *_: (0,) * n)


def _rms(x, g):
    return x * lax.rsqrt(jnp.mean(x * x, axis=-1, keepdims=True) + EPS) * g


def _bdot(a, b):
    return jnp.dot(a.astype(BF16), b, preferred_element_type=F32)


def _store_row_tiles(ref, x):
    m = x.shape[0]
    for s in range(ROW_TILE):
        ref[pl.ds(s, m, stride=ROW_TILE), :] = x[:, s * LANES:(s + 1) * LANES]


def _load_row_tiles(ref, m):
    return jnp.concatenate([ref[pl.ds(s, m, stride=ROW_TILE), :] for s in range(ROW_TILE)], axis=-1)


def _row_tile_copy(src_ref, src_row, dst_ref, dst_row, sem, n_rows=1):
    src = src_ref.at[pl.ds(pl.multiple_of(src_row * ROW_TILE, ROW_TILE), n_rows * ROW_TILE)]
    dst = dst_ref.at[pl.ds(pl.multiple_of(dst_row * ROW_TILE, ROW_TILE), n_rows * ROW_TILE)]
    return pltpu.make_async_copy(src, dst, sem)


def _lru_coeffs(xc, wg_ref, bg_ref, lam_ref):
    gl = _bdot(xc, wg_ref[...]) + bg_ref[...]
    r = jax.nn.sigmoid(gl[:, :LRU_WIDTH])
    i = jax.nn.sigmoid(gl[:, LRU_WIDTH:])
    log_a = -LRU_C * r * jax.nn.softplus(-lam_ref[...])
    a = jnp.exp(log_a)
    u = jnp.sqrt(-jnp.tanh(log_a) * (1.0 + a * a)) * (i * xc)
    return a, u


def _cf_post(gc, gln_ref, bln_ref):
    mu = jnp.mean(gc, axis=-1, keepdims=True)
    c = gc - mu
    y = c * lax.rsqrt(jnp.mean(c * c, axis=-1, keepdims=True) + EPS)
    return jax.nn.silu(y * gln_ref[...] + bln_ref[...])


def _mixer_out(x, y_lru, y_cf, wout_ref):
    y = _bdot(y_lru, wout_ref[:LRU_WIDTH, :]) + _bdot(y_cf, wout_ref[LRU_WIDTH:, :])
    return x + y


def _mixer_prompt_kernel(x_ref, gmix_ref, win_ref, wlc_ref, blc_ref, wg_ref, bg_ref, lam_ref,
                         wcc_ref, bcc_ref, gln_ref, bln_ref, wout_ref,
                         x1_ref, h_ref, ltail_ref, ctail_ref,
                         zx_ext, glu_ext, h_carry):
    tt = MIX_TILE
    j = pl.program_id(1)

    @pl.when(j == 0)
    def _():
        zx_ext[0:LRU_TAIL_ROWS, :] = jnp.zeros((LRU_TAIL_ROWS, LRU_WIDTH), F32)
        glu_ext[0:CF_TAIL_ROWS, :] = jnp.zeros((CF_TAIL_ROWS, CF_WIDTH), F32)
        h_carry[...] = jnp.zeros((1, LRU_WIDTH), F32)

    x = x_ref[...]
    z = _bdot(_rms(x, gmix_ref[...]), win_ref[...])
    zx = z[:, :LRU_WIDTH]
    zg = z[:, LRU_WIDTH:2 * LRU_WIDTH]
    za = z[:, 2 * LRU_WIDTH:2 * LRU_WIDTH + CF_WIDTH]
    zb = z[:, 2 * LRU_WIDTH + CF_WIDTH:]

    zx_ext[LRU_TAIL_ROWS:LRU_TAIL_ROWS + tt, :] = zx
    off = LRU_TAIL_ROWS - (LRU_CONV - 1)
    xc = blc_ref[...]
    for k in range(LRU_CONV):
        xc = xc + wlc_ref[k:k + 1, :] * zx_ext[off + k:off + k + tt, :]
    a, u = _lru_coeffs(xc, wg_ref, bg_ref, lam_ref)
    sub = lax.broadcasted_iota(I32, (tt, LRU_WIDTH), 0) & (SUBLANES - 1)
    d = 1
    while d < SUBLANES:
        keep = sub >= d
        a_sh = jnp.where(keep, pltpu.roll(a, d, 0), 1.0)
        u_sh = jnp.where(keep, pltpu.roll(u, d, 0), 0.0)
        u = u + a * u_sh
        a = a * a_sh
        d *= 2
    h_last = h_carry[...]
    groups = []
    for r in range(0, tt, SUBLANES):
        hg = a[r:r + SUBLANES, :] * h_last + u[r:r + SUBLANES, :]
        groups.append(hg)
        h_last = hg[SUBLANES - 1:SUBLANES, :]
    hs = jnp.concatenate(groups, axis=0)
    h_carry[...] = h_last
    y_lru = hs * jax.nn.gelu(zg)

    glu = za * jax.nn.sigmoid(zb)
    glu_ext[CF_TAIL_ROWS:CF_TAIL_ROWS + tt, :] = glu
    off = CF_TAIL_ROWS - (CF_CONV - 1)
    gc = bcc_ref[...]
    for s in range(SUBLANES):
        rows = tt if s == 0 else tt + SUBLANES
        q = None
        for k in range(CF_CONV):
            if (off + k) % SUBLANES == s:
                base = off + k - s
                term = wcc_ref[k:k + 1, :] * glu_ext[base:base + rows, :]
                q = term if q is None else q + term
        gc = gc + (q if s == 0 else q[s:s + tt, :])
    y_cf = _cf_post(gc, gln_ref, bln_ref)

    x1_ref[...] = _mixer_out(x, y_lru, y_cf, wout_ref)

    h_ref[0] = h_last
    ltail_ref[0] = zx_ext[LRU_TAIL_ROWS + tt - (LRU_CONV - 1):LRU_TAIL_ROWS + tt, :]
    ctail_ref[0] = glu_ext[CF_TAIL_ROWS + tt - (CF_CONV - 1):CF_TAIL_ROWS + tt, :]
    zx_ext[0:LRU_TAIL_ROWS, :] = zx_ext[tt:tt + LRU_TAIL_ROWS, :]
    glu_ext[0:CF_TAIL_ROWS, :] = glu_ext[tt:tt + CF_TAIL_ROWS, :]


def _mixer_weight_specs():
    return [
        _full((1, D_MODEL)),
        _full((D_MODEL, IN_WIDTH)),
        _full((LRU_CONV, LRU_WIDTH)), _full((1, LRU_WIDTH)),
        _full((LRU_WIDTH, 2 * LRU_WIDTH)), _full((1, 2 * LRU_WIDTH)),
        _full((1, LRU_WIDTH)),
        _full((CF_CONV, CF_WIDTH)), _full((1, CF_WIDTH)),
        _full((1, CF_WIDTH)), _full((1, CF_WIDTH)),
        _full((D_MODEL, D_MODEL)),
    ]


def _mixer_prompt(x, mix_w, batch, seq):
    tt = MIX_TILE
    nt = seq // tt
    return pl.pallas_call(
        _mixer_prompt_kernel,
        grid=(batch, nt),
        in_specs=[pl.BlockSpec((tt, D_MODEL), lambda b, j: (b * nt + j, 0))] + _mixer_weight_specs(),
        out_specs=[
            pl.BlockSpec((tt, D_MODEL), lambda b, j: (b * nt + j, 0)),
            pl.BlockSpec((1, 1, LRU_WIDTH), lambda b, j: (b, 0, 0)),
            pl.BlockSpec((1, LRU_CONV - 1, LRU_WIDTH), lambda b, j: (b, 0, 0)),
            pl.BlockSpec((1, CF_CONV - 1, CF_WIDTH), lambda b, j: (b, 0, 0)),
        ],
        out_shape=[
            jax.ShapeDtypeStruct((batch * seq, D_MODEL), F32),
            jax.ShapeDtypeStruct((batch, 1, LRU_WIDTH), F32),
            jax.ShapeDtypeStruct((batch, LRU_CONV - 1, LRU_WIDTH), F32),
            jax.ShapeDtypeStruct((batch, CF_CONV - 1, CF_WIDTH), F32),
        ],
        scratch_shapes=[
            pltpu.VMEM((LRU_TAIL_ROWS + tt, LRU_WIDTH), F32),
            pltpu.VMEM((CF_TAIL_ROWS + tt, CF_WIDTH), F32),
            pltpu.VMEM((1, LRU_WIDTH), F32),
        ],
        compiler_params=_cparams(("arbitrary", "arbitrary")),
        name="mixer_prompt",
    )(x, *mix_w)


def _mixer_sample_kernel(x_ref, gmix_ref, win_ref, wlc_ref, blc_ref, wg_ref, bg_ref, lam_ref,
                         wcc_ref, bcc_ref, gln_ref, bln_ref, wout_ref,
                         h0_ref, lbuf_ref, cbuf_ref,
                         x1_ref, h_ref, ltail_ref, ctail_ref):
    x = x_ref[...]
    z = _bdot(_rms(x, gmix_ref[...]), win_ref[...])
    zx = z[:, :LRU_WIDTH]
    zg = z[:, LRU_WIDTH:2 * LRU_WIDTH]
    za = z[:, 2 * LRU_WIDTH:2 * LRU_WIDTH + CF_WIDTH]
    zb = z[:, 2 * LRU_WIDTH + CF_WIDTH:]

    xc = blc_ref[...] + wlc_ref[LRU_CONV - 1:LRU_CONV, :] * zx
    xc = xc + jnp.sum(lbuf_ref[...] * wlc_ref[0:LRU_CONV - 1, :][None], axis=1)
    a, u = _lru_coeffs(xc, wg_ref, bg_ref, lam_ref)
    h = a * h0_ref[...] + u
    y_lru = h * jax.nn.gelu(zg)

    glu = za * jax.nn.sigmoid(zb)
    gc = bcc_ref[...] + wcc_ref[CF_CONV - 1:CF_CONV, :] * glu
    gc = gc + jnp.sum(cbuf_ref[...] * wcc_ref[0:CF_CONV - 1, :][None], axis=1)
    y_cf = _cf_post(gc, gln_ref, bln_ref)

    x1_ref[...] = _mixer_out(x, y_lru, y_cf, wout_ref)
    h_ref[...] = h
    ltail_ref[:, 0:LRU_CONV - 2, :] = lbuf_ref[:, 1:LRU_CONV - 1, :]
    ctail_ref[:, 0:CF_CONV - 2, :] = cbuf_ref[:, 1:CF_CONV - 1, :]
    for i in range(x.shape[0]):
        ltail_ref[i, LRU_CONV - 2:LRU_CONV - 1, :] = zx[i:i + 1, :]
        ctail_ref[i, CF_CONV - 2:CF_CONV - 1, :] = glu[i:i + 1, :]


def _mixer_sample(x, mix_w, h0, lbuf, cbuf):
    n = x.shape[0]
    g = SAMPLE_MIX_GROUP
    row = lambda w: pl.BlockSpec((g, w), lambda i: (i, 0))
    return pl.pallas_call(
        _mixer_sample_kernel,
        grid=(n // g,),
        in_specs=[row(D_MODEL)] + _mixer_weight_specs() + [
            row(LRU_WIDTH),
            pl.BlockSpec((g, LRU_CONV - 1, LRU_WIDTH), lambda i: (i, 0, 0)),
            pl.BlockSpec((g, CF_CONV - 1, CF_WIDTH), lambda i: (i, 0, 0)),
        ],
        out_specs=[row(D_MODEL), row(LRU_WIDTH),
                   pl.BlockSpec((g, LRU_CONV - 1, LRU_WIDTH), lambda i: (i, 0, 0)),
                   pl.BlockSpec((g, CF_CONV - 1, CF_WIDTH), lambda i: (i, 0, 0))],
        out_shape=[
            jax.ShapeDtypeStruct((n, D_MODEL), F32),
            jax.ShapeDtypeStruct((n, LRU_WIDTH), F32),
            jax.ShapeDtypeStruct((n, LRU_CONV - 1, LRU_WIDTH), F32),
            jax.ShapeDtypeStruct((n, CF_CONV - 1, CF_WIDTH), F32),
        ],
        compiler_params=_cparams(("arbitrary",)),
        name="mixer_sample",
    )(x, *mix_w, h0, lbuf, cbuf)


def _memkv_kernel(mem_ref, gmem_ref, wk_ref, wv_ref, k_ref, v_ref, kb_ref, vb_ref):
    mn = _rms(mem_ref[...], gmem_ref[...]).astype(BF16)
    k = jnp.dot(mn, wk_ref[...], preferred_element_type=F32)
    v = jnp.dot(mn, wv_ref[...], preferred_element_type=F32)
    for h in range(XA_HEADS):
        sl = slice(h * XA_HEAD_DIM, (h + 1) * XA_HEAD_DIM)
        k_ref[0, :, h, :] = k[:, sl]
        v_ref[0, :, h, :] = v[:, sl]
    kb_ref[...] = k.astype(BF16)
    vb_ref[...] = v.astype(BF16)


def _memkv(mem, g_mem, wk, wv):
    n = mem.shape[0]
    t = N_MEM
    row = pl.BlockSpec((t, D_MODEL), lambda i: (i, 0))
    state = pl.BlockSpec((1, N_MEM, XA_HEADS, XA_HEAD_DIM), lambda i: (i, 0, 0, 0))
    state_shape = jax.ShapeDtypeStruct((n // t, N_MEM, XA_HEADS, XA_HEAD_DIM), F32)
    return pl.pallas_call(
        _memkv_kernel,
        grid=(n // t,),
        in_specs=[row, _full((1, D_MODEL)), _full((D_MODEL, D_MODEL)), _full((D_MODEL, D_MODEL))],
        out_specs=[state, state, row, row],
        out_shape=[state_shape] * 2 + [jax.ShapeDtypeStruct((n, D_MODEL), BF16)] * 2,
        compiler_params=_cparams(("arbitrary",)),
        name="memory_kv",
    )(mem, g_mem, wk, wv)


def _router(x2, gmoe_ref, wrt_ref, brt_ref, xn_ref, tope_ref, gates_ref):
    xn = _rms(x2, gmoe_ref[...])
    _store_row_tiles(xn_ref, xn)
    logits = lax.dot_general(wrt_ref[...], xn.astype(BF16), (((1,), (1,)), ((), ())),
                             preferred_element_type=F32) + brt_ref[...]
    e_iota = lax.broadcasted_iota(I32, logits.shape, 0)
    work = logits
    vals, idxs = [], []
    for _ in range(TOP_K):
        m = jnp.max(work, axis=0, keepdims=True)
        idx = jnp.min(jnp.where(work == m, e_iota, N_EXPERTS), axis=0, keepdims=True)
        vals.append(m)
        idxs.append(idx)
        work = jnp.where(e_iota == idx, -jnp.inf, work)
    ex = [jnp.exp(v - vals[0]) for v in vals]
    den = ex[0] + ex[1] + ex[2] + ex[3]
    tope_ref[...] = jnp.concatenate(idxs, axis=0)
    gates_ref[...] = jnp.concatenate([e / den for e in ex], axis=0)


def _router_specs():
    return [_full((1, D_MODEL)), _full((N_EXPERTS, D_MODEL)), _full((N_EXPERTS, 1))]


def _router_out(n, tile, index):
    specs = [
        pl.BlockSpec((tile, D_MODEL), lambda *a: (index(*a), 0)),
        pl.BlockSpec((tile * ROW_TILE, LANES), lambda *a: (index(*a), 0)),
        pl.BlockSpec((TOP_K, tile), lambda *a: (0, index(*a))),
        pl.BlockSpec((TOP_K, tile), lambda *a: (0, index(*a))),
    ]
    shapes = [
        jax.ShapeDtypeStruct((n, D_MODEL), F32),
        jax.ShapeDtypeStruct((n * ROW_TILE, LANES), F32),
        jax.ShapeDtypeStruct((TOP_K, n), I32),
        jax.ShapeDtypeStruct((TOP_K, n), F32),
    ]
    return specs, shapes


def _attn_prompt_kernel(x1_ref, k_ref, v_ref, gxa_ref, wq_ref, wo_ref, gmoe_ref, wrt_ref, brt_ref,
                        x2_ref, xn_ref, tope_ref, gates_ref):
    x1 = x1_ref[...]
    q = (_bdot(_rms(x1, gxa_ref[...]), wq_ref[...]) * (XA_HEAD_DIM ** -0.5)).astype(BF16)
    outs = []
    for h in range(XA_HEADS):
        sl = slice(h * XA_HEAD_DIM, (h + 1) * XA_HEAD_DIM)
        s = lax.dot_general(q[:, sl], k_ref[:, sl], (((1,), (1,)), ((), ())), preferred_element_type=F32)
        p = jnp.exp(s - jnp.max(s, axis=-1, keepdims=True))
        p = p / jnp.sum(p, axis=-1, keepdims=True)
        outs.append(_bdot(p, v_ref[:, sl]))
    o = jnp.concatenate(outs, axis=-1)
    x2 = x1 + _bdot(o, wo_ref[...])
    x2_ref[...] = x2
    _router(x2, gmoe_ref, wrt_ref, brt_ref, xn_ref, tope_ref, gates_ref)


def _attn_prompt(x1, k, v, gxa, wq, wo, router_w, batch, seq):
    tq = ATT_TILE
    nt = seq // tq
    out_specs, out_shapes = _router_out(batch * seq, tq, lambda b, j: b * nt + j)
    kv = pl.BlockSpec((N_MEM, D_MODEL), lambda b, j: (b, 0))
    return pl.pallas_call(
        _attn_prompt_kernel,
        grid=(batch, nt),
        in_specs=[pl.BlockSpec((tq, D_MODEL), lambda b, j: (b * nt + j, 0)), kv, kv,
                  _full((1, D_MODEL)), _full((D_MODEL, D_MODEL)), _full((D_MODEL, D_MODEL))] + _router_specs(),
        out_specs=out_specs,
        out_shape=out_shapes,
        compiler_params=_cparams(("arbitrary", "arbitrary")),
        name="attn_prompt",
    )(x1, k, v, gxa, wq, wo, *router_w)


def _qproj_kernel(x1_ref, gxa_ref, wq_ref, q_ref):
    q_ref[...] = _bdot(_rms(x1_ref[...], gxa_ref[...]), wq_ref[...]) * (XA_HEAD_DIM ** -0.5)


def _attn_sample_core_kernel(q_ref, k_ref, v_ref, o_ref):
    n = N_MEM * SUBLANES
    col_head = lax.broadcasted_iota(I32, (SUBLANES, n), 1) & (SUBLANES - 1)
    same_head = col_head == lax.broadcasted_iota(I32, (SUBLANES, n), 0)
    kv_pad = jnp.zeros((N_MEM, SUBLANES - XA_HEADS, XA_HEAD_DIM), F32)
    q_pad = jnp.zeros((SUBLANES - XA_HEADS, XA_HEAD_DIM), F32)
    for g in range(q_ref.shape[0]):
        k8 = jnp.concatenate([k_ref[g], kv_pad], axis=1).reshape(n, XA_HEAD_DIM).astype(BF16)
        v8 = jnp.concatenate([v_ref[g], kv_pad], axis=1).reshape(n, XA_HEAD_DIM).astype(BF16)
        q8 = jnp.concatenate([q_ref[g], q_pad], axis=0).astype(BF16)
        s = lax.dot_general(q8, k8, (((1,), (1,)), ((), ())), preferred_element_type=F32)
        m = jnp.max(jnp.where(same_head, s, -jnp.inf), axis=-1, keepdims=True)
        p = jnp.where(same_head, jnp.exp(s - m), 0.0)
        den = jnp.sum(p, axis=-1, keepdims=True)
        o8 = jnp.dot(p.astype(BF16), v8, preferred_element_type=F32)
        o_ref[g] = (o8 / den)[0:XA_HEADS, :]


def _oproj_router_kernel(x1_ref, o_ref, wo_ref, gmoe_ref, wrt_ref, brt_ref,
                         x2_ref, xn_ref, tope_ref, gates_ref):
    x2 = x1_ref[...] + _bdot(o_ref[...], wo_ref[...])
    x2_ref[...] = x2
    _router(x2, gmoe_ref, wrt_ref, brt_ref, xn_ref, tope_ref, gates_ref)


def _attn_sample(x1, k, v, gxa, wq, wo, router_w):
    n = x1.shape[0]
    g = SAMPLE_GROUP
    q = pl.pallas_call(
        _qproj_kernel,
        grid=(1,),
        in_specs=[_full((n, D_MODEL)), _full((1, D_MODEL)), _full((D_MODEL, D_MODEL))],
        out_specs=_full((n, D_MODEL)),
        out_shape=jax.ShapeDtypeStruct((n, D_MODEL), F32),
        compiler_params=_cparams(("arbitrary",)),
        name="qproj_sample",
    )(x1, gxa, wq)
    kv = pl.BlockSpec((g, N_MEM, XA_HEADS, XA_HEAD_DIM), lambda i: (i, 0, 0, 0))
    row = pl.BlockSpec((g, XA_HEADS, XA_HEAD_DIM), lambda i: (i, 0, 0))
    o = pl.pallas_call(
        _attn_sample_core_kernel,
        grid=(n // g,),
        in_specs=[row, kv, kv],
        out_specs=row,
        out_shape=jax.ShapeDtypeStruct((n, XA_HEADS, XA_HEAD_DIM), F32),
        compiler_params=_cparams(("arbitrary",)),
        name="attn_sample_core",
    )(q.reshape(n, XA_HEADS, XA_HEAD_DIM), k, v).reshape(n, D_MODEL)
    out_specs, out_shapes = _router_out(n, n, lambda i: 0)
    return pl.pallas_call(
        _oproj_router_kernel,
        grid=(1,),
        in_specs=[_full((n, D_MODEL)), _full((n, D_MODEL)), _full((D_MODEL, D_MODEL))] + _router_specs(),
        out_specs=out_specs,
        out_shape=out_shapes,
        compiler_params=_cparams(("arbitrary",)),
        name="oproj_router_sample",
    )(x1, o, wo, *router_w)


def _moe_num_blocks(n_assign):
    return pl.cdiv(n_assign, MOE_BLOCK) + N_EXPERTS


def _slots_kernel(tp_ref, ts_ref, destp_ref, dests_ref, blke_ref, nact_ref, cnt_ref, pstart_ref,
                  rankp, ranks):
    c = LANES
    r_i = lax.broadcasted_iota(I32, (c, c), 0)
    c_i = lax.broadcasted_iota(I32, (c, c), 1)
    upper = (r_i < c_i).astype(BF16)
    e_iota = lax.broadcasted_iota(I32, (N_EXPERTS, c), 0)

    def make_body(top_ref, rank_ref):
        def body(ci, carry):
            lo = pl.multiple_of(ci * c, c)
            top = top_ref[:, pl.ds(lo, c)]
            hit = [e_iota == top[k:k + 1, :] for k in range(TOP_K)]
            cnt = sum(m.astype(F32) for m in hit)
            base = carry + jnp.dot(cnt.astype(BF16), upper, preferred_element_type=F32)
            rank_ref[:, pl.ds(lo, c)] = jnp.concatenate(
                [jnp.sum(jnp.where(m, base, 0.0), axis=0, keepdims=True) for m in hit], axis=0)
            return carry + jnp.sum(cnt, axis=1, keepdims=True)
        return body

    counts = jnp.zeros((N_EXPERTS, 1), F32)
    counts = lax.fori_loop(0, tp_ref.shape[1] // c, make_body(tp_ref, rankp), counts)
    counts = lax.fori_loop(0, ts_ref.shape[1] // c, make_body(ts_ref, ranks), counts)

    padded = jnp.floor((counts + (MOE_BLOCK - 1)) / MOE_BLOCK) * MOE_BLOCK
    er = lax.broadcasted_iota(I32, (N_EXPERTS, N_EXPERTS), 0)
    ec = lax.broadcasted_iota(I32, (N_EXPERTS, N_EXPERTS), 1)
    padded_row = jnp.sum(jnp.where(er == ec, padded, 0.0), axis=0, keepdims=True)
    cum = jnp.sum(jnp.where(ec <= er, padded_row, 0.0), axis=1, keepdims=True)
    pstart = cum - padded
    cnt_ref[...] = jnp.sum(jnp.where(er == ec, counts, 0.0), axis=0, keepdims=True).astype(I32)
    pstart_ref[...] = jnp.sum(jnp.where(er == ec, pstart, 0.0), axis=0, keepdims=True).astype(I32)

    def dest(top_ref, rank_ref, out_ref):
        n = top_ref.shape[1]
        ei = lax.broadcasted_iota(I32, (N_EXPERTS, n), 0)
        rows = []
        for k in range(TOP_K):
            start = jnp.sum(jnp.where(ei == top_ref[k:k + 1, :], pstart, 0.0), axis=0, keepdims=True)
            rows.append(start + rank_ref[k:k + 1, :])
        out_ref[...] = jnp.concatenate(rows, axis=0).astype(I32)

    dest(tp_ref, rankp, destp_ref)
    dest(ts_ref, ranks, dests_ref)

    nb = blke_ref.shape[1]
    blk_lo = lax.broadcasted_iota(I32, (N_EXPERTS, nb), 1).astype(F32) * MOE_BLOCK
    blk = jnp.sum((cum <= blk_lo).astype(F32), axis=0, keepdims=True)
    blke_ref[...] = jnp.minimum(blk, N_EXPERTS - 1).astype(I32)
    nact_ref[...] = (cum[N_EXPERTS - 1:N_EXPERTS, :] / MOE_BLOCK).astype(I32)


def _slots(top_p, top_s):
    n_p, n_s = top_p.shape[1], top_s.shape[1]
    nb = _moe_num_blocks((n_p + n_s) * TOP_K)
    return pl.pallas_call(
        _slots_kernel,
        grid=(1,),
        in_specs=[_full((TOP_K, n_p)), _full((TOP_K, n_s))],
        out_specs=[_full((TOP_K, n_p)), _full((TOP_K, n_s)), _full((1, nb)), _full((1, 1)),
                   _full((1, N_EXPERTS)), _full((1, N_EXPERTS))],
        out_shape=[
            jax.ShapeDtypeStruct((TOP_K, n_p), I32),
            jax.ShapeDtypeStruct((TOP_K, n_s), I32),
            jax.ShapeDtypeStruct((1, nb), I32),
            jax.ShapeDtypeStruct((1, 1), I32),
            jax.ShapeDtypeStruct((1, N_EXPERTS), I32),
            jax.ShapeDtypeStruct((1, N_EXPERTS), I32),
        ],
        scratch_shapes=[pltpu.VMEM((TOP_K, n_p), F32), pltpu.VMEM((TOP_K, n_s), F32)],
        compiler_params=_cparams(("arbitrary",)),
        name="moe_slots",
    )(top_p, top_s)


def _for_rows(n_rows, fn):
    def body(i, c):
        for u in range(DMA_UNROLL):
            fn(i * DMA_UNROLL + u)
        return c
    lax.fori_loop(0, n_rows // DMA_UNROLL, body, 0)


def _dispatch_kernel(cnt_ref, pstart_ref, nact_ref, destp_ref, dests_ref, xnp_ref, xns_ref, xnp_hbm, xs_ref,
                     zeros, sem, zsem):
    i = pl.program_id(0)
    last = pl.num_programs(0) - 1

    def scatter(x_ref, dest_ref, hbm_ref=None, first_row=0):
        n = x_ref.shape[0] // ROW_TILE

        def one(t):
            for k in range(TOP_K):
                if hbm_ref is not None and k >= TOP_K // 2:
                    src, row = hbm_ref, first_row + t
                else:
                    src, row = x_ref, t
                _row_tile_copy(src, row, xs_ref, dest_ref[t * TOP_K + k], sem).start(priority=k % 2)

        _for_rows(n, one)
        _row_tile_copy(xs_ref, 0, xs_ref, 0, sem, n * TOP_K).wait()

    @pl.when(i < last)
    def _():
        scatter(xnp_ref, destp_ref, xnp_hbm, i * (xnp_ref.shape[0] // ROW_TILE))

    @pl.when(i == last)
    def _():
        scatter(xns_ref, dests_ref)
        zeros[...] = jnp.zeros(zeros.shape, F32)

        def pad_copies(e, act):
            c = cnt_ref[e]
            npad = (-c) & (MOE_BLOCK - 1)
            base = pstart_ref[e] + c
            for bit in range(MOE_BLOCK.bit_length() - 1):
                size = 1 << bit

                @pl.when((npad & size) != 0)
                def _():
                    act(_row_tile_copy(zeros, 0, xs_ref, base + (npad & (size - 1)), zsem, size))

        def tail_copies(act):
            def body(j, c):
                act(_row_tile_copy(zeros, 0, xs_ref, j * MOE_BLOCK, zsem, MOE_BLOCK))
                return c
            lax.fori_loop(nact_ref[0], xs_ref.shape[0] // (MOE_BLOCK * ROW_TILE), body, 0)

        for act in (lambda d: d.start(), lambda d: d.wait()):
            lax.fori_loop(0, N_EXPERTS, lambda e, c, act=act: (pad_copies(e, act), c)[1], 0)
            tail_copies(act)


def _dispatch(cnt, pstart, nact, dest_p, dest_s, xn_p, xn_s, n_rows):
    n_p, n_s = xn_p.shape[0] // ROW_TILE, xn_s.shape[0] // ROW_TILE
    tile = DISPATCH_TILE
    nt = n_p // tile
    smem = pl.BlockSpec(memory_space=pltpu.SMEM)
    return pl.pallas_call(
        _dispatch_kernel,
        grid=(nt + 1,),
        in_specs=[
            smem, smem, smem,
            pl.BlockSpec((TOP_K * tile,), lambda i: (jnp.minimum(i, nt - 1),), memory_space=pltpu.SMEM),
            smem,
            pl.BlockSpec((tile * ROW_TILE, LANES), lambda i: (jnp.minimum(i, nt - 1), 0)),
            _full((n_s * ROW_TILE, LANES)),
            pl.BlockSpec(memory_space=pl.ANY),
        ],
        out_specs=pl.BlockSpec(memory_space=pl.ANY),
        out_shape=jax.ShapeDtypeStruct((n_rows * ROW_TILE, LANES), F32),
        scratch_shapes=[pltpu.VMEM((MOE_BLOCK * ROW_TILE, LANES), F32), pltpu.SemaphoreType.DMA,
                        pltpu.SemaphoreType.DMA],
        compiler_params=_cparams(("arbitrary",)),
        name="moe_dispatch",
    )(cnt, pstart, nact, dest_p, dest_s, xn_p, xn_s, xn_p)


def _combine_kernel(dest_ref, dest_next_ref, gates_ref, x2_ref, gfin_ref, ys_ref, y_ref, bufs, sems):
    tile = x2_ref.shape[0]
    i = pl.program_id(0)
    slot = i % 2

    def gather(d_ref, b):
        def one(t):
            for k in range(TOP_K):
                _row_tile_copy(ys_ref, d_ref[t * TOP_K + k], bufs.at[b, k], t, sems.at[b]).start(priority=k % 2)
        _for_rows(tile, one)

    @pl.when(i == 0)
    def _():
        gather(dest_ref, 0)

    for b in range(2):
        @pl.when(slot == b)
        def _():
            pltpu.make_async_copy(bufs.at[b], bufs.at[b], sems.at[b]).wait()

            @pl.when(i + 1 < pl.num_programs(0))
            def _():
                gather(dest_next_ref, 1 - b)

    buf = bufs.at[slot]
    sumsq = jnp.zeros((tile, 1), F32)
    gate = [jnp.broadcast_to(gates_ref[:, k:k + 1], (tile, LANES)) for k in range(TOP_K)]
    for s in range(ROW_TILE):
        cols = slice(s * LANES, (s + 1) * LANES)
        acc = x2_ref[:, cols]
        for k in range(TOP_K):
            acc = acc + gate[k] * buf[k, pl.ds(s, tile, stride=ROW_TILE), :]
        y_ref[:, cols] = acc
        sumsq = sumsq + jnp.sum(acc * acc, axis=-1, keepdims=True)
    scale = lax.rsqrt(sumsq / D_MODEL + EPS)
    for s in range(ROW_TILE):
        cols = slice(s * LANES, (s + 1) * LANES)
        y_ref[:, cols] = y_ref[:, cols] * scale * gfin_ref[:, cols]


def _combine(dest, gates_t, x2, g_final, ys):
    n = x2.shape[0]
    tile = min(COMBINE_TILE, n)
    row = pl.BlockSpec((tile, D_MODEL), lambda i: (i, 0))
    nt = n // tile
    return pl.pallas_call(
        _combine_kernel,
        grid=(nt,),
        in_specs=[
            pl.BlockSpec((TOP_K * tile,), lambda i: (i,), memory_space=pltpu.SMEM),
            pl.BlockSpec((TOP_K * tile,), lambda i: (jnp.minimum(i + 1, nt - 1),), memory_space=pltpu.SMEM),
            pl.BlockSpec((tile, TOP_K), lambda i: (i, 0)),
            row,
            _full((1, D_MODEL)),
            pl.BlockSpec(memory_space=pl.ANY),
        ],
        out_specs=row,
        out_shape=jax.ShapeDtypeStruct((n, D_MODEL), F32),
        scratch_shapes=[pltpu.VMEM((2, TOP_K, tile * ROW_TILE, LANES), F32), pltpu.SemaphoreType.DMA((2,))],
        compiler_params=_cparams(("arbitrary",)),
        name="moe_combine",
    )(dest, dest, gates_t, x2, g_final, ys)


def _experts_kernel(blke_ref, nact_ref, xs_ref, w1_hbm, b1a_ref, b1b_ref, w2_hbm, b2a_ref, b2b_ref, ys_ref,
                    w1f, w2f, w1b, w2b, wsem, started):
    j = pl.program_id(0)
    nact = nact_ref[0]
    last = blke_ref.shape[0] - 1
    first, second = 2 * j, 2 * j + 1
    e0, e1 = blke_ref[first], blke_ref[second]
    active0, active1 = first < nact, second < nact
    together = jnp.logical_and(active1, e1 == e0)
    half = MOE_BLOCK * ROW_TILE

    def fetch(expert, slot):
        return (pltpu.make_async_copy(w1_hbm.at[expert], w1f.at[slot], wsem.at[slot]),
                pltpu.make_async_copy(w2_hbm.at[expert], w2f.at[slot], wsem.at[slot]))

    def switch_to(blk):
        e = blke_ref[blk]
        n = started[0]
        jn = lax.while_loop(lambda jj: jnp.logical_and(jj < nact, blke_ref[jnp.minimum(jj, last)] == e),
                            lambda jj: jj + 1, blk + 1)
        for b in range(2):
            @pl.when(n % 2 == b)
            def _():
                for d in fetch(e, b):
                    d.wait()

                @pl.when(jn < nact)
                def _():
                    for d in fetch(blke_ref[jnp.minimum(jn, last)], 1 - b):
                        d.start(priority=1)

                w1b[...] = w1f[b].astype(BF16)
                w2b[...] = w2f[b].astype(BF16)
        started[0] = n + 1

    def ffn(x_ref, n_rows, b1_ref, b2_ref, y_ref):
        x = _load_row_tiles(x_ref, n_rows).astype(BF16)
        gu = jnp.dot(x, w1b[...], preferred_element_type=F32) + b1_ref[0]
        g = jnp.minimum(gu[:, :D_EXPERT], SWIGLU_LIMIT)
        u = jnp.clip(gu[:, D_EXPERT:], -SWIGLU_LIMIT, SWIGLU_LIMIT)
        h = (u + 1.0) * (g * jax.nn.sigmoid(SWIGLU_ALPHA * g))
        _store_row_tiles(y_ref, _bdot(h, w2b[...]) + b2_ref[0])

    @pl.when(j == 0)
    def _():
        started[0] = 0
        for d in fetch(e0, 0):
            d.start()

    @pl.when(jnp.logical_and(active0, jnp.logical_or(j == 0, e0 != blke_ref[jnp.maximum(first - 1, 0)])))
    def _():
        switch_to(first)

    @pl.when(together)
    def _():
        ffn(xs_ref, 2 * MOE_BLOCK, b1a_ref, b2a_ref, ys_ref)

    @pl.when(jnp.logical_and(active0, jnp.logical_not(together)))
    def _():
        ffn(xs_ref.at[pl.ds(0, half)], MOE_BLOCK, b1a_ref, b2a_ref, ys_ref.at[pl.ds(0, half)])

        @pl.when(active1)
        def _():
            switch_to(second)
            ffn(xs_ref.at[pl.ds(half, half)], MOE_BLOCK, b1b_ref, b2b_ref, ys_ref.at[pl.ds(half, half)])

        @pl.when(jnp.logical_not(active1))
        def _():
            ys_ref[pl.ds(half, half), :] = jnp.zeros((half, LANES), F32)

    @pl.when(jnp.logical_not(active0))
    def _():
        ys_ref[...] = jnp.zeros(ys_ref.shape, F32)


def _experts(blk_e, nact, xs, w1, b1, w2, b2):
    nb = xs.shape[0] // (MOE_BLOCK * ROW_TILE)
    assert nb % 2 == 0
    blk = lambda j, be, na: (jnp.minimum(j, (na[0] - 1) // 2), 0)
    exp = lambda h: (lambda j, be, na: (be[jnp.minimum(2 * j + h, na[0] - 1)], 0, 0))
    rows = (2 * MOE_BLOCK * ROW_TILE, LANES)
    return pl.pallas_call(
        _experts_kernel,
        grid_spec=pltpu.PrefetchScalarGridSpec(
            num_scalar_prefetch=2,
            grid=(nb // 2,),
            in_specs=[
                pl.BlockSpec(rows, blk),
                pl.BlockSpec(memory_space=pl.ANY),
                pl.BlockSpec((1, 1, 2 * D_EXPERT), exp(0)),
                pl.BlockSpec((1, 1, 2 * D_EXPERT), exp(1)),
                pl.BlockSpec(memory_space=pl.ANY),
                pl.BlockSpec((1, 1, D_MODEL), exp(0)),
                pl.BlockSpec((1, 1, D_MODEL), exp(1)),
            ],
            out_specs=pl.BlockSpec(rows, lambda j, be, na: (j, 0)),
            scratch_shapes=[
                pltpu.VMEM((2, D_MODEL, 2 * D_EXPERT), F32), pltpu.VMEM((2, D_EXPERT, D_MODEL), F32),
                pltpu.VMEM((D_MODEL, 2 * D_EXPERT), BF16), pltpu.VMEM((D_EXPERT, D_MODEL), BF16),
                pltpu.SemaphoreType.DMA((2,)), pltpu.SMEM((1,), I32),
            ],
        ),
        out_shape=jax.ShapeDtypeStruct(xs.shape, F32),
        compiler_params=_cparams(("arbitrary",)),
        name="moe_experts",
    )(blk_e, nact, xs, w1, b1, b1, w2, b2, b2)


def _block_diag(w):
    h, d, _ = w.shape
    eye = jnp.eye(h, dtype=w.dtype)
    return (eye[:, None, :, None] * w[:, :, None, :]).reshape(h * d, h * d)


def kernel(x_prompt, x_sample, state_lru_h, cache_lru_conv, cache_cf_conv, cache_mem_k, cache_mem_v,
           mem_prompt, g_mix, w_in, w_lru_conv, b_lru_conv, w_lru_a, b_lru_a, w_lru_x, b_lru_x,
           lru_lambda, w_cf_conv, b_cf_conv, g_cf_ln, b_cf_ln, w_out, g_xa, g_mem, w_q, w_k, w_v, w_o,
           g_moe, w_router, b_router, w_e1, b_e1, w_e2, b_e2, g_final):
    depth = g_mix.shape[0]
    assert depth == 1, "single-layer stack"
    batch, seq, _ = x_prompt.shape
    n_s = x_sample.shape[0]
    assert x_sample.shape[1] == 1
    n_p = batch * seq
    l = 0

    mix_w = (
        g_mix[l][None], w_in[l].astype(BF16), w_lru_conv[l], b_lru_conv[l][None],
        jnp.concatenate([_block_diag(w_lru_a[l]), _block_diag(w_lru_x[l])], axis=1).astype(BF16),
        jnp.concatenate([b_lru_a[l], b_lru_x[l]])[None], lru_lambda[l][None],
        w_cf_conv[l], b_cf_conv[l][None], g_cf_ln[l][None], b_cf_ln[l][None], w_out[l].astype(BF16),
    )
    router_w = (g_moe[l][None], w_router[l].T.astype(BF16), b_router[l][:, None])
    wq, wo = w_q[l].astype(BF16), w_o[l].astype(BF16)
    gxa = g_xa[l][None]

    xp = x_prompt.reshape(n_p, D_MODEL)
    xs_ = x_sample.reshape(n_s, D_MODEL)
    x1p, hp, lcp, ccp = _mixer_prompt(xp, mix_w, batch, seq)
    x1s, hs, lru_conv_s, cf_conv_s = _mixer_sample(xs_, mix_w, state_lru_h[l], cache_lru_conv[l], cache_cf_conv[l])

    mk, mv, mkb, mvb = _memkv(mem_prompt.reshape(batch * N_MEM, D_MODEL), g_mem[l][None],
                              w_k[l].astype(BF16), w_v[l].astype(BF16))
    x2p, xnp_, top_p, gates_p = _attn_prompt(x1p, mkb, mvb, gxa, wq, wo, router_w, batch, seq)
    x2s, xns, top_s, gates_s = _attn_sample(x1s, cache_mem_k[l], cache_mem_v[l], gxa, wq, wo, router_w)

    dest_p, dest_s, blk_e, nact, cnt, pstart = _slots(top_p, top_s)
    dest_p, dest_s = dest_p.T.reshape(-1), dest_s.T.reshape(-1)
    nb = _moe_num_blocks((n_p + n_s) * TOP_K)
    nact = nact.reshape(1)
    xs = _dispatch(cnt.reshape(N_EXPERTS), pstart.reshape(N_EXPERTS), nact, dest_p, dest_s, xnp_, xns,
                   nb * MOE_BLOCK)
    ys = _experts(blk_e.reshape(nb), nact, xs, w_e1[l], b_e1[l][:, None, :], w_e2[l], b_e2[l][:, None, :])
    gfin = g_final[None]
    y_p = _combine(dest_p, gates_p.T, x2p, gfin, ys)
    y_s = _combine(dest_s, gates_s.T, x2s, gfin, ys)

    return (
        y_p.reshape(batch, seq, D_MODEL),
        y_s.reshape(n_s, 1, D_MODEL),
        hp.reshape(depth, batch, LRU_WIDTH),
        lcp[None],
        ccp[None],
        mk[None],
        mv[None],
        hs[None],
        lru_conv_s[None],
        cf_conv_s[None],
    )
```

```python
import jax
import jax.numpy as jnp
from jax import lax
from jax.experimental import pallas as pl
from jax.experimental.pallas import tpu as pltpu

F32 = jnp.float32
BF16 = jnp.bfloat16
I32 = jnp.int32

D_MODEL = 1024
LRU_WIDTH = 512
CF_WIDTH = 512
LRU_CONV = 4
LRU_C = 8.0
CF_CONV = 31
IN_WIDTH = 2 * LRU_WIDTH + 2 * CF_WIDTH
N_MEM = 256
XA_HEADS = 4
XA_HEAD_DIM = D_MODEL // XA_HEADS
N_EXPERTS = 32
TOP_K = 4
D_EXPERT = D_MODEL
SWIGLU_LIMIT = 7.0
SWIGLU_ALPHA = 1.702
EPS = 1e-6

LANES = 128
SUBLANES = 8
ROW_TILE = D_MODEL // LANES
assert ROW_TILE == SUBLANES
VMEM_LIMIT = 56 * 1024 * 1024

MIX_TILE = 1024
LRU_TAIL_ROWS = 8
CF_TAIL_ROWS = 32
ATT_TILE = 1024
SAMPLE_GROUP = 4
SAMPLE_MIX_GROUP = 32
MOE_BLOCK = 256
SLOT_CHUNK = 256
DMA_UNROLL = 16
DISPATCH_TILE = 4096
COMBINE_TILE = 256


def _cparams(sem):
    return pltpu.CompilerParams(dimension_semantics=sem, vmem_limit_bytes=VMEM_LIMIT)


def _full(shape):
    n = len(shape)
    return pl.BlockSpec(shape, lambda *_: (0,) * n)


def _rms(x, g):
    return x * lax.rsqrt(jnp.mean(x * x, axis=-1, keepdims=True) + EPS) * g


def _bdot(a, b):
    return jnp.dot(a.astype(BF16), b, preferred_element_type=F32)


def _store_row_tiles(ref, x):
    m = x.shape[0]
    for s in range(ROW_TILE):
        ref[pl.ds(s, m, stride=ROW_TILE), :] = x[:, s * LANES:(s + 1) * LANES]


def _load_row_tiles(ref, m):
    return jnp.concatenate([ref[pl.ds(s, m, stride=ROW_TILE), :] for s in range(ROW_TILE)], axis=-1)


def _row_tile_copy(src_ref, src_row, dst_ref, dst_row, sem, n_rows=1):
    src = src_ref.at[pl.ds(pl.multiple_of(src_row * ROW_TILE, ROW_TILE), n_rows * ROW_TILE)]
    dst = dst_ref.at[pl.ds(pl.multiple_of(dst_row * ROW_TILE, ROW_TILE), n_rows * ROW_TILE)]
    return pltpu.make_async_copy(src, dst, sem)


def _lru_coeffs(xc, wg_ref, bg_ref, lam_ref):
    gl = _bdot(xc, wg_ref[...]) + bg_ref[...]
    r = jax.nn.sigmoid(gl[:, :LRU_WIDTH])
    i = jax.nn.sigmoid(gl[:, LRU_WIDTH:])
    log_a = -LRU_C * r * jax.nn.softplus(-lam_ref[...])
    a = jnp.exp(log_a)
    u = jnp.sqrt(-jnp.tanh(log_a) * (1.0 + a * a)) * (i * xc)
    return a, u


def _cf_post(gc, gln_ref, bln_ref):
    mu = jnp.mean(gc, axis=-1, keepdims=True)
    c = gc - mu
    y = c * lax.rsqrt(jnp.mean(c * c, axis=-1, keepdims=True) + EPS)
    return jax.nn.silu(y * gln_ref[...] + bln_ref[...])


def _mixer_out(x, y_lru, y_cf, wout_ref):
    y = _bdot(y_lru, wout_ref[:LRU_WIDTH, :]) + _bdot(y_cf, wout_ref[LRU_WIDTH:, :])
    return x + y


def _mixer_prompt_kernel(x_ref, gmix_ref, win_ref, wlc_ref, blc_ref, wg_ref, bg_ref, lam_ref,
                         wcc_ref, bcc_ref, gln_ref, bln_ref, wout_ref,
                         x1_ref, h_ref, ltail_ref, ctail_ref,
                         zx_ext, glu_ext, h_carry):
    tt = MIX_TILE
    j = pl.program_id(1)

    @pl.when(j == 0)
    def _():
        zx_ext[0:LRU_TAIL_ROWS, :] = jnp.zeros((LRU_TAIL_ROWS, LRU_WIDTH), F32)
        glu_ext[0:CF_TAIL_ROWS, :] = jnp.zeros((CF_TAIL_ROWS, CF_WIDTH), F32)
        h_carry[...] = jnp.zeros((1, LRU_WIDTH), F32)

    x = x_ref[...]
    z = _bdot(_rms(x, gmix_ref[...]), win_ref[...])
    zx = z[:, :LRU_WIDTH]
    zg = z[:, LRU_WIDTH:2 * LRU_WIDTH]
    za = z[:, 2 * LRU_WIDTH:2 * LRU_WIDTH + CF_WIDTH]
    zb = z[:, 2 * LRU_WIDTH + CF_WIDTH:]

    zx_ext[LRU_TAIL_ROWS:LRU_TAIL_ROWS + tt, :] = zx
    off = LRU_TAIL_ROWS - (LRU_CONV - 1)
    xc = blc_ref[...]
    for k in range(LRU_CONV):
        xc = xc + wlc_ref[k:k + 1, :] * zx_ext[off + k:off + k + tt, :]
    a, u = _lru_coeffs(xc, wg_ref, bg_ref, lam_ref)
    sub = lax.broadcasted_iota(I32, (tt, LRU_WIDTH), 0) & (SUBLANES - 1)
    d = 1
    while d < SUBLANES:
        keep = sub >= d
        a_sh = jnp.where(keep, pltpu.roll(a, d, 0), 1.0)
        u_sh = jnp.where(keep, pltpu.roll(u, d, 0), 0.0)
        u = u + a * u_sh
        a = a * a_sh
        d *= 2
    h_last = h_carry[...]
    groups = []
    for r in range(0, tt, SUBLANES):
        hg = a[r:r + SUBLANES, :] * h_last + u[r:r + SUBLANES, :]
        groups.append(hg)
        h_last = hg[SUBLANES - 1:SUBLANES, :]
    hs = jnp.concatenate(groups, axis=0)
    h_carry[...] = h_last
    y_lru = hs * jax.nn.gelu(zg)

    glu = za * jax.nn.sigmoid(zb)
    glu_ext[CF_TAIL_ROWS:CF_TAIL_ROWS + tt, :] = glu
    off = CF_TAIL_ROWS - (CF_CONV - 1)
    gc = bcc_ref[...]
    for s in range(SUBLANES):
        rows = tt if s == 0 else tt + SUBLANES
        q = None
        for k in range(CF_CONV):
            if (off + k) % SUBLANES == s:
                base = off + k - s
                term = wcc_ref[k:k + 1, :] * glu_ext[base:base + rows, :]
                q = term if q is None else q + term
        gc = gc + (q if s == 0 else q[s:s + tt, :])
    y_cf = _cf_post(gc, gln_ref, bln_ref)

    x1_ref[...] = _mixer_out(x, y_lru, y_cf, wout_ref)

    h_ref[0] = h_last
    ltail_ref[0] = zx_ext[LRU_TAIL_ROWS + tt - (LRU_CONV - 1):LRU_TAIL_ROWS + tt, :]
    ctail_ref[0] = glu_ext[CF_TAIL_ROWS + tt - (CF_CONV - 1):CF_TAIL_ROWS + tt, :]
    zx_ext[0:LRU_TAIL_ROWS, :] = zx_ext[tt:tt + LRU_TAIL_ROWS, :]
    glu_ext[0:CF_TAIL_ROWS, :] = glu_ext[tt:tt + CF_TAIL_ROWS, :]


def _mixer_weight_specs():
    return [
        _full((1, D_MODEL)),
        _full((D_MODEL, IN_WIDTH)),
        _full((LRU_CONV, LRU_WIDTH)), _full((1, LRU_WIDTH)),
        _full((LRU_WIDTH, 2 * LRU_WIDTH)), _full((1, 2 * LRU_WIDTH)),
        _full((1, LRU_WIDTH)),
        _full((CF_CONV, CF_WIDTH)), _full((1, CF_WIDTH)),
        _full((1, CF_WIDTH)), _full((1, CF_WIDTH)),
        _full((D_MODEL, D_MODEL)),
    ]


def _mixer_prompt(x, mix_w, batch, seq):
    tt = MIX_TILE
    nt = seq // tt
    return pl.pallas_call(
        _mixer_prompt_kernel,
        grid=(batch, nt),
        in_specs=[pl.BlockSpec((tt, D_MODEL), lambda b, j: (b * nt + j, 0))] + _mixer_weight_specs(),
        out_specs=[
            pl.BlockSpec((tt, D_MODEL), lambda b, j: (b * nt + j, 0)),
            pl.BlockSpec((1, 1, LRU_WIDTH), lambda b, j: (b, 0, 0)),
            pl.BlockSpec((1, LRU_CONV - 1, LRU_WIDTH), lambda b, j: (b, 0, 0)),
            pl.BlockSpec((1, CF_CONV - 1, CF_WIDTH), lambda b, j: (b, 0, 0)),
        ],
        out_shape=[
            jax.ShapeDtypeStruct((batch * seq, D_MODEL), F32),
            jax.ShapeDtypeStruct((batch, 1, LRU_WIDTH), F32),
            jax.ShapeDtypeStruct((batch, LRU_CONV - 1, LRU_WIDTH), F32),
            jax.ShapeDtypeStruct((batch, CF_CONV - 1, CF_WIDTH), F32),
        ],
        scratch_shapes=[
            pltpu.VMEM((LRU_TAIL_ROWS + tt, LRU_WIDTH), F32),
            pltpu.VMEM((CF_TAIL_ROWS + tt, CF_WIDTH), F32),
            pltpu.VMEM((1, LRU_WIDTH), F32),
        ],
        compiler_params=_cparams(("arbitrary", "arbitrary")),
        name="mixer_prompt",
    )(x, *mix_w)


def _mixer_sample_kernel(x_ref, gmix_ref, win_ref, wlc_ref, blc_ref, wg_ref, bg_ref, lam_ref,
                         wcc_ref, bcc_ref, gln_ref, bln_ref, wout_ref,
                         h0_ref, lbuf_ref, cbuf_ref,
                         x1_ref, h_ref, ltail_ref, ctail_ref):
    x = x_ref[...]
    z = _bdot(_rms(x, gmix_ref[...]), win_ref[...])
    zx = z[:, :LRU_WIDTH]
    zg = z[:, LRU_WIDTH:2 * LRU_WIDTH]
    za = z[:, 2 * LRU_WIDTH:2 * LRU_WIDTH + CF_WIDTH]
    zb = z[:, 2 * LRU_WIDTH + CF_WIDTH:]

    xc = blc_ref[...] + wlc_ref[LRU_CONV - 1:LRU_CONV, :] * zx
    xc = xc + jnp.sum(lbuf_ref[...] * wlc_ref[0:LRU_CONV - 1, :][None], axis=1)
    a, u = _lru_coeffs(xc, wg_ref, bg_ref, lam_ref)
    h = a * h0_ref[...] + u
    y_lru = h * jax.nn.gelu(zg)

    glu = za * jax.nn.sigmoid(zb)
    gc = bcc_ref[...] + wcc_ref[CF_CONV - 1:CF_CONV, :] * glu
    gc = gc + jnp.sum(cbuf_ref[...] * wcc_ref[0:CF_CONV - 1, :][None], axis=1)
    y_cf = _cf_post(gc, gln_ref, bln_ref)

    x1_ref[...] = _mixer_out(x, y_lru, y_cf, wout_ref)
    h_ref[...] = h
    ltail_ref[:, 0:LRU_CONV - 2, :] = lbuf_ref[:, 1:LRU_CONV - 1, :]
    ctail_ref[:, 0:CF_CONV - 2, :] = cbuf_ref[:, 1:CF_CONV - 1, :]
    for i in range(x.shape[0]):
        ltail_ref[i, LRU_CONV - 2:LRU_CONV - 1, :] = zx[i:i + 1, :]
        ctail_ref[i, CF_CONV - 2:CF_CONV - 1, :] = glu[i:i + 1, :]


def _mixer_sample(x, mix_w, h0, lbuf, cbuf):
    n = x.shape[0]
    g = SAMPLE_MIX_GROUP
    row = lambda w: pl.BlockSpec((g, w), lambda i: (i, 0))
    return pl.pallas_call(
        _mixer_sample_kernel,
        grid=(n // g,),
        in_specs=[row(D_MODEL)] + _mixer_weight_specs() + [
            row(LRU_WIDTH),
            pl.BlockSpec((g, LRU_CONV - 1, LRU_WIDTH), lambda i: (i, 0, 0)),
            pl.BlockSpec((g, CF_CONV - 1, CF_WIDTH), lambda i: (i, 0, 0)),
        ],
        out_specs=[row(D_MODEL), row(LRU_WIDTH),
                   pl.BlockSpec((g, LRU_CONV - 1, LRU_WIDTH), lambda i: (i, 0, 0)),
                   pl.BlockSpec((g, CF_CONV - 1, CF_WIDTH), lambda i: (i, 0, 0))],
        out_shape=[
            jax.ShapeDtypeStruct((n, D_MODEL), F32),
            jax.ShapeDtypeStruct((n, LRU_WIDTH), F32),
            jax.ShapeDtypeStruct((n, LRU_CONV - 1, LRU_WIDTH), F32),
            jax.ShapeDtypeStruct((n, CF_CONV - 1, CF_WIDTH), F32),
        ],
        compiler_params=_cparams(("arbitrary",)),
        name="mixer_sample",
    )(x, *mix_w, h0, lbuf, cbuf)


def _memkv_kernel(mem_ref, gmem_ref, wk_ref, wv_ref, k_ref, v_ref, kb_ref, vb_ref):
    mn = _rms(mem_ref[...], gmem_ref[...]).astype(BF16)
    k = jnp.dot(mn, wk_ref[...], preferred_element_type=F32)
    v = jnp.dot(mn, wv_ref[...], preferred_element_type=F32)
    for h in range(XA_HEADS):
        sl = slice(h * XA_HEAD_DIM, (h + 1) * XA_HEAD_DIM)
        k_ref[0, :, h, :] = k[:, sl]
        v_ref[0, :, h, :] = v[:, sl]
    kb_ref[...] = k.astype(BF16)
    vb_ref[...] = v.astype(BF16)


def _memkv(mem, g_mem, wk, wv):
    n = mem.shape[0]
    t = N_MEM
    row = pl.BlockSpec((t, D_MODEL), lambda i: (i, 0))
    state = pl.BlockSpec((1, N_MEM, XA_HEADS, XA_HEAD_DIM), lambda i: (i, 0, 0, 0))
    state_shape = jax.ShapeDtypeStruct((n // t, N_MEM, XA_HEADS, XA_HEAD_DIM), F32)
    return pl.pallas_call(
        _memkv_kernel,
        grid=(n // t,),
        in_specs=[row, _full((1, D_MODEL)), _full((D_MODEL, D_MODEL)), _full((D_MODEL, D_MODEL))],
        out_specs=[state, state, row, row],
        out_shape=[state_shape] * 2 + [jax.ShapeDtypeStruct((n, D_MODEL), BF16)] * 2,
        compiler_params=_cparams(("arbitrary",)),
        name="memory_kv",
    )(mem, g_mem, wk, wv)


def _router(x2, gmoe_ref, wrt_ref, brt_ref, xn_ref, tope_ref, gates_ref):
    xn = _rms(x2, gmoe_ref[...])
    _store_row_tiles(xn_ref, xn)
    logits = lax.dot_general(wrt_ref[...], xn.astype(BF16), (((1,), (1,)), ((), ())),
                             preferred_element_type=F32) + brt_ref[...]
    e_iota = lax.broadcasted_iota(I32, logits.shape, 0)
    work = logits
    vals, idxs = [], []
    for _ in range(TOP_K):
        m = jnp.max(work, axis=0, keepdims=True)
        idx = jnp.min(jnp.where(work == m, e_iota, N_EXPERTS), axis=0, keepdims=True)
        vals.append(m)
        idxs.append(idx)
        work = jnp.where(e_iota == idx, -jnp.inf, work)
    ex = [jnp.exp(v - vals[0]) for v in vals]
    den = ex[0] + ex[1] + ex[2] + ex[3]
    tope_ref[...] = jnp.concatenate(idxs, axis=0)
    gates_ref[...] = jnp.concatenate([e / den for e in ex], axis=0)


def _router_specs():
    return [_full((1, D_MODEL)), _full((N_EXPERTS, D_MODEL)), _full((N_EXPERTS, 1))]


def _router_out(n, tile, index):
    specs = [
        pl.BlockSpec((tile, D_MODEL), lambda *a: (index(*a), 0)),
        pl.BlockSpec((tile * ROW_TILE, LANES), lambda *a: (index(*a), 0)),
        pl.BlockSpec((TOP_K, tile), lambda *a: (0, index(*a))),
        pl.BlockSpec((TOP_K, tile), lambda *a: (0, index(*a))),
    ]
    shapes = [
        jax.ShapeDtypeStruct((n, D_MODEL), F32),
        jax.ShapeDtypeStruct((n * ROW_TILE, LANES), F32),
        jax.ShapeDtypeStruct((TOP_K, n), I32),
        jax.ShapeDtypeStruct((TOP_K, n), F32),
    ]
    return specs, shapes


def _attn_prompt_kernel(x1_ref, k_ref, v_ref, gxa_ref, wq_ref, wo_ref, gmoe_ref, wrt_ref, brt_ref,
                        x2_ref, xn_ref, tope_ref, gates_ref):
    x1 = x1_ref[...]
    q = (_bdot(_rms(x1, gxa_ref[...]), wq_ref[...]) * (XA_HEAD_DIM ** -0.5)).astype(BF16)
    outs = []
    for h in range(XA_HEADS):
        sl = slice(h * XA_HEAD_DIM, (h + 1) * XA_HEAD_DIM)
        s = lax.dot_general(q[:, sl], k_ref[:, sl], (((1,), (1,)), ((), ())), preferred_element_type=F32)
        p = jnp.exp(s - jnp.max(s, axis=-1, keepdims=True))
        p = p / jnp.sum(p, axis=-1, keepdims=True)
        outs.append(_bdot(p, v_ref[:, sl]))
    o = jnp.concatenate(outs, axis=-1)
    x2 = x1 + _bdot(o, wo_ref[...])
    x2_ref[...] = x2
    _router(x2, gmoe_ref, wrt_ref, brt_ref, xn_ref, tope_ref, gates_ref)


def _attn_prompt(x1, k, v, gxa, wq, wo, router_w, batch, seq):
    tq = ATT_TILE
    nt = seq // tq
    out_specs, out_shapes = _router_out(batch * seq, tq, lambda b, j: b * nt + j)
    kv = pl.BlockSpec((N_MEM, D_MODEL), lambda b, j: (b, 0))
    return pl.pallas_call(
        _attn_prompt_kernel,
        grid=(batch, nt),
        in_specs=[pl.BlockSpec((tq, D_MODEL), lambda b, j: (b * nt + j, 0)), kv, kv,
                  _full((1, D_MODEL)), _full((D_MODEL, D_MODEL)), _full((D_MODEL, D_MODEL))] + _router_specs(),
        out_specs=out_specs,
        out_shape=out_shapes,
        compiler_params=_cparams(("arbitrary", "arbitrary")),
        name="attn_prompt",
    )(x1, k, v, gxa, wq, wo, *router_w)


def _qproj_kernel(x1_ref, gxa_ref, wq_ref, q_ref):
    q_ref[...] = _bdot(_rms(x1_ref[...], gxa_ref[...]), wq_ref[...]) * (XA_HEAD_DIM ** -0.5)


def _attn_sample_core_kernel(q_ref, k_ref, v_ref, o_ref):
    n = N_MEM * SUBLANES
    col_head = lax.broadcasted_iota(I32, (SUBLANES, n), 1) & (SUBLANES - 1)
    same_head = col_head == lax.broadcasted_iota(I32, (SUBLANES, n), 0)
    kv_pad = jnp.zeros((N_MEM, SUBLANES - XA_HEADS, XA_HEAD_DIM), F32)
    q_pad = jnp.zeros((SUBLANES - XA_HEADS, XA_HEAD_DIM), F32)
    for g in range(q_ref.shape[0]):
        k8 = jnp.concatenate([k_ref[g], kv_pad], axis=1).reshape(n, XA_HEAD_DIM).astype(BF16)
        v8 = jnp.concatenate([v_ref[g], kv_pad], axis=1).reshape(n, XA_HEAD_DIM).astype(BF16)
        q8 = jnp.concatenate([q_ref[g], q_pad], axis=0).astype(BF16)
        s = lax.dot_general(q8, k8, (((1,), (1,)), ((), ())), preferred_element_type=F32)
        m = jnp.max(jnp.where(same_head, s, -jnp.inf), axis=-1, keepdims=True)
        p = jnp.where(same_head, jnp.exp(s - m), 0.0)
        den = jnp.sum(p, axis=-1, keepdims=True)
        o8 = jnp.dot(p.astype(BF16), v8, preferred_element_type=F32)
        o_ref[g] = (o8 / den)[0:XA_HEADS, :]


def _oproj_router_kernel(x1_ref, o_ref, wo_ref, gmoe_ref, wrt_ref, brt_ref,
                         x2_ref, xn_ref, tope_ref, gates_ref):
    x2 = x1_ref[...] + _bdot(o_ref[...], wo_ref[...])
    x2_ref[...] = x2
    _router(x2, gmoe_ref, wrt_ref, brt_ref, xn_ref, tope_ref, gates_ref)


def _attn_sample(x1, k, v, gxa, wq, wo, router_w):
    n = x1.shape[0]
    g = SAMPLE_GROUP
    q = pl.pallas_call(
        _qproj_kernel,
        grid=(1,),
        in_specs=[_full((n, D_MODEL)), _full((1, D_MODEL)), _full((D_MODEL, D_MODEL))],
        out_specs=_full((n, D_MODEL)),
        out_shape=jax.ShapeDtypeStruct((n, D_MODEL), F32),
        compiler_params=_cparams(("arbitrary",)),
        name="qproj_sample",
    )(x1, gxa, wq)
    kv = pl.BlockSpec((g, N_MEM, XA_HEADS, XA_HEAD_DIM), lambda i: (i, 0, 0, 0))
    row = pl.BlockSpec((g, XA_HEADS, XA_HEAD_DIM), lambda i: (i, 0, 0))
    o = pl.pallas_call(
        _attn_sample_core_kernel,
        grid=(n // g,),
        in_specs=[row, kv, kv],
        out_specs=row,
        out_shape=jax.ShapeDtypeStruct((n, XA_HEADS, XA_HEAD_DIM), F32),
        compiler_params=_cparams(("arbitrary",)),
        name="attn_sample_core",
    )(q.reshape(n, XA_HEADS, XA_HEAD_DIM), k, v).reshape(n, D_MODEL)
    out_specs, out_shapes = _router_out(n, n, lambda i: 0)
    return pl.pallas_call(
        _oproj_router_kernel,
        grid=(1,),
        in_specs=[_full((n, D_MODEL)), _full((n, D_MODEL)), _full((D_MODEL, D_MODEL))] + _router_specs(),
        out_specs=out_specs,
        out_shape=out_shapes,
        compiler_params=_cparams(("arbitrary",)),
        name="oproj_router_sample",
    )(x1, o, wo, *router_w)


def _moe_num_blocks(n_assign):
    return pl.cdiv(n_assign, MOE_BLOCK) + N_EXPERTS


def _slots_kernel(tp_ref, ts_ref, destp_ref, dests_ref, blke_ref, nact_ref, cnt_ref, pstart_ref,
                  rankp, ranks):
    def make_body(top_ref, rank_ref):
        c = min(SLOT_CHUNK, top_ref.shape[1])
        r_i = lax.broadcasted_iota(I32, (c, c), 0)
        c_i = lax.broadcasted_iota(I32, (c, c), 1)
        upper = (r_i < c_i).astype(BF16)
        e_iota = lax.broadcasted_iota(I32, (N_EXPERTS, c), 0)

        def body(ci, carry):
            lo = pl.multiple_of(ci * c, c)
            top = top_ref[:, pl.ds(lo, c)]
            hit = [e_iota == top[k:k + 1, :] for k in range(TOP_K)]
            cnt = sum(m.astype(F32) for m in hit)
            base = carry + jnp.dot(cnt.astype(BF16), upper, preferred_element_type=F32)
            rank_ref[:, pl.ds(lo, c)] = jnp.concatenate(
                [jnp.sum(jnp.where(m, base, 0.0), axis=0, keepdims=True) for m in hit], axis=0)
            return carry + jnp.sum(cnt, axis=1, keepdims=True)
        return top_ref.shape[1] // c, body

    counts = jnp.zeros((N_EXPERTS, 1), F32)
    for top_ref, rank_ref in ((tp_ref, rankp), (ts_ref, ranks)):
        trips, body = make_body(top_ref, rank_ref)
        counts = lax.fori_loop(0, trips, body, counts)

    padded = jnp.floor((counts + (MOE_BLOCK - 1)) / MOE_BLOCK) * MOE_BLOCK
    er = lax.broadcasted_iota(I32, (N_EXPERTS, N_EXPERTS), 0)
    ec = lax.broadcasted_iota(I32, (N_EXPERTS, N_EXPERTS), 1)
    padded_row = jnp.sum(jnp.where(er == ec, padded, 0.0), axis=0, keepdims=True)
    cum = jnp.sum(jnp.where(ec <= er, padded_row, 0.0), axis=1, keepdims=True)
    pstart = cum - padded
    cnt_ref[...] = jnp.sum(jnp.where(er == ec, counts, 0.0), axis=0, keepdims=True).astype(I32)
    pstart_ref[...] = jnp.sum(jnp.where(er == ec, pstart, 0.0), axis=0, keepdims=True).astype(I32)

    def dest(top_ref, rank_ref, out_ref):
        n = top_ref.shape[1]
        ei = lax.broadcasted_iota(I32, (N_EXPERTS, n), 0)
        rows = []
        for k in range(TOP_K):
            start = jnp.sum(jnp.where(ei == top_ref[k:k + 1, :], pstart, 0.0), axis=0, keepdims=True)
            rows.append(start + rank_ref[k:k + 1, :])
        out_ref[...] = jnp.concatenate(rows, axis=0).astype(I32)

    dest(tp_ref, rankp, destp_ref)
    dest(ts_ref, ranks, dests_ref)

    nb = blke_ref.shape[1]
    blk_lo = lax.broadcasted_iota(I32, (N_EXPERTS, nb), 1).astype(F32) * MOE_BLOCK
    blk = jnp.sum((cum <= blk_lo).astype(F32), axis=0, keepdims=True)
    blke_ref[...] = jnp.minimum(blk, N_EXPERTS - 1).astype(I32)
    nact_ref[...] = (cum[N_EXPERTS - 1:N_EXPERTS, :] / MOE_BLOCK).astype(I32)


def _slots(top_p, top_s):
    n_p, n_s = top_p.shape[1], top_s.shape[1]
    nb = _moe_num_blocks((n_p + n_s) * TOP_K)
    return pl.pallas_call(
        _slots_kernel,
        grid=(1,),
        in_specs=[_full((TOP_K, n_p)), _full((TOP_K, n_s))],
        out_specs=[_full((TOP_K, n_p)), _full((TOP_K, n_s)), _full((1, nb)), _full((1, 1)),
                   _full((1, N_EXPERTS)), _full((1, N_EXPERTS))],
        out_shape=[
            jax.ShapeDtypeStruct((TOP_K, n_p), I32),
            jax.ShapeDtypeStruct((TOP_K, n_s), I32),
            jax.ShapeDtypeStruct((1, nb), I32),
            jax.ShapeDtypeStruct((1, 1), I32),
            jax.ShapeDtypeStruct((1, N_EXPERTS), I32),
            jax.ShapeDtypeStruct((1, N_EXPERTS), I32),
        ],
        scratch_shapes=[pltpu.VMEM((TOP_K, n_p), F32), pltpu.VMEM((TOP_K, n_s), F32)],
        compiler_params=_cparams(("arbitrary",)),
        name="moe_slots",
    )(top_p, top_s)


def _for_rows(n_rows, fn):
    def body(i, c):
        for u in range(DMA_UNROLL):
            fn(i * DMA_UNROLL + u)
        return c
    lax.fori_loop(0, n_rows // DMA_UNROLL, body, 0)


def _dispatch_kernel(cnt_ref, pstart_ref, nact_ref, destp_ref, dests_ref, xnp_ref, xns_ref, xs_ref,
                     zeros, sem, zsem):
    i = pl.program_id(0)
    last = pl.num_programs(0) - 1

    def scatter(x_ref, dest_ref):
        n = x_ref.shape[0] // ROW_TILE

        def one(t):
            for k in range(TOP_K):
                _row_tile_copy(x_ref, t, xs_ref, dest_ref[t * TOP_K + k], sem).start(priority=k % 2)

        _for_rows(n, one)
        _row_tile_copy(xs_ref, 0, xs_ref, 0, sem, n * TOP_K).wait()

    @pl.when(i < last)
    def _():
        scatter(xnp_ref, destp_ref)

    @pl.when(i == last)
    def _():
        scatter(xns_ref, dests_ref)
        zeros[...] = jnp.zeros(zeros.shape, F32)

        def pad_copies(e, act):
            c = cnt_ref[e]
            npad = (-c) & (MOE_BLOCK - 1)
            base = pstart_ref[e] + c
            for bit in range(MOE_BLOCK.bit_length() - 1):
                size = 1 << bit

                @pl.when((npad & size) != 0)
                def _():
                    act(_row_tile_copy(zeros, 0, xs_ref, base + (npad & (size - 1)), zsem, size))

        def tail_copies(act):
            def body(j, c):
                act(_row_tile_copy(zeros, 0, xs_ref, j * MOE_BLOCK, zsem, MOE_BLOCK))
                return c
            lax.fori_loop(nact_ref[0], xs_ref.shape[0] // (MOE_BLOCK * ROW_TILE), body, 0)

        for act in (lambda d: d.start(), lambda d: d.wait()):
            lax.fori_loop(0, N_EXPERTS, lambda e, c, act=act: (pad_copies(e, act), c)[1], 0)
            tail_copies(act)


def _dispatch(cnt, pstart, nact, dest_p, dest_s, xn_p, xn_s, n_rows):
    n_p, n_s = xn_p.shape[0] // ROW_TILE, xn_s.shape[0] // ROW_TILE
    tile = DISPATCH_TILE
    nt = n_p // tile
    smem = pl.BlockSpec(memory_space=pltpu.SMEM)
    return pl.pallas_call(
        _dispatch_kernel,
        grid=(nt + 1,),
        in_specs=[
            smem, smem, smem,
            pl.BlockSpec((TOP_K * tile,), lambda i: (jnp.minimum(i, nt - 1),), memory_space=pltpu.SMEM),
            smem,
            pl.BlockSpec((tile * ROW_TILE, LANES), lambda i: (jnp.minimum(i, nt - 1), 0)),
            _full((n_s * ROW_TILE, LANES)),
        ],
        out_specs=pl.BlockSpec(memory_space=pl.ANY),
        out_shape=jax.ShapeDtypeStruct((n_rows * ROW_TILE, LANES), F32),
        scratch_shapes=[pltpu.VMEM((MOE_BLOCK * ROW_TILE, LANES), F32), pltpu.SemaphoreType.DMA,
                        pltpu.SemaphoreType.DMA],
        compiler_params=_cparams(("arbitrary",)),
        name="moe_dispatch",
    )(cnt, pstart, nact, dest_p, dest_s, xn_p, xn_s)


def _combine_kernel(dest_ref, dest_next_ref, gates_ref, x2_ref, gfin_ref, ys_ref, y_ref, bufs, sems):
    tile = x2_ref.shape[0]
    i = pl.program_id(0)
    slot = i % 2

    def gather(d_ref, b):
        def one(t):
            for k in range(TOP_K):
                _row_tile_copy(ys_ref, d_ref[t * TOP_K + k], bufs.at[b, k], t, sems.at[b]).start(priority=k % 2)
        _for_rows(tile, one)

    @pl.when(i == 0)
    def _():
        gather(dest_ref, 0)

    for b in range(2):
        @pl.when(slot == b)
        def _():
            pltpu.make_async_copy(bufs.at[b], bufs.at[b], sems.at[b]).wait()

            @pl.when(i + 1 < pl.num_programs(0))
            def _():
                gather(dest_next_ref, 1 - b)

    buf = bufs.at[slot]
    sumsq = jnp.zeros((tile, 1), F32)
    gate = [jnp.broadcast_to(gates_ref[:, k:k + 1], (tile, LANES)) for k in range(TOP_K)]
    for s in range(ROW_TILE):
        cols = slice(s * LANES, (s + 1) * LANES)
        acc = x2_ref[:, cols]
        for k in range(TOP_K):
            acc = acc + gate[k] * buf[k, pl.ds(s, tile, stride=ROW_TILE), :]
        y_ref[:, cols] = acc
        sumsq = sumsq + jnp.sum(acc * acc, axis=-1, keepdims=True)
    scale = lax.rsqrt(sumsq / D_MODEL + EPS)
    for s in range(ROW_TILE):
        cols = slice(s * LANES, (s + 1) * LANES)
        y_ref[:, cols] = y_ref[:, cols] * scale * gfin_ref[:, cols]


def _combine(dest, gates_t, x2, g_final, ys):
    n = x2.shape[0]
    tile = min(COMBINE_TILE, n)
    row = pl.BlockSpec((tile, D_MODEL), lambda i: (i, 0))
    nt = n // tile
    return pl.pallas_call(
        _combine_kernel,
        grid=(nt,),
        in_specs=[
            pl.BlockSpec((TOP_K * tile,), lambda i: (i,), memory_space=pltpu.SMEM),
            pl.BlockSpec((TOP_K * tile,), lambda i: (jnp.minimum(i + 1, nt - 1),), memory_space=pltpu.SMEM),
            pl.BlockSpec((tile, TOP_K), lambda i: (i, 0)),
            row,
            _full((1, D_MODEL)),
            pl.BlockSpec(memory_space=pl.ANY),
        ],
        out_specs=row,
        out_shape=jax.ShapeDtypeStruct((n, D_MODEL), F32),
        scratch_shapes=[pltpu.VMEM((2, TOP_K, tile * ROW_TILE, LANES), F32), pltpu.SemaphoreType.DMA((2,))],
        compiler_params=_cparams(("arbitrary",)),
        name="moe_combine",
    )(dest, dest, gates_t, x2, g_final, ys)


def _experts_kernel(blke_ref, nact_ref, xs_ref, w1_hbm, b1a_ref, b1b_ref, w2_hbm, b2a_ref, b2b_ref, ys_ref,
                    w1f, w2f, w1b, w2b, wsem, started):
    j = pl.program_id(0)
    nact = nact_ref[0]
    last = blke_ref.shape[0] - 1
    first, second = 2 * j, 2 * j + 1
    e0, e1 = blke_ref[first], blke_ref[second]
    active0, active1 = first < nact, second < nact
    together = jnp.logical_and(active1, e1 == e0)
    half = MOE_BLOCK * ROW_TILE

    def fetch(expert, slot):
        return (pltpu.make_async_copy(w1_hbm.at[expert], w1f.at[slot], wsem.at[slot]),
                pltpu.make_async_copy(w2_hbm.at[expert], w2f.at[slot], wsem.at[slot]))

    def switch_to(blk):
        e = blke_ref[blk]
        n = started[0]
        jn = lax.while_loop(lambda jj: jnp.logical_and(jj < nact, blke_ref[jnp.minimum(jj, last)] == e),
                            lambda jj: jj + 1, blk + 1)
        for b in range(2):
            @pl.when(n % 2 == b)
            def _():
                for d in fetch(e, b):
                    d.wait()

                @pl.when(jn < nact)
                def _():
                    for d in fetch(blke_ref[jnp.minimum(jn, last)], 1 - b):
                        d.start(priority=1)

                w1b[...] = w1f[b].astype(BF16)
                w2b[...] = w2f[b].astype(BF16)
        started[0] = n + 1

    def ffn(x_ref, n_rows, b1_ref, b2_ref, y_ref):
        x = _load_row_tiles(x_ref, n_rows).astype(BF16)
        gu = jnp.dot(x, w1b[...], preferred_element_type=F32) + b1_ref[0]
        g = jnp.minimum(gu[:, :D_EXPERT], SWIGLU_LIMIT)
        u = jnp.clip(gu[:, D_EXPERT:], -SWIGLU_LIMIT, SWIGLU_LIMIT)
        h = (u + 1.0) * (g * jax.nn.sigmoid(SWIGLU_ALPHA * g))
        _store_row_tiles(y_ref, _bdot(h, w2b[...]) + b2_ref[0])

    @pl.when(j == 0)
    def _():
        started[0] = 0
        for d in fetch(e0, 0):
            d.start()

    @pl.when(jnp.logical_and(active0, jnp.logical_or(j == 0, e0 != blke_ref[jnp.maximum(first - 1, 0)])))
    def _():
        switch_to(first)

    @pl.when(together)
    def _():
        ffn(xs_ref, 2 * MOE_BLOCK, b1a_ref, b2a_ref, ys_ref)

    @pl.when(jnp.logical_and(active0, jnp.logical_not(together)))
    def _():
        ffn(xs_ref.at[pl.ds(0, half)], MOE_BLOCK, b1a_ref, b2a_ref, ys_ref.at[pl.ds(0, half)])

        @pl.when(active1)
        def _():
            switch_to(second)
            ffn(xs_ref.at[pl.ds(half, half)], MOE_BLOCK, b1b_ref, b2b_ref, ys_ref.at[pl.ds(half, half)])

        @pl.when(jnp.logical_not(active1))
        def _():
            ys_ref[pl.ds(half, half), :] = jnp.zeros((half, LANES), F32)

    @pl.when(jnp.logical_not(active0))
    def _():
        ys_ref[...] = jnp.zeros(ys_ref.shape, F32)


def _experts(blk_e, nact, xs, w1, b1, w2, b2):
    nb = xs.shape[0] // (MOE_BLOCK * ROW_TILE)
    assert nb % 2 == 0
    blk = lambda j, be, na: (jnp.minimum(j, (na[0] - 1) // 2), 0)
    exp = lambda h: (lambda j, be, na: (be[jnp.minimum(2 * j + h, na[0] - 1)], 0, 0))
    rows = (2 * MOE_BLOCK * ROW_TILE, LANES)
    return pl.pallas_call(
        _experts_kernel,
        grid_spec=pltpu.PrefetchScalarGridSpec(
            num_scalar_prefetch=2,
            grid=(nb // 2,),
            in_specs=[
                pl.BlockSpec(rows, blk),
                pl.BlockSpec(memory_space=pl.ANY),
                pl.BlockSpec((1, 1, 2 * D_EXPERT), exp(0)),
                pl.BlockSpec((1, 1, 2 * D_EXPERT), exp(1)),
                pl.BlockSpec(memory_space=pl.ANY),
                pl.BlockSpec((1, 1, D_MODEL), exp(0)),
                pl.BlockSpec((1, 1, D_MODEL), exp(1)),
            ],
            out_specs=pl.BlockSpec(rows, lambda j, be, na: (j, 0)),
            scratch_shapes=[
                pltpu.VMEM((2, D_MODEL, 2 * D_EXPERT), F32), pltpu.VMEM((2, D_EXPERT, D_MODEL), F32),
                pltpu.VMEM((D_MODEL, 2 * D_EXPERT), BF16), pltpu.VMEM((D_EXPERT, D_MODEL), BF16),
                pltpu.SemaphoreType.DMA((2,)), pltpu.SMEM((1,), I32),
            ],
        ),
        out_shape=jax.ShapeDtypeStruct(xs.shape, F32),
        compiler_params=_cparams(("arbitrary",)),
        name="moe_experts",
    )(blk_e, nact, xs, w1, b1, b1, w2, b2, b2)


def _block_diag(w):
    h, d, _ = w.shape
    eye = jnp.eye(h, dtype=w.dtype)
    return (eye[:, None, :, None] * w[:, :, None, :]).reshape(h * d, h * d)


def kernel(x_prompt, x_sample, state_lru_h, cache_lru_conv, cache_cf_conv, cache_mem_k, cache_mem_v,
           mem_prompt, g_mix, w_in, w_lru_conv, b_lru_conv, w_lru_a, b_lru_a, w_lru_x, b_lru_x,
           lru_lambda, w_cf_conv, b_cf_conv, g_cf_ln, b_cf_ln, w_out, g_xa, g_mem, w_q, w_k, w_v, w_o,
           g_moe, w_router, b_router, w_e1, b_e1, w_e2, b_e2, g_final):
    depth = g_mix.shape[0]
    assert depth == 1, "single-layer stack"
    batch, seq, _ = x_prompt.shape
    n_s = x_sample.shape[0]
    assert x_sample.shape[1] == 1
    n_p = batch * seq
    l = 0

    mix_w = (
        g_mix[l][None], w_in[l].astype(BF16), w_lru_conv[l], b_lru_conv[l][None],
        jnp.concatenate([_block_diag(w_lru_a[l]), _block_diag(w_lru_x[l])], axis=1).astype(BF16),
        jnp.concatenate([b_lru_a[l], b_lru_x[l]])[None], lru_lambda[l][None],
        w_cf_conv[l], b_cf_conv[l][None], g_cf_ln[l][None], b_cf_ln[l][None], w_out[l].astype(BF16),
    )
    router_w = (g_moe[l][None], w_router[l].T.astype(BF16), b_router[l][:, None])
    wq, wo = w_q[l].astype(BF16), w_o[l].astype(BF16)
    gxa = g_xa[l][None]

    xp = x_prompt.reshape(n_p, D_MODEL)
    xs_ = x_sample.reshape(n_s, D_MODEL)
    x1p, hp, lcp, ccp = _mixer_prompt(xp, mix_w, batch, seq)
    x1s, hs, lru_conv_s, cf_conv_s = _mixer_sample(xs_, mix_w, state_lru_h[l], cache_lru_conv[l], cache_cf_conv[l])

    mk, mv, mkb, mvb = _memkv(mem_prompt.reshape(batch * N_MEM, D_MODEL), g_mem[l][None],
                              w_k[l].astype(BF16), w_v[l].astype(BF16))
    x2p, xnp_, top_p, gates_p = _attn_prompt(x1p, mkb, mvb, gxa, wq, wo, router_w, batch, seq)
    x2s, xns, top_s, gates_s = _attn_sample(x1s, cache_mem_k[l], cache_mem_v[l], gxa, wq, wo, router_w)

    dest_p, dest_s, blk_e, nact, cnt, pstart = _slots(top_p, top_s)
    dest_p, dest_s = dest_p.T.reshape(-1), dest_s.T.reshape(-1)
    nb = _moe_num_blocks((n_p + n_s) * TOP_K)
    nact = nact.reshape(1)
    xs = _dispatch(cnt.reshape(N_EXPERTS), pstart.reshape(N_EXPERTS), nact, dest_p, dest_s, xnp_, xns,
                   nb * MOE_BLOCK)
    ys = _experts(blk_e.reshape(nb), nact, xs, w_e1[l], b_e1[l][:, None, :], w_e2[l], b_e2[l][:, None, :])
    gfin = g_final[None]
    y_p = _combine(dest_p, gates_p.T, x2p, gfin, ys)
    y_s = _combine(dest_s, gates_s.T, x2s, gfin, ys)

    return (
        y_p.reshape(batch, seq, D_MODEL),
        y_s.reshape(n_s, 1, D_MODEL),
        hp.reshape(depth, batch, LRU_WIDTH),
        lcp[None],
        ccp[None],
        mk[None],
        mv[None],
        hs[None],
        lru_conv_s[None],
        cf_conv_s[None],
    )
```

```python
import jax
import jax.numpy as jnp
from jax import lax
from jax.experimental import pallas as pl
from jax.experimental.pallas import tpu as pltpu

F32 = jnp.float32
BF16 = jnp.bfloat16
I32 = jnp.int32

D_MODEL = 1024
LRU_WIDTH = 512
CF_WIDTH = 512
LRU_CONV = 4
LRU_C = 8.0
CF_CONV = 31
IN_WIDTH = 2 * LRU_WIDTH + 2 * CF_WIDTH
N_MEM = 256
XA_HEADS = 4
XA_HEAD_DIM = D_MODEL // XA_HEADS
N_EXPERTS = 32
TOP_K = 4
D_EXPERT = D_MODEL
SWIGLU_LIMIT = 7.0
SWIGLU_ALPHA = 1.702
EPS = 1e-6

LANES = 128
SUBLANES = 8
ROW_TILE = D_MODEL // LANES
assert ROW_TILE == SUBLANES
VMEM_LIMIT = 56 * 1024 * 1024

MIX_STEPS = 128
ATT_TILE = 1024
SAMPLE_GROUP = 4
SAMPLE_MIX_GROUP = 32
MOE_BLOCK = 256
SLOT_CHUNK = 256
DMA_UNROLL = 16
DISPATCH_TILE = 4096
COMBINE_TILE = 256


def _cparams(sem):
    return pltpu.CompilerParams(dimension_semantics=sem, vmem_limit_bytes=VMEM_LIMIT)


def _full(shape):
    n = len(shape)
    return pl.BlockSpec(shape, lambda *_: (0,) * n)


def _rms(x, g):
    return x * lax.rsqrt(jnp.mean(x * x, axis=-1, keepdims=True) + EPS) * g


def _bdot(a, b):
    return jnp.dot(a.astype(BF16), b, preferred_element_type=F32)


def _store_row_tiles(ref, x):
    m = x.shape[0]
    for s in range(ROW_TILE):
        ref[pl.ds(s, m, stride=ROW_TILE), :] = x[:, s * LANES:(s + 1) * LANES]


def _load_row_tiles(ref, m):
    return jnp.concatenate([ref[pl.ds(s, m, stride=ROW_TILE), :] for s in range(ROW_TILE)], axis=-1)


def _row_tile_copy(src_ref, src_row, dst_ref, dst_row, sem, n_rows=1):
    src = src_ref.at[pl.ds(pl.multiple_of(src_row * ROW_TILE, ROW_TILE), n_rows * ROW_TILE)]
    dst = dst_ref.at[pl.ds(pl.multiple_of(dst_row * ROW_TILE, ROW_TILE), n_rows * ROW_TILE)]
    return pltpu.make_async_copy(src, dst, sem)


def _lru_coeffs(xc, wg_ref, bg_ref, lam_ref):
    gl = _bdot(xc, wg_ref[...]) + bg_ref[...]
    r = jax.nn.sigmoid(gl[:, :LRU_WIDTH])
    i = jax.nn.sigmoid(gl[:, LRU_WIDTH:])
    log_a = -LRU_C * r * jax.nn.softplus(-lam_ref[...])
    a = jnp.exp(log_a)
    u = jnp.sqrt(-jnp.tanh(log_a) * (1.0 + a * a)) * (i * xc)
    return a, u


def _cf_post(gc, gln_ref, bln_ref):
    mu = jnp.mean(gc, axis=-1, keepdims=True)
    c = gc - mu
    y = c * lax.rsqrt(jnp.mean(c * c, axis=-1, keepdims=True) + EPS)
    return jax.nn.silu(y * gln_ref[...] + bln_ref[...])


def _mixer_out(x, y_lru, y_cf, wout_ref):
    y = _bdot(y_lru, wout_ref[:LRU_WIDTH, :]) + _bdot(y_cf, wout_ref[LRU_WIDTH:, :])
    return x + y


def _mixer_prompt_kernel(x_hbm, gmix_ref, win_ref, wlc_ref, blc_ref, wg_ref, bg_ref, lam_ref,
                         wcc_ref, bcc_ref, gln_ref, bln_ref, wout_ref,
                         x1_hbm, h_ref, ltail_ref, ctail_ref,
                         xbuf, obuf, zx_ext, glu_ext, h_carry, isem, osem):
    ts = MIX_STEPS
    nb = xbuf.shape[2]
    rows = ts * nb
    j = pl.program_id(0)
    nj = pl.num_programs(0)
    hl = (LRU_CONV - 1) * nb
    hc = (CF_CONV - 1) * nb

    def copies_in(step, s):
        return [pltpu.make_async_copy(x_hbm.at[b, pl.ds(step * ts, ts)], xbuf.at[s, pl.ds(0, ts), b], isem.at[s])
                for b in range(nb)]

    def copies_out(step, s):
        return [pltpu.make_async_copy(obuf.at[s, pl.ds(0, ts), b], x1_hbm.at[b, pl.ds(step * ts, ts)], osem.at[s])
                for b in range(nb)]

    @pl.when(j == 0)
    def _():
        zx_ext[0:hl, :] = jnp.zeros((hl, LRU_WIDTH), F32)
        glu_ext[0:hc, :] = jnp.zeros((hc, CF_WIDTH), F32)
        h_carry[...] = jnp.zeros(h_carry.shape, F32)
        for cp in copies_in(0, 0):
            cp.start()

    for s in range(2):
        @pl.when(j % 2 == s)
        def _():
            for cp in copies_in(j, s):
                cp.wait()

            @pl.when(j + 1 < nj)
            def _():
                for cp in copies_in(j + 1, 1 - s):
                    cp.start()

            @pl.when(j >= 2)
            def _():
                for cp in copies_out(j - 2, s):
                    cp.wait()

    x = xbuf[j % 2].reshape(rows, D_MODEL)
    z = _bdot(_rms(x, gmix_ref[...]), win_ref[...])
    zx = z[:, :LRU_WIDTH]
    zg = z[:, LRU_WIDTH:2 * LRU_WIDTH]
    za = z[:, 2 * LRU_WIDTH:2 * LRU_WIDTH + CF_WIDTH]
    zb = z[:, 2 * LRU_WIDTH + CF_WIDTH:]

    zx_ext[hl:hl + rows, :] = zx
    xc = blc_ref[...]
    for k in range(LRU_CONV):
        xc = xc + wlc_ref[k:k + 1, :] * zx_ext[k * nb:k * nb + rows, :]
    a, u = _lru_coeffs(xc, wg_ref, bg_ref, lam_ref)
    h = h_carry[...]
    steps = []
    for t in range(ts):
        h = a[t * nb:(t + 1) * nb, :] * h + u[t * nb:(t + 1) * nb, :]
        steps.append(h)
    h_carry[...] = h
    y_lru = jnp.concatenate(steps, axis=0) * jax.nn.gelu(zg)

    glu = za * jax.nn.sigmoid(zb)
    glu_ext[hc:hc + rows, :] = glu
    gc = bcc_ref[...]
    for k in range(CF_CONV):
        gc = gc + wcc_ref[k:k + 1, :] * glu_ext[k * nb:k * nb + rows, :]
    y_cf = _cf_post(gc, gln_ref, bln_ref)

    x1 = _mixer_out(x, y_lru, y_cf, wout_ref)
    for s in range(2):
        @pl.when(j % 2 == s)
        def _():
            obuf[s] = x1.reshape(ts, nb, D_MODEL)
            for cp in copies_out(j, s):
                cp.start()

            @pl.when(j == nj - 1)
            def _():
                for cp in copies_out(j, s):
                    cp.wait()

                @pl.when(j >= 1)
                def _():
                    for cp in copies_out(j - 1, 1 - s):
                        cp.wait()

    zx_ext[0:hl, :] = zx_ext[rows:rows + hl, :]
    glu_ext[0:hc, :] = glu_ext[rows:rows + hc, :]

    @pl.when(j == nj - 1)
    def _():
        h_ref[...] = h
        for b in range(nb):
            for r in range(LRU_CONV - 1):
                ltail_ref[b, r:r + 1, :] = zx_ext[r * nb + b:r * nb + b + 1, :]
            for r in range(CF_CONV - 1):
                ctail_ref[b, r:r + 1, :] = glu_ext[r * nb + b:r * nb + b + 1, :]


def _mixer_weight_specs():
    return [
        _full((1, D_MODEL)),
        _full((D_MODEL, IN_WIDTH)),
        _full((LRU_CONV, LRU_WIDTH)), _full((1, LRU_WIDTH)),
        _full((LRU_WIDTH, 2 * LRU_WIDTH)), _full((1, 2 * LRU_WIDTH)),
        _full((1, LRU_WIDTH)),
        _full((CF_CONV, CF_WIDTH)), _full((1, CF_WIDTH)),
        _full((1, CF_WIDTH)), _full((1, CF_WIDTH)),
        _full((D_MODEL, D_MODEL)),
    ]


def _mixer_prompt(x, mix_w):
    batch, seq, _ = x.shape
    ts = MIX_STEPS
    any_ = pl.BlockSpec(memory_space=pl.ANY)
    return pl.pallas_call(
        _mixer_prompt_kernel,
        grid=(seq // ts,),
        in_specs=[any_] + _mixer_weight_specs(),
        out_specs=[
            any_,
            _full((batch, LRU_WIDTH)),
            _full((batch, LRU_CONV - 1, LRU_WIDTH)),
            _full((batch, CF_CONV - 1, CF_WIDTH)),
        ],
        out_shape=[
            jax.ShapeDtypeStruct((batch, seq, D_MODEL), F32),
            jax.ShapeDtypeStruct((batch, LRU_WIDTH), F32),
            jax.ShapeDtypeStruct((batch, LRU_CONV - 1, LRU_WIDTH), F32),
            jax.ShapeDtypeStruct((batch, CF_CONV - 1, CF_WIDTH), F32),
        ],
        scratch_shapes=[
            pltpu.VMEM((2, ts, batch, D_MODEL), F32),
            pltpu.VMEM((2, ts, batch, D_MODEL), F32),
            pltpu.VMEM(((LRU_CONV - 1 + ts) * batch, LRU_WIDTH), F32),
            pltpu.VMEM(((CF_CONV - 1 + ts) * batch, CF_WIDTH), F32),
            pltpu.VMEM((batch, LRU_WIDTH), F32),
            pltpu.SemaphoreType.DMA((2,)), pltpu.SemaphoreType.DMA((2,)),
        ],
        compiler_params=_cparams(("arbitrary",)),
        name="mixer_prompt",
    )(x, *mix_w)


def _mixer_sample_kernel(x_ref, gmix_ref, win_ref, wlc_ref, blc_ref, wg_ref, bg_ref, lam_ref,
                         wcc_ref, bcc_ref, gln_ref, bln_ref, wout_ref,
                         h0_ref, lbuf_ref, cbuf_ref,
                         x1_ref, h_ref, ltail_ref, ctail_ref):
    x = x_ref[...]
    z = _bdot(_rms(x, gmix_ref[...]), win_ref[...])
    zx = z[:, :LRU_WIDTH]
    zg = z[:, LRU_WIDTH:2 * LRU_WIDTH]
    za = z[:, 2 * LRU_WIDTH:2 * LRU_WIDTH + CF_WIDTH]
    zb = z[:, 2 * LRU_WIDTH + CF_WIDTH:]

    xc = blc_ref[...] + wlc_ref[LRU_CONV - 1:LRU_CONV, :] * zx
    xc = xc + jnp.sum(lbuf_ref[...] * wlc_ref[0:LRU_CONV - 1, :][None], axis=1)
    a, u = _lru_coeffs(xc, wg_ref, bg_ref, lam_ref)
    h = a * h0_ref[...] + u
    y_lru = h * jax.nn.gelu(zg)

    glu = za * jax.nn.sigmoid(zb)
    gc = bcc_ref[...] + wcc_ref[CF_CONV - 1:CF_CONV, :] * glu
    gc = gc + jnp.sum(cbuf_ref[...] * wcc_ref[0:CF_CONV - 1, :][None], axis=1)
    y_cf = _cf_post(gc, gln_ref, bln_ref)

    x1_ref[...] = _mixer_out(x, y_lru, y_cf, wout_ref)
    h_ref[...] = h
    ltail_ref[:, 0:LRU_CONV - 2, :] = lbuf_ref[:, 1:LRU_CONV - 1, :]
    ctail_ref[:, 0:CF_CONV - 2, :] = cbuf_ref[:, 1:CF_CONV - 1, :]
    for i in range(x.shape[0]):
        ltail_ref[i, LRU_CONV - 2:LRU_CONV - 1, :] = zx[i:i + 1, :]
        ctail_ref[i, CF_CONV - 2:CF_CONV - 1, :] = glu[i:i + 1, :]


def _mixer_sample(x, mix_w, h0, lbuf, cbuf):
    n = x.shape[0]
    g = SAMPLE_MIX_GROUP
    row = lambda w: pl.BlockSpec((g, w), lambda i: (i, 0))
    return pl.pallas_call(
        _mixer_sample_kernel,
        grid=(n // g,),
        in_specs=[row(D_MODEL)] + _mixer_weight_specs() + [
            row(LRU_WIDTH),
            pl.BlockSpec((g, LRU_CONV - 1, LRU_WIDTH), lambda i: (i, 0, 0)),
            pl.BlockSpec((g, CF_CONV - 1, CF_WIDTH), lambda i: (i, 0, 0)),
        ],
        out_specs=[row(D_MODEL), row(LRU_WIDTH),
                   pl.BlockSpec((g, LRU_CONV - 1, LRU_WIDTH), lambda i: (i, 0, 0)),
                   pl.BlockSpec((g, CF_CONV - 1, CF_WIDTH), lambda i: (i, 0, 0))],
        out_shape=[
            jax.ShapeDtypeStruct((n, D_MODEL), F32),
            jax.ShapeDtypeStruct((n, LRU_WIDTH), F32),
            jax.ShapeDtypeStruct((n, LRU_CONV - 1, LRU_WIDTH), F32),
            jax.ShapeDtypeStruct((n, CF_CONV - 1, CF_WIDTH), F32),
        ],
        compiler_params=_cparams(("arbitrary",)),
        name="mixer_sample",
    )(x, *mix_w, h0, lbuf, cbuf)


def _memkv_kernel(mem_ref, gmem_ref, wk_ref, wv_ref, k_ref, v_ref, kb_ref, vb_ref):
    mn = _rms(mem_ref[...], gmem_ref[...]).astype(BF16)
    k = jnp.dot(mn, wk_ref[...], preferred_element_type=F32)
    v = jnp.dot(mn, wv_ref[...], preferred_element_type=F32)
    for h in range(XA_HEADS):
        sl = slice(h * XA_HEAD_DIM, (h + 1) * XA_HEAD_DIM)
        k_ref[0, :, h, :] = k[:, sl]
        v_ref[0, :, h, :] = v[:, sl]
    kb_ref[...] = k.astype(BF16)
    vb_ref[...] = v.astype(BF16)


def _memkv(mem, g_mem, wk, wv):
    n = mem.shape[0]
    t = N_MEM
    row = pl.BlockSpec((t, D_MODEL), lambda i: (i, 0))
    state = pl.BlockSpec((1, N_MEM, XA_HEADS, XA_HEAD_DIM), lambda i: (i, 0, 0, 0))
    state_shape = jax.ShapeDtypeStruct((n // t, N_MEM, XA_HEADS, XA_HEAD_DIM), F32)
    return pl.pallas_call(
        _memkv_kernel,
        grid=(n // t,),
        in_specs=[row, _full((1, D_MODEL)), _full((D_MODEL, D_MODEL)), _full((D_MODEL, D_MODEL))],
        out_specs=[state, state, row, row],
        out_shape=[state_shape] * 2 + [jax.ShapeDtypeStruct((n, D_MODEL), BF16)] * 2,
        compiler_params=_cparams(("arbitrary",)),
        name="memory_kv",
    )(mem, g_mem, wk, wv)


def _router(x2, gmoe_ref, wrt_ref, brt_ref, xn_ref, tope_ref, gates_ref):
    xn = _rms(x2, gmoe_ref[...])
    _store_row_tiles(xn_ref, xn)
    logits = lax.dot_general(wrt_ref[...], xn.astype(BF16), (((1,), (1,)), ((), ())),
                             preferred_element_type=F32) + brt_ref[...]
    e_iota = lax.broadcasted_iota(I32, logits.shape, 0)
    work = logits
    vals, idxs = [], []
    for _ in range(TOP_K):
        m = jnp.max(work, axis=0, keepdims=True)
        idx = jnp.min(jnp.where(work == m, e_iota, N_EXPERTS), axis=0, keepdims=True)
        vals.append(m)
        idxs.append(idx)
        work = jnp.where(e_iota == idx, -jnp.inf, work)
    ex = [jnp.exp(v - vals[0]) for v in vals]
    den = ex[0] + ex[1] + ex[2] + ex[3]
    tope_ref[...] = jnp.concatenate(idxs, axis=0)
    gates_ref[...] = jnp.concatenate([e / den for e in ex], axis=0)


def _router_specs():
    return [_full((1, D_MODEL)), _full((N_EXPERTS, D_MODEL)), _full((N_EXPERTS, 1))]


def _router_out(n, tile, index):
    specs = [
        pl.BlockSpec((tile, D_MODEL), lambda *a: (index(*a), 0)),
        pl.BlockSpec((tile * ROW_TILE, LANES), lambda *a: (index(*a), 0)),
        pl.BlockSpec((TOP_K, tile), lambda *a: (0, index(*a))),
        pl.BlockSpec((TOP_K, tile), lambda *a: (0, index(*a))),
    ]
    shapes = [
        jax.ShapeDtypeStruct((n, D_MODEL), F32),
        jax.ShapeDtypeStruct((n * ROW_TILE, LANES), F32),
        jax.ShapeDtypeStruct((TOP_K, n), I32),
        jax.ShapeDtypeStruct((TOP_K, n), F32),
    ]
    return specs, shapes


def _attn_prompt_kernel(x1_ref, k_ref, v_ref, gxa_ref, wq_ref, wo_ref, gmoe_ref, wrt_ref, brt_ref,
                        x2_ref, xn_ref, tope_ref, gates_ref):
    x1 = x1_ref[...]
    q = (_bdot(_rms(x1, gxa_ref[...]), wq_ref[...]) * (XA_HEAD_DIM ** -0.5)).astype(BF16)
    outs = []
    for h in range(XA_HEADS):
        sl = slice(h * XA_HEAD_DIM, (h + 1) * XA_HEAD_DIM)
        s = lax.dot_general(q[:, sl], k_ref[:, sl], (((1,), (1,)), ((), ())), preferred_element_type=F32)
        p = jnp.exp(s - jnp.max(s, axis=-1, keepdims=True))
        p = p / jnp.sum(p, axis=-1, keepdims=True)
        outs.append(_bdot(p, v_ref[:, sl]))
    o = jnp.concatenate(outs, axis=-1)
    x2 = x1 + _bdot(o, wo_ref[...])
    x2_ref[...] = x2
    _router(x2, gmoe_ref, wrt_ref, brt_ref, xn_ref, tope_ref, gates_ref)


def _attn_prompt(x1, k, v, gxa, wq, wo, router_w, batch, seq):
    tq = ATT_TILE
    nt = seq // tq
    out_specs, out_shapes = _router_out(batch * seq, tq, lambda b, j: b * nt + j)
    kv = pl.BlockSpec((N_MEM, D_MODEL), lambda b, j: (b, 0))
    return pl.pallas_call(
        _attn_prompt_kernel,
        grid=(batch, nt),
        in_specs=[pl.BlockSpec((tq, D_MODEL), lambda b, j: (b * nt + j, 0)), kv, kv,
                  _full((1, D_MODEL)), _full((D_MODEL, D_MODEL)), _full((D_MODEL, D_MODEL))] + _router_specs(),
        out_specs=out_specs,
        out_shape=out_shapes,
        compiler_params=_cparams(("arbitrary", "arbitrary")),
        name="attn_prompt",
    )(x1, k, v, gxa, wq, wo, *router_w)


def _qproj_kernel(x1_ref, gxa_ref, wq_ref, q_ref):
    q_ref[...] = _bdot(_rms(x1_ref[...], gxa_ref[...]), wq_ref[...]) * (XA_HEAD_DIM ** -0.5)


def _attn_sample_core_kernel(q_ref, k_ref, v_ref, o_ref):
    n = N_MEM * SUBLANES
    col_head = lax.broadcasted_iota(I32, (SUBLANES, n), 1) & (SUBLANES - 1)
    same_head = col_head == lax.broadcasted_iota(I32, (SUBLANES, n), 0)
    kv_pad = jnp.zeros((N_MEM, SUBLANES - XA_HEADS, XA_HEAD_DIM), F32)
    q_pad = jnp.zeros((SUBLANES - XA_HEADS, XA_HEAD_DIM), F32)
    for g in range(q_ref.shape[0]):
        k8 = jnp.concatenate([k_ref[g], kv_pad], axis=1).reshape(n, XA_HEAD_DIM).astype(BF16)
        v8 = jnp.concatenate([v_ref[g], kv_pad], axis=1).reshape(n, XA_HEAD_DIM).astype(BF16)
        q8 = jnp.concatenate([q_ref[g], q_pad], axis=0).astype(BF16)
        s = lax.dot_general(q8, k8, (((1,), (1,)), ((), ())), preferred_element_type=F32)
        m = jnp.max(jnp.where(same_head, s, -jnp.inf), axis=-1, keepdims=True)
        p = jnp.where(same_head, jnp.exp(s - m), 0.0)
        den = jnp.sum(p, axis=-1, keepdims=True)
        o8 = jnp.dot(p.astype(BF16), v8, preferred_element_type=F32)
        o_ref[g] = (o8 / den)[0:XA_HEADS, :]


def _oproj_router_kernel(x1_ref, o_ref, wo_ref, gmoe_ref, wrt_ref, brt_ref,
                         x2_ref, xn_ref, tope_ref, gates_ref):
    x2 = x1_ref[...] + _bdot(o_ref[...], wo_ref[...])
    x2_ref[...] = x2
    _router(x2, gmoe_ref, wrt_ref, brt_ref, xn_ref, tope_ref, gates_ref)


def _attn_sample(x1, k, v, gxa, wq, wo, router_w):
    n = x1.shape[0]
    g = SAMPLE_GROUP
    q = pl.pallas_call(
        _qproj_kernel,
        grid=(1,),
        in_specs=[_full((n, D_MODEL)), _full((1, D_MODEL)), _full((D_MODEL, D_MODEL))],
        out_specs=_full((n, D_MODEL)),
        out_shape=jax.ShapeDtypeStruct((n, D_MODEL), F32),
        compiler_params=_cparams(("arbitrary",)),
        name="qproj_sample",
    )(x1, gxa, wq)
    kv = pl.BlockSpec((g, N_MEM, XA_HEADS, XA_HEAD_DIM), lambda i: (i, 0, 0, 0))
    row = pl.BlockSpec((g, XA_HEADS, XA_HEAD_DIM), lambda i: (i, 0, 0))
    o = pl.pallas_call(
        _attn_sample_core_kernel,
        grid=(n // g,),
        in_specs=[row, kv, kv],
        out_specs=row,
        out_shape=jax.ShapeDtypeStruct((n, XA_HEADS, XA_HEAD_DIM), F32),
        compiler_params=_cparams(("arbitrary",)),
        name="attn_sample_core",
    )(q.reshape(n, XA_HEADS, XA_HEAD_DIM), k, v).reshape(n, D_MODEL)
    out_specs, out_shapes = _router_out(n, n, lambda i: 0)
    return pl.pallas_call(
        _oproj_router_kernel,
        grid=(1,),
        in_specs=[_full((n, D_MODEL)), _full((n, D_MODEL)), _full((D_MODEL, D_MODEL))] + _router_specs(),
        out_specs=out_specs,
        out_shape=out_shapes,
        compiler_params=_cparams(("arbitrary",)),
        name="oproj_router_sample",
    )(x1, o, wo, *router_w)


def _moe_num_blocks(n_assign):
    return pl.cdiv(n_assign, MOE_BLOCK) + N_EXPERTS


def _slots_kernel(tp_ref, ts_ref, destp_ref, dests_ref, blke_ref, nact_ref, cnt_ref, pstart_ref,
                  rankp, ranks):
    def make_body(top_ref, rank_ref):
        c = min(SLOT_CHUNK, top_ref.shape[1])
        r_i = lax.broadcasted_iota(I32, (c, c), 0)
        c_i = lax.broadcasted_iota(I32, (c, c), 1)
        upper = (r_i < c_i).astype(BF16)
        e_iota = lax.broadcasted_iota(I32, (N_EXPERTS, c), 0)

        def body(ci, carry):
            lo = pl.multiple_of(ci * c, c)
            top = top_ref[:, pl.ds(lo, c)]
            hit = [e_iota == top[k:k + 1, :] for k in range(TOP_K)]
            cnt = sum(m.astype(F32) for m in hit)
            base = carry + jnp.dot(cnt.astype(BF16), upper, preferred_element_type=F32)
            rank_ref[:, pl.ds(lo, c)] = jnp.concatenate(
                [jnp.sum(jnp.where(m, base, 0.0), axis=0, keepdims=True) for m in hit], axis=0)
            return carry + jnp.sum(cnt, axis=1, keepdims=True)
        return top_ref.shape[1] // c, body

    counts = jnp.zeros((N_EXPERTS, 1), F32)
    for top_ref, rank_ref in ((tp_ref, rankp), (ts_ref, ranks)):
        trips, body = make_body(top_ref, rank_ref)
        counts = lax.fori_loop(0, trips, body, counts)

    padded = jnp.floor((counts + (MOE_BLOCK - 1)) / MOE_BLOCK) * MOE_BLOCK
    er = lax.broadcasted_iota(I32, (N_EXPERTS, N_EXPERTS), 0)
    ec = lax.broadcasted_iota(I32, (N_EXPERTS, N_EXPERTS), 1)
    padded_row = jnp.sum(jnp.where(er == ec, padded, 0.0), axis=0, keepdims=True)
    cum = jnp.sum(jnp.where(ec <= er, padded_row, 0.0), axis=1, keepdims=True)
    pstart = cum - padded
    cnt_ref[...] = jnp.sum(jnp.where(er == ec, counts, 0.0), axis=0, keepdims=True).astype(I32)
    pstart_ref[...] = jnp.sum(jnp.where(er == ec, pstart, 0.0), axis=0, keepdims=True).astype(I32)

    def dest(top_ref, rank_ref, out_ref):
        n = top_ref.shape[1]
        ei = lax.broadcasted_iota(I32, (N_EXPERTS, n), 0)
        rows = []
        for k in range(TOP_K):
            start = jnp.sum(jnp.where(ei == top_ref[k:k + 1, :], pstart, 0.0), axis=0, keepdims=True)
            rows.append(start + rank_ref[k:k + 1, :])
        out_ref[...] = jnp.concatenate(rows, axis=0).astype(I32)

    dest(tp_ref, rankp, destp_ref)
    dest(ts_ref, ranks, dests_ref)

    nb = blke_ref.shape[1]
    blk_lo = lax.broadcasted_iota(I32, (N_EXPERTS, nb), 1).astype(F32) * MOE_BLOCK
    blk = jnp.sum((cum <= blk_lo).astype(F32), axis=0, keepdims=True)
    blke_ref[...] = jnp.minimum(blk, N_EXPERTS - 1).astype(I32)
    nact_ref[...] = (cum[N_EXPERTS - 1:N_EXPERTS, :] / MOE_BLOCK).astype(I32)


def _slots(top_p, top_s):
    n_p, n_s = top_p.shape[1], top_s.shape[1]
    nb = _moe_num_blocks((n_p + n_s) * TOP_K)
    return pl.pallas_call(
        _slots_kernel,
        grid=(1,),
        in_specs=[_full((TOP_K, n_p)), _full((TOP_K, n_s))],
        out_specs=[_full((TOP_K, n_p)), _full((TOP_K, n_s)), _full((1, nb)), _full((1, 1)),
                   _full((1, N_EXPERTS)), _full((1, N_EXPERTS))],
        out_shape=[
            jax.ShapeDtypeStruct((TOP_K, n_p), I32),
            jax.ShapeDtypeStruct((TOP_K, n_s), I32),
            jax.ShapeDtypeStruct((1, nb), I32),
            jax.ShapeDtypeStruct((1, 1), I32),
            jax.ShapeDtypeStruct((1, N_EXPERTS), I32),
            jax.ShapeDtypeStruct((1, N_EXPERTS), I32),
        ],
        scratch_shapes=[pltpu.VMEM((TOP_K, n_p), F32), pltpu.VMEM((TOP_K, n_s), F32)],
        compiler_params=_cparams(("arbitrary",)),
        name="moe_slots",
    )(top_p, top_s)


def _for_rows(n_rows, fn):
    def body(i, c):
        for u in range(DMA_UNROLL):
            fn(i * DMA_UNROLL + u)
        return c
    lax.fori_loop(0, n_rows // DMA_UNROLL, body, 0)


def _dispatch_kernel(cnt_ref, pstart_ref, nact_ref, destp_ref, dests_ref, xnp_ref, xns_ref, xs_ref,
                     zeros, sem, zsem):
    i = pl.program_id(0)
    last = pl.num_programs(0) - 1

    def scatter(x_ref, dest_ref):
        n = x_ref.shape[0] // ROW_TILE

        def one(t):
            for k in range(TOP_K):
                _row_tile_copy(x_ref, t, xs_ref, dest_ref[t * TOP_K + k], sem).start(priority=k % 2)

        _for_rows(n, one)
        _row_tile_copy(xs_ref, 0, xs_ref, 0, sem, n * TOP_K).wait()

    @pl.when(i < last)
    def _():
        scatter(xnp_ref, destp_ref)

    @pl.when(i == last)
    def _():
        scatter(xns_ref, dests_ref)
        zeros[...] = jnp.zeros(zeros.shape, F32)

        def pad_copies(e, act):
            c = cnt_ref[e]
            npad = (-c) & (MOE_BLOCK - 1)
            base = pstart_ref[e] + c
            for bit in range(MOE_BLOCK.bit_length() - 1):
                size = 1 << bit

                @pl.when((npad & size) != 0)
                def _():
                    act(_row_tile_copy(zeros, 0, xs_ref, base + (npad & (size - 1)), zsem, size))

        def tail_copies(act):
            def body(j, c):
                act(_row_tile_copy(zeros, 0, xs_ref, j * MOE_BLOCK, zsem, MOE_BLOCK))
                return c
            lax.fori_loop(nact_ref[0], xs_ref.shape[0] // (MOE_BLOCK * ROW_TILE), body, 0)

        for act in (lambda d: d.start(), lambda d: d.wait()):
            lax.fori_loop(0, N_EXPERTS, lambda e, c, act=act: (pad_copies(e, act), c)[1], 0)
            tail_copies(act)


def _dispatch(cnt, pstart, nact, dest_p, dest_s, xn_p, xn_s, n_rows):
    n_p, n_s = xn_p.shape[0] // ROW_TILE, xn_s.shape[0] // ROW_TILE
    tile = DISPATCH_TILE
    nt = n_p // tile
    smem = pl.BlockSpec(memory_space=pltpu.SMEM)
    return pl.pallas_call(
        _dispatch_kernel,
        grid=(nt + 1,),
        in_specs=[
            smem, smem, smem,
            pl.BlockSpec((TOP_K * tile,), lambda i: (jnp.minimum(i, nt - 1),), memory_space=pltpu.SMEM),
            smem,
            pl.BlockSpec((tile * ROW_TILE, LANES), lambda i: (jnp.minimum(i, nt - 1), 0)),
            _full((n_s * ROW_TILE, LANES)),
        ],
        out_specs=pl.BlockSpec(memory_space=pl.ANY),
        out_shape=jax.ShapeDtypeStruct((n_rows * ROW_TILE, LANES), F32),
        scratch_shapes=[pltpu.VMEM((MOE_BLOCK * ROW_TILE, LANES), F32), pltpu.SemaphoreType.DMA,
                        pltpu.SemaphoreType.DMA],
        compiler_params=_cparams(("arbitrary",)),
        name="moe_dispatch",
    )(cnt, pstart, nact, dest_p, dest_s, xn_p, xn_s)


def _combine_kernel(dest_ref, dest_next_ref, gates_ref, x2_ref, gfin_ref, ys_ref, y_ref, bufs, sems):
    tile = x2_ref.shape[0]
    i = pl.program_id(0)
    slot = i % 2

    def gather(d_ref, b):
        def one(t):
            for k in range(TOP_K):
                _row_tile_copy(ys_ref, d_ref[t * TOP_K + k], bufs.at[b, k], t, sems.at[b]).start(priority=k % 2)
        _for_rows(tile, one)

    @pl.when(i == 0)
    def _():
        gather(dest_ref, 0)

    for b in range(2):
        @pl.when(slot == b)
        def _():
            pltpu.make_async_copy(bufs.at[b], bufs.at[b], sems.at[b]).wait()

            @pl.when(i + 1 < pl.num_programs(0))
            def _():
                gather(dest_next_ref, 1 - b)

    buf = bufs.at[slot]
    sumsq = jnp.zeros((tile, 1), F32)
    gate = [jnp.broadcast_to(gates_ref[:, k:k + 1], (tile, LANES)) for k in range(TOP_K)]
    for s in range(ROW_TILE):
        cols = slice(s * LANES, (s + 1) * LANES)
        acc = x2_ref[:, cols]
        for k in range(TOP_K):
            acc = acc + gate[k] * buf[k, pl.ds(s, tile, stride=ROW_TILE), :]
        y_ref[:, cols] = acc
        sumsq = sumsq + jnp.sum(acc * acc, axis=-1, keepdims=True)
    scale = lax.rsqrt(sumsq / D_MODEL + EPS)
    for s in range(ROW_TILE):
        cols = slice(s * LANES, (s + 1) * LANES)
        y_ref[:, cols] = y_ref[:, cols] * scale * gfin_ref[:, cols]


def _combine(dest, gates_t, x2, g_final, ys):
    n = x2.shape[0]
    tile = min(COMBINE_TILE, n)
    row = pl.BlockSpec((tile, D_MODEL), lambda i: (i, 0))
    nt = n // tile
    return pl.pallas_call(
        _combine_kernel,
        grid=(nt,),
        in_specs=[
            pl.BlockSpec((TOP_K * tile,), lambda i: (i,), memory_space=pltpu.SMEM),
            pl.BlockSpec((TOP_K * tile,), lambda i: (jnp.minimum(i + 1, nt - 1),), memory_space=pltpu.SMEM),
            pl.BlockSpec((tile, TOP_K), lambda i: (i, 0)),
            row,
            _full((1, D_MODEL)),
            pl.BlockSpec(memory_space=pl.ANY),
        ],
        out_specs=row,
        out_shape=jax.ShapeDtypeStruct((n, D_MODEL), F32),
        scratch_shapes=[pltpu.VMEM((2, TOP_K, tile * ROW_TILE, LANES), F32), pltpu.SemaphoreType.DMA((2,))],
        compiler_params=_cparams(("arbitrary",)),
        name="moe_combine",
    )(dest, dest, gates_t, x2, g_final, ys)


def _experts_kernel(blke_ref, nact_ref, xs_ref, w1_hbm, b1a_ref, b1b_ref, w2_hbm, b2a_ref, b2b_ref, ys_ref,
                    w1f, w2f, w1b, w2b, wsem, started):
    j = pl.program_id(0)
    nact = nact_ref[0]
    last = blke_ref.shape[0] - 1
    first, second = 2 * j, 2 * j + 1
    e0, e1 = blke_ref[first], blke_ref[second]
    active0, active1 = first < nact, second < nact
    together = jnp.logical_and(active1, e1 == e0)
    half = MOE_BLOCK * ROW_TILE

    def fetch(expert, slot):
        return (pltpu.make_async_copy(w1_hbm.at[expert], w1f.at[slot], wsem.at[slot]),
                pltpu.make_async_copy(w2_hbm.at[expert], w2f.at[slot], wsem.at[slot]))

    def switch_to(blk):
        e = blke_ref[blk]
        n = started[0]
        jn = lax.while_loop(lambda jj: jnp.logical_and(jj < nact, blke_ref[jnp.minimum(jj, last)] == e),
                            lambda jj: jj + 1, blk + 1)
        for b in range(2):
            @pl.when(n % 2 == b)
            def _():
                for d in fetch(e, b):
                    d.wait()

                @pl.when(jn < nact)
                def _():
                    for d in fetch(blke_ref[jnp.minimum(jn, last)], 1 - b):
                        d.start(priority=1)

                w1b[...] = w1f[b].astype(BF16)
                w2b[...] = w2f[b].astype(BF16)
        started[0] = n + 1

    def ffn(x_ref, n_rows, b1_ref, b2_ref, y_ref):
        x = _load_row_tiles(x_ref, n_rows).astype(BF16)
        gu = jnp.dot(x, w1b[...], preferred_element_type=F32) + b1_ref[0]
        g = jnp.minimum(gu[:, :D_EXPERT], SWIGLU_LIMIT)
        u = jnp.clip(gu[:, D_EXPERT:], -SWIGLU_LIMIT, SWIGLU_LIMIT)
        h = (u + 1.0) * (g * jax.nn.sigmoid(SWIGLU_ALPHA * g))
        _store_row_tiles(y_ref, _bdot(h, w2b[...]) + b2_ref[0])

    @pl.when(j == 0)
    def _():
        started[0] = 0
        for d in fetch(e0, 0):
            d.start()

    @pl.when(jnp.logical_and(active0, jnp.logical_or(j == 0, e0 != blke_ref[jnp.maximum(first - 1, 0)])))
    def _():
        switch_to(first)

    @pl.when(together)
    def _():
        ffn(xs_ref, 2 * MOE_BLOCK, b1a_ref, b2a_ref, ys_ref)

    @pl.when(jnp.logical_and(active0, jnp.logical_not(together)))
    def _():
        ffn(xs_ref.at[pl.ds(0, half)], MOE_BLOCK, b1a_ref, b2a_ref, ys_ref.at[pl.ds(0, half)])

        @pl.when(active1)
        def _():
            switch_to(second)
            ffn(xs_ref.at[pl.ds(half, half)], MOE_BLOCK, b1b_ref, b2b_ref, ys_ref.at[pl.ds(half, half)])

        @pl.when(jnp.logical_not(active1))
        def _():
            ys_ref[pl.ds(half, half), :] = jnp.zeros((half, LANES), F32)

    @pl.when(jnp.logical_not(active0))
    def _():
        ys_ref[...] = jnp.zeros(ys_ref.shape, F32)


def _experts(blk_e, nact, xs, w1, b1, w2, b2):
    nb = xs.shape[0] // (MOE_BLOCK * ROW_TILE)
    assert nb % 2 == 0
    blk = lambda j, be, na: (jnp.minimum(j, (na[0] - 1) // 2), 0)
    exp = lambda h: (lambda j, be, na: (be[jnp.minimum(2 * j + h, na[0] - 1)], 0, 0))
    rows = (2 * MOE_BLOCK * ROW_TILE, LANES)
    return pl.pallas_call(
        _experts_kernel,
        grid_spec=pltpu.PrefetchScalarGridSpec(
            num_scalar_prefetch=2,
            grid=(nb // 2,),
            in_specs=[
                pl.BlockSpec(rows, blk),
                pl.BlockSpec(memory_space=pl.ANY),
                pl.BlockSpec((1, 1, 2 * D_EXPERT), exp(0)),
                pl.BlockSpec((1, 1, 2 * D_EXPERT), exp(1)),
                pl.BlockSpec(memory_space=pl.ANY),
                pl.BlockSpec((1, 1, D_MODEL), exp(0)),
                pl.BlockSpec((1, 1, D_MODEL), exp(1)),
            ],
            out_specs=pl.BlockSpec(rows, lambda j, be, na: (j, 0)),
            scratch_shapes=[
                pltpu.VMEM((2, D_MODEL, 2 * D_EXPERT), F32), pltpu.VMEM((2, D_EXPERT, D_MODEL), F32),
                pltpu.VMEM((D_MODEL, 2 * D_EXPERT), BF16), pltpu.VMEM((D_EXPERT, D_MODEL), BF16),
                pltpu.SemaphoreType.DMA((2,)), pltpu.SMEM((1,), I32),
            ],
        ),
        out_shape=jax.ShapeDtypeStruct(xs.shape, F32),
        compiler_params=_cparams(("arbitrary",)),
        name="moe_experts",
    )(blk_e, nact, xs, w1, b1, b1, w2, b2, b2)


def _block_diag(w):
    h, d, _ = w.shape
    eye = jnp.eye(h, dtype=w.dtype)
    return (eye[:, None, :, None] * w[:, :, None, :]).reshape(h * d, h * d)


def kernel(x_prompt, x_sample, state_lru_h, cache_lru_conv, cache_cf_conv, cache_mem_k, cache_mem_v,
           mem_prompt, g_mix, w_in, w_lru_conv, b_lru_conv, w_lru_a, b_lru_a, w_lru_x, b_lru_x,
           lru_lambda, w_cf_conv, b_cf_conv, g_cf_ln, b_cf_ln, w_out, g_xa, g_mem, w_q, w_k, w_v, w_o,
           g_moe, w_router, b_router, w_e1, b_e1, w_e2, b_e2, g_final):
    depth = g_mix.shape[0]
    assert depth == 1, "single-layer stack"
    batch, seq, _ = x_prompt.shape
    n_s = x_sample.shape[0]
    assert x_sample.shape[1] == 1
    n_p = batch * seq
    l = 0

    mix_w = (
        g_mix[l][None], w_in[l].astype(BF16), w_lru_conv[l], b_lru_conv[l][None],
        jnp.concatenate([_block_diag(w_lru_a[l]), _block_diag(w_lru_x[l])], axis=1).astype(BF16),
        jnp.concatenate([b_lru_a[l], b_lru_x[l]])[None], lru_lambda[l][None],
        w_cf_conv[l], b_cf_conv[l][None], g_cf_ln[l][None], b_cf_ln[l][None], w_out[l].astype(BF16),
    )
    router_w = (g_moe[l][None], w_router[l].T.astype(BF16), b_router[l][:, None])
    wq, wo = w_q[l].astype(BF16), w_o[l].astype(BF16)
    gxa = g_xa[l][None]

    xs_ = x_sample.reshape(n_s, D_MODEL)
    x1p, hp, lcp, ccp = _mixer_prompt(x_prompt, mix_w)
    x1p = x1p.reshape(n_p, D_MODEL)
    x1s, hs, lru_conv_s, cf_conv_s = _mixer_sample(xs_, mix_w, state_lru_h[l], cache_lru_conv[l], cache_cf_conv[l])

    mk, mv, mkb, mvb = _memkv(mem_prompt.reshape(batch * N_MEM, D_MODEL), g_mem[l][None],
                              w_k[l].astype(BF16), w_v[l].astype(BF16))
    x2p, xnp_, top_p, gates_p = _attn_prompt(x1p, mkb, mvb, gxa, wq, wo, router_w, batch, seq)
    x2s, xns, top_s, gates_s = _attn_sample(x1s, cache_mem_k[l], cache_mem_v[l], gxa, wq, wo, router_w)

    dest_p, dest_s, blk_e, nact, cnt, pstart = _slots(top_p, top_s)
    dest_p, dest_s = dest_p.T.reshape(-1), dest_s.T.reshape(-1)
    nb = _moe_num_blocks((n_p + n_s) * TOP_K)
    nact = nact.reshape(1)
    xs = _dispatch(cnt.reshape(N_EXPERTS), pstart.reshape(N_EXPERTS), nact, dest_p, dest_s, xnp_, xns,
                   nb * MOE_BLOCK)
    ys = _experts(blk_e.reshape(nb), nact, xs, w_e1[l], b_e1[l][:, None, :], w_e2[l], b_e2[l][:, None, :])
    gfin = g_final[None]
    y_p = _combine(dest_p, gates_p.T, x2p, gfin, ys)
    y_s = _combine(dest_s, gates_s.T, x2s, gfin, ys)

    return (
        y_p.reshape(batch, seq, D_MODEL),
        y_s.reshape(n_s, 1, D_MODEL),
        hp[None],
        lcp[None],
        ccp[None],
        mk[None],
        mv[None],
        hs[None],
        lru_conv_s[None],
        cf_conv_s[None],
    )
```

```python
import jax
import jax.numpy as jnp
from jax import lax
from jax.experimental import pallas as pl
from jax.experimental.pallas import tpu as pltpu

F32 = jnp.float32
BF16 = jnp.bfloat16
I32 = jnp.int32

D_MODEL = 1024
LRU_WIDTH = 512
CF_WIDTH = 512
LRU_CONV = 4
LRU_C = 8.0
CF_CONV = 31
IN_WIDTH = 2 * LRU_WIDTH + 2 * CF_WIDTH
N_MEM = 256
XA_HEADS = 4
XA_HEAD_DIM = D_MODEL // XA_HEADS
N_EXPERTS = 32
TOP_K = 4
D_EXPERT = D_MODEL
SWIGLU_LIMIT = 7.0
SWIGLU_ALPHA = 1.702
EPS = 1e-6

LANES = 128
SUBLANES = 8
ROW_TILE = D_MODEL // LANES
assert ROW_TILE == SUBLANES
VMEM_LIMIT = 56 * 1024 * 1024

MIX_STEPS = 128
ATT_TILE = 1024
SAMPLE_GROUP = 4
SAMPLE_MIX_GROUP = 32
MOE_BLOCK = 256
EXPERT_STEP_BLOCKS = 4
SLOT_CHUNK = 256
DMA_UNROLL = 16
DISPATCH_TILE = 4096
COMBINE_TILE = 256


def _cparams(sem):
    return pltpu.CompilerParams(dimension_semantics=sem, vmem_limit_bytes=VMEM_LIMIT)


def _full(shape):
    n = len(shape)
    return pl.BlockSpec(shape, lambda *_: (0,) * n)


def _rms(x, g):
    return x * lax.rsqrt(jnp.mean(x * x, axis=-1, keepdims=True) + EPS) * g


def _bdot(a, b):
    return jnp.dot(a.astype(BF16), b, preferred_element_type=F32)


def _store_row_tiles(ref, x):
    m = x.shape[0]
    for s in range(ROW_TILE):
        ref[pl.ds(s, m, stride=ROW_TILE), :] = x[:, s * LANES:(s + 1) * LANES]


def _load_row_tiles(ref, m):
    return jnp.concatenate([ref[pl.ds(s, m, stride=ROW_TILE), :] for s in range(ROW_TILE)], axis=-1)


def _row_tile_copy(src_ref, src_row, dst_ref, dst_row, sem, n_rows=1):
    src = src_ref.at[pl.ds(pl.multiple_of(src_row * ROW_TILE, ROW_TILE), n_rows * ROW_TILE)]
    dst = dst_ref.at[pl.ds(pl.multiple_of(dst_row * ROW_TILE, ROW_TILE), n_rows * ROW_TILE)]
    return pltpu.make_async_copy(src, dst, sem)


def _lru_coeffs(xc, wg_ref, bg_ref, lam_ref):
    gl = _bdot(xc, wg_ref[...]) + bg_ref[...]
    r = jax.nn.sigmoid(gl[:, :LRU_WIDTH])
    i = jax.nn.sigmoid(gl[:, LRU_WIDTH:])
    log_a = -LRU_C * r * jax.nn.softplus(-lam_ref[...])
    a = jnp.exp(log_a)
    u = jnp.sqrt(-jnp.tanh(log_a) * (1.0 + a * a)) * (i * xc)
    return a, u


def _cf_post(gc, gln_ref, bln_ref):
    mu = jnp.mean(gc, axis=-1, keepdims=True)
    c = gc - mu
    y = c * lax.rsqrt(jnp.mean(c * c, axis=-1, keepdims=True) + EPS)
    return jax.nn.silu(y * gln_ref[...] + bln_ref[...])


def _mixer_out(x, y_lru, y_cf, wout_ref):
    y = _bdot(y_lru, wout_ref[:LRU_WIDTH, :]) + _bdot(y_cf, wout_ref[LRU_WIDTH:, :])
    return x + y


def _mixer_prompt_kernel(x_hbm, gmix_ref, win_ref, wlc_ref, blc_ref, wg_ref, bg_ref, lam_ref,
                         wcc_ref, bcc_ref, gln_ref, bln_ref, wout_ref,
                         x1_hbm, h_ref, ltail_ref, ctail_ref,
                         xbuf, obuf, zx_ext, glu_ext, h_carry, isem, osem):
    ts = MIX_STEPS
    nb = xbuf.shape[2]
    rows = ts * nb
    j = pl.program_id(0)
    nj = pl.num_programs(0)
    hl = (LRU_CONV - 1) * nb
    hc = (CF_CONV - 1) * nb

    def copies_in(step, s):
        return [pltpu.make_async_copy(x_hbm.at[b, pl.ds(step * ts, ts)], xbuf.at[s, pl.ds(0, ts), b], isem.at[s])
                for b in range(nb)]

    def copies_out(step, s):
        return [pltpu.make_async_copy(obuf.at[s, pl.ds(0, ts), b], x1_hbm.at[b, pl.ds(step * ts, ts)], osem.at[s])
                for b in range(nb)]

    @pl.when(j == 0)
    def _():
        zx_ext[0:hl, :] = jnp.zeros((hl, LRU_WIDTH), F32)
        glu_ext[0:hc, :] = jnp.zeros((hc, CF_WIDTH), F32)
        h_carry[...] = jnp.zeros(h_carry.shape, F32)
        for cp in copies_in(0, 0):
            cp.start()

    for s in range(2):
        @pl.when(j % 2 == s)
        def _():
            for cp in copies_in(j, s):
                cp.wait()

            @pl.when(j + 1 < nj)
            def _():
                for cp in copies_in(j + 1, 1 - s):
                    cp.start()

            @pl.when(j >= 2)
            def _():
                for cp in copies_out(j - 2, s):
                    cp.wait()

    x = xbuf[j % 2].reshape(rows, D_MODEL)
    z = _bdot(_rms(x, gmix_ref[...]), win_ref[...])
    zx = z[:, :LRU_WIDTH]
    zg = z[:, LRU_WIDTH:2 * LRU_WIDTH]
    za = z[:, 2 * LRU_WIDTH:2 * LRU_WIDTH + CF_WIDTH]
    zb = z[:, 2 * LRU_WIDTH + CF_WIDTH:]

    zx_ext[hl:hl + rows, :] = zx
    xc = blc_ref[...]
    for k in range(LRU_CONV):
        xc = xc + wlc_ref[k:k + 1, :] * zx_ext[k * nb:k * nb + rows, :]
    a, u = _lru_coeffs(xc, wg_ref, bg_ref, lam_ref)
    h = h_carry[...]
    steps = []
    for t in range(ts):
        h = a[t * nb:(t + 1) * nb, :] * h + u[t * nb:(t + 1) * nb, :]
        steps.append(h)
    h_carry[...] = h
    y_lru = jnp.concatenate(steps, axis=0) * jax.nn.gelu(zg)

    glu = za * jax.nn.sigmoid(zb)
    glu_ext[hc:hc + rows, :] = glu
    gc = bcc_ref[...]
    for k in range(CF_CONV):
        gc = gc + wcc_ref[k:k + 1, :] * glu_ext[k * nb:k * nb + rows, :]
    y_cf = _cf_post(gc, gln_ref, bln_ref)

    x1 = _mixer_out(x, y_lru, y_cf, wout_ref)
    for s in range(2):
        @pl.when(j % 2 == s)
        def _():
            obuf[s] = x1.reshape(ts, nb, D_MODEL)
            for cp in copies_out(j, s):
                cp.start()

            @pl.when(j == nj - 1)
            def _():
                for cp in copies_out(j, s):
                    cp.wait()

                @pl.when(j >= 1)
                def _():
                    for cp in copies_out(j - 1, 1 - s):
                        cp.wait()

    zx_ext[0:hl, :] = zx_ext[rows:rows + hl, :]
    glu_ext[0:hc, :] = glu_ext[rows:rows + hc, :]

    @pl.when(j == nj - 1)
    def _():
        h_ref[...] = h
        for b in range(nb):
            for r in range(LRU_CONV - 1):
                ltail_ref[b, r:r + 1, :] = zx_ext[r * nb + b:r * nb + b + 1, :]
            for r in range(CF_CONV - 1):
                ctail_ref[b, r:r + 1, :] = glu_ext[r * nb + b:r * nb + b + 1, :]


def _mixer_weight_specs():
    return [
        _full((1, D_MODEL)),
        _full((D_MODEL, IN_WIDTH)),
        _full((LRU_CONV, LRU_WIDTH)), _full((1, LRU_WIDTH)),
        _full((LRU_WIDTH, 2 * LRU_WIDTH)), _full((1, 2 * LRU_WIDTH)),
        _full((1, LRU_WIDTH)),
        _full((CF_CONV, CF_WIDTH)), _full((1, CF_WIDTH)),
        _full((1, CF_WIDTH)), _full((1, CF_WIDTH)),
        _full((D_MODEL, D_MODEL)),
    ]


def _mixer_prompt(x, mix_w):
    batch, seq, _ = x.shape
    ts = MIX_STEPS
    any_ = pl.BlockSpec(memory_space=pl.ANY)
    return pl.pallas_call(
        _mixer_prompt_kernel,
        grid=(seq // ts,),
        in_specs=[any_] + _mixer_weight_specs(),
        out_specs=[
            any_,
            _full((batch, LRU_WIDTH)),
            _full((batch, LRU_CONV - 1, LRU_WIDTH)),
            _full((batch, CF_CONV - 1, CF_WIDTH)),
        ],
        out_shape=[
            jax.ShapeDtypeStruct((batch, seq, D_MODEL), F32),
            jax.ShapeDtypeStruct((batch, LRU_WIDTH), F32),
            jax.ShapeDtypeStruct((batch, LRU_CONV - 1, LRU_WIDTH), F32),
            jax.ShapeDtypeStruct((batch, CF_CONV - 1, CF_WIDTH), F32),
        ],
        scratch_shapes=[
            pltpu.VMEM((2, ts, batch, D_MODEL), F32),
            pltpu.VMEM((2, ts, batch, D_MODEL), F32),
            pltpu.VMEM(((LRU_CONV - 1 + ts) * batch, LRU_WIDTH), F32),
            pltpu.VMEM(((CF_CONV - 1 + ts) * batch, CF_WIDTH), F32),
            pltpu.VMEM((batch, LRU_WIDTH), F32),
            pltpu.SemaphoreType.DMA((2,)), pltpu.SemaphoreType.DMA((2,)),
        ],
        compiler_params=_cparams(("arbitrary",)),
        name="mixer_prompt",
    )(x, *mix_w)


def _mixer_sample_kernel(x_ref, gmix_ref, win_ref, wlc_ref, blc_ref, wg_ref, bg_ref, lam_ref,
                         wcc_ref, bcc_ref, gln_ref, bln_ref, wout_ref,
                         h0_ref, lbuf_ref, cbuf_ref,
                         x1_ref, h_ref, ltail_ref, ctail_ref):
    x = x_ref[...]
    z = _bdot(_rms(x, gmix_ref[...]), win_ref[...])
    zx = z[:, :LRU_WIDTH]
    zg = z[:, LRU_WIDTH:2 * LRU_WIDTH]
    za = z[:, 2 * LRU_WIDTH:2 * LRU_WIDTH + CF_WIDTH]
    zb = z[:, 2 * LRU_WIDTH + CF_WIDTH:]

    xc = blc_ref[...] + wlc_ref[LRU_CONV - 1:LRU_CONV, :] * zx
    xc = xc + jnp.sum(lbuf_ref[...] * wlc_ref[0:LRU_CONV - 1, :][None], axis=1)
    a, u = _lru_coeffs(xc, wg_ref, bg_ref, lam_ref)
    h = a * h0_ref[...] + u
    y_lru = h * jax.nn.gelu(zg)

    glu = za * jax.nn.sigmoid(zb)
    gc = bcc_ref[...] + wcc_ref[CF_CONV - 1:CF_CONV, :] * glu
    gc = gc + jnp.sum(cbuf_ref[...] * wcc_ref[0:CF_CONV - 1, :][None], axis=1)
    y_cf = _cf_post(gc, gln_ref, bln_ref)

    x1_ref[...] = _mixer_out(x, y_lru, y_cf, wout_ref)
    h_ref[...] = h
    ltail_ref[:, 0:LRU_CONV - 2, :] = lbuf_ref[:, 1:LRU_CONV - 1, :]
    ctail_ref[:, 0:CF_CONV - 2, :] = cbuf_ref[:, 1:CF_CONV - 1, :]
    for i in range(x.shape[0]):
        ltail_ref[i, LRU_CONV - 2:LRU_CONV - 1, :] = zx[i:i + 1, :]
        ctail_ref[i, CF_CONV - 2:CF_CONV - 1, :] = glu[i:i + 1, :]


def _mixer_sample(x, mix_w, h0, lbuf, cbuf):
    n = x.shape[0]
    g = SAMPLE_MIX_GROUP
    row = lambda w: pl.BlockSpec((g, w), lambda i: (i, 0))
    return pl.pallas_call(
        _mixer_sample_kernel,
        grid=(n // g,),
        in_specs=[row(D_MODEL)] + _mixer_weight_specs() + [
            row(LRU_WIDTH),
            pl.BlockSpec((g, LRU_CONV - 1, LRU_WIDTH), lambda i: (i, 0, 0)),
            pl.BlockSpec((g, CF_CONV - 1, CF_WIDTH), lambda i: (i, 0, 0)),
        ],
        out_specs=[row(D_MODEL), row(LRU_WIDTH),
                   pl.BlockSpec((g, LRU_CONV - 1, LRU_WIDTH), lambda i: (i, 0, 0)),
                   pl.BlockSpec((g, CF_CONV - 1, CF_WIDTH), lambda i: (i, 0, 0))],
        out_shape=[
            jax.ShapeDtypeStruct((n, D_MODEL), F32),
            jax.ShapeDtypeStruct((n, LRU_WIDTH), F32),
            jax.ShapeDtypeStruct((n, LRU_CONV - 1, LRU_WIDTH), F32),
            jax.ShapeDtypeStruct((n, CF_CONV - 1, CF_WIDTH), F32),
        ],
        compiler_params=_cparams(("arbitrary",)),
        name="mixer_sample",
    )(x, *mix_w, h0, lbuf, cbuf)


def _memkv_kernel(mem_ref, gmem_ref, wk_ref, wv_ref, k_ref, v_ref, kb_ref, vb_ref):
    mn = _rms(mem_ref[...], gmem_ref[...]).astype(BF16)
    k = jnp.dot(mn, wk_ref[...], preferred_element_type=F32)
    v = jnp.dot(mn, wv_ref[...], preferred_element_type=F32)
    for h in range(XA_HEADS):
        sl = slice(h * XA_HEAD_DIM, (h + 1) * XA_HEAD_DIM)
        k_ref[0, :, h, :] = k[:, sl]
        v_ref[0, :, h, :] = v[:, sl]
    kb_ref[...] = k.astype(BF16)
    vb_ref[...] = v.astype(BF16)


def _memkv(mem, g_mem, wk, wv):
    n = mem.shape[0]
    t = N_MEM
    row = pl.BlockSpec((t, D_MODEL), lambda i: (i, 0))
    state = pl.BlockSpec((1, N_MEM, XA_HEADS, XA_HEAD_DIM), lambda i: (i, 0, 0, 0))
    state_shape = jax.ShapeDtypeStruct((n // t, N_MEM, XA_HEADS, XA_HEAD_DIM), F32)
    return pl.pallas_call(
        _memkv_kernel,
        grid=(n // t,),
        in_specs=[row, _full((1, D_MODEL)), _full((D_MODEL, D_MODEL)), _full((D_MODEL, D_MODEL))],
        out_specs=[state, state, row, row],
        out_shape=[state_shape] * 2 + [jax.ShapeDtypeStruct((n, D_MODEL), BF16)] * 2,
        compiler_params=_cparams(("arbitrary",)),
        name="memory_kv",
    )(mem, g_mem, wk, wv)


def _router(x2, gmoe_ref, wrt_ref, brt_ref, xn_ref, tope_ref, gates_ref):
    xn = _rms(x2, gmoe_ref[...])
    _store_row_tiles(xn_ref, xn)
    logits = lax.dot_general(wrt_ref[...], xn.astype(BF16), (((1,), (1,)), ((), ())),
                             preferred_element_type=F32) + brt_ref[...]
    e_iota = lax.broadcasted_iota(I32, logits.shape, 0)
    work = logits
    vals, idxs = [], []
    for _ in range(TOP_K):
        m = jnp.max(work, axis=0, keepdims=True)
        idx = jnp.min(jnp.where(work == m, e_iota, N_EXPERTS), axis=0, keepdims=True)
        vals.append(m)
        idxs.append(idx)
        work = jnp.where(e_iota == idx, -jnp.inf, work)
    ex = [jnp.exp(v - vals[0]) for v in vals]
    den = ex[0] + ex[1] + ex[2] + ex[3]
    tope_ref[...] = jnp.concatenate(idxs, axis=0)
    gates_ref[...] = jnp.concatenate([e / den for e in ex], axis=0)


def _router_specs():
    return [_full((1, D_MODEL)), _full((N_EXPERTS, D_MODEL)), _full((N_EXPERTS, 1))]


def _router_out(n, tile, index):
    specs = [
        pl.BlockSpec((tile, D_MODEL), lambda *a: (index(*a), 0)),
        pl.BlockSpec((tile * ROW_TILE, LANES), lambda *a: (index(*a), 0)),
        pl.BlockSpec((TOP_K, tile), lambda *a: (0, index(*a))),
        pl.BlockSpec((TOP_K, tile), lambda *a: (0, index(*a))),
    ]
    shapes = [
        jax.ShapeDtypeStruct((n, D_MODEL), F32),
        jax.ShapeDtypeStruct((n * ROW_TILE, LANES), F32),
        jax.ShapeDtypeStruct((TOP_K, n), I32),
        jax.ShapeDtypeStruct((TOP_K, n), F32),
    ]
    return specs, shapes


def _attn_prompt_kernel(x1_ref, k_ref, v_ref, gxa_ref, wq_ref, wo_ref, gmoe_ref, wrt_ref, brt_ref,
                        x2_ref, xn_ref, tope_ref, gates_ref):
    x1 = x1_ref[...]
    q = (_bdot(_rms(x1, gxa_ref[...]), wq_ref[...]) * (XA_HEAD_DIM ** -0.5)).astype(BF16)
    outs = []
    for h in range(XA_HEADS):
        sl = slice(h * XA_HEAD_DIM, (h + 1) * XA_HEAD_DIM)
        s = lax.dot_general(q[:, sl], k_ref[:, sl], (((1,), (1,)), ((), ())), preferred_element_type=F32)
        p = jnp.exp(s - jnp.max(s, axis=-1, keepdims=True))
        p = p / jnp.sum(p, axis=-1, keepdims=True)
        outs.append(_bdot(p, v_ref[:, sl]))
    o = jnp.concatenate(outs, axis=-1)
    x2 = x1 + _bdot(o, wo_ref[...])
    x2_ref[...] = x2
    _router(x2, gmoe_ref, wrt_ref, brt_ref, xn_ref, tope_ref, gates_ref)


def _attn_prompt(x1, k, v, gxa, wq, wo, router_w, batch, seq):
    tq = ATT_TILE
    nt = seq // tq
    out_specs, out_shapes = _router_out(batch * seq, tq, lambda b, j: b * nt + j)
    kv = pl.BlockSpec((N_MEM, D_MODEL), lambda b, j: (b, 0))
    return pl.pallas_call(
        _attn_prompt_kernel,
        grid=(batch, nt),
        in_specs=[pl.BlockSpec((tq, D_MODEL), lambda b, j: (b * nt + j, 0)), kv, kv,
                  _full((1, D_MODEL)), _full((D_MODEL, D_MODEL)), _full((D_MODEL, D_MODEL))] + _router_specs(),
        out_specs=out_specs,
        out_shape=out_shapes,
        compiler_params=_cparams(("arbitrary", "arbitrary")),
        name="attn_prompt",
    )(x1, k, v, gxa, wq, wo, *router_w)


def _qproj_kernel(x1_ref, gxa_ref, wq_ref, q_ref):
    q_ref[...] = _bdot(_rms(x1_ref[...], gxa_ref[...]), wq_ref[...]) * (XA_HEAD_DIM ** -0.5)


def _attn_sample_core_kernel(q_ref, k_ref, v_ref, o_ref):
    n = N_MEM * SUBLANES
    col_head = lax.broadcasted_iota(I32, (SUBLANES, n), 1) & (SUBLANES - 1)
    same_head = col_head == lax.broadcasted_iota(I32, (SUBLANES, n), 0)
    kv_pad = jnp.zeros((N_MEM, SUBLANES - XA_HEADS, XA_HEAD_DIM), F32)
    q_pad = jnp.zeros((SUBLANES - XA_HEADS, XA_HEAD_DIM), F32)
    for g in range(q_ref.shape[0]):
        k8 = jnp.concatenate([k_ref[g], kv_pad], axis=1).reshape(n, XA_HEAD_DIM).astype(BF16)
        v8 = jnp.concatenate([v_ref[g], kv_pad], axis=1).reshape(n, XA_HEAD_DIM).astype(BF16)
        q8 = jnp.concatenate([q_ref[g], q_pad], axis=0).astype(BF16)
        s = lax.dot_general(q8, k8, (((1,), (1,)), ((), ())), preferred_element_type=F32)
        m = jnp.max(jnp.where(same_head, s, -jnp.inf), axis=-1, keepdims=True)
        p = jnp.where(same_head, jnp.exp(s - m), 0.0)
        den = jnp.sum(p, axis=-1, keepdims=True)
        o8 = jnp.dot(p.astype(BF16), v8, preferred_element_type=F32)
        o_ref[g] = (o8 / den)[0:XA_HEADS, :]


def _oproj_router_kernel(x1_ref, o_ref, wo_ref, gmoe_ref, wrt_ref, brt_ref,
                         x2_ref, xn_ref, tope_ref, gates_ref):
    x2 = x1_ref[...] + _bdot(o_ref[...], wo_ref[...])
    x2_ref[...] = x2
    _router(x2, gmoe_ref, wrt_ref, brt_ref, xn_ref, tope_ref, gates_ref)


def _attn_sample(x1, k, v, gxa, wq, wo, router_w):
    n = x1.shape[0]
    g = SAMPLE_GROUP
    q = pl.pallas_call(
        _qproj_kernel,
        grid=(1,),
        in_specs=[_full((n, D_MODEL)), _full((1, D_MODEL)), _full((D_MODEL, D_MODEL))],
        out_specs=_full((n, D_MODEL)),
        out_shape=jax.ShapeDtypeStruct((n, D_MODEL), F32),
        compiler_params=_cparams(("arbitrary",)),
        name="qproj_sample",
    )(x1, gxa, wq)
    kv = pl.BlockSpec((g, N_MEM, XA_HEADS, XA_HEAD_DIM), lambda i: (i, 0, 0, 0))
    row = pl.BlockSpec((g, XA_HEADS, XA_HEAD_DIM), lambda i: (i, 0, 0))
    o = pl.pallas_call(
        _attn_sample_core_kernel,
        grid=(n // g,),
        in_specs=[row, kv, kv],
        out_specs=row,
        out_shape=jax.ShapeDtypeStruct((n, XA_HEADS, XA_HEAD_DIM), F32),
        compiler_params=_cparams(("arbitrary",)),
        name="attn_sample_core",
    )(q.reshape(n, XA_HEADS, XA_HEAD_DIM), k, v).reshape(n, D_MODEL)
    out_specs, out_shapes = _router_out(n, n, lambda i: 0)
    return pl.pallas_call(
        _oproj_router_kernel,
        grid=(1,),
        in_specs=[_full((n, D_MODEL)), _full((n, D_MODEL)), _full((D_MODEL, D_MODEL))] + _router_specs(),
        out_specs=out_specs,
        out_shape=out_shapes,
        compiler_params=_cparams(("arbitrary",)),
        name="oproj_router_sample",
    )(x1, o, wo, *router_w)


def _moe_num_blocks(n_assign):
    nb = pl.cdiv(n_assign, MOE_BLOCK) + N_EXPERTS
    return pl.cdiv(nb, EXPERT_STEP_BLOCKS) * EXPERT_STEP_BLOCKS


def _slots_kernel(tp_ref, ts_ref, destp_ref, dests_ref, blke_ref, nact_ref, cnt_ref, pstart_ref,
                  rankp, ranks):
    def make_body(top_ref, rank_ref):
        c = min(SLOT_CHUNK, top_ref.shape[1])
        r_i = lax.broadcasted_iota(I32, (c, c), 0)
        c_i = lax.broadcasted_iota(I32, (c, c), 1)
        upper = (r_i < c_i).astype(BF16)
        e_iota = lax.broadcasted_iota(I32, (N_EXPERTS, c), 0)

        def body(ci, carry):
            lo = pl.multiple_of(ci * c, c)
            top = top_ref[:, pl.ds(lo, c)]
            hit = [e_iota == top[k:k + 1, :] for k in range(TOP_K)]
            cnt = sum(m.astype(F32) for m in hit)
            base = carry + jnp.dot(cnt.astype(BF16), upper, preferred_element_type=F32)
            rank_ref[:, pl.ds(lo, c)] = jnp.concatenate(
                [jnp.sum(jnp.where(m, base, 0.0), axis=0, keepdims=True) for m in hit], axis=0)
            return carry + jnp.sum(cnt, axis=1, keepdims=True)
        return top_ref.shape[1] // c, body

    counts = jnp.zeros((N_EXPERTS, 1), F32)
    for top_ref, rank_ref in ((tp_ref, rankp), (ts_ref, ranks)):
        trips, body = make_body(top_ref, rank_ref)
        counts = lax.fori_loop(0, trips, body, counts)

    padded = jnp.floor((counts + (MOE_BLOCK - 1)) / MOE_BLOCK) * MOE_BLOCK
    er = lax.broadcasted_iota(I32, (N_EXPERTS, N_EXPERTS), 0)
    ec = lax.broadcasted_iota(I32, (N_EXPERTS, N_EXPERTS), 1)
    padded_row = jnp.sum(jnp.where(er == ec, padded, 0.0), axis=0, keepdims=True)
    cum = jnp.sum(jnp.where(ec <= er, padded_row, 0.0), axis=1, keepdims=True)
    pstart = cum - padded
    cnt_ref[...] = jnp.sum(jnp.where(er == ec, counts, 0.0), axis=0, keepdims=True).astype(I32)
    pstart_ref[...] = jnp.sum(jnp.where(er == ec, pstart, 0.0), axis=0, keepdims=True).astype(I32)

    def dest(top_ref, rank_ref, out_ref):
        n = top_ref.shape[1]
        ei = lax.broadcasted_iota(I32, (N_EXPERTS, n), 0)
        rows = []
        for k in range(TOP_K):
            start = jnp.sum(jnp.where(ei == top_ref[k:k + 1, :], pstart, 0.0), axis=0, keepdims=True)
            rows.append(start + rank_ref[k:k + 1, :])
        out_ref[...] = jnp.concatenate(rows, axis=0).astype(I32)

    dest(tp_ref, rankp, destp_ref)
    dest(ts_ref, ranks, dests_ref)

    nb = blke_ref.shape[1]
    blk_lo = lax.broadcasted_iota(I32, (N_EXPERTS, nb), 1).astype(F32) * MOE_BLOCK
    blk = jnp.sum((cum <= blk_lo).astype(F32), axis=0, keepdims=True)
    blke_ref[...] = jnp.minimum(blk, N_EXPERTS - 1).astype(I32)
    nact_ref[...] = (cum[N_EXPERTS - 1:N_EXPERTS, :] / MOE_BLOCK).astype(I32)


def _slots(top_p, top_s):
    n_p, n_s = top_p.shape[1], top_s.shape[1]
    nb = _moe_num_blocks((n_p + n_s) * TOP_K)
    return pl.pallas_call(
        _slots_kernel,
        grid=(1,),
        in_specs=[_full((TOP_K, n_p)), _full((TOP_K, n_s))],
        out_specs=[_full((TOP_K, n_p)), _full((TOP_K, n_s)), _full((1, nb)), _full((1, 1)),
                   _full((1, N_EXPERTS)), _full((1, N_EXPERTS))],
        out_shape=[
            jax.ShapeDtypeStruct((TOP_K, n_p), I32),
            jax.ShapeDtypeStruct((TOP_K, n_s), I32),
            jax.ShapeDtypeStruct((1, nb), I32),
            jax.ShapeDtypeStruct((1, 1), I32),
            jax.ShapeDtypeStruct((1, N_EXPERTS), I32),
            jax.ShapeDtypeStruct((1, N_EXPERTS), I32),
        ],
        scratch_shapes=[pltpu.VMEM((TOP_K, n_p), F32), pltpu.VMEM((TOP_K, n_s), F32)],
        compiler_params=_cparams(("arbitrary",)),
        name="moe_slots",
    )(top_p, top_s)


def _for_rows(n_rows, fn):
    def body(i, c):
        for u in range(DMA_UNROLL):
            fn(i * DMA_UNROLL + u)
        return c
    lax.fori_loop(0, n_rows // DMA_UNROLL, body, 0)


def _dispatch_kernel(cnt_ref, pstart_ref, nact_ref, destp_ref, dests_ref, xnp_ref, xns_ref, xs_ref,
                     zeros, sem, zsem):
    i = pl.program_id(0)
    last = pl.num_programs(0) - 1

    def scatter(x_ref, dest_ref):
        n = x_ref.shape[0] // ROW_TILE

        def one(t):
            for k in range(TOP_K):
                _row_tile_copy(x_ref, t, xs_ref, dest_ref[t * TOP_K + k], sem).start(priority=k % 2)

        _for_rows(n, one)
        _row_tile_copy(xs_ref, 0, xs_ref, 0, sem, n * TOP_K).wait()

    @pl.when(i < last)
    def _():
        scatter(xnp_ref, destp_ref)

    @pl.when(i == last)
    def _():
        scatter(xns_ref, dests_ref)
        zeros[...] = jnp.zeros(zeros.shape, F32)

        def pad_copies(e, act):
            c = cnt_ref[e]
            npad = (-c) & (MOE_BLOCK - 1)
            base = pstart_ref[e] + c
            for bit in range(MOE_BLOCK.bit_length() - 1):
                size = 1 << bit

                @pl.when((npad & size) != 0)
                def _():
                    act(_row_tile_copy(zeros, 0, xs_ref, base + (npad & (size - 1)), zsem, size))

        def tail_copies(act):
            def body(j, c):
                act(_row_tile_copy(zeros, 0, xs_ref, j * MOE_BLOCK, zsem, MOE_BLOCK))
                return c
            lax.fori_loop(nact_ref[0], xs_ref.shape[0] // (MOE_BLOCK * ROW_TILE), body, 0)

        for act in (lambda d: d.start(), lambda d: d.wait()):
            lax.fori_loop(0, N_EXPERTS, lambda e, c, act=act: (pad_copies(e, act), c)[1], 0)
            tail_copies(act)


def _dispatch(cnt, pstart, nact, dest_p, dest_s, xn_p, xn_s, n_rows):
    n_p, n_s = xn_p.shape[0] // ROW_TILE, xn_s.shape[0] // ROW_TILE
    tile = DISPATCH_TILE
    nt = n_p // tile
    smem = pl.BlockSpec(memory_space=pltpu.SMEM)
    return pl.pallas_call(
        _dispatch_kernel,
        grid=(nt + 1,),
        in_specs=[
            smem, smem, smem,
            pl.BlockSpec((TOP_K * tile,), lambda i: (jnp.minimum(i, nt - 1),), memory_space=pltpu.SMEM),
            smem,
            pl.BlockSpec((tile * ROW_TILE, LANES), lambda i: (jnp.minimum(i, nt - 1), 0)),
            _full((n_s * ROW_TILE, LANES)),
        ],
        out_specs=pl.BlockSpec(memory_space=pl.ANY),
        out_shape=jax.ShapeDtypeStruct((n_rows * ROW_TILE, LANES), F32),
        scratch_shapes=[pltpu.VMEM((MOE_BLOCK * ROW_TILE, LANES), F32), pltpu.SemaphoreType.DMA,
                        pltpu.SemaphoreType.DMA],
        compiler_params=_cparams(("arbitrary",)),
        name="moe_dispatch",
    )(cnt, pstart, nact, dest_p, dest_s, xn_p, xn_s)


def _combine_kernel(dest_ref, dest_next_ref, gates_ref, x2_ref, gfin_ref, ys_ref, y_ref, bufs, sems):
    tile = x2_ref.shape[0]
    i = pl.program_id(0)
    slot = i % 2

    def gather(d_ref, b):
        def one(t):
            for k in range(TOP_K):
                _row_tile_copy(ys_ref, d_ref[t * TOP_K + k], bufs.at[b, k], t, sems.at[b]).start(priority=k % 2)
        _for_rows(tile, one)

    @pl.when(i == 0)
    def _():
        gather(dest_ref, 0)

    for b in range(2):
        @pl.when(slot == b)
        def _():
            pltpu.make_async_copy(bufs.at[b], bufs.at[b], sems.at[b]).wait()

            @pl.when(i + 1 < pl.num_programs(0))
            def _():
                gather(dest_next_ref, 1 - b)

    buf = bufs.at[slot]
    sumsq = jnp.zeros((tile, 1), F32)
    gate = [jnp.broadcast_to(gates_ref[:, k:k + 1], (tile, LANES)) for k in range(TOP_K)]
    for s in range(ROW_TILE):
        cols = slice(s * LANES, (s + 1) * LANES)
        acc = x2_ref[:, cols]
        for k in range(TOP_K):
            acc = acc + gate[k] * buf[k, pl.ds(s, tile, stride=ROW_TILE), :]
        y_ref[:, cols] = acc
        sumsq = sumsq + jnp.sum(acc * acc, axis=-1, keepdims=True)
    scale = lax.rsqrt(sumsq / D_MODEL + EPS)
    for s in range(ROW_TILE):
        cols = slice(s * LANES, (s + 1) * LANES)
        y_ref[:, cols] = y_ref[:, cols] * scale * gfin_ref[:, cols]


def _combine(dest, gates_t, x2, g_final, ys):
    n = x2.shape[0]
    tile = min(COMBINE_TILE, n)
    row = pl.BlockSpec((tile, D_MODEL), lambda i: (i, 0))
    nt = n // tile
    return pl.pallas_call(
        _combine_kernel,
        grid=(nt,),
        in_specs=[
            pl.BlockSpec((TOP_K * tile,), lambda i: (i,), memory_space=pltpu.SMEM),
            pl.BlockSpec((TOP_K * tile,), lambda i: (jnp.minimum(i + 1, nt - 1),), memory_space=pltpu.SMEM),
            pl.BlockSpec((tile, TOP_K), lambda i: (i, 0)),
            row,
            _full((1, D_MODEL)),
            pl.BlockSpec(memory_space=pl.ANY),
        ],
        out_specs=row,
        out_shape=jax.ShapeDtypeStruct((n, D_MODEL), F32),
        scratch_shapes=[pltpu.VMEM((2, TOP_K, tile * ROW_TILE, LANES), F32), pltpu.SemaphoreType.DMA((2,))],
        compiler_params=_cparams(("arbitrary",)),
        name="moe_combine",
    )(dest, dest, gates_t, x2, g_final, ys)


def _experts_kernel(blke_ref, nact_ref, xs_ref, w1_hbm, w2_hbm, *rest):
    b1_refs, b2_refs = rest[:EXPERT_STEP_BLOCKS], rest[EXPERT_STEP_BLOCKS:2 * EXPERT_STEP_BLOCKS]
    ys_ref, w1f, w2f, w1b, w2b, wsem, started = rest[2 * EXPERT_STEP_BLOCKS:]
    j = pl.program_id(0)
    nact = nact_ref[0]
    last = blke_ref.shape[0] - 1

    def fetch(expert, slot):
        return (pltpu.make_async_copy(w1_hbm.at[expert], w1f.at[slot], wsem.at[slot]),
                pltpu.make_async_copy(w2_hbm.at[expert], w2f.at[slot], wsem.at[slot]))

    def switch_to(blk):
        e = blke_ref[blk]
        n = started[0]
        jn = lax.while_loop(lambda jj: jnp.logical_and(jj < nact, blke_ref[jnp.minimum(jj, last)] == e),
                            lambda jj: jj + 1, blk + 1)
        for b in range(2):
            @pl.when(n % 2 == b)
            def _():
                for d in fetch(e, b):
                    d.wait()

                @pl.when(jn < nact)
                def _():
                    for d in fetch(blke_ref[jnp.minimum(jn, last)], 1 - b):
                        d.start(priority=1)

                w1b[...] = w1f[b].astype(BF16)
                w2b[...] = w2f[b].astype(BF16)
        started[0] = n + 1

    def ffn(row0, n_rows, b1_ref, b2_ref):
        rows = pl.ds(row0 * ROW_TILE, n_rows * ROW_TILE)
        x = _load_row_tiles(xs_ref.at[rows], n_rows).astype(BF16)
        gu = jnp.dot(x, w1b[...], preferred_element_type=F32) + b1_ref[0]
        g = jnp.minimum(gu[:, :D_EXPERT], SWIGLU_LIMIT)
        u = jnp.clip(gu[:, D_EXPERT:], -SWIGLU_LIMIT, SWIGLU_LIMIT)
        h = (u + 1.0) * (g * jax.nn.sigmoid(SWIGLU_ALPHA * g))
        _store_row_tiles(ys_ref.at[rows], _bdot(h, w2b[...]) + b2_ref[0])

    def zero(row0, n_rows):
        ys_ref[pl.ds(row0 * ROW_TILE, n_rows * ROW_TILE), :] = jnp.zeros((n_rows * ROW_TILE, LANES), F32)

    @pl.when(j == 0)
    def _():
        started[0] = 0
        for d in fetch(blke_ref[0], 0):
            d.start()

    for p in range(0, EXPERT_STEP_BLOCKS, 2):
        first, second = EXPERT_STEP_BLOCKS * j + p, EXPERT_STEP_BLOCKS * j + p + 1
        e0, e1 = blke_ref[first], blke_ref[second]
        active0, active1 = first < nact, second < nact
        together = jnp.logical_and(active1, e1 == e0)
        row0 = p * MOE_BLOCK

        @pl.when(jnp.logical_and(active0, jnp.logical_or(first == 0, e0 != blke_ref[jnp.maximum(first - 1, 0)])))
        def _():
            switch_to(first)

        @pl.when(together)
        def _():
            ffn(row0, 2 * MOE_BLOCK, b1_refs[p], b2_refs[p])

        @pl.when(jnp.logical_and(active0, jnp.logical_not(together)))
        def _():
            ffn(row0, MOE_BLOCK, b1_refs[p], b2_refs[p])

            @pl.when(active1)
            def _():
                switch_to(second)
                ffn(row0 + MOE_BLOCK, MOE_BLOCK, b1_refs[p + 1], b2_refs[p + 1])

            @pl.when(jnp.logical_not(active1))
            def _():
                zero(row0 + MOE_BLOCK, MOE_BLOCK)

        @pl.when(jnp.logical_not(active0))
        def _():
            zero(row0, 2 * MOE_BLOCK)


def _experts(blk_e, nact, xs, w1, b1, w2, b2):
    nb = xs.shape[0] // (MOE_BLOCK * ROW_TILE)
    step = EXPERT_STEP_BLOCKS
    assert nb % step == 0 and step % 2 == 0
    blk = lambda j, be, na: (jnp.minimum(j, (na[0] - 1) // step), 0)
    exp = lambda h: (lambda j, be, na: (be[jnp.minimum(step * j + h, na[0] - 1)], 0, 0))
    rows = (step * MOE_BLOCK * ROW_TILE, LANES)
    any_ = pl.BlockSpec(memory_space=pl.ANY)
    return pl.pallas_call(
        _experts_kernel,
        grid_spec=pltpu.PrefetchScalarGridSpec(
            num_scalar_prefetch=2,
            grid=(nb // step,),
            in_specs=[pl.BlockSpec(rows, blk), any_, any_]
            + [pl.BlockSpec((1, 1, 2 * D_EXPERT), exp(h)) for h in range(step)]
            + [pl.BlockSpec((1, 1, D_MODEL), exp(h)) for h in range(step)],
            out_specs=pl.BlockSpec(rows, lambda j, be, na: (j, 0)),
            scratch_shapes=[
                pltpu.VMEM((2, D_MODEL, 2 * D_EXPERT), F32), pltpu.VMEM((2, D_EXPERT, D_MODEL), F32),
                pltpu.VMEM((D_MODEL, 2 * D_EXPERT), BF16), pltpu.VMEM((D_EXPERT, D_MODEL), BF16),
                pltpu.SemaphoreType.DMA((2,)), pltpu.SMEM((1,), I32),
            ],
        ),
        out_shape=jax.ShapeDtypeStruct(xs.shape, F32),
        compiler_params=_cparams(("arbitrary",)),
        name="moe_experts",
    )(blk_e, nact, xs, w1, w2, *([b1] * step), *([b2] * step))


def _block_diag(w):
    h, d, _ = w.shape
    eye = jnp.eye(h, dtype=w.dtype)
    return (eye[:, None, :, None] * w[:, :, None, :]).reshape(h * d, h * d)


def kernel(x_prompt, x_sample, state_lru_h, cache_lru_conv, cache_cf_conv, cache_mem_k, cache_mem_v,
           mem_prompt, g_mix, w_in, w_lru_conv, b_lru_conv, w_lru_a, b_lru_a, w_lru_x, b_lru_x,
           lru_lambda, w_cf_conv, b_cf_conv, g_cf_ln, b_cf_ln, w_out, g_xa, g_mem, w_q, w_k, w_v, w_o,
           g_moe, w_router, b_router, w_e1, b_e1, w_e2, b_e2, g_final):
    depth = g_mix.shape[0]
    assert depth == 1, "single-layer stack"
    batch, seq, _ = x_prompt.shape
    n_s = x_sample.shape[0]
    assert x_sample.shape[1] == 1
    n_p = batch * seq
    l = 0

    mix_w = (
        g_mix[l][None], w_in[l].astype(BF16), w_lru_conv[l], b_lru_conv[l][None],
        jnp.concatenate([_block_diag(w_lru_a[l]), _block_diag(w_lru_x[l])], axis=1).astype(BF16),
        jnp.concatenate([b_lru_a[l], b_lru_x[l]])[None], lru_lambda[l][None],
        w_cf_conv[l], b_cf_conv[l][None], g_cf_ln[l][None], b_cf_ln[l][None], w_out[l].astype(BF16),
    )
    router_w = (g_moe[l][None], w_router[l].T.astype(BF16), b_router[l][:, None])
    wq, wo = w_q[l].astype(BF16), w_o[l].astype(BF16)
    gxa = g_xa[l][None]

    xs_ = x_sample.reshape(n_s, D_MODEL)
    x1p, hp, lcp, ccp = _mixer_prompt(x_prompt, mix_w)
    x1p = x1p.reshape(n_p, D_MODEL)
    x1s, hs, lru_conv_s, cf_conv_s = _mixer_sample(xs_, mix_w, state_lru_h[l], cache_lru_conv[l], cache_cf_conv[l])

    mk, mv, mkb, mvb = _memkv(mem_prompt.reshape(batch * N_MEM, D_MODEL), g_mem[l][None],
                              w_k[l].astype(BF16), w_v[l].astype(BF16))
    x2p, xnp_, top_p, gates_p = _attn_prompt(x1p, mkb, mvb, gxa, wq, wo, router_w, batch, seq)
    x2s, xns, top_s, gates_s = _attn_sample(x1s, cache_mem_k[l], cache_mem_v[l], gxa, wq, wo, router_w)

    dest_p, dest_s, blk_e, nact, cnt, pstart = _slots(top_p, top_s)
    dest_p, dest_s = dest_p.T.reshape(-1), dest_s.T.reshape(-1)
    nb = _moe_num_blocks((n_p + n_s) * TOP_K)
    nact = nact.reshape(1)
    xs = _dispatch(cnt.reshape(N_EXPERTS), pstart.reshape(N_EXPERTS), nact, dest_p, dest_s, xnp_, xns,
                   nb * MOE_BLOCK)
    ys = _experts(blk_e.reshape(nb), nact, xs, w_e1[l], b_e1[l][:, None, :], w_e2[l], b_e2[l][:, None, :])
    gfin = g_final[None]
    y_p = _combine(dest_p, gates_p.T, x2p, gfin, ys)
    y_s = _combine(dest_s, gates_s.T, x2s, gfin, ys)

    return (
        y_p.reshape(batch, seq, D_MODEL),
        y_s.reshape(n_s, 1, D_MODEL),
        hp[None],
        lcp[None],
        ccp[None],
        mk[None],
        mv[None],
        hs[None],
        lru_conv_s[None],
        cf_conv_s[None],
    )
```

```python
import jax
import jax.numpy as jnp
from jax import lax
from jax.experimental import pallas as pl
from jax.experimental.pallas import tpu as pltpu

F32 = jnp.float32
BF16 = jnp.bfloat16
I32 = jnp.int32

D_MODEL = 1024
LRU_WIDTH = 512
CF_WIDTH = 512
LRU_CONV = 4
LRU_C = 8.0
CF_CONV = 31
IN_WIDTH = 2 * LRU_WIDTH + 2 * CF_WIDTH
N_MEM = 256
XA_HEADS = 4
XA_HEAD_DIM = D_MODEL // XA_HEADS
N_EXPERTS = 32
TOP_K = 4
D_EXPERT = D_MODEL
SWIGLU_LIMIT = 7.0
SWIGLU_ALPHA = 1.702
EPS = 1e-6

LANES = 128
SUBLANES = 8
ROW_TILE = D_MODEL // LANES
assert ROW_TILE == SUBLANES
VMEM_LIMIT = 56 * 1024 * 1024

MIX_STEPS = 128
ATT_TILE = 1024
SAMPLE_GROUP = 4
SAMPLE_MIX_GROUP = 32
MOE_BLOCK = 256
EXPERT_STEP_BLOCKS = 4
SLOT_CHUNK = 256
DMA_UNROLL = 16
DISPATCH_TILE = 4096
COMBINE_TILE = 256


def _cparams(sem):
    return pltpu.CompilerParams(dimension_semantics=sem, vmem_limit_bytes=VMEM_LIMIT)


def _full(shape):
    n = len(shape)
    return pl.BlockSpec(shape, lambda *_: (0,) * n)


def _rms(x, g):
    return x * lax.rsqrt(jnp.mean(x * x, axis=-1, keepdims=True) + EPS) * g


def _bdot(a, b):
    return jnp.dot(a.astype(BF16), b, preferred_element_type=F32)


def _store_row_tiles(ref, x):
    m = x.shape[0]
    for s in range(ROW_TILE):
        ref[pl.ds(s, m, stride=ROW_TILE), :] = x[:, s * LANES:(s + 1) * LANES]


def _load_row_tiles(ref, m):
    return jnp.concatenate([ref[pl.ds(s, m, stride=ROW_TILE), :] for s in range(ROW_TILE)], axis=-1)


def _row_tile_copy(src_ref, src_row, dst_ref, dst_row, sem, n_rows=1):
    src = src_ref.at[pl.ds(pl.multiple_of(src_row * ROW_TILE, ROW_TILE), n_rows * ROW_TILE)]
    dst = dst_ref.at[pl.ds(pl.multiple_of(dst_row * ROW_TILE, ROW_TILE), n_rows * ROW_TILE)]
    return pltpu.make_async_copy(src, dst, sem)


def _lru_coeffs(xc, wg_ref, bg_ref, lam_ref):
    gl = _bdot(xc, wg_ref[...]) + bg_ref[...]
    r = jax.nn.sigmoid(gl[:, :LRU_WIDTH])
    i = jax.nn.sigmoid(gl[:, LRU_WIDTH:])
    log_a = -LRU_C * r * jax.nn.softplus(-lam_ref[...])
    a = jnp.exp(log_a)
    u = jnp.sqrt(-jnp.tanh(log_a) * (1.0 + a * a)) * (i * xc)
    return a, u


def _cf_post(gc, gln_ref, bln_ref):
    mu = jnp.mean(gc, axis=-1, keepdims=True)
    c = gc - mu
    y = c * lax.rsqrt(jnp.mean(c * c, axis=-1, keepdims=True) + EPS)
    return jax.nn.silu(y * gln_ref[...] + bln_ref[...])


def _mixer_out(x, y_lru, y_cf, wout_ref):
    y = _bdot(y_lru, wout_ref[:LRU_WIDTH, :]) + _bdot(y_cf, wout_ref[LRU_WIDTH:, :])
    return x + y


def _mixer_prompt_kernel(x_hbm, gmix_ref, win_ref, wlc_ref, blc_ref, wg_ref, bg_ref, lam_ref,
                         wcc_ref, bcc_ref, gln_ref, bln_ref, wout_ref,
                         x1_hbm, h_ref, ltail_ref, ctail_ref, winb_ref, wgb_ref, woutb_ref,
                         xbuf, obuf, zx_ext, glu_ext, h_carry, isem, osem):
    ts = MIX_STEPS
    nb = xbuf.shape[2]
    rows = ts * nb
    j = pl.program_id(0)
    nj = pl.num_programs(0)
    hl = (LRU_CONV - 1) * nb
    hc = (CF_CONV - 1) * nb

    def copies_in(step, s):
        return [pltpu.make_async_copy(x_hbm.at[b, pl.ds(step * ts, ts)], xbuf.at[s, pl.ds(0, ts), b], isem.at[s])
                for b in range(nb)]

    def copies_out(step, s):
        return [pltpu.make_async_copy(obuf.at[s, pl.ds(0, ts), b], x1_hbm.at[b, pl.ds(step * ts, ts)], osem.at[s])
                for b in range(nb)]

    @pl.when(j == 0)
    def _():
        zx_ext[0:hl, :] = jnp.zeros((hl, LRU_WIDTH), F32)
        glu_ext[0:hc, :] = jnp.zeros((hc, CF_WIDTH), F32)
        h_carry[...] = jnp.zeros(h_carry.shape, F32)
        for cp in copies_in(0, 0):
            cp.start()
        winb_ref[...] = win_ref[...].astype(BF16)
        wgb_ref[...] = wg_ref[...].astype(BF16)
        woutb_ref[...] = wout_ref[...].astype(BF16)

    for s in range(2):
        @pl.when(j % 2 == s)
        def _():
            for cp in copies_in(j, s):
                cp.wait()

            @pl.when(j + 1 < nj)
            def _():
                for cp in copies_in(j + 1, 1 - s):
                    cp.start()

            @pl.when(j >= 2)
            def _():
                for cp in copies_out(j - 2, s):
                    cp.wait()

    x = xbuf[j % 2].reshape(rows, D_MODEL)
    z = _bdot(_rms(x, gmix_ref[...]), winb_ref[...])
    zx = z[:, :LRU_WIDTH]
    zg = z[:, LRU_WIDTH:2 * LRU_WIDTH]
    za = z[:, 2 * LRU_WIDTH:2 * LRU_WIDTH + CF_WIDTH]
    zb = z[:, 2 * LRU_WIDTH + CF_WIDTH:]

    zx_ext[hl:hl + rows, :] = zx
    xc = blc_ref[...]
    for k in range(LRU_CONV):
        xc = xc + wlc_ref[k:k + 1, :] * zx_ext[k * nb:k * nb + rows, :]
    a, u = _lru_coeffs(xc, wgb_ref, bg_ref, lam_ref)
    h = h_carry[...]
    steps = []
    for t in range(ts):
        h = a[t * nb:(t + 1) * nb, :] * h + u[t * nb:(t + 1) * nb, :]
        steps.append(h)
    h_carry[...] = h
    y_lru = jnp.concatenate(steps, axis=0) * jax.nn.gelu(zg)

    glu = za * jax.nn.sigmoid(zb)
    glu_ext[hc:hc + rows, :] = glu
    gc = bcc_ref[...]
    for k in range(CF_CONV):
        gc = gc + wcc_ref[k:k + 1, :] * glu_ext[k * nb:k * nb + rows, :]
    y_cf = _cf_post(gc, gln_ref, bln_ref)

    x1 = _mixer_out(x, y_lru, y_cf, woutb_ref)
    for s in range(2):
        @pl.when(j % 2 == s)
        def _():
            obuf[s] = x1.reshape(ts, nb, D_MODEL)
            for cp in copies_out(j, s):
                cp.start()

            @pl.when(j == nj - 1)
            def _():
                for cp in copies_out(j, s):
                    cp.wait()

                @pl.when(j >= 1)
                def _():
                    for cp in copies_out(j - 1, 1 - s):
                        cp.wait()

    zx_ext[0:hl, :] = zx_ext[rows:rows + hl, :]
    glu_ext[0:hc, :] = glu_ext[rows:rows + hc, :]

    @pl.when(j == nj - 1)
    def _():
        h_ref[...] = h
        for b in range(nb):
            for r in range(LRU_CONV - 1):
                ltail_ref[b, r:r + 1, :] = zx_ext[r * nb + b:r * nb + b + 1, :]
            for r in range(CF_CONV - 1):
                ctail_ref[b, r:r + 1, :] = glu_ext[r * nb + b:r * nb + b + 1, :]


def _mixer_weight_specs():
    return [
        _full((1, D_MODEL)),
        _full((D_MODEL, IN_WIDTH)),
        _full((LRU_CONV, LRU_WIDTH)), _full((1, LRU_WIDTH)),
        _full((LRU_WIDTH, 2 * LRU_WIDTH)), _full((1, 2 * LRU_WIDTH)),
        _full((1, LRU_WIDTH)),
        _full((CF_CONV, CF_WIDTH)), _full((1, CF_WIDTH)),
        _full((1, CF_WIDTH)), _full((1, CF_WIDTH)),
        _full((D_MODEL, D_MODEL)),
    ]


def _mixer_prompt(x, mix_w):
    batch, seq, _ = x.shape
    ts = MIX_STEPS
    any_ = pl.BlockSpec(memory_space=pl.ANY)
    return pl.pallas_call(
        _mixer_prompt_kernel,
        grid=(seq // ts,),
        in_specs=[any_] + _mixer_weight_specs(),
        out_specs=[
            any_,
            _full((batch, LRU_WIDTH)),
            _full((batch, LRU_CONV - 1, LRU_WIDTH)),
            _full((batch, CF_CONV - 1, CF_WIDTH)),
            _full((D_MODEL, IN_WIDTH)), _full((LRU_WIDTH, 2 * LRU_WIDTH)), _full((D_MODEL, D_MODEL)),
        ],
        out_shape=[
            jax.ShapeDtypeStruct((batch, seq, D_MODEL), F32),
            jax.ShapeDtypeStruct((batch, LRU_WIDTH), F32),
            jax.ShapeDtypeStruct((batch, LRU_CONV - 1, LRU_WIDTH), F32),
            jax.ShapeDtypeStruct((batch, CF_CONV - 1, CF_WIDTH), F32),
            jax.ShapeDtypeStruct((D_MODEL, IN_WIDTH), BF16),
            jax.ShapeDtypeStruct((LRU_WIDTH, 2 * LRU_WIDTH), BF16),
            jax.ShapeDtypeStruct((D_MODEL, D_MODEL), BF16),
        ],
        scratch_shapes=[
            pltpu.VMEM((2, ts, batch, D_MODEL), F32),
            pltpu.VMEM((2, ts, batch, D_MODEL), F32),
            pltpu.VMEM(((LRU_CONV - 1 + ts) * batch, LRU_WIDTH), F32),
            pltpu.VMEM(((CF_CONV - 1 + ts) * batch, CF_WIDTH), F32),
            pltpu.VMEM((batch, LRU_WIDTH), F32),
            pltpu.SemaphoreType.DMA((2,)), pltpu.SemaphoreType.DMA((2,)),
        ],
        compiler_params=_cparams(("arbitrary",)),
        name="mixer_prompt",
    )(x, *mix_w)


def _mixer_sample_kernel(x_ref, gmix_ref, win_ref, wlc_ref, blc_ref, wg_ref, bg_ref, lam_ref,
                         wcc_ref, bcc_ref, gln_ref, bln_ref, wout_ref,
                         h0_ref, lbuf_ref, cbuf_ref,
                         x1_ref, h_ref, ltail_ref, ctail_ref):
    x = x_ref[...]
    z = _bdot(_rms(x, gmix_ref[...]), win_ref[...])
    zx = z[:, :LRU_WIDTH]
    zg = z[:, LRU_WIDTH:2 * LRU_WIDTH]
    za = z[:, 2 * LRU_WIDTH:2 * LRU_WIDTH + CF_WIDTH]
    zb = z[:, 2 * LRU_WIDTH + CF_WIDTH:]

    xc = blc_ref[...] + wlc_ref[LRU_CONV - 1:LRU_CONV, :] * zx
    xc = xc + jnp.sum(lbuf_ref[...] * wlc_ref[0:LRU_CONV - 1, :][None], axis=1)
    a, u = _lru_coeffs(xc, wg_ref, bg_ref, lam_ref)
    h = a * h0_ref[...] + u
    y_lru = h * jax.nn.gelu(zg)

    glu = za * jax.nn.sigmoid(zb)
    gc = bcc_ref[...] + wcc_ref[CF_CONV - 1:CF_CONV, :] * glu
    gc = gc + jnp.sum(cbuf_ref[...] * wcc_ref[0:CF_CONV - 1, :][None], axis=1)
    y_cf = _cf_post(gc, gln_ref, bln_ref)

    x1_ref[...] = _mixer_out(x, y_lru, y_cf, wout_ref)
    h_ref[...] = h
    ltail_ref[:, 0:LRU_CONV - 2, :] = lbuf_ref[:, 1:LRU_CONV - 1, :]
    ctail_ref[:, 0:CF_CONV - 2, :] = cbuf_ref[:, 1:CF_CONV - 1, :]
    for i in range(x.shape[0]):
        ltail_ref[i, LRU_CONV - 2:LRU_CONV - 1, :] = zx[i:i + 1, :]
        ctail_ref[i, CF_CONV - 2:CF_CONV - 1, :] = glu[i:i + 1, :]


def _mixer_sample(x, mix_w, h0, lbuf, cbuf):
    n = x.shape[0]
    g = SAMPLE_MIX_GROUP
    row = lambda w: pl.BlockSpec((g, w), lambda i: (i, 0))
    return pl.pallas_call(
        _mixer_sample_kernel,
        grid=(n // g,),
        in_specs=[row(D_MODEL)] + _mixer_weight_specs() + [
            row(LRU_WIDTH),
            pl.BlockSpec((g, LRU_CONV - 1, LRU_WIDTH), lambda i: (i, 0, 0)),
            pl.BlockSpec((g, CF_CONV - 1, CF_WIDTH), lambda i: (i, 0, 0)),
        ],
        out_specs=[row(D_MODEL), row(LRU_WIDTH),
                   pl.BlockSpec((g, LRU_CONV - 1, LRU_WIDTH), lambda i: (i, 0, 0)),
                   pl.BlockSpec((g, CF_CONV - 1, CF_WIDTH), lambda i: (i, 0, 0))],
        out_shape=[
            jax.ShapeDtypeStruct((n, D_MODEL), F32),
            jax.ShapeDtypeStruct((n, LRU_WIDTH), F32),
            jax.ShapeDtypeStruct((n, LRU_CONV - 1, LRU_WIDTH), F32),
            jax.ShapeDtypeStruct((n, CF_CONV - 1, CF_WIDTH), F32),
        ],
        compiler_params=_cparams(("arbitrary",)),
        name="mixer_sample",
    )(x, *mix_w, h0, lbuf, cbuf)


def _memkv_kernel(mem_ref, gmem_ref, wk_ref, wv_ref, k_ref, v_ref, kb_ref, vb_ref, wkb, wvb):
    @pl.when(pl.program_id(0) == 0)
    def _():
        wkb[...] = wk_ref[...].astype(BF16)
        wvb[...] = wv_ref[...].astype(BF16)

    mn = _rms(mem_ref[...], gmem_ref[...]).astype(BF16)
    k = jnp.dot(mn, wkb[...], preferred_element_type=F32)
    v = jnp.dot(mn, wvb[...], preferred_element_type=F32)
    for h in range(XA_HEADS):
        sl = slice(h * XA_HEAD_DIM, (h + 1) * XA_HEAD_DIM)
        k_ref[0, :, h, :] = k[:, sl]
        v_ref[0, :, h, :] = v[:, sl]
    kb_ref[...] = k.astype(BF16)
    vb_ref[...] = v.astype(BF16)


def _memkv(mem, g_mem, wk, wv):
    n = mem.shape[0]
    t = N_MEM
    row = pl.BlockSpec((t, D_MODEL), lambda i: (i, 0))
    state = pl.BlockSpec((1, N_MEM, XA_HEADS, XA_HEAD_DIM), lambda i: (i, 0, 0, 0))
    state_shape = jax.ShapeDtypeStruct((n // t, N_MEM, XA_HEADS, XA_HEAD_DIM), F32)
    return pl.pallas_call(
        _memkv_kernel,
        grid=(n // t,),
        in_specs=[row, _full((1, D_MODEL)), _full((D_MODEL, D_MODEL)), _full((D_MODEL, D_MODEL))],
        out_specs=[state, state, row, row],
        out_shape=[state_shape] * 2 + [jax.ShapeDtypeStruct((n, D_MODEL), BF16)] * 2,
        scratch_shapes=[pltpu.VMEM((D_MODEL, D_MODEL), BF16)] * 2,
        compiler_params=_cparams(("arbitrary",)),
        name="memory_kv",
    )(mem, g_mem, wk, wv)


def _router(x2, gmoe_ref, wrt_ref, brt_ref, xn_ref, tope_ref, gates_ref):
    xn = _rms(x2, gmoe_ref[...])
    _store_row_tiles(xn_ref, xn)
    logits = lax.dot_general(wrt_ref[...], xn.astype(BF16), (((1,), (1,)), ((), ())),
                             preferred_element_type=F32) + brt_ref[...]
    e_iota = lax.broadcasted_iota(I32, logits.shape, 0)
    work = logits
    vals, idxs = [], []
    for _ in range(TOP_K):
        m = jnp.max(work, axis=0, keepdims=True)
        idx = jnp.min(jnp.where(work == m, e_iota, N_EXPERTS), axis=0, keepdims=True)
        vals.append(m)
        idxs.append(idx)
        work = jnp.where(e_iota == idx, -jnp.inf, work)
    ex = [jnp.exp(v - vals[0]) for v in vals]
    den = ex[0] + ex[1] + ex[2] + ex[3]
    tope_ref[...] = jnp.concatenate(idxs, axis=0)
    gates_ref[...] = jnp.concatenate([e / den for e in ex], axis=0)


def _router_specs():
    return [_full((1, D_MODEL)), _full((N_EXPERTS, D_MODEL)), _full((N_EXPERTS, 1))]


def _router_out(n, tile, index):
    specs = [
        pl.BlockSpec((tile, D_MODEL), lambda *a: (index(*a), 0)),
        pl.BlockSpec((tile * ROW_TILE, LANES), lambda *a: (index(*a), 0)),
        pl.BlockSpec((TOP_K, tile), lambda *a: (0, index(*a))),
        pl.BlockSpec((TOP_K, tile), lambda *a: (0, index(*a))),
    ]
    shapes = [
        jax.ShapeDtypeStruct((n, D_MODEL), F32),
        jax.ShapeDtypeStruct((n * ROW_TILE, LANES), F32),
        jax.ShapeDtypeStruct((TOP_K, n), I32),
        jax.ShapeDtypeStruct((TOP_K, n), F32),
    ]
    return specs, shapes


def _attn_prompt_kernel(x1_ref, k_ref, v_ref, gxa_ref, wq_ref, wo_ref, gmoe_ref, wrt_ref, brt_ref,
                        x2_ref, xn_ref, tope_ref, gates_ref, wqb_ref, wob_ref):
    @pl.when(jnp.logical_and(pl.program_id(0) == 0, pl.program_id(1) == 0))
    def _():
        wqb_ref[...] = wq_ref[...].astype(BF16)
        wob_ref[...] = wo_ref[...].astype(BF16)

    x1 = x1_ref[...]
    q = (_bdot(_rms(x1, gxa_ref[...]), wqb_ref[...]) * (XA_HEAD_DIM ** -0.5)).astype(BF16)
    outs = []
    for h in range(XA_HEADS):
        sl = slice(h * XA_HEAD_DIM, (h + 1) * XA_HEAD_DIM)
        s = lax.dot_general(q[:, sl], k_ref[:, sl], (((1,), (1,)), ((), ())), preferred_element_type=F32)
        p = jnp.exp(s - jnp.max(s, axis=-1, keepdims=True))
        p = p / jnp.sum(p, axis=-1, keepdims=True)
        outs.append(_bdot(p, v_ref[:, sl]))
    o = jnp.concatenate(outs, axis=-1)
    x2 = x1 + _bdot(o, wob_ref[...])
    x2_ref[...] = x2
    _router(x2, gmoe_ref, wrt_ref, brt_ref, xn_ref, tope_ref, gates_ref)


def _attn_prompt(x1, k, v, gxa, wq, wo, router_w, batch, seq):
    tq = ATT_TILE
    nt = seq // tq
    out_specs, out_shapes = _router_out(batch * seq, tq, lambda b, j: b * nt + j)
    kv = pl.BlockSpec((N_MEM, D_MODEL), lambda b, j: (b, 0))
    return pl.pallas_call(
        _attn_prompt_kernel,
        grid=(batch, nt),
        in_specs=[pl.BlockSpec((tq, D_MODEL), lambda b, j: (b * nt + j, 0)), kv, kv,
                  _full((1, D_MODEL)), _full((D_MODEL, D_MODEL)), _full((D_MODEL, D_MODEL))] + _router_specs(),
        out_specs=out_specs + [_full((D_MODEL, D_MODEL))] * 2,
        out_shape=out_shapes + [jax.ShapeDtypeStruct((D_MODEL, D_MODEL), BF16)] * 2,
        compiler_params=_cparams(("arbitrary", "arbitrary")),
        name="attn_prompt",
    )(x1, k, v, gxa, wq, wo, *router_w)


def _qproj_kernel(x1_ref, gxa_ref, wq_ref, q_ref):
    q_ref[...] = _bdot(_rms(x1_ref[...], gxa_ref[...]), wq_ref[...]) * (XA_HEAD_DIM ** -0.5)


def _attn_sample_core_kernel(q_ref, k_ref, v_ref, o_ref):
    n = N_MEM * SUBLANES
    col_head = lax.broadcasted_iota(I32, (SUBLANES, n), 1) & (SUBLANES - 1)
    same_head = col_head == lax.broadcasted_iota(I32, (SUBLANES, n), 0)
    kv_pad = jnp.zeros((N_MEM, SUBLANES - XA_HEADS, XA_HEAD_DIM), F32)
    q_pad = jnp.zeros((SUBLANES - XA_HEADS, XA_HEAD_DIM), F32)
    for g in range(q_ref.shape[0]):
        k8 = jnp.concatenate([k_ref[g], kv_pad], axis=1).reshape(n, XA_HEAD_DIM).astype(BF16)
        v8 = jnp.concatenate([v_ref[g], kv_pad], axis=1).reshape(n, XA_HEAD_DIM).astype(BF16)
        q8 = jnp.concatenate([q_ref[g], q_pad], axis=0).astype(BF16)
        s = lax.dot_general(q8, k8, (((1,), (1,)), ((), ())), preferred_element_type=F32)
        m = jnp.max(jnp.where(same_head, s, -jnp.inf), axis=-1, keepdims=True)
        p = jnp.where(same_head, jnp.exp(s - m), 0.0)
        den = jnp.sum(p, axis=-1, keepdims=True)
        o8 = jnp.dot(p.astype(BF16), v8, preferred_element_type=F32)
        o_ref[g] = (o8 / den)[0:XA_HEADS, :]


def _oproj_router_kernel(x1_ref, o_ref, wo_ref, gmoe_ref, wrt_ref, brt_ref,
                         x2_ref, xn_ref, tope_ref, gates_ref):
    x2 = x1_ref[...] + _bdot(o_ref[...], wo_ref[...])
    x2_ref[...] = x2
    _router(x2, gmoe_ref, wrt_ref, brt_ref, xn_ref, tope_ref, gates_ref)


def _attn_sample(x1, k, v, gxa, wq, wo, router_w):
    n = x1.shape[0]
    g = SAMPLE_GROUP
    q = pl.pallas_call(
        _qproj_kernel,
        grid=(1,),
        in_specs=[_full((n, D_MODEL)), _full((1, D_MODEL)), _full((D_MODEL, D_MODEL))],
        out_specs=_full((n, D_MODEL)),
        out_shape=jax.ShapeDtypeStruct((n, D_MODEL), F32),
        compiler_params=_cparams(("arbitrary",)),
        name="qproj_sample",
    )(x1, gxa, wq)
    kv = pl.BlockSpec((g, N_MEM, XA_HEADS, XA_HEAD_DIM), lambda i: (i, 0, 0, 0))
    row = pl.BlockSpec((g, XA_HEADS, XA_HEAD_DIM), lambda i: (i, 0, 0))
    o = pl.pallas_call(
        _attn_sample_core_kernel,
        grid=(n // g,),
        in_specs=[row, kv, kv],
        out_specs=row,
        out_shape=jax.ShapeDtypeStruct((n, XA_HEADS, XA_HEAD_DIM), F32),
        compiler_params=_cparams(("arbitrary",)),
        name="attn_sample_core",
    )(q.reshape(n, XA_HEADS, XA_HEAD_DIM), k, v).reshape(n, D_MODEL)
    out_specs, out_shapes = _router_out(n, n, lambda i: 0)
    return pl.pallas_call(
        _oproj_router_kernel,
        grid=(1,),
        in_specs=[_full((n, D_MODEL)), _full((n, D_MODEL)), _full((D_MODEL, D_MODEL))] + _router_specs(),
        out_specs=out_specs,
        out_shape=out_shapes,
        compiler_params=_cparams(("arbitrary",)),
        name="oproj_router_sample",
    )(x1, o, wo, *router_w)


def _moe_num_blocks(n_assign):
    nb = pl.cdiv(n_assign, MOE_BLOCK) + N_EXPERTS
    return pl.cdiv(nb, EXPERT_STEP_BLOCKS) * EXPERT_STEP_BLOCKS


def _slots_kernel(tp_ref, ts_ref, destp_ref, dests_ref, blke_ref, nact_ref, cnt_ref, pstart_ref,
                  rankp, ranks):
    def make_body(top_ref, rank_ref):
        c = min(SLOT_CHUNK, top_ref.shape[1])
        r_i = lax.broadcasted_iota(I32, (c, c), 0)
        c_i = lax.broadcasted_iota(I32, (c, c), 1)
        upper = (r_i < c_i).astype(BF16)
        e_iota = lax.broadcasted_iota(I32, (N_EXPERTS, c), 0)

        def body(ci, carry):
            lo = pl.multiple_of(ci * c, c)
            top = top_ref[:, pl.ds(lo, c)]
            hit = [e_iota == top[k:k + 1, :] for k in range(TOP_K)]
            cnt = sum(m.astype(F32) for m in hit)
            base = carry + jnp.dot(cnt.astype(BF16), upper, preferred_element_type=F32)
            rank_ref[:, pl.ds(lo, c)] = jnp.concatenate(
                [jnp.sum(jnp.where(m, base, 0.0), axis=0, keepdims=True) for m in hit], axis=0)
            return carry + jnp.sum(cnt, axis=1, keepdims=True)
        return top_ref.shape[1] // c, body

    counts = jnp.zeros((N_EXPERTS, 1), F32)
    for top_ref, rank_ref in ((tp_ref, rankp), (ts_ref, ranks)):
        trips, body = make_body(top_ref, rank_ref)
        counts = lax.fori_loop(0, trips, body, counts)

    padded = jnp.floor((counts + (MOE_BLOCK - 1)) / MOE_BLOCK) * MOE_BLOCK
    er = lax.broadcasted_iota(I32, (N_EXPERTS, N_EXPERTS), 0)
    ec = lax.broadcasted_iota(I32, (N_EXPERTS, N_EXPERTS), 1)
    padded_row = jnp.sum(jnp.where(er == ec, padded, 0.0), axis=0, keepdims=True)
    cum = jnp.sum(jnp.where(ec <= er, padded_row, 0.0), axis=1, keepdims=True)
    pstart = cum - padded
    cnt_ref[...] = jnp.sum(jnp.where(er == ec, counts, 0.0), axis=0, keepdims=True).astype(I32)
    pstart_ref[...] = jnp.sum(jnp.where(er == ec, pstart, 0.0), axis=0, keepdims=True).astype(I32)

    def dest(top_ref, rank_ref, out_ref):
        n = top_ref.shape[1]
        ei = lax.broadcasted_iota(I32, (N_EXPERTS, n), 0)
        rows = []
        for k in range(TOP_K):
            start = jnp.sum(jnp.where(ei == top_ref[k:k + 1, :], pstart, 0.0), axis=0, keepdims=True)
            rows.append(start + rank_ref[k:k + 1, :])
        out_ref[...] = jnp.concatenate(rows, axis=0).astype(I32)

    dest(tp_ref, rankp, destp_ref)
    dest(ts_ref, ranks, dests_ref)

    nb = blke_ref.shape[1]
    blk_lo = lax.broadcasted_iota(I32, (N_EXPERTS, nb), 1).astype(F32) * MOE_BLOCK
    blk = jnp.sum((cum <= blk_lo).astype(F32), axis=0, keepdims=True)
    blke_ref[...] = jnp.minimum(blk, N_EXPERTS - 1).astype(I32)
    nact_ref[...] = (cum[N_EXPERTS - 1:N_EXPERTS, :] / MOE_BLOCK).astype(I32)


def _slots(top_p, top_s):
    n_p, n_s = top_p.shape[1], top_s.shape[1]
    nb = _moe_num_blocks((n_p + n_s) * TOP_K)
    return pl.pallas_call(
        _slots_kernel,
        grid=(1,),
        in_specs=[_full((TOP_K, n_p)), _full((TOP_K, n_s))],
        out_specs=[_full((TOP_K, n_p)), _full((TOP_K, n_s)), _full((1, nb)), _full((1, 1)),
                   _full((1, N_EXPERTS)), _full((1, N_EXPERTS))],
        out_shape=[
            jax.ShapeDtypeStruct((TOP_K, n_p), I32),
            jax.ShapeDtypeStruct((TOP_K, n_s), I32),
            jax.ShapeDtypeStruct((1, nb), I32),
            jax.ShapeDtypeStruct((1, 1), I32),
            jax.ShapeDtypeStruct((1, N_EXPERTS), I32),
            jax.ShapeDtypeStruct((1, N_EXPERTS), I32),
        ],
        scratch_shapes=[pltpu.VMEM((TOP_K, n_p), F32), pltpu.VMEM((TOP_K, n_s), F32)],
        compiler_params=_cparams(("arbitrary",)),
        name="moe_slots",
    )(top_p, top_s)


def _for_rows(n_rows, fn):
    def body(i, c):
        for u in range(DMA_UNROLL):
            fn(i * DMA_UNROLL + u)
        return c
    lax.fori_loop(0, n_rows // DMA_UNROLL, body, 0)


def _dispatch_kernel(cnt_ref, pstart_ref, nact_ref, destp_ref, dests_ref, xnp_ref, xns_ref, xs_ref,
                     zeros, sem, zsem):
    i = pl.program_id(0)
    last = pl.num_programs(0) - 1

    def scatter(x_ref, dest_ref):
        n = x_ref.shape[0] // ROW_TILE

        def one(t):
            for k in range(TOP_K):
                _row_tile_copy(x_ref, t, xs_ref, dest_ref[t * TOP_K + k], sem).start(priority=k % 2)

        _for_rows(n, one)
        _row_tile_copy(xs_ref, 0, xs_ref, 0, sem, n * TOP_K).wait()

    @pl.when(i < last)
    def _():
        scatter(xnp_ref, destp_ref)

    @pl.when(i == last)
    def _():
        scatter(xns_ref, dests_ref)
        zeros[...] = jnp.zeros(zeros.shape, F32)

        def pad_copies(e, act):
            c = cnt_ref[e]
            npad = (-c) & (MOE_BLOCK - 1)
            base = pstart_ref[e] + c
            for bit in range(MOE_BLOCK.bit_length() - 1):
                size = 1 << bit

                @pl.when((npad & size) != 0)
                def _():
                    act(_row_tile_copy(zeros, 0, xs_ref, base + (npad & (size - 1)), zsem, size))

        def tail_copies(act):
            def body(j, c):
                act(_row_tile_copy(zeros, 0, xs_ref, j * MOE_BLOCK, zsem, MOE_BLOCK))
                return c
            lax.fori_loop(nact_ref[0], xs_ref.shape[0] // (MOE_BLOCK * ROW_TILE), body, 0)

        for act in (lambda d: d.start(), lambda d: d.wait()):
            lax.fori_loop(0, N_EXPERTS, lambda e, c, act=act: (pad_copies(e, act), c)[1], 0)
            tail_copies(act)


def _dispatch(cnt, pstart, nact, dest_p, dest_s, xn_p, xn_s, n_rows):
    n_p, n_s = xn_p.shape[0] // ROW_TILE, xn_s.shape[0] // ROW_TILE
    tile = DISPATCH_TILE
    nt = n_p // tile
    smem = pl.BlockSpec(memory_space=pltpu.SMEM)
    return pl.pallas_call(
        _dispatch_kernel,
        grid=(nt + 1,),
        in_specs=[
            smem, smem, smem,
            pl.BlockSpec((TOP_K * tile,), lambda i: (jnp.minimum(i, nt - 1),), memory_space=pltpu.SMEM),
            smem,
            pl.BlockSpec((tile * ROW_TILE, LANES), lambda i: (jnp.minimum(i, nt - 1), 0)),
            _full((n_s * ROW_TILE, LANES)),
        ],
        out_specs=pl.BlockSpec(memory_space=pl.ANY),
        out_shape=jax.ShapeDtypeStruct((n_rows * ROW_TILE, LANES), F32),
        scratch_shapes=[pltpu.VMEM((MOE_BLOCK * ROW_TILE, LANES), F32), pltpu.SemaphoreType.DMA,
                        pltpu.SemaphoreType.DMA],
        compiler_params=_cparams(("arbitrary",)),
        name="moe_dispatch",
    )(cnt, pstart, nact, dest_p, dest_s, xn_p, xn_s)


def _combine_kernel(dest_ref, dest_next_ref, gates_ref, x2_ref, gfin_ref, ys_ref, y_ref, bufs, sems):
    tile = x2_ref.shape[0]
    i = pl.program_id(0)
    slot = i % 2

    def gather(d_ref, b):
        def one(t):
            for k in range(TOP_K):
                _row_tile_copy(ys_ref, d_ref[t * TOP_K + k], bufs.at[b, k], t, sems.at[b]).start(priority=k % 2)
        _for_rows(tile, one)

    @pl.when(i == 0)
    def _():
        gather(dest_ref, 0)

    for b in range(2):
        @pl.when(slot == b)
        def _():
            pltpu.make_async_copy(bufs.at[b], bufs.at[b], sems.at[b]).wait()

            @pl.when(i + 1 < pl.num_programs(0))
            def _():
                gather(dest_next_ref, 1 - b)

    buf = bufs.at[slot]
    sumsq = jnp.zeros((tile, 1), F32)
    gate = [jnp.broadcast_to(gates_ref[:, k:k + 1], (tile, LANES)) for k in range(TOP_K)]
    for s in range(ROW_TILE):
        cols = slice(s * LANES, (s + 1) * LANES)
        acc = x2_ref[:, cols]
        for k in range(TOP_K):
            acc = acc + gate[k] * buf[k, pl.ds(s, tile, stride=ROW_TILE), :]
        y_ref[:, cols] = acc
        sumsq = sumsq + jnp.sum(acc * acc, axis=-1, keepdims=True)
    scale = lax.rsqrt(sumsq / D_MODEL + EPS)
    for s in range(ROW_TILE):
        cols = slice(s * LANES, (s + 1) * LANES)
        y_ref[:, cols] = y_ref[:, cols] * scale * gfin_ref[:, cols]


def _combine(dest, gates_t, x2, g_final, ys):
    n = x2.shape[0]
    tile = min(COMBINE_TILE, n)
    row = pl.BlockSpec((tile, D_MODEL), lambda i: (i, 0))
    nt = n // tile
    return pl.pallas_call(
        _combine_kernel,
        grid=(nt,),
        in_specs=[
            pl.BlockSpec((TOP_K * tile,), lambda i: (i,), memory_space=pltpu.SMEM),
            pl.BlockSpec((TOP_K * tile,), lambda i: (jnp.minimum(i + 1, nt - 1),), memory_space=pltpu.SMEM),
            pl.BlockSpec((tile, TOP_K), lambda i: (i, 0)),
            row,
            _full((1, D_MODEL)),
            pl.BlockSpec(memory_space=pl.ANY),
        ],
        out_specs=row,
        out_shape=jax.ShapeDtypeStruct((n, D_MODEL), F32),
        scratch_shapes=[pltpu.VMEM((2, TOP_K, tile * ROW_TILE, LANES), F32), pltpu.SemaphoreType.DMA((2,))],
        compiler_params=_cparams(("arbitrary",)),
        name="moe_combine",
    )(dest, dest, gates_t, x2, g_final, ys)


def _experts_kernel(blke_ref, nact_ref, xs_ref, w1_hbm, w2_hbm, *rest):
    b1_refs, b2_refs = rest[:EXPERT_STEP_BLOCKS], rest[EXPERT_STEP_BLOCKS:2 * EXPERT_STEP_BLOCKS]
    ys_ref, w1f, w2f, w1b, w2b, wsem, started = rest[2 * EXPERT_STEP_BLOCKS:]
    j = pl.program_id(0)
    nact = nact_ref[0]
    last = blke_ref.shape[0] - 1

    def fetch(expert, slot):
        return (pltpu.make_async_copy(w1_hbm.at[expert], w1f.at[slot], wsem.at[slot]),
                pltpu.make_async_copy(w2_hbm.at[expert], w2f.at[slot], wsem.at[slot]))

    def switch_to(blk):
        e = blke_ref[blk]
        n = started[0]
        jn = lax.while_loop(lambda jj: jnp.logical_and(jj < nact, blke_ref[jnp.minimum(jj, last)] == e),
                            lambda jj: jj + 1, blk + 1)
        for b in range(2):
            @pl.when(n % 2 == b)
            def _():
                for d in fetch(e, b):
                    d.wait()

                @pl.when(jn < nact)
                def _():
                    for d in fetch(blke_ref[jnp.minimum(jn, last)], 1 - b):
                        d.start(priority=1)

                w1b[...] = w1f[b].astype(BF16)
                w2b[...] = w2f[b].astype(BF16)
        started[0] = n + 1

    def ffn(row0, n_rows, b1_ref, b2_ref):
        rows = pl.ds(row0 * ROW_TILE, n_rows * ROW_TILE)
        x = _load_row_tiles(xs_ref.at[rows], n_rows).astype(BF16)
        gu = jnp.dot(x, w1b[...], preferred_element_type=F32) + b1_ref[0]
        g = jnp.minimum(gu[:, :D_EXPERT], SWIGLU_LIMIT)
        u = jnp.clip(gu[:, D_EXPERT:], -SWIGLU_LIMIT, SWIGLU_LIMIT)
        h = (u + 1.0) * (g * jax.nn.sigmoid(SWIGLU_ALPHA * g))
        _store_row_tiles(ys_ref.at[rows], _bdot(h, w2b[...]) + b2_ref[0])

    def zero(row0, n_rows):
        ys_ref[pl.ds(row0 * ROW_TILE, n_rows * ROW_TILE), :] = jnp.zeros((n_rows * ROW_TILE, LANES), F32)

    @pl.when(j == 0)
    def _():
        started[0] = 0
        for d in fetch(blke_ref[0], 0):
            d.start()

    for p in range(0, EXPERT_STEP_BLOCKS, 2):
        first, second = EXPERT_STEP_BLOCKS * j + p, EXPERT_STEP_BLOCKS * j + p + 1
        e0, e1 = blke_ref[first], blke_ref[second]
        active0, active1 = first < nact, second < nact
        together = jnp.logical_and(active1, e1 == e0)
        row0 = p * MOE_BLOCK

        @pl.when(jnp.logical_and(active0, jnp.logical_or(first == 0, e0 != blke_ref[jnp.maximum(first - 1, 0)])))
        def _():
            switch_to(first)

        @pl.when(together)
        def _():
            ffn(row0, 2 * MOE_BLOCK, b1_refs[p], b2_refs[p])

        @pl.when(jnp.logical_and(active0, jnp.logical_not(together)))
        def _():
            ffn(row0, MOE_BLOCK, b1_refs[p], b2_refs[p])

            @pl.when(active1)
            def _():
                switch_to(second)
                ffn(row0 + MOE_BLOCK, MOE_BLOCK, b1_refs[p + 1], b2_refs[p + 1])

            @pl.when(jnp.logical_not(active1))
            def _():
                zero(row0 + MOE_BLOCK, MOE_BLOCK)

        @pl.when(jnp.logical_not(active0))
        def _():
            zero(row0, 2 * MOE_BLOCK)


def _experts(blk_e, nact, xs, w1, b1, w2, b2):
    nb = xs.shape[0] // (MOE_BLOCK * ROW_TILE)
    step = EXPERT_STEP_BLOCKS
    assert nb % step == 0 and step % 2 == 0
    blk = lambda j, be, na: (jnp.minimum(j, (na[0] - 1) // step), 0)
    exp = lambda h: (lambda j, be, na: (be[jnp.minimum(step * j + h, na[0] - 1)], 0, 0))
    rows = (step * MOE_BLOCK * ROW_TILE, LANES)
    any_ = pl.BlockSpec(memory_space=pl.ANY)
    return pl.pallas_call(
        _experts_kernel,
        grid_spec=pltpu.PrefetchScalarGridSpec(
            num_scalar_prefetch=2,
            grid=(nb // step,),
            in_specs=[pl.BlockSpec(rows, blk), any_, any_]
            + [pl.BlockSpec((1, 1, 2 * D_EXPERT), exp(h)) for h in range(step)]
            + [pl.BlockSpec((1, 1, D_MODEL), exp(h)) for h in range(step)],
            out_specs=pl.BlockSpec(rows, lambda j, be, na: (j, 0)),
            scratch_shapes=[
                pltpu.VMEM((2, D_MODEL, 2 * D_EXPERT), F32), pltpu.VMEM((2, D_EXPERT, D_MODEL), F32),
                pltpu.VMEM((D_MODEL, 2 * D_EXPERT), BF16), pltpu.VMEM((D_EXPERT, D_MODEL), BF16),
                pltpu.SemaphoreType.DMA((2,)), pltpu.SMEM((1,), I32),
            ],
        ),
        out_shape=jax.ShapeDtypeStruct(xs.shape, F32),
        compiler_params=_cparams(("arbitrary",)),
        name="moe_experts",
    )(blk_e, nact, xs, w1, w2, *([b1] * step), *([b2] * step))


def _block_diag(w):
    h, d, _ = w.shape
    eye = jnp.eye(h, dtype=w.dtype)
    return (eye[:, None, :, None] * w[:, :, None, :]).reshape(h * d, h * d)


def kernel(x_prompt, x_sample, state_lru_h, cache_lru_conv, cache_cf_conv, cache_mem_k, cache_mem_v,
           mem_prompt, g_mix, w_in, w_lru_conv, b_lru_conv, w_lru_a, b_lru_a, w_lru_x, b_lru_x,
           lru_lambda, w_cf_conv, b_cf_conv, g_cf_ln, b_cf_ln, w_out, g_xa, g_mem, w_q, w_k, w_v, w_o,
           g_moe, w_router, b_router, w_e1, b_e1, w_e2, b_e2, g_final):
    depth = g_mix.shape[0]
    assert depth == 1, "single-layer stack"
    batch, seq, _ = x_prompt.shape
    n_s = x_sample.shape[0]
    assert x_sample.shape[1] == 1
    n_p = batch * seq
    l = 0

    mix_w = (
        g_mix[l][None], w_in[l], w_lru_conv[l], b_lru_conv[l][None],
        jnp.concatenate([_block_diag(w_lru_a[l]), _block_diag(w_lru_x[l])], axis=1),
        jnp.concatenate([b_lru_a[l], b_lru_x[l]])[None], lru_lambda[l][None],
        w_cf_conv[l], b_cf_conv[l][None], g_cf_ln[l][None], b_cf_ln[l][None], w_out[l],
    )
    router_w = (g_moe[l][None], w_router[l].T.astype(BF16), b_router[l][:, None])
    gxa = g_xa[l][None]

    xs_ = x_sample.reshape(n_s, D_MODEL)
    x1p, hp, lcp, ccp, win_b, wg_b, wout_b = _mixer_prompt(x_prompt, mix_w)
    x1p = x1p.reshape(n_p, D_MODEL)
    mix_w_b = mix_w[:1] + (win_b,) + mix_w[2:4] + (wg_b,) + mix_w[5:-1] + (wout_b,)
    x1s, hs, lru_conv_s, cf_conv_s = _mixer_sample(xs_, mix_w_b, state_lru_h[l], cache_lru_conv[l], cache_cf_conv[l])

    mk, mv, mkb, mvb = _memkv(mem_prompt.reshape(batch * N_MEM, D_MODEL), g_mem[l][None],
                              w_k[l], w_v[l])
    x2p, xnp_, top_p, gates_p, wq, wo = _attn_prompt(x1p, mkb, mvb, gxa, w_q[l], w_o[l], router_w, batch, seq)
    x2s, xns, top_s, gates_s = _attn_sample(x1s, cache_mem_k[l], cache_mem_v[l], gxa, wq, wo, router_w)

    dest_p, dest_s, blk_e, nact, cnt, pstart = _slots(top_p, top_s)
    dest_p, dest_s = dest_p.T.reshape(-1), dest_s.T.reshape(-1)
    nb = _moe_num_blocks((n_p + n_s) * TOP_K)
    nact = nact.reshape(1)
    xs = _dispatch(cnt.reshape(N_EXPERTS), pstart.reshape(N_EXPERTS), nact, dest_p, dest_s, xnp_, xns,
                   nb * MOE_BLOCK)
    ys = _experts(blk_e.reshape(nb), nact, xs, w_e1[l], b_e1[l][:, None, :], w_e2[l], b_e2[l][:, None, :])
    gfin = g_final[None]
    y_p = _combine(dest_p, gates_p.T, x2p, gfin, ys)
    y_s = _combine(dest_s, gates_s.T, x2s, gfin, ys)

    return (
        y_p.reshape(batch, seq, D_MODEL),
        y_s.reshape(n_s, 1, D_MODEL),
        hp[None],
        lcp[None],
        ccp[None],
        mk[None],
        mv[None],
        hs[None],
        lru_conv_s[None],
        cf_conv_s[None],
    )
```

```python
import jax
import jax.numpy as jnp
from jax import lax
from jax.experimental import pallas as pl
from jax.experimental.pallas import tpu as pltpu

F32 = jnp.float32
BF16 = jnp.bfloat16
I32 = jnp.int32

D_MODEL = 1024
LRU_WIDTH = 512
CF_WIDTH = 512
LRU_CONV = 4
LRU_C = 8.0
CF_CONV = 31
IN_WIDTH = 2 * LRU_WIDTH + 2 * CF_WIDTH
N_MEM = 256
XA_HEADS = 4
XA_HEAD_DIM = D_MODEL // XA_HEADS
N_EXPERTS = 32
TOP_K = 4
D_EXPERT = D_MODEL
SWIGLU_LIMIT = 7.0
SWIGLU_ALPHA = 1.702
EPS = 1e-6

LANES = 128
SUBLANES = 8
ROW_TILE = D_MODEL // LANES
assert ROW_TILE == SUBLANES
VMEM_LIMIT = 56 * 1024 * 1024

MIX_STEPS = 128
ATT_TILE = 1024
SAMPLE_GROUP = 4
SAMPLE_MIX_GROUP = 32
MOE_BLOCK = 256
EXPERT_STEP_BLOCKS = 4
SLOT_CHUNK = 256
DMA_UNROLL = 16
DISPATCH_TILE = 4096
COMBINE_TILE = 256


def _cparams(sem):
    return pltpu.CompilerParams(dimension_semantics=sem, vmem_limit_bytes=VMEM_LIMIT)


def _full(shape):
    n = len(shape)
    return pl.BlockSpec(shape, lambda *_: (0,) * n)


def _layer0(shape):
    n = len(shape)
    return pl.BlockSpec((None,) + tuple(shape), lambda *_: (0,) * (n + 1))


def _rms(x, g):
    return x * lax.rsqrt(jnp.mean(x * x, axis=-1, keepdims=True) + EPS) * g


def _bdot(a, b):
    return jnp.dot(a.astype(BF16), b, preferred_element_type=F32)


def _store_row_tiles(ref, x):
    m = x.shape[0]
    for s in range(ROW_TILE):
        ref[pl.ds(s, m, stride=ROW_TILE), :] = x[:, s * LANES:(s + 1) * LANES]


def _load_row_tiles(ref, m):
    return jnp.concatenate([ref[pl.ds(s, m, stride=ROW_TILE), :] for s in range(ROW_TILE)], axis=-1)


def _row_tile_copy(src_ref, src_row, dst_ref, dst_row, sem, n_rows=1):
    src = src_ref.at[pl.ds(pl.multiple_of(src_row * ROW_TILE, ROW_TILE), n_rows * ROW_TILE)]
    dst = dst_ref.at[pl.ds(pl.multiple_of(dst_row * ROW_TILE, ROW_TILE), n_rows * ROW_TILE)]
    return pltpu.make_async_copy(src, dst, sem)


def _lru_coeffs(xc, wg_ref, bg_ref, lam_ref):
    gl = _bdot(xc, wg_ref[...]) + bg_ref[...]
    r = jax.nn.sigmoid(gl[:, :LRU_WIDTH])
    i = jax.nn.sigmoid(gl[:, LRU_WIDTH:])
    log_a = -LRU_C * r * jax.nn.softplus(-lam_ref[...])
    a = jnp.exp(log_a)
    u = jnp.sqrt(-jnp.tanh(log_a) * (1.0 + a * a)) * (i * xc)
    return a, u


def _cf_post(gc, gln_ref, bln_ref):
    mu = jnp.mean(gc, axis=-1, keepdims=True)
    c = gc - mu
    y = c * lax.rsqrt(jnp.mean(c * c, axis=-1, keepdims=True) + EPS)
    return jax.nn.silu(y * gln_ref[...] + bln_ref[...])


def _mixer_out(x, y_lru, y_cf, wout_ref):
    y = _bdot(y_lru, wout_ref[:LRU_WIDTH, :]) + _bdot(y_cf, wout_ref[LRU_WIDTH:, :])
    return x + y


def _mixer_prompt_kernel(x_hbm, gmix_ref, win_ref, wlc_ref, blc_ref, wg_ref, bg_ref, lam_ref,
                         wcc_ref, bcc_ref, gln_ref, bln_ref, wout_ref,
                         x1_hbm, h_ref, ltail_ref, ctail_ref, winb_ref, wgb_ref, woutb_ref,
                         xbuf, obuf, zx_ext, glu_ext, h_carry, isem, osem):
    ts = MIX_STEPS
    nb = xbuf.shape[2]
    rows = ts * nb
    j = pl.program_id(0)
    nj = pl.num_programs(0)
    hl = (LRU_CONV - 1) * nb
    hc = (CF_CONV - 1) * nb

    def copies_in(step, s):
        return [pltpu.make_async_copy(x_hbm.at[b, pl.ds(step * ts, ts)], xbuf.at[s, pl.ds(0, ts), b], isem.at[s])
                for b in range(nb)]

    def copies_out(step, s):
        return [pltpu.make_async_copy(obuf.at[s, pl.ds(0, ts), b], x1_hbm.at[b, pl.ds(step * ts, ts)], osem.at[s])
                for b in range(nb)]

    @pl.when(j == 0)
    def _():
        zx_ext[0:hl, :] = jnp.zeros((hl, LRU_WIDTH), F32)
        glu_ext[0:hc, :] = jnp.zeros((hc, CF_WIDTH), F32)
        h_carry[...] = jnp.zeros(h_carry.shape, F32)
        for cp in copies_in(0, 0):
            cp.start()
        winb_ref[...] = win_ref[...].astype(BF16)
        wgb_ref[...] = wg_ref[...].astype(BF16)
        woutb_ref[...] = wout_ref[...].astype(BF16)

    for s in range(2):
        @pl.when(j % 2 == s)
        def _():
            for cp in copies_in(j, s):
                cp.wait()

            @pl.when(j + 1 < nj)
            def _():
                for cp in copies_in(j + 1, 1 - s):
                    cp.start()

            @pl.when(j >= 2)
            def _():
                for cp in copies_out(j - 2, s):
                    cp.wait()

    x = xbuf[j % 2].reshape(rows, D_MODEL)
    z = _bdot(_rms(x, gmix_ref[...]), winb_ref[...])
    zx = z[:, :LRU_WIDTH]
    zg = z[:, LRU_WIDTH:2 * LRU_WIDTH]
    za = z[:, 2 * LRU_WIDTH:2 * LRU_WIDTH + CF_WIDTH]
    zb = z[:, 2 * LRU_WIDTH + CF_WIDTH:]

    zx_ext[hl:hl + rows, :] = zx
    xc = blc_ref[...]
    for k in range(LRU_CONV):
        xc = xc + wlc_ref[k:k + 1, :] * zx_ext[k * nb:k * nb + rows, :]
    a, u = _lru_coeffs(xc, wgb_ref, bg_ref, lam_ref)
    h = h_carry[...]
    steps = []
    for t in range(ts):
        h = a[t * nb:(t + 1) * nb, :] * h + u[t * nb:(t + 1) * nb, :]
        steps.append(h)
    h_carry[...] = h
    y_lru = jnp.concatenate(steps, axis=0) * jax.nn.gelu(zg)

    glu = za * jax.nn.sigmoid(zb)
    glu_ext[hc:hc + rows, :] = glu
    gc = bcc_ref[...]
    for k in range(CF_CONV):
        gc = gc + wcc_ref[k:k + 1, :] * glu_ext[k * nb:k * nb + rows, :]
    y_cf = _cf_post(gc, gln_ref, bln_ref)

    x1 = _mixer_out(x, y_lru, y_cf, woutb_ref)
    for s in range(2):
        @pl.when(j % 2 == s)
        def _():
            obuf[s] = x1.reshape(ts, nb, D_MODEL)
            for cp in copies_out(j, s):
                cp.start()

            @pl.when(j == nj - 1)
            def _():
                for cp in copies_out(j, s):
                    cp.wait()

                @pl.when(j >= 1)
                def _():
                    for cp in copies_out(j - 1, 1 - s):
                        cp.wait()

    zx_ext[0:hl, :] = zx_ext[rows:rows + hl, :]
    glu_ext[0:hc, :] = glu_ext[rows:rows + hc, :]

    @pl.when(j == nj - 1)
    def _():
        h_ref[...] = h
        for b in range(nb):
            for r in range(LRU_CONV - 1):
                ltail_ref[b, r:r + 1, :] = zx_ext[r * nb + b:r * nb + b + 1, :]
            for r in range(CF_CONV - 1):
                ctail_ref[b, r:r + 1, :] = glu_ext[r * nb + b:r * nb + b + 1, :]


def _mixer_weight_specs(stacked=False):
    dense = _layer0 if stacked else _full
    return [
        _full((1, D_MODEL)),
        dense((D_MODEL, IN_WIDTH)),
        _full((LRU_CONV, LRU_WIDTH)), _full((1, LRU_WIDTH)),
        _full((LRU_WIDTH, 2 * LRU_WIDTH)), _full((1, 2 * LRU_WIDTH)),
        _full((1, LRU_WIDTH)),
        _full((CF_CONV, CF_WIDTH)), _full((1, CF_WIDTH)),
        _full((1, CF_WIDTH)), _full((1, CF_WIDTH)),
        dense((D_MODEL, D_MODEL)),
    ]


def _mixer_prompt(x, mix_w):
    batch, seq, _ = x.shape
    ts = MIX_STEPS
    any_ = pl.BlockSpec(memory_space=pl.ANY)
    return pl.pallas_call(
        _mixer_prompt_kernel,
        grid=(seq // ts,),
        in_specs=[any_] + _mixer_weight_specs(stacked=True),
        out_specs=[
            any_,
            _full((batch, LRU_WIDTH)),
            _full((batch, LRU_CONV - 1, LRU_WIDTH)),
            _full((batch, CF_CONV - 1, CF_WIDTH)),
            _full((D_MODEL, IN_WIDTH)), _full((LRU_WIDTH, 2 * LRU_WIDTH)), _full((D_MODEL, D_MODEL)),
        ],
        out_shape=[
            jax.ShapeDtypeStruct((batch, seq, D_MODEL), F32),
            jax.ShapeDtypeStruct((batch, LRU_WIDTH), F32),
            jax.ShapeDtypeStruct((batch, LRU_CONV - 1, LRU_WIDTH), F32),
            jax.ShapeDtypeStruct((batch, CF_CONV - 1, CF_WIDTH), F32),
            jax.ShapeDtypeStruct((D_MODEL, IN_WIDTH), BF16),
            jax.ShapeDtypeStruct((LRU_WIDTH, 2 * LRU_WIDTH), BF16),
            jax.ShapeDtypeStruct((D_MODEL, D_MODEL), BF16),
        ],
        scratch_shapes=[
            pltpu.VMEM((2, ts, batch, D_MODEL), F32),
            pltpu.VMEM((2, ts, batch, D_MODEL), F32),
            pltpu.VMEM(((LRU_CONV - 1 + ts) * batch, LRU_WIDTH), F32),
            pltpu.VMEM(((CF_CONV - 1 + ts) * batch, CF_WIDTH), F32),
            pltpu.VMEM((batch, LRU_WIDTH), F32),
            pltpu.SemaphoreType.DMA((2,)), pltpu.SemaphoreType.DMA((2,)),
        ],
        compiler_params=_cparams(("arbitrary",)),
        name="mixer_prompt",
    )(x, *mix_w)


def _mixer_sample_kernel(x_ref, gmix_ref, win_ref, wlc_ref, blc_ref, wg_ref, bg_ref, lam_ref,
                         wcc_ref, bcc_ref, gln_ref, bln_ref, wout_ref,
                         h0_ref, lbuf_ref, cbuf_ref,
                         x1_ref, h_ref, ltail_ref, ctail_ref):
    x = x_ref[...]
    z = _bdot(_rms(x, gmix_ref[...]), win_ref[...])
    zx = z[:, :LRU_WIDTH]
    zg = z[:, LRU_WIDTH:2 * LRU_WIDTH]
    za = z[:, 2 * LRU_WIDTH:2 * LRU_WIDTH + CF_WIDTH]
    zb = z[:, 2 * LRU_WIDTH + CF_WIDTH:]

    xc = blc_ref[...] + wlc_ref[LRU_CONV - 1:LRU_CONV, :] * zx
    xc = xc + jnp.sum(lbuf_ref[...] * wlc_ref[0:LRU_CONV - 1, :][None], axis=1)
    a, u = _lru_coeffs(xc, wg_ref, bg_ref, lam_ref)
    h = a * h0_ref[...] + u
    y_lru = h * jax.nn.gelu(zg)

    glu = za * jax.nn.sigmoid(zb)
    gc = bcc_ref[...] + wcc_ref[CF_CONV - 1:CF_CONV, :] * glu
    gc = gc + jnp.sum(cbuf_ref[...] * wcc_ref[0:CF_CONV - 1, :][None], axis=1)
    y_cf = _cf_post(gc, gln_ref, bln_ref)

    x1_ref[...] = _mixer_out(x, y_lru, y_cf, wout_ref)
    h_ref[...] = h
    ltail_ref[:, 0:LRU_CONV - 2, :] = lbuf_ref[:, 1:LRU_CONV - 1, :]
    ctail_ref[:, 0:CF_CONV - 2, :] = cbuf_ref[:, 1:CF_CONV - 1, :]
    for i in range(x.shape[0]):
        ltail_ref[i, LRU_CONV - 2:LRU_CONV - 1, :] = zx[i:i + 1, :]
        ctail_ref[i, CF_CONV - 2:CF_CONV - 1, :] = glu[i:i + 1, :]


def _mixer_sample(x, mix_w, h0, lbuf, cbuf):
    n = x.shape[0]
    g = SAMPLE_MIX_GROUP
    row = lambda w: pl.BlockSpec((g, w), lambda i: (i, 0))
    return pl.pallas_call(
        _mixer_sample_kernel,
        grid=(n // g,),
        in_specs=[row(D_MODEL)] + _mixer_weight_specs() + [
            row(LRU_WIDTH),
            pl.BlockSpec((g, LRU_CONV - 1, LRU_WIDTH), lambda i: (i, 0, 0)),
            pl.BlockSpec((g, CF_CONV - 1, CF_WIDTH), lambda i: (i, 0, 0)),
        ],
        out_specs=[row(D_MODEL), row(LRU_WIDTH),
                   pl.BlockSpec((g, LRU_CONV - 1, LRU_WIDTH), lambda i: (i, 0, 0)),
                   pl.BlockSpec((g, CF_CONV - 1, CF_WIDTH), lambda i: (i, 0, 0))],
        out_shape=[
            jax.ShapeDtypeStruct((n, D_MODEL), F32),
            jax.ShapeDtypeStruct((n, LRU_WIDTH), F32),
            jax.ShapeDtypeStruct((n, LRU_CONV - 1, LRU_WIDTH), F32),
            jax.ShapeDtypeStruct((n, CF_CONV - 1, CF_WIDTH), F32),
        ],
        compiler_params=_cparams(("arbitrary",)),
        name="mixer_sample",
    )(x, *mix_w, h0, lbuf, cbuf)


def _memkv_kernel(mem_ref, gmem_ref, wk_ref, wv_ref, k_ref, v_ref, kb_ref, vb_ref, wkb, wvb):
    @pl.when(pl.program_id(0) == 0)
    def _():
        wkb[...] = wk_ref[...].astype(BF16)
        wvb[...] = wv_ref[...].astype(BF16)

    mn = _rms(mem_ref[...], gmem_ref[...]).astype(BF16)
    k = jnp.dot(mn, wkb[...], preferred_element_type=F32)
    v = jnp.dot(mn, wvb[...], preferred_element_type=F32)
    for h in range(XA_HEADS):
        sl = slice(h * XA_HEAD_DIM, (h + 1) * XA_HEAD_DIM)
        k_ref[0, :, h, :] = k[:, sl]
        v_ref[0, :, h, :] = v[:, sl]
    kb_ref[...] = k.astype(BF16)
    vb_ref[...] = v.astype(BF16)


def _memkv(mem, g_mem, wk, wv):
    n = mem.shape[0]
    t = N_MEM
    row = pl.BlockSpec((t, D_MODEL), lambda i: (i, 0))
    state = pl.BlockSpec((1, N_MEM, XA_HEADS, XA_HEAD_DIM), lambda i: (i, 0, 0, 0))
    state_shape = jax.ShapeDtypeStruct((n // t, N_MEM, XA_HEADS, XA_HEAD_DIM), F32)
    return pl.pallas_call(
        _memkv_kernel,
        grid=(n // t,),
        in_specs=[row, _full((1, D_MODEL)), _layer0((D_MODEL, D_MODEL)), _layer0((D_MODEL, D_MODEL))],
        out_specs=[state, state, row, row],
        out_shape=[state_shape] * 2 + [jax.ShapeDtypeStruct((n, D_MODEL), BF16)] * 2,
        scratch_shapes=[pltpu.VMEM((D_MODEL, D_MODEL), BF16)] * 2,
        compiler_params=_cparams(("arbitrary",)),
        name="memory_kv",
    )(mem, g_mem, wk, wv)


def _router(x2, gmoe_ref, wrt_ref, brt_ref, xn_ref, tope_ref, gates_ref):
    xn = _rms(x2, gmoe_ref[...])
    _store_row_tiles(xn_ref, xn)
    logits = lax.dot_general(wrt_ref[...], xn.astype(BF16), (((1,), (1,)), ((), ())),
                             preferred_element_type=F32) + brt_ref[...]
    e_iota = lax.broadcasted_iota(I32, logits.shape, 0)
    work = logits
    vals, idxs = [], []
    for _ in range(TOP_K):
        m = jnp.max(work, axis=0, keepdims=True)
        idx = jnp.min(jnp.where(work == m, e_iota, N_EXPERTS), axis=0, keepdims=True)
        vals.append(m)
        idxs.append(idx)
        work = jnp.where(e_iota == idx, -jnp.inf, work)
    ex = [jnp.exp(v - vals[0]) for v in vals]
    den = ex[0] + ex[1] + ex[2] + ex[3]
    tope_ref[...] = jnp.concatenate(idxs, axis=0)
    gates_ref[...] = jnp.concatenate([e / den for e in ex], axis=0)


def _router_specs():
    return [_full((1, D_MODEL)), _full((N_EXPERTS, D_MODEL)), _full((N_EXPERTS, 1))]


def _router_out(n, tile, index):
    specs = [
        pl.BlockSpec((tile, D_MODEL), lambda *a: (index(*a), 0)),
        pl.BlockSpec((tile * ROW_TILE, LANES), lambda *a: (index(*a), 0)),
        pl.BlockSpec((TOP_K, tile), lambda *a: (0, index(*a))),
        pl.BlockSpec((TOP_K, tile), lambda *a: (0, index(*a))),
    ]
    shapes = [
        jax.ShapeDtypeStruct((n, D_MODEL), F32),
        jax.ShapeDtypeStruct((n * ROW_TILE, LANES), F32),
        jax.ShapeDtypeStruct((TOP_K, n), I32),
        jax.ShapeDtypeStruct((TOP_K, n), F32),
    ]
    return specs, shapes


def _attn_prompt_kernel(x1_ref, k_ref, v_ref, gxa_ref, wq_ref, wo_ref, gmoe_ref, wrt_ref, brt_ref,
                        x2_ref, xn_ref, tope_ref, gates_ref, wqb_ref, wob_ref):
    @pl.when(jnp.logical_and(pl.program_id(0) == 0, pl.program_id(1) == 0))
    def _():
        wqb_ref[...] = wq_ref[...].astype(BF16)
        wob_ref[...] = wo_ref[...].astype(BF16)

    x1 = x1_ref[...]
    q = (_bdot(_rms(x1, gxa_ref[...]), wqb_ref[...]) * (XA_HEAD_DIM ** -0.5)).astype(BF16)
    outs = []
    for h in range(XA_HEADS):
        sl = slice(h * XA_HEAD_DIM, (h + 1) * XA_HEAD_DIM)
        s = lax.dot_general(q[:, sl], k_ref[:, sl], (((1,), (1,)), ((), ())), preferred_element_type=F32)
        p = jnp.exp(s - jnp.max(s, axis=-1, keepdims=True))
        p = p / jnp.sum(p, axis=-1, keepdims=True)
        outs.append(_bdot(p, v_ref[:, sl]))
    o = jnp.concatenate(outs, axis=-1)
    x2 = x1 + _bdot(o, wob_ref[...])
    x2_ref[...] = x2
    _router(x2, gmoe_ref, wrt_ref, brt_ref, xn_ref, tope_ref, gates_ref)


def _attn_prompt(x1, k, v, gxa, wq, wo, router_w, batch, seq):
    tq = ATT_TILE
    nt = seq // tq
    out_specs, out_shapes = _router_out(batch * seq, tq, lambda b, j: b * nt + j)
    kv = pl.BlockSpec((N_MEM, D_MODEL), lambda b, j: (b, 0))
    return pl.pallas_call(
        _attn_prompt_kernel,
        grid=(batch, nt),
        in_specs=[pl.BlockSpec((tq, D_MODEL), lambda b, j: (b * nt + j, 0)), kv, kv,
                  _full((1, D_MODEL)), _layer0((D_MODEL, D_MODEL)), _layer0((D_MODEL, D_MODEL))] + _router_specs(),
        out_specs=out_specs + [_full((D_MODEL, D_MODEL))] * 2,
        out_shape=out_shapes + [jax.ShapeDtypeStruct((D_MODEL, D_MODEL), BF16)] * 2,
        compiler_params=_cparams(("arbitrary", "arbitrary")),
        name="attn_prompt",
    )(x1, k, v, gxa, wq, wo, *router_w)


def _qproj_kernel(x1_ref, gxa_ref, wq_ref, q_ref):
    q_ref[...] = _bdot(_rms(x1_ref[...], gxa_ref[...]), wq_ref[...]) * (XA_HEAD_DIM ** -0.5)


def _attn_sample_core_kernel(q_ref, k_ref, v_ref, o_ref):
    n = N_MEM * SUBLANES
    col_head = lax.broadcasted_iota(I32, (SUBLANES, n), 1) & (SUBLANES - 1)
    same_head = col_head == lax.broadcasted_iota(I32, (SUBLANES, n), 0)
    kv_pad = jnp.zeros((N_MEM, SUBLANES - XA_HEADS, XA_HEAD_DIM), F32)
    q_pad = jnp.zeros((SUBLANES - XA_HEADS, XA_HEAD_DIM), F32)
    for g in range(q_ref.shape[0]):
        k8 = jnp.concatenate([k_ref[g], kv_pad], axis=1).reshape(n, XA_HEAD_DIM).astype(BF16)
        v8 = jnp.concatenate([v_ref[g], kv_pad], axis=1).reshape(n, XA_HEAD_DIM).astype(BF16)
        q8 = jnp.concatenate([q_ref[g], q_pad], axis=0).astype(BF16)
        s = lax.dot_general(q8, k8, (((1,), (1,)), ((), ())), preferred_element_type=F32)
        m = jnp.max(jnp.where(same_head, s, -jnp.inf), axis=-1, keepdims=True)
        p = jnp.where(same_head, jnp.exp(s - m), 0.0)
        den = jnp.sum(p, axis=-1, keepdims=True)
        o8 = jnp.dot(p.astype(BF16), v8, preferred_element_type=F32)
        o_ref[g] = (o8 / den)[0:XA_HEADS, :]


def _oproj_router_kernel(x1_ref, o_ref, wo_ref, gmoe_ref, wrt_ref, brt_ref,
                         x2_ref, xn_ref, tope_ref, gates_ref):
    x2 = x1_ref[...] + _bdot(o_ref[...], wo_ref[...])
    x2_ref[...] = x2
    _router(x2, gmoe_ref, wrt_ref, brt_ref, xn_ref, tope_ref, gates_ref)


def _attn_sample(x1, k, v, gxa, wq, wo, router_w):
    n = x1.shape[0]
    g = SAMPLE_GROUP
    q = pl.pallas_call(
        _qproj_kernel,
        grid=(1,),
        in_specs=[_full((n, D_MODEL)), _full((1, D_MODEL)), _full((D_MODEL, D_MODEL))],
        out_specs=_full((n, D_MODEL)),
        out_shape=jax.ShapeDtypeStruct((n, D_MODEL), F32),
        compiler_params=_cparams(("arbitrary",)),
        name="qproj_sample",
    )(x1, gxa, wq)
    kv = pl.BlockSpec((g, N_MEM, XA_HEADS, XA_HEAD_DIM), lambda i: (i, 0, 0, 0))
    row = pl.BlockSpec((g, XA_HEADS, XA_HEAD_DIM), lambda i: (i, 0, 0))
    o = pl.pallas_call(
        _attn_sample_core_kernel,
        grid=(n // g,),
        in_specs=[row, kv, kv],
        out_specs=row,
        out_shape=jax.ShapeDtypeStruct((n, XA_HEADS, XA_HEAD_DIM), F32),
        compiler_params=_cparams(("arbitrary",)),
        name="attn_sample_core",
    )(q.reshape(n, XA_HEADS, XA_HEAD_DIM), k, v).reshape(n, D_MODEL)
    out_specs, out_shapes = _router_out(n, n, lambda i: 0)
    return pl.pallas_call(
        _oproj_router_kernel,
        grid=(1,),
        in_specs=[_full((n, D_MODEL)), _full((n, D_MODEL)), _full((D_MODEL, D_MODEL))] + _router_specs(),
        out_specs=out_specs,
        out_shape=out_shapes,
        compiler_params=_cparams(("arbitrary",)),
        name="oproj_router_sample",
    )(x1, o, wo, *router_w)


def _moe_num_blocks(n_assign):
    nb = pl.cdiv(n_assign, MOE_BLOCK) + N_EXPERTS
    return pl.cdiv(nb, EXPERT_STEP_BLOCKS) * EXPERT_STEP_BLOCKS


def _slots_kernel(tp_ref, ts_ref, destp_ref, dests_ref, blke_ref, nact_ref, cnt_ref, pstart_ref,
                  rankp, ranks):
    def make_body(top_ref, rank_ref):
        c = min(SLOT_CHUNK, top_ref.shape[1])
        r_i = lax.broadcasted_iota(I32, (c, c), 0)
        c_i = lax.broadcasted_iota(I32, (c, c), 1)
        upper = (r_i < c_i).astype(BF16)
        e_iota = lax.broadcasted_iota(I32, (N_EXPERTS, c), 0)

        def body(ci, carry):
            lo = pl.multiple_of(ci * c, c)
            top = top_ref[:, pl.ds(lo, c)]
            hit = [e_iota == top[k:k + 1, :] for k in range(TOP_K)]
            cnt = sum(m.astype(F32) for m in hit)
            base = carry + jnp.dot(cnt.astype(BF16), upper, preferred_element_type=F32)
            rank_ref[:, pl.ds(lo, c)] = jnp.concatenate(
                [jnp.sum(jnp.where(m, base, 0.0), axis=0, keepdims=True) for m in hit], axis=0)
            return carry + jnp.sum(cnt, axis=1, keepdims=True)
        return top_ref.shape[1] // c, body

    counts = jnp.zeros((N_EXPERTS, 1), F32)
    for top_ref, rank_ref in ((tp_ref, rankp), (ts_ref, ranks)):
        trips, body = make_body(top_ref, rank_ref)
        counts = lax.fori_loop(0, trips, body, counts)

    padded = jnp.floor((counts + (MOE_BLOCK - 1)) / MOE_BLOCK) * MOE_BLOCK
    er = lax.broadcasted_iota(I32, (N_EXPERTS, N_EXPERTS), 0)
    ec = lax.broadcasted_iota(I32, (N_EXPERTS, N_EXPERTS), 1)
    padded_row = jnp.sum(jnp.where(er == ec, padded, 0.0), axis=0, keepdims=True)
    cum = jnp.sum(jnp.where(ec <= er, padded_row, 0.0), axis=1, keepdims=True)
    pstart = cum - padded
    cnt_ref[...] = jnp.sum(jnp.where(er == ec, counts, 0.0), axis=0, keepdims=True).astype(I32)
    pstart_ref[...] = jnp.sum(jnp.where(er == ec, pstart, 0.0), axis=0, keepdims=True).astype(I32)

    def dest(top_ref, rank_ref, out_ref):
        n = top_ref.shape[1]
        ei = lax.broadcasted_iota(I32, (N_EXPERTS, n), 0)
        rows = []
        for k in range(TOP_K):
            start = jnp.sum(jnp.where(ei == top_ref[k:k + 1, :], pstart, 0.0), axis=0, keepdims=True)
            rows.append(start + rank_ref[k:k + 1, :])
        out_ref[...] = jnp.concatenate(rows, axis=0).astype(I32)

    dest(tp_ref, rankp, destp_ref)
    dest(ts_ref, ranks, dests_ref)

    nb = blke_ref.shape[1]
    blk_lo = lax.broadcasted_iota(I32, (N_EXPERTS, nb), 1).astype(F32) * MOE_BLOCK
    blk = jnp.sum((cum <= blk_lo).astype(F32), axis=0, keepdims=True)
    blke_ref[...] = jnp.minimum(blk, N_EXPERTS - 1).astype(I32)
    nact_ref[...] = (cum[N_EXPERTS - 1:N_EXPERTS, :] / MOE_BLOCK).astype(I32)


def _slots(top_p, top_s):
    n_p, n_s = top_p.shape[1], top_s.shape[1]
    nb = _moe_num_blocks((n_p + n_s) * TOP_K)
    return pl.pallas_call(
        _slots_kernel,
        grid=(1,),
        in_specs=[_full((TOP_K, n_p)), _full((TOP_K, n_s))],
        out_specs=[_full((TOP_K, n_p)), _full((TOP_K, n_s)), _full((1, nb)), _full((1, 1)),
                   _full((1, N_EXPERTS)), _full((1, N_EXPERTS))],
        out_shape=[
            jax.ShapeDtypeStruct((TOP_K, n_p), I32),
            jax.ShapeDtypeStruct((TOP_K, n_s), I32),
            jax.ShapeDtypeStruct((1, nb), I32),
            jax.ShapeDtypeStruct((1, 1), I32),
            jax.ShapeDtypeStruct((1, N_EXPERTS), I32),
            jax.ShapeDtypeStruct((1, N_EXPERTS), I32),
        ],
        scratch_shapes=[pltpu.VMEM((TOP_K, n_p), F32), pltpu.VMEM((TOP_K, n_s), F32)],
        compiler_params=_cparams(("arbitrary",)),
        name="moe_slots",
    )(top_p, top_s)


def _for_rows(n_rows, fn):
    def body(i, c):
        for u in range(DMA_UNROLL):
            fn(i * DMA_UNROLL + u)
        return c
    lax.fori_loop(0, n_rows // DMA_UNROLL, body, 0)


def _dispatch_kernel(cnt_ref, pstart_ref, nact_ref, destp_ref, dests_ref, xnp_ref, xns_ref, xs_ref,
                     zeros, sem, zsem):
    i = pl.program_id(0)
    last = pl.num_programs(0) - 1

    def scatter(x_ref, dest_ref):
        n = x_ref.shape[0] // ROW_TILE

        def one(t):
            for k in range(TOP_K):
                _row_tile_copy(x_ref, t, xs_ref, dest_ref[t * TOP_K + k], sem).start(priority=k % 2)

        _for_rows(n, one)
        _row_tile_copy(xs_ref, 0, xs_ref, 0, sem, n * TOP_K).wait()

    @pl.when(i < last)
    def _():
        scatter(xnp_ref, destp_ref)

    @pl.when(i == last)
    def _():
        scatter(xns_ref, dests_ref)
        zeros[...] = jnp.zeros(zeros.shape, F32)

        def pad_copies(e, act):
            c = cnt_ref[e]
            npad = (-c) & (MOE_BLOCK - 1)
            base = pstart_ref[e] + c
            for bit in range(MOE_BLOCK.bit_length() - 1):
                size = 1 << bit

                @pl.when((npad & size) != 0)
                def _():
                    act(_row_tile_copy(zeros, 0, xs_ref, base + (npad & (size - 1)), zsem, size))

        def tail_copies(act):
            def body(j, c):
                act(_row_tile_copy(zeros, 0, xs_ref, j * MOE_BLOCK, zsem, MOE_BLOCK))
                return c
            lax.fori_loop(nact_ref[0], xs_ref.shape[0] // (MOE_BLOCK * ROW_TILE), body, 0)

        for act in (lambda d: d.start(), lambda d: d.wait()):
            lax.fori_loop(0, N_EXPERTS, lambda e, c, act=act: (pad_copies(e, act), c)[1], 0)
            tail_copies(act)


def _dispatch(cnt, pstart, nact, dest_p, dest_s, xn_p, xn_s, n_rows):
    n_p, n_s = xn_p.shape[0] // ROW_TILE, xn_s.shape[0] // ROW_TILE
    tile = DISPATCH_TILE
    nt = n_p // tile
    smem = pl.BlockSpec(memory_space=pltpu.SMEM)
    return pl.pallas_call(
        _dispatch_kernel,
        grid=(nt + 1,),
        in_specs=[
            smem, smem, smem,
            pl.BlockSpec((TOP_K * tile,), lambda i: (jnp.minimum(i, nt - 1),), memory_space=pltpu.SMEM),
            smem,
            pl.BlockSpec((tile * ROW_TILE, LANES), lambda i: (jnp.minimum(i, nt - 1), 0)),
            _full((n_s * ROW_TILE, LANES)),
        ],
        out_specs=pl.BlockSpec(memory_space=pl.ANY),
        out_shape=jax.ShapeDtypeStruct((n_rows * ROW_TILE, LANES), F32),
        scratch_shapes=[pltpu.VMEM((MOE_BLOCK * ROW_TILE, LANES), F32), pltpu.SemaphoreType.DMA,
                        pltpu.SemaphoreType.DMA],
        compiler_params=_cparams(("arbitrary",)),
        name="moe_dispatch",
    )(cnt, pstart, nact, dest_p, dest_s, xn_p, xn_s)


def _combine_kernel(dest_ref, dest_next_ref, gates_ref, x2_ref, gfin_ref, ys_ref, y_ref, bufs, sems):
    tile = x2_ref.shape[0]
    i = pl.program_id(0)
    slot = i % 2

    def gather(d_ref, b):
        def one(t):
            for k in range(TOP_K):
                _row_tile_copy(ys_ref, d_ref[t * TOP_K + k], bufs.at[b, k], t, sems.at[b]).start(priority=k % 2)
        _for_rows(tile, one)

    @pl.when(i == 0)
    def _():
        gather(dest_ref, 0)

    for b in range(2):
        @pl.when(slot == b)
        def _():
            pltpu.make_async_copy(bufs.at[b], bufs.at[b], sems.at[b]).wait()

            @pl.when(i + 1 < pl.num_programs(0))
            def _():
                gather(dest_next_ref, 1 - b)

    buf = bufs.at[slot]
    sumsq = jnp.zeros((tile, 1), F32)
    gate = [jnp.broadcast_to(gates_ref[:, k:k + 1], (tile, LANES)) for k in range(TOP_K)]
    for s in range(ROW_TILE):
        cols = slice(s * LANES, (s + 1) * LANES)
        acc = x2_ref[:, cols]
        for k in range(TOP_K):
            acc = acc + gate[k] * buf[k, pl.ds(s, tile, stride=ROW_TILE), :]
        y_ref[:, cols] = acc
        sumsq = sumsq + jnp.sum(acc * acc, axis=-1, keepdims=True)
    scale = lax.rsqrt(sumsq / D_MODEL + EPS)
    for s in range(ROW_TILE):
        cols = slice(s * LANES, (s + 1) * LANES)
        y_ref[:, cols] = y_ref[:, cols] * scale * gfin_ref[:, cols]


def _combine(dest, gates_t, x2, g_final, ys):
    n = x2.shape[0]
    tile = min(COMBINE_TILE, n)
    row = pl.BlockSpec((tile, D_MODEL), lambda i: (i, 0))
    nt = n // tile
    return pl.pallas_call(
        _combine_kernel,
        grid=(nt,),
        in_specs=[
            pl.BlockSpec((TOP_K * tile,), lambda i: (i,), memory_space=pltpu.SMEM),
            pl.BlockSpec((TOP_K * tile,), lambda i: (jnp.minimum(i + 1, nt - 1),), memory_space=pltpu.SMEM),
            pl.BlockSpec((tile, TOP_K), lambda i: (i, 0)),
            row,
            _full((1, D_MODEL)),
            pl.BlockSpec(memory_space=pl.ANY),
        ],
        out_specs=row,
        out_shape=jax.ShapeDtypeStruct((n, D_MODEL), F32),
        scratch_shapes=[pltpu.VMEM((2, TOP_K, tile * ROW_TILE, LANES), F32), pltpu.SemaphoreType.DMA((2,))],
        compiler_params=_cparams(("arbitrary",)),
        name="moe_combine",
    )(dest, dest, gates_t, x2, g_final, ys)


def _experts_kernel(blke_ref, nact_ref, xs_ref, w1_hbm, w2_hbm, *rest):
    b1_refs, b2_refs = rest[:EXPERT_STEP_BLOCKS], rest[EXPERT_STEP_BLOCKS:2 * EXPERT_STEP_BLOCKS]
    ys_ref, w1f, w2f, w1b, w2b, wsem, started = rest[2 * EXPERT_STEP_BLOCKS:]
    j = pl.program_id(0)
    nact = nact_ref[0]
    last = blke_ref.shape[0] - 1

    def fetch(expert, slot):
        return (pltpu.make_async_copy(w1_hbm.at[expert], w1f.at[slot], wsem.at[slot]),
                pltpu.make_async_copy(w2_hbm.at[expert], w2f.at[slot], wsem.at[slot]))

    def switch_to(blk):
        e = blke_ref[blk]
        n = started[0]
        jn = lax.while_loop(lambda jj: jnp.logical_and(jj < nact, blke_ref[jnp.minimum(jj, last)] == e),
                            lambda jj: jj + 1, blk + 1)
        for b in range(2):
            @pl.when(n % 2 == b)
            def _():
                for d in fetch(e, b):
                    d.wait()

                @pl.when(jn < nact)
                def _():
                    for d in fetch(blke_ref[jnp.minimum(jn, last)], 1 - b):
                        d.start(priority=1)

                w1b[...] = w1f[b].astype(BF16)
                w2b[...] = w2f[b].astype(BF16)
        started[0] = n + 1

    def ffn(row0, n_rows, b1_ref, b2_ref):
        rows = pl.ds(row0 * ROW_TILE, n_rows * ROW_TILE)
        x = _load_row_tiles(xs_ref.at[rows], n_rows).astype(BF16)
        gu = jnp.dot(x, w1b[...], preferred_element_type=F32) + b1_ref[0]
        g = jnp.minimum(gu[:, :D_EXPERT], SWIGLU_LIMIT)
        u = jnp.clip(gu[:, D_EXPERT:], -SWIGLU_LIMIT, SWIGLU_LIMIT)
        h = (u + 1.0) * (g * jax.nn.sigmoid(SWIGLU_ALPHA * g))
        _store_row_tiles(ys_ref.at[rows], _bdot(h, w2b[...]) + b2_ref[0])

    def zero(row0, n_rows):
        ys_ref[pl.ds(row0 * ROW_TILE, n_rows * ROW_TILE), :] = jnp.zeros((n_rows * ROW_TILE, LANES), F32)

    @pl.when(j == 0)
    def _():
        started[0] = 0
        for d in fetch(blke_ref[0], 0):
            d.start()

    for p in range(0, EXPERT_STEP_BLOCKS, 2):
        first, second = EXPERT_STEP_BLOCKS * j + p, EXPERT_STEP_BLOCKS * j + p + 1
        e0, e1 = blke_ref[first], blke_ref[second]
        active0, active1 = first < nact, second < nact
        together = jnp.logical_and(active1, e1 == e0)
        row0 = p * MOE_BLOCK

        @pl.when(jnp.logical_and(active0, jnp.logical_or(first == 0, e0 != blke_ref[jnp.maximum(first - 1, 0)])))
        def _():
            switch_to(first)

        @pl.when(together)
        def _():
            ffn(row0, 2 * MOE_BLOCK, b1_refs[p], b2_refs[p])

        @pl.when(jnp.logical_and(active0, jnp.logical_not(together)))
        def _():
            ffn(row0, MOE_BLOCK, b1_refs[p], b2_refs[p])

            @pl.when(active1)
            def _():
                switch_to(second)
                ffn(row0 + MOE_BLOCK, MOE_BLOCK, b1_refs[p + 1], b2_refs[p + 1])

            @pl.when(jnp.logical_not(active1))
            def _():
                zero(row0 + MOE_BLOCK, MOE_BLOCK)

        @pl.when(jnp.logical_not(active0))
        def _():
            zero(row0, 2 * MOE_BLOCK)


def _experts(blk_e, nact, xs, w1, b1, w2, b2):
    nb = xs.shape[0] // (MOE_BLOCK * ROW_TILE)
    step = EXPERT_STEP_BLOCKS
    assert nb % step == 0 and step % 2 == 0
    blk = lambda j, be, na: (jnp.minimum(j, (na[0] - 1) // step), 0)
    exp = lambda h: (lambda j, be, na: (be[jnp.minimum(step * j + h, na[0] - 1)], 0, 0))
    rows = (step * MOE_BLOCK * ROW_TILE, LANES)
    any_ = pl.BlockSpec(memory_space=pl.ANY)
    return pl.pallas_call(
        _experts_kernel,
        grid_spec=pltpu.PrefetchScalarGridSpec(
            num_scalar_prefetch=2,
            grid=(nb // step,),
            in_specs=[pl.BlockSpec(rows, blk), any_, any_]
            + [pl.BlockSpec((1, 1, 2 * D_EXPERT), exp(h)) for h in range(step)]
            + [pl.BlockSpec((1, 1, D_MODEL), exp(h)) for h in range(step)],
            out_specs=pl.BlockSpec(rows, lambda j, be, na: (j, 0)),
            scratch_shapes=[
                pltpu.VMEM((2, D_MODEL, 2 * D_EXPERT), F32), pltpu.VMEM((2, D_EXPERT, D_MODEL), F32),
                pltpu.VMEM((D_MODEL, 2 * D_EXPERT), BF16), pltpu.VMEM((D_EXPERT, D_MODEL), BF16),
                pltpu.SemaphoreType.DMA((2,)), pltpu.SMEM((1,), I32),
            ],
        ),
        out_shape=jax.ShapeDtypeStruct(xs.shape, F32),
        compiler_params=_cparams(("arbitrary",)),
        name="moe_experts",
    )(blk_e, nact, xs, w1, w2, *([b1] * step), *([b2] * step))


def _block_diag(w):
    h, d, _ = w.shape
    eye = jnp.eye(h, dtype=w.dtype)
    return (eye[:, None, :, None] * w[:, :, None, :]).reshape(h * d, h * d)


def kernel(x_prompt, x_sample, state_lru_h, cache_lru_conv, cache_cf_conv, cache_mem_k, cache_mem_v,
           mem_prompt, g_mix, w_in, w_lru_conv, b_lru_conv, w_lru_a, b_lru_a, w_lru_x, b_lru_x,
           lru_lambda, w_cf_conv, b_cf_conv, g_cf_ln, b_cf_ln, w_out, g_xa, g_mem, w_q, w_k, w_v, w_o,
           g_moe, w_router, b_router, w_e1, b_e1, w_e2, b_e2, g_final):
    depth = g_mix.shape[0]
    assert depth == 1, "single-layer stack"
    batch, seq, _ = x_prompt.shape
    n_s = x_sample.shape[0]
    assert x_sample.shape[1] == 1
    n_p = batch * seq
    l = 0

    mix_w = (
        g_mix[l][None], w_in, w_lru_conv[l], b_lru_conv[l][None],
        jnp.concatenate([_block_diag(w_lru_a[l]), _block_diag(w_lru_x[l])], axis=1),
        jnp.concatenate([b_lru_a[l], b_lru_x[l]])[None], lru_lambda[l][None],
        w_cf_conv[l], b_cf_conv[l][None], g_cf_ln[l][None], b_cf_ln[l][None], w_out,
    )
    router_w = (g_moe[l][None], w_router[l].T.astype(BF16), b_router[l][:, None])
    gxa = g_xa[l][None]

    xs_ = x_sample.reshape(n_s, D_MODEL)
    x1p, hp, lcp, ccp, win_b, wg_b, wout_b = _mixer_prompt(x_prompt, mix_w)
    x1p = x1p.reshape(n_p, D_MODEL)
    mix_w_b = mix_w[:1] + (win_b,) + mix_w[2:4] + (wg_b,) + mix_w[5:-1] + (wout_b,)
    x1s, hs, lru_conv_s, cf_conv_s = _mixer_sample(xs_, mix_w_b, state_lru_h[l], cache_lru_conv[l], cache_cf_conv[l])

    mk, mv, mkb, mvb = _memkv(mem_prompt.reshape(batch * N_MEM, D_MODEL), g_mem[l][None],
                              w_k, w_v)
    x2p, xnp_, top_p, gates_p, wq, wo = _attn_prompt(x1p, mkb, mvb, gxa, w_q, w_o, router_w, batch, seq)
    x2s, xns, top_s, gates_s = _attn_sample(x1s, cache_mem_k[l], cache_mem_v[l], gxa, wq, wo, router_w)

    dest_p, dest_s, blk_e, nact, cnt, pstart = _slots(top_p, top_s)
    dest_p, dest_s = dest_p.T.reshape(-1), dest_s.T.reshape(-1)
    nb = _moe_num_blocks((n_p + n_s) * TOP_K)
    nact = nact.reshape(1)
    xs = _dispatch(cnt.reshape(N_EXPERTS), pstart.reshape(N_EXPERTS), nact, dest_p, dest_s, xnp_, xns,
                   nb * MOE_BLOCK)
    ys = _experts(blk_e.reshape(nb), nact, xs, w_e1[l], b_e1[l][:, None, :], w_e2[l], b_e2[l][:, None, :])
    gfin = g_final[None]
    y_p = _combine(dest_p, gates_p.T, x2p, gfin, ys)
    y_s = _combine(dest_s, gates_s.T, x2s, gfin, ys)

    return (
        y_p.reshape(batch, seq, D_MODEL),
        y_s.reshape(n_s, 1, D_MODEL),
        hp[None],
        lcp[None],
        ccp[None],
        mk[None],
        mv[None],
        hs[None],
        lru_conv_s[None],
        cf_conv_s[None],
    )
```

```python
import jax
import jax.numpy as jnp
from jax import lax
from jax.experimental import pallas as pl
from jax.experimental.pallas import tpu as pltpu

F32 = jnp.float32
BF16 = jnp.bfloat16
I32 = jnp.int32

D_MODEL = 1024
LRU_WIDTH = 512
CF_WIDTH = 512
LRU_CONV = 4
LRU_C = 8.0
CF_CONV = 31
IN_WIDTH = 2 * LRU_WIDTH + 2 * CF_WIDTH
N_MEM = 256
XA_HEADS = 4
XA_HEAD_DIM = D_MODEL // XA_HEADS
N_EXPERTS = 32
TOP_K = 4
D_EXPERT = D_MODEL
SWIGLU_LIMIT = 7.0
SWIGLU_ALPHA = 1.702
EPS = 1e-6

LANES = 128
SUBLANES = 8
ROW_TILE = D_MODEL // LANES
assert ROW_TILE == SUBLANES
VMEM_LIMIT = 56 * 1024 * 1024

MIX_STEPS = 128
ATT_TILE = 1024
SAMPLE_GROUP = 4
SAMPLE_MIX_GROUP = 32
MOE_BLOCK = 256
EXPERT_STEP_BLOCKS = 4
SLOT_CHUNK = 256
DMA_UNROLL = 16
DISPATCH_TILE = 4096
COMBINE_TILE = 256


def _cparams(sem):
    return pltpu.CompilerParams(dimension_semantics=sem, vmem_limit_bytes=VMEM_LIMIT)


def _full(shape):
    n = len(shape)
    return pl.BlockSpec(shape, lambda *_: (0,) * n)


def _layer0(shape):
    n = len(shape)
    return pl.BlockSpec((None,) + tuple(shape), lambda *_: (0,) * (n + 1))


def _rms(x, g):
    return x * lax.rsqrt(jnp.mean(x * x, axis=-1, keepdims=True) + EPS) * g


def _bdot(a, b):
    return jnp.dot(a.astype(BF16), b, preferred_element_type=F32)


def _store_row_tiles(ref, x):
    m = x.shape[0]
    for s in range(ROW_TILE):
        ref[pl.ds(s, m, stride=ROW_TILE), :] = x[:, s * LANES:(s + 1) * LANES]


def _load_row_tiles(ref, m):
    return jnp.concatenate([ref[pl.ds(s, m, stride=ROW_TILE), :] for s in range(ROW_TILE)], axis=-1)


def _row_tile_copy(src_ref, src_row, dst_ref, dst_row, sem, n_rows=1):
    src = src_ref.at[pl.ds(pl.multiple_of(src_row * ROW_TILE, ROW_TILE), n_rows * ROW_TILE)]
    dst = dst_ref.at[pl.ds(pl.multiple_of(dst_row * ROW_TILE, ROW_TILE), n_rows * ROW_TILE)]
    return pltpu.make_async_copy(src, dst, sem)


def _lru_coeffs(xc, wg_ref, bg_ref, lam_ref):
    gl = _bdot(xc, wg_ref[...]) + bg_ref[...]
    r = jax.nn.sigmoid(gl[:, :LRU_WIDTH])
    i = jax.nn.sigmoid(gl[:, LRU_WIDTH:])
    log_a = -LRU_C * r * jax.nn.softplus(-lam_ref[...])
    a = jnp.exp(log_a)
    u = jnp.sqrt(-jnp.tanh(log_a) * (1.0 + a * a)) * (i * xc)
    return a, u


def _cf_post(gc, gln_ref, bln_ref):
    mu = jnp.mean(gc, axis=-1, keepdims=True)
    c = gc - mu
    y = c * lax.rsqrt(jnp.mean(c * c, axis=-1, keepdims=True) + EPS)
    return jax.nn.silu(y * gln_ref[...] + bln_ref[...])


def _mixer_out(x, y_lru, y_cf, wout_ref):
    y = _bdot(y_lru, wout_ref[:LRU_WIDTH, :]) + _bdot(y_cf, wout_ref[LRU_WIDTH:, :])
    return x + y


def _mixer_prompt_kernel(x_hbm, gmix_ref, win_ref, wlc_ref, blc_ref, wg_ref, bg_ref, lam_ref,
                         wcc_ref, bcc_ref, gln_ref, bln_ref, wout_ref,
                         x1_hbm, h_ref, ltail_ref, ctail_ref, winb_ref, wgb_ref, woutb_ref,
                         xbuf, obuf, zx_ext, glu_ext, h_carry, isem, osem):
    ts = MIX_STEPS
    nb = xbuf.shape[2]
    rows = ts * nb
    j = pl.program_id(0)
    nj = pl.num_programs(0)
    hl = (LRU_CONV - 1) * nb
    hc = (CF_CONV - 1) * nb

    def copies_in(step, s):
        return [pltpu.make_async_copy(x_hbm.at[b, pl.ds(step * ts, ts)], xbuf.at[s, pl.ds(0, ts), b], isem.at[s])
                for b in range(nb)]

    def copies_out(step, s):
        return [pltpu.make_async_copy(obuf.at[s, pl.ds(0, ts), b], x1_hbm.at[b, pl.ds(step * ts, ts)], osem.at[s])
                for b in range(nb)]

    @pl.when(j == 0)
    def _():
        zx_ext[0:hl, :] = jnp.zeros((hl, LRU_WIDTH), F32)
        glu_ext[0:hc, :] = jnp.zeros((hc, CF_WIDTH), F32)
        h_carry[...] = jnp.zeros(h_carry.shape, F32)
        for cp in copies_in(0, 0):
            cp.start()
        winb_ref[...] = win_ref[...].astype(BF16)
        wgb_ref[...] = wg_ref[...].astype(BF16)
        woutb_ref[...] = wout_ref[...].astype(BF16)

    for s in range(2):
        @pl.when(j % 2 == s)
        def _():
            for cp in copies_in(j, s):
                cp.wait()

            @pl.when(j + 1 < nj)
            def _():
                for cp in copies_in(j + 1, 1 - s):
                    cp.start()

            @pl.when(j >= 2)
            def _():
                for cp in copies_out(j - 2, s):
                    cp.wait()

    x = xbuf[j % 2].reshape(rows, D_MODEL)
    z = _bdot(_rms(x, gmix_ref[...]), winb_ref[...])
    zx = z[:, :LRU_WIDTH]
    zg = z[:, LRU_WIDTH:2 * LRU_WIDTH]
    za = z[:, 2 * LRU_WIDTH:2 * LRU_WIDTH + CF_WIDTH]
    zb = z[:, 2 * LRU_WIDTH + CF_WIDTH:]

    zx_ext[hl:hl + rows, :] = zx
    xc = blc_ref[...]
    for k in range(LRU_CONV):
        xc = xc + wlc_ref[k:k + 1, :] * zx_ext[k * nb:k * nb + rows, :]
    a, u = _lru_coeffs(xc, wgb_ref, bg_ref, lam_ref)
    h = h_carry[...]
    steps = []
    for t in range(ts):
        h = a[t * nb:(t + 1) * nb, :] * h + u[t * nb:(t + 1) * nb, :]
        steps.append(h)
    h_carry[...] = h
    y_lru = jnp.concatenate(steps, axis=0) * jax.nn.gelu(zg)

    glu = za * jax.nn.sigmoid(zb)
    glu_ext[hc:hc + rows, :] = glu
    gc = bcc_ref[...]
    for k in range(CF_CONV):
        gc = gc + wcc_ref[k:k + 1, :] * glu_ext[k * nb:k * nb + rows, :]
    y_cf = _cf_post(gc, gln_ref, bln_ref)

    x1 = _mixer_out(x, y_lru, y_cf, woutb_ref)
    for s in range(2):
        @pl.when(j % 2 == s)
        def _():
            obuf[s] = x1.reshape(ts, nb, D_MODEL)
            for cp in copies_out(j, s):
                cp.start()

            @pl.when(j == nj - 1)
            def _():
                for cp in copies_out(j, s):
                    cp.wait()

                @pl.when(j >= 1)
                def _():
                    for cp in copies_out(j - 1, 1 - s):
                        cp.wait()

    zx_ext[0:hl, :] = zx_ext[rows:rows + hl, :]
    glu_ext[0:hc, :] = glu_ext[rows:rows + hc, :]

    @pl.when(j == nj - 1)
    def _():
        h_ref[...] = h
        for b in range(nb):
            for r in range(LRU_CONV - 1):
                ltail_ref[b, r:r + 1, :] = zx_ext[r * nb + b:r * nb + b + 1, :]
            for r in range(CF_CONV - 1):
                ctail_ref[b, r:r + 1, :] = glu_ext[r * nb + b:r * nb + b + 1, :]


def _mixer_weight_specs(stacked=False):
    dense = _layer0 if stacked else _full
    return [
        _full((1, D_MODEL)),
        dense((D_MODEL, IN_WIDTH)),
        _full((LRU_CONV, LRU_WIDTH)), _full((1, LRU_WIDTH)),
        _full((LRU_WIDTH, 2 * LRU_WIDTH)), _full((1, 2 * LRU_WIDTH)),
        _full((1, LRU_WIDTH)),
        _full((CF_CONV, CF_WIDTH)), _full((1, CF_WIDTH)),
        _full((1, CF_WIDTH)), _full((1, CF_WIDTH)),
        dense((D_MODEL, D_MODEL)),
    ]


def _mixer_prompt(x, mix_w):
    batch, seq, _ = x.shape
    ts = MIX_STEPS
    any_ = pl.BlockSpec(memory_space=pl.ANY)
    return pl.pallas_call(
        _mixer_prompt_kernel,
        grid=(seq // ts,),
        in_specs=[any_] + _mixer_weight_specs(stacked=True),
        out_specs=[
            any_,
            _full((batch, LRU_WIDTH)),
            _full((batch, LRU_CONV - 1, LRU_WIDTH)),
            _full((batch, CF_CONV - 1, CF_WIDTH)),
            _full((D_MODEL, IN_WIDTH)), _full((LRU_WIDTH, 2 * LRU_WIDTH)), _full((D_MODEL, D_MODEL)),
        ],
        out_shape=[
            jax.ShapeDtypeStruct((batch, seq, D_MODEL), F32),
            jax.ShapeDtypeStruct((batch, LRU_WIDTH), F32),
            jax.ShapeDtypeStruct((batch, LRU_CONV - 1, LRU_WIDTH), F32),
            jax.ShapeDtypeStruct((batch, CF_CONV - 1, CF_WIDTH), F32),
            jax.ShapeDtypeStruct((D_MODEL, IN_WIDTH), BF16),
            jax.ShapeDtypeStruct((LRU_WIDTH, 2 * LRU_WIDTH), BF16),
            jax.ShapeDtypeStruct((D_MODEL, D_MODEL), BF16),
        ],
        scratch_shapes=[
            pltpu.VMEM((2, ts, batch, D_MODEL), F32),
            pltpu.VMEM((2, ts, batch, D_MODEL), F32),
            pltpu.VMEM(((LRU_CONV - 1 + ts) * batch, LRU_WIDTH), F32),
            pltpu.VMEM(((CF_CONV - 1 + ts) * batch, CF_WIDTH), F32),
            pltpu.VMEM((batch, LRU_WIDTH), F32),
            pltpu.SemaphoreType.DMA((2,)), pltpu.SemaphoreType.DMA((2,)),
        ],
        compiler_params=_cparams(("arbitrary",)),
        name="mixer_prompt",
    )(x, *mix_w)


def _mixer_sample_kernel(x_ref, gmix_ref, win_ref, wlc_ref, blc_ref, wg_ref, bg_ref, lam_ref,
                         wcc_ref, bcc_ref, gln_ref, bln_ref, wout_ref,
                         h0_ref, lbuf_ref, cbuf_ref,
                         x1_ref, h_ref, ltail_ref, ctail_ref):
    x = x_ref[...]
    z = _bdot(_rms(x, gmix_ref[...]), win_ref[...])
    zx = z[:, :LRU_WIDTH]
    zg = z[:, LRU_WIDTH:2 * LRU_WIDTH]
    za = z[:, 2 * LRU_WIDTH:2 * LRU_WIDTH + CF_WIDTH]
    zb = z[:, 2 * LRU_WIDTH + CF_WIDTH:]

    xc = blc_ref[...] + wlc_ref[LRU_CONV - 1:LRU_CONV, :] * zx
    xc = xc + jnp.sum(lbuf_ref[...] * wlc_ref[0:LRU_CONV - 1, :][None], axis=1)
    a, u = _lru_coeffs(xc, wg_ref, bg_ref, lam_ref)
    h = a * h0_ref[...] + u
    y_lru = h * jax.nn.gelu(zg)

    glu = za * jax.nn.sigmoid(zb)
    gc = bcc_ref[...] + wcc_ref[CF_CONV - 1:CF_CONV, :] * glu
    gc = gc + jnp.sum(cbuf_ref[...] * wcc_ref[0:CF_CONV - 1, :][None], axis=1)
    y_cf = _cf_post(gc, gln_ref, bln_ref)

    x1_ref[...] = _mixer_out(x, y_lru, y_cf, wout_ref)
    h_ref[...] = h
    ltail_ref[:, 0:LRU_CONV - 2, :] = lbuf_ref[:, 1:LRU_CONV - 1, :]
    ctail_ref[:, 0:CF_CONV - 2, :] = cbuf_ref[:, 1:CF_CONV - 1, :]
    for i in range(x.shape[0]):
        ltail_ref[i, LRU_CONV - 2:LRU_CONV - 1, :] = zx[i:i + 1, :]
        ctail_ref[i, CF_CONV - 2:CF_CONV - 1, :] = glu[i:i + 1, :]


def _mixer_sample(x, mix_w, h0, lbuf, cbuf):
    n = x.shape[0]
    g = SAMPLE_MIX_GROUP
    row = lambda w: pl.BlockSpec((g, w), lambda i: (i, 0))
    state = lambda w: pl.BlockSpec((None, g, w), lambda i: (0, i, 0))
    window = lambda k, w: pl.BlockSpec((None, g, k, w), lambda i: (0, i, 0, 0))
    return pl.pallas_call(
        _mixer_sample_kernel,
        grid=(n // g,),
        in_specs=[row(D_MODEL)] + _mixer_weight_specs() + [
            state(LRU_WIDTH),
            window(LRU_CONV - 1, LRU_WIDTH),
            window(CF_CONV - 1, CF_WIDTH),
        ],
        out_specs=[row(D_MODEL), state(LRU_WIDTH),
                   window(LRU_CONV - 1, LRU_WIDTH),
                   window(CF_CONV - 1, CF_WIDTH)],
        out_shape=[
            jax.ShapeDtypeStruct((n, D_MODEL), F32),
            jax.ShapeDtypeStruct((1, n, LRU_WIDTH), F32),
            jax.ShapeDtypeStruct((1, n, LRU_CONV - 1, LRU_WIDTH), F32),
            jax.ShapeDtypeStruct((1, n, CF_CONV - 1, CF_WIDTH), F32),
        ],
        compiler_params=_cparams(("arbitrary",)),
        name="mixer_sample",
    )(x, *mix_w, h0, lbuf, cbuf)


def _memkv_kernel(mem_ref, gmem_ref, wk_ref, wv_ref, k_ref, v_ref, kb_ref, vb_ref, wkb, wvb):
    @pl.when(pl.program_id(0) == 0)
    def _():
        wkb[...] = wk_ref[...].astype(BF16)
        wvb[...] = wv_ref[...].astype(BF16)

    mn = _rms(mem_ref[...], gmem_ref[...]).astype(BF16)
    k = jnp.dot(mn, wkb[...], preferred_element_type=F32)
    v = jnp.dot(mn, wvb[...], preferred_element_type=F32)
    for h in range(XA_HEADS):
        sl = slice(h * XA_HEAD_DIM, (h + 1) * XA_HEAD_DIM)
        k_ref[0, :, h, :] = k[:, sl]
        v_ref[0, :, h, :] = v[:, sl]
    kb_ref[...] = k.astype(BF16)
    vb_ref[...] = v.astype(BF16)


def _memkv(mem, g_mem, wk, wv):
    n = mem.shape[0]
    t = N_MEM
    row = pl.BlockSpec((t, D_MODEL), lambda i: (i, 0))
    state = pl.BlockSpec((1, N_MEM, XA_HEADS, XA_HEAD_DIM), lambda i: (i, 0, 0, 0))
    state_shape = jax.ShapeDtypeStruct((n // t, N_MEM, XA_HEADS, XA_HEAD_DIM), F32)
    return pl.pallas_call(
        _memkv_kernel,
        grid=(n // t,),
        in_specs=[row, _full((1, D_MODEL)), _layer0((D_MODEL, D_MODEL)), _layer0((D_MODEL, D_MODEL))],
        out_specs=[state, state, row, row],
        out_shape=[state_shape] * 2 + [jax.ShapeDtypeStruct((n, D_MODEL), BF16)] * 2,
        scratch_shapes=[pltpu.VMEM((D_MODEL, D_MODEL), BF16)] * 2,
        compiler_params=_cparams(("arbitrary",)),
        name="memory_kv",
    )(mem, g_mem, wk, wv)


def _router(x2, gmoe_ref, wrt_ref, brt_ref, xn_ref, tope_ref, gates_ref):
    xn = _rms(x2, gmoe_ref[...])
    _store_row_tiles(xn_ref, xn)
    logits = lax.dot_general(wrt_ref[...], xn.astype(BF16), (((1,), (1,)), ((), ())),
                             preferred_element_type=F32) + brt_ref[...]
    e_iota = lax.broadcasted_iota(I32, logits.shape, 0)
    work = logits
    vals, idxs = [], []
    for _ in range(TOP_K):
        m = jnp.max(work, axis=0, keepdims=True)
        idx = jnp.min(jnp.where(work == m, e_iota, N_EXPERTS), axis=0, keepdims=True)
        vals.append(m)
        idxs.append(idx)
        work = jnp.where(e_iota == idx, -jnp.inf, work)
    ex = [jnp.exp(v - vals[0]) for v in vals]
    den = ex[0] + ex[1] + ex[2] + ex[3]
    tope_ref[...] = jnp.concatenate(idxs, axis=0)
    gates_ref[...] = jnp.concatenate([e / den for e in ex], axis=0)


def _router_specs():
    return [_full((1, D_MODEL)), _full((N_EXPERTS, D_MODEL)), _full((N_EXPERTS, 1))]


def _router_out(n, tile, index):
    specs = [
        pl.BlockSpec((tile, D_MODEL), lambda *a: (index(*a), 0)),
        pl.BlockSpec((tile * ROW_TILE, LANES), lambda *a: (index(*a), 0)),
        pl.BlockSpec((TOP_K, tile), lambda *a: (0, index(*a))),
        pl.BlockSpec((TOP_K, tile), lambda *a: (0, index(*a))),
    ]
    shapes = [
        jax.ShapeDtypeStruct((n, D_MODEL), F32),
        jax.ShapeDtypeStruct((n * ROW_TILE, LANES), F32),
        jax.ShapeDtypeStruct((TOP_K, n), I32),
        jax.ShapeDtypeStruct((TOP_K, n), F32),
    ]
    return specs, shapes


def _attn_prompt_kernel(x1_ref, k_ref, v_ref, gxa_ref, wq_ref, wo_ref, gmoe_ref, wrt_ref, brt_ref,
                        x2_ref, xn_ref, tope_ref, gates_ref, wqb_ref, wob_ref):
    @pl.when(jnp.logical_and(pl.program_id(0) == 0, pl.program_id(1) == 0))
    def _():
        wqb_ref[...] = wq_ref[...].astype(BF16)
        wob_ref[...] = wo_ref[...].astype(BF16)

    x1 = x1_ref[...]
    q = (_bdot(_rms(x1, gxa_ref[...]), wqb_ref[...]) * (XA_HEAD_DIM ** -0.5)).astype(BF16)
    outs = []
    for h in range(XA_HEADS):
        sl = slice(h * XA_HEAD_DIM, (h + 1) * XA_HEAD_DIM)
        s = lax.dot_general(q[:, sl], k_ref[:, sl], (((1,), (1,)), ((), ())), preferred_element_type=F32)
        p = jnp.exp(s - jnp.max(s, axis=-1, keepdims=True))
        p = p / jnp.sum(p, axis=-1, keepdims=True)
        outs.append(_bdot(p, v_ref[:, sl]))
    o = jnp.concatenate(outs, axis=-1)
    x2 = x1 + _bdot(o, wob_ref[...])
    x2_ref[...] = x2
    _router(x2, gmoe_ref, wrt_ref, brt_ref, xn_ref, tope_ref, gates_ref)


def _attn_prompt(x1, k, v, gxa, wq, wo, router_w, batch, seq):
    tq = ATT_TILE
    nt = seq // tq
    out_specs, out_shapes = _router_out(batch * seq, tq, lambda b, j: b * nt + j)
    kv = pl.BlockSpec((N_MEM, D_MODEL), lambda b, j: (b, 0))
    return pl.pallas_call(
        _attn_prompt_kernel,
        grid=(batch, nt),
        in_specs=[pl.BlockSpec((tq, D_MODEL), lambda b, j: (b * nt + j, 0)), kv, kv,
                  _full((1, D_MODEL)), _layer0((D_MODEL, D_MODEL)), _layer0((D_MODEL, D_MODEL))] + _router_specs(),
        out_specs=out_specs + [_full((D_MODEL, D_MODEL))] * 2,
        out_shape=out_shapes + [jax.ShapeDtypeStruct((D_MODEL, D_MODEL), BF16)] * 2,
        compiler_params=_cparams(("arbitrary", "arbitrary")),
        name="attn_prompt",
    )(x1, k, v, gxa, wq, wo, *router_w)


def _qproj_kernel(x1_ref, gxa_ref, wq_ref, q_ref):
    q_ref[...] = _bdot(_rms(x1_ref[...], gxa_ref[...]), wq_ref[...]) * (XA_HEAD_DIM ** -0.5)


def _attn_sample_core_kernel(q_ref, k_ref, v_ref, o_ref):
    n = N_MEM * SUBLANES
    col_head = lax.broadcasted_iota(I32, (SUBLANES, n), 1) & (SUBLANES - 1)
    same_head = col_head == lax.broadcasted_iota(I32, (SUBLANES, n), 0)
    kv_pad = jnp.zeros((N_MEM, SUBLANES - XA_HEADS, XA_HEAD_DIM), F32)
    q_pad = jnp.zeros((SUBLANES - XA_HEADS, XA_HEAD_DIM), F32)
    for g in range(q_ref.shape[0]):
        k8 = jnp.concatenate([k_ref[g], kv_pad], axis=1).reshape(n, XA_HEAD_DIM).astype(BF16)
        v8 = jnp.concatenate([v_ref[g], kv_pad], axis=1).reshape(n, XA_HEAD_DIM).astype(BF16)
        q8 = jnp.concatenate([q_ref[g], q_pad], axis=0).astype(BF16)
        s = lax.dot_general(q8, k8, (((1,), (1,)), ((), ())), preferred_element_type=F32)
        m = jnp.max(jnp.where(same_head, s, -jnp.inf), axis=-1, keepdims=True)
        p = jnp.where(same_head, jnp.exp(s - m), 0.0)
        den = jnp.sum(p, axis=-1, keepdims=True)
        o8 = jnp.dot(p.astype(BF16), v8, preferred_element_type=F32)
        o_ref[g] = (o8 / den)[0:XA_HEADS, :]


def _oproj_router_kernel(x1_ref, o_ref, wo_ref, gmoe_ref, wrt_ref, brt_ref,
                         x2_ref, xn_ref, tope_ref, gates_ref):
    x2 = x1_ref[...] + _bdot(o_ref[...], wo_ref[...])
    x2_ref[...] = x2
    _router(x2, gmoe_ref, wrt_ref, brt_ref, xn_ref, tope_ref, gates_ref)


def _attn_sample(x1, k, v, gxa, wq, wo, router_w):
    n = x1.shape[0]
    g = SAMPLE_GROUP
    q = pl.pallas_call(
        _qproj_kernel,
        grid=(1,),
        in_specs=[_full((n, D_MODEL)), _full((1, D_MODEL)), _full((D_MODEL, D_MODEL))],
        out_specs=_full((n, D_MODEL)),
        out_shape=jax.ShapeDtypeStruct((n, D_MODEL), F32),
        compiler_params=_cparams(("arbitrary",)),
        name="qproj_sample",
    )(x1, gxa, wq)
    kv = pl.BlockSpec((g, N_MEM, XA_HEADS, XA_HEAD_DIM), lambda i: (i, 0, 0, 0))
    row = pl.BlockSpec((g, XA_HEADS, XA_HEAD_DIM), lambda i: (i, 0, 0))
    o = pl.pallas_call(
        _attn_sample_core_kernel,
        grid=(n // g,),
        in_specs=[row, kv, kv],
        out_specs=row,
        out_shape=jax.ShapeDtypeStruct((n, XA_HEADS, XA_HEAD_DIM), F32),
        compiler_params=_cparams(("arbitrary",)),
        name="attn_sample_core",
    )(q.reshape(n, XA_HEADS, XA_HEAD_DIM), k, v).reshape(n, D_MODEL)
    out_specs, out_shapes = _router_out(n, n, lambda i: 0)
    return pl.pallas_call(
        _oproj_router_kernel,
        grid=(1,),
        in_specs=[_full((n, D_MODEL)), _full((n, D_MODEL)), _full((D_MODEL, D_MODEL))] + _router_specs(),
        out_specs=out_specs,
        out_shape=out_shapes,
        compiler_params=_cparams(("arbitrary",)),
        name="oproj_router_sample",
    )(x1, o, wo, *router_w)


def _moe_num_blocks(n_assign):
    nb = pl.cdiv(n_assign, MOE_BLOCK) + N_EXPERTS
    return pl.cdiv(nb, EXPERT_STEP_BLOCKS) * EXPERT_STEP_BLOCKS


def _slots_kernel(tp_ref, ts_ref, destp_ref, dests_ref, blke_ref, nact_ref, cnt_ref, pstart_ref,
                  rankp, ranks):
    def make_body(top_ref, rank_ref):
        c = min(SLOT_CHUNK, top_ref.shape[1])
        r_i = lax.broadcasted_iota(I32, (c, c), 0)
        c_i = lax.broadcasted_iota(I32, (c, c), 1)
        upper = (r_i < c_i).astype(BF16)
        e_iota = lax.broadcasted_iota(I32, (N_EXPERTS, c), 0)

        def body(ci, carry):
            lo = pl.multiple_of(ci * c, c)
            top = top_ref[:, pl.ds(lo, c)]
            hit = [e_iota == top[k:k + 1, :] for k in range(TOP_K)]
            cnt = sum(m.astype(F32) for m in hit)
            base = carry + jnp.dot(cnt.astype(BF16), upper, preferred_element_type=F32)
            rank_ref[:, pl.ds(lo, c)] = jnp.concatenate(
                [jnp.sum(jnp.where(m, base, 0.0), axis=0, keepdims=True) for m in hit], axis=0)
            return carry + jnp.sum(cnt, axis=1, keepdims=True)
        return top_ref.shape[1] // c, body

    counts = jnp.zeros((N_EXPERTS, 1), F32)
    for top_ref, rank_ref in ((tp_ref, rankp), (ts_ref, ranks)):
        trips, body = make_body(top_ref, rank_ref)
        counts = lax.fori_loop(0, trips, body, counts)

    padded = jnp.floor((counts + (MOE_BLOCK - 1)) / MOE_BLOCK) * MOE_BLOCK
    er = lax.broadcasted_iota(I32, (N_EXPERTS, N_EXPERTS), 0)
    ec = lax.broadcasted_iota(I32, (N_EXPERTS, N_EXPERTS), 1)
    padded_row = jnp.sum(jnp.where(er == ec, padded, 0.0), axis=0, keepdims=True)
    cum = jnp.sum(jnp.where(ec <= er, padded_row, 0.0), axis=1, keepdims=True)
    pstart = cum - padded
    cnt_ref[...] = jnp.sum(jnp.where(er == ec, counts, 0.0), axis=0, keepdims=True).astype(I32)
    pstart_ref[...] = jnp.sum(jnp.where(er == ec, pstart, 0.0), axis=0, keepdims=True).astype(I32)

    def dest(top_ref, rank_ref, out_ref):
        n = top_ref.shape[1]
        ei = lax.broadcasted_iota(I32, (N_EXPERTS, n), 0)
        rows = []
        for k in range(TOP_K):
            start = jnp.sum(jnp.where(ei == top_ref[k:k + 1, :], pstart, 0.0), axis=0, keepdims=True)
            rows.append(start + rank_ref[k:k + 1, :])
        out_ref[...] = jnp.concatenate(rows, axis=0).astype(I32)

    dest(tp_ref, rankp, destp_ref)
    dest(ts_ref, ranks, dests_ref)

    nb = blke_ref.shape[1]
    blk_lo = lax.broadcasted_iota(I32, (N_EXPERTS, nb), 1).astype(F32) * MOE_BLOCK
    blk = jnp.sum((cum <= blk_lo).astype(F32), axis=0, keepdims=True)
    blke_ref[...] = jnp.minimum(blk, N_EXPERTS - 1).astype(I32)
    nact_ref[...] = (cum[N_EXPERTS - 1:N_EXPERTS, :] / MOE_BLOCK).astype(I32)


def _slots(top_p, top_s):
    n_p, n_s = top_p.shape[1], top_s.shape[1]
    nb = _moe_num_blocks((n_p + n_s) * TOP_K)
    return pl.pallas_call(
        _slots_kernel,
        grid=(1,),
        in_specs=[_full((TOP_K, n_p)), _full((TOP_K, n_s))],
        out_specs=[_full((TOP_K, n_p)), _full((TOP_K, n_s)), _full((1, nb)), _full((1, 1)),
                   _full((1, N_EXPERTS)), _full((1, N_EXPERTS))],
        out_shape=[
            jax.ShapeDtypeStruct((TOP_K, n_p), I32),
            jax.ShapeDtypeStruct((TOP_K, n_s), I32),
            jax.ShapeDtypeStruct((1, nb), I32),
            jax.ShapeDtypeStruct((1, 1), I32),
            jax.ShapeDtypeStruct((1, N_EXPERTS), I32),
            jax.ShapeDtypeStruct((1, N_EXPERTS), I32),
        ],
        scratch_shapes=[pltpu.VMEM((TOP_K, n_p), F32), pltpu.VMEM((TOP_K, n_s), F32)],
        compiler_params=_cparams(("arbitrary",)),
        name="moe_slots",
    )(top_p, top_s)


def _for_rows(n_rows, fn):
    def body(i, c):
        for u in range(DMA_UNROLL):
            fn(i * DMA_UNROLL + u)
        return c
    lax.fori_loop(0, n_rows // DMA_UNROLL, body, 0)


def _dispatch_kernel(cnt_ref, pstart_ref, nact_ref, destp_ref, dests_ref, xnp_ref, xns_ref, xs_ref,
                     zeros, sem, zsem):
    i = pl.program_id(0)
    last = pl.num_programs(0) - 1

    def scatter(x_ref, dest_ref):
        n = x_ref.shape[0] // ROW_TILE

        def one(t):
            for k in range(TOP_K):
                _row_tile_copy(x_ref, t, xs_ref, dest_ref[t * TOP_K + k], sem).start(priority=k % 2)

        _for_rows(n, one)
        _row_tile_copy(xs_ref, 0, xs_ref, 0, sem, n * TOP_K).wait()

    @pl.when(i < last)
    def _():
        scatter(xnp_ref, destp_ref)

    @pl.when(i == last)
    def _():
        scatter(xns_ref, dests_ref)
        zeros[...] = jnp.zeros(zeros.shape, F32)

        def pad_copies(e, act):
            c = cnt_ref[e]
            npad = (-c) & (MOE_BLOCK - 1)
            base = pstart_ref[e] + c
            for bit in range(MOE_BLOCK.bit_length() - 1):
                size = 1 << bit

                @pl.when((npad & size) != 0)
                def _():
                    act(_row_tile_copy(zeros, 0, xs_ref, base + (npad & (size - 1)), zsem, size))

        def tail_copies(act):
            def body(j, c):
                act(_row_tile_copy(zeros, 0, xs_ref, j * MOE_BLOCK, zsem, MOE_BLOCK))
                return c
            lax.fori_loop(nact_ref[0], xs_ref.shape[0] // (MOE_BLOCK * ROW_TILE), body, 0)

        for act in (lambda d: d.start(), lambda d: d.wait()):
            lax.fori_loop(0, N_EXPERTS, lambda e, c, act=act: (pad_copies(e, act), c)[1], 0)
            tail_copies(act)


def _dispatch(cnt, pstart, nact, dest_p, dest_s, xn_p, xn_s, n_rows):
    n_p, n_s = xn_p.shape[0] // ROW_TILE, xn_s.shape[0] // ROW_TILE
    tile = DISPATCH_TILE
    nt = n_p // tile
    smem = pl.BlockSpec(memory_space=pltpu.SMEM)
    return pl.pallas_call(
        _dispatch_kernel,
        grid=(nt + 1,),
        in_specs=[
            smem, smem, smem,
            pl.BlockSpec((TOP_K * tile,), lambda i: (jnp.minimum(i, nt - 1),), memory_space=pltpu.SMEM),
            smem,
            pl.BlockSpec((tile * ROW_TILE, LANES), lambda i: (jnp.minimum(i, nt - 1), 0)),
            _full((n_s * ROW_TILE, LANES)),
        ],
        out_specs=pl.BlockSpec(memory_space=pl.ANY),
        out_shape=jax.ShapeDtypeStruct((n_rows * ROW_TILE, LANES), F32),
        scratch_shapes=[pltpu.VMEM((MOE_BLOCK * ROW_TILE, LANES), F32), pltpu.SemaphoreType.DMA,
                        pltpu.SemaphoreType.DMA],
        compiler_params=_cparams(("arbitrary",)),
        name="moe_dispatch",
    )(cnt, pstart, nact, dest_p, dest_s, xn_p, xn_s)


def _combine_kernel(dest_ref, dest_next_ref, gates_ref, x2_ref, gfin_ref, ys_ref, y_ref, bufs, sems):
    tile = x2_ref.shape[0]
    i = pl.program_id(0)
    slot = i % 2

    def gather(d_ref, b):
        def one(t):
            for k in range(TOP_K):
                _row_tile_copy(ys_ref, d_ref[t * TOP_K + k], bufs.at[b, k], t, sems.at[b]).start(priority=k % 2)
        _for_rows(tile, one)

    @pl.when(i == 0)
    def _():
        gather(dest_ref, 0)

    for b in range(2):
        @pl.when(slot == b)
        def _():
            pltpu.make_async_copy(bufs.at[b], bufs.at[b], sems.at[b]).wait()

            @pl.when(i + 1 < pl.num_programs(0))
            def _():
                gather(dest_next_ref, 1 - b)

    buf = bufs.at[slot]
    sumsq = jnp.zeros((tile, 1), F32)
    gate = [jnp.broadcast_to(gates_ref[:, k:k + 1], (tile, LANES)) for k in range(TOP_K)]
    for s in range(ROW_TILE):
        cols = slice(s * LANES, (s + 1) * LANES)
        acc = x2_ref[:, cols]
        for k in range(TOP_K):
            acc = acc + gate[k] * buf[k, pl.ds(s, tile, stride=ROW_TILE), :]
        y_ref[:, cols] = acc
        sumsq = sumsq + jnp.sum(acc * acc, axis=-1, keepdims=True)
    scale = lax.rsqrt(sumsq / D_MODEL + EPS)
    for s in range(ROW_TILE):
        cols = slice(s * LANES, (s + 1) * LANES)
        y_ref[:, cols] = y_ref[:, cols] * scale * gfin_ref[:, cols]


def _combine(dest, gates_t, x2, g_final, ys):
    n = x2.shape[0]
    tile = min(COMBINE_TILE, n)
    row = pl.BlockSpec((tile, D_MODEL), lambda i: (i, 0))
    nt = n // tile
    return pl.pallas_call(
        _combine_kernel,
        grid=(nt,),
        in_specs=[
            pl.BlockSpec((TOP_K * tile,), lambda i: (i,), memory_space=pltpu.SMEM),
            pl.BlockSpec((TOP_K * tile,), lambda i: (jnp.minimum(i + 1, nt - 1),), memory_space=pltpu.SMEM),
            pl.BlockSpec((tile, TOP_K), lambda i: (i, 0)),
            row,
            _full((1, D_MODEL)),
            pl.BlockSpec(memory_space=pl.ANY),
        ],
        out_specs=row,
        out_shape=jax.ShapeDtypeStruct((n, D_MODEL), F32),
        scratch_shapes=[pltpu.VMEM((2, TOP_K, tile * ROW_TILE, LANES), F32), pltpu.SemaphoreType.DMA((2,))],
        compiler_params=_cparams(("arbitrary",)),
        name="moe_combine",
    )(dest, dest, gates_t, x2, g_final, ys)


def _experts_kernel(blke_ref, nact_ref, xs_ref, w1_hbm, w2_hbm, *rest):
    b1_refs, b2_refs = rest[:EXPERT_STEP_BLOCKS], rest[EXPERT_STEP_BLOCKS:2 * EXPERT_STEP_BLOCKS]
    ys_ref, w1f, w2f, w1b, w2b, wsem, started = rest[2 * EXPERT_STEP_BLOCKS:]
    j = pl.program_id(0)
    nact = nact_ref[0]
    last = blke_ref.shape[0] - 1

    def fetch(expert, slot):
        return (pltpu.make_async_copy(w1_hbm.at[expert], w1f.at[slot], wsem.at[slot]),
                pltpu.make_async_copy(w2_hbm.at[expert], w2f.at[slot], wsem.at[slot]))

    def switch_to(blk):
        e = blke_ref[blk]
        n = started[0]
        jn = lax.while_loop(lambda jj: jnp.logical_and(jj < nact, blke_ref[jnp.minimum(jj, last)] == e),
                            lambda jj: jj + 1, blk + 1)
        for b in range(2):
            @pl.when(n % 2 == b)
            def _():
                for d in fetch(e, b):
                    d.wait()

                @pl.when(jn < nact)
                def _():
                    for d in fetch(blke_ref[jnp.minimum(jn, last)], 1 - b):
                        d.start(priority=1)

                w1b[...] = w1f[b].astype(BF16)
                w2b[...] = w2f[b].astype(BF16)
        started[0] = n + 1

    def ffn(row0, n_rows, b1_ref, b2_ref):
        rows = pl.ds(row0 * ROW_TILE, n_rows * ROW_TILE)
        x = _load_row_tiles(xs_ref.at[rows], n_rows).astype(BF16)
        gu = jnp.dot(x, w1b[...], preferred_element_type=F32) + b1_ref[0]
        g = jnp.minimum(gu[:, :D_EXPERT], SWIGLU_LIMIT)
        u = jnp.clip(gu[:, D_EXPERT:], -SWIGLU_LIMIT, SWIGLU_LIMIT)
        h = (u + 1.0) * (g * jax.nn.sigmoid(SWIGLU_ALPHA * g))
        _store_row_tiles(ys_ref.at[rows], _bdot(h, w2b[...]) + b2_ref[0])

    def zero(row0, n_rows):
        ys_ref[pl.ds(row0 * ROW_TILE, n_rows * ROW_TILE), :] = jnp.zeros((n_rows * ROW_TILE, LANES), F32)

    @pl.when(j == 0)
    def _():
        started[0] = 0
        for d in fetch(blke_ref[0], 0):
            d.start()

    for p in range(0, EXPERT_STEP_BLOCKS, 2):
        first, second = EXPERT_STEP_BLOCKS * j + p, EXPERT_STEP_BLOCKS * j + p + 1
        e0, e1 = blke_ref[first], blke_ref[second]
        active0, active1 = first < nact, second < nact
        together = jnp.logical_and(active1, e1 == e0)
        row0 = p * MOE_BLOCK

        @pl.when(jnp.logical_and(active0, jnp.logical_or(first == 0, e0 != blke_ref[jnp.maximum(first - 1, 0)])))
        def _():
            switch_to(first)

        @pl.when(together)
        def _():
            ffn(row0, 2 * MOE_BLOCK, b1_refs[p], b2_refs[p])

        @pl.when(jnp.logical_and(active0, jnp.logical_not(together)))
        def _():
            ffn(row0, MOE_BLOCK, b1_refs[p], b2_refs[p])

            @pl.when(active1)
            def _():
                switch_to(second)
                ffn(row0 + MOE_BLOCK, MOE_BLOCK, b1_refs[p + 1], b2_refs[p + 1])

            @pl.when(jnp.logical_not(active1))
            def _():
                zero(row0 + MOE_BLOCK, MOE_BLOCK)

        @pl.when(jnp.logical_not(active0))
        def _():
            zero(row0, 2 * MOE_BLOCK)


def _experts(blk_e, nact, xs, w1, b1, w2, b2):
    nb = xs.shape[0] // (MOE_BLOCK * ROW_TILE)
    step = EXPERT_STEP_BLOCKS
    assert nb % step == 0 and step % 2 == 0
    blk = lambda j, be, na: (jnp.minimum(j, (na[0] - 1) // step), 0)
    exp = lambda h: (lambda j, be, na: (be[jnp.minimum(step * j + h, na[0] - 1)], 0, 0))
    rows = (step * MOE_BLOCK * ROW_TILE, LANES)
    any_ = pl.BlockSpec(memory_space=pl.ANY)
    return pl.pallas_call(
        _experts_kernel,
        grid_spec=pltpu.PrefetchScalarGridSpec(
            num_scalar_prefetch=2,
            grid=(nb // step,),
            in_specs=[pl.BlockSpec(rows, blk), any_, any_]
            + [pl.BlockSpec((1, 1, 2 * D_EXPERT), exp(h)) for h in range(step)]
            + [pl.BlockSpec((1, 1, D_MODEL), exp(h)) for h in range(step)],
            out_specs=pl.BlockSpec(rows, lambda j, be, na: (j, 0)),
            scratch_shapes=[
                pltpu.VMEM((2, D_MODEL, 2 * D_EXPERT), F32), pltpu.VMEM((2, D_EXPERT, D_MODEL), F32),
                pltpu.VMEM((D_MODEL, 2 * D_EXPERT), BF16), pltpu.VMEM((D_EXPERT, D_MODEL), BF16),
                pltpu.SemaphoreType.DMA((2,)), pltpu.SMEM((1,), I32),
            ],
        ),
        out_shape=jax.ShapeDtypeStruct(xs.shape, F32),
        compiler_params=_cparams(("arbitrary",)),
        name="moe_experts",
    )(blk_e, nact, xs, w1, w2, *([b1] * step), *([b2] * step))


def _block_diag(w):
    h, d, _ = w.shape
    eye = jnp.eye(h, dtype=w.dtype)
    return (eye[:, None, :, None] * w[:, :, None, :]).reshape(h * d, h * d)


def kernel(x_prompt, x_sample, state_lru_h, cache_lru_conv, cache_cf_conv, cache_mem_k, cache_mem_v,
           mem_prompt, g_mix, w_in, w_lru_conv, b_lru_conv, w_lru_a, b_lru_a, w_lru_x, b_lru_x,
           lru_lambda, w_cf_conv, b_cf_conv, g_cf_ln, b_cf_ln, w_out, g_xa, g_mem, w_q, w_k, w_v, w_o,
           g_moe, w_router, b_router, w_e1, b_e1, w_e2, b_e2, g_final):
    depth = g_mix.shape[0]
    assert depth == 1, "single-layer stack"
    batch, seq, _ = x_prompt.shape
    n_s = x_sample.shape[0]
    assert x_sample.shape[1] == 1
    n_p = batch * seq
    l = 0

    mix_w = (
        g_mix[l][None], w_in, w_lru_conv[l], b_lru_conv[l][None],
        jnp.concatenate([_block_diag(w_lru_a[l]), _block_diag(w_lru_x[l])], axis=1),
        jnp.concatenate([b_lru_a[l], b_lru_x[l]])[None], lru_lambda[l][None],
        w_cf_conv[l], b_cf_conv[l][None], g_cf_ln[l][None], b_cf_ln[l][None], w_out,
    )
    router_w = (g_moe[l][None], w_router[l].T.astype(BF16), b_router[l][:, None])
    gxa = g_xa[l][None]

    xs_ = x_sample.reshape(n_s, D_MODEL)
    x1p, hp, lcp, ccp, win_b, wg_b, wout_b = _mixer_prompt(x_prompt, mix_w)
    x1p = x1p.reshape(n_p, D_MODEL)
    mix_w_b = mix_w[:1] + (win_b,) + mix_w[2:4] + (wg_b,) + mix_w[5:-1] + (wout_b,)
    x1s, hs, lru_conv_s, cf_conv_s = _mixer_sample(xs_, mix_w_b, state_lru_h, cache_lru_conv, cache_cf_conv)

    mk, mv, mkb, mvb = _memkv(mem_prompt.reshape(batch * N_MEM, D_MODEL), g_mem[l][None],
                              w_k, w_v)
    x2p, xnp_, top_p, gates_p, wq, wo = _attn_prompt(x1p, mkb, mvb, gxa, w_q, w_o, router_w, batch, seq)
    x2s, xns, top_s, gates_s = _attn_sample(x1s, cache_mem_k[l], cache_mem_v[l], gxa, wq, wo, router_w)

    dest_p, dest_s, blk_e, nact, cnt, pstart = _slots(top_p, top_s)
    dest_p, dest_s = dest_p.T.reshape(-1), dest_s.T.reshape(-1)
    nb = _moe_num_blocks((n_p + n_s) * TOP_K)
    nact = nact.reshape(1)
    xs = _dispatch(cnt.reshape(N_EXPERTS), pstart.reshape(N_EXPERTS), nact, dest_p, dest_s, xnp_, xns,
                   nb * MOE_BLOCK)
    ys = _experts(blk_e.reshape(nb), nact, xs, w_e1[l], b_e1[l][:, None, :], w_e2[l], b_e2[l][:, None, :])
    gfin = g_final[None]
    y_p = _combine(dest_p, gates_p.T, x2p, gfin, ys)
    y_s = _combine(dest_s, gates_s.T, x2s, gfin, ys)

    return (
        y_p.reshape(batch, seq, D_MODEL),
        y_s.reshape(n_s, 1, D_MODEL),
        hp[None],
        lcp[None],
        ccp[None],
        mk[None],
        mv[None],
        hs,
        lru_conv_s,
        cf_conv_s,
    )
```

```python
import jax
import jax.numpy as jnp
from jax import lax
from jax.experimental import pallas as pl
from jax.experimental.pallas import tpu as pltpu

F32 = jnp.float32
BF16 = jnp.bfloat16
I32 = jnp.int32

D_MODEL = 1024
LRU_WIDTH = 512
CF_WIDTH = 512
LRU_CONV = 4
LRU_C = 8.0
CF_CONV = 31
IN_WIDTH = 2 * LRU_WIDTH + 2 * CF_WIDTH
N_MEM = 256
XA_HEADS = 4
XA_HEAD_DIM = D_MODEL // XA_HEADS
N_EXPERTS = 32
TOP_K = 4
D_EXPERT = D_MODEL
SWIGLU_LIMIT = 7.0
SWIGLU_ALPHA = 1.702
EPS = 1e-6

LANES = 128
SUBLANES = 8
ROW_TILE = D_MODEL // LANES
assert ROW_TILE == SUBLANES
VMEM_LIMIT = 56 * 1024 * 1024

MIX_STEPS = 128
ATT_TILE = 1024
SAMPLE_GROUP = 4
SAMPLE_MIX_GROUP = 32
MOE_BLOCK = 256
EXPERT_STEP_BLOCKS = 4
SLOT_CHUNK = 256
DMA_UNROLL = 16
DISPATCH_TILE = 4096
COMBINE_TILE = 256


def _cparams(sem):
    return pltpu.CompilerParams(dimension_semantics=sem, vmem_limit_bytes=VMEM_LIMIT)


def _full(shape):
    n = len(shape)
    return pl.BlockSpec(shape, lambda *_: (0,) * n)


def _layer0(shape):
    n = len(shape)
    return pl.BlockSpec((None,) + tuple(shape), lambda *_: (0,) * (n + 1))


def _rms(x, g):
    return x * lax.rsqrt(jnp.mean(x * x, axis=-1, keepdims=True) + EPS) * g


def _bdot(a, b):
    return jnp.dot(a.astype(BF16), b, preferred_element_type=F32)


def _store_row_tiles(ref, x):
    m = x.shape[0]
    for s in range(ROW_TILE):
        ref[pl.ds(s, m, stride=ROW_TILE), :] = x[:, s * LANES:(s + 1) * LANES]


def _load_row_tiles(ref, m):
    return jnp.concatenate([ref[pl.ds(s, m, stride=ROW_TILE), :] for s in range(ROW_TILE)], axis=-1)


def _row_tile_copy(src_ref, src_row, dst_ref, dst_row, sem, n_rows=1):
    src = src_ref.at[pl.ds(pl.multiple_of(src_row * ROW_TILE, ROW_TILE), n_rows * ROW_TILE)]
    dst = dst_ref.at[pl.ds(pl.multiple_of(dst_row * ROW_TILE, ROW_TILE), n_rows * ROW_TILE)]
    return pltpu.make_async_copy(src, dst, sem)


def _lru_coeffs(xc, wg_ref, bg_ref, lam_ref):
    gl = _bdot(xc, wg_ref[...]) + bg_ref[...]
    r = jax.nn.sigmoid(gl[:, :LRU_WIDTH])
    i = jax.nn.sigmoid(gl[:, LRU_WIDTH:])
    log_a = -LRU_C * r * jax.nn.softplus(-lam_ref[...])
    a = jnp.exp(log_a)
    u = jnp.sqrt(-jnp.tanh(log_a) * (1.0 + a * a)) * (i * xc)
    return a, u


def _cf_post(gc, gln_ref, bln_ref):
    mu = jnp.mean(gc, axis=-1, keepdims=True)
    c = gc - mu
    y = c * lax.rsqrt(jnp.mean(c * c, axis=-1, keepdims=True) + EPS)
    return jax.nn.silu(y * gln_ref[...] + bln_ref[...])


def _mixer_out(x, y_lru, y_cf, wout_ref):
    y = _bdot(y_lru, wout_ref[:LRU_WIDTH, :]) + _bdot(y_cf, wout_ref[LRU_WIDTH:, :])
    return x + y


def _mixer_prompt_kernel(x_hbm, gmix_ref, win_ref, wlc_ref, blc_ref, wg_ref, bg_ref, lam_ref,
                         wcc_ref, bcc_ref, gln_ref, bln_ref, wout_ref,
                         x1_hbm, h_ref, ltail_ref, ctail_ref, winb_ref, wgb_ref, woutb_ref,
                         xbuf, obuf, zx_ext, glu_ext, h_carry, isem, osem):
    ts = MIX_STEPS
    nb = xbuf.shape[2]
    rows = ts * nb
    j = pl.program_id(0)
    nj = pl.num_programs(0)
    hl = (LRU_CONV - 1) * nb
    hc = (CF_CONV - 1) * nb

    def copies_in(step, s):
        return [pltpu.make_async_copy(x_hbm.at[b, pl.ds(step * ts, ts)], xbuf.at[s, pl.ds(0, ts), b], isem.at[s])
                for b in range(nb)]

    def copies_out(step, s):
        return [pltpu.make_async_copy(obuf.at[s, pl.ds(0, ts), b], x1_hbm.at[b, pl.ds(step * ts, ts)], osem.at[s])
                for b in range(nb)]

    @pl.when(j == 0)
    def _():
        zx_ext[0:hl, :] = jnp.zeros((hl, LRU_WIDTH), F32)
        glu_ext[0:hc, :] = jnp.zeros((hc, CF_WIDTH), F32)
        h_carry[...] = jnp.zeros(h_carry.shape, F32)
        for cp in copies_in(0, 0):
            cp.start()
        winb_ref[...] = win_ref[...].astype(BF16)
        wgb_ref[...] = wg_ref[...].astype(BF16)
        woutb_ref[...] = wout_ref[...].astype(BF16)

    for s in range(2):
        @pl.when(j % 2 == s)
        def _():
            for cp in copies_in(j, s):
                cp.wait()

            @pl.when(j + 1 < nj)
            def _():
                for cp in copies_in(j + 1, 1 - s):
                    cp.start()

            @pl.when(j >= 2)
            def _():
                for cp in copies_out(j - 2, s):
                    cp.wait()

    x = xbuf[j % 2].reshape(rows, D_MODEL)
    z = _bdot(_rms(x, gmix_ref[...]), winb_ref[...])
    zx = z[:, :LRU_WIDTH]
    zg = z[:, LRU_WIDTH:2 * LRU_WIDTH]
    za = z[:, 2 * LRU_WIDTH:2 * LRU_WIDTH + CF_WIDTH]
    zb = z[:, 2 * LRU_WIDTH + CF_WIDTH:]

    zx_ext[hl:hl + rows, :] = zx
    xc = blc_ref[...]
    for k in range(LRU_CONV):
        xc = xc + wlc_ref[k:k + 1, :] * zx_ext[k * nb:k * nb + rows, :]
    a, u = _lru_coeffs(xc, wgb_ref, bg_ref, lam_ref)
    h = h_carry[...]
    steps = []
    for t in range(ts):
        h = a[t * nb:(t + 1) * nb, :] * h + u[t * nb:(t + 1) * nb, :]
        steps.append(h)
    h_carry[...] = h
    y_lru = jnp.concatenate(steps, axis=0) * jax.nn.gelu(zg)

    glu = za * jax.nn.sigmoid(zb)
    glu_ext[hc:hc + rows, :] = glu
    gc = bcc_ref[...]
    for k in range(CF_CONV):
        gc = gc + wcc_ref[k:k + 1, :] * glu_ext[k * nb:k * nb + rows, :]
    y_cf = _cf_post(gc, gln_ref, bln_ref)

    x1 = _mixer_out(x, y_lru, y_cf, woutb_ref)
    for s in range(2):
        @pl.when(j % 2 == s)
        def _():
            obuf[s] = x1.reshape(ts, nb, D_MODEL)
            for cp in copies_out(j, s):
                cp.start()

            @pl.when(j == nj - 1)
            def _():
                for cp in copies_out(j, s):
                    cp.wait()

                @pl.when(j >= 1)
                def _():
                    for cp in copies_out(j - 1, 1 - s):
                        cp.wait()

    zx_ext[0:hl, :] = zx_ext[rows:rows + hl, :]
    glu_ext[0:hc, :] = glu_ext[rows:rows + hc, :]

    @pl.when(j == nj - 1)
    def _():
        h_ref[...] = h
        for b in range(nb):
            for r in range(LRU_CONV - 1):
                ltail_ref[b, r:r + 1, :] = zx_ext[r * nb + b:r * nb + b + 1, :]
            for r in range(CF_CONV - 1):
                ctail_ref[b, r:r + 1, :] = glu_ext[r * nb + b:r * nb + b + 1, :]


def _mixer_weight_specs(stacked=False):
    dense = _layer0 if stacked else _full
    return [
        _full((1, D_MODEL)),
        dense((D_MODEL, IN_WIDTH)),
        _full((LRU_CONV, LRU_WIDTH)), _full((1, LRU_WIDTH)),
        _full((LRU_WIDTH, 2 * LRU_WIDTH)), _full((1, 2 * LRU_WIDTH)),
        _full((1, LRU_WIDTH)),
        _full((CF_CONV, CF_WIDTH)), _full((1, CF_WIDTH)),
        _full((1, CF_WIDTH)), _full((1, CF_WIDTH)),
        dense((D_MODEL, D_MODEL)),
    ]


def _mixer_prompt(x, mix_w):
    batch, seq, _ = x.shape
    ts = MIX_STEPS
    any_ = pl.BlockSpec(memory_space=pl.ANY)
    return pl.pallas_call(
        _mixer_prompt_kernel,
        grid=(seq // ts,),
        in_specs=[any_] + _mixer_weight_specs(stacked=True),
        out_specs=[
            any_,
            _full((batch, LRU_WIDTH)),
            _full((batch, LRU_CONV - 1, LRU_WIDTH)),
            _full((batch, CF_CONV - 1, CF_WIDTH)),
            _full((D_MODEL, IN_WIDTH)), _full((LRU_WIDTH, 2 * LRU_WIDTH)), _full((D_MODEL, D_MODEL)),
        ],
        out_shape=[
            jax.ShapeDtypeStruct((batch, seq, D_MODEL), F32),
            jax.ShapeDtypeStruct((batch, LRU_WIDTH), F32),
            jax.ShapeDtypeStruct((batch, LRU_CONV - 1, LRU_WIDTH), F32),
            jax.ShapeDtypeStruct((batch, CF_CONV - 1, CF_WIDTH), F32),
            jax.ShapeDtypeStruct((D_MODEL, IN_WIDTH), BF16),
            jax.ShapeDtypeStruct((LRU_WIDTH, 2 * LRU_WIDTH), BF16),
            jax.ShapeDtypeStruct((D_MODEL, D_MODEL), BF16),
        ],
        scratch_shapes=[
            pltpu.VMEM((2, ts, batch, D_MODEL), F32),
            pltpu.VMEM((2, ts, batch, D_MODEL), F32),
            pltpu.VMEM(((LRU_CONV - 1 + ts) * batch, LRU_WIDTH), F32),
            pltpu.VMEM(((CF_CONV - 1 + ts) * batch, CF_WIDTH), F32),
            pltpu.VMEM((batch, LRU_WIDTH), F32),
            pltpu.SemaphoreType.DMA((2,)), pltpu.SemaphoreType.DMA((2,)),
        ],
        compiler_params=_cparams(("arbitrary",)),
        name="mixer_prompt",
    )(x, *mix_w)


def _mixer_sample_kernel(x_ref, gmix_ref, win_ref, wlc_ref, blc_ref, wg_ref, bg_ref, lam_ref,
                         wcc_ref, bcc_ref, gln_ref, bln_ref, wout_ref,
                         h0_ref, lbuf_ref, cbuf_ref,
                         x1_ref, h_ref, ltail_ref, ctail_ref):
    x = x_ref[...]
    z = _bdot(_rms(x, gmix_ref[...]), win_ref[...])
    zx = z[:, :LRU_WIDTH]
    zg = z[:, LRU_WIDTH:2 * LRU_WIDTH]
    za = z[:, 2 * LRU_WIDTH:2 * LRU_WIDTH + CF_WIDTH]
    zb = z[:, 2 * LRU_WIDTH + CF_WIDTH:]

    xc = blc_ref[...] + wlc_ref[LRU_CONV - 1:LRU_CONV, :] * zx
    for k in range(LRU_CONV - 1):
        xc = xc + lbuf_ref[k] * wlc_ref[k:k + 1, :]
    a, u = _lru_coeffs(xc, wg_ref, bg_ref, lam_ref)
    h = a * h0_ref[...] + u
    y_lru = h * jax.nn.gelu(zg)

    glu = za * jax.nn.sigmoid(zb)
    gc = bcc_ref[...] + wcc_ref[CF_CONV - 1:CF_CONV, :] * glu
    for k in range(CF_CONV - 1):
        gc = gc + cbuf_ref[k] * wcc_ref[k:k + 1, :]
    y_cf = _cf_post(gc, gln_ref, bln_ref)

    x1_ref[...] = _mixer_out(x, y_lru, y_cf, wout_ref)
    h_ref[...] = h
    ltail_ref[0:LRU_CONV - 2] = lbuf_ref[1:LRU_CONV - 1]
    ctail_ref[0:CF_CONV - 2] = cbuf_ref[1:CF_CONV - 1]
    ltail_ref[LRU_CONV - 2] = zx
    ctail_ref[CF_CONV - 2] = glu


def _mixer_sample(x, mix_w, h0, lbuf, cbuf):
    n = x.shape[0]
    g = SAMPLE_MIX_GROUP
    row = lambda w: pl.BlockSpec((g, w), lambda i: (i, 0))
    state = lambda w: pl.BlockSpec((None, g, w), lambda i: (0, i, 0))
    window = lambda k, w: pl.BlockSpec((None, k, g, w), lambda i: (0, 0, i, 0))
    return pl.pallas_call(
        _mixer_sample_kernel,
        grid=(n // g,),
        in_specs=[row(D_MODEL)] + _mixer_weight_specs() + [
            state(LRU_WIDTH),
            window(LRU_CONV - 1, LRU_WIDTH),
            window(CF_CONV - 1, CF_WIDTH),
        ],
        out_specs=[row(D_MODEL), state(LRU_WIDTH),
                   window(LRU_CONV - 1, LRU_WIDTH),
                   window(CF_CONV - 1, CF_WIDTH)],
        out_shape=[
            jax.ShapeDtypeStruct((n, D_MODEL), F32),
            jax.ShapeDtypeStruct((1, n, LRU_WIDTH), F32),
            jax.ShapeDtypeStruct((1, LRU_CONV - 1, n, LRU_WIDTH), F32),
            jax.ShapeDtypeStruct((1, CF_CONV - 1, n, CF_WIDTH), F32),
        ],
        compiler_params=_cparams(("arbitrary",)),
        name="mixer_sample",
    )(x, *mix_w, h0, lbuf, cbuf)


def _memkv_kernel(mem_ref, gmem_ref, wk_ref, wv_ref, k_ref, v_ref, kb_ref, vb_ref, wkb, wvb):
    @pl.when(pl.program_id(0) == 0)
    def _():
        wkb[...] = wk_ref[...].astype(BF16)
        wvb[...] = wv_ref[...].astype(BF16)

    mn = _rms(mem_ref[...], gmem_ref[...]).astype(BF16)
    k = jnp.dot(mn, wkb[...], preferred_element_type=F32)
    v = jnp.dot(mn, wvb[...], preferred_element_type=F32)
    for h in range(XA_HEADS):
        sl = slice(h * XA_HEAD_DIM, (h + 1) * XA_HEAD_DIM)
        k_ref[0, :, h, :] = k[:, sl]
        v_ref[0, :, h, :] = v[:, sl]
    kb_ref[...] = k.astype(BF16)
    vb_ref[...] = v.astype(BF16)


def _memkv(mem, g_mem, wk, wv):
    n = mem.shape[0]
    t = N_MEM
    row = pl.BlockSpec((t, D_MODEL), lambda i: (i, 0))
    state = pl.BlockSpec((1, N_MEM, XA_HEADS, XA_HEAD_DIM), lambda i: (i, 0, 0, 0))
    state_shape = jax.ShapeDtypeStruct((n // t, N_MEM, XA_HEADS, XA_HEAD_DIM), F32)
    return pl.pallas_call(
        _memkv_kernel,
        grid=(n // t,),
        in_specs=[row, _full((1, D_MODEL)), _layer0((D_MODEL, D_MODEL)), _layer0((D_MODEL, D_MODEL))],
        out_specs=[state, state, row, row],
        out_shape=[state_shape] * 2 + [jax.ShapeDtypeStruct((n, D_MODEL), BF16)] * 2,
        scratch_shapes=[pltpu.VMEM((D_MODEL, D_MODEL), BF16)] * 2,
        compiler_params=_cparams(("arbitrary",)),
        name="memory_kv",
    )(mem, g_mem, wk, wv)


def _router(x2, gmoe_ref, wrt_ref, brt_ref, xn_ref, tope_ref, gates_ref):
    xn = _rms(x2, gmoe_ref[...])
    _store_row_tiles(xn_ref, xn)
    logits = lax.dot_general(wrt_ref[...], xn.astype(BF16), (((1,), (1,)), ((), ())),
                             preferred_element_type=F32) + brt_ref[...]
    e_iota = lax.broadcasted_iota(I32, logits.shape, 0)
    work = logits
    vals, idxs = [], []
    for _ in range(TOP_K):
        m = jnp.max(work, axis=0, keepdims=True)
        idx = jnp.min(jnp.where(work == m, e_iota, N_EXPERTS), axis=0, keepdims=True)
        vals.append(m)
        idxs.append(idx)
        work = jnp.where(e_iota == idx, -jnp.inf, work)
    ex = [jnp.exp(v - vals[0]) for v in vals]
    den = ex[0] + ex[1] + ex[2] + ex[3]
    tope_ref[...] = jnp.concatenate(idxs, axis=0)
    gates_ref[...] = jnp.concatenate([e / den for e in ex], axis=0)


def _router_specs():
    return [_full((1, D_MODEL)), _full((N_EXPERTS, D_MODEL)), _full((N_EXPERTS, 1))]


def _router_out(n, tile, index):
    specs = [
        pl.BlockSpec((tile, D_MODEL), lambda *a: (index(*a), 0)),
        pl.BlockSpec((tile * ROW_TILE, LANES), lambda *a: (index(*a), 0)),
        pl.BlockSpec((TOP_K, tile), lambda *a: (0, index(*a))),
        pl.BlockSpec((TOP_K, tile), lambda *a: (0, index(*a))),
    ]
    shapes = [
        jax.ShapeDtypeStruct((n, D_MODEL), F32),
        jax.ShapeDtypeStruct((n * ROW_TILE, LANES), F32),
        jax.ShapeDtypeStruct((TOP_K, n), I32),
        jax.ShapeDtypeStruct((TOP_K, n), F32),
    ]
    return specs, shapes


def _attn_prompt_kernel(x1_ref, k_ref, v_ref, gxa_ref, wq_ref, wo_ref, gmoe_ref, wrt_ref, brt_ref,
                        x2_ref, xn_ref, tope_ref, gates_ref, wqb_ref, wob_ref):
    @pl.when(jnp.logical_and(pl.program_id(0) == 0, pl.program_id(1) == 0))
    def _():
        wqb_ref[...] = wq_ref[...].astype(BF16)
        wob_ref[...] = wo_ref[...].astype(BF16)

    x1 = x1_ref[...]
    q = (_bdot(_rms(x1, gxa_ref[...]), wqb_ref[...]) * (XA_HEAD_DIM ** -0.5)).astype(BF16)
    outs = []
    for h in range(XA_HEADS):
        sl = slice(h * XA_HEAD_DIM, (h + 1) * XA_HEAD_DIM)
        s = lax.dot_general(q[:, sl], k_ref[:, sl], (((1,), (1,)), ((), ())), preferred_element_type=F32)
        p = jnp.exp(s - jnp.max(s, axis=-1, keepdims=True))
        p = p / jnp.sum(p, axis=-1, keepdims=True)
        outs.append(_bdot(p, v_ref[:, sl]))
    o = jnp.concatenate(outs, axis=-1)
    x2 = x1 + _bdot(o, wob_ref[...])
    x2_ref[...] = x2
    _router(x2, gmoe_ref, wrt_ref, brt_ref, xn_ref, tope_ref, gates_ref)


def _attn_prompt(x1, k, v, gxa, wq, wo, router_w, batch, seq):
    tq = ATT_TILE
    nt = seq // tq
    out_specs, out_shapes = _router_out(batch * seq, tq, lambda b, j: b * nt + j)
    kv = pl.BlockSpec((N_MEM, D_MODEL), lambda b, j: (b, 0))
    return pl.pallas_call(
        _attn_prompt_kernel,
        grid=(batch, nt),
        in_specs=[pl.BlockSpec((tq, D_MODEL), lambda b, j: (b * nt + j, 0)), kv, kv,
                  _full((1, D_MODEL)), _layer0((D_MODEL, D_MODEL)), _layer0((D_MODEL, D_MODEL))] + _router_specs(),
        out_specs=out_specs + [_full((D_MODEL, D_MODEL))] * 2,
        out_shape=out_shapes + [jax.ShapeDtypeStruct((D_MODEL, D_MODEL), BF16)] * 2,
        compiler_params=_cparams(("arbitrary", "arbitrary")),
        name="attn_prompt",
    )(x1, k, v, gxa, wq, wo, *router_w)


def _qproj_kernel(x1_ref, gxa_ref, wq_ref, q_ref):
    q_ref[...] = _bdot(_rms(x1_ref[...], gxa_ref[...]), wq_ref[...]) * (XA_HEAD_DIM ** -0.5)


def _attn_sample_core_kernel(q_ref, k_ref, v_ref, o_ref):
    n = N_MEM * SUBLANES
    col_head = lax.broadcasted_iota(I32, (SUBLANES, n), 1) & (SUBLANES - 1)
    same_head = col_head == lax.broadcasted_iota(I32, (SUBLANES, n), 0)
    kv_pad = jnp.zeros((N_MEM, SUBLANES - XA_HEADS, XA_HEAD_DIM), F32)
    q_pad = jnp.zeros((SUBLANES - XA_HEADS, XA_HEAD_DIM), F32)
    for g in range(q_ref.shape[0]):
        k8 = jnp.concatenate([k_ref[g], kv_pad], axis=1).reshape(n, XA_HEAD_DIM).astype(BF16)
        v8 = jnp.concatenate([v_ref[g], kv_pad], axis=1).reshape(n, XA_HEAD_DIM).astype(BF16)
        q8 = jnp.concatenate([q_ref[g], q_pad], axis=0).astype(BF16)
        s = lax.dot_general(q8, k8, (((1,), (1,)), ((), ())), preferred_element_type=F32)
        m = jnp.max(jnp.where(same_head, s, -jnp.inf), axis=-1, keepdims=True)
        p = jnp.where(same_head, jnp.exp(s - m), 0.0)
        den = jnp.sum(p, axis=-1, keepdims=True)
        o8 = jnp.dot(p.astype(BF16), v8, preferred_element_type=F32)
        o_ref[g] = (o8 / den)[0:XA_HEADS, :]


def _oproj_router_kernel(x1_ref, o_ref, wo_ref, gmoe_ref, wrt_ref, brt_ref,
                         x2_ref, xn_ref, tope_ref, gates_ref):
    x2 = x1_ref[...] + _bdot(o_ref[...], wo_ref[...])
    x2_ref[...] = x2
    _router(x2, gmoe_ref, wrt_ref, brt_ref, xn_ref, tope_ref, gates_ref)


def _attn_sample(x1, k, v, gxa, wq, wo, router_w):
    n = x1.shape[0]
    g = SAMPLE_GROUP
    q = pl.pallas_call(
        _qproj_kernel,
        grid=(1,),
        in_specs=[_full((n, D_MODEL)), _full((1, D_MODEL)), _full((D_MODEL, D_MODEL))],
        out_specs=_full((n, D_MODEL)),
        out_shape=jax.ShapeDtypeStruct((n, D_MODEL), F32),
        compiler_params=_cparams(("arbitrary",)),
        name="qproj_sample",
    )(x1, gxa, wq)
    kv = pl.BlockSpec((g, N_MEM, XA_HEADS, XA_HEAD_DIM), lambda i: (i, 0, 0, 0))
    row = pl.BlockSpec((g, XA_HEADS, XA_HEAD_DIM), lambda i: (i, 0, 0))
    o = pl.pallas_call(
        _attn_sample_core_kernel,
        grid=(n // g,),
        in_specs=[row, kv, kv],
        out_specs=row,
        out_shape=jax.ShapeDtypeStruct((n, XA_HEADS, XA_HEAD_DIM), F32),
        compiler_params=_cparams(("arbitrary",)),
        name="attn_sample_core",
    )(q.reshape(n, XA_HEADS, XA_HEAD_DIM), k, v).reshape(n, D_MODEL)
    out_specs, out_shapes = _router_out(n, n, lambda i: 0)
    return pl.pallas_call(
        _oproj_router_kernel,
        grid=(1,),
        in_specs=[_full((n, D_MODEL)), _full((n, D_MODEL)), _full((D_MODEL, D_MODEL))] + _router_specs(),
        out_specs=out_specs,
        out_shape=out_shapes,
        compiler_params=_cparams(("arbitrary",)),
        name="oproj_router_sample",
    )(x1, o, wo, *router_w)


def _moe_num_blocks(n_assign):
    nb = pl.cdiv(n_assign, MOE_BLOCK) + N_EXPERTS
    return pl.cdiv(nb, EXPERT_STEP_BLOCKS) * EXPERT_STEP_BLOCKS


def _slots_kernel(tp_ref, ts_ref, destp_ref, dests_ref, blke_ref, nact_ref, cnt_ref, pstart_ref,
                  rankp, ranks):
    def make_body(top_ref, rank_ref):
        c = min(SLOT_CHUNK, top_ref.shape[1])
        r_i = lax.broadcasted_iota(I32, (c, c), 0)
        c_i = lax.broadcasted_iota(I32, (c, c), 1)
        upper = (r_i < c_i).astype(BF16)
        e_iota = lax.broadcasted_iota(I32, (N_EXPERTS, c), 0)

        def body(ci, carry):
            lo = pl.multiple_of(ci * c, c)
            top = top_ref[:, pl.ds(lo, c)]
            hit = [e_iota == top[k:k + 1, :] for k in range(TOP_K)]
            cnt = sum(m.astype(F32) for m in hit)
            base = carry + jnp.dot(cnt.astype(BF16), upper, preferred_element_type=F32)
            rank_ref[:, pl.ds(lo, c)] = jnp.concatenate(
                [jnp.sum(jnp.where(m, base, 0.0), axis=0, keepdims=True) for m in hit], axis=0)
            return carry + jnp.sum(cnt, axis=1, keepdims=True)
        return top_ref.shape[1] // c, body

    counts = jnp.zeros((N_EXPERTS, 1), F32)
    for top_ref, rank_ref in ((tp_ref, rankp), (ts_ref, ranks)):
        trips, body = make_body(top_ref, rank_ref)
        counts = lax.fori_loop(0, trips, body, counts)

    padded = jnp.floor((counts + (MOE_BLOCK - 1)) / MOE_BLOCK) * MOE_BLOCK
    er = lax.broadcasted_iota(I32, (N_EXPERTS, N_EXPERTS), 0)
    ec = lax.broadcasted_iota(I32, (N_EXPERTS, N_EXPERTS), 1)
    padded_row = jnp.sum(jnp.where(er == ec, padded, 0.0), axis=0, keepdims=True)
    cum = jnp.sum(jnp.where(ec <= er, padded_row, 0.0), axis=1, keepdims=True)
    pstart = cum - padded
    cnt_ref[...] = jnp.sum(jnp.where(er == ec, counts, 0.0), axis=0, keepdims=True).astype(I32)
    pstart_ref[...] = jnp.sum(jnp.where(er == ec, pstart, 0.0), axis=0, keepdims=True).astype(I32)

    def dest(top_ref, rank_ref, out_ref):
        n = top_ref.shape[1]
        ei = lax.broadcasted_iota(I32, (N_EXPERTS, n), 0)
        rows = []
        for k in range(TOP_K):
            start = jnp.sum(jnp.where(ei == top_ref[k:k + 1, :], pstart, 0.0), axis=0, keepdims=True)
            rows.append(start + rank_ref[k:k + 1, :])
        out_ref[...] = jnp.concatenate(rows, axis=0).astype(I32)

    dest(tp_ref, rankp, destp_ref)
    dest(ts_ref, ranks, dests_ref)

    nb = blke_ref.shape[1]
    blk_lo = lax.broadcasted_iota(I32, (N_EXPERTS, nb), 1).astype(F32) * MOE_BLOCK
    blk = jnp.sum((cum <= blk_lo).astype(F32), axis=0, keepdims=True)
    blke_ref[...] = jnp.minimum(blk, N_EXPERTS - 1).astype(I32)
    nact_ref[...] = (cum[N_EXPERTS - 1:N_EXPERTS, :] / MOE_BLOCK).astype(I32)


def _slots(top_p, top_s):
    n_p, n_s = top_p.shape[1], top_s.shape[1]
    nb = _moe_num_blocks((n_p + n_s) * TOP_K)
    return pl.pallas_call(
        _slots_kernel,
        grid=(1,),
        in_specs=[_full((TOP_K, n_p)), _full((TOP_K, n_s))],
        out_specs=[_full((TOP_K, n_p)), _full((TOP_K, n_s)), _full((1, nb)), _full((1, 1)),
                   _full((1, N_EXPERTS)), _full((1, N_EXPERTS))],
        out_shape=[
            jax.ShapeDtypeStruct((TOP_K, n_p), I32),
            jax.ShapeDtypeStruct((TOP_K, n_s), I32),
            jax.ShapeDtypeStruct((1, nb), I32),
            jax.ShapeDtypeStruct((1, 1), I32),
            jax.ShapeDtypeStruct((1, N_EXPERTS), I32),
            jax.ShapeDtypeStruct((1, N_EXPERTS), I32),
        ],
        scratch_shapes=[pltpu.VMEM((TOP_K, n_p), F32), pltpu.VMEM((TOP_K, n_s), F32)],
        compiler_params=_cparams(("arbitrary",)),
        name="moe_slots",
    )(top_p, top_s)


def _for_rows(n_rows, fn):
    def body(i, c):
        for u in range(DMA_UNROLL):
            fn(i * DMA_UNROLL + u)
        return c
    lax.fori_loop(0, n_rows // DMA_UNROLL, body, 0)


def _dispatch_kernel(cnt_ref, pstart_ref, nact_ref, destp_ref, dests_ref, xnp_ref, xns_ref, xs_ref,
                     zeros, sem, zsem):
    i = pl.program_id(0)
    last = pl.num_programs(0) - 1

    def scatter(x_ref, dest_ref):
        n = x_ref.shape[0] // ROW_TILE

        def one(t):
            for k in range(TOP_K):
                _row_tile_copy(x_ref, t, xs_ref, dest_ref[t * TOP_K + k], sem).start(priority=k % 2)

        _for_rows(n, one)
        _row_tile_copy(xs_ref, 0, xs_ref, 0, sem, n * TOP_K).wait()

    @pl.when(i < last)
    def _():
        scatter(xnp_ref, destp_ref)

    @pl.when(i == last)
    def _():
        scatter(xns_ref, dests_ref)
        zeros[...] = jnp.zeros(zeros.shape, F32)

        def pad_copies(e, act):
            c = cnt_ref[e]
            npad = (-c) & (MOE_BLOCK - 1)
            base = pstart_ref[e] + c
            for bit in range(MOE_BLOCK.bit_length() - 1):
                size = 1 << bit

                @pl.when((npad & size) != 0)
                def _():
                    act(_row_tile_copy(zeros, 0, xs_ref, base + (npad & (size - 1)), zsem, size))

        def tail_copies(act):
            def body(j, c):
                act(_row_tile_copy(zeros, 0, xs_ref, j * MOE_BLOCK, zsem, MOE_BLOCK))
                return c
            lax.fori_loop(nact_ref[0], xs_ref.shape[0] // (MOE_BLOCK * ROW_TILE), body, 0)

        for act in (lambda d: d.start(), lambda d: d.wait()):
            lax.fori_loop(0, N_EXPERTS, lambda e, c, act=act: (pad_copies(e, act), c)[1], 0)
            tail_copies(act)


def _dispatch(cnt, pstart, nact, dest_p, dest_s, xn_p, xn_s, n_rows):
    n_p, n_s = xn_p.shape[0] // ROW_TILE, xn_s.shape[0] // ROW_TILE
    tile = DISPATCH_TILE
    nt = n_p // tile
    smem = pl.BlockSpec(memory_space=pltpu.SMEM)
    return pl.pallas_call(
        _dispatch_kernel,
        grid=(nt + 1,),
        in_specs=[
            smem, smem, smem,
            pl.BlockSpec((TOP_K * tile,), lambda i: (jnp.minimum(i, nt - 1),), memory_space=pltpu.SMEM),
            smem,
            pl.BlockSpec((tile * ROW_TILE, LANES), lambda i: (jnp.minimum(i, nt - 1), 0)),
            _full((n_s * ROW_TILE, LANES)),
        ],
        out_specs=pl.BlockSpec(memory_space=pl.ANY),
        out_shape=jax.ShapeDtypeStruct((n_rows * ROW_TILE, LANES), F32),
        scratch_shapes=[pltpu.VMEM((MOE_BLOCK * ROW_TILE, LANES), F32), pltpu.SemaphoreType.DMA,
                        pltpu.SemaphoreType.DMA],
        compiler_params=_cparams(("arbitrary",)),
        name="moe_dispatch",
    )(cnt, pstart, nact, dest_p, dest_s, xn_p, xn_s)


def _combine_kernel(dest_ref, dest_next_ref, gates_ref, x2_ref, gfin_ref, ys_ref, y_ref, bufs, sems):
    tile = x2_ref.shape[0]
    i = pl.program_id(0)
    slot = i % 2

    def gather(d_ref, b):
        def one(t):
            for k in range(TOP_K):
                _row_tile_copy(ys_ref, d_ref[t * TOP_K + k], bufs.at[b, k], t, sems.at[b]).start(priority=k % 2)
        _for_rows(tile, one)

    @pl.when(i == 0)
    def _():
        gather(dest_ref, 0)

    for b in range(2):
        @pl.when(slot == b)
        def _():
            pltpu.make_async_copy(bufs.at[b], bufs.at[b], sems.at[b]).wait()

            @pl.when(i + 1 < pl.num_programs(0))
            def _():
                gather(dest_next_ref, 1 - b)

    buf = bufs.at[slot]
    sumsq = jnp.zeros((tile, 1), F32)
    gate = [jnp.broadcast_to(gates_ref[:, k:k + 1], (tile, LANES)) for k in range(TOP_K)]
    for s in range(ROW_TILE):
        cols = slice(s * LANES, (s + 1) * LANES)
        acc = x2_ref[:, cols]
        for k in range(TOP_K):
            acc = acc + gate[k] * buf[k, pl.ds(s, tile, stride=ROW_TILE), :]
        y_ref[:, cols] = acc
        sumsq = sumsq + jnp.sum(acc * acc, axis=-1, keepdims=True)
    scale = lax.rsqrt(sumsq / D_MODEL + EPS)
    for s in range(ROW_TILE):
        cols = slice(s * LANES, (s + 1) * LANES)
        y_ref[:, cols] = y_ref[:, cols] * scale * gfin_ref[:, cols]


def _combine(dest, gates_t, x2, g_final, ys):
    n = x2.shape[0]
    tile = min(COMBINE_TILE, n)
    row = pl.BlockSpec((tile, D_MODEL), lambda i: (i, 0))
    nt = n // tile
    return pl.pallas_call(
        _combine_kernel,
        grid=(nt,),
        in_specs=[
            pl.BlockSpec((TOP_K * tile,), lambda i: (i,), memory_space=pltpu.SMEM),
            pl.BlockSpec((TOP_K * tile,), lambda i: (jnp.minimum(i + 1, nt - 1),), memory_space=pltpu.SMEM),
            pl.BlockSpec((tile, TOP_K), lambda i: (i, 0)),
            row,
            _full((1, D_MODEL)),
            pl.BlockSpec(memory_space=pl.ANY),
        ],
        out_specs=row,
        out_shape=jax.ShapeDtypeStruct((n, D_MODEL), F32),
        scratch_shapes=[pltpu.VMEM((2, TOP_K, tile * ROW_TILE, LANES), F32), pltpu.SemaphoreType.DMA((2,))],
        compiler_params=_cparams(("arbitrary",)),
        name="moe_combine",
    )(dest, dest, gates_t, x2, g_final, ys)


def _experts_kernel(blke_ref, nact_ref, xs_ref, w1_hbm, w2_hbm, *rest):
    b1_refs, b2_refs = rest[:EXPERT_STEP_BLOCKS], rest[EXPERT_STEP_BLOCKS:2 * EXPERT_STEP_BLOCKS]
    ys_ref, w1f, w2f, w1b, w2b, wsem, started = rest[2 * EXPERT_STEP_BLOCKS:]
    j = pl.program_id(0)
    nact = nact_ref[0]
    last = blke_ref.shape[0] - 1

    def fetch(expert, slot):
        return (pltpu.make_async_copy(w1_hbm.at[expert], w1f.at[slot], wsem.at[slot]),
                pltpu.make_async_copy(w2_hbm.at[expert], w2f.at[slot], wsem.at[slot]))

    def switch_to(blk):
        e = blke_ref[blk]
        n = started[0]
        jn = lax.while_loop(lambda jj: jnp.logical_and(jj < nact, blke_ref[jnp.minimum(jj, last)] == e),
                            lambda jj: jj + 1, blk + 1)
        for b in range(2):
            @pl.when(n % 2 == b)
            def _():
                for d in fetch(e, b):
                    d.wait()

                @pl.when(jn < nact)
                def _():
                    for d in fetch(blke_ref[jnp.minimum(jn, last)], 1 - b):
                        d.start(priority=1)

                w1b[...] = w1f[b].astype(BF16)
                w2b[...] = w2f[b].astype(BF16)
        started[0] = n + 1

    def ffn(row0, n_rows, b1_ref, b2_ref):
        rows = pl.ds(row0 * ROW_TILE, n_rows * ROW_TILE)
        x = _load_row_tiles(xs_ref.at[rows], n_rows).astype(BF16)
        gu = jnp.dot(x, w1b[...], preferred_element_type=F32) + b1_ref[0]
        g = jnp.minimum(gu[:, :D_EXPERT], SWIGLU_LIMIT)
        u = jnp.clip(gu[:, D_EXPERT:], -SWIGLU_LIMIT, SWIGLU_LIMIT)
        h = (u + 1.0) * (g * jax.nn.sigmoid(SWIGLU_ALPHA * g))
        _store_row_tiles(ys_ref.at[rows], _bdot(h, w2b[...]) + b2_ref[0])

    def zero(row0, n_rows):
        ys_ref[pl.ds(row0 * ROW_TILE, n_rows * ROW_TILE), :] = jnp.zeros((n_rows * ROW_TILE, LANES), F32)

    @pl.when(j == 0)
    def _():
        started[0] = 0
        for d in fetch(blke_ref[0], 0):
            d.start()

    for p in range(0, EXPERT_STEP_BLOCKS, 2):
        first, second = EXPERT_STEP_BLOCKS * j + p, EXPERT_STEP_BLOCKS * j + p + 1
        e0, e1 = blke_ref[first], blke_ref[second]
        active0, active1 = first < nact, second < nact
        together = jnp.logical_and(active1, e1 == e0)
        row0 = p * MOE_BLOCK

        @pl.when(jnp.logical_and(active0, jnp.logical_or(first == 0, e0 != blke_ref[jnp.maximum(first - 1, 0)])))
        def _():
            switch_to(first)

        @pl.when(together)
        def _():
            ffn(row0, 2 * MOE_BLOCK, b1_refs[p], b2_refs[p])

        @pl.when(jnp.logical_and(active0, jnp.logical_not(together)))
        def _():
            ffn(row0, MOE_BLOCK, b1_refs[p], b2_refs[p])

            @pl.when(active1)
            def _():
                switch_to(second)
                ffn(row0 + MOE_BLOCK, MOE_BLOCK, b1_refs[p + 1], b2_refs[p + 1])

            @pl.when(jnp.logical_not(active1))
            def _():
                zero(row0 + MOE_BLOCK, MOE_BLOCK)

        @pl.when(jnp.logical_not(active0))
        def _():
            zero(row0, 2 * MOE_BLOCK)


def _experts(blk_e, nact, xs, w1, b1, w2, b2):
    nb = xs.shape[0] // (MOE_BLOCK * ROW_TILE)
    step = EXPERT_STEP_BLOCKS
    assert nb % step == 0 and step % 2 == 0
    blk = lambda j, be, na: (jnp.minimum(j, (na[0] - 1) // step), 0)
    exp = lambda h: (lambda j, be, na: (be[jnp.minimum(step * j + h, na[0] - 1)], 0, 0))
    rows = (step * MOE_BLOCK * ROW_TILE, LANES)
    any_ = pl.BlockSpec(memory_space=pl.ANY)
    return pl.pallas_call(
        _experts_kernel,
        grid_spec=pltpu.PrefetchScalarGridSpec(
            num_scalar_prefetch=2,
            grid=(nb // step,),
            in_specs=[pl.BlockSpec(rows, blk), any_, any_]
            + [pl.BlockSpec((1, 1, 2 * D_EXPERT), exp(h)) for h in range(step)]
            + [pl.BlockSpec((1, 1, D_MODEL), exp(h)) for h in range(step)],
            out_specs=pl.BlockSpec(rows, lambda j, be, na: (j, 0)),
            scratch_shapes=[
                pltpu.VMEM((2, D_MODEL, 2 * D_EXPERT), F32), pltpu.VMEM((2, D_EXPERT, D_MODEL), F32),
                pltpu.VMEM((D_MODEL, 2 * D_EXPERT), BF16), pltpu.VMEM((D_EXPERT, D_MODEL), BF16),
                pltpu.SemaphoreType.DMA((2,)), pltpu.SMEM((1,), I32),
            ],
        ),
        out_shape=jax.ShapeDtypeStruct(xs.shape, F32),
        compiler_params=_cparams(("arbitrary",)),
        name="moe_experts",
    )(blk_e, nact, xs, w1, w2, *([b1] * step), *([b2] * step))


def _block_diag(w):
    h, d, _ = w.shape
    eye = jnp.eye(h, dtype=w.dtype)
    return (eye[:, None, :, None] * w[:, :, None, :]).reshape(h * d, h * d)


def kernel(x_prompt, x_sample, state_lru_h, cache_lru_conv, cache_cf_conv, cache_mem_k, cache_mem_v,
           mem_prompt, g_mix, w_in, w_lru_conv, b_lru_conv, w_lru_a, b_lru_a, w_lru_x, b_lru_x,
           lru_lambda, w_cf_conv, b_cf_conv, g_cf_ln, b_cf_ln, w_out, g_xa, g_mem, w_q, w_k, w_v, w_o,
           g_moe, w_router, b_router, w_e1, b_e1, w_e2, b_e2, g_final):
    depth = g_mix.shape[0]
    assert depth == 1, "single-layer stack"
    batch, seq, _ = x_prompt.shape
    n_s = x_sample.shape[0]
    assert x_sample.shape[1] == 1
    n_p = batch * seq
    l = 0

    mix_w = (
        g_mix[l][None], w_in, w_lru_conv[l], b_lru_conv[l][None],
        jnp.concatenate([_block_diag(w_lru_a[l]), _block_diag(w_lru_x[l])], axis=1),
        jnp.concatenate([b_lru_a[l], b_lru_x[l]])[None], lru_lambda[l][None],
        w_cf_conv[l], b_cf_conv[l][None], g_cf_ln[l][None], b_cf_ln[l][None], w_out,
    )
    router_w = (g_moe[l][None], w_router[l].T.astype(BF16), b_router[l][:, None])
    gxa = g_xa[l][None]

    xs_ = x_sample.reshape(n_s, D_MODEL)
    x1p, hp, lcp, ccp, win_b, wg_b, wout_b = _mixer_prompt(x_prompt, mix_w)
    x1p = x1p.reshape(n_p, D_MODEL)
    mix_w_b = mix_w[:1] + (win_b,) + mix_w[2:4] + (wg_b,) + mix_w[5:-1] + (wout_b,)
    x1s, hs, lru_conv_s, cf_conv_s = _mixer_sample(
        xs_, mix_w_b, state_lru_h, jnp.swapaxes(cache_lru_conv, 1, 2), jnp.swapaxes(cache_cf_conv, 1, 2))

    mk, mv, mkb, mvb = _memkv(mem_prompt.reshape(batch * N_MEM, D_MODEL), g_mem[l][None],
                              w_k, w_v)
    x2p, xnp_, top_p, gates_p, wq, wo = _attn_prompt(x1p, mkb, mvb, gxa, w_q, w_o, router_w, batch, seq)
    x2s, xns, top_s, gates_s = _attn_sample(x1s, cache_mem_k[l], cache_mem_v[l], gxa, wq, wo, router_w)

    dest_p, dest_s, blk_e, nact, cnt, pstart = _slots(top_p, top_s)
    dest_p, dest_s = dest_p.T.reshape(-1), dest_s.T.reshape(-1)
    nb = _moe_num_blocks((n_p + n_s) * TOP_K)
    nact = nact.reshape(1)
    xs = _dispatch(cnt.reshape(N_EXPERTS), pstart.reshape(N_EXPERTS), nact, dest_p, dest_s, xnp_, xns,
                   nb * MOE_BLOCK)
    ys = _experts(blk_e.reshape(nb), nact, xs, w_e1[l], b_e1[l][:, None, :], w_e2[l], b_e2[l][:, None, :])
    gfin = g_final[None]
    y_p = _combine(dest_p, gates_p.T, x2p, gfin, ys)
    y_s = _combine(dest_s, gates_s.T, x2s, gfin, ys)

    return (
        y_p.reshape(batch, seq, D_MODEL),
        y_s.reshape(n_s, 1, D_MODEL),
        hp[None],
        lcp[None],
        ccp[None],
        mk[None],
        mv[None],
        hs,
        jnp.swapaxes(lru_conv_s, 1, 2),
        jnp.swapaxes(cf_conv_s, 1, 2),
    )
```
